```python
import math
import jax, jax.numpy as jnp
from jax import lax
import numpy as np

D_MODEL = 1024
BATCH = 8
SEQ = 4096
DEPTH = 2

D_A = D_MODEL
K_A = 3
D_B = D_MODEL
K_B = 31
N_MEM = 256
N_HEADS = 4
HEAD_DIM = D_MODEL // N_HEADS
D_ATT = N_HEADS * HEAD_DIM
N_BRANCH = 3
D_FF = 2816
K_F = 3
EPS = 1e-6
IN_SPLITS = (3 * D_A, 3 * D_A + 2 * D_B, 3 * D_A + 2 * D_B + D_ATT)
D_IN = 3 * D_A + 2 * D_B + D_ATT + N_BRANCH * D_MODEL

kernel_name = "hybrid_shortconv_conformer_memattn_block"


def _rmsnorm(x, g):
    xf = x.astype(jnp.float32)
    r = lax.rsqrt(jnp.mean(xf * xf, axis=-1, keepdims=True) + EPS)
    return (xf * r).astype(x.dtype) * g


def _layernorm(x, g, b):
    xf = x.astype(jnp.float32)
    mu = jnp.mean(xf, axis=-1, keepdims=True)
    var = jnp.mean(jnp.square(xf - mu), axis=-1, keepdims=True)
    return ((xf - mu) * lax.rsqrt(var + EPS)).astype(x.dtype) * g + b


def _causal_dwconv(x, w):
    k, c = w.shape
    return lax.conv_general_dilated(
        x, w[:, None, :].astype(x.dtype), window_strides=(1,), padding=[(k - 1, 0)],
        dimension_numbers=("NWC", "WIO", "NWC"), feature_group_count=c)


def _fwd_setup_inputs(seed: int = 0) -> dict:
    key = jax.random.key(seed)
    ks = jax.random.split(key, 24)
    L, D = DEPTH, D_MODEL

    def nrm(k, shape, fan_in):
        return jax.random.normal(k, shape, jnp.float32) * (fan_in ** -0.5)

    def gain(k, shape):
        return 1.0 + 0.02 * jax.random.normal(k, shape, jnp.float32)

    def small(k, shape):
        return 0.02 * jax.random.normal(k, shape, jnp.float32)

    return {
        "x": jax.random.normal(ks[0], (BATCH, SEQ, D), jnp.float32),
        "mem": jax.random.normal(ks[1], (BATCH, N_MEM, D), jnp.float32),
        "norm_mix_g": gain(ks[2], (L, D)),
        "norm_mem_g": gain(ks[3], (L, D)),
        "w_in": nrm(ks[4], (L, D, D_IN), D),
        "b_gate": small(ks[5], (L, N_BRANCH * D)),
        "conv_a_w": nrm(ks[6], (L, K_A, D_A), K_A),
        "w_a_out": nrm(ks[7], (L, D_A, D), D_A),
        "conv_b_w": nrm(ks[8], (L, K_B, D_B), K_B),
        "conv_b_bias": small(ks[9], (L, D_B)),
        "ln_b_g": gain(ks[10], (L, D_B)),
        "ln_b_b": small(ks[11], (L, D_B)),
        "w_b_out": nrm(ks[12], (L, D_B, D), D_B),
        "w_kv": nrm(ks[13], (L, D, 2 * D_ATT), D),
        "w_att_out": nrm(ks[14], (L, D_ATT, D), D_ATT),
        "w_o": nrm(ks[15], (L, D, D), D),
        "norm_ffn_g": gain(ks[16], (L, D)),
        "w_up": nrm(ks[17], (L, D, 2 * D_FF), D),
        "conv_ffn_w": nrm(ks[18], (L, K_F, 2 * D_FF), K_F),
        "w_down": nrm(ks[19], (L, D_FF, D), D_FF),
        "norm_final_g": gain(ks[20], (D,)),
    }


def _mixer(x, mem, norm_mix_g, norm_mem_g, w_in, b_gate, conv_a_w, w_a_out,
           conv_b_w, conv_b_bias, ln_b_g, ln_b_b, w_b_out, w_kv, w_att_out, w_o):
    bsz, seq, _ = x.shape
    h = _rmsnorm(x, norm_mix_g)
    proj = h @ w_in
    p_a, p_b, q, p_g = jnp.split(proj, IN_SPLITS, axis=-1)

    gb, gc, v = jnp.split(p_a, 3, axis=-1)
    y_a = (gb * _causal_dwconv(gc * v, conv_a_w)) @ w_a_out

    u, ug = jnp.split(p_b, 2, axis=-1)
    u = u * jax.nn.sigmoid(ug)
    u = _causal_dwconv(u, conv_b_w) + conv_b_bias
    u = jax.nn.silu(_layernorm(u, ln_b_g, ln_b_b))
    y_b = u @ w_b_out

    memn = _rmsnorm(mem, norm_mem_g)
    k, vm = jnp.split(memn @ w_kv, 2, axis=-1)
    qh = q.reshape(bsz, seq, N_HEADS, HEAD_DIM)
    kh = k.reshape(bsz, N_MEM, N_HEADS, HEAD_DIM)
    vh = vm.reshape(bsz, N_MEM, N_HEADS, HEAD_DIM)
    s = jnp.einsum("bshd,bmhd->bhsm", qh, kh).astype(jnp.float32) * (1.0 / math.sqrt(HEAD_DIM))
    pr = jax.nn.softmax(s, axis=-1).astype(x.dtype)
    o = jnp.einsum("bhsm,bmhd->bshd", pr, vh).reshape(bsz, seq, D_ATT)
    y_c = o @ w_att_out

    g = jax.nn.sigmoid((p_g + b_gate).reshape(bsz, seq, N_BRANCH, D_MODEL))
    merged = g[:, :, 0] * y_a + g[:, :, 1] * y_b + g[:, :, 2] * y_c
    return merged @ w_o


def _conv_ffn(x, norm_ffn_g, w_up, conv_ffn_w, w_down):
    h = _rmsnorm(x, norm_ffn_g)
    u = _causal_dwconv(h @ w_up, conv_ffn_w)
    gt, up = jnp.split(u, 2, axis=-1)
    return (jax.nn.silu(gt) * up) @ w_down


def _fwd_reference(x, mem, norm_mix_g, norm_mem_g, w_in, b_gate, conv_a_w, w_a_out,
              conv_b_w, conv_b_bias, ln_b_g, ln_b_b, w_b_out, w_kv, w_att_out, w_o,
              norm_ffn_g, w_up, conv_ffn_w, w_down, norm_final_g):
    for l in range(DEPTH):
        x = x + _mixer(x, mem, norm_mix_g[l], norm_mem_g[l], w_in[l], b_gate[l],
                       conv_a_w[l], w_a_out[l], conv_b_w[l], conv_b_bias[l],
                       ln_b_g[l], ln_b_b[l], w_b_out[l], w_kv[l], w_att_out[l], w_o[l])
        x = x + _conv_ffn(x, norm_ffn_g[l], w_up[l], conv_ffn_w[l], w_down[l])
    return _rmsnorm(x, norm_final_g)


import jax as _jax
import jax.numpy as _jnp

TWIN_FORMAT = 'train_step'
FWD_PARAMS = ['x', 'mem', 'norm_mix_g', 'norm_mem_g', 'w_in', 'b_gate', 'conv_a_w', 'w_a_out', 'conv_b_w', 'conv_b_bias', 'ln_b_g', 'ln_b_b', 'w_b_out', 'w_kv', 'w_att_out', 'w_o', 'norm_ffn_g', 'w_up', 'conv_ffn_w', 'w_down', 'norm_final_g']
TWIN_WEIGHTS = ['norm_mix_g', 'norm_mem_g', 'w_in', 'b_gate', 'conv_a_w', 'w_a_out', 'conv_b_w', 'conv_b_bias', 'ln_b_g', 'ln_b_b', 'w_b_out', 'w_kv', 'w_att_out', 'w_o', 'norm_ffn_g', 'w_up', 'conv_ffn_w', 'w_down', 'norm_final_g']
TWIN_DIFF_INPUT = 'x'
TWIN_INPUTS = ['x', 'mem', 'norm_mix_g', 'norm_mem_g', 'w_in', 'b_gate', 'conv_a_w', 'w_a_out', 'conv_b_w', 'conv_b_bias', 'ln_b_g', 'ln_b_b', 'w_b_out', 'w_kv', 'w_att_out', 'w_o', 'norm_ffn_g', 'w_up', 'conv_ffn_w', 'w_down', 'norm_final_g', 'loss_target', 'm_norm_mix_g', 'm_norm_mem_g', 'm_w_in', 'm_b_gate', 'm_conv_a_w', 'm_w_a_out', 'm_conv_b_w', 'm_conv_b_bias', 'm_ln_b_g', 'm_ln_b_b', 'm_w_b_out', 'm_w_kv', 'm_w_att_out', 'm_w_o', 'm_norm_ffn_g', 'm_w_up', 'm_conv_ffn_w', 'm_w_down', 'm_norm_final_g', 'v_norm_mix_g', 'v_norm_mem_g', 'v_w_in', 'v_b_gate', 'v_conv_a_w', 'v_w_a_out', 'v_conv_b_w', 'v_conv_b_bias', 'v_ln_b_g', 'v_ln_b_b', 'v_w_b_out', 'v_w_kv', 'v_w_att_out', 'v_w_o', 'v_norm_ffn_g', 'v_w_up', 'v_conv_ffn_w', 'v_w_down', 'v_norm_final_g']
TWIN_OUTPUTS = ['loss', 'grad_x', 'grad_norm_mix_g', 'grad_norm_mem_g', 'grad_w_in', 'grad_b_gate', 'grad_conv_a_w', 'grad_w_a_out', 'grad_conv_b_w', 'grad_conv_b_bias', 'grad_ln_b_g', 'grad_ln_b_b', 'grad_w_b_out', 'grad_w_kv', 'grad_w_att_out', 'grad_w_o', 'grad_norm_ffn_g', 'grad_w_up', 'grad_conv_ffn_w', 'grad_w_down', 'grad_norm_final_g', 'delta_norm_mix_g', 'delta_norm_mem_g', 'delta_w_in', 'delta_b_gate', 'delta_conv_a_w', 'delta_w_a_out', 'delta_conv_b_w', 'delta_conv_b_bias', 'delta_ln_b_g', 'delta_ln_b_b', 'delta_w_b_out', 'delta_w_kv', 'delta_w_att_out', 'delta_w_o', 'delta_norm_ffn_g', 'delta_w_up', 'delta_conv_ffn_w', 'delta_w_down', 'delta_norm_final_g', 'new_m_norm_mix_g', 'new_m_norm_mem_g', 'new_m_w_in', 'new_m_b_gate', 'new_m_conv_a_w', 'new_m_w_a_out', 'new_m_conv_b_w', 'new_m_conv_b_bias', 'new_m_ln_b_g', 'new_m_ln_b_b', 'new_m_w_b_out', 'new_m_w_kv', 'new_m_w_att_out', 'new_m_w_o', 'new_m_norm_ffn_g', 'new_m_w_up', 'new_m_conv_ffn_w', 'new_m_w_down', 'new_m_norm_final_g', 'new_v_norm_mix_g', 'new_v_norm_mem_g', 'new_v_w_in', 'new_v_b_gate', 'new_v_conv_a_w', 'new_v_w_a_out', 'new_v_conv_b_w', 'new_v_conv_b_bias', 'new_v_ln_b_g', 'new_v_ln_b_b', 'new_v_w_b_out', 'new_v_w_kv', 'new_v_w_att_out', 'new_v_w_o', 'new_v_norm_ffn_g', 'new_v_w_up', 'new_v_conv_ffn_w', 'new_v_w_down', 'new_v_norm_final_g']
TWIN_LEAF_KINDS = {'loss': 'loss', 'grad_x': 'grad_x', 'grad_norm_mix_g': 'grad_w', 'grad_norm_mem_g': 'grad_w', 'grad_w_in': 'grad_w', 'grad_b_gate': 'grad_w', 'grad_conv_a_w': 'grad_w', 'grad_w_a_out': 'grad_w', 'grad_conv_b_w': 'grad_w', 'grad_conv_b_bias': 'grad_w', 'grad_ln_b_g': 'grad_w', 'grad_ln_b_b': 'grad_w', 'grad_w_b_out': 'grad_w', 'grad_w_kv': 'grad_w', 'grad_w_att_out': 'grad_w', 'grad_w_o': 'grad_w', 'grad_norm_ffn_g': 'grad_w', 'grad_w_up': 'grad_w', 'grad_conv_ffn_w': 'grad_w', 'grad_w_down': 'grad_w', 'grad_norm_final_g': 'grad_w', 'delta_norm_mix_g': 'delta_w', 'delta_norm_mem_g': 'delta_w', 'delta_w_in': 'delta_w', 'delta_b_gate': 'delta_w', 'delta_conv_a_w': 'delta_w', 'delta_w_a_out': 'delta_w', 'delta_conv_b_w': 'delta_w', 'delta_conv_b_bias': 'delta_w', 'delta_ln_b_g': 'delta_w', 'delta_ln_b_b': 'delta_w', 'delta_w_b_out': 'delta_w', 'delta_w_kv': 'delta_w', 'delta_w_att_out': 'delta_w', 'delta_w_o': 'delta_w', 'delta_norm_ffn_g': 'delta_w', 'delta_w_up': 'delta_w', 'delta_conv_ffn_w': 'delta_w', 'delta_w_down': 'delta_w', 'delta_norm_final_g': 'delta_w', 'new_m_norm_mix_g': 'new_m', 'new_m_norm_mem_g': 'new_m', 'new_m_w_in': 'new_m', 'new_m_b_gate': 'new_m', 'new_m_conv_a_w': 'new_m', 'new_m_w_a_out': 'new_m', 'new_m_conv_b_w': 'new_m', 'new_m_conv_b_bias': 'new_m', 'new_m_ln_b_g': 'new_m', 'new_m_ln_b_b': 'new_m', 'new_m_w_b_out': 'new_m', 'new_m_w_kv': 'new_m', 'new_m_w_att_out': 'new_m', 'new_m_w_o': 'new_m', 'new_m_norm_ffn_g': 'new_m', 'new_m_w_up': 'new_m', 'new_m_conv_ffn_w': 'new_m', 'new_m_w_down': 'new_m', 'new_m_norm_final_g': 'new_m', 'new_v_norm_mix_g': 'new_v', 'new_v_norm_mem_g': 'new_v', 'new_v_w_in': 'new_v', 'new_v_b_gate': 'new_v', 'new_v_conv_a_w': 'new_v', 'new_v_w_a_out': 'new_v', 'new_v_conv_b_w': 'new_v', 'new_v_conv_b_bias': 'new_v', 'new_v_ln_b_g': 'new_v', 'new_v_ln_b_b': 'new_v', 'new_v_w_b_out': 'new_v', 'new_v_w_kv': 'new_v', 'new_v_w_att_out': 'new_v', 'new_v_w_o': 'new_v', 'new_v_norm_ffn_g': 'new_v', 'new_v_w_up': 'new_v', 'new_v_conv_ffn_w': 'new_v', 'new_v_w_down': 'new_v', 'new_v_norm_final_g': 'new_v'}


def _forward(args):
    return _fwd_reference(*[args[k] for k in FWD_PARAMS])


def _output_shape():
    out = _jax.eval_shape(lambda: _forward(_fwd_setup_inputs(0)))
    return out.shape, out.dtype

N_MICROBATCH = 1
ADAM_LR = 0.001
ADAM_B1 = 0.9
ADAM_B2 = 0.999
ADAM_EPS = 1e-08
ADAM_WD = 0.01
ADAM_STEP = 10
PER_EXAMPLE_BATCH_AXIS = {'x': 0, 'mem': 0, 'loss_target': 0}
SHARED_INPUTS = []
_WEIGHT_DTYPES = {'norm_mix_g': _jnp.float32, 'norm_mem_g': _jnp.float32, 'w_in': _jnp.float32, 'b_gate': _jnp.float32, 'conv_a_w': _jnp.float32, 'w_a_out': _jnp.float32, 'conv_b_w': _jnp.float32, 'conv_b_bias': _jnp.float32, 'ln_b_g': _jnp.float32, 'ln_b_b': _jnp.float32, 'w_b_out': _jnp.float32, 'w_kv': _jnp.float32, 'w_att_out': _jnp.float32, 'w_o': _jnp.float32, 'norm_ffn_g': _jnp.float32, 'w_up': _jnp.float32, 'conv_ffn_w': _jnp.float32, 'w_down': _jnp.float32, 'norm_final_g': _jnp.float32}
MOMENT_SCALE = {'norm_mix_g': 1.833239e-01, 'norm_mem_g': 1.438165e-02, 'w_in': 5.919509e-02, 'b_gate': 2.581123e-02, 'conv_a_w': 9.550713e-02, 'w_a_out': 9.311064e-02, 'conv_b_w': 5.792073e-02, 'conv_b_bias': 1.259172e-01, 'ln_b_g': 6.724201e-02, 'ln_b_b': 5.759757e-02, 'w_b_out': 5.576063e-02, 'w_kv': 9.807323e-03, 'w_att_out': 9.912886e-03, 'w_o': 1.088536e-01, 'norm_ffn_g': 1.197705e-01, 'w_up': 5.179526e-02, 'conv_ffn_w': 5.262684e-02, 'w_down': 8.456423e-02, 'norm_final_g': 3.199809e+01}


def _to_microbatches(a, axis):
    t = _jnp.moveaxis(a, axis, 0)
    t = t.reshape((N_MICROBATCH, t.shape[0] // N_MICROBATCH) + t.shape[1:])
    return _jnp.moveaxis(t, 1, axis + 1)


def setup_inputs(seed: int = 0) -> dict:
    inp = _fwd_setup_inputs(seed)
    key = _jax.random.fold_in(_jax.random.key(seed), 7919)
    shape, _ = _output_shape()
    out = dict(inp)
    out["loss_target"] = _jax.random.normal(_jax.random.fold_in(key, 0), shape, _jnp.float32)
    for i, name in enumerate(TWIN_WEIGHTS):
        w = inp[name].astype(_jnp.float32)
        if MOMENT_SCALE is None:
            s = _jnp.sqrt(_jnp.mean(_jnp.square(w)) + 1e-30)
        else:
            s = MOMENT_SCALE[name]
        km, kv = _jax.random.split(_jax.random.fold_in(key, i + 1))
        out[name] = w
        out["m_" + name] = s * _jax.random.normal(km, w.shape, _jnp.float32)
        out["v_" + name] = (s * s) * _jax.random.uniform(kv, w.shape, _jnp.float32, 0.5, 1.5)
    if N_MICROBATCH > 1:
        for name, axis in PER_EXAMPLE_BATCH_AXIS.items():
            out[name] = _to_microbatches(out[name], axis)
    return {'x': out['x'], 'mem': out['mem'], 'norm_mix_g': out['norm_mix_g'], 'norm_mem_g': out['norm_mem_g'], 'w_in': out['w_in'], 'b_gate': out['b_gate'], 'conv_a_w': out['conv_a_w'], 'w_a_out': out['w_a_out'], 'conv_b_w': out['conv_b_w'], 'conv_b_bias': out['conv_b_bias'], 'ln_b_g': out['ln_b_g'], 'ln_b_b': out['ln_b_b'], 'w_b_out': out['w_b_out'], 'w_kv': out['w_kv'], 'w_att_out': out['w_att_out'], 'w_o': out['w_o'], 'norm_ffn_g': out['norm_ffn_g'], 'w_up': out['w_up'], 'conv_ffn_w': out['conv_ffn_w'], 'w_down': out['w_down'], 'norm_final_g': out['norm_final_g'], 'loss_target': out['loss_target'], 'm_norm_mix_g': out['m_norm_mix_g'], 'm_norm_mem_g': out['m_norm_mem_g'], 'm_w_in': out['m_w_in'], 'm_b_gate': out['m_b_gate'], 'm_conv_a_w': out['m_conv_a_w'], 'm_w_a_out': out['m_w_a_out'], 'm_conv_b_w': out['m_conv_b_w'], 'm_conv_b_bias': out['m_conv_b_bias'], 'm_ln_b_g': out['m_ln_b_g'], 'm_ln_b_b': out['m_ln_b_b'], 'm_w_b_out': out['m_w_b_out'], 'm_w_kv': out['m_w_kv'], 'm_w_att_out': out['m_w_att_out'], 'm_w_o': out['m_w_o'], 'm_norm_ffn_g': out['m_norm_ffn_g'], 'm_w_up': out['m_w_up'], 'm_conv_ffn_w': out['m_conv_ffn_w'], 'm_w_down': out['m_w_down'], 'm_norm_final_g': out['m_norm_final_g'], 'v_norm_mix_g': out['v_norm_mix_g'], 'v_norm_mem_g': out['v_norm_mem_g'], 'v_w_in': out['v_w_in'], 'v_b_gate': out['v_b_gate'], 'v_conv_a_w': out['v_conv_a_w'], 'v_w_a_out': out['v_w_a_out'], 'v_conv_b_w': out['v_conv_b_w'], 'v_conv_b_bias': out['v_conv_b_bias'], 'v_ln_b_g': out['v_ln_b_g'], 'v_ln_b_b': out['v_ln_b_b'], 'v_w_b_out': out['v_w_b_out'], 'v_w_kv': out['v_w_kv'], 'v_w_att_out': out['v_w_att_out'], 'v_w_o': out['v_w_o'], 'v_norm_ffn_g': out['v_norm_ffn_g'], 'v_w_up': out['v_w_up'], 'v_conv_ffn_w': out['v_conv_ffn_w'], 'v_w_down': out['v_w_down'], 'v_norm_final_g': out['v_norm_final_g']}


def _loss(weights, diff, rest, loss_target):
    with _jax.named_scope("forward"):
        args = {**rest, TWIN_DIFF_INPUT: diff, **{k: w.astype(_WEIGHT_DTYPES[k]) for k, w in weights.items()}}
        y = _forward(args)
    with _jax.named_scope("loss_head"):
        err = _jnp.square(y.astype(_jnp.float32) - loss_target)
        return 0.5 * _jnp.sum(_jnp.mean(err, axis=-1)) if err.ndim else 0.5 * err


def _adamw(w, g, m, v):
    m = ADAM_B1 * m + (1.0 - ADAM_B1) * g
    v = ADAM_B2 * v + (1.0 - ADAM_B2) * _jnp.square(g)
    m_hat = m / (1.0 - ADAM_B1 ** ADAM_STEP)
    v_hat = v / (1.0 - ADAM_B2 ** ADAM_STEP)
    delta = -ADAM_LR * (m_hat / (_jnp.sqrt(v_hat) + ADAM_EPS) + ADAM_WD * w)
    return delta, m, v


def reference(x, mem, norm_mix_g, norm_mem_g, w_in, b_gate, conv_a_w, w_a_out, conv_b_w, conv_b_bias, ln_b_g, ln_b_b, w_b_out, w_kv, w_att_out, w_o, norm_ffn_g, w_up, conv_ffn_w, w_down, norm_final_g, loss_target, m_norm_mix_g, m_norm_mem_g, m_w_in, m_b_gate, m_conv_a_w, m_w_a_out, m_conv_b_w, m_conv_b_bias, m_ln_b_g, m_ln_b_b, m_w_b_out, m_w_kv, m_w_att_out, m_w_o, m_norm_ffn_g, m_w_up, m_conv_ffn_w, m_w_down, m_norm_final_g, v_norm_mix_g, v_norm_mem_g, v_w_in, v_b_gate, v_conv_a_w, v_w_a_out, v_conv_b_w, v_conv_b_bias, v_ln_b_g, v_ln_b_b, v_w_b_out, v_w_kv, v_w_att_out, v_w_o, v_norm_ffn_g, v_w_up, v_conv_ffn_w, v_w_down, v_norm_final_g):
    given = dict(x=x, mem=mem, norm_mix_g=norm_mix_g, norm_mem_g=norm_mem_g, w_in=w_in, b_gate=b_gate, conv_a_w=conv_a_w, w_a_out=w_a_out, conv_b_w=conv_b_w, conv_b_bias=conv_b_bias, ln_b_g=ln_b_g, ln_b_b=ln_b_b, w_b_out=w_b_out, w_kv=w_kv, w_att_out=w_att_out, w_o=w_o, norm_ffn_g=norm_ffn_g, w_up=w_up, conv_ffn_w=conv_ffn_w, w_down=w_down, norm_final_g=norm_final_g, loss_target=loss_target, m_norm_mix_g=m_norm_mix_g, m_norm_mem_g=m_norm_mem_g, m_w_in=m_w_in, m_b_gate=m_b_gate, m_conv_a_w=m_conv_a_w, m_w_a_out=m_w_a_out, m_conv_b_w=m_conv_b_w, m_conv_b_bias=m_conv_b_bias, m_ln_b_g=m_ln_b_g, m_ln_b_b=m_ln_b_b, m_w_b_out=m_w_b_out, m_w_kv=m_w_kv, m_w_att_out=m_w_att_out, m_w_o=m_w_o, m_norm_ffn_g=m_norm_ffn_g, m_w_up=m_w_up, m_conv_ffn_w=m_conv_ffn_w, m_w_down=m_w_down, m_norm_final_g=m_norm_final_g, v_norm_mix_g=v_norm_mix_g, v_norm_mem_g=v_norm_mem_g, v_w_in=v_w_in, v_b_gate=v_b_gate, v_conv_a_w=v_conv_a_w, v_w_a_out=v_w_a_out, v_conv_b_w=v_conv_b_w, v_conv_b_bias=v_conv_b_bias, v_ln_b_g=v_ln_b_g, v_ln_b_b=v_ln_b_b, v_w_b_out=v_w_b_out, v_w_kv=v_w_kv, v_w_att_out=v_w_att_out, v_w_o=v_w_o, v_norm_ffn_g=v_norm_ffn_g, v_w_up=v_w_up, v_conv_ffn_w=v_conv_ffn_w, v_w_down=v_w_down, v_norm_final_g=v_norm_final_g)
    weights = {n: given[n] for n in TWIN_WEIGHTS}
    shared = {n: given[n] for n in SHARED_INPUTS}
    per_example = {n: given[n] for n in ['x', 'mem']}
    grad_fn = _jax.value_and_grad(_loss, argnums=(0, 1))

    def one_microbatch(ex, loss_target):
        ex = dict(ex)
        diff = ex.pop(TWIN_DIFF_INPUT)
        return grad_fn(weights, diff, {**shared, **ex}, loss_target)

    if N_MICROBATCH == 1:
        loss, (grad_w, grad_x) = one_microbatch(per_example, given["loss_target"])
    else:
        def body(carry, xs):
            loss_sum, grad_sum = carry
            l_k, (gw_k, gx_k) = one_microbatch(xs[0], xs[1])
            with _jax.named_scope("update"):
                return (loss_sum + l_k, _jax.tree.map(_jnp.add, grad_sum, gw_k)), gx_k

        init = (_jnp.zeros((), _jnp.float32), _jax.tree.map(_jnp.zeros_like, weights))
        (loss, grad_w), grad_x = _jax.lax.scan(body, init, (per_example, given["loss_target"]))
    with _jax.named_scope("update"):
        delta_w, new_m, new_v = {}, {}, {}
        for n in TWIN_WEIGHTS:
            delta_w[n], new_m[n], new_v[n] = _adamw(weights[n], grad_w[n], given["m_" + n], given["v_" + n])
    return (loss, grad_x, *[grad_w[n] for n in TWIN_WEIGHTS], *[delta_w[n] for n in TWIN_WEIGHTS],
            *[new_m[n] for n in TWIN_WEIGHTS], *[new_v[n] for n in TWIN_WEIGHTS])
```

```python
import functools

import jax
import jax.numpy as jnp
from jax import lax
from jax.experimental import pallas as pl
from jax.experimental.pallas import tpu as pltpu

F32 = jnp.float32
BF = jnp.bfloat16
SDS = jax.ShapeDtypeStruct
MESH = pl.DeviceIdType.MESH
ANY = pl.BlockSpec(memory_space=pl.ANY)

N_DEV = 8
DEPTH = 2
D = 1024
N_HEADS = 4
HEAD = D // N_HEADS
D_FF = 2816
K_A, K_B, K_F = 3, 31, 3
NORM_EPS = 1e-6

C_IN = 9 * D // N_DEV
C_KV = 2 * D // N_DEV
C_UP = 2 * D_FF // N_DEV
LANE = 128
C_UP_P = -(-C_UP // LANE) * LANE
FF_P = 4 * C_UP_P
R_O = D // N_DEV
R_DN = D_FF // N_DEV

VMEM_LIMIT = 56 * 1024 * 1024
TM = 512
TR = 256
H_S, H_L = 16, 32

ADAM_LR, ADAM_B1, ADAM_B2, ADAM_EPS, ADAM_WD, ADAM_STEP = 0.001, 0.9, 0.999, 1e-08, 0.01, 10

UP_ORDER = (0, 4, 1, 5, 2, 6, 3, 7)


def _pcall(body, **kw):
    return pl.pallas_call(body, **kw)


def _cp(sem=None, **kw):
    return pltpu.CompilerParams(dimension_semantics=sem, vmem_limit_bytes=VMEM_LIMIT, **kw)


def _dot(a, b):
    return jnp.dot(a, b, preferred_element_type=F32)


def _dot_nt(a, b):
    return lax.dot_general(a, b, (((1,), (1,)), ((), ())), preferred_element_type=F32)


def _dot_tn(a, b):
    return lax.dot_general(a, b, (((0,), (0,)), ((), ())), preferred_element_type=F32)


def _sigmoid(z):
    return 1.0 / (1.0 + jnp.exp(-z))


def _rms(xv):
    return lax.rsqrt(jnp.mean(xv * xv, axis=-1, keepdims=True) + NORM_EPS)


def _up_slot(idx):
    return jnp.where(idx < 4, 2 * idx, 2 * (idx - 4) + 1)


def _dn_row(idx):
    return C_UP_P * (idx // 2) + R_DN * (idx % 2)


def _mm(a, b, *, ta=False, tb=False, out_dtype=BF, tm=TM, tn=512, tk=None, name):
    m, k_dim = (a.shape[1], a.shape[0]) if ta else a.shape
    n = b.shape[0] if tb else b.shape[1]
    tm, tn = min(tm, m), min(tn, n)
    tk = k_dim if tk is None else min(tk, k_dim)
    nk = k_dim // tk
    assert m % tm == 0 and n % tn == 0 and k_dim % tk == 0
    dims = (((0 if ta else 1,), (1 if tb else 0,)), ((), ()))

    def body(a_ref, b_ref, o_ref, *scratch):
        part = lax.dot_general(a_ref[...], b_ref[...], dims, preferred_element_type=F32)
        if nk == 1:
            o_ref[...] = part.astype(o_ref.dtype)
            return
        acc = scratch[0]
        k = pl.program_id(2)

        @pl.when(k == 0)
        def _():
            acc[...] = part

        @pl.when(k > 0)
        def _():
            acc[...] += part

        @pl.when(k == nk - 1)
        def _():
            o_ref[...] = acc[...].astype(o_ref.dtype)

    a_spec = pl.BlockSpec((tk, tm), lambda i, j, k: (k, i)) if ta else pl.BlockSpec((tm, tk), lambda i, j, k: (i, k))
    b_spec = pl.BlockSpec((tn, tk), lambda i, j, k: (j, k)) if tb else pl.BlockSpec((tk, tn), lambda i, j, k: (k, j))
    return _pcall(
        body, grid=(m // tm, n // tn, nk), in_specs=[a_spec, b_spec],
        out_specs=pl.BlockSpec((tm, tn), lambda i, j, k: (i, j)),
        out_shape=SDS((m, n), out_dtype),
        scratch_shapes=[pltpu.VMEM((tm, tn), F32)] if nk > 1 else [],
        compiler_params=_cp(("parallel", "parallel", "arbitrary")), name=name)(a, b)


def _mm_res_norm(a, w, x, g, *, name):
    s, k_dim = a.shape
    tm = min(TM, s)

    def body(a_ref, w_ref, x_ref, g_ref, xo_ref, h_ref):
        xo = x_ref[...] + _dot(a_ref[...], w_ref[...])
        xo_ref[...] = xo
        h_ref[...] = ((xo * _rms(xo)) * g_ref[...]).astype(BF)

    return _pcall(
        body, grid=(s // tm,),
        in_specs=[pl.BlockSpec((tm, k_dim), lambda i: (i, 0)), pl.BlockSpec((k_dim, D), lambda i: (0, 0)),
                  pl.BlockSpec((tm, D), lambda i: (i, 0)), pl.BlockSpec((1, D), lambda i: (0, 0))],
        out_specs=[pl.BlockSpec((tm, D), lambda i: (i, 0))] * 2,
        out_shape=[SDS((s, D), F32), SDS((s, D), BF)],
        compiler_params=_cp(("parallel",)), name=name)(a, w, x, g)


def _mm_nt_normbwd(da, w, x, dres, g, *, tk, name):
    s, k_dim = da.shape
    tm = min(TM, s)
    nk = k_dim // tk
    assert k_dim % tk == 0

    def body(da_ref, w_ref, x_ref, dres_ref, g_ref, dx_ref, dxb_ref, dg_ref, acc):
        i, k = pl.program_id(0), pl.program_id(1)
        part = _dot_nt(da_ref[...], w_ref[...])

        @pl.when(k == 0)
        def _():
            acc[...] = part

        @pl.when(k > 0)
        def _():
            acc[...] += part

        @pl.when((i == 0) & (k == 0))
        def _():
            dg_ref[...] = jnp.zeros_like(dg_ref)

        @pl.when(k == nk - 1)
        def _():
            dh = acc[...]
            xv = x_ref[...]
            r = _rms(xv)
            xn = xv * r
            dg_ref[0:1, :] += jnp.sum(dh * xn, axis=0, keepdims=True)
            dxn = dh * g_ref[...]
            dx = dres_ref[...] + r * (dxn - xn * jnp.mean(dxn * xn, axis=-1, keepdims=True))
            dx_ref[...] = dx
            dxb_ref[...] = dx.astype(BF)

    row = lambda i, k: (i, 0)
    return _pcall(
        body, grid=(s // tm, nk),
        in_specs=[pl.BlockSpec((tm, tk), lambda i, k: (i, k)), pl.BlockSpec((D, tk), lambda i, k: (0, k)),
                  pl.BlockSpec((tm, D), row), pl.BlockSpec((tm, D), row), pl.BlockSpec((1, D), lambda i, k: (0, 0))],
        out_specs=[pl.BlockSpec((tm, D), row), pl.BlockSpec((tm, D), row), pl.BlockSpec((8, D), lambda i, k: (0, 0))],
        out_shape=[SDS((s, D), F32), SDS((s, D), BF), SDS((8, D), F32)],
        scratch_shapes=[pltpu.VMEM((tm, D), F32)],
        compiler_params=_cp(("arbitrary", "arbitrary")), name=name)(da, w, x, dres, g)


def _rms_fwd(x, g, *, name):
    s = x.shape[0]
    tm = min(TM, s)

    def body(x_ref, g_ref, h_ref):
        xv = x_ref[...]
        h_ref[...] = ((xv * _rms(xv)) * g_ref[...]).astype(BF)

    return _pcall(
        body, grid=(s // tm,),
        in_specs=[pl.BlockSpec((tm, D), lambda i: (i, 0)), pl.BlockSpec((1, D), lambda i: (0, 0))],
        out_specs=pl.BlockSpec((tm, D), lambda i: (i, 0)), out_shape=SDS((s, D), BF),
        compiler_params=_cp(("parallel",)), name=name)(x, g)


def _loss_head(x, tgt, g, *, name):
    s = x.shape[0]
    tm = min(TM, s)

    def body(x_ref, t_ref, g_ref, dx_ref, dxb_ref, sums_ref):
        @pl.when(pl.program_id(0) == 0)
        def _():
            sums_ref[...] = jnp.zeros_like(sums_ref)

        xv = x_ref[...]
        r = _rms(xv)
        xn = xv * r
        diff = xn * g_ref[...] - t_ref[...]
        sums_ref[0:1, :] += jnp.sum(diff * diff, axis=0, keepdims=True)
        dy = diff * (1.0 / D)
        sums_ref[1:2, :] += jnp.sum(dy * xn, axis=0, keepdims=True)
        dxn = dy * g_ref[...]
        dx = r * (dxn - xn * jnp.mean(dxn * xn, axis=-1, keepdims=True))
        dx_ref[...] = dx
        dxb_ref[...] = dx.astype(BF)

    row = lambda i: (i, 0)
    return _pcall(
        body, grid=(s // tm,),
        in_specs=[pl.BlockSpec((tm, D), row), pl.BlockSpec((tm, D), row), pl.BlockSpec((1, D), lambda i: (0, 0))],
        out_specs=[pl.BlockSpec((tm, D), row), pl.BlockSpec((tm, D), row), pl.BlockSpec((8, D), lambda i: (0, 0))],
        out_shape=[SDS((s, D), F32), SDS((s, D), BF), SDS((8, D), F32)],
        compiler_params=_cp(("arbitrary",)), name=name)(x, tgt, g)


def _halo_before(i, tr, h):
    return jnp.maximum(i * (tr // h) - 1, 0)


def _halo_after(i, tr, h, s):
    return jnp.minimum((i + 1) * (tr // h), s // h - 1)


def _taps(buf, w_ref, sl, k_w, base, rows):
    acc = None
    for k in range(k_w):
        t = w_ref[k:k + 1, sl] * buf[base + k:base + k + rows, sl]
        acc = t if acc is None else acc + t
    return acc


def _taps_rev(buf, w_ref, sl, k_w, rows):
    acc = None
    for k in range(k_w):
        t = w_ref[k:k + 1, sl] * buf[k_w - 1 - k:k_w - 1 - k + rows, sl]
        acc = t if acc is None else acc + t
    return acc


def _tap_grads(dw_ref, dc, buf, sl, k_w, base, rows):
    for k in range(k_w):
        dw_ref[k:k + 1, sl] += jnp.sum(dc * buf[base + k:base + k + rows, sl], axis=0, keepdims=True)


def _bra_fwd(proj, cw, *, name):
    s = proj.shape[0]
    tr, h = min(TR, s), H_S

    def body(cur, halo, w_ref, za_ref, cvb):
        i = pl.program_id(0)
        hv = halo[:, D:2 * D].astype(F32) * halo[:, 2 * D:3 * D].astype(F32)
        cvb[0:h, :] = jnp.where(i == 0, 0.0, hv)
        cvb[h:h + tr, :] = cur[:, D:2 * D].astype(F32) * cur[:, 2 * D:3 * D].astype(F32)
        for c in range(D // LANE):
            sl = slice(LANE * c, LANE * c + LANE)
            ca = _taps(cvb, w_ref, sl, K_A, h - (K_A - 1), tr)
            za_ref[:, sl] = (cur[:, sl].astype(F32) * ca).astype(BF)

    return _pcall(
        body, grid=(s // tr,),
        in_specs=[pl.BlockSpec((tr, 3 * D), lambda i: (i, 0)),
                  pl.BlockSpec((h, 3 * D), lambda i: (_halo_before(i, tr, h), 0)),
                  pl.BlockSpec((8, D), lambda i: (0, 0))],
        out_specs=pl.BlockSpec((tr, D), lambda i: (i, 0)), out_shape=SDS((s, D), BF),
        scratch_shapes=[pltpu.VMEM((h + tr, D), F32)],
        compiler_params=_cp(("parallel",)), name=name)(proj, proj, cw)


def _bra_bwd(proj, dza, cw, *, name):
    s = proj.shape[0]
    tr, h = min(TR, s), H_S
    n = s // tr

    def body(before, cur, after, dz_cur, dz_after, w_ref, da_ref, dw_ref, cvb, dcab):
        i = pl.program_id(0)

        @pl.when(i == 0)
        def _():
            dw_ref[...] = jnp.zeros_like(dw_ref)

        first, last = i == 0, i == n - 1
        cvb[0:h, :] = jnp.where(first, 0.0, before[:, D:2 * D].astype(F32) * before[:, 2 * D:3 * D].astype(F32))
        cvb[h:h + tr, :] = cur[:, D:2 * D].astype(F32) * cur[:, 2 * D:3 * D].astype(F32)
        dcab[0:tr, :] = dz_cur[...].astype(F32) * cur[:, 0:D].astype(F32)
        dcab[tr:tr + h, :] = jnp.where(last, 0.0, dz_after[...].astype(F32) * after[:, 0:D].astype(F32))
        for c in range(D // LANE):
            sl = slice(LANE * c, LANE * c + LANE)
            gc = cur[:, D + LANE * c:D + LANE * c + LANE].astype(F32)
            v = cur[:, 2 * D + LANE * c:2 * D + LANE * c + LANE].astype(F32)
            ca = _taps(cvb, w_ref, sl, K_A, h - (K_A - 1), tr)
            da_ref[:, sl] = (dz_cur[:, sl].astype(F32) * ca).astype(BF)
            dcv = _taps_rev(dcab, w_ref, sl, K_A, tr)
            da_ref[:, D + LANE * c:D + LANE * c + LANE] = (dcv * v).astype(BF)
            da_ref[:, 2 * D + LANE * c:2 * D + LANE * c + LANE] = (dcv * gc).astype(BF)
            _tap_grads(dw_ref, dcab[0:tr, sl], cvb, sl, K_A, h - (K_A - 1), tr)

    return _pcall(
        body, grid=(n,),
        in_specs=[pl.BlockSpec((h, 3 * D), lambda i: (_halo_before(i, tr, h), 0)),
                  pl.BlockSpec((tr, 3 * D), lambda i: (i, 0)),
                  pl.BlockSpec((h, 3 * D), lambda i: (_halo_after(i, tr, h, s), 0)),
                  pl.BlockSpec((tr, D), lambda i: (i, 0)),
                  pl.BlockSpec((h, D), lambda i: (_halo_after(i, tr, h, s), 0)),
                  pl.BlockSpec((8, D), lambda i: (0, 0))],
        out_specs=[pl.BlockSpec((tr, 3 * D), lambda i: (i, 0)), pl.BlockSpec((8, D), lambda i: (0, 0))],
        out_shape=[SDS((s, 3 * D), BF), SDS((8, D), F32)],
        scratch_shapes=[pltpu.VMEM((h + tr, D), F32), pltpu.VMEM((tr + h, D), F32)],
        compiler_params=_cp(("arbitrary",)), name=name)(proj, proj, proj, dza, dza, cw)


_U_COL, _UG_COL = 3, 4


def _brb_conv_fwd(proj, cw, bias, *, name):
    s = proj.shape[0]
    tr, h = min(TR, s), H_L

    def body(u_cur, ug_cur, u_halo, ug_halo, w_ref, b_ref, cb_ref, glb):
        i = pl.program_id(0)
        glb[0:h, :] = jnp.where(i == 0, 0.0, u_halo[...].astype(F32) * _sigmoid(ug_halo[...].astype(F32)))
        glb[h:h + tr, :] = u_cur[...].astype(F32) * _sigmoid(ug_cur[...].astype(F32))
        for c in range(D // LANE):
            sl = slice(LANE * c, LANE * c + LANE)
            cb_ref[:, sl] = (_taps(glb, w_ref, sl, K_B, h - (K_B - 1), tr) + b_ref[:, sl]).astype(BF)

    return _pcall(
        body, grid=(s // tr,),
        in_specs=[pl.BlockSpec((tr, D), lambda i: (i, _U_COL)), pl.BlockSpec((tr, D), lambda i: (i, _UG_COL)),
                  pl.BlockSpec((h, D), lambda i: (_halo_before(i, tr, h), _U_COL)),
                  pl.BlockSpec((h, D), lambda i: (_halo_before(i, tr, h), _UG_COL)),
                  pl.BlockSpec((32, D), lambda i: (0, 0)), pl.BlockSpec((1, D), lambda i: (0, 0))],
        out_specs=pl.BlockSpec((tr, D), lambda i: (i, 0)), out_shape=SDS((s, D), BF),
        scratch_shapes=[pltpu.VMEM((h + tr, D), F32)],
        compiler_params=_cp(("parallel",)), name=name)(proj, proj, proj, proj, cw, bias)


def _brb_conv_bwd(proj, dcb, cw, *, name):
    s = proj.shape[0]
    tr, h = min(TR, s), H_L
    n = s // tr

    def body(u_before, ug_before, u_cur, ug_cur, d_cur, d_after, w_ref, db_ref, dw_ref, glb, dcbb):
        i = pl.program_id(0)

        @pl.when(i == 0)
        def _():
            dw_ref[...] = jnp.zeros_like(dw_ref)

        glb[0:h, :] = jnp.where(i == 0, 0.0, u_before[...].astype(F32) * _sigmoid(ug_before[...].astype(F32)))
        glb[h:h + tr, :] = u_cur[...].astype(F32) * _sigmoid(ug_cur[...].astype(F32))
        dcbb[0:tr, :] = d_cur[...].astype(F32)
        dcbb[tr:tr + h, :] = jnp.where(i == n - 1, 0.0, d_after[...].astype(F32))
        for c in range(D // LANE):
            sl = slice(LANE * c, LANE * c + LANE)
            dglu = _taps_rev(dcbb, w_ref, sl, K_B, tr)
            u = u_cur[:, sl].astype(F32)
            sg = _sigmoid(ug_cur[:, sl].astype(F32))
            db_ref[:, sl] = (dglu * sg).astype(BF)
            db_ref[:, D + LANE * c:D + LANE * c + LANE] = (dglu * u * sg * (1.0 - sg)).astype(BF)
            _tap_grads(dw_ref, dcbb[0:tr, sl], glb, sl, K_B, h - (K_B - 1), tr)

    return _pcall(
        body, grid=(n,),
        in_specs=[pl.BlockSpec((h, D), lambda i: (_halo_before(i, tr, h), _U_COL)),
                  pl.BlockSpec((h, D), lambda i: (_halo_before(i, tr, h), _UG_COL)),
                  pl.BlockSpec((tr, D), lambda i: (i, _U_COL)), pl.BlockSpec((tr, D), lambda i: (i, _UG_COL)),
                  pl.BlockSpec((tr, D), lambda i: (i, 0)),
                  pl.BlockSpec((h, D), lambda i: (_halo_after(i, tr, h, s), 0)),
                  pl.BlockSpec((32, D), lambda i: (0, 0))],
        out_specs=[pl.BlockSpec((tr, 2 * D), lambda i: (i, 0)), pl.BlockSpec((32, D), lambda i: (0, 0))],
        out_shape=[SDS((s, 2 * D), BF), SDS((32, D), F32)],
        scratch_shapes=[pltpu.VMEM((h + tr, D), F32), pltpu.VMEM((tr + h, D), F32)],
        compiler_params=_cp(("arbitrary",)), name=name)(proj, proj, proj, proj, dcb, dcb, cw)


def _ln_silu_fwd(cb, g, b, *, name):
    s = cb.shape[0]
    tm = min(TM, s)

    def body(cb_ref, g_ref, b_ref, sb_ref):
        z = cb_ref[...].astype(F32)
        zc = z - jnp.mean(z, axis=-1, keepdims=True)
        ln = (zc * lax.rsqrt(jnp.mean(zc * zc, axis=-1, keepdims=True) + NORM_EPS)) * g_ref[...] + b_ref[...]
        sb_ref[...] = (ln * _sigmoid(ln)).astype(BF)

    row = lambda i: (i, 0)
    vec = pl.BlockSpec((1, D), lambda i: (0, 0))
    return _pcall(
        body, grid=(s // tm,), in_specs=[pl.BlockSpec((tm, D), row), vec, vec],
        out_specs=pl.BlockSpec((tm, D), row), out_shape=SDS((s, D), BF),
        compiler_params=_cp(("parallel",)), name=name)(cb, g, b)


def _ln_silu_bwd(cb, dsb, g, b, *, name):
    s = cb.shape[0]
    tm = min(TM, s)

    def body(cb_ref, dsb_ref, g_ref, b_ref, dcb_ref, sums_ref):
        @pl.when(pl.program_id(0) == 0)
        def _():
            sums_ref[...] = jnp.zeros_like(sums_ref)

        z = cb_ref[...].astype(F32)
        zc = z - jnp.mean(z, axis=-1, keepdims=True)
        rstd = lax.rsqrt(jnp.mean(zc * zc, axis=-1, keepdims=True) + NORM_EPS)
        lnh = zc * rstd
        ln = lnh * g_ref[...] + b_ref[...]
        sg = _sigmoid(ln)
        dln = dsb_ref[...].astype(F32) * (sg * (1.0 + ln * (1.0 - sg)))
        sums_ref[0:1, :] += jnp.sum(dln * lnh, axis=0, keepdims=True)
        sums_ref[1:2, :] += jnp.sum(dln, axis=0, keepdims=True)
        dlnh = dln * g_ref[...]
        dz = rstd * (dlnh - jnp.mean(dlnh, axis=-1, keepdims=True)
                     - lnh * jnp.mean(dlnh * lnh, axis=-1, keepdims=True))
        sums_ref[2:3, :] += jnp.sum(dz, axis=0, keepdims=True)
        dcb_ref[...] = dz.astype(BF)

    row = lambda i: (i, 0)
    vec = pl.BlockSpec((1, D), lambda i: (0, 0))
    return _pcall(
        body, grid=(s // tm,), in_specs=[pl.BlockSpec((tm, D), row), pl.BlockSpec((tm, D), row), vec, vec],
        out_specs=[pl.BlockSpec((tm, D), row), pl.BlockSpec((8, D), lambda i: (0, 0))],
        out_shape=[SDS((s, D), BF), SDS((8, D), F32)],
        compiler_params=_cp(("arbitrary",)), name=name)(cb, dsb, g, b)


_Q_COL = 5 * D // HEAD


def _kv_prep(mem, g, wkv, *, name):
    m = mem.shape[0]

    def body(mem_ref, g_ref, w_ref, memn_ref, kv_ref):
        mv = mem_ref[...]
        memn = ((mv * _rms(mv)) * g_ref[...]).astype(BF)
        memn_ref[...] = memn
        kv_ref[...] = _dot(memn, w_ref[...]).astype(BF)

    return _pcall(body, out_shape=[SDS((m, D), BF), SDS((m, 2 * D), BF)],
                  compiler_params=_cp(), name=name)(mem, g, wkv)


def _softmax_rows(q, k):
    sc = _dot_nt(q, k) * (1.0 / (HEAD ** 0.5))
    e = jnp.exp(sc - jnp.max(sc, axis=-1, keepdims=True))
    return e / jnp.sum(e, axis=-1, keepdims=True)


def _attn_fwd(proj, kv, *, name):
    s, m = proj.shape[0], kv.shape[0]
    tm = min(TM, s)

    def body(q_ref, k_ref, v_ref, o_ref):
        p = _softmax_rows(q_ref[...], k_ref[...])
        o_ref[...] = _dot(p.astype(BF), v_ref[...]).astype(BF)

    return _pcall(
        body, grid=(s // tm, N_HEADS),
        in_specs=[pl.BlockSpec((tm, HEAD), lambda i, hd: (i, _Q_COL + hd)),
                  pl.BlockSpec((m, HEAD), lambda i, hd: (0, hd)),
                  pl.BlockSpec((m, HEAD), lambda i, hd: (0, N_HEADS + hd))],
        out_specs=pl.BlockSpec((tm, HEAD), lambda i, hd: (i, hd)), out_shape=SDS((s, D), BF),
        compiler_params=_cp(("parallel", "parallel")), name=name)(proj, kv, kv)


def _attn_bwd(proj, kv, do, *, name):
    s, m = proj.shape[0], kv.shape[0]
    tm = min(TM, s)

    def body(q_ref, k_ref, v_ref, do_ref, dq_ref, dk_ref, dv_ref):
        @pl.when(pl.program_id(1) == 0)
        def _():
            dk_ref[...] = jnp.zeros_like(dk_ref)
            dv_ref[...] = jnp.zeros_like(dv_ref)

        q, k, dov = q_ref[...], k_ref[...], do_ref[...]
        p = _softmax_rows(q, k)
        dp = _dot_nt(dov, v_ref[...])
        dv_ref[...] += _dot_tn(p.astype(BF), dov)
        ds = (p * (dp - jnp.sum(dp * p, axis=-1, keepdims=True)) * (1.0 / (HEAD ** 0.5))).astype(BF)
        dq_ref[...] = _dot(ds, k).astype(BF)
        dk_ref[...] += _dot_tn(ds, q)

    return _pcall(
        body, grid=(N_HEADS, s // tm),
        in_specs=[pl.BlockSpec((tm, HEAD), lambda hd, i: (i, _Q_COL + hd)),
                  pl.BlockSpec((m, HEAD), lambda hd, i: (0, hd)),
                  pl.BlockSpec((m, HEAD), lambda hd, i: (0, N_HEADS + hd)),
                  pl.BlockSpec((tm, HEAD), lambda hd, i: (i, hd))],
        out_specs=[pl.BlockSpec((tm, HEAD), lambda hd, i: (i, hd)),
                   pl.BlockSpec((m, HEAD), lambda hd, i: (0, hd)),
                   pl.BlockSpec((m, HEAD), lambda hd, i: (0, hd))],
        out_shape=[SDS((s, D), BF), SDS((m, D), F32), SDS((m, D), F32)],
        compiler_params=_cp(("parallel", "arbitrary")), name=name)(proj, kv, kv, do)


def _kv_bwd(mem, g, memn, dk, dv, wkv, *, name):
    def body(mem_ref, g_ref, memn_ref, dk_ref, dv_ref, w_ref, dw_ref, dg_ref):
        dkb, dvb = dk_ref[...].astype(BF), dv_ref[...].astype(BF)
        memn = memn_ref[...]
        dw_ref[:, 0:D] = _dot_tn(memn, dkb).astype(BF)
        dw_ref[:, D:2 * D] = _dot_tn(memn, dvb).astype(BF)
        dmemn = _dot_nt(dkb, w_ref[:, 0:D]) + _dot_nt(dvb, w_ref[:, D:2 * D])
        mv = mem_ref[...]
        dg_ref[...] = jnp.zeros_like(dg_ref)
        dg_ref[0:1, :] = jnp.sum(dmemn * (mv * _rms(mv)), axis=0, keepdims=True)

    return _pcall(body, out_shape=[SDS((D, 2 * D), BF), SDS((8, D), F32)],
                  compiler_params=_cp(), name=name)(mem, g, memn, dk, dv, wkv)


_TM_MIX = 256


def _mix_out(x, za, sb, o, proj, w4, bg, g_next, *, name):
    s = x.shape[0]
    tm = min(_TM_MIX, s)

    def body(x_ref, za_ref, sb_ref, o_ref, pg_ref, w4_ref, bg_ref, gn_ref,
             ya_ref, yb_ref, yc_ref, mg_ref, x1_ref, h_ref):
        ys = (_dot(za_ref[...], w4_ref[0]), _dot(sb_ref[...], w4_ref[1]), _dot(o_ref[...], w4_ref[2]))
        merged = None
        for j, (y, y_ref) in enumerate(zip(ys, (ya_ref, yb_ref, yc_ref))):
            y_ref[...] = y.astype(BF)
            gate = _sigmoid(pg_ref[:, j * D:(j + 1) * D].astype(F32) + bg_ref[:, j * D:(j + 1) * D])
            merged = gate * y if merged is None else merged + gate * y
        mg = merged.astype(BF)
        mg_ref[...] = mg
        x1 = x_ref[...] + _dot(mg, w4_ref[3])
        x1_ref[...] = x1
        h_ref[...] = ((x1 * _rms(x1)) * gn_ref[...]).astype(BF)

    row = lambda i: (i, 0)
    act = pl.BlockSpec((tm, D), row)
    return _pcall(
        body, grid=(s // tm,),
        in_specs=[act, act, act, act, pl.BlockSpec((tm, 3 * D), lambda i: (i, 2)),
                  pl.BlockSpec((4, D, D), lambda i: (0, 0, 0)), pl.BlockSpec((1, 3 * D), lambda i: (0, 0)),
                  pl.BlockSpec((1, D), lambda i: (0, 0))],
        out_specs=[act] * 6,
        out_shape=[SDS((s, D), BF)] * 4 + [SDS((s, D), F32), SDS((s, D), BF)],
        compiler_params=_cp(("parallel",)), name=name)(x, za, sb, o, proj, w4, bg, g_next)


def _mix_bwd(dxb, ya, yb, yc, proj, w4, bg, *, name):
    s = dxb.shape[0]
    tm = min(_TM_MIX, s)

    def body(dx_ref, ya_ref, yb_ref, yc_ref, pg_ref, w4_ref, bg_ref,
             dya_ref, dyb_ref, dyc_ref, dza_ref, dsb_ref, do_ref, dgt_ref, dbg_ref):
        @pl.when(pl.program_id(0) == 0)
        def _():
            dbg_ref[...] = jnp.zeros_like(dbg_ref)

        dm = _dot_nt(dx_ref[...], w4_ref[3])
        for j, (y_ref, dy_ref, din_ref) in enumerate(zip((ya_ref, yb_ref, yc_ref), (dya_ref, dyb_ref, dyc_ref),
                                                         (dza_ref, dsb_ref, do_ref))):
            cols = slice(j * D, (j + 1) * D)
            gate = _sigmoid(pg_ref[:, cols].astype(F32) + bg_ref[:, cols])
            dy = (dm * gate).astype(BF)
            dy_ref[...] = dy
            din_ref[...] = _dot_nt(dy, w4_ref[j]).astype(BF)
            dpre = dm * y_ref[...].astype(F32) * gate * (1.0 - gate)
            dgt_ref[:, cols] = dpre.astype(BF)
            dbg_ref[0:1, cols] += jnp.sum(dpre, axis=0, keepdims=True)

    row = lambda i: (i, 0)
    act = pl.BlockSpec((tm, D), row)
    return _pcall(
        body, grid=(s // tm,),
        in_specs=[act, act, act, act, pl.BlockSpec((tm, 3 * D), lambda i: (i, 2)),
                  pl.BlockSpec((4, D, D), lambda i: (0, 0, 0)), pl.BlockSpec((1, 3 * D), lambda i: (0, 0))],
        out_specs=[act] * 6 + [pl.BlockSpec((tm, 3 * D), row), pl.BlockSpec((8, 3 * D), lambda i: (0, 0))],
        out_shape=[SDS((s, D), BF)] * 6 + [SDS((s, 3 * D), BF), SDS((8, 3 * D), F32)],
        compiler_params=_cp(("arbitrary",)), name=name)(dxb, ya, yb, yc, proj, w4, bg)


_PAIR = 2 * C_UP_P


def _ffn_act(u2, cw, *, name):
    s = u2.shape[0]
    tr, h = min(TR, s), H_S

    def body(cur, halo, w_ref, act_ref, ub):
        i = pl.program_id(1)
        ub[0:h, :] = jnp.where(i == 0, 0.0, halo[...].astype(F32))
        ub[h:h + tr, :] = cur[...].astype(F32)
        for c in range(C_UP_P // LANE):
            gl = slice(LANE * c, LANE * c + LANE)
            ul = slice(C_UP_P + LANE * c, C_UP_P + LANE * c + LANE)
            gt = _taps(ub, w_ref, gl, K_F, h - (K_F - 1), tr)
            up = _taps(ub, w_ref, ul, K_F, h - (K_F - 1), tr)
            act_ref[:, gl] = (gt * _sigmoid(gt) * up).astype(BF)

    return _pcall(
        body, grid=(4, s // tr),
        in_specs=[pl.BlockSpec((tr, _PAIR), lambda p, i: (i, p)),
                  pl.BlockSpec((h, _PAIR), lambda p, i: (_halo_before(i, tr, h), p)),
                  pl.BlockSpec((8, _PAIR), lambda p, i: (0, p))],
        out_specs=pl.BlockSpec((tr, C_UP_P), lambda p, i: (i, p)), out_shape=SDS((s, FF_P), BF),
        scratch_shapes=[pltpu.VMEM((h + tr, _PAIR), F32)],
        compiler_params=_cp(("parallel", "parallel")), name=name)(u2, u2, cw)


def _ffn_bwd(u2, dact, cw, *, name):
    s = u2.shape[0]
    tr, h = min(TR, s), H_S
    n = s // tr
    ext = tr + h

    def body(before, cur, after, da_cur, da_after, w_ref, du_ref, dw_ref, ub, dcb):
        i = pl.program_id(1)

        @pl.when(i == 0)
        def _():
            dw_ref[...] = jnp.zeros_like(dw_ref)

        ub[0:h, :] = jnp.where(i == 0, 0.0, before[...].astype(F32))
        ub[h:h + tr, :] = cur[...].astype(F32)
        ub[h + tr:h + tr + h, :] = jnp.where(i == n - 1, 0.0, after[...].astype(F32))
        for c in range(C_UP_P // LANE):
            gl = slice(LANE * c, LANE * c + LANE)
            ul = slice(C_UP_P + LANE * c, C_UP_P + LANE * c + LANE)
            gt = _taps(ub, w_ref, gl, K_F, h - (K_F - 1), ext)
            up = _taps(ub, w_ref, ul, K_F, h - (K_F - 1), ext)
            da = jnp.concatenate([da_cur[:, gl].astype(F32),
                                  jnp.where(i == n - 1, 0.0, da_after[:, gl].astype(F32))], axis=0)
            sg = _sigmoid(gt)
            dcb[:, gl] = da * up * (sg * (1.0 + gt * (1.0 - sg)))
            dcb[:, ul] = da * (gt * sg)
        for c in range(_PAIR // LANE):
            sl = slice(LANE * c, LANE * c + LANE)
            du_ref[:, sl] = _taps_rev(dcb, w_ref, sl, K_F, tr).astype(BF)
            _tap_grads(dw_ref, dcb[0:tr, sl], ub, sl, K_F, h - (K_F - 1), tr)

    return _pcall(
        body, grid=(4, n),
        in_specs=[pl.BlockSpec((h, _PAIR), lambda p, i: (_halo_before(i, tr, h), p)),
                  pl.BlockSpec((tr, _PAIR), lambda p, i: (i, p)),
                  pl.BlockSpec((h, _PAIR), lambda p, i: (_halo_after(i, tr, h, s), p)),
                  pl.BlockSpec((tr, C_UP_P), lambda p, i: (i, p)),
                  pl.BlockSpec((h, C_UP_P), lambda p, i: (_halo_after(i, tr, h, s), p)),
                  pl.BlockSpec((8, _PAIR), lambda p, i: (0, p))],
        out_specs=[pl.BlockSpec((tr, _PAIR), lambda p, i: (i, p)), pl.BlockSpec((8, _PAIR), lambda p, i: (0, p))],
        out_shape=[SDS((s, 2 * FF_P), BF), SDS((8, 2 * FF_P), F32)],
        scratch_shapes=[pltpu.VMEM((h + tr + h, _PAIR), F32), pltpu.VMEM((ext, _PAIR), F32)],
        compiler_params=_cp(("parallel", "arbitrary")), name=name)(u2, u2, u2, dact, dact, cw)


def _relations():
    x, y, c = lax.axis_index("x"), lax.axis_index("y"), lax.axis_index("c")
    out = []
    for r in range(1, N_DEV):
        rx, ry, rc = (r >> 2) & 1, (r >> 1) & 1, r & 1
        out.append((r, (x ^ rx, y ^ ry, c ^ rc)))
    return out


def _my_index():
    return 4 * lax.axis_index("x") + 2 * lax.axis_index("y") + lax.axis_index("c")


def _exchange(n_arrays, src_of, dst_of, refs):
    ssem, rsem, lsem = refs
    me = _my_index()
    local = []
    for a in range(n_arrays):
        loc = pltpu.make_async_copy(src_of(a, me), dst_of(a, me), lsem.at[a])
        loc.start()
        local.append(loc)

    def copy(a, r, peer, src_idx, dst_idx):
        return pltpu.make_async_remote_copy(
            src_ref=src_of(a, src_idx), dst_ref=dst_of(a, dst_idx), send_sem=ssem.at[a, r - 1],
            recv_sem=rsem.at[a, r - 1], device_id=peer, device_id_type=MESH)

    peers = [(r, peer, 4 * peer[0] + 2 * peer[1] + peer[2]) for r, peer in _relations()]
    for r, peer, p_idx in peers:
        for a in range(n_arrays):
            copy(a, r, peer, p_idx, me).start()
    for r, peer, p_idx in peers:
        for a in range(n_arrays):
            copy(a, r, peer, p_idx, me).wait_send()
            copy(a, r, peer, me, p_idx).wait_recv()
    for loc in local:
        loc.wait()


def _sem_scratch(n_arrays):
    return [pltpu.SemaphoreType.DMA((n_arrays, N_DEV - 1)), pltpu.SemaphoreType.DMA((n_arrays, N_DEV - 1)),
            pltpu.SemaphoreType.DMA((n_arrays,))]


def _allgather_weights(win, wup, wkv, w4, wdn, cv, *, name):
    dn_init = jnp.zeros((FF_P, D), BF)

    def body(win_r, wup_r, wkv_r, w4_r, wdn_r, cv_r, dn0_r, win_g, wup_g, wkv_g, w4_g, wdn_g, cv_g, *sems):
        del dn0_r
        srcs = (win_r, wup_r, wkv_r, w4_r, wdn_r, cv_r)

        def dst_of(a, idx):
            if a == 0:
                return win_g.at[:, pl.ds(pl.multiple_of(idx * C_IN, LANE), C_IN)]
            if a == 1:
                return wup_g.at[:, pl.ds(pl.multiple_of(_up_slot(idx) * C_UP_P, LANE), C_UP_P)]
            if a == 2:
                return wkv_g.at[:, pl.ds(pl.multiple_of(idx * C_KV, LANE), C_KV)]
            if a == 3:
                return w4_g.at[:, pl.ds(pl.multiple_of(idx * R_O, 16), R_O), :]
            if a == 4:
                return wdn_g.at[pl.ds(pl.multiple_of(_dn_row(idx), 16), R_DN), :]
            return cv_g.at[idx]

        _exchange(6, lambda a, idx: srcs[a], dst_of, sems)

    return _pcall(
        body, in_specs=[ANY] * 7, out_specs=[ANY] * 6,
        out_shape=[SDS((D, 9 * D), BF), SDS((D, 2 * FF_P), BF), SDS((D, 2 * D), BF), SDS((4, D, D), BF),
                   SDS((FF_P, D), BF), SDS((N_DEV,) + cv.shape, F32)],
        input_output_aliases={6: 4}, scratch_shapes=_sem_scratch(6),
        compiler_params=pltpu.CompilerParams(has_side_effects=True), name=name)(win, wup, wkv, w4, wdn, cv, dn_init)


def _scatter_grads(dwin, dwup, dwkv, dw4, dwdn, *, name):
    def body(dwin_r, dwup_r, dwkv_r, dw4_r, dwdn_r, swin, swup, swkv, sw4, swdn, *sems):
        def src_of(a, idx):
            if a == 0:
                return dwin_r.at[:, pl.ds(pl.multiple_of(idx * C_IN, LANE), C_IN)]
            if a == 1:
                return dwup_r.at[:, pl.ds(pl.multiple_of(_up_slot(idx) * C_UP_P, LANE), C_UP_P)]
            if a == 2:
                return dwkv_r.at[:, pl.ds(pl.multiple_of(idx * C_KV, LANE), C_KV)]
            if a == 3:
                return dw4_r.at[:, pl.ds(pl.multiple_of(idx * R_O, 16), R_O), :]
            return dwdn_r.at[pl.ds(pl.multiple_of(_dn_row(idx), 16), R_DN), :]

        outs = (swin, swup, swkv, sw4, swdn)
        _exchange(5, src_of, lambda a, idx: outs[a].at[idx], sems)

    return _pcall(
        body, in_specs=[ANY] * 5, out_specs=[ANY] * 5,
        out_shape=[SDS((N_DEV, D, C_IN), BF), SDS((N_DEV, D, C_UP_P), BF), SDS((N_DEV, D, C_KV), BF),
                   SDS((N_DEV, 4, R_O, D), BF), SDS((N_DEV, R_DN, D), BF)],
        scratch_shapes=_sem_scratch(5),
        compiler_params=pltpu.CompilerParams(has_side_effects=True), name=name)(dwin, dwup, dwkv, dw4, dwdn)


def _allreduce_small(pack, *, name):
    rows = pack.shape[0]

    def body(p_ref, out_ref, gath, ssem, rsem, lsem):
        _exchange(1, lambda a, idx: p_ref, lambda a, idx: gath.at[idx], (ssem, rsem, lsem))
        total = gath[0]
        for d in range(1, N_DEV):
            total = total + gath[d]
        out_ref[...] = total

    vm = pl.BlockSpec(memory_space=pltpu.VMEM)
    return _pcall(
        body, in_specs=[vm], out_specs=vm, out_shape=SDS(pack.shape, F32),
        scratch_shapes=[pltpu.VMEM((N_DEV, rows, pack.shape[1]), F32)] + _sem_scratch(1),
        compiler_params=_cp(has_side_effects=True), name=name)(pack)


def _adam(g, w, m, v):
    nm = ADAM_B1 * m + (1.0 - ADAM_B1) * g
    nv = ADAM_B2 * v + (1.0 - ADAM_B2) * (g * g)
    m_hat = nm / (1.0 - ADAM_B1 ** ADAM_STEP)
    v_hat = nv / (1.0 - ADAM_B2 ** ADAM_STEP)
    return -ADAM_LR * (m_hat / (jnp.sqrt(v_hat) + ADAM_EPS) + ADAM_WD * w), nm, nv


def _adamw_staged(st0, st1, w, m, v, *, name):
    _, rows, cols = w.shape
    tr = max(t for t in range(16, 129, 16) if rows % t == 0)
    nr = rows // tr

    def body(s0_ref, s1_ref, w_ref, m_ref, v_ref, g_ref, d_ref, nm_ref, nv_ref):
        for layer, s_ref in enumerate((s0_ref, s1_ref)):
            @pl.when(pl.program_id(0) == layer)
            def _(s_ref=s_ref):
                total = s_ref[0].astype(F32)
                for dev in range(1, N_DEV):
                    total = total + s_ref[dev].astype(F32)
                g_ref[0] = total

        d_ref[0], nm_ref[0], nv_ref[0] = _adam(g_ref[0], w_ref[0], m_ref[0], v_ref[0])

    st_spec = lambda layer: pl.BlockSpec(
        (N_DEV, tr, cols), lambda l, i: (0, jnp.where(l == layer, i, (nr - 1) * (1 - layer)), 0))
    par = pl.BlockSpec((1, tr, cols), lambda l, i: (l, i, 0))
    return _pcall(
        body, grid=(DEPTH, nr), in_specs=[st_spec(0), st_spec(1), par, par, par], out_specs=[par] * 4,
        out_shape=[SDS(w.shape, F32)] * 4,
        compiler_params=_cp(("arbitrary", "arbitrary")), name=name)(st0, st1, w, m, v)


def _adamw_small(g, w, m, v, *, name):
    def body(g_ref, w_ref, m_ref, v_ref, d_ref, nm_ref, nv_ref):
        d_ref[...], nm_ref[...], nv_ref[...] = _adam(g_ref[...], w_ref[...], m_ref[...], v_ref[...])

    return _pcall(body, out_shape=[SDS(g.shape, F32)] * 3, compiler_params=_cp(), name=name)(g, w, m, v)


def _pack_rows(arrays):
    flat = jnp.concatenate([a.reshape(-1).astype(F32) for a in arrays])
    rows = -(-flat.shape[0] // (8 * D)) * 8
    return jnp.pad(flat, (0, rows * D - flat.shape[0])).reshape(rows, D)


def _unpack_rows(pack, like):
    flat = pack.reshape(-1)
    out, at = [], 0
    for a in like:
        out.append(flat[at:at + a.size].reshape(a.shape))
        at += a.size
    return out


def _layer_fwd(x, h, mem, wts, small, g_next, tag):
    win, wup, wkv, w4, wdn, cw_a, cw_b, cw_f = wts
    proj = _mm(h, win, tn=1024, name=f"proj_{tag}")
    za = _bra_fwd(proj, cw_a, name=f"bra_fwd_{tag}")
    cb = _brb_conv_fwd(proj, cw_b, small["conv_b_bias"], name=f"brb_conv_fwd_{tag}")
    sb = _ln_silu_fwd(cb, small["ln_b_g"], small["ln_b_b"], name=f"ln_silu_fwd_{tag}")
    memn, kv = _kv_prep(mem, small["norm_mem_g"], wkv, name=f"kv_prep_{tag}")
    o = _attn_fwd(proj, kv, name=f"attn_fwd_{tag}")
    ya, yb, yc, mg, x1, h2 = _mix_out(x, za, sb, o, proj, w4, small["b_gate"], small["norm_ffn_g"],
                                      name=f"mix_out_{tag}")
    u2 = _mm(h2, wup, tn=768, name=f"up_{tag}")
    act = _ffn_act(u2, cw_f, name=f"ffn_act_{tag}")
    x2, h_next = _mm_res_norm(act, wdn, x1, g_next, name=f"down_{tag}")
    saved = dict(x=x, h=h, proj=proj, za=za, cb=cb, sb=sb, memn=memn, kv=kv, o=o, ya=ya, yb=yb, yc=yc,
                 mg=mg, x1=x1, h2=h2, u2=u2, act=act)
    return x2, h_next, saved


def _layer_bwd(dx2, dx2b, mem, wts, small, sv, tag):
    win, wup, wkv, w4, wdn, cw_a, cw_b, cw_f = wts
    dact = _mm(dx2b, wdn, tb=True, tn=768, name=f"d_act_{tag}")
    dwdn = _mm(sv["act"], dx2b, ta=True, tm=768, tn=1024, tk=512, name=f"dw_down_{tag}")
    du2, dcw_f = _ffn_bwd(sv["u2"], dact, cw_f, name=f"ffn_bwd_{tag}")
    dx1, dx1b, dg_ffn = _mm_nt_normbwd(du2, wup, sv["x1"], dx2, small["norm_ffn_g"], tk=1536, name=f"d_h2_{tag}")
    dwup = _mm(sv["h2"], du2, ta=True, tm=1024, tn=768, tk=512, name=f"dw_up_{tag}")

    dya, dyb, dyc, dza, dsb, do, dgate, dbg = _mix_bwd(dx1b, sv["ya"], sv["yb"], sv["yc"], sv["proj"], w4,
                                                      small["b_gate"], name=f"mix_bwd_{tag}")
    dw4 = jnp.stack([
        _mm(a, b, ta=True, tm=1024, tn=1024, tk=512, name=f"dw_{nm}_{tag}")
        for nm, a, b in (("a_out", sv["za"], dya), ("b_out", sv["sb"], dyb), ("att_out", sv["o"], dyc),
                         ("o", sv["mg"], dx1b))])
    d_a, dcw_a = _bra_bwd(sv["proj"], dza, cw_a, name=f"bra_bwd_{tag}")
    dcb, ln_sums = _ln_silu_bwd(sv["cb"], dsb, small["ln_b_g"], small["ln_b_b"], name=f"ln_silu_bwd_{tag}")
    d_b, dcw_b = _brb_conv_bwd(sv["proj"], dcb, cw_b, name=f"brb_conv_bwd_{tag}")
    dq, dk, dv = _attn_bwd(sv["proj"], sv["kv"], do, name=f"attn_bwd_{tag}")
    dwkv, dg_mem = _kv_bwd(mem, small["norm_mem_g"], sv["memn"], dk, dv, wkv, name=f"kv_bwd_{tag}")
    dproj = jnp.concatenate([d_a, d_b, dq, dgate], axis=1)
    dx, dxb, dg_mix = _mm_nt_normbwd(dproj, win, sv["x"], dx1, small["norm_mix_g"], tk=1536, name=f"d_h_{tag}")
    dwin = _mm(sv["h"], dproj, ta=True, tm=1024, tn=1024, tk=512, name=f"dw_in_{tag}")

    small_grads = [dg_mix[0:1], dg_mem[0:1], dbg[0:1].reshape(3, D), ln_sums[2:3], ln_sums[0:1], ln_sums[1:2],
                   dg_ffn[0:1], dcw_a[0:K_A], dcw_b[0:K_B], dcw_f[0:K_F].reshape(K_F * 2 * FF_P // D, D)]
    return dx, dxb, (dwin, dwup, dwkv, dw4, dwdn), small_grads


_SMALL_ROWS = (1, 1, 3, 1, 1, 1, 1, K_A, K_B, K_F * 2 * FF_P // D)
_CV_ROWS = 48


def kernel(x, mem, norm_mix_g, norm_mem_g, w_in, b_gate, conv_a_w, w_a_out, conv_b_w, conv_b_bias, ln_b_g, ln_b_b, w_b_out, w_kv, w_att_out, w_o, norm_ffn_g, w_up, conv_ffn_w, w_down, norm_final_g, loss_target, m_norm_mix_g, m_norm_mem_g, m_w_in, m_b_gate, m_conv_a_w, m_w_a_out, m_conv_b_w, m_conv_b_bias, m_ln_b_g, m_ln_b_b, m_w_b_out, m_w_kv, m_w_att_out, m_w_o, m_norm_ffn_g, m_w_up, m_conv_ffn_w, m_w_down, m_norm_final_g, v_norm_mix_g, v_norm_mem_g, v_w_in, v_b_gate, v_conv_a_w, v_w_a_out, v_conv_b_w, v_conv_b_bias, v_ln_b_g, v_ln_b_b, v_w_b_out, v_w_kv, v_w_att_out, v_w_o, v_norm_ffn_g, v_w_up, v_conv_ffn_w, v_w_down, v_norm_final_g):
    me = _my_index()
    x0, mem0, tgt = x.reshape(x.shape[1:]), mem.reshape(mem.shape[1:]), loss_target.reshape(x.shape[1:])
    up_pad = ((0, 0), (0, 0), (0, C_UP_P - C_UP))

    wts, smalls = [], []
    for l in range(DEPTH):
        cv = jnp.zeros((_CV_ROWS, C_UP_P), F32)
        cv = cv.at[0:K_F, 0:C_UP].set(conv_ffn_w[l]).at[3:3 + K_A, 0:R_O].set(conv_a_w[l])
        cv = cv.at[8:8 + K_B, 0:R_O].set(conv_b_w[l])
        win, wup, wkv, w4, wdn, cvg = _allgather_weights(
            w_in[l].astype(BF), jnp.pad(w_up[l], up_pad[1:]).astype(BF), w_kv[l].astype(BF),
            jnp.stack([w_a_out[l], w_b_out[l], w_att_out[l], w_o[l]]).astype(BF), w_down[l].astype(BF), cv,
            name=f"allgather_l{l}")
        cw_f = jnp.stack([cvg[d, 0:K_F, :] for d in UP_ORDER], axis=1).reshape(K_F, 2 * FF_P)
        cw_a = cvg[:, 3:3 + K_A, 0:R_O].transpose(1, 0, 2).reshape(K_A, D)
        cw_b = cvg[:, 8:8 + K_B, 0:R_O].transpose(1, 0, 2).reshape(K_B, D)
        wts.append((win, wup, wkv, w4, wdn, jnp.pad(cw_a, ((0, 8 - K_A), (0, 0))),
                    jnp.pad(cw_b, ((0, 32 - K_B), (0, 0))), jnp.pad(cw_f, ((0, 8 - K_F), (0, 0)))))
        smalls.append(dict(
            norm_mix_g=norm_mix_g[l][None], norm_mem_g=norm_mem_g[l][None], b_gate=b_gate[l][None],
            conv_b_bias=conv_b_bias[l][None], ln_b_g=ln_b_g[l][None], ln_b_b=ln_b_b[l][None],
            norm_ffn_g=norm_ffn_g[l][None]))

    saved = []
    xs = x0
    h = _rms_fwd(xs, smalls[0]["norm_mix_g"], name="rms_fwd")
    for l in range(DEPTH):
        g_next = smalls[l + 1]["norm_mix_g"] if l + 1 < DEPTH else norm_final_g[None]
        xs, h, sv = _layer_fwd(xs, h, mem0, wts[l], smalls[l], g_next, f"l{l}")
        saved.append(sv)
    dx, dxb, head_sums = _loss_head(xs, tgt, norm_final_g[None], name="loss_head")

    staged = [None] * DEPTH
    small_grads = [None] * DEPTH
    for l in reversed(range(DEPTH)):
        dx, dxb, big, small_grads[l] = _layer_bwd(dx, dxb, mem0, wts[l], smalls[l], saved[l], f"l{l}")
        staged[l] = _scatter_grads(*big, name=f"scatter_grads_l{l}")

    pack = jnp.concatenate(small_grads[0] + small_grads[1] + [head_sums[1:2], head_sums[0:1]], axis=0)
    pack = jnp.pad(pack, ((0, -pack.shape[0] % 8), (0, 0)))
    total = _allreduce_small(pack, name="allreduce_small")
    per_layer = sum(_SMALL_ROWS)
    parts = []
    for l in range(DEPTH):
        at, one = l * per_layer, []
        for rows in _SMALL_ROWS:
            one.append(total[at:at + rows])
            at += rows
        parts.append(one)
    g_final = total[DEPTH * per_layer]
    loss = 0.5 / D * jnp.sum(total[DEPTH * per_layer + 1])

    def both(i):
        return jnp.stack([parts[0][i], parts[1][i]])

    g_norm_mix, g_norm_mem = both(0)[:, 0], both(1)[:, 0]
    g_b_gate = both(2).reshape(DEPTH, 3 * D)
    g_cbias, g_lng, g_lnb, g_norm_ffn = both(3)[:, 0], both(4)[:, 0], both(5)[:, 0], both(6)[:, 0]
    g_conv_a = lax.dynamic_slice_in_dim(both(7), me * R_O, R_O, axis=2)
    g_conv_b = lax.dynamic_slice_in_dim(both(8), me * R_O, R_O, axis=2)
    g_conv_f = lax.dynamic_slice_in_dim(both(9).reshape(DEPTH, K_F, 2 * FF_P), _up_slot(me) * C_UP_P, C_UP, axis=2)

    small_g = [g_norm_mix, g_norm_mem, g_b_gate, g_conv_a, g_conv_b, g_cbias, g_lng, g_lnb, g_norm_ffn, g_conv_f,
               g_final]
    small_w = [norm_mix_g, norm_mem_g, b_gate, conv_a_w, conv_b_w, conv_b_bias, ln_b_g, ln_b_b, norm_ffn_g,
               conv_ffn_w, norm_final_g]
    small_m = [m_norm_mix_g, m_norm_mem_g, m_b_gate, m_conv_a_w, m_conv_b_w, m_conv_b_bias, m_ln_b_g, m_ln_b_b,
               m_norm_ffn_g, m_conv_ffn_w, m_norm_final_g]
    small_v = [v_norm_mix_g, v_norm_mem_g, v_b_gate, v_conv_a_w, v_conv_b_w, v_conv_b_bias, v_ln_b_g, v_ln_b_b,
               v_norm_ffn_g, v_conv_ffn_w, v_norm_final_g]
    upd = _adamw_small(_pack_rows(small_g), _pack_rows(small_w), _pack_rows(small_m), _pack_rows(small_v),
                       name="adamw_small")
    s_d, s_m, s_v = (_unpack_rows(p, small_w) for p in upd)
    (d_norm_mix, d_norm_mem, d_b_gate, d_conv_a, d_conv_b, d_cbias, d_lng, d_lnb, d_norm_ffn, d_conv_f,
     d_final) = s_d
    (nm_norm_mix, nm_norm_mem, nm_b_gate, nm_conv_a, nm_conv_b, nm_cbias, nm_lng, nm_lnb, nm_norm_ffn, nm_conv_f,
     nm_final) = s_m
    (nv_norm_mix, nv_norm_mem, nv_b_gate, nv_conv_a, nv_conv_b, nv_cbias, nv_lng, nv_lnb, nv_norm_ffn, nv_conv_f,
     nv_final) = s_v

    def big_update(i, w, m, v, name):
        return _adamw_staged(staged[0][i], staged[1][i], w, m, v, name=name)

    r_in = big_update(0, w_in, m_w_in, v_w_in, "adamw_w_in")
    r_up = [a[:, :, 0:C_UP] for a in _adamw_staged(
        staged[0][1], staged[1][1], jnp.pad(w_up, up_pad), jnp.pad(m_w_up, up_pad), jnp.pad(v_w_up, up_pad),
        name="adamw_w_up")]
    r_kv = big_update(2, w_kv, m_w_kv, v_w_kv, "adamw_w_kv")
    r_dn = big_update(4, w_down, m_w_down, v_w_down, "adamw_w_down")

    def four(a, b, c, d_):
        return jnp.stack([a, b, c, d_], axis=1).reshape(DEPTH, 4 * R_O, D)

    r_4 = _adamw_staged(
        staged[0][3].reshape(N_DEV, 4 * R_O, D), staged[1][3].reshape(N_DEV, 4 * R_O, D),
        four(w_a_out, w_b_out, w_att_out, w_o), four(m_w_a_out, m_w_b_out, m_w_att_out, m_w_o),
        four(v_w_a_out, v_w_b_out, v_w_att_out, v_w_o), name="adamw_w_out")
    r_a, r_b, r_att, r_o = ([a.reshape(DEPTH, 4, R_O, D)[:, j] for a in r_4] for j in range(4))

    grads = [g_norm_mix, g_norm_mem, r_in[0], g_b_gate, g_conv_a, r_a[0], g_conv_b, g_cbias, g_lng, g_lnb, r_b[0],
             r_kv[0], r_att[0], r_o[0], g_norm_ffn, r_up[0], g_conv_f, r_dn[0], g_final]
    deltas = [d_norm_mix, d_norm_mem, r_in[1], d_b_gate, d_conv_a, r_a[1], d_conv_b, d_cbias, d_lng, d_lnb, r_b[1],
              r_kv[1], r_att[1], r_o[1], d_norm_ffn, r_up[1], d_conv_f, r_dn[1], d_final]
    new_m = [nm_norm_mix, nm_norm_mem, r_in[2], nm_b_gate, nm_conv_a, r_a[2], nm_conv_b, nm_cbias, nm_lng, nm_lnb,
             r_b[2], r_kv[2], r_att[2], r_o[2], nm_norm_ffn, r_up[2], nm_conv_f, r_dn[2], nm_final]
    new_v = [nv_norm_mix, nv_norm_mem, r_in[3], nv_b_gate, nv_conv_a, r_a[3], nv_conv_b, nv_cbias, nv_lng, nv_lnb,
             r_b[3], r_kv[3], r_att[3], r_o[3], nv_norm_ffn, r_up[3], nv_conv_f, r_dn[3], nv_final]
    return (loss, dx[None], *grads, *deltas, *new_m, *new_v)
```

```python
import functools

import jax
import jax.numpy as jnp
from jax import lax
from jax.experimental import pallas as pl
from jax.experimental.pallas import tpu as pltpu

F32 = jnp.float32
BF = jnp.bfloat16
SDS = jax.ShapeDtypeStruct
MESH = pl.DeviceIdType.MESH
ANY = pl.BlockSpec(memory_space=pl.ANY)

N_DEV = 8
DEPTH = 2
D = 1024
N_HEADS = 4
HEAD = D // N_HEADS
D_FF = 2816
K_A, K_B, K_F = 3, 31, 3
NORM_EPS = 1e-6

C_IN = 9 * D // N_DEV
C_KV = 2 * D // N_DEV
C_UP = 2 * D_FF // N_DEV
LANE = 128
C_UP_P = -(-C_UP // LANE) * LANE
FF_P = 4 * C_UP_P
R_O = D // N_DEV
R_DN = D_FF // N_DEV

VMEM_LIMIT = 56 * 1024 * 1024
TM = 512
TR = 256
H_S, H_L = 16, 32

ADAM_LR, ADAM_B1, ADAM_B2, ADAM_EPS, ADAM_WD, ADAM_STEP = 0.001, 0.9, 0.999, 1e-08, 0.01, 10

UP_ORDER = (0, 4, 1, 5, 2, 6, 3, 7)


def _pcall(body, **kw):
    return pl.pallas_call(body, **kw)


def _cp(sem=None, **kw):
    return pltpu.CompilerParams(dimension_semantics=sem, vmem_limit_bytes=VMEM_LIMIT, **kw)


def _dot(a, b):
    return jnp.dot(a, b, preferred_element_type=F32)


def _dot_nt(a, b):
    return lax.dot_general(a, b, (((1,), (1,)), ((), ())), preferred_element_type=F32)


def _dot_tn(a, b):
    return lax.dot_general(a, b, (((0,), (0,)), ((), ())), preferred_element_type=F32)


def _sigmoid(z):
    return 1.0 / (1.0 + jnp.exp(-z))


def _rms(xv):
    return lax.rsqrt(jnp.mean(xv * xv, axis=-1, keepdims=True) + NORM_EPS)


def _up_slot(idx):
    return jnp.where(idx < 4, 2 * idx, 2 * (idx - 4) + 1)


def _dn_row(idx):
    return C_UP_P * (idx // 2) + R_DN * (idx % 2)


def _mm(a, b, *, ta=False, tb=False, out_dtype=BF, tm=TM, tn=512, tk=None, name):
    m, k_dim = (a.shape[1], a.shape[0]) if ta else a.shape
    n = b.shape[0] if tb else b.shape[1]
    tm, tn = min(tm, m), min(tn, n)
    tk = k_dim if tk is None else min(tk, k_dim)
    nk = k_dim // tk
    assert m % tm == 0 and n % tn == 0 and k_dim % tk == 0
    dims = (((0 if ta else 1,), (1 if tb else 0,)), ((), ()))

    def body(a_ref, b_ref, o_ref, *scratch):
        part = lax.dot_general(a_ref[...], b_ref[...], dims, preferred_element_type=F32)
        if nk == 1:
            o_ref[...] = part.astype(o_ref.dtype)
            return
        acc = scratch[0]
        k = pl.program_id(2)

        @pl.when(k == 0)
        def _():
            acc[...] = part

        @pl.when(k > 0)
        def _():
            acc[...] += part

        @pl.when(k == nk - 1)
        def _():
            o_ref[...] = acc[...].astype(o_ref.dtype)

    a_spec = pl.BlockSpec((tk, tm), lambda i, j, k: (k, i)) if ta else pl.BlockSpec((tm, tk), lambda i, j, k: (i, k))
    b_spec = pl.BlockSpec((tn, tk), lambda i, j, k: (j, k)) if tb else pl.BlockSpec((tk, tn), lambda i, j, k: (k, j))
    return _pcall(
        body, grid=(m // tm, n // tn, nk), in_specs=[a_spec, b_spec],
        out_specs=pl.BlockSpec((tm, tn), lambda i, j, k: (i, j)),
        out_shape=SDS((m, n), out_dtype),
        scratch_shapes=[pltpu.VMEM((tm, tn), F32)] if nk > 1 else [],
        compiler_params=_cp(("parallel", "parallel", "arbitrary")), name=name)(a, b)


def _mm_res_norm(a, w, x, g, *, name):
    s, k_dim = a.shape
    tm = min(TM, s)

    def body(a_ref, w_ref, x_ref, g_ref, xo_ref, h_ref):
        xo = x_ref[...] + _dot(a_ref[...], w_ref[...])
        xo_ref[...] = xo
        h_ref[...] = ((xo * _rms(xo)) * g_ref[...]).astype(BF)

    return _pcall(
        body, grid=(s // tm,),
        in_specs=[pl.BlockSpec((tm, k_dim), lambda i: (i, 0)), pl.BlockSpec((k_dim, D), lambda i: (0, 0)),
                  pl.BlockSpec((tm, D), lambda i: (i, 0)), pl.BlockSpec((1, D), lambda i: (0, 0))],
        out_specs=[pl.BlockSpec((tm, D), lambda i: (i, 0))] * 2,
        out_shape=[SDS((s, D), F32), SDS((s, D), BF)],
        compiler_params=_cp(("parallel",)), name=name)(a, w, x, g)


def _mm_nt_normbwd(da, w, x, dres, g, *, tk, name):
    s, k_dim = da.shape
    tm = min(TM, s)
    nk = k_dim // tk
    assert k_dim % tk == 0

    def body(da_ref, w_ref, x_ref, dres_ref, g_ref, dx_ref, dxb_ref, dg_ref, acc):
        i, k = pl.program_id(0), pl.program_id(1)
        part = _dot_nt(da_ref[...], w_ref[...])

        @pl.when(k == 0)
        def _():
            acc[...] = part

        @pl.when(k > 0)
        def _():
            acc[...] += part

        @pl.when((i == 0) & (k == 0))
        def _():
            dg_ref[...] = jnp.zeros_like(dg_ref)

        @pl.when(k == nk - 1)
        def _():
            dh = acc[...]
            xv = x_ref[...]
            r = _rms(xv)
            xn = xv * r
            dg_ref[0:1, :] += jnp.sum(dh * xn, axis=0, keepdims=True)
            dxn = dh * g_ref[...]
            dx = dres_ref[...] + r * (dxn - xn * jnp.mean(dxn * xn, axis=-1, keepdims=True))
            dx_ref[...] = dx
            dxb_ref[...] = dx.astype(BF)

    row = lambda i, k: (i, 0)
    return _pcall(
        body, grid=(s // tm, nk),
        in_specs=[pl.BlockSpec((tm, tk), lambda i, k: (i, k)), pl.BlockSpec((D, tk), lambda i, k: (0, k)),
                  pl.BlockSpec((tm, D), row), pl.BlockSpec((tm, D), row), pl.BlockSpec((1, D), lambda i, k: (0, 0))],
        out_specs=[pl.BlockSpec((tm, D), row), pl.BlockSpec((tm, D), row), pl.BlockSpec((8, D), lambda i, k: (0, 0))],
        out_shape=[SDS((s, D), F32), SDS((s, D), BF), SDS((8, D), F32)],
        scratch_shapes=[pltpu.VMEM((tm, D), F32)],
        compiler_params=_cp(("arbitrary", "arbitrary")), name=name)(da, w, x, dres, g)


def _rms_fwd(x, g, *, name):
    s = x.shape[0]
    tm = min(TM, s)

    def body(x_ref, g_ref, h_ref):
        xv = x_ref[...]
        h_ref[...] = ((xv * _rms(xv)) * g_ref[...]).astype(BF)

    return _pcall(
        body, grid=(s // tm,),
        in_specs=[pl.BlockSpec((tm, D), lambda i: (i, 0)), pl.BlockSpec((1, D), lambda i: (0, 0))],
        out_specs=pl.BlockSpec((tm, D), lambda i: (i, 0)), out_shape=SDS((s, D), BF),
        compiler_params=_cp(("parallel",)), name=name)(x, g)


def _loss_head(x, tgt, g, *, name):
    s = x.shape[0]
    tm = min(TM, s)

    def body(x_ref, t_ref, g_ref, dx_ref, dxb_ref, sums_ref):
        @pl.when(pl.program_id(0) == 0)
        def _():
            sums_ref[...] = jnp.zeros_like(sums_ref)

        xv = x_ref[...]
        r = _rms(xv)
        xn = xv * r
        diff = xn * g_ref[...] - t_ref[...]
        sums_ref[0:1, :] += jnp.sum(diff * diff, axis=0, keepdims=True)
        dy = diff * (1.0 / D)
        sums_ref[1:2, :] += jnp.sum(dy * xn, axis=0, keepdims=True)
        dxn = dy * g_ref[...]
        dx = r * (dxn - xn * jnp.mean(dxn * xn, axis=-1, keepdims=True))
        dx_ref[...] = dx
        dxb_ref[...] = dx.astype(BF)

    row = lambda i: (i, 0)
    return _pcall(
        body, grid=(s // tm,),
        in_specs=[pl.BlockSpec((tm, D), row), pl.BlockSpec((tm, D), row), pl.BlockSpec((1, D), lambda i: (0, 0))],
        out_specs=[pl.BlockSpec((tm, D), row), pl.BlockSpec((tm, D), row), pl.BlockSpec((8, D), lambda i: (0, 0))],
        out_shape=[SDS((s, D), F32), SDS((s, D), BF), SDS((8, D), F32)],
        compiler_params=_cp(("arbitrary",)), name=name)(x, tgt, g)


def _halo_before(i, tr, h):
    return jnp.maximum(i * (tr // h) - 1, 0)


def _halo_after(i, tr, h, s):
    return jnp.minimum((i + 1) * (tr // h), s // h - 1)


def _taps(buf, w_ref, sl, k_w, base, rows):
    acc = None
    for k in range(k_w):
        t = w_ref[k:k + 1, sl] * buf[base + k:base + k + rows, sl]
        acc = t if acc is None else acc + t
    return acc


def _taps_rev(buf, w_ref, sl, k_w, rows):
    acc = None
    for k in range(k_w):
        t = w_ref[k:k + 1, sl] * buf[k_w - 1 - k:k_w - 1 - k + rows, sl]
        acc = t if acc is None else acc + t
    return acc


def _tap_grads(dw_ref, dc, buf, sl, k_w, base, rows):
    for k in range(k_w):
        dw_ref[k:k + 1, sl] += jnp.sum(dc * buf[base + k:base + k + rows, sl], axis=0, keepdims=True)


def _bra_fwd(proj, cw, *, name):
    s = proj.shape[0]
    tr, h = min(TR, s), H_S

    def body(cur, halo, w_ref, za_ref, cvb):
        i = pl.program_id(0)
        hv = halo[:, D:2 * D].astype(F32) * halo[:, 2 * D:3 * D].astype(F32)
        cvb[0:h, :] = jnp.where(i == 0, 0.0, hv)
        cvb[h:h + tr, :] = cur[:, D:2 * D].astype(F32) * cur[:, 2 * D:3 * D].astype(F32)
        for c in range(D // LANE):
            sl = slice(LANE * c, LANE * c + LANE)
            ca = _taps(cvb, w_ref, sl, K_A, h - (K_A - 1), tr)
            za_ref[:, sl] = (cur[:, sl].astype(F32) * ca).astype(BF)

    return _pcall(
        body, grid=(s // tr,),
        in_specs=[pl.BlockSpec((tr, 3 * D), lambda i: (i, 0)),
                  pl.BlockSpec((h, 3 * D), lambda i: (_halo_before(i, tr, h), 0)),
                  pl.BlockSpec((8, D), lambda i: (0, 0))],
        out_specs=pl.BlockSpec((tr, D), lambda i: (i, 0)), out_shape=SDS((s, D), BF),
        scratch_shapes=[pltpu.VMEM((h + tr, D), F32)],
        compiler_params=_cp(("parallel",)), name=name)(proj, proj, cw)


def _bra_bwd(proj, dza, cw, *, name):
    s = proj.shape[0]
    tr, h = min(TR, s), H_S
    n = s // tr

    def body(before, cur, after, dz_cur, dz_after, w_ref, da_ref, dw_ref, cvb, dcab):
        i = pl.program_id(0)

        @pl.when(i == 0)
        def _():
            dw_ref[...] = jnp.zeros_like(dw_ref)

        first, last = i == 0, i == n - 1
        cvb[0:h, :] = jnp.where(first, 0.0, before[:, D:2 * D].astype(F32) * before[:, 2 * D:3 * D].astype(F32))
        cvb[h:h + tr, :] = cur[:, D:2 * D].astype(F32) * cur[:, 2 * D:3 * D].astype(F32)
        dcab[0:tr, :] = dz_cur[...].astype(F32) * cur[:, 0:D].astype(F32)
        dcab[tr:tr + h, :] = jnp.where(last, 0.0, dz_after[...].astype(F32) * after[:, 0:D].astype(F32))
        for c in range(D // LANE):
            sl = slice(LANE * c, LANE * c + LANE)
            gc = cur[:, D + LANE * c:D + LANE * c + LANE].astype(F32)
            v = cur[:, 2 * D + LANE * c:2 * D + LANE * c + LANE].astype(F32)
            ca = _taps(cvb, w_ref, sl, K_A, h - (K_A - 1), tr)
            da_ref[:, sl] = (dz_cur[:, sl].astype(F32) * ca).astype(BF)
            dcv = _taps_rev(dcab, w_ref, sl, K_A, tr)
            da_ref[:, D + LANE * c:D + LANE * c + LANE] = (dcv * v).astype(BF)
            da_ref[:, 2 * D + LANE * c:2 * D + LANE * c + LANE] = (dcv * gc).astype(BF)
            _tap_grads(dw_ref, dcab[0:tr, sl], cvb, sl, K_A, h - (K_A - 1), tr)

    return _pcall(
        body, grid=(n,),
        in_specs=[pl.BlockSpec((h, 3 * D), lambda i: (_halo_before(i, tr, h), 0)),
                  pl.BlockSpec((tr, 3 * D), lambda i: (i, 0)),
                  pl.BlockSpec((h, 3 * D), lambda i: (_halo_after(i, tr, h, s), 0)),
                  pl.BlockSpec((tr, D), lambda i: (i, 0)),
                  pl.BlockSpec((h, D), lambda i: (_halo_after(i, tr, h, s), 0)),
                  pl.BlockSpec((8, D), lambda i: (0, 0))],
        out_specs=[pl.BlockSpec((tr, 3 * D), lambda i: (i, 0)), pl.BlockSpec((8, D), lambda i: (0, 0))],
        out_shape=[SDS((s, 3 * D), BF), SDS((8, D), F32)],
        scratch_shapes=[pltpu.VMEM((h + tr, D), F32), pltpu.VMEM((tr + h, D), F32)],
        compiler_params=_cp(("arbitrary",)), name=name)(proj, proj, proj, dza, dza, cw)


_U_COL, _UG_COL = 3, 4


def _brb_conv_fwd(proj, cw, bias, *, name):
    s = proj.shape[0]
    tr, h = min(TR, s), H_L

    def body(u_cur, ug_cur, u_halo, ug_halo, w_ref, b_ref, cb_ref, glb):
        i = pl.program_id(0)
        glb[0:h, :] = jnp.where(i == 0, 0.0, u_halo[...].astype(F32) * _sigmoid(ug_halo[...].astype(F32)))
        glb[h:h + tr, :] = u_cur[...].astype(F32) * _sigmoid(ug_cur[...].astype(F32))
        for c in range(D // LANE):
            sl = slice(LANE * c, LANE * c + LANE)
            cb_ref[:, sl] = (_taps(glb, w_ref, sl, K_B, h - (K_B - 1), tr) + b_ref[:, sl]).astype(BF)

    return _pcall(
        body, grid=(s // tr,),
        in_specs=[pl.BlockSpec((tr, D), lambda i: (i, _U_COL)), pl.BlockSpec((tr, D), lambda i: (i, _UG_COL)),
                  pl.BlockSpec((h, D), lambda i: (_halo_before(i, tr, h), _U_COL)),
                  pl.BlockSpec((h, D), lambda i: (_halo_before(i, tr, h), _UG_COL)),
                  pl.BlockSpec((32, D), lambda i: (0, 0)), pl.BlockSpec((1, D), lambda i: (0, 0))],
        out_specs=pl.BlockSpec((tr, D), lambda i: (i, 0)), out_shape=SDS((s, D), BF),
        scratch_shapes=[pltpu.VMEM((h + tr, D), F32)],
        compiler_params=_cp(("parallel",)), name=name)(proj, proj, proj, proj, cw, bias)


def _brb_conv_bwd(proj, dcb, cw, *, name):
    s = proj.shape[0]
    tr, h = min(TR, s), H_L
    n = s // tr

    def body(u_before, ug_before, u_cur, ug_cur, d_cur, d_after, w_ref, db_ref, dw_ref, glb, dcbb):
        i = pl.program_id(0)

        @pl.when(i == 0)
        def _():
            dw_ref[...] = jnp.zeros_like(dw_ref)

        glb[0:h, :] = jnp.where(i == 0, 0.0, u_before[...].astype(F32) * _sigmoid(ug_before[...].astype(F32)))
        glb[h:h + tr, :] = u_cur[...].astype(F32) * _sigmoid(ug_cur[...].astype(F32))
        dcbb[0:tr, :] = d_cur[...].astype(F32)
        dcbb[tr:tr + h, :] = jnp.where(i == n - 1, 0.0, d_after[...].astype(F32))
        for c in range(D // LANE):
            sl = slice(LANE * c, LANE * c + LANE)
            dglu = _taps_rev(dcbb, w_ref, sl, K_B, tr)
            u = u_cur[:, sl].astype(F32)
            sg = _sigmoid(ug_cur[:, sl].astype(F32))
            db_ref[:, sl] = (dglu * sg).astype(BF)
            db_ref[:, D + LANE * c:D + LANE * c + LANE] = (dglu * u * sg * (1.0 - sg)).astype(BF)
            _tap_grads(dw_ref, dcbb[0:tr, sl], glb, sl, K_B, h - (K_B - 1), tr)

    return _pcall(
        body, grid=(n,),
        in_specs=[pl.BlockSpec((h, D), lambda i: (_halo_before(i, tr, h), _U_COL)),
                  pl.BlockSpec((h, D), lambda i: (_halo_before(i, tr, h), _UG_COL)),
                  pl.BlockSpec((tr, D), lambda i: (i, _U_COL)), pl.BlockSpec((tr, D), lambda i: (i, _UG_COL)),
                  pl.BlockSpec((tr, D), lambda i: (i, 0)),
                  pl.BlockSpec((h, D), lambda i: (_halo_after(i, tr, h, s), 0)),
                  pl.BlockSpec((32, D), lambda i: (0, 0))],
        out_specs=[pl.BlockSpec((tr, 2 * D), lambda i: (i, 0)), pl.BlockSpec((32, D), lambda i: (0, 0))],
        out_shape=[SDS((s, 2 * D), BF), SDS((32, D), F32)],
        scratch_shapes=[pltpu.VMEM((h + tr, D), F32), pltpu.VMEM((tr + h, D), F32)],
        compiler_params=_cp(("arbitrary",)), name=name)(proj, proj, proj, proj, dcb, dcb, cw)


def _ln_silu_fwd(cb, g, b, *, name):
    s = cb.shape[0]
    tm = min(TM, s)

    def body(cb_ref, g_ref, b_ref, sb_ref):
        z = cb_ref[...].astype(F32)
        zc = z - jnp.mean(z, axis=-1, keepdims=True)
        ln = (zc * lax.rsqrt(jnp.mean(zc * zc, axis=-1, keepdims=True) + NORM_EPS)) * g_ref[...] + b_ref[...]
        sb_ref[...] = (ln * _sigmoid(ln)).astype(BF)

    row = lambda i: (i, 0)
    vec = pl.BlockSpec((1, D), lambda i: (0, 0))
    return _pcall(
        body, grid=(s // tm,), in_specs=[pl.BlockSpec((tm, D), row), vec, vec],
        out_specs=pl.BlockSpec((tm, D), row), out_shape=SDS((s, D), BF),
        compiler_params=_cp(("parallel",)), name=name)(cb, g, b)


def _ln_silu_bwd(cb, dsb, g, b, *, name):
    s = cb.shape[0]
    tm = min(TM, s)

    def body(cb_ref, dsb_ref, g_ref, b_ref, dcb_ref, sums_ref):
        @pl.when(pl.program_id(0) == 0)
        def _():
            sums_ref[...] = jnp.zeros_like(sums_ref)

        z = cb_ref[...].astype(F32)
        zc = z - jnp.mean(z, axis=-1, keepdims=True)
        rstd = lax.rsqrt(jnp.mean(zc * zc, axis=-1, keepdims=True) + NORM_EPS)
        lnh = zc * rstd
        ln = lnh * g_ref[...] + b_ref[...]
        sg = _sigmoid(ln)
        dln = dsb_ref[...].astype(F32) * (sg * (1.0 + ln * (1.0 - sg)))
        sums_ref[0:1, :] += jnp.sum(dln * lnh, axis=0, keepdims=True)
        sums_ref[1:2, :] += jnp.sum(dln, axis=0, keepdims=True)
        dlnh = dln * g_ref[...]
        dz = rstd * (dlnh - jnp.mean(dlnh, axis=-1, keepdims=True)
                     - lnh * jnp.mean(dlnh * lnh, axis=-1, keepdims=True))
        sums_ref[2:3, :] += jnp.sum(dz, axis=0, keepdims=True)
        dcb_ref[...] = dz.astype(BF)

    row = lambda i: (i, 0)
    vec = pl.BlockSpec((1, D), lambda i: (0, 0))
    return _pcall(
        body, grid=(s // tm,), in_specs=[pl.BlockSpec((tm, D), row), pl.BlockSpec((tm, D), row), vec, vec],
        out_specs=[pl.BlockSpec((tm, D), row), pl.BlockSpec((8, D), lambda i: (0, 0))],
        out_shape=[SDS((s, D), BF), SDS((8, D), F32)],
        compiler_params=_cp(("arbitrary",)), name=name)(cb, dsb, g, b)


_Q_COL = 5 * D // HEAD


def _kv_prep(mem, g, wkv, *, name):
    m = mem.shape[0]

    def body(mem_ref, g_ref, w_ref, memn_ref, kv_ref):
        mv = mem_ref[...]
        memn = ((mv * _rms(mv)) * g_ref[...]).astype(BF)
        memn_ref[...] = memn
        kv_ref[...] = _dot(memn, w_ref[...]).astype(BF)

    return _pcall(body, out_shape=[SDS((m, D), BF), SDS((m, 2 * D), BF)],
                  compiler_params=_cp(), name=name)(mem, g, wkv)


def _softmax_rows(q, k):
    sc = _dot_nt(q, k) * (1.0 / (HEAD ** 0.5))
    e = jnp.exp(sc - jnp.max(sc, axis=-1, keepdims=True))
    return e / jnp.sum(e, axis=-1, keepdims=True)


def _attn_fwd(proj, kv, *, name):
    s, m = proj.shape[0], kv.shape[0]
    tm = min(TM, s)

    def body(q_ref, k_ref, v_ref, o_ref):
        p = _softmax_rows(q_ref[...], k_ref[...])
        o_ref[...] = _dot(p.astype(BF), v_ref[...]).astype(BF)

    return _pcall(
        body, grid=(s // tm, N_HEADS),
        in_specs=[pl.BlockSpec((tm, HEAD), lambda i, hd: (i, _Q_COL + hd)),
                  pl.BlockSpec((m, HEAD), lambda i, hd: (0, hd)),
                  pl.BlockSpec((m, HEAD), lambda i, hd: (0, N_HEADS + hd))],
        out_specs=pl.BlockSpec((tm, HEAD), lambda i, hd: (i, hd)), out_shape=SDS((s, D), BF),
        compiler_params=_cp(("parallel", "parallel")), name=name)(proj, kv, kv)


def _attn_bwd(proj, kv, do, *, name):
    s, m = proj.shape[0], kv.shape[0]
    tm = min(TM, s)

    def body(q_ref, k_ref, v_ref, do_ref, dq_ref, dk_ref, dv_ref):
        @pl.when(pl.program_id(1) == 0)
        def _():
            dk_ref[...] = jnp.zeros_like(dk_ref)
            dv_ref[...] = jnp.zeros_like(dv_ref)

        q, k, dov = q_ref[...], k_ref[...], do_ref[...]
        p = _softmax_rows(q, k)
        dp = _dot_nt(dov, v_ref[...])
        dv_ref[...] += _dot_tn(p.astype(BF), dov)
        ds = (p * (dp - jnp.sum(dp * p, axis=-1, keepdims=True)) * (1.0 / (HEAD ** 0.5))).astype(BF)
        dq_ref[...] = _dot(ds, k).astype(BF)
        dk_ref[...] += _dot_tn(ds, q)

    return _pcall(
        body, grid=(N_HEADS, s // tm),
        in_specs=[pl.BlockSpec((tm, HEAD), lambda hd, i: (i, _Q_COL + hd)),
                  pl.BlockSpec((m, HEAD), lambda hd, i: (0, hd)),
                  pl.BlockSpec((m, HEAD), lambda hd, i: (0, N_HEADS + hd)),
                  pl.BlockSpec((tm, HEAD), lambda hd, i: (i, hd))],
        out_specs=[pl.BlockSpec((tm, HEAD), lambda hd, i: (i, hd)),
                   pl.BlockSpec((m, HEAD), lambda hd, i: (0, hd)),
                   pl.BlockSpec((m, HEAD), lambda hd, i: (0, hd))],
        out_shape=[SDS((s, D), BF), SDS((m, D), F32), SDS((m, D), F32)],
        compiler_params=_cp(("parallel", "arbitrary")), name=name)(proj, kv, kv, do)


def _kv_bwd(mem, g, memn, dk, dv, wkv, *, name):
    def body(mem_ref, g_ref, memn_ref, dk_ref, dv_ref, w_ref, dw_ref, dg_ref):
        dkb, dvb = dk_ref[...].astype(BF), dv_ref[...].astype(BF)
        memn = memn_ref[...]
        dw_ref[:, 0:D] = _dot_tn(memn, dkb).astype(BF)
        dw_ref[:, D:2 * D] = _dot_tn(memn, dvb).astype(BF)
        dmemn = _dot_nt(dkb, w_ref[:, 0:D]) + _dot_nt(dvb, w_ref[:, D:2 * D])
        mv = mem_ref[...]
        dg_ref[...] = jnp.zeros_like(dg_ref)
        dg_ref[0:1, :] = jnp.sum(dmemn * (mv * _rms(mv)), axis=0, keepdims=True)

    return _pcall(body, out_shape=[SDS((D, 2 * D), BF), SDS((8, D), F32)],
                  compiler_params=_cp(), name=name)(mem, g, memn, dk, dv, wkv)


_TM_MIX = 256


def _mix_out(x, za, sb, o, proj, w4, bg, g_next, *, name):
    s = x.shape[0]
    tm = min(_TM_MIX, s)

    def body(x_ref, za_ref, sb_ref, o_ref, pg_ref, w4_ref, bg_ref, gn_ref,
             ya_ref, yb_ref, yc_ref, mg_ref, x1_ref, h_ref):
        ys = (_dot(za_ref[...], w4_ref[0]), _dot(sb_ref[...], w4_ref[1]), _dot(o_ref[...], w4_ref[2]))
        merged = None
        for j, (y, y_ref) in enumerate(zip(ys, (ya_ref, yb_ref, yc_ref))):
            y_ref[...] = y.astype(BF)
            gate = _sigmoid(pg_ref[:, j * D:(j + 1) * D].astype(F32) + bg_ref[:, j * D:(j + 1) * D])
            merged = gate * y if merged is None else merged + gate * y
        mg = merged.astype(BF)
        mg_ref[...] = mg
        x1 = x_ref[...] + _dot(mg, w4_ref[3])
        x1_ref[...] = x1
        h_ref[...] = ((x1 * _rms(x1)) * gn_ref[...]).astype(BF)

    row = lambda i: (i, 0)
    act = pl.BlockSpec((tm, D), row)
    return _pcall(
        body, grid=(s // tm,),
        in_specs=[act, act, act, act, pl.BlockSpec((tm, 3 * D), lambda i: (i, 2)),
                  pl.BlockSpec((4, D, D), lambda i: (0, 0, 0)), pl.BlockSpec((1, 3 * D), lambda i: (0, 0)),
                  pl.BlockSpec((1, D), lambda i: (0, 0))],
        out_specs=[act] * 6,
        out_shape=[SDS((s, D), BF)] * 4 + [SDS((s, D), F32), SDS((s, D), BF)],
        compiler_params=_cp(("parallel",)), name=name)(x, za, sb, o, proj, w4, bg, g_next)


def _mix_bwd(dxb, ya, yb, yc, proj, w4, bg, *, name):
    s = dxb.shape[0]
    tm = min(_TM_MIX, s)

    def body(dx_ref, ya_ref, yb_ref, yc_ref, pg_ref, w4_ref, bg_ref,
             dya_ref, dyb_ref, dyc_ref, dza_ref, dsb_ref, do_ref, dgt_ref, dbg_ref):
        @pl.when(pl.program_id(0) == 0)
        def _():
            dbg_ref[...] = jnp.zeros_like(dbg_ref)

        dm = _dot_nt(dx_ref[...], w4_ref[3])
        for j, (y_ref, dy_ref, din_ref) in enumerate(zip((ya_ref, yb_ref, yc_ref), (dya_ref, dyb_ref, dyc_ref),
                                                         (dza_ref, dsb_ref, do_ref))):
            cols = slice(j * D, (j + 1) * D)
            gate = _sigmoid(pg_ref[:, cols].astype(F32) + bg_ref[:, cols])
            dy = (dm * gate).astype(BF)
            dy_ref[...] = dy
            din_ref[...] = _dot_nt(dy, w4_ref[j]).astype(BF)
            dpre = dm * y_ref[...].astype(F32) * gate * (1.0 - gate)
            dgt_ref[:, cols] = dpre.astype(BF)
            dbg_ref[0:1, cols] += jnp.sum(dpre, axis=0, keepdims=True)

    row = lambda i: (i, 0)
    act = pl.BlockSpec((tm, D), row)
    return _pcall(
        body, grid=(s // tm,),
        in_specs=[act, act, act, act, pl.BlockSpec((tm, 3 * D), lambda i: (i, 2)),
                  pl.BlockSpec((4, D, D), lambda i: (0, 0, 0)), pl.BlockSpec((1, 3 * D), lambda i: (0, 0))],
        out_specs=[act] * 6 + [pl.BlockSpec((tm, 3 * D), row), pl.BlockSpec((8, 3 * D), lambda i: (0, 0))],
        out_shape=[SDS((s, D), BF)] * 6 + [SDS((s, 3 * D), BF), SDS((8, 3 * D), F32)],
        compiler_params=_cp(("arbitrary",)), name=name)(dxb, ya, yb, yc, proj, w4, bg)


_PAIR = 2 * C_UP_P


def _ffn_act(u2, cw, *, name):
    s = u2.shape[0]
    tr, h = min(TR, s), H_S

    def body(cur, halo, w_ref, act_ref, ub):
        i = pl.program_id(1)
        ub[0:h, :] = jnp.where(i == 0, 0.0, halo[...].astype(F32))
        ub[h:h + tr, :] = cur[...].astype(F32)
        for c in range(C_UP_P // LANE):
            gl = slice(LANE * c, LANE * c + LANE)
            ul = slice(C_UP_P + LANE * c, C_UP_P + LANE * c + LANE)
            gt = _taps(ub, w_ref, gl, K_F, h - (K_F - 1), tr)
            up = _taps(ub, w_ref, ul, K_F, h - (K_F - 1), tr)
            act_ref[:, gl] = (gt * _sigmoid(gt) * up).astype(BF)

    return _pcall(
        body, grid=(4, s // tr),
        in_specs=[pl.BlockSpec((tr, _PAIR), lambda p, i: (i, p)),
                  pl.BlockSpec((h, _PAIR), lambda p, i: (_halo_before(i, tr, h), p)),
                  pl.BlockSpec((8, _PAIR), lambda p, i: (0, p))],
        out_specs=pl.BlockSpec((tr, C_UP_P), lambda p, i: (i, p)), out_shape=SDS((s, FF_P), BF),
        scratch_shapes=[pltpu.VMEM((h + tr, _PAIR), F32)],
        compiler_params=_cp(("parallel", "parallel")), name=name)(u2, u2, cw)


def _ffn_bwd(u2, dact, cw, *, name):
    s = u2.shape[0]
    tr, h = min(TR, s), H_S
    n = s // tr
    ext = tr + h

    def body(before, cur, after, da_cur, da_after, w_ref, du_ref, dw_ref, ub, dcb):
        i = pl.program_id(1)

        @pl.when(i == 0)
        def _():
            dw_ref[...] = jnp.zeros_like(dw_ref)

        ub[0:h, :] = jnp.where(i == 0, 0.0, before[...].astype(F32))
        ub[h:h + tr, :] = cur[...].astype(F32)
        ub[h + tr:h + tr + h, :] = jnp.where(i == n - 1, 0.0, after[...].astype(F32))
        for c in range(C_UP_P // LANE):
            gl = slice(LANE * c, LANE * c + LANE)
            ul = slice(C_UP_P + LANE * c, C_UP_P + LANE * c + LANE)
            gt = _taps(ub, w_ref, gl, K_F, h - (K_F - 1), ext)
            up = _taps(ub, w_ref, ul, K_F, h - (K_F - 1), ext)
            da = jnp.concatenate([da_cur[:, gl].astype(F32),
                                  jnp.where(i == n - 1, 0.0, da_after[:, gl].astype(F32))], axis=0)
            sg = _sigmoid(gt)
            dcb[:, gl] = da * up * (sg * (1.0 + gt * (1.0 - sg)))
            dcb[:, ul] = da * (gt * sg)
        for c in range(_PAIR // LANE):
            sl = slice(LANE * c, LANE * c + LANE)
            du_ref[:, sl] = _taps_rev(dcb, w_ref, sl, K_F, tr).astype(BF)
            _tap_grads(dw_ref, dcb[0:tr, sl], ub, sl, K_F, h - (K_F - 1), tr)

    return _pcall(
        body, grid=(4, n),
        in_specs=[pl.BlockSpec((h, _PAIR), lambda p, i: (_halo_before(i, tr, h), p)),
                  pl.BlockSpec((tr, _PAIR), lambda p, i: (i, p)),
                  pl.BlockSpec((h, _PAIR), lambda p, i: (_halo_after(i, tr, h, s), p)),
                  pl.BlockSpec((tr, C_UP_P), lambda p, i: (i, p)),
                  pl.BlockSpec((h, C_UP_P), lambda p, i: (_halo_after(i, tr, h, s), p)),
                  pl.BlockSpec((8, _PAIR), lambda p, i: (0, p))],
        out_specs=[pl.BlockSpec((tr, _PAIR), lambda p, i: (i, p)), pl.BlockSpec((8, _PAIR), lambda p, i: (0, p))],
        out_shape=[SDS((s, 2 * FF_P), BF), SDS((8, 2 * FF_P), F32)],
        scratch_shapes=[pltpu.VMEM((h + tr + h, _PAIR), F32), pltpu.VMEM((ext, _PAIR), F32)],
        compiler_params=_cp(("parallel", "arbitrary")), name=name)(u2, u2, u2, dact, dact, cw)


def _relations():
    x, y, c = lax.axis_index("x"), lax.axis_index("y"), lax.axis_index("c")
    out = []
    for r in range(1, N_DEV):
        rx, ry, rc = (r >> 2) & 1, (r >> 1) & 1, r & 1
        out.append((r, (x ^ rx, y ^ ry, c ^ rc)))
    return out


def _my_index():
    return 4 * lax.axis_index("x") + 2 * lax.axis_index("y") + lax.axis_index("c")


def _exchange(n_arrays, src_of, dst_of, refs):
    ssem, rsem, lsem = refs
    me = _my_index()
    local = []
    for a in range(n_arrays):
        loc = pltpu.make_async_copy(src_of(a, me), dst_of(a, me), lsem.at[a])
        loc.start()
        local.append(loc)

    def copy(a, r, peer, src_idx, dst_idx):
        return pltpu.make_async_remote_copy(
            src_ref=src_of(a, src_idx), dst_ref=dst_of(a, dst_idx), send_sem=ssem.at[a, r - 1],
            recv_sem=rsem.at[a, r - 1], device_id=peer, device_id_type=MESH)

    peers = [(r, peer, 4 * peer[0] + 2 * peer[1] + peer[2]) for r, peer in _relations()]
    for r, peer, p_idx in peers:
        for a in range(n_arrays):
            copy(a, r, peer, p_idx, me).start()
    for r, peer, p_idx in peers:
        for a in range(n_arrays):
            copy(a, r, peer, p_idx, me).wait_send()
            copy(a, r, peer, me, p_idx).wait_recv()
    for loc in local:
        loc.wait()


def _sem_scratch(n_arrays):
    return [pltpu.SemaphoreType.DMA((n_arrays, N_DEV - 1)), pltpu.SemaphoreType.DMA((n_arrays, N_DEV - 1)),
            pltpu.SemaphoreType.DMA((n_arrays,))]


def _slab(kind, ref, idx):
    if kind == "win":
        return ref.at[:, pl.ds(pl.multiple_of(idx * C_IN, LANE), C_IN)]
    if kind == "wup":
        return ref.at[:, pl.ds(pl.multiple_of(_up_slot(idx) * C_UP_P, LANE), C_UP_P)]
    if kind == "wkv":
        return ref.at[:, pl.ds(pl.multiple_of(idx * C_KV, LANE), C_KV)]
    if kind == "w4":
        return ref.at[:, pl.ds(pl.multiple_of(idx * R_O, 16), R_O), :]
    if kind == "wdn":
        return ref.at[pl.ds(pl.multiple_of(_dn_row(idx), 16), R_DN), :]
    assert kind == "cv"
    return ref.at[idx]


_WHOLE = {"win": ((D, 9 * D), BF), "wup": ((D, 2 * FF_P), BF), "wkv": ((D, 2 * D), BF), "w4": ((4, D, D), BF),
          "wdn": ((FF_P, D), BF)}
_SHARD = {"win": (D, C_IN), "wup": (D, C_UP_P), "wkv": (D, C_KV), "w4": (4, R_O, D), "wdn": (R_DN, D)}
HBM_SPEC = pl.BlockSpec(memory_space=pltpu.HBM)
SEM_SPEC = pl.BlockSpec(memory_space=pltpu.SEMAPHORE)
_DATAFLOW = pltpu.SideEffectType.DATAFLOW_SIDE_EFFECTING


def _gather_maps(kinds):
    return (lambda srcs, a, idx: srcs[a]), (lambda lands, a, idx: _slab(kinds[a], lands[a], idx))


def _scatter_maps(kinds):
    return (lambda srcs, a, idx: _slab(kinds[a], srcs[a], idx)), (lambda lands, a, idx: lands[a].at[idx])


def _place_own(srcs, maps, out_shapes, zero_init, *, name):
    n = len(srcs)
    src_of, dst_of = maps
    inits = [jnp.zeros(out_shapes[a].shape, out_shapes[a].dtype) for a in zero_init]

    def body(*refs):
        src_r, land_r, sem = refs[:n], refs[n + len(inits):n + len(inits) + n], refs[-1]
        me = _my_index()
        copies = [pltpu.make_async_copy(src_of(src_r, a, me), dst_of(land_r, a, me), sem.at[a]) for a in range(n)]
        for cp in copies:
            cp.start()
        for cp in copies:
            cp.wait()

    return _pcall(
        body, in_specs=[ANY] * (n + len(inits)), out_specs=[ANY] * n, out_shape=list(out_shapes),
        input_output_aliases={n + j: a for j, a in enumerate(zero_init)},
        scratch_shapes=[pltpu.SemaphoreType.DMA((n,))],
        compiler_params=pltpu.CompilerParams(has_side_effects=True), name=name)(*srcs, *inits)


def _peer_copies(n, src_of, dst_of, src_r, land_r, ssem, rsem):
    me = _my_index()
    out = []
    for r, peer in _relations():
        p_idx = 4 * peer[0] + 2 * peer[1] + peer[2]
        for a in range(n):
            def copy(src_idx, dst_idx, a=a, r=r, peer=peer):
                sem = a * (N_DEV - 1) + r - 1
                return pltpu.make_async_remote_copy(
                    src_ref=src_of(src_r, a, src_idx), dst_ref=dst_of(land_r, a, dst_idx),
                    send_sem=ssem.at[sem], recv_sem=rsem.at[sem], device_id=peer, device_id_type=MESH)
            out.append((functools.partial(copy, p_idx, me), functools.partial(copy, me, p_idx)))
    return out


def _exchange_start(srcs, lands, maps, after, *, name):
    n = len(srcs)
    src_of, dst_of = maps

    def body(*refs):
        src_r, land_r = refs[:n], refs[n:2 * n]
        ssem, rsem, token = refs[2 * n + 1], refs[2 * n + 2], refs[-1]
        for send, _ in _peer_copies(n, src_of, dst_of, src_r, land_r, ssem, rsem):
            send().start()
        token[...] = jnp.zeros_like(token)

    flight = list(srcs) + list(lands)
    outs = pl.pallas_call(
        body, name=name,
        out_shape=(pltpu.SemaphoreType.DMA((n * (N_DEV - 1),)), pltpu.SemaphoreType.DMA((n * (N_DEV - 1),)),
                   *[pltpu.HBM(a.shape, a.dtype) for a in flight], SDS((8, LANE), F32)),
        in_specs=[HBM_SPEC] * (2 * n) + [ANY],
        out_specs=(SEM_SPEC, SEM_SPEC, *[HBM_SPEC] * (2 * n), pl.BlockSpec(memory_space=pltpu.VMEM)),
        input_output_aliases={i: 2 + i for i in range(2 * n)},
        compiler_params=pltpu.CompilerParams(has_side_effects=_DATAFLOW),
    )(*[pltpu.with_memory_space_constraint(a, pltpu.HBM) for a in flight], after)
    return (outs[0], outs[1], list(outs[2:2 + 2 * n])), outs[-1]


def _exchange_wait(handle, maps, after, *, name):
    ssem, rsem, flight = handle
    n = len(flight) // 2
    src_of, dst_of = maps

    def body(*refs):
        src_r, land_r, ssem_r, rsem_r = refs[:n], refs[n:2 * n], refs[2 * n], refs[2 * n + 1]
        for send, arrival in _peer_copies(n, src_of, dst_of, src_r, land_r, ssem_r, rsem_r):
            send().wait_send()
            arrival().wait_recv()

    outs = pl.pallas_call(
        body, name=name, out_shape=[pltpu.HBM(a.shape, a.dtype) for a in flight],
        in_specs=[HBM_SPEC] * (2 * n) + [SEM_SPEC, SEM_SPEC, ANY], out_specs=[HBM_SPEC] * (2 * n),
        input_output_aliases={i: i for i in range(2 * n)},
        compiler_params=pltpu.CompilerParams(has_side_effects=_DATAFLOW),
    )(*flight, ssem, rsem, after)
    return list(outs[n:])


def _allreduce_small(pack, *, name):
    rows = pack.shape[0]

    def body(p_ref, out_ref, gath, ssem, rsem, lsem):
        _exchange(1, lambda a, idx: p_ref, lambda a, idx: gath.at[idx], (ssem, rsem, lsem))
        total = gath[0]
        for d in range(1, N_DEV):
            total = total + gath[d]
        out_ref[...] = total

    vm = pl.BlockSpec(memory_space=pltpu.VMEM)
    return _pcall(
        body, in_specs=[vm], out_specs=vm, out_shape=SDS(pack.shape, F32),
        scratch_shapes=[pltpu.VMEM((N_DEV, rows, pack.shape[1]), F32)] + _sem_scratch(1),
        compiler_params=_cp(has_side_effects=True), name=name)(pack)


def _adam(g, w, m, v):
    nm = ADAM_B1 * m + (1.0 - ADAM_B1) * g
    nv = ADAM_B2 * v + (1.0 - ADAM_B2) * (g * g)
    m_hat = nm / (1.0 - ADAM_B1 ** ADAM_STEP)
    v_hat = nv / (1.0 - ADAM_B2 ** ADAM_STEP)
    return -ADAM_LR * (m_hat / (jnp.sqrt(v_hat) + ADAM_EPS) + ADAM_WD * w), nm, nv


def _adamw_staged(st0, st1, w, m, v, *, name):
    _, rows, cols = w.shape
    tr = max(t for t in range(16, 129, 16) if rows % t == 0)
    nr = rows // tr

    def body(s0_ref, s1_ref, w_ref, m_ref, v_ref, g_ref, d_ref, nm_ref, nv_ref):
        for layer, s_ref in enumerate((s0_ref, s1_ref)):
            @pl.when(pl.program_id(0) == layer)
            def _(s_ref=s_ref):
                total = s_ref[0].astype(F32)
                for dev in range(1, N_DEV):
                    total = total + s_ref[dev].astype(F32)
                g_ref[0] = total

        d_ref[0], nm_ref[0], nv_ref[0] = _adam(g_ref[0], w_ref[0], m_ref[0], v_ref[0])

    st_spec = lambda layer: pl.BlockSpec(
        (N_DEV, tr, cols), lambda l, i: (0, jnp.where(l == layer, i, (nr - 1) * (1 - layer)), 0))
    par = pl.BlockSpec((1, tr, cols), lambda l, i: (l, i, 0))
    return _pcall(
        body, grid=(DEPTH, nr), in_specs=[st_spec(0), st_spec(1), par, par, par], out_specs=[par] * 4,
        out_shape=[SDS(w.shape, F32)] * 4,
        compiler_params=_cp(("arbitrary", "arbitrary")), name=name)(st0, st1, w, m, v)


def _adamw_small(g, w, m, v, *, name):
    def body(g_ref, w_ref, m_ref, v_ref, d_ref, nm_ref, nv_ref):
        d_ref[...], nm_ref[...], nv_ref[...] = _adam(g_ref[...], w_ref[...], m_ref[...], v_ref[...])

    return _pcall(body, out_shape=[SDS(g.shape, F32)] * 3, compiler_params=_cp(), name=name)(g, w, m, v)


def _pack_rows(arrays):
    flat = jnp.concatenate([a.reshape(-1).astype(F32) for a in arrays])
    rows = -(-flat.shape[0] // (8 * D)) * 8
    return jnp.pad(flat, (0, rows * D - flat.shape[0])).reshape(rows, D)


def _unpack_rows(pack, like):
    flat = pack.reshape(-1)
    out, at = [], 0
    for a in like:
        out.append(flat[at:at + a.size].reshape(a.shape))
        at += a.size
    return out


def _layer_fwd(x, h, mem, win, rest_of_weights, small, g_next, tag):
    proj = _mm(h, win, tn=1024, name=f"proj_{tag}")
    wup, wkv, w4, wdn, cw_a, cw_b, cw_f = rest_of_weights(proj)
    za = _bra_fwd(proj, cw_a, name=f"bra_fwd_{tag}")
    cb = _brb_conv_fwd(proj, cw_b, small["conv_b_bias"], name=f"brb_conv_fwd_{tag}")
    sb = _ln_silu_fwd(cb, small["ln_b_g"], small["ln_b_b"], name=f"ln_silu_fwd_{tag}")
    memn, kv = _kv_prep(mem, small["norm_mem_g"], wkv, name=f"kv_prep_{tag}")
    o = _attn_fwd(proj, kv, name=f"attn_fwd_{tag}")
    ya, yb, yc, mg, x1, h2 = _mix_out(x, za, sb, o, proj, w4, small["b_gate"], small["norm_ffn_g"],
                                      name=f"mix_out_{tag}")
    u2 = _mm(h2, wup, tn=768, name=f"up_{tag}")
    act = _ffn_act(u2, cw_f, name=f"ffn_act_{tag}")
    x2, h_next = _mm_res_norm(act, wdn, x1, g_next, name=f"down_{tag}")
    saved = dict(x=x, h=h, proj=proj, za=za, cb=cb, sb=sb, memn=memn, kv=kv, o=o, ya=ya, yb=yb, yc=yc,
                 mg=mg, x1=x1, h2=h2, u2=u2, act=act)
    return x2, h_next, (win, wup, wkv, w4, wdn, cw_a, cw_b, cw_f), saved


def _behind(operand, token):
    return operand + token[0:1, 0:1]


def _layer_bwd(dx2, dx2b, mem, wts, small, sv, start, tag):
    win, wup, wkv, w4, wdn, cw_a, cw_b, cw_f = wts
    dact = _mm(dx2b, wdn, tb=True, tn=768, name=f"d_act_{tag}")
    dwdn = _mm(sv["act"], dx2b, ta=True, tm=768, tn=1024, tk=512, name=f"dw_down_{tag}")
    du2, dcw_f = _ffn_bwd(sv["u2"], dact, cw_f, name=f"ffn_bwd_{tag}")
    dwup = _mm(sv["h2"], du2, ta=True, tm=1024, tn=768, tk=512, name=f"dw_up_{tag}")
    token = start(("wdn", "wup"), (dwdn, dwup), f"ffn_{tag}")
    dx1, dx1b, dg_ffn = _mm_nt_normbwd(du2, wup, sv["x1"], dx2, _behind(small["norm_ffn_g"], token), tk=1536,
                                       name=f"d_h2_{tag}")

    dya, dyb, dyc, dza, dsb, do, dgate, dbg = _mix_bwd(dx1b, sv["ya"], sv["yb"], sv["yc"], sv["proj"], w4,
                                                      small["b_gate"], name=f"mix_bwd_{tag}")
    dw4 = jnp.stack([
        _mm(a, b, ta=True, tm=1024, tn=1024, tk=512, name=f"dw_{nm}_{tag}")
        for nm, a, b in (("a_out", sv["za"], dya), ("b_out", sv["sb"], dyb), ("att_out", sv["o"], dyc),
                         ("o", sv["mg"], dx1b))])
    dq, dk, dv = _attn_bwd(sv["proj"], sv["kv"], do, name=f"attn_bwd_{tag}")
    dwkv, dg_mem = _kv_bwd(mem, small["norm_mem_g"], sv["memn"], dk, dv, wkv, name=f"kv_bwd_{tag}")
    token = start(("w4", "wkv"), (dw4, dwkv), f"mix_{tag}")
    d_a, dcw_a = _bra_bwd(sv["proj"], dza, _behind(cw_a, token), name=f"bra_bwd_{tag}")
    dcb, ln_sums = _ln_silu_bwd(sv["cb"], dsb, small["ln_b_g"], small["ln_b_b"], name=f"ln_silu_bwd_{tag}")
    d_b, dcw_b = _brb_conv_bwd(sv["proj"], dcb, cw_b, name=f"brb_conv_bwd_{tag}")
    dproj = jnp.concatenate([d_a, d_b, dq, dgate], axis=1)
    dwin = _mm(sv["h"], dproj, ta=True, tm=1024, tn=1024, tk=512, name=f"dw_in_{tag}")
    token = start(("win",), (dwin,), f"in_{tag}")
    dx, dxb, dg_mix = _mm_nt_normbwd(dproj, win, sv["x"], dx1, _behind(small["norm_mix_g"], token), tk=1536,
                                     name=f"d_h_{tag}")

    small_grads = [dg_mix[0:1], dg_mem[0:1], dbg[0:1].reshape(3, D), ln_sums[2:3], ln_sums[0:1], ln_sums[1:2],
                   dg_ffn[0:1], dcw_a[0:K_A], dcw_b[0:K_B], dcw_f[0:K_F].reshape(K_F * 2 * FF_P // D, D)]
    return dx, dxb, small_grads, token


_SMALL_ROWS = (1, 1, 3, 1, 1, 1, 1, K_A, K_B, K_F * 2 * FF_P // D)
_CV_ROWS = 48


def kernel(x, mem, norm_mix_g, norm_mem_g, w_in, b_gate, conv_a_w, w_a_out, conv_b_w, conv_b_bias, ln_b_g, ln_b_b, w_b_out, w_kv, w_att_out, w_o, norm_ffn_g, w_up, conv_ffn_w, w_down, norm_final_g, loss_target, m_norm_mix_g, m_norm_mem_g, m_w_in, m_b_gate, m_conv_a_w, m_w_a_out, m_conv_b_w, m_conv_b_bias, m_ln_b_g, m_ln_b_b, m_w_b_out, m_w_kv, m_w_att_out, m_w_o, m_norm_ffn_g, m_w_up, m_conv_ffn_w, m_w_down, m_norm_final_g, v_norm_mix_g, v_norm_mem_g, v_w_in, v_b_gate, v_conv_a_w, v_w_a_out, v_conv_b_w, v_conv_b_bias, v_ln_b_g, v_ln_b_b, v_w_b_out, v_w_kv, v_w_att_out, v_w_o, v_norm_ffn_g, v_w_up, v_conv_ffn_w, v_w_down, v_norm_final_g):
    me = _my_index()
    x0, mem0, tgt = x.reshape(x.shape[1:]), mem.reshape(mem.shape[1:]), loss_target.reshape(x.shape[1:])
    up_pad = ((0, 0), (0, 0), (0, C_UP_P - C_UP))

    ag_groups = (("win",), ("wup", "wkv", "w4", "wdn", "cv"))
    kinds = ag_groups[0] + ag_groups[1]
    smalls, ag_handles = [], []
    token = jnp.zeros((8, LANE), F32)
    for l in range(DEPTH):
        cv = jnp.zeros((_CV_ROWS, C_UP_P), F32)
        cv = cv.at[0:K_F, 0:C_UP].set(conv_ffn_w[l]).at[3:3 + K_A, 0:R_O].set(conv_a_w[l])
        cv = cv.at[8:8 + K_B, 0:R_O].set(conv_b_w[l])
        shards = dict(
            win=w_in[l].astype(BF), wup=jnp.pad(w_up[l], up_pad[1:]).astype(BF), wkv=w_kv[l].astype(BF),
            w4=jnp.stack([w_a_out[l], w_b_out[l], w_att_out[l], w_o[l]]).astype(BF), wdn=w_down[l].astype(BF), cv=cv)
        whole = [SDS(*_WHOLE[k]) for k in kinds[:-1]] + [SDS((N_DEV,) + cv.shape, F32)]
        lands = dict(zip(kinds, _place_own([shards[k] for k in kinds], _gather_maps(kinds), whole,
                                           (kinds.index("wdn"),), name=f"ag_own_l{l}")))
        per_layer = []
        for g, grp in enumerate(ag_groups):
            handle, token = _exchange_start([shards[k] for k in grp], [lands[k] for k in grp], _gather_maps(grp),
                                            token, name=f"ag_start_l{l}_g{g}")
            per_layer.append(handle)
        ag_handles.append(per_layer)
        smalls.append(dict(
            norm_mix_g=norm_mix_g[l][None], norm_mem_g=norm_mem_g[l][None], b_gate=b_gate[l][None],
            conv_b_bias=conv_b_bias[l][None], ln_b_g=ln_b_g[l][None], ln_b_b=ln_b_b[l][None],
            norm_ffn_g=norm_ffn_g[l][None]))

    def rest_of_weights(l):
        def wait(after):
            wup, wkv, w4, wdn, cvg = _exchange_wait(ag_handles[l][1], _gather_maps(ag_groups[1]), after,
                                                    name=f"ag_wait_l{l}_g1")
            cw_f = jnp.stack([cvg[d, 0:K_F, :] for d in UP_ORDER], axis=1).reshape(K_F, 2 * FF_P)
            cw_a = cvg[:, 3:3 + K_A, 0:R_O].transpose(1, 0, 2).reshape(K_A, D)
            cw_b = cvg[:, 8:8 + K_B, 0:R_O].transpose(1, 0, 2).reshape(K_B, D)
            return (wup, wkv, w4, wdn, jnp.pad(cw_a, ((0, 8 - K_A), (0, 0))),
                    jnp.pad(cw_b, ((0, 32 - K_B), (0, 0))), jnp.pad(cw_f, ((0, 8 - K_F), (0, 0))))
        return wait

    wts, saved = [], []
    xs = x0
    h = _rms_fwd(xs, smalls[0]["norm_mix_g"], name="rms_fwd")
    behind = token
    for l in range(DEPTH):
        g_next = smalls[l + 1]["norm_mix_g"] if l + 1 < DEPTH else norm_final_g[None]
        (win,) = _exchange_wait(ag_handles[l][0], _gather_maps(ag_groups[0]), behind, name=f"ag_wait_l{l}_g0")
        xs, h, w_l, sv = _layer_fwd(xs, h, mem0, win, rest_of_weights(l), smalls[l], g_next, f"l{l}")
        behind = h
        wts.append(w_l)
        saved.append(sv)
    dx, dxb, head_sums = _loss_head(xs, tgt, norm_final_g[None], name="loss_head")

    rs_handles = []
    small_grads = [None] * DEPTH

    def start_scatter(grp, arrays, name):
        maps = _scatter_maps(grp)
        lands = _place_own(list(arrays), maps, [SDS((N_DEV,) + _SHARD[k], BF) for k in grp], (),
                           name=f"rs_own_{name}")
        handle, tok = _exchange_start(list(arrays), lands, maps, rs_handles[-1][2] if rs_handles else head_sums,
                                      name=f"rs_start_{name}")
        rs_handles.append((grp, handle, tok, name))
        return tok

    for l in reversed(range(DEPTH)):
        dx, dxb, small_grads[l], token = _layer_bwd(dx, dxb, mem0, wts[l], smalls[l], saved[l], start_scatter,
                                                    f"l{l}")

    staged = [dict() for _ in range(DEPTH)]
    for grp, handle, _, name in rs_handles[:-1]:
        staged[int(name[-1])].update(zip(grp, _exchange_wait(handle, _scatter_maps(grp), dx, name=f"rs_wait_{name}")))

    pack = jnp.concatenate(small_grads[0] + small_grads[1] + [head_sums[1:2], head_sums[0:1]], axis=0)
    pack = jnp.pad(pack, ((0, -pack.shape[0] % 8), (0, 0)))
    total = _allreduce_small(pack, name="allreduce_small")
    per_layer = sum(_SMALL_ROWS)
    parts = []
    for l in range(DEPTH):
        at, one = l * per_layer, []
        for rows in _SMALL_ROWS:
            one.append(total[at:at + rows])
            at += rows
        parts.append(one)
    g_final = total[DEPTH * per_layer]
    loss = 0.5 / D * jnp.sum(total[DEPTH * per_layer + 1])

    def both(i):
        return jnp.stack([parts[0][i], parts[1][i]])

    g_norm_mix, g_norm_mem = both(0)[:, 0], both(1)[:, 0]
    g_b_gate = both(2).reshape(DEPTH, 3 * D)
    g_cbias, g_lng, g_lnb, g_norm_ffn = both(3)[:, 0], both(4)[:, 0], both(5)[:, 0], both(6)[:, 0]
    g_conv_a = lax.dynamic_slice_in_dim(both(7), me * R_O, R_O, axis=2)
    g_conv_b = lax.dynamic_slice_in_dim(both(8), me * R_O, R_O, axis=2)
    g_conv_f = lax.dynamic_slice_in_dim(both(9).reshape(DEPTH, K_F, 2 * FF_P), _up_slot(me) * C_UP_P, C_UP, axis=2)

    small_g = [g_norm_mix, g_norm_mem, g_b_gate, g_conv_a, g_conv_b, g_cbias, g_lng, g_lnb, g_norm_ffn, g_conv_f,
               g_final]
    small_w = [norm_mix_g, norm_mem_g, b_gate, conv_a_w, conv_b_w, conv_b_bias, ln_b_g, ln_b_b, norm_ffn_g,
               conv_ffn_w, norm_final_g]
    small_m = [m_norm_mix_g, m_norm_mem_g, m_b_gate, m_conv_a_w, m_conv_b_w, m_conv_b_bias, m_ln_b_g, m_ln_b_b,
               m_norm_ffn_g, m_conv_ffn_w, m_norm_final_g]
    small_v = [v_norm_mix_g, v_norm_mem_g, v_b_gate, v_conv_a_w, v_conv_b_w, v_conv_b_bias, v_ln_b_g, v_ln_b_b,
               v_norm_ffn_g, v_conv_ffn_w, v_norm_final_g]
    upd = _adamw_small(_pack_rows(small_g), _pack_rows(small_w), _pack_rows(small_m), _pack_rows(small_v),
                       name="adamw_small")
    s_d, s_m, s_v = (_unpack_rows(p, small_w) for p in upd)
    (d_norm_mix, d_norm_mem, d_b_gate, d_conv_a, d_conv_b, d_cbias, d_lng, d_lnb, d_norm_ffn, d_conv_f,
     d_final) = s_d
    (nm_norm_mix, nm_norm_mem, nm_b_gate, nm_conv_a, nm_conv_b, nm_cbias, nm_lng, nm_lnb, nm_norm_ffn, nm_conv_f,
     nm_final) = s_m
    (nv_norm_mix, nv_norm_mem, nv_b_gate, nv_conv_a, nv_conv_b, nv_cbias, nv_lng, nv_lnb, nv_norm_ffn, nv_conv_f,
     nv_final) = s_v

    def big_update(kind, w, m, v, name):
        return _adamw_staged(staged[0][kind], staged[1][kind], w, m, v, name=name)

    r_up = [a[:, :, 0:C_UP] for a in big_update("wup", jnp.pad(w_up, up_pad), jnp.pad(m_w_up, up_pad),
                                                jnp.pad(v_w_up, up_pad), "adamw_w_up")]
    r_kv = big_update("wkv", w_kv, m_w_kv, v_w_kv, "adamw_w_kv")
    r_dn = big_update("wdn", w_down, m_w_down, v_w_down, "adamw_w_down")

    def four(a, b, c, d_):
        return jnp.stack([a, b, c, d_], axis=1).reshape(DEPTH, 4 * R_O, D)

    r_4 = _adamw_staged(
        staged[0]["w4"].reshape(N_DEV, 4 * R_O, D), staged[1]["w4"].reshape(N_DEV, 4 * R_O, D),
        four(w_a_out, w_b_out, w_att_out, w_o), four(m_w_a_out, m_w_b_out, m_w_att_out, m_w_o),
        four(v_w_a_out, v_w_b_out, v_w_att_out, v_w_o), name="adamw_w_out")
    grp, handle, _, name = rs_handles[-1]
    staged[0].update(zip(grp, _exchange_wait(handle, _scatter_maps(grp), r_4[0], name=f"rs_wait_{name}")))
    r_in = big_update("win", w_in, m_w_in, v_w_in, "adamw_w_in")
    r_a, r_b, r_att, r_o = ([a.reshape(DEPTH, 4, R_O, D)[:, j] for a in r_4] for j in range(4))

    grads = [g_norm_mix, g_norm_mem, r_in[0], g_b_gate, g_conv_a, r_a[0], g_conv_b, g_cbias, g_lng, g_lnb, r_b[0],
             r_kv[0], r_att[0], r_o[0], g_norm_ffn, r_up[0], g_conv_f, r_dn[0], g_final]
    deltas = [d_norm_mix, d_norm_mem, r_in[1], d_b_gate, d_conv_a, r_a[1], d_conv_b, d_cbias, d_lng, d_lnb, r_b[1],
              r_kv[1], r_att[1], r_o[1], d_norm_ffn, r_up[1], d_conv_f, r_dn[1], d_final]
    new_m = [nm_norm_mix, nm_norm_mem, r_in[2], nm_b_gate, nm_conv_a, r_a[2], nm_conv_b, nm_cbias, nm_lng, nm_lnb,
             r_b[2], r_kv[2], r_att[2], r_o[2], nm_norm_ffn, r_up[2], nm_conv_f, r_dn[2], nm_final]
    new_v = [nv_norm_mix, nv_norm_mem, r_in[3], nv_b_gate, nv_conv_a, r_a[3], nv_conv_b, nv_cbias, nv_lng, nv_lnb,
             r_b[3], r_kv[3], r_att[3], r_o[3], nv_norm_ffn, r_up[3], nv_conv_f, r_dn[3], nv_final]
    return (loss, dx[None], *grads, *deltas, *new_m, *new_v)
```

```python
import functools

import jax
import jax.numpy as jnp
from jax import lax
from jax.experimental import pallas as pl
from jax.experimental.pallas import tpu as pltpu

F32 = jnp.float32
BF = jnp.bfloat16
SDS = jax.ShapeDtypeStruct
MESH = pl.DeviceIdType.MESH
ANY = pl.BlockSpec(memory_space=pl.ANY)

N_DEV = 8
DEPTH = 2
D = 1024
N_HEADS = 4
HEAD = D // N_HEADS
D_FF = 2816
K_A, K_B, K_F = 3, 31, 3
NORM_EPS = 1e-6

C_IN = 9 * D // N_DEV
C_KV = 2 * D // N_DEV
C_UP = 2 * D_FF // N_DEV
LANE = 128
C_UP_P = -(-C_UP // LANE) * LANE
FF_P = 4 * C_UP_P
R_O = D // N_DEV
R_DN = D_FF // N_DEV

VMEM_LIMIT = 56 * 1024 * 1024
TM = 512
TR = 256
H_S, H_L = 16, 32

ADAM_LR, ADAM_B1, ADAM_B2, ADAM_EPS, ADAM_WD, ADAM_STEP = 0.001, 0.9, 0.999, 1e-08, 0.01, 10

UP_ORDER = (0, 4, 1, 5, 2, 6, 3, 7)


def _pcall(body, **kw):
    return pl.pallas_call(body, **kw)


def _cp(sem=None, **kw):
    return pltpu.CompilerParams(dimension_semantics=sem, vmem_limit_bytes=VMEM_LIMIT, **kw)


def _dot(a, b):
    return jnp.dot(a, b, preferred_element_type=F32)


def _dot_nt(a, b):
    return lax.dot_general(a, b, (((1,), (1,)), ((), ())), preferred_element_type=F32)


def _dot_tn(a, b):
    return lax.dot_general(a, b, (((0,), (0,)), ((), ())), preferred_element_type=F32)


def _sigmoid(z):
    return 1.0 / (1.0 + jnp.exp(-z))


def _rms(xv):
    return lax.rsqrt(jnp.mean(xv * xv, axis=-1, keepdims=True) + NORM_EPS)


def _up_slot(idx):
    return jnp.where(idx < 4, 2 * idx, 2 * (idx - 4) + 1)


def _dn_row(idx):
    return C_UP_P * (idx // 2) + R_DN * (idx % 2)


def _mm(a, b, *, ta=False, tb=False, b_slabs=False, out_slabs=False, out_dtype=BF, tm=TM, tn=512, tk=None, name):
    m, k_dim = (a.shape[1], a.shape[0]) if ta else a.shape
    if b_slabs:
        tn = b.shape[2]
        n = b.shape[0] * tn
    else:
        n = b.shape[0] if tb else b.shape[1]
    tm, tn = min(tm, m), min(tn, n)
    tk = k_dim if tk is None else min(tk, k_dim)
    nk = k_dim // tk
    assert m % tm == 0 and n % tn == 0 and k_dim % tk == 0
    dims = (((0 if ta else 1,), (1 if tb else 0,)), ((), ()))

    def body(a_ref, b_ref, o_ref, *scratch):
        part = lax.dot_general(a_ref[...], b_ref[...], dims, preferred_element_type=F32)
        if nk == 1:
            o_ref[...] = part.astype(o_ref.dtype)
            return
        acc = scratch[0]
        k = pl.program_id(2)

        @pl.when(k == 0)
        def _():
            acc[...] = part

        @pl.when(k > 0)
        def _():
            acc[...] += part

        @pl.when(k == nk - 1)
        def _():
            o_ref[...] = acc[...].astype(o_ref.dtype)

    a_spec = pl.BlockSpec((tk, tm), lambda i, j, k: (k, i)) if ta else pl.BlockSpec((tm, tk), lambda i, j, k: (i, k))
    if b_slabs:
        b_spec = pl.BlockSpec((None, tk, tn), lambda i, j, k: (j, k, 0))
    elif tb:
        b_spec = pl.BlockSpec((tn, tk), lambda i, j, k: (j, k))
    else:
        b_spec = pl.BlockSpec((tk, tn), lambda i, j, k: (k, j))
    if out_slabs:
        out_spec, out_shape = pl.BlockSpec((None, tm, tn), lambda i, j, k: (j, i, 0)), SDS((n // tn, m, tn), out_dtype)
    else:
        out_spec, out_shape = pl.BlockSpec((tm, tn), lambda i, j, k: (i, j)), SDS((m, n), out_dtype)
    return _pcall(
        body, grid=(m // tm, n // tn, nk), in_specs=[a_spec, b_spec],
        out_specs=out_spec, out_shape=out_shape,
        scratch_shapes=[pltpu.VMEM((tm, tn), F32)] if nk > 1 else [],
        compiler_params=_cp(("parallel", "parallel", "arbitrary")), name=name)(a, b)


def _mm_res_norm(a, w, x, g, *, name):
    s, k_dim = a.shape
    tm = min(TM, s)

    def body(a_ref, w_ref, x_ref, g_ref, xo_ref, h_ref):
        xo = x_ref[...] + _dot(a_ref[...], w_ref[...])
        xo_ref[...] = xo
        h_ref[...] = ((xo * _rms(xo)) * g_ref[...]).astype(BF)

    return _pcall(
        body, grid=(s // tm,),
        in_specs=[pl.BlockSpec((tm, k_dim), lambda i: (i, 0)), pl.BlockSpec((k_dim, D), lambda i: (0, 0)),
                  pl.BlockSpec((tm, D), lambda i: (i, 0)), pl.BlockSpec((1, D), lambda i: (0, 0))],
        out_specs=[pl.BlockSpec((tm, D), lambda i: (i, 0))] * 2,
        out_shape=[SDS((s, D), F32), SDS((s, D), BF)],
        compiler_params=_cp(("parallel",)), name=name)(a, w, x, g)


def _mm_nt_normbwd(da, w, x, dres, g, *, name):
    s, k_dim = da.shape
    tm = min(TM, s)
    nk, _, tk = w.shape
    assert k_dim == nk * tk

    def body(da_ref, w_ref, x_ref, dres_ref, g_ref, dx_ref, dxb_ref, dg_ref, acc):
        i, k = pl.program_id(0), pl.program_id(1)
        part = _dot_nt(da_ref[...], w_ref[...])

        @pl.when(k == 0)
        def _():
            acc[...] = part

        @pl.when(k > 0)
        def _():
            acc[...] += part

        @pl.when((i == 0) & (k == 0))
        def _():
            dg_ref[...] = jnp.zeros_like(dg_ref)

        @pl.when(k == nk - 1)
        def _():
            dh = acc[...]
            xv = x_ref[...]
            r = _rms(xv)
            xn = xv * r
            dg_ref[0:1, :] += jnp.sum(dh * xn, axis=0, keepdims=True)
            dxn = dh * g_ref[...]
            dx = dres_ref[...] + r * (dxn - xn * jnp.mean(dxn * xn, axis=-1, keepdims=True))
            dx_ref[...] = dx
            dxb_ref[...] = dx.astype(BF)

    row = lambda i, k: (i, 0)
    return _pcall(
        body, grid=(s // tm, nk),
        in_specs=[pl.BlockSpec((tm, tk), lambda i, k: (i, k)), pl.BlockSpec((None, D, tk), lambda i, k: (k, 0, 0)),
                  pl.BlockSpec((tm, D), row), pl.BlockSpec((tm, D), row), pl.BlockSpec((1, D), lambda i, k: (0, 0))],
        out_specs=[pl.BlockSpec((tm, D), row), pl.BlockSpec((tm, D), row), pl.BlockSpec((8, D), lambda i, k: (0, 0))],
        out_shape=[SDS((s, D), F32), SDS((s, D), BF), SDS((8, D), F32)],
        scratch_shapes=[pltpu.VMEM((tm, D), F32)],
        compiler_params=_cp(("arbitrary", "arbitrary")), name=name)(da, w, x, dres, g)


def _rms_fwd(x, g, *, name):
    s = x.shape[0]
    tm = min(TM, s)

    def body(x_ref, g_ref, h_ref):
        xv = x_ref[...]
        h_ref[...] = ((xv * _rms(xv)) * g_ref[...]).astype(BF)

    return _pcall(
        body, grid=(s // tm,),
        in_specs=[pl.BlockSpec((tm, D), lambda i: (i, 0)), pl.BlockSpec((1, D), lambda i: (0, 0))],
        out_specs=pl.BlockSpec((tm, D), lambda i: (i, 0)), out_shape=SDS((s, D), BF),
        compiler_params=_cp(("parallel",)), name=name)(x, g)


def _loss_head(x, tgt, g, *, name):
    s = x.shape[0]
    tm = min(TM, s)

    def body(x_ref, t_ref, g_ref, dx_ref, dxb_ref, sums_ref):
        @pl.when(pl.program_id(0) == 0)
        def _():
            sums_ref[...] = jnp.zeros_like(sums_ref)

        xv = x_ref[...]
        r = _rms(xv)
        xn = xv * r
        diff = xn * g_ref[...] - t_ref[...]
        sums_ref[0:1, :] += jnp.sum(diff * diff, axis=0, keepdims=True)
        dy = diff * (1.0 / D)
        sums_ref[1:2, :] += jnp.sum(dy * xn, axis=0, keepdims=True)
        dxn = dy * g_ref[...]
        dx = r * (dxn - xn * jnp.mean(dxn * xn, axis=-1, keepdims=True))
        dx_ref[...] = dx
        dxb_ref[...] = dx.astype(BF)

    row = lambda i: (i, 0)
    return _pcall(
        body, grid=(s // tm,),
        in_specs=[pl.BlockSpec((tm, D), row), pl.BlockSpec((tm, D), row), pl.BlockSpec((1, D), lambda i: (0, 0))],
        out_specs=[pl.BlockSpec((tm, D), row), pl.BlockSpec((tm, D), row), pl.BlockSpec((8, D), lambda i: (0, 0))],
        out_shape=[SDS((s, D), F32), SDS((s, D), BF), SDS((8, D), F32)],
        compiler_params=_cp(("arbitrary",)), name=name)(x, tgt, g)


def _halo_before(i, tr, h):
    return jnp.maximum(i * (tr // h) - 1, 0)


def _halo_after(i, tr, h, s):
    return jnp.minimum((i + 1) * (tr // h), s // h - 1)


def _taps(buf, w_ref, sl, k_w, base, rows):
    acc = None
    for k in range(k_w):
        t = w_ref[k:k + 1, sl] * buf[base + k:base + k + rows, sl]
        acc = t if acc is None else acc + t
    return acc


def _taps_rev(buf, w_ref, sl, k_w, rows):
    acc = None
    for k in range(k_w):
        t = w_ref[k:k + 1, sl] * buf[k_w - 1 - k:k_w - 1 - k + rows, sl]
        acc = t if acc is None else acc + t
    return acc


def _tap_grads(dw_ref, dc, buf, sl, k_w, base, rows):
    for k in range(k_w):
        dw_ref[k:k + 1, sl] += jnp.sum(dc * buf[base + k:base + k + rows, sl], axis=0, keepdims=True)


def _bra_fwd(proj, cw, *, name):
    s = proj.shape[0]
    tr, h = min(TR, s), H_S

    def body(cur, halo, w_ref, za_ref, cvb):
        i = pl.program_id(0)
        hv = halo[:, D:2 * D].astype(F32) * halo[:, 2 * D:3 * D].astype(F32)
        cvb[0:h, :] = jnp.where(i == 0, 0.0, hv)
        cvb[h:h + tr, :] = cur[:, D:2 * D].astype(F32) * cur[:, 2 * D:3 * D].astype(F32)
        for c in range(D // LANE):
            sl = slice(LANE * c, LANE * c + LANE)
            ca = _taps(cvb, w_ref, sl, K_A, h - (K_A - 1), tr)
            za_ref[:, sl] = (cur[:, sl].astype(F32) * ca).astype(BF)

    return _pcall(
        body, grid=(s // tr,),
        in_specs=[pl.BlockSpec((tr, 3 * D), lambda i: (i, 0)),
                  pl.BlockSpec((h, 3 * D), lambda i: (_halo_before(i, tr, h), 0)),
                  pl.BlockSpec((8, D), lambda i: (0, 0))],
        out_specs=pl.BlockSpec((tr, D), lambda i: (i, 0)), out_shape=SDS((s, D), BF),
        scratch_shapes=[pltpu.VMEM((h + tr, D), F32)],
        compiler_params=_cp(("parallel",)), name=name)(proj, proj, cw)


def _bra_bwd(proj, dza, cw, *, name):
    s = proj.shape[0]
    tr, h = min(TR, s), H_S
    n = s // tr

    def body(before, cur, after, dz_cur, dz_after, w_ref, da_ref, dw_ref, cvb, dcab):
        i = pl.program_id(0)

        @pl.when(i == 0)
        def _():
            dw_ref[...] = jnp.zeros_like(dw_ref)

        first, last = i == 0, i == n - 1
        cvb[0:h, :] = jnp.where(first, 0.0, before[:, D:2 * D].astype(F32) * before[:, 2 * D:3 * D].astype(F32))
        cvb[h:h + tr, :] = cur[:, D:2 * D].astype(F32) * cur[:, 2 * D:3 * D].astype(F32)
        dcab[0:tr, :] = dz_cur[...].astype(F32) * cur[:, 0:D].astype(F32)
        dcab[tr:tr + h, :] = jnp.where(last, 0.0, dz_after[...].astype(F32) * after[:, 0:D].astype(F32))
        for c in range(D // LANE):
            sl = slice(LANE * c, LANE * c + LANE)
            gc = cur[:, D + LANE * c:D + LANE * c + LANE].astype(F32)
            v = cur[:, 2 * D + LANE * c:2 * D + LANE * c + LANE].astype(F32)
            ca = _taps(cvb, w_ref, sl, K_A, h - (K_A - 1), tr)
            da_ref[:, sl] = (dz_cur[:, sl].astype(F32) * ca).astype(BF)
            dcv = _taps_rev(dcab, w_ref, sl, K_A, tr)
            da_ref[:, D + LANE * c:D + LANE * c + LANE] = (dcv * v).astype(BF)
            da_ref[:, 2 * D + LANE * c:2 * D + LANE * c + LANE] = (dcv * gc).astype(BF)
            _tap_grads(dw_ref, dcab[0:tr, sl], cvb, sl, K_A, h - (K_A - 1), tr)

    return _pcall(
        body, grid=(n,),
        in_specs=[pl.BlockSpec((h, 3 * D), lambda i: (_halo_before(i, tr, h), 0)),
                  pl.BlockSpec((tr, 3 * D), lambda i: (i, 0)),
                  pl.BlockSpec((h, 3 * D), lambda i: (_halo_after(i, tr, h, s), 0)),
                  pl.BlockSpec((tr, D), lambda i: (i, 0)),
                  pl.BlockSpec((h, D), lambda i: (_halo_after(i, tr, h, s), 0)),
                  pl.BlockSpec((8, D), lambda i: (0, 0))],
        out_specs=[pl.BlockSpec((tr, 3 * D), lambda i: (i, 0)), pl.BlockSpec((8, D), lambda i: (0, 0))],
        out_shape=[SDS((s, 3 * D), BF), SDS((8, D), F32)],
        scratch_shapes=[pltpu.VMEM((h + tr, D), F32), pltpu.VMEM((tr + h, D), F32)],
        compiler_params=_cp(("arbitrary",)), name=name)(proj, proj, proj, dza, dza, cw)


_U_COL, _UG_COL = 3, 4


def _brb_conv_fwd(proj, cw, bias, *, name):
    s = proj.shape[0]
    tr, h = min(TR, s), H_L

    def body(u_cur, ug_cur, u_halo, ug_halo, w_ref, b_ref, cb_ref, glb):
        i = pl.program_id(0)
        glb[0:h, :] = jnp.where(i == 0, 0.0, u_halo[...].astype(F32) * _sigmoid(ug_halo[...].astype(F32)))
        glb[h:h + tr, :] = u_cur[...].astype(F32) * _sigmoid(ug_cur[...].astype(F32))
        for c in range(D // LANE):
            sl = slice(LANE * c, LANE * c + LANE)
            cb_ref[:, sl] = (_taps(glb, w_ref, sl, K_B, h - (K_B - 1), tr) + b_ref[:, sl]).astype(BF)

    return _pcall(
        body, grid=(s // tr,),
        in_specs=[pl.BlockSpec((tr, D), lambda i: (i, _U_COL)), pl.BlockSpec((tr, D), lambda i: (i, _UG_COL)),
                  pl.BlockSpec((h, D), lambda i: (_halo_before(i, tr, h), _U_COL)),
                  pl.BlockSpec((h, D), lambda i: (_halo_before(i, tr, h), _UG_COL)),
                  pl.BlockSpec((32, D), lambda i: (0, 0)), pl.BlockSpec((1, D), lambda i: (0, 0))],
        out_specs=pl.BlockSpec((tr, D), lambda i: (i, 0)), out_shape=SDS((s, D), BF),
        scratch_shapes=[pltpu.VMEM((h + tr, D), F32)],
        compiler_params=_cp(("parallel",)), name=name)(proj, proj, proj, proj, cw, bias)


def _brb_conv_bwd(proj, dcb, cw, *, name):
    s = proj.shape[0]
    tr, h = min(TR, s), H_L
    n = s // tr

    def body(u_before, ug_before, u_cur, ug_cur, d_cur, d_after, w_ref, db_ref, dw_ref, glb, dcbb):
        i = pl.program_id(0)

        @pl.when(i == 0)
        def _():
            dw_ref[...] = jnp.zeros_like(dw_ref)

        glb[0:h, :] = jnp.where(i == 0, 0.0, u_before[...].astype(F32) * _sigmoid(ug_before[...].astype(F32)))
        glb[h:h + tr, :] = u_cur[...].astype(F32) * _sigmoid(ug_cur[...].astype(F32))
        dcbb[0:tr, :] = d_cur[...].astype(F32)
        dcbb[tr:tr + h, :] = jnp.where(i == n - 1, 0.0, d_after[...].astype(F32))
        for c in range(D // LANE):
            sl = slice(LANE * c, LANE * c + LANE)
            dglu = _taps_rev(dcbb, w_ref, sl, K_B, tr)
            u = u_cur[:, sl].astype(F32)
            sg = _sigmoid(ug_cur[:, sl].astype(F32))
            db_ref[:, sl] = (dglu * sg).astype(BF)
            db_ref[:, D + LANE * c:D + LANE * c + LANE] = (dglu * u * sg * (1.0 - sg)).astype(BF)
            _tap_grads(dw_ref, dcbb[0:tr, sl], glb, sl, K_B, h - (K_B - 1), tr)

    return _pcall(
        body, grid=(n,),
        in_specs=[pl.BlockSpec((h, D), lambda i: (_halo_before(i, tr, h), _U_COL)),
                  pl.BlockSpec((h, D), lambda i: (_halo_before(i, tr, h), _UG_COL)),
                  pl.BlockSpec((tr, D), lambda i: (i, _U_COL)), pl.BlockSpec((tr, D), lambda i: (i, _UG_COL)),
                  pl.BlockSpec((tr, D), lambda i: (i, 0)),
                  pl.BlockSpec((h, D), lambda i: (_halo_after(i, tr, h, s), 0)),
                  pl.BlockSpec((32, D), lambda i: (0, 0))],
        out_specs=[pl.BlockSpec((tr, 2 * D), lambda i: (i, 0)), pl.BlockSpec((32, D), lambda i: (0, 0))],
        out_shape=[SDS((s, 2 * D), BF), SDS((32, D), F32)],
        scratch_shapes=[pltpu.VMEM((h + tr, D), F32), pltpu.VMEM((tr + h, D), F32)],
        compiler_params=_cp(("arbitrary",)), name=name)(proj, proj, proj, proj, dcb, dcb, cw)


def _ln_silu_fwd(cb, g, b, *, name):
    s = cb.shape[0]
    tm = min(TM, s)

    def body(cb_ref, g_ref, b_ref, sb_ref):
        z = cb_ref[...].astype(F32)
        zc = z - jnp.mean(z, axis=-1, keepdims=True)
        ln = (zc * lax.rsqrt(jnp.mean(zc * zc, axis=-1, keepdims=True) + NORM_EPS)) * g_ref[...] + b_ref[...]
        sb_ref[...] = (ln * _sigmoid(ln)).astype(BF)

    row = lambda i: (i, 0)
    vec = pl.BlockSpec((1, D), lambda i: (0, 0))
    return _pcall(
        body, grid=(s // tm,), in_specs=[pl.BlockSpec((tm, D), row), vec, vec],
        out_specs=pl.BlockSpec((tm, D), row), out_shape=SDS((s, D), BF),
        compiler_params=_cp(("parallel",)), name=name)(cb, g, b)


def _ln_silu_bwd(cb, dsb, g, b, *, name):
    s = cb.shape[0]
    tm = min(TM, s)

    def body(cb_ref, dsb_ref, g_ref, b_ref, dcb_ref, sums_ref):
        @pl.when(pl.program_id(0) == 0)
        def _():
            sums_ref[...] = jnp.zeros_like(sums_ref)

        z = cb_ref[...].astype(F32)
        zc = z - jnp.mean(z, axis=-1, keepdims=True)
        rstd = lax.rsqrt(jnp.mean(zc * zc, axis=-1, keepdims=True) + NORM_EPS)
        lnh = zc * rstd
        ln = lnh * g_ref[...] + b_ref[...]
        sg = _sigmoid(ln)
        dln = dsb_ref[...].astype(F32) * (sg * (1.0 + ln * (1.0 - sg)))
        sums_ref[0:1, :] += jnp.sum(dln * lnh, axis=0, keepdims=True)
        sums_ref[1:2, :] += jnp.sum(dln, axis=0, keepdims=True)
        dlnh = dln * g_ref[...]
        dz = rstd * (dlnh - jnp.mean(dlnh, axis=-1, keepdims=True)
                     - lnh * jnp.mean(dlnh * lnh, axis=-1, keepdims=True))
        sums_ref[2:3, :] += jnp.sum(dz, axis=0, keepdims=True)
        dcb_ref[...] = dz.astype(BF)

    row = lambda i: (i, 0)
    vec = pl.BlockSpec((1, D), lambda i: (0, 0))
    return _pcall(
        body, grid=(s // tm,), in_specs=[pl.BlockSpec((tm, D), row), pl.BlockSpec((tm, D), row), vec, vec],
        out_specs=[pl.BlockSpec((tm, D), row), pl.BlockSpec((8, D), lambda i: (0, 0))],
        out_shape=[SDS((s, D), BF), SDS((8, D), F32)],
        compiler_params=_cp(("arbitrary",)), name=name)(cb, dsb, g, b)


_Q_COL = 5 * D // HEAD


def _kv_prep(mem, g, wkv, *, name):
    m = mem.shape[0]

    def body(mem_ref, g_ref, w_ref, memn_ref, kv_ref):
        mv = mem_ref[...]
        memn = ((mv * _rms(mv)) * g_ref[...]).astype(BF)
        memn_ref[...] = memn
        for dev in range(N_DEV):
            kv_ref[:, dev * C_KV:(dev + 1) * C_KV] = _dot(memn, w_ref[dev]).astype(BF)

    return _pcall(body, out_shape=[SDS((m, D), BF), SDS((m, 2 * D), BF)],
                  compiler_params=_cp(), name=name)(mem, g, wkv)


def _softmax_rows(q, k):
    sc = _dot_nt(q, k) * (1.0 / (HEAD ** 0.5))
    e = jnp.exp(sc - jnp.max(sc, axis=-1, keepdims=True))
    return e / jnp.sum(e, axis=-1, keepdims=True)


def _attn_fwd(proj, kv, *, name):
    s, m = proj.shape[0], kv.shape[0]
    tm = min(TM, s)

    def body(q_ref, k_ref, v_ref, o_ref):
        p = _softmax_rows(q_ref[...], k_ref[...])
        o_ref[...] = _dot(p.astype(BF), v_ref[...]).astype(BF)

    return _pcall(
        body, grid=(s // tm, N_HEADS),
        in_specs=[pl.BlockSpec((tm, HEAD), lambda i, hd: (i, _Q_COL + hd)),
                  pl.BlockSpec((m, HEAD), lambda i, hd: (0, hd)),
                  pl.BlockSpec((m, HEAD), lambda i, hd: (0, N_HEADS + hd))],
        out_specs=pl.BlockSpec((tm, HEAD), lambda i, hd: (i, hd)), out_shape=SDS((s, D), BF),
        compiler_params=_cp(("parallel", "parallel")), name=name)(proj, kv, kv)


def _attn_bwd(proj, kv, do, *, name):
    s, m = proj.shape[0], kv.shape[0]
    tm = min(TM, s)

    def body(q_ref, k_ref, v_ref, do_ref, dq_ref, dk_ref, dv_ref):
        @pl.when(pl.program_id(1) == 0)
        def _():
            dk_ref[...] = jnp.zeros_like(dk_ref)
            dv_ref[...] = jnp.zeros_like(dv_ref)

        q, k, dov = q_ref[...], k_ref[...], do_ref[...]
        p = _softmax_rows(q, k)
        dp = _dot_nt(dov, v_ref[...])
        dv_ref[...] += _dot_tn(p.astype(BF), dov)
        ds = (p * (dp - jnp.sum(dp * p, axis=-1, keepdims=True)) * (1.0 / (HEAD ** 0.5))).astype(BF)
        dq_ref[...] = _dot(ds, k).astype(BF)
        dk_ref[...] += _dot_tn(ds, q)

    return _pcall(
        body, grid=(N_HEADS, s // tm),
        in_specs=[pl.BlockSpec((tm, HEAD), lambda hd, i: (i, _Q_COL + hd)),
                  pl.BlockSpec((m, HEAD), lambda hd, i: (0, hd)),
                  pl.BlockSpec((m, HEAD), lambda hd, i: (0, N_HEADS + hd)),
                  pl.BlockSpec((tm, HEAD), lambda hd, i: (i, hd))],
        out_specs=[pl.BlockSpec((tm, HEAD), lambda hd, i: (i, hd)),
                   pl.BlockSpec((m, HEAD), lambda hd, i: (0, hd)),
                   pl.BlockSpec((m, HEAD), lambda hd, i: (0, hd))],
        out_shape=[SDS((s, D), BF), SDS((m, D), F32), SDS((m, D), F32)],
        compiler_params=_cp(("parallel", "arbitrary")), name=name)(proj, kv, kv, do)


def _kv_bwd(mem, g, memn, dk, dv, wkv, *, name):
    def body(mem_ref, g_ref, memn_ref, dk_ref, dv_ref, w_ref, dw_ref, dg_ref):
        memn = memn_ref[...]
        dmemn = None
        for dev in range(N_DEV):
            d_ref, col = (dk_ref, dev) if dev < N_HEADS else (dv_ref, dev - N_HEADS)
            dslab = d_ref[:, col * C_KV:(col + 1) * C_KV].astype(BF)
            dw_ref[dev] = _dot_tn(memn, dslab).astype(BF)
            part = _dot_nt(dslab, w_ref[dev])
            dmemn = part if dmemn is None else dmemn + part
        mv = mem_ref[...]
        dg_ref[...] = jnp.zeros_like(dg_ref)
        dg_ref[0:1, :] = jnp.sum(dmemn * (mv * _rms(mv)), axis=0, keepdims=True)

    assert C_KV == HEAD
    return _pcall(body, out_shape=[SDS((N_DEV, D, C_KV), BF), SDS((8, D), F32)],
                  compiler_params=_cp(), name=name)(mem, g, memn, dk, dv, wkv)


_TM_MIX = 256


def _mix_out(x, za, sb, o, proj, w4, bg, g_next, *, name):
    s = x.shape[0]
    tm = min(_TM_MIX, s)

    def body(x_ref, za_ref, sb_ref, o_ref, pg_ref, w4_ref, bg_ref, gn_ref,
             ya_ref, yb_ref, yc_ref, mg_ref, x1_ref, h_ref):
        ys = (_dot(za_ref[...], w4_ref[0]), _dot(sb_ref[...], w4_ref[1]), _dot(o_ref[...], w4_ref[2]))
        merged = None
        for j, (y, y_ref) in enumerate(zip(ys, (ya_ref, yb_ref, yc_ref))):
            y_ref[...] = y.astype(BF)
            gate = _sigmoid(pg_ref[:, j * D:(j + 1) * D].astype(F32) + bg_ref[:, j * D:(j + 1) * D])
            merged = gate * y if merged is None else merged + gate * y
        mg = merged.astype(BF)
        mg_ref[...] = mg
        x1 = x_ref[...] + _dot(mg, w4_ref[3])
        x1_ref[...] = x1
        h_ref[...] = ((x1 * _rms(x1)) * gn_ref[...]).astype(BF)

    row = lambda i: (i, 0)
    act = pl.BlockSpec((tm, D), row)
    return _pcall(
        body, grid=(s // tm,),
        in_specs=[act, act, act, act, pl.BlockSpec((tm, 3 * D), lambda i: (i, 2)),
                  pl.BlockSpec((4, D, D), lambda i: (0, 0, 0)), pl.BlockSpec((1, 3 * D), lambda i: (0, 0)),
                  pl.BlockSpec((1, D), lambda i: (0, 0))],
        out_specs=[act] * 6,
        out_shape=[SDS((s, D), BF)] * 4 + [SDS((s, D), F32), SDS((s, D), BF)],
        compiler_params=_cp(("parallel",)), name=name)(x, za, sb, o, proj, w4, bg, g_next)


def _mix_bwd(dxb, ya, yb, yc, proj, w4, bg, *, name):
    s = dxb.shape[0]
    tm = min(_TM_MIX, s)

    def body(dx_ref, ya_ref, yb_ref, yc_ref, pg_ref, w4_ref, bg_ref,
             dya_ref, dyb_ref, dyc_ref, dza_ref, dsb_ref, do_ref, dgt_ref, dbg_ref):
        @pl.when(pl.program_id(0) == 0)
        def _():
            dbg_ref[...] = jnp.zeros_like(dbg_ref)

        dm = _dot_nt(dx_ref[...], w4_ref[3])
        for j, (y_ref, dy_ref, din_ref) in enumerate(zip((ya_ref, yb_ref, yc_ref), (dya_ref, dyb_ref, dyc_ref),
                                                         (dza_ref, dsb_ref, do_ref))):
            cols = slice(j * D, (j + 1) * D)
            gate = _sigmoid(pg_ref[:, cols].astype(F32) + bg_ref[:, cols])
            dy = (dm * gate).astype(BF)
            dy_ref[...] = dy
            din_ref[...] = _dot_nt(dy, w4_ref[j]).astype(BF)
            dpre = dm * y_ref[...].astype(F32) * gate * (1.0 - gate)
            dgt_ref[:, cols] = dpre.astype(BF)
            dbg_ref[0:1, cols] += jnp.sum(dpre, axis=0, keepdims=True)

    row = lambda i: (i, 0)
    act = pl.BlockSpec((tm, D), row)
    return _pcall(
        body, grid=(s // tm,),
        in_specs=[act, act, act, act, pl.BlockSpec((tm, 3 * D), lambda i: (i, 2)),
                  pl.BlockSpec((4, D, D), lambda i: (0, 0, 0)), pl.BlockSpec((1, 3 * D), lambda i: (0, 0))],
        out_specs=[act] * 6 + [pl.BlockSpec((tm, 3 * D), row), pl.BlockSpec((8, 3 * D), lambda i: (0, 0))],
        out_shape=[SDS((s, D), BF)] * 6 + [SDS((s, 3 * D), BF), SDS((8, 3 * D), F32)],
        compiler_params=_cp(("arbitrary",)), name=name)(dxb, ya, yb, yc, proj, w4, bg)


_PAIR = 2 * C_UP_P


def _ffn_act(u2, cw, *, name):
    s = u2.shape[0]
    tr, h = min(TR, s), H_S

    def body(cur, halo, w_ref, act_ref, ub):
        i = pl.program_id(1)
        ub[0:h, :] = jnp.where(i == 0, 0.0, halo[...].astype(F32))
        ub[h:h + tr, :] = cur[...].astype(F32)
        for c in range(C_UP_P // LANE):
            gl = slice(LANE * c, LANE * c + LANE)
            ul = slice(C_UP_P + LANE * c, C_UP_P + LANE * c + LANE)
            gt = _taps(ub, w_ref, gl, K_F, h - (K_F - 1), tr)
            up = _taps(ub, w_ref, ul, K_F, h - (K_F - 1), tr)
            act_ref[:, gl] = (gt * _sigmoid(gt) * up).astype(BF)

    return _pcall(
        body, grid=(4, s // tr),
        in_specs=[pl.BlockSpec((tr, _PAIR), lambda p, i: (i, p)),
                  pl.BlockSpec((h, _PAIR), lambda p, i: (_halo_before(i, tr, h), p)),
                  pl.BlockSpec((8, _PAIR), lambda p, i: (0, p))],
        out_specs=pl.BlockSpec((tr, C_UP_P), lambda p, i: (i, p)), out_shape=SDS((s, FF_P), BF),
        scratch_shapes=[pltpu.VMEM((h + tr, _PAIR), F32)],
        compiler_params=_cp(("parallel", "parallel")), name=name)(u2, u2, cw)


def _ffn_bwd(u2, dact, cw, *, name):
    s = u2.shape[0]
    tr, h = min(TR, s), H_S
    n = s // tr
    ext = tr + h

    def body(before, cur, after, da_cur, da_after, w_ref, du_ref, dw_ref, ub, dcb):
        i = pl.program_id(1)

        @pl.when(i == 0)
        def _():
            dw_ref[...] = jnp.zeros_like(dw_ref)

        ub[0:h, :] = jnp.where(i == 0, 0.0, before[...].astype(F32))
        ub[h:h + tr, :] = cur[...].astype(F32)
        ub[h + tr:h + tr + h, :] = jnp.where(i == n - 1, 0.0, after[...].astype(F32))
        for c in range(C_UP_P // LANE):
            gl = slice(LANE * c, LANE * c + LANE)
            ul = slice(C_UP_P + LANE * c, C_UP_P + LANE * c + LANE)
            gt = _taps(ub, w_ref, gl, K_F, h - (K_F - 1), ext)
            up = _taps(ub, w_ref, ul, K_F, h - (K_F - 1), ext)
            da = jnp.concatenate([da_cur[:, gl].astype(F32),
                                  jnp.where(i == n - 1, 0.0, da_after[:, gl].astype(F32))], axis=0)
            sg = _sigmoid(gt)
            dcb[:, gl] = da * up * (sg * (1.0 + gt * (1.0 - sg)))
            dcb[:, ul] = da * (gt * sg)
        for c in range(_PAIR // LANE):
            sl = slice(LANE * c, LANE * c + LANE)
            du_ref[:, sl] = _taps_rev(dcb, w_ref, sl, K_F, tr).astype(BF)
            _tap_grads(dw_ref, dcb[0:tr, sl], ub, sl, K_F, h - (K_F - 1), tr)

    return _pcall(
        body, grid=(4, n),
        in_specs=[pl.BlockSpec((h, _PAIR), lambda p, i: (_halo_before(i, tr, h), p)),
                  pl.BlockSpec((tr, _PAIR), lambda p, i: (i, p)),
                  pl.BlockSpec((h, _PAIR), lambda p, i: (_halo_after(i, tr, h, s), p)),
                  pl.BlockSpec((tr, C_UP_P), lambda p, i: (i, p)),
                  pl.BlockSpec((h, C_UP_P), lambda p, i: (_halo_after(i, tr, h, s), p)),
                  pl.BlockSpec((8, _PAIR), lambda p, i: (0, p))],
        out_specs=[pl.BlockSpec((tr, _PAIR), lambda p, i: (i, p)), pl.BlockSpec((8, _PAIR), lambda p, i: (0, p))],
        out_shape=[SDS((s, 2 * FF_P), BF), SDS((8, 2 * FF_P), F32)],
        scratch_shapes=[pltpu.VMEM((h + tr + h, _PAIR), F32), pltpu.VMEM((ext, _PAIR), F32)],
        compiler_params=_cp(("parallel", "arbitrary")), name=name)(u2, u2, u2, dact, dact, cw)


def _relations():
    x, y, c = lax.axis_index("x"), lax.axis_index("y"), lax.axis_index("c")
    out = []
    for r in range(1, N_DEV):
        rx, ry, rc = (r >> 2) & 1, (r >> 1) & 1, r & 1
        out.append((r, (x ^ rx, y ^ ry, c ^ rc)))
    return out


def _my_index():
    return 4 * lax.axis_index("x") + 2 * lax.axis_index("y") + lax.axis_index("c")


def _exchange(n_arrays, src_of, dst_of, refs):
    ssem, rsem, lsem = refs
    me = _my_index()
    local = []
    for a in range(n_arrays):
        loc = pltpu.make_async_copy(src_of(a, me), dst_of(a, me), lsem.at[a])
        loc.start()
        local.append(loc)

    def copy(a, r, peer, src_idx, dst_idx):
        return pltpu.make_async_remote_copy(
            src_ref=src_of(a, src_idx), dst_ref=dst_of(a, dst_idx), send_sem=ssem.at[a, r - 1],
            recv_sem=rsem.at[a, r - 1], device_id=peer, device_id_type=MESH)

    peers = [(r, peer, 4 * peer[0] + 2 * peer[1] + peer[2]) for r, peer in _relations()]
    for r, peer, p_idx in peers:
        for a in range(n_arrays):
            copy(a, r, peer, p_idx, me).start()
    for r, peer, p_idx in peers:
        for a in range(n_arrays):
            copy(a, r, peer, p_idx, me).wait_send()
            copy(a, r, peer, me, p_idx).wait_recv()
    for loc in local:
        loc.wait()


def _sem_scratch(n_arrays):
    return [pltpu.SemaphoreType.DMA((n_arrays, N_DEV - 1)), pltpu.SemaphoreType.DMA((n_arrays, N_DEV - 1)),
            pltpu.SemaphoreType.DMA((n_arrays,))]


def _slab(kind, ref, idx):
    if kind in ("win", "wkv"):
        return ref.at[idx]
    if kind == "wup":
        return ref.at[_up_slot(idx)]
    if kind == "w4":
        return ref.at[:, pl.ds(pl.multiple_of(idx * R_O, 16), R_O), :]
    if kind == "wdn":
        return ref.at[pl.ds(pl.multiple_of(_dn_row(idx), 16), R_DN), :]
    assert kind == "cv"
    return ref.at[idx]


_WHOLE = {"win": ((N_DEV, D, C_IN), BF), "wup": ((N_DEV, D, C_UP_P), BF), "wkv": ((N_DEV, D, C_KV), BF),
          "w4": ((4, D, D), BF), "wdn": ((FF_P, D), BF)}
_SHARD = {"win": (D, C_IN), "wup": (D, C_UP_P), "wkv": (D, C_KV), "w4": (4, R_O, D), "wdn": (R_DN, D)}
HBM_SPEC = pl.BlockSpec(memory_space=pltpu.HBM)
SEM_SPEC = pl.BlockSpec(memory_space=pltpu.SEMAPHORE)
_DATAFLOW = pltpu.SideEffectType.DATAFLOW_SIDE_EFFECTING


def _gather_maps(kinds):
    return (lambda srcs, a, idx: srcs[a]), (lambda lands, a, idx: _slab(kinds[a], lands[a], idx))


def _scatter_maps(kinds):
    return (lambda srcs, a, idx: _slab(kinds[a], srcs[a], idx)), (lambda lands, a, idx: lands[a].at[idx])


def _place_own(srcs, maps, out_shapes, zero_init, *, name):
    n = len(srcs)
    src_of, dst_of = maps
    inits = [jnp.zeros(out_shapes[a].shape, out_shapes[a].dtype) for a in zero_init]

    def body(*refs):
        src_r, land_r, sem = refs[:n], refs[n + len(inits):n + len(inits) + n], refs[-1]
        me = _my_index()
        copies = [pltpu.make_async_copy(src_of(src_r, a, me), dst_of(land_r, a, me), sem.at[a]) for a in range(n)]
        for cp in copies:
            cp.start()
        for cp in copies:
            cp.wait()

    return _pcall(
        body, in_specs=[ANY] * (n + len(inits)), out_specs=[ANY] * n, out_shape=list(out_shapes),
        input_output_aliases={n + j: a for j, a in enumerate(zero_init)},
        scratch_shapes=[pltpu.SemaphoreType.DMA((n,))],
        compiler_params=pltpu.CompilerParams(has_side_effects=True), name=name)(*srcs, *inits)


def _peer_copies(n, src_of, dst_of, src_r, land_r, ssem, rsem):
    me = _my_index()
    out = []
    for r, peer in _relations():
        p_idx = 4 * peer[0] + 2 * peer[1] + peer[2]
        for a in range(n):
            def copy(src_idx, dst_idx, a=a, r=r, peer=peer):
                sem = a * (N_DEV - 1) + r - 1
                return pltpu.make_async_remote_copy(
                    src_ref=src_of(src_r, a, src_idx), dst_ref=dst_of(land_r, a, dst_idx),
                    send_sem=ssem.at[sem], recv_sem=rsem.at[sem], device_id=peer, device_id_type=MESH)
            out.append((functools.partial(copy, p_idx, me), functools.partial(copy, me, p_idx)))
    return out


def _exchange_start(srcs, lands, maps, after, *, name):
    n = len(srcs)
    src_of, dst_of = maps

    def body(*refs):
        src_r, land_r = refs[:n], refs[n:2 * n]
        ssem, rsem, token = refs[2 * n + 1], refs[2 * n + 2], refs[-1]
        for send, _ in _peer_copies(n, src_of, dst_of, src_r, land_r, ssem, rsem):
            send().start()
        token[...] = jnp.zeros_like(token)

    flight = list(srcs) + list(lands)
    outs = pl.pallas_call(
        body, name=name,
        out_shape=(pltpu.SemaphoreType.DMA((n * (N_DEV - 1),)), pltpu.SemaphoreType.DMA((n * (N_DEV - 1),)),
                   *[pltpu.HBM(a.shape, a.dtype) for a in flight], SDS((8, LANE), F32)),
        in_specs=[HBM_SPEC] * (2 * n) + [ANY],
        out_specs=(SEM_SPEC, SEM_SPEC, *[HBM_SPEC] * (2 * n), pl.BlockSpec(memory_space=pltpu.VMEM)),
        input_output_aliases={i: 2 + i for i in range(2 * n)},
        compiler_params=pltpu.CompilerParams(has_side_effects=_DATAFLOW),
    )(*[pltpu.with_memory_space_constraint(a, pltpu.HBM) for a in flight], after)
    return (outs[0], outs[1], list(outs[2:2 + 2 * n])), outs[-1]


def _exchange_wait(handle, maps, after, *, name):
    ssem, rsem, flight = handle
    n = len(flight) // 2
    src_of, dst_of = maps

    def body(*refs):
        src_r, land_r, ssem_r, rsem_r = refs[:n], refs[n:2 * n], refs[2 * n], refs[2 * n + 1]
        for send, arrival in _peer_copies(n, src_of, dst_of, src_r, land_r, ssem_r, rsem_r):
            send().wait_send()
            arrival().wait_recv()

    outs = pl.pallas_call(
        body, name=name, out_shape=[pltpu.HBM(a.shape, a.dtype) for a in flight],
        in_specs=[HBM_SPEC] * (2 * n) + [SEM_SPEC, SEM_SPEC, ANY], out_specs=[HBM_SPEC] * (2 * n),
        input_output_aliases={i: i for i in range(2 * n)},
        compiler_params=pltpu.CompilerParams(has_side_effects=_DATAFLOW),
    )(*flight, ssem, rsem, after)
    return list(outs[n:])


def _allreduce_small(pack, *, name):
    rows = pack.shape[0]

    def body(p_ref, out_ref, gath, ssem, rsem, lsem):
        _exchange(1, lambda a, idx: p_ref, lambda a, idx: gath.at[idx], (ssem, rsem, lsem))
        total = gath[0]
        for d in range(1, N_DEV):
            total = total + gath[d]
        out_ref[...] = total

    vm = pl.BlockSpec(memory_space=pltpu.VMEM)
    return _pcall(
        body, in_specs=[vm], out_specs=vm, out_shape=SDS(pack.shape, F32),
        scratch_shapes=[pltpu.VMEM((N_DEV, rows, pack.shape[1]), F32)] + _sem_scratch(1),
        compiler_params=_cp(has_side_effects=True), name=name)(pack)


def _adam(g, w, m, v):
    nm = ADAM_B1 * m + (1.0 - ADAM_B1) * g
    nv = ADAM_B2 * v + (1.0 - ADAM_B2) * (g * g)
    m_hat = nm / (1.0 - ADAM_B1 ** ADAM_STEP)
    v_hat = nv / (1.0 - ADAM_B2 ** ADAM_STEP)
    return -ADAM_LR * (m_hat / (jnp.sqrt(v_hat) + ADAM_EPS) + ADAM_WD * w), nm, nv


def _adamw_staged(st0, st1, w, m, v, *, name):
    _, rows, cols = w.shape
    tr = max(t for t in range(16, 129, 16) if rows % t == 0)
    nr = rows // tr

    def body(s0_ref, s1_ref, w_ref, m_ref, v_ref, g_ref, d_ref, nm_ref, nv_ref):
        for layer, s_ref in enumerate((s0_ref, s1_ref)):
            @pl.when(pl.program_id(0) == layer)
            def _(s_ref=s_ref):
                total = s_ref[0].astype(F32)
                for dev in range(1, N_DEV):
                    total = total + s_ref[dev].astype(F32)
                g_ref[0] = total

        d_ref[0], nm_ref[0], nv_ref[0] = _adam(g_ref[0], w_ref[0], m_ref[0], v_ref[0])

    st_spec = lambda layer: pl.BlockSpec(
        (N_DEV, tr, cols), lambda l, i: (0, jnp.where(l == layer, i, (nr - 1) * (1 - layer)), 0))
    par = pl.BlockSpec((1, tr, cols), lambda l, i: (l, i, 0))
    return _pcall(
        body, grid=(DEPTH, nr), in_specs=[st_spec(0), st_spec(1), par, par, par], out_specs=[par] * 4,
        out_shape=[SDS(w.shape, F32)] * 4,
        compiler_params=_cp(("arbitrary", "arbitrary")), name=name)(st0, st1, w, m, v)


def _adamw_small(g, w, m, v, *, name):
    def body(g_ref, w_ref, m_ref, v_ref, d_ref, nm_ref, nv_ref):
        d_ref[...], nm_ref[...], nv_ref[...] = _adam(g_ref[...], w_ref[...], m_ref[...], v_ref[...])

    return _pcall(body, out_shape=[SDS(g.shape, F32)] * 3, compiler_params=_cp(), name=name)(g, w, m, v)


def _pack_rows(arrays):
    flat = jnp.concatenate([a.reshape(-1).astype(F32) for a in arrays])
    rows = -(-flat.shape[0] // (8 * D)) * 8
    return jnp.pad(flat, (0, rows * D - flat.shape[0])).reshape(rows, D)


def _unpack_rows(pack, like):
    flat = pack.reshape(-1)
    out, at = [], 0
    for a in like:
        out.append(flat[at:at + a.size].reshape(a.shape))
        at += a.size
    return out


def _layer_fwd(x, h, mem, win, rest_of_weights, small, g_next, tag):
    proj = _mm(h, win, b_slabs=True, name=f"proj_{tag}")
    wup, wkv, w4, wdn, cw_a, cw_b, cw_f = rest_of_weights(proj)
    za = _bra_fwd(proj, cw_a, name=f"bra_fwd_{tag}")
    cb = _brb_conv_fwd(proj, cw_b, small["conv_b_bias"], name=f"brb_conv_fwd_{tag}")
    sb = _ln_silu_fwd(cb, small["ln_b_g"], small["ln_b_b"], name=f"ln_silu_fwd_{tag}")
    memn, kv = _kv_prep(mem, small["norm_mem_g"], wkv, name=f"kv_prep_{tag}")
    o = _attn_fwd(proj, kv, name=f"attn_fwd_{tag}")
    ya, yb, yc, mg, x1, h2 = _mix_out(x, za, sb, o, proj, w4, small["b_gate"], small["norm_ffn_g"],
                                      name=f"mix_out_{tag}")
    u2 = _mm(h2, wup, b_slabs=True, name=f"up_{tag}")
    act = _ffn_act(u2, cw_f, name=f"ffn_act_{tag}")
    x2, h_next = _mm_res_norm(act, wdn, x1, g_next, name=f"down_{tag}")
    saved = dict(x=x, h=h, proj=proj, za=za, cb=cb, sb=sb, memn=memn, kv=kv, o=o, ya=ya, yb=yb, yc=yc,
                 mg=mg, x1=x1, h2=h2, u2=u2, act=act)
    return x2, h_next, (win, wup, wkv, w4, wdn, cw_a, cw_b, cw_f), saved


def _behind(operand, token):
    return operand + token[0:1, 0:1]


def _layer_bwd(dx2, dx2b, mem, wts, small, sv, start, tag):
    win, wup, wkv, w4, wdn, cw_a, cw_b, cw_f = wts
    dact = _mm(dx2b, wdn, tb=True, tn=768, name=f"d_act_{tag}")
    dwdn = _mm(sv["act"], dx2b, ta=True, tm=768, tn=1024, tk=512, name=f"dw_down_{tag}")
    du2, dcw_f = _ffn_bwd(sv["u2"], dact, cw_f, name=f"ffn_bwd_{tag}")
    dwup = _mm(sv["h2"], du2, ta=True, out_slabs=True, tm=1024, tn=C_UP_P, tk=512, name=f"dw_up_{tag}")
    token = start(("wdn", "wup"), (dwdn, dwup), f"ffn_{tag}")
    dx1, dx1b, dg_ffn = _mm_nt_normbwd(du2, wup, sv["x1"], dx2, _behind(small["norm_ffn_g"], token),
                                       name=f"d_h2_{tag}")

    dya, dyb, dyc, dza, dsb, do, dgate, dbg = _mix_bwd(dx1b, sv["ya"], sv["yb"], sv["yc"], sv["proj"], w4,
                                                      small["b_gate"], name=f"mix_bwd_{tag}")
    dw4 = jnp.stack([
        _mm(a, b, ta=True, tm=1024, tn=1024, tk=512, name=f"dw_{nm}_{tag}")
        for nm, a, b in (("a_out", sv["za"], dya), ("b_out", sv["sb"], dyb), ("att_out", sv["o"], dyc),
                         ("o", sv["mg"], dx1b))])
    dq, dk, dv = _attn_bwd(sv["proj"], sv["kv"], do, name=f"attn_bwd_{tag}")
    dwkv, dg_mem = _kv_bwd(mem, small["norm_mem_g"], sv["memn"], dk, dv, wkv, name=f"kv_bwd_{tag}")
    token = start(("w4", "wkv"), (dw4, dwkv), f"mix_{tag}")
    d_a, dcw_a = _bra_bwd(sv["proj"], dza, _behind(cw_a, token), name=f"bra_bwd_{tag}")
    dcb, ln_sums = _ln_silu_bwd(sv["cb"], dsb, small["ln_b_g"], small["ln_b_b"], name=f"ln_silu_bwd_{tag}")
    d_b, dcw_b = _brb_conv_bwd(sv["proj"], dcb, cw_b, name=f"brb_conv_bwd_{tag}")
    dproj = jnp.concatenate([d_a, d_b, dq, dgate], axis=1)
    dwin = _mm(sv["h"], dproj, ta=True, out_slabs=True, tm=1024, tn=C_IN, tk=512, name=f"dw_in_{tag}")
    token = start(("win",), (dwin,), f"in_{tag}")
    dx, dxb, dg_mix = _mm_nt_normbwd(dproj, win, sv["x"], dx1, _behind(small["norm_mix_g"], token),
                                     name=f"d_h_{tag}")

    small_grads = [dg_mix[0:1], dg_mem[0:1], dbg[0:1].reshape(3, D), ln_sums[2:3], ln_sums[0:1], ln_sums[1:2],
                   dg_ffn[0:1], dcw_a[0:K_A], dcw_b[0:K_B], dcw_f[0:K_F].reshape(K_F * 2 * FF_P // D, D)]
    return dx, dxb, small_grads, token


_SMALL_ROWS = (1, 1, 3, 1, 1, 1, 1, K_A, K_B, K_F * 2 * FF_P // D)
_CV_ROWS = 48


def kernel(x, mem, norm_mix_g, norm_mem_g, w_in, b_gate, conv_a_w, w_a_out, conv_b_w, conv_b_bias, ln_b_g, ln_b_b, w_b_out, w_kv, w_att_out, w_o, norm_ffn_g, w_up, conv_ffn_w, w_down, norm_final_g, loss_target, m_norm_mix_g, m_norm_mem_g, m_w_in, m_b_gate, m_conv_a_w, m_w_a_out, m_conv_b_w, m_conv_b_bias, m_ln_b_g, m_ln_b_b, m_w_b_out, m_w_kv, m_w_att_out, m_w_o, m_norm_ffn_g, m_w_up, m_conv_ffn_w, m_w_down, m_norm_final_g, v_norm_mix_g, v_norm_mem_g, v_w_in, v_b_gate, v_conv_a_w, v_w_a_out, v_conv_b_w, v_conv_b_bias, v_ln_b_g, v_ln_b_b, v_w_b_out, v_w_kv, v_w_att_out, v_w_o, v_norm_ffn_g, v_w_up, v_conv_ffn_w, v_w_down, v_norm_final_g):
    me = _my_index()
    x0, mem0, tgt = x.reshape(x.shape[1:]), mem.reshape(mem.shape[1:]), loss_target.reshape(x.shape[1:])
    up_pad = ((0, 0), (0, 0), (0, C_UP_P - C_UP))

    ag_groups = (("win",), ("wup", "wkv", "w4", "wdn", "cv"))
    kinds = ag_groups[0] + ag_groups[1]
    smalls, ag_handles = [], []
    token = jnp.zeros((8, LANE), F32)
    for l in range(DEPTH):
        cv = jnp.zeros((_CV_ROWS, C_UP_P), F32)
        cv = cv.at[0:K_F, 0:C_UP].set(conv_ffn_w[l]).at[3:3 + K_A, 0:R_O].set(conv_a_w[l])
        cv = cv.at[8:8 + K_B, 0:R_O].set(conv_b_w[l])
        shards = dict(
            win=w_in[l].astype(BF), wup=jnp.pad(w_up[l], up_pad[1:]).astype(BF), wkv=w_kv[l].astype(BF),
            w4=jnp.stack([w_a_out[l], w_b_out[l], w_att_out[l], w_o[l]]).astype(BF), wdn=w_down[l].astype(BF), cv=cv)
        whole = [SDS(*_WHOLE[k]) for k in kinds[:-1]] + [SDS((N_DEV,) + cv.shape, F32)]
        lands = dict(zip(kinds, _place_own([shards[k] for k in kinds], _gather_maps(kinds), whole,
                                           (kinds.index("wdn"),), name=f"ag_own_l{l}")))
        per_layer = []
        for g, grp in enumerate(ag_groups):
            handle, token = _exchange_start([shards[k] for k in grp], [lands[k] for k in grp], _gather_maps(grp),
                                            token, name=f"ag_start_l{l}_g{g}")
            per_layer.append(handle)
        ag_handles.append(per_layer)
        smalls.append(dict(
            norm_mix_g=norm_mix_g[l][None], norm_mem_g=norm_mem_g[l][None], b_gate=b_gate[l][None],
            conv_b_bias=conv_b_bias[l][None], ln_b_g=ln_b_g[l][None], ln_b_b=ln_b_b[l][None],
            norm_ffn_g=norm_ffn_g[l][None]))

    def rest_of_weights(l):
        def wait(after):
            wup, wkv, w4, wdn, cvg = _exchange_wait(ag_handles[l][1], _gather_maps(ag_groups[1]), after,
                                                    name=f"ag_wait_l{l}_g1")
            cw_f = jnp.stack([cvg[d, 0:K_F, :] for d in UP_ORDER], axis=1).reshape(K_F, 2 * FF_P)
            cw_a = cvg[:, 3:3 + K_A, 0:R_O].transpose(1, 0, 2).reshape(K_A, D)
            cw_b = cvg[:, 8:8 + K_B, 0:R_O].transpose(1, 0, 2).reshape(K_B, D)
            return (wup, wkv, w4, wdn, jnp.pad(cw_a, ((0, 8 - K_A), (0, 0))),
                    jnp.pad(cw_b, ((0, 32 - K_B), (0, 0))), jnp.pad(cw_f, ((0, 8 - K_F), (0, 0))))
        return wait

    wts, saved = [], []
    xs = x0
    h = _rms_fwd(xs, smalls[0]["norm_mix_g"], name="rms_fwd")
    behind = token
    for l in range(DEPTH):
        g_next = smalls[l + 1]["norm_mix_g"] if l + 1 < DEPTH else norm_final_g[None]
        (win,) = _exchange_wait(ag_handles[l][0], _gather_maps(ag_groups[0]), behind, name=f"ag_wait_l{l}_g0")
        xs, h, w_l, sv = _layer_fwd(xs, h, mem0, win, rest_of_weights(l), smalls[l], g_next, f"l{l}")
        behind = h
        wts.append(w_l)
        saved.append(sv)
    dx, dxb, head_sums = _loss_head(xs, tgt, norm_final_g[None], name="loss_head")

    rs_handles = []
    small_grads = [None] * DEPTH

    def start_scatter(grp, arrays, name):
        maps = _scatter_maps(grp)
        lands = _place_own(list(arrays), maps, [SDS((N_DEV,) + _SHARD[k], BF) for k in grp], (),
                           name=f"rs_own_{name}")
        handle, tok = _exchange_start(list(arrays), lands, maps, rs_handles[-1][2] if rs_handles else head_sums,
                                      name=f"rs_start_{name}")
        rs_handles.append((grp, handle, tok, name))
        return tok

    for l in reversed(range(DEPTH)):
        dx, dxb, small_grads[l], token = _layer_bwd(dx, dxb, mem0, wts[l], smalls[l], saved[l], start_scatter,
                                                    f"l{l}")

    staged = [dict() for _ in range(DEPTH)]
    for grp, handle, _, name in rs_handles[:-1]:
        staged[int(name[-1])].update(zip(grp, _exchange_wait(handle, _scatter_maps(grp), dx, name=f"rs_wait_{name}")))

    pack = jnp.concatenate(small_grads[0] + small_grads[1] + [head_sums[1:2], head_sums[0:1]], axis=0)
    pack = jnp.pad(pack, ((0, -pack.shape[0] % 8), (0, 0)))
    total = _allreduce_small(pack, name="allreduce_small")
    per_layer = sum(_SMALL_ROWS)
    parts = []
    for l in range(DEPTH):
        at, one = l * per_layer, []
        for rows in _SMALL_ROWS:
            one.append(total[at:at + rows])
            at += rows
        parts.append(one)
    g_final = total[DEPTH * per_layer]
    loss = 0.5 / D * jnp.sum(total[DEPTH * per_layer + 1])

    def both(i):
        return jnp.stack([parts[0][i], parts[1][i]])

    g_norm_mix, g_norm_mem = both(0)[:, 0], both(1)[:, 0]
    g_b_gate = both(2).reshape(DEPTH, 3 * D)
    g_cbias, g_lng, g_lnb, g_norm_ffn = both(3)[:, 0], both(4)[:, 0], both(5)[:, 0], both(6)[:, 0]
    g_conv_a = lax.dynamic_slice_in_dim(both(7), me * R_O, R_O, axis=2)
    g_conv_b = lax.dynamic_slice_in_dim(both(8), me * R_O, R_O, axis=2)
    g_conv_f = lax.dynamic_slice_in_dim(both(9).reshape(DEPTH, K_F, 2 * FF_P), _up_slot(me) * C_UP_P, C_UP, axis=2)

    small_g = [g_norm_mix, g_norm_mem, g_b_gate, g_conv_a, g_conv_b, g_cbias, g_lng, g_lnb, g_norm_ffn, g_conv_f,
               g_final]
    small_w = [norm_mix_g, norm_mem_g, b_gate, conv_a_w, conv_b_w, conv_b_bias, ln_b_g, ln_b_b, norm_ffn_g,
               conv_ffn_w, norm_final_g]
    small_m = [m_norm_mix_g, m_norm_mem_g, m_b_gate, m_conv_a_w, m_conv_b_w, m_conv_b_bias, m_ln_b_g, m_ln_b_b,
               m_norm_ffn_g, m_conv_ffn_w, m_norm_final_g]
    small_v = [v_norm_mix_g, v_norm_mem_g, v_b_gate, v_conv_a_w, v_conv_b_w, v_conv_b_bias, v_ln_b_g, v_ln_b_b,
               v_norm_ffn_g, v_conv_ffn_w, v_norm_final_g]
    upd = _adamw_small(_pack_rows(small_g), _pack_rows(small_w), _pack_rows(small_m), _pack_rows(small_v),
                       name="adamw_small")
    s_d, s_m, s_v = (_unpack_rows(p, small_w) for p in upd)
    (d_norm_mix, d_norm_mem, d_b_gate, d_conv_a, d_conv_b, d_cbias, d_lng, d_lnb, d_norm_ffn, d_conv_f,
     d_final) = s_d
    (nm_norm_mix, nm_norm_mem, nm_b_gate, nm_conv_a, nm_conv_b, nm_cbias, nm_lng, nm_lnb, nm_norm_ffn, nm_conv_f,
     nm_final) = s_m
    (nv_norm_mix, nv_norm_mem, nv_b_gate, nv_conv_a, nv_conv_b, nv_cbias, nv_lng, nv_lnb, nv_norm_ffn, nv_conv_f,
     nv_final) = s_v

    def big_update(kind, w, m, v, name):
        return _adamw_staged(staged[0][kind], staged[1][kind], w, m, v, name=name)

    r_up = [a[:, :, 0:C_UP] for a in big_update("wup", jnp.pad(w_up, up_pad), jnp.pad(m_w_up, up_pad),
                                                jnp.pad(v_w_up, up_pad), "adamw_w_up")]
    r_kv = big_update("wkv", w_kv, m_w_kv, v_w_kv, "adamw_w_kv")
    r_dn = big_update("wdn", w_down, m_w_down, v_w_down, "adamw_w_down")

    def four(a, b, c, d_):
        return jnp.stack([a, b, c, d_], axis=1).reshape(DEPTH, 4 * R_O, D)

    r_4 = _adamw_staged(
        staged[0]["w4"].reshape(N_DEV, 4 * R_O, D), staged[1]["w4"].reshape(N_DEV, 4 * R_O, D),
        four(w_a_out, w_b_out, w_att_out, w_o), four(m_w_a_out, m_w_b_out, m_w_att_out, m_w_o),
        four(v_w_a_out, v_w_b_out, v_w_att_out, v_w_o), name="adamw_w_out")
    grp, handle, _, name = rs_handles[-1]
    staged[0].update(zip(grp, _exchange_wait(handle, _scatter_maps(grp), r_4[0], name=f"rs_wait_{name}")))
    r_in = big_update("win", w_in, m_w_in, v_w_in, "adamw_w_in")
    r_a, r_b, r_att, r_o = ([a.reshape(DEPTH, 4, R_O, D)[:, j] for a in r_4] for j in range(4))

    grads = [g_norm_mix, g_norm_mem, r_in[0], g_b_gate, g_conv_a, r_a[0], g_conv_b, g_cbias, g_lng, g_lnb, r_b[0],
             r_kv[0], r_att[0], r_o[0], g_norm_ffn, r_up[0], g_conv_f, r_dn[0], g_final]
    deltas = [d_norm_mix, d_norm_mem, r_in[1], d_b_gate, d_conv_a, r_a[1], d_conv_b, d_cbias, d_lng, d_lnb, r_b[1],
              r_kv[1], r_att[1], r_o[1], d_norm_ffn, r_up[1], d_conv_f, r_dn[1], d_final]
    new_m = [nm_norm_mix, nm_norm_mem, r_in[2], nm_b_gate, nm_conv_a, r_a[2], nm_conv_b, nm_cbias, nm_lng, nm_lnb,
             r_b[2], r_kv[2], r_att[2], r_o[2], nm_norm_ffn, r_up[2], nm_conv_f, r_dn[2], nm_final]
    new_v = [nv_norm_mix, nv_norm_mem, r_in[3], nv_b_gate, nv_conv_a, r_a[3], nv_conv_b, nv_cbias, nv_lng, nv_lnb,
             r_b[3], r_kv[3], r_att[3], r_o[3], nv_norm_ffn, r_up[3], nv_conv_f, r_dn[3], nv_final]
    return (loss, dx[None], *grads, *deltas, *new_m, *new_v)
```

```python
import functools

import jax
import jax.numpy as jnp
from jax import lax
from jax.experimental import pallas as pl
from jax.experimental.pallas import tpu as pltpu

F32 = jnp.float32
BF = jnp.bfloat16
SDS = jax.ShapeDtypeStruct
MESH = pl.DeviceIdType.MESH
ANY = pl.BlockSpec(memory_space=pl.ANY)

N_DEV = 8
DEPTH = 2
D = 1024
N_HEADS = 4
HEAD = D // N_HEADS
D_FF = 2816
K_A, K_B, K_F = 3, 31, 3
NORM_EPS = 1e-6

C_IN = 9 * D // N_DEV
C_KV = 2 * D // N_DEV
C_UP = 2 * D_FF // N_DEV
LANE = 128
C_UP_P = -(-C_UP // LANE) * LANE
FF_P = 4 * C_UP_P
R_O = D // N_DEV
R_DN = D_FF // N_DEV

VMEM_LIMIT = 56 * 1024 * 1024
TM = 512
TR = 256
H_S, H_L = 16, 32

ADAM_LR, ADAM_B1, ADAM_B2, ADAM_EPS, ADAM_WD, ADAM_STEP = 0.001, 0.9, 0.999, 1e-08, 0.01, 10

UP_ORDER = (0, 4, 1, 5, 2, 6, 3, 7)


def _pcall(body, **kw):
    return pl.pallas_call(body, **kw)


def _cp(sem=None, **kw):
    return pltpu.CompilerParams(dimension_semantics=sem, vmem_limit_bytes=VMEM_LIMIT, **kw)


def _dot(a, b):
    return jnp.dot(a, b, preferred_element_type=F32)


def _dot_nt(a, b):
    return lax.dot_general(a, b, (((1,), (1,)), ((), ())), preferred_element_type=F32)


def _dot_tn(a, b):
    return lax.dot_general(a, b, (((0,), (0,)), ((), ())), preferred_element_type=F32)


def _sigmoid(z):
    return 1.0 / (1.0 + jnp.exp(-z))


def _rms(xv):
    return lax.rsqrt(jnp.mean(xv * xv, axis=-1, keepdims=True) + NORM_EPS)


def _up_slot(idx):
    return jnp.where(idx < 4, 2 * idx, 2 * (idx - 4) + 1)


def _dn_row(idx):
    return C_UP_P * (idx // 2) + R_DN * (idx % 2)


def _mm(a, b, *, ta=False, tb=False, b_slabs=False, out_slabs=False, out_dtype=BF, tm=TM, tn=512, tk=None, name):
    m, k_dim = (a.shape[1], a.shape[0]) if ta else a.shape
    if b_slabs:
        tn = b.shape[2]
        n = b.shape[0] * tn
    else:
        n = b.shape[0] if tb else b.shape[1]
    tm, tn = min(tm, m), min(tn, n)
    tk = k_dim if tk is None else min(tk, k_dim)
    nk = k_dim // tk
    assert m % tm == 0 and n % tn == 0 and k_dim % tk == 0
    dims = (((0 if ta else 1,), (1 if tb else 0,)), ((), ()))

    def body(a_ref, b_ref, o_ref, *scratch):
        part = lax.dot_general(a_ref[...], b_ref[...], dims, preferred_element_type=F32)
        if nk == 1:
            o_ref[...] = part.astype(o_ref.dtype)
            return
        acc = scratch[0]
        k = pl.program_id(2)

        @pl.when(k == 0)
        def _():
            acc[...] = part

        @pl.when(k > 0)
        def _():
            acc[...] += part

        @pl.when(k == nk - 1)
        def _():
            o_ref[...] = acc[...].astype(o_ref.dtype)

    a_spec = pl.BlockSpec((tk, tm), lambda i, j, k: (k, i)) if ta else pl.BlockSpec((tm, tk), lambda i, j, k: (i, k))
    if b_slabs:
        b_spec = pl.BlockSpec((None, tk, tn), lambda i, j, k: (j, k, 0))
    elif tb:
        b_spec = pl.BlockSpec((tn, tk), lambda i, j, k: (j, k))
    else:
        b_spec = pl.BlockSpec((tk, tn), lambda i, j, k: (k, j))
    if out_slabs:
        out_spec, out_shape = pl.BlockSpec((None, tm, tn), lambda i, j, k: (j, i, 0)), SDS((n // tn, m, tn), out_dtype)
    else:
        out_spec, out_shape = pl.BlockSpec((tm, tn), lambda i, j, k: (i, j)), SDS((m, n), out_dtype)
    return _pcall(
        body, grid=(m // tm, n // tn, nk), in_specs=[a_spec, b_spec],
        out_specs=out_spec, out_shape=out_shape,
        scratch_shapes=[pltpu.VMEM((tm, tn), F32)] if nk > 1 else [],
        compiler_params=_cp(("parallel", "parallel", "arbitrary")), name=name)(a, b)


def _mm_res_norm(a, w, x, g, *, name):
    s, k_dim = a.shape
    tm = min(TM, s)

    def body(a_ref, w_ref, x_ref, g_ref, xo_ref, h_ref):
        xo = x_ref[...] + _dot(a_ref[...], w_ref[...])
        xo_ref[...] = xo
        h_ref[...] = ((xo * _rms(xo)) * g_ref[...]).astype(BF)

    return _pcall(
        body, grid=(s // tm,),
        in_specs=[pl.BlockSpec((tm, k_dim), lambda i: (i, 0)), pl.BlockSpec((k_dim, D), lambda i: (0, 0)),
                  pl.BlockSpec((tm, D), lambda i: (i, 0)), pl.BlockSpec((1, D), lambda i: (0, 0))],
        out_specs=[pl.BlockSpec((tm, D), lambda i: (i, 0))] * 2,
        out_shape=[SDS((s, D), F32), SDS((s, D), BF)],
        compiler_params=_cp(("parallel",)), name=name)(a, w, x, g)


def _mm_nt_normbwd(da, w, x, dres, g, *, name):
    s, k_dim = da.shape
    tm = min(TM, s)
    nk, _, tk = w.shape
    assert k_dim == nk * tk

    def body(da_ref, w_ref, x_ref, dres_ref, g_ref, dx_ref, dxb_ref, dg_ref, acc):
        i, k = pl.program_id(0), pl.program_id(1)
        part = _dot_nt(da_ref[...], w_ref[...])

        @pl.when(k == 0)
        def _():
            acc[...] = part

        @pl.when(k > 0)
        def _():
            acc[...] += part

        @pl.when((i == 0) & (k == 0))
        def _():
            dg_ref[...] = jnp.zeros_like(dg_ref)

        @pl.when(k == nk - 1)
        def _():
            dh = acc[...]
            xv = x_ref[...]
            r = _rms(xv)
            xn = xv * r
            dg_ref[0:1, :] += jnp.sum(dh * xn, axis=0, keepdims=True)
            dxn = dh * g_ref[...]
            dx = dres_ref[...] + r * (dxn - xn * jnp.mean(dxn * xn, axis=-1, keepdims=True))
            dx_ref[...] = dx
            dxb_ref[...] = dx.astype(BF)

    row = lambda i, k: (i, 0)
    return _pcall(
        body, grid=(s // tm, nk),
        in_specs=[pl.BlockSpec((tm, tk), lambda i, k: (i, k)), pl.BlockSpec((None, D, tk), lambda i, k: (k, 0, 0)),
                  pl.BlockSpec((tm, D), row), pl.BlockSpec((tm, D), row), pl.BlockSpec((1, D), lambda i, k: (0, 0))],
        out_specs=[pl.BlockSpec((tm, D), row), pl.BlockSpec((tm, D), row), pl.BlockSpec((8, D), lambda i, k: (0, 0))],
        out_shape=[SDS((s, D), F32), SDS((s, D), BF), SDS((8, D), F32)],
        scratch_shapes=[pltpu.VMEM((tm, D), F32)],
        compiler_params=_cp(("arbitrary", "arbitrary")), name=name)(da, w, x, dres, g)


def _rms_fwd(x, g, *, name):
    s = x.shape[0]
    tm = min(TM, s)

    def body(x_ref, g_ref, h_ref):
        xv = x_ref[...]
        h_ref[...] = ((xv * _rms(xv)) * g_ref[...]).astype(BF)

    return _pcall(
        body, grid=(s // tm,),
        in_specs=[pl.BlockSpec((tm, D), lambda i: (i, 0)), pl.BlockSpec((1, D), lambda i: (0, 0))],
        out_specs=pl.BlockSpec((tm, D), lambda i: (i, 0)), out_shape=SDS((s, D), BF),
        compiler_params=_cp(("parallel",)), name=name)(x, g)


def _loss_head(x, tgt, g, *, name):
    s = x.shape[0]
    tm = min(TM, s)

    def body(x_ref, t_ref, g_ref, dx_ref, dxb_ref, sums_ref):
        @pl.when(pl.program_id(0) == 0)
        def _():
            sums_ref[...] = jnp.zeros_like(sums_ref)

        xv = x_ref[...]
        r = _rms(xv)
        xn = xv * r
        diff = xn * g_ref[...] - t_ref[...]
        sums_ref[0:1, :] += jnp.sum(diff * diff, axis=0, keepdims=True)
        dy = diff * (1.0 / D)
        sums_ref[1:2, :] += jnp.sum(dy * xn, axis=0, keepdims=True)
        dxn = dy * g_ref[...]
        dx = r * (dxn - xn * jnp.mean(dxn * xn, axis=-1, keepdims=True))
        dx_ref[...] = dx
        dxb_ref[...] = dx.astype(BF)

    row = lambda i: (i, 0)
    return _pcall(
        body, grid=(s // tm,),
        in_specs=[pl.BlockSpec((tm, D), row), pl.BlockSpec((tm, D), row), pl.BlockSpec((1, D), lambda i: (0, 0))],
        out_specs=[pl.BlockSpec((tm, D), row), pl.BlockSpec((tm, D), row), pl.BlockSpec((8, D), lambda i: (0, 0))],
        out_shape=[SDS((s, D), F32), SDS((s, D), BF), SDS((8, D), F32)],
        compiler_params=_cp(("arbitrary",)), name=name)(x, tgt, g)


def _halo_before(i, tr, h):
    return jnp.maximum(i * (tr // h) - 1, 0)


def _halo_after(i, tr, h, s):
    return jnp.minimum((i + 1) * (tr // h), s // h - 1)


def _taps(buf, w_ref, sl, k_w, base, rows):
    acc = None
    for k in range(k_w):
        t = w_ref[k:k + 1, sl] * buf[base + k:base + k + rows, sl]
        acc = t if acc is None else acc + t
    return acc


def _taps_rev(buf, w_ref, sl, k_w, rows):
    acc = None
    for k in range(k_w):
        t = w_ref[k:k + 1, sl] * buf[k_w - 1 - k:k_w - 1 - k + rows, sl]
        acc = t if acc is None else acc + t
    return acc


def _tap_grads(dw_ref, dc, buf, sl, k_w, base, rows):
    for k in range(k_w):
        dw_ref[k:k + 1, sl] += jnp.sum(dc * buf[base + k:base + k + rows, sl], axis=0, keepdims=True)


def _bra_fwd(proj, cw, *, name):
    s = proj.shape[0]
    tr, h = min(TR, s), H_S

    def body(cur, halo, w_ref, za_ref, cvb):
        i = pl.program_id(0)
        hv = halo[:, D:2 * D].astype(F32) * halo[:, 2 * D:3 * D].astype(F32)
        cvb[0:h, :] = jnp.where(i == 0, 0.0, hv)
        cvb[h:h + tr, :] = cur[:, D:2 * D].astype(F32) * cur[:, 2 * D:3 * D].astype(F32)
        for c in range(D // LANE):
            sl = slice(LANE * c, LANE * c + LANE)
            ca = _taps(cvb, w_ref, sl, K_A, h - (K_A - 1), tr)
            za_ref[:, sl] = (cur[:, sl].astype(F32) * ca).astype(BF)

    return _pcall(
        body, grid=(s // tr,),
        in_specs=[pl.BlockSpec((tr, 3 * D), lambda i: (i, 0)),
                  pl.BlockSpec((h, 3 * D), lambda i: (_halo_before(i, tr, h), 0)),
                  pl.BlockSpec((8, D), lambda i: (0, 0))],
        out_specs=pl.BlockSpec((tr, D), lambda i: (i, 0)), out_shape=SDS((s, D), BF),
        scratch_shapes=[pltpu.VMEM((h + tr, D), F32)],
        compiler_params=_cp(("parallel",)), name=name)(proj, proj, cw)


def _bra_bwd(proj, dza, cw, *, name):
    s = proj.shape[0]
    tr, h = min(TR, s), H_S
    n = s // tr

    def body(before, cur, after, dz_cur, dz_after, w_ref, da_ref, dw_ref, cvb, dcab):
        i = pl.program_id(0)

        @pl.when(i == 0)
        def _():
            dw_ref[...] = jnp.zeros_like(dw_ref)

        first, last = i == 0, i == n - 1
        cvb[0:h, :] = jnp.where(first, 0.0, before[:, D:2 * D].astype(F32) * before[:, 2 * D:3 * D].astype(F32))
        cvb[h:h + tr, :] = cur[:, D:2 * D].astype(F32) * cur[:, 2 * D:3 * D].astype(F32)
        dcab[0:tr, :] = dz_cur[...].astype(F32) * cur[:, 0:D].astype(F32)
        dcab[tr:tr + h, :] = jnp.where(last, 0.0, dz_after[...].astype(F32) * after[:, 0:D].astype(F32))
        for c in range(D // LANE):
            sl = slice(LANE * c, LANE * c + LANE)
            gc = cur[:, D + LANE * c:D + LANE * c + LANE].astype(F32)
            v = cur[:, 2 * D + LANE * c:2 * D + LANE * c + LANE].astype(F32)
            ca = _taps(cvb, w_ref, sl, K_A, h - (K_A - 1), tr)
            da_ref[:, sl] = (dz_cur[:, sl].astype(F32) * ca).astype(BF)
            dcv = _taps_rev(dcab, w_ref, sl, K_A, tr)
            da_ref[:, D + LANE * c:D + LANE * c + LANE] = (dcv * v).astype(BF)
            da_ref[:, 2 * D + LANE * c:2 * D + LANE * c + LANE] = (dcv * gc).astype(BF)
            _tap_grads(dw_ref, dcab[0:tr, sl], cvb, sl, K_A, h - (K_A - 1), tr)

    return _pcall(
        body, grid=(n,),
        in_specs=[pl.BlockSpec((h, 3 * D), lambda i: (_halo_before(i, tr, h), 0)),
                  pl.BlockSpec((tr, 3 * D), lambda i: (i, 0)),
                  pl.BlockSpec((h, 3 * D), lambda i: (_halo_after(i, tr, h, s), 0)),
                  pl.BlockSpec((tr, D), lambda i: (i, 0)),
                  pl.BlockSpec((h, D), lambda i: (_halo_after(i, tr, h, s), 0)),
                  pl.BlockSpec((8, D), lambda i: (0, 0))],
        out_specs=[pl.BlockSpec((tr, 3 * D), lambda i: (i, 0)), pl.BlockSpec((8, D), lambda i: (0, 0))],
        out_shape=[SDS((s, 3 * D), BF), SDS((8, D), F32)],
        scratch_shapes=[pltpu.VMEM((h + tr, D), F32), pltpu.VMEM((tr + h, D), F32)],
        compiler_params=_cp(("arbitrary",)), name=name)(proj, proj, proj, dza, dza, cw)


_U_COL, _UG_COL = 3, 4


def _brb_conv_fwd(proj, cw, bias, *, name):
    s = proj.shape[0]
    tr, h = min(TR, s), H_L

    def body(u_cur, ug_cur, u_halo, ug_halo, w_ref, b_ref, cb_ref, glb):
        i = pl.program_id(0)
        glb[0:h, :] = jnp.where(i == 0, 0.0, u_halo[...].astype(F32) * _sigmoid(ug_halo[...].astype(F32)))
        glb[h:h + tr, :] = u_cur[...].astype(F32) * _sigmoid(ug_cur[...].astype(F32))
        for c in range(D // LANE):
            sl = slice(LANE * c, LANE * c + LANE)
            cb_ref[:, sl] = (_taps(glb, w_ref, sl, K_B, h - (K_B - 1), tr) + b_ref[:, sl]).astype(BF)

    return _pcall(
        body, grid=(s // tr,),
        in_specs=[pl.BlockSpec((tr, D), lambda i: (i, _U_COL)), pl.BlockSpec((tr, D), lambda i: (i, _UG_COL)),
                  pl.BlockSpec((h, D), lambda i: (_halo_before(i, tr, h), _U_COL)),
                  pl.BlockSpec((h, D), lambda i: (_halo_before(i, tr, h), _UG_COL)),
                  pl.BlockSpec((32, D), lambda i: (0, 0)), pl.BlockSpec((1, D), lambda i: (0, 0))],
        out_specs=pl.BlockSpec((tr, D), lambda i: (i, 0)), out_shape=SDS((s, D), BF),
        scratch_shapes=[pltpu.VMEM((h + tr, D), F32)],
        compiler_params=_cp(("parallel",)), name=name)(proj, proj, proj, proj, cw, bias)


def _brb_conv_bwd(proj, dcb, cw, *, name):
    s = proj.shape[0]
    tr, h = min(TR, s), H_L
    n = s // tr

    def body(u_before, ug_before, u_cur, ug_cur, d_cur, d_after, w_ref, db_ref, dw_ref, glb, dcbb):
        i = pl.program_id(0)

        @pl.when(i == 0)
        def _():
            dw_ref[...] = jnp.zeros_like(dw_ref)

        glb[0:h, :] = jnp.where(i == 0, 0.0, u_before[...].astype(F32) * _sigmoid(ug_before[...].astype(F32)))
        glb[h:h + tr, :] = u_cur[...].astype(F32) * _sigmoid(ug_cur[...].astype(F32))
        dcbb[0:tr, :] = d_cur[...].astype(F32)
        dcbb[tr:tr + h, :] = jnp.where(i == n - 1, 0.0, d_after[...].astype(F32))
        for c in range(D // LANE):
            sl = slice(LANE * c, LANE * c + LANE)
            dglu = _taps_rev(dcbb, w_ref, sl, K_B, tr)
            u = u_cur[:, sl].astype(F32)
            sg = _sigmoid(ug_cur[:, sl].astype(F32))
            db_ref[:, sl] = (dglu * sg).astype(BF)
            db_ref[:, D + LANE * c:D + LANE * c + LANE] = (dglu * u * sg * (1.0 - sg)).astype(BF)
            _tap_grads(dw_ref, dcbb[0:tr, sl], glb, sl, K_B, h - (K_B - 1), tr)

    return _pcall(
        body, grid=(n,),
        in_specs=[pl.BlockSpec((h, D), lambda i: (_halo_before(i, tr, h), _U_COL)),
                  pl.BlockSpec((h, D), lambda i: (_halo_before(i, tr, h), _UG_COL)),
                  pl.BlockSpec((tr, D), lambda i: (i, _U_COL)), pl.BlockSpec((tr, D), lambda i: (i, _UG_COL)),
                  pl.BlockSpec((tr, D), lambda i: (i, 0)),
                  pl.BlockSpec((h, D), lambda i: (_halo_after(i, tr, h, s), 0)),
                  pl.BlockSpec((32, D), lambda i: (0, 0))],
        out_specs=[pl.BlockSpec((tr, 2 * D), lambda i: (i, 0)), pl.BlockSpec((32, D), lambda i: (0, 0))],
        out_shape=[SDS((s, 2 * D), BF), SDS((32, D), F32)],
        scratch_shapes=[pltpu.VMEM((h + tr, D), F32), pltpu.VMEM((tr + h, D), F32)],
        compiler_params=_cp(("arbitrary",)), name=name)(proj, proj, proj, proj, dcb, dcb, cw)


def _ln_silu_fwd(cb, g, b, *, name):
    s = cb.shape[0]
    tm = min(TM, s)

    def body(cb_ref, g_ref, b_ref, sb_ref):
        z = cb_ref[...].astype(F32)
        zc = z - jnp.mean(z, axis=-1, keepdims=True)
        ln = (zc * lax.rsqrt(jnp.mean(zc * zc, axis=-1, keepdims=True) + NORM_EPS)) * g_ref[...] + b_ref[...]
        sb_ref[...] = (ln * _sigmoid(ln)).astype(BF)

    row = lambda i: (i, 0)
    vec = pl.BlockSpec((1, D), lambda i: (0, 0))
    return _pcall(
        body, grid=(s // tm,), in_specs=[pl.BlockSpec((tm, D), row), vec, vec],
        out_specs=pl.BlockSpec((tm, D), row), out_shape=SDS((s, D), BF),
        compiler_params=_cp(("parallel",)), name=name)(cb, g, b)


def _ln_silu_bwd(cb, dsb, g, b, *, name):
    s = cb.shape[0]
    tm = min(TM, s)

    def body(cb_ref, dsb_ref, g_ref, b_ref, dcb_ref, sums_ref):
        @pl.when(pl.program_id(0) == 0)
        def _():
            sums_ref[...] = jnp.zeros_like(sums_ref)

        z = cb_ref[...].astype(F32)
        zc = z - jnp.mean(z, axis=-1, keepdims=True)
        rstd = lax.rsqrt(jnp.mean(zc * zc, axis=-1, keepdims=True) + NORM_EPS)
        lnh = zc * rstd
        ln = lnh * g_ref[...] + b_ref[...]
        sg = _sigmoid(ln)
        dln = dsb_ref[...].astype(F32) * (sg * (1.0 + ln * (1.0 - sg)))
        sums_ref[0:1, :] += jnp.sum(dln * lnh, axis=0, keepdims=True)
        sums_ref[1:2, :] += jnp.sum(dln, axis=0, keepdims=True)
        dlnh = dln * g_ref[...]
        dz = rstd * (dlnh - jnp.mean(dlnh, axis=-1, keepdims=True)
                     - lnh * jnp.mean(dlnh * lnh, axis=-1, keepdims=True))
        sums_ref[2:3, :] += jnp.sum(dz, axis=0, keepdims=True)
        dcb_ref[...] = dz.astype(BF)

    row = lambda i: (i, 0)
    vec = pl.BlockSpec((1, D), lambda i: (0, 0))
    return _pcall(
        body, grid=(s // tm,), in_specs=[pl.BlockSpec((tm, D), row), pl.BlockSpec((tm, D), row), vec, vec],
        out_specs=[pl.BlockSpec((tm, D), row), pl.BlockSpec((8, D), lambda i: (0, 0))],
        out_shape=[SDS((s, D), BF), SDS((8, D), F32)],
        compiler_params=_cp(("arbitrary",)), name=name)(cb, dsb, g, b)


_Q_COL = 5 * D // HEAD


def _kv_prep(mem, g, wkv, *, name):
    m = mem.shape[0]

    def body(mem_ref, g_ref, w_ref, memn_ref, kv_ref):
        mv = mem_ref[...]
        memn = ((mv * _rms(mv)) * g_ref[...]).astype(BF)
        memn_ref[...] = memn
        for dev in range(N_DEV):
            kv_ref[:, dev * C_KV:(dev + 1) * C_KV] = _dot(memn, w_ref[dev]).astype(BF)

    return _pcall(body, out_shape=[SDS((m, D), BF), SDS((m, 2 * D), BF)],
                  compiler_params=_cp(), name=name)(mem, g, wkv)


def _softmax_rows(q, k):
    sc = _dot_nt(q, k) * (1.0 / (HEAD ** 0.5))
    e = jnp.exp(sc - jnp.max(sc, axis=-1, keepdims=True))
    return e / jnp.sum(e, axis=-1, keepdims=True)


def _attn_fwd(proj, kv, *, name):
    s, m = proj.shape[0], kv.shape[0]
    tm = min(TM, s)

    def body(q_ref, k_ref, v_ref, o_ref):
        p = _softmax_rows(q_ref[...], k_ref[...])
        o_ref[...] = _dot(p.astype(BF), v_ref[...]).astype(BF)

    return _pcall(
        body, grid=(s // tm, N_HEADS),
        in_specs=[pl.BlockSpec((tm, HEAD), lambda i, hd: (i, _Q_COL + hd)),
                  pl.BlockSpec((m, HEAD), lambda i, hd: (0, hd)),
                  pl.BlockSpec((m, HEAD), lambda i, hd: (0, N_HEADS + hd))],
        out_specs=pl.BlockSpec((tm, HEAD), lambda i, hd: (i, hd)), out_shape=SDS((s, D), BF),
        compiler_params=_cp(("parallel", "parallel")), name=name)(proj, kv, kv)


def _attn_bwd(proj, kv, do, *, name):
    s, m = proj.shape[0], kv.shape[0]
    tm = min(TM, s)

    def body(q_ref, k_ref, v_ref, do_ref, dq_ref, dk_ref, dv_ref):
        @pl.when(pl.program_id(1) == 0)
        def _():
            dk_ref[...] = jnp.zeros_like(dk_ref)
            dv_ref[...] = jnp.zeros_like(dv_ref)

        q, k, dov = q_ref[...], k_ref[...], do_ref[...]
        p = _softmax_rows(q, k)
        dp = _dot_nt(dov, v_ref[...])
        dv_ref[...] += _dot_tn(p.astype(BF), dov)
        ds = (p * (dp - jnp.sum(dp * p, axis=-1, keepdims=True)) * (1.0 / (HEAD ** 0.5))).astype(BF)
        dq_ref[...] = _dot(ds, k).astype(BF)
        dk_ref[...] += _dot_tn(ds, q)

    return _pcall(
        body, grid=(N_HEADS, s // tm),
        in_specs=[pl.BlockSpec((tm, HEAD), lambda hd, i: (i, _Q_COL + hd)),
                  pl.BlockSpec((m, HEAD), lambda hd, i: (0, hd)),
                  pl.BlockSpec((m, HEAD), lambda hd, i: (0, N_HEADS + hd)),
                  pl.BlockSpec((tm, HEAD), lambda hd, i: (i, hd))],
        out_specs=[pl.BlockSpec((tm, HEAD), lambda hd, i: (i, hd)),
                   pl.BlockSpec((m, HEAD), lambda hd, i: (0, hd)),
                   pl.BlockSpec((m, HEAD), lambda hd, i: (0, hd))],
        out_shape=[SDS((s, D), BF), SDS((m, D), F32), SDS((m, D), F32)],
        compiler_params=_cp(("parallel", "arbitrary")), name=name)(proj, kv, kv, do)


def _kv_bwd(mem, g, memn, dk, dv, wkv, *, name):
    def body(mem_ref, g_ref, memn_ref, dk_ref, dv_ref, w_ref, dw_ref, dg_ref):
        memn = memn_ref[...]
        dmemn = None
        for dev in range(N_DEV):
            d_ref, col = (dk_ref, dev) if dev < N_HEADS else (dv_ref, dev - N_HEADS)
            dslab = d_ref[:, col * C_KV:(col + 1) * C_KV].astype(BF)
            dw_ref[dev] = _dot_tn(memn, dslab).astype(BF)
            part = _dot_nt(dslab, w_ref[dev])
            dmemn = part if dmemn is None else dmemn + part
        mv = mem_ref[...]
        dg_ref[...] = jnp.zeros_like(dg_ref)
        dg_ref[0:1, :] = jnp.sum(dmemn * (mv * _rms(mv)), axis=0, keepdims=True)

    assert C_KV == HEAD
    return _pcall(body, out_shape=[SDS((N_DEV, D, C_KV), BF), SDS((8, D), F32)],
                  compiler_params=_cp(), name=name)(mem, g, memn, dk, dv, wkv)


_TM_MIX = 256


def _mix_out(x, za, sb, o, proj, w4, bg, g_next, *, name):
    s = x.shape[0]
    tm = min(_TM_MIX, s)

    def body(x_ref, za_ref, sb_ref, o_ref, pg_ref, w4_ref, bg_ref, gn_ref,
             ya_ref, yb_ref, yc_ref, mg_ref, x1_ref, h_ref):
        ys = (_dot(za_ref[...], w4_ref[0]), _dot(sb_ref[...], w4_ref[1]), _dot(o_ref[...], w4_ref[2]))
        merged = None
        for j, (y, y_ref) in enumerate(zip(ys, (ya_ref, yb_ref, yc_ref))):
            y_ref[...] = y.astype(BF)
            gate = _sigmoid(pg_ref[:, j * D:(j + 1) * D].astype(F32) + bg_ref[:, j * D:(j + 1) * D])
            merged = gate * y if merged is None else merged + gate * y
        mg = merged.astype(BF)
        mg_ref[...] = mg
        x1 = x_ref[...] + _dot(mg, w4_ref[3])
        x1_ref[...] = x1
        h_ref[...] = ((x1 * _rms(x1)) * gn_ref[...]).astype(BF)

    row = lambda i: (i, 0)
    act = pl.BlockSpec((tm, D), row)
    return _pcall(
        body, grid=(s // tm,),
        in_specs=[act, act, act, act, pl.BlockSpec((tm, 3 * D), lambda i: (i, 2)),
                  pl.BlockSpec((4, D, D), lambda i: (0, 0, 0)), pl.BlockSpec((1, 3 * D), lambda i: (0, 0)),
                  pl.BlockSpec((1, D), lambda i: (0, 0))],
        out_specs=[act] * 6,
        out_shape=[SDS((s, D), BF)] * 4 + [SDS((s, D), F32), SDS((s, D), BF)],
        compiler_params=_cp(("parallel",)), name=name)(x, za, sb, o, proj, w4, bg, g_next)


def _mix_bwd(dxb, ya, yb, yc, proj, w4, bg, *, name):
    s = dxb.shape[0]
    tm = min(_TM_MIX, s)

    def body(dx_ref, ya_ref, yb_ref, yc_ref, pg_ref, w4_ref, bg_ref,
             dya_ref, dyb_ref, dyc_ref, dza_ref, dsb_ref, do_ref, dgt_ref, dbg_ref):
        @pl.when(pl.program_id(0) == 0)
        def _():
            dbg_ref[...] = jnp.zeros_like(dbg_ref)

        dm = _dot_nt(dx_ref[...], w4_ref[3])
        for j, (y_ref, dy_ref, din_ref) in enumerate(zip((ya_ref, yb_ref, yc_ref), (dya_ref, dyb_ref, dyc_ref),
                                                         (dza_ref, dsb_ref, do_ref))):
            cols = slice(j * D, (j + 1) * D)
            gate = _sigmoid(pg_ref[:, cols].astype(F32) + bg_ref[:, cols])
            dy = (dm * gate).astype(BF)
            dy_ref[...] = dy
            din_ref[...] = _dot_nt(dy, w4_ref[j]).astype(BF)
            dpre = dm * y_ref[...].astype(F32) * gate * (1.0 - gate)
            dgt_ref[:, cols] = dpre.astype(BF)
            dbg_ref[0:1, cols] += jnp.sum(dpre, axis=0, keepdims=True)

    row = lambda i: (i, 0)
    act = pl.BlockSpec((tm, D), row)
    return _pcall(
        body, grid=(s // tm,),
        in_specs=[act, act, act, act, pl.BlockSpec((tm, 3 * D), lambda i: (i, 2)),
                  pl.BlockSpec((4, D, D), lambda i: (0, 0, 0)), pl.BlockSpec((1, 3 * D), lambda i: (0, 0))],
        out_specs=[act] * 6 + [pl.BlockSpec((tm, 3 * D), row), pl.BlockSpec((8, 3 * D), lambda i: (0, 0))],
        out_shape=[SDS((s, D), BF)] * 6 + [SDS((s, 3 * D), BF), SDS((8, 3 * D), F32)],
        compiler_params=_cp(("arbitrary",)), name=name)(dxb, ya, yb, yc, proj, w4, bg)


_PAIR = 2 * C_UP_P


def _ffn_act(u2, cw, *, name):
    s = u2.shape[0]
    tr, h = min(TR, s), H_S

    def body(cur, halo, w_ref, act_ref, ub):
        i = pl.program_id(1)
        ub[0:h, :] = jnp.where(i == 0, 0.0, halo[...].astype(F32))
        ub[h:h + tr, :] = cur[...].astype(F32)
        for c in range(C_UP_P // LANE):
            gl = slice(LANE * c, LANE * c + LANE)
            ul = slice(C_UP_P + LANE * c, C_UP_P + LANE * c + LANE)
            gt = _taps(ub, w_ref, gl, K_F, h - (K_F - 1), tr)
            up = _taps(ub, w_ref, ul, K_F, h - (K_F - 1), tr)
            act_ref[:, gl] = (gt * _sigmoid(gt) * up).astype(BF)

    return _pcall(
        body, grid=(4, s // tr),
        in_specs=[pl.BlockSpec((tr, _PAIR), lambda p, i: (i, p)),
                  pl.BlockSpec((h, _PAIR), lambda p, i: (_halo_before(i, tr, h), p)),
                  pl.BlockSpec((8, _PAIR), lambda p, i: (0, p))],
        out_specs=pl.BlockSpec((tr, C_UP_P), lambda p, i: (i, p)), out_shape=SDS((s, FF_P), BF),
        scratch_shapes=[pltpu.VMEM((h + tr, _PAIR), F32)],
        compiler_params=_cp(("parallel", "parallel")), name=name)(u2, u2, cw)


def _ffn_bwd(u2, dact, cw, *, name):
    s = u2.shape[0]
    tr, h = min(TR, s), H_S
    n = s // tr
    ext = tr + h

    def body(before, cur, after, da_cur, da_after, w_ref, du_ref, dw_ref, ub, dcb):
        i = pl.program_id(1)

        @pl.when(i == 0)
        def _():
            dw_ref[...] = jnp.zeros_like(dw_ref)

        ub[0:h, :] = jnp.where(i == 0, 0.0, before[...].astype(F32))
        ub[h:h + tr, :] = cur[...].astype(F32)
        ub[h + tr:h + tr + h, :] = jnp.where(i == n - 1, 0.0, after[...].astype(F32))
        for c in range(C_UP_P // LANE):
            gl = slice(LANE * c, LANE * c + LANE)
            ul = slice(C_UP_P + LANE * c, C_UP_P + LANE * c + LANE)
            gt = _taps(ub, w_ref, gl, K_F, h - (K_F - 1), ext)
            up = _taps(ub, w_ref, ul, K_F, h - (K_F - 1), ext)
            da = jnp.concatenate([da_cur[:, gl].astype(F32),
                                  jnp.where(i == n - 1, 0.0, da_after[:, gl].astype(F32))], axis=0)
            sg = _sigmoid(gt)
            dcb[:, gl] = da * up * (sg * (1.0 + gt * (1.0 - sg)))
            dcb[:, ul] = da * (gt * sg)
        for c in range(_PAIR // LANE):
            sl = slice(LANE * c, LANE * c + LANE)
            du_ref[:, sl] = _taps_rev(dcb, w_ref, sl, K_F, tr).astype(BF)
            _tap_grads(dw_ref, dcb[0:tr, sl], ub, sl, K_F, h - (K_F - 1), tr)

    return _pcall(
        body, grid=(4, n),
        in_specs=[pl.BlockSpec((h, _PAIR), lambda p, i: (_halo_before(i, tr, h), p)),
                  pl.BlockSpec((tr, _PAIR), lambda p, i: (i, p)),
                  pl.BlockSpec((h, _PAIR), lambda p, i: (_halo_after(i, tr, h, s), p)),
                  pl.BlockSpec((tr, C_UP_P), lambda p, i: (i, p)),
                  pl.BlockSpec((h, C_UP_P), lambda p, i: (_halo_after(i, tr, h, s), p)),
                  pl.BlockSpec((8, _PAIR), lambda p, i: (0, p))],
        out_specs=[pl.BlockSpec((tr, _PAIR), lambda p, i: (i, p)), pl.BlockSpec((8, _PAIR), lambda p, i: (0, p))],
        out_shape=[SDS((s, 2 * FF_P), BF), SDS((8, 2 * FF_P), F32)],
        scratch_shapes=[pltpu.VMEM((h + tr + h, _PAIR), F32), pltpu.VMEM((ext, _PAIR), F32)],
        compiler_params=_cp(("parallel", "arbitrary")), name=name)(u2, u2, u2, dact, dact, cw)


def _relations():
    x, y, c = lax.axis_index("x"), lax.axis_index("y"), lax.axis_index("c")
    out = []
    for r in range(1, N_DEV):
        rx, ry, rc = (r >> 2) & 1, (r >> 1) & 1, r & 1
        out.append((r, (x ^ rx, y ^ ry, c ^ rc)))
    return out


def _my_index():
    return 4 * lax.axis_index("x") + 2 * lax.axis_index("y") + lax.axis_index("c")


def _exchange(n_arrays, src_of, dst_of, refs):
    ssem, rsem, lsem = refs
    me = _my_index()
    local = []
    for a in range(n_arrays):
        loc = pltpu.make_async_copy(src_of(a, me), dst_of(a, me), lsem.at[a])
        loc.start()
        local.append(loc)

    def copy(a, r, peer, src_idx, dst_idx):
        return pltpu.make_async_remote_copy(
            src_ref=src_of(a, src_idx), dst_ref=dst_of(a, dst_idx), send_sem=ssem.at[a, r - 1],
            recv_sem=rsem.at[a, r - 1], device_id=peer, device_id_type=MESH)

    peers = [(r, peer, 4 * peer[0] + 2 * peer[1] + peer[2]) for r, peer in _relations()]
    for r, peer, p_idx in peers:
        for a in range(n_arrays):
            copy(a, r, peer, p_idx, me).start()
    for r, peer, p_idx in peers:
        for a in range(n_arrays):
            copy(a, r, peer, p_idx, me).wait_send()
            copy(a, r, peer, me, p_idx).wait_recv()
    for loc in local:
        loc.wait()


def _sem_scratch(n_arrays):
    return [pltpu.SemaphoreType.DMA((n_arrays, N_DEV - 1)), pltpu.SemaphoreType.DMA((n_arrays, N_DEV - 1)),
            pltpu.SemaphoreType.DMA((n_arrays,))]


def _slab(kind, ref, idx):
    if kind in ("win", "wkv"):
        return ref.at[idx]
    if kind == "wup":
        return ref.at[_up_slot(idx)]
    if kind == "w4":
        return ref.at[:, pl.ds(pl.multiple_of(idx * R_O, 16), R_O), :]
    if kind == "wdn":
        return ref.at[pl.ds(pl.multiple_of(_dn_row(idx), 16), R_DN), :]
    assert kind == "cv"
    return ref.at[idx]


_WHOLE = {"win": ((N_DEV, D, C_IN), BF), "wup": ((N_DEV, D, C_UP_P), BF), "wkv": ((N_DEV, D, C_KV), BF),
          "w4": ((4, D, D), BF), "wdn": ((FF_P, D), BF)}
_SHARD = {"win": (D, C_IN), "wup": (D, C_UP_P), "wkv": (D, C_KV), "w4": (4, R_O, D), "wdn": (R_DN, D)}
HBM_SPEC = pl.BlockSpec(memory_space=pltpu.HBM)
SEM_SPEC = pl.BlockSpec(memory_space=pltpu.SEMAPHORE)
_DATAFLOW = pltpu.SideEffectType.DATAFLOW_SIDE_EFFECTING


def _gather_maps(kinds):
    return ((lambda srcs, lands, a, idx: _slab(kinds[a], lands[a], _my_index())),
            (lambda lands, a, idx: _slab(kinds[a], lands[a], idx)))


def _scatter_maps(kinds):
    return ((lambda srcs, lands, a, idx: _slab(kinds[a], srcs[a], idx)),
            (lambda lands, a, idx: lands[a].at[idx]))


def _own_slab_blocks(kind, shard_shape):
    if kind == "w4":
        return (1, shard_shape, (lambda i, me: (0, me[0], 0)), (lambda i, me: (0, 0, 0)),
                (lambda i, me: (me[0], 0, 0, 0)))
    if kind == "wdn":
        rows = 32
        return (R_DN // rows, (rows, D), (lambda i, me: (_dn_row(me[0]) // rows + i, 0)), (lambda i, me: (i, 0)),
                (lambda i, me: (me[0], i, 0)))
    rows = min(256, shard_shape[0])
    slot = _up_slot if kind == "wup" else (lambda m: m)
    return (shard_shape[0] // rows, (rows, shard_shape[1]), (lambda i, me: (slot(me[0]), i, 0)),
            (lambda i, me: (i, 0)), (lambda i, me: (me[0], i, 0)))


def _place_own(kind, src, out_sds, gather, me_arr, *, name):
    shard_shape = src.shape if gather else out_sds.shape[1:]
    steps, blk, whole_idx, shard_idx, staging_idx = _own_slab_blocks(kind, shard_shape)
    slotted = kind not in ("w4", "wdn")
    whole_spec = pl.BlockSpec(((None,) if slotted else ()) + tuple(blk), whole_idx)
    if gather:
        in_spec, out_spec = pl.BlockSpec(tuple(blk), shard_idx), whole_spec
    else:
        in_spec, out_spec = whole_spec, pl.BlockSpec((None,) + tuple(blk), staging_idx)
    zero_init = gather and kind == "wdn"

    def body(me_ref, src_ref, *rest):
        rest[-1][...] = src_ref[...].astype(rest[-1].dtype)

    operands = (me_arr, src) + ((jnp.zeros(out_sds.shape, out_sds.dtype),) if zero_init else ())
    return _pcall(
        body,
        grid_spec=pltpu.PrefetchScalarGridSpec(
            num_scalar_prefetch=1, grid=(steps,), in_specs=[in_spec] + ([ANY] if zero_init else []),
            out_specs=out_spec),
        out_shape=out_sds, input_output_aliases={2: 0} if zero_init else {},
        compiler_params=_cp(("arbitrary",)), name=name)(*operands)


def _peer_copies(n, src_of, dst_of, src_r, land_r, ssem, rsem):
    me = _my_index()
    out = []
    for r, peer in _relations():
        p_idx = 4 * peer[0] + 2 * peer[1] + peer[2]
        for a in range(n):
            def copy(src_idx, dst_idx, a=a, r=r, peer=peer):
                sem = a * (N_DEV - 1) + r - 1
                return pltpu.make_async_remote_copy(
                    src_ref=src_of(src_r, land_r, a, src_idx), dst_ref=dst_of(land_r, a, dst_idx),
                    send_sem=ssem.at[sem], recv_sem=rsem.at[sem], device_id=peer, device_id_type=MESH)
            out.append((functools.partial(copy, p_idx, me), functools.partial(copy, me, p_idx)))
    return out


def _exchange_start(srcs, lands, maps, after, *, name):
    n, ns = len(lands), len(srcs)
    src_of, dst_of = maps

    def body(*refs):
        src_r, land_r = refs[:ns], refs[ns:ns + n]
        ssem, rsem, token = refs[ns + n + 1], refs[ns + n + 2], refs[-1]
        for send, _ in _peer_copies(n, src_of, dst_of, src_r, land_r, ssem, rsem):
            send().start()
        token[...] = jnp.zeros_like(token)

    flight = list(srcs) + list(lands)
    outs = pl.pallas_call(
        body, name=name,
        out_shape=(pltpu.SemaphoreType.DMA((n * (N_DEV - 1),)), pltpu.SemaphoreType.DMA((n * (N_DEV - 1),)),
                   *[pltpu.HBM(a.shape, a.dtype) for a in flight], SDS((8, LANE), F32)),
        in_specs=[HBM_SPEC] * (ns + n) + [ANY],
        out_specs=(SEM_SPEC, SEM_SPEC, *[HBM_SPEC] * (ns + n), pl.BlockSpec(memory_space=pltpu.VMEM)),
        input_output_aliases={i: 2 + i for i in range(ns + n)},
        compiler_params=pltpu.CompilerParams(has_side_effects=_DATAFLOW),
    )(*[pltpu.with_memory_space_constraint(a, pltpu.HBM) for a in flight], after)
    return (outs[0], outs[1], list(outs[2:2 + ns + n]), ns), outs[-1]


def _exchange_wait(handle, maps, after, *, name):
    ssem, rsem, flight, ns = handle
    n = len(flight) - ns
    src_of, dst_of = maps

    def body(*refs):
        src_r, land_r, ssem_r, rsem_r = refs[:ns], refs[ns:ns + n], refs[ns + n], refs[ns + n + 1]
        for send, arrival in _peer_copies(n, src_of, dst_of, src_r, land_r, ssem_r, rsem_r):
            send().wait_send()
            arrival().wait_recv()

    outs = pl.pallas_call(
        body, name=name, out_shape=[pltpu.HBM(a.shape, a.dtype) for a in flight],
        in_specs=[HBM_SPEC] * (ns + n) + [SEM_SPEC, SEM_SPEC, ANY], out_specs=[HBM_SPEC] * (ns + n),
        input_output_aliases={i: i for i in range(ns + n)},
        compiler_params=pltpu.CompilerParams(has_side_effects=_DATAFLOW),
    )(*flight, ssem, rsem, after)
    return list(outs[ns:])


def _allreduce_small(pack, *, name):
    rows = pack.shape[0]

    def body(p_ref, out_ref, gath, ssem, rsem, lsem):
        _exchange(1, lambda a, idx: p_ref, lambda a, idx: gath.at[idx], (ssem, rsem, lsem))
        total = gath[0]
        for d in range(1, N_DEV):
            total = total + gath[d]
        out_ref[...] = total

    vm = pl.BlockSpec(memory_space=pltpu.VMEM)
    return _pcall(
        body, in_specs=[vm], out_specs=vm, out_shape=SDS(pack.shape, F32),
        scratch_shapes=[pltpu.VMEM((N_DEV, rows, pack.shape[1]), F32)] + _sem_scratch(1),
        compiler_params=_cp(has_side_effects=True), name=name)(pack)


def _adam(g, w, m, v):
    nm = ADAM_B1 * m + (1.0 - ADAM_B1) * g
    nv = ADAM_B2 * v + (1.0 - ADAM_B2) * (g * g)
    m_hat = nm / (1.0 - ADAM_B1 ** ADAM_STEP)
    v_hat = nv / (1.0 - ADAM_B2 ** ADAM_STEP)
    return -ADAM_LR * (m_hat / (jnp.sqrt(v_hat) + ADAM_EPS) + ADAM_WD * w), nm, nv


def _adamw_staged(st0, st1, w, m, v, *, name):
    _, rows, cols = w.shape
    tr = max(t for t in range(16, 129, 16) if rows % t == 0)
    nr = rows // tr

    def body(s0_ref, s1_ref, w_ref, m_ref, v_ref, g_ref, d_ref, nm_ref, nv_ref):
        for layer, s_ref in enumerate((s0_ref, s1_ref)):
            @pl.when(pl.program_id(0) == layer)
            def _(s_ref=s_ref):
                total = s_ref[0].astype(F32)
                for dev in range(1, N_DEV):
                    total = total + s_ref[dev].astype(F32)
                g_ref[0] = total

        d_ref[0], nm_ref[0], nv_ref[0] = _adam(g_ref[0], w_ref[0], m_ref[0], v_ref[0])

    st_spec = lambda layer: pl.BlockSpec(
        (N_DEV, tr, cols), lambda l, i: (0, jnp.where(l == layer, i, (nr - 1) * (1 - layer)), 0))
    par = pl.BlockSpec((1, tr, cols), lambda l, i: (l, i, 0))
    return _pcall(
        body, grid=(DEPTH, nr), in_specs=[st_spec(0), st_spec(1), par, par, par], out_specs=[par] * 4,
        out_shape=[SDS(w.shape, F32)] * 4,
        compiler_params=_cp(("arbitrary", "arbitrary")), name=name)(st0, st1, w, m, v)


def _adamw_small(g, w, m, v, *, name):
    def body(g_ref, w_ref, m_ref, v_ref, d_ref, nm_ref, nv_ref):
        d_ref[...], nm_ref[...], nv_ref[...] = _adam(g_ref[...], w_ref[...], m_ref[...], v_ref[...])

    return _pcall(body, out_shape=[SDS(g.shape, F32)] * 3, compiler_params=_cp(), name=name)(g, w, m, v)


def _pack_rows(arrays):
    flat = jnp.concatenate([a.reshape(-1).astype(F32) for a in arrays])
    rows = -(-flat.shape[0] // (8 * D)) * 8
    return jnp.pad(flat, (0, rows * D - flat.shape[0])).reshape(rows, D)


def _unpack_rows(pack, like):
    flat = pack.reshape(-1)
    out, at = [], 0
    for a in like:
        out.append(flat[at:at + a.size].reshape(a.shape))
        at += a.size
    return out


def _layer_fwd(x, h, mem, win, rest_of_weights, small, g_next, tag):
    proj = _mm(h, win, b_slabs=True, tm=1024, name=f"proj_{tag}")
    wup, wkv, w4, wdn, cw_a, cw_b, cw_f = rest_of_weights(proj)
    za = _bra_fwd(proj, cw_a, name=f"bra_fwd_{tag}")
    cb = _brb_conv_fwd(proj, cw_b, small["conv_b_bias"], name=f"brb_conv_fwd_{tag}")
    sb = _ln_silu_fwd(cb, small["ln_b_g"], small["ln_b_b"], name=f"ln_silu_fwd_{tag}")
    memn, kv = _kv_prep(mem, small["norm_mem_g"], wkv, name=f"kv_prep_{tag}")
    o = _attn_fwd(proj, kv, name=f"attn_fwd_{tag}")
    ya, yb, yc, mg, x1, h2 = _mix_out(x, za, sb, o, proj, w4, small["b_gate"], small["norm_ffn_g"],
                                      name=f"mix_out_{tag}")
    u2 = _mm(h2, wup, b_slabs=True, tm=1024, name=f"up_{tag}")
    act = _ffn_act(u2, cw_f, name=f"ffn_act_{tag}")
    x2, h_next = _mm_res_norm(act, wdn, x1, g_next, name=f"down_{tag}")
    saved = dict(x=x, h=h, proj=proj, za=za, cb=cb, sb=sb, memn=memn, kv=kv, o=o, ya=ya, yb=yb, yc=yc,
                 mg=mg, x1=x1, h2=h2, u2=u2, act=act)
    return x2, h_next, (win, wup, wkv, w4, wdn, cw_a, cw_b, cw_f), saved


def _behind(operand, token):
    return operand + token[0:1, 0:1]


def _layer_bwd(dx2, dx2b, mem, wts, small, sv, start, tag):
    win, wup, wkv, w4, wdn, cw_a, cw_b, cw_f = wts
    dact = _mm(dx2b, wdn, tb=True, tm=1024, tn=768, name=f"d_act_{tag}")
    dwdn = _mm(sv["act"], dx2b, ta=True, tm=1536, tn=1024, tk=1024, name=f"dw_down_{tag}")
    du2, dcw_f = _ffn_bwd(sv["u2"], dact, cw_f, name=f"ffn_bwd_{tag}")
    dwup = _mm(sv["h2"], du2, ta=True, out_slabs=True, tm=1024, tn=C_UP_P, tk=1024, name=f"dw_up_{tag}")
    token = start(("wdn", "wup"), (dwdn, dwup), f"ffn_{tag}")
    dx1, dx1b, dg_ffn = _mm_nt_normbwd(du2, wup, sv["x1"], dx2, _behind(small["norm_ffn_g"], token),
                                       name=f"d_h2_{tag}")

    dya, dyb, dyc, dza, dsb, do, dgate, dbg = _mix_bwd(dx1b, sv["ya"], sv["yb"], sv["yc"], sv["proj"], w4,
                                                      small["b_gate"], name=f"mix_bwd_{tag}")
    dw4 = jnp.stack([
        _mm(a, b, ta=True, tm=1024, tn=1024, tk=1024, name=f"dw_{nm}_{tag}")
        for nm, a, b in (("a_out", sv["za"], dya), ("b_out", sv["sb"], dyb), ("att_out", sv["o"], dyc),
                         ("o", sv["mg"], dx1b))])
    dq, dk, dv = _attn_bwd(sv["proj"], sv["kv"], do, name=f"attn_bwd_{tag}")
    dwkv, dg_mem = _kv_bwd(mem, small["norm_mem_g"], sv["memn"], dk, dv, wkv, name=f"kv_bwd_{tag}")
    token = start(("w4", "wkv"), (dw4, dwkv), f"mix_{tag}")
    d_a, dcw_a = _bra_bwd(sv["proj"], dza, _behind(cw_a, token), name=f"bra_bwd_{tag}")
    dcb, ln_sums = _ln_silu_bwd(sv["cb"], dsb, small["ln_b_g"], small["ln_b_b"], name=f"ln_silu_bwd_{tag}")
    d_b, dcw_b = _brb_conv_bwd(sv["proj"], dcb, cw_b, name=f"brb_conv_bwd_{tag}")
    dproj = jnp.concatenate([d_a, d_b, dq, dgate], axis=1)
    dwin = _mm(sv["h"], dproj, ta=True, out_slabs=True, tm=1024, tn=C_IN, tk=1024, name=f"dw_in_{tag}")
    token = start(("win",), (dwin,), f"in_{tag}")
    dx, dxb, dg_mix = _mm_nt_normbwd(dproj, win, sv["x"], dx1, _behind(small["norm_mix_g"], token),
                                     name=f"d_h_{tag}")

    small_grads = [dg_mix[0:1], dg_mem[0:1], dbg[0:1].reshape(3, D), ln_sums[2:3], ln_sums[0:1], ln_sums[1:2],
                   dg_ffn[0:1], dcw_a[0:K_A], dcw_b[0:K_B], dcw_f[0:K_F].reshape(K_F * 2 * FF_P // D, D)]
    return dx, dxb, small_grads, token


_SMALL_ROWS = (1, 1, 3, 1, 1, 1, 1, K_A, K_B, K_F * 2 * FF_P // D)
_CV_ROWS = 48


def kernel(x, mem, norm_mix_g, norm_mem_g, w_in, b_gate, conv_a_w, w_a_out, conv_b_w, conv_b_bias, ln_b_g, ln_b_b, w_b_out, w_kv, w_att_out, w_o, norm_ffn_g, w_up, conv_ffn_w, w_down, norm_final_g, loss_target, m_norm_mix_g, m_norm_mem_g, m_w_in, m_b_gate, m_conv_a_w, m_w_a_out, m_conv_b_w, m_conv_b_bias, m_ln_b_g, m_ln_b_b, m_w_b_out, m_w_kv, m_w_att_out, m_w_o, m_norm_ffn_g, m_w_up, m_conv_ffn_w, m_w_down, m_norm_final_g, v_norm_mix_g, v_norm_mem_g, v_w_in, v_b_gate, v_conv_a_w, v_w_a_out, v_conv_b_w, v_conv_b_bias, v_ln_b_g, v_ln_b_b, v_w_b_out, v_w_kv, v_w_att_out, v_w_o, v_norm_ffn_g, v_w_up, v_conv_ffn_w, v_w_down, v_norm_final_g):
    me = _my_index()
    me_arr = me.astype(jnp.int32).reshape(1)
    x0, mem0, tgt = x.reshape(x.shape[1:]), mem.reshape(mem.shape[1:]), loss_target.reshape(x.shape[1:])
    up_pad = ((0, 0), (0, 0), (0, C_UP_P - C_UP))

    ag_groups = (("win",), ("wup", "wkv", "w4", "wdn", "cv"))
    kinds = ag_groups[0] + ag_groups[1]
    smalls, ag_handles = [], []
    token = jnp.zeros((8, LANE), F32)
    for l in range(DEPTH):
        cv = jnp.zeros((_CV_ROWS, C_UP_P), F32)
        cv = cv.at[0:K_F, 0:C_UP].set(conv_ffn_w[l]).at[3:3 + K_A, 0:R_O].set(conv_a_w[l])
        cv = cv.at[8:8 + K_B, 0:R_O].set(conv_b_w[l])
        shards = dict(
            win=w_in[l], wup=jnp.pad(w_up[l], up_pad[1:]), wkv=w_kv[l],
            w4=jnp.stack([w_a_out[l], w_b_out[l], w_att_out[l], w_o[l]]), wdn=w_down[l], cv=cv)
        whole = dict({k: SDS(*_WHOLE[k]) for k in kinds[:-1]}, cv=SDS((N_DEV,) + cv.shape, F32))
        lands = {k: _place_own(k, shards[k], whole[k], True, me_arr, name=f"ag_own_{k}_l{l}") for k in kinds}
        per_layer = []
        for g, grp in enumerate(ag_groups):
            handle, token = _exchange_start([], [lands[k] for k in grp], _gather_maps(grp), token,
                                            name=f"ag_start_l{l}_g{g}")
            per_layer.append(handle)
        ag_handles.append(per_layer)
        smalls.append(dict(
            norm_mix_g=norm_mix_g[l][None], norm_mem_g=norm_mem_g[l][None], b_gate=b_gate[l][None],
            conv_b_bias=conv_b_bias[l][None], ln_b_g=ln_b_g[l][None], ln_b_b=ln_b_b[l][None],
            norm_ffn_g=norm_ffn_g[l][None]))

    def rest_of_weights(l):
        def wait(after):
            wup, wkv, w4, wdn, cvg = _exchange_wait(ag_handles[l][1], _gather_maps(ag_groups[1]), after,
                                                    name=f"ag_wait_l{l}_g1")
            cw_f = jnp.stack([cvg[d, 0:K_F, :] for d in UP_ORDER], axis=1).reshape(K_F, 2 * FF_P)
            cw_a = cvg[:, 3:3 + K_A, 0:R_O].transpose(1, 0, 2).reshape(K_A, D)
            cw_b = cvg[:, 8:8 + K_B, 0:R_O].transpose(1, 0, 2).reshape(K_B, D)
            return (wup, wkv, w4, wdn, jnp.pad(cw_a, ((0, 8 - K_A), (0, 0))),
                    jnp.pad(cw_b, ((0, 32 - K_B), (0, 0))), jnp.pad(cw_f, ((0, 8 - K_F), (0, 0))))
        return wait

    wts, saved = [], []
    xs = x0
    h = _rms_fwd(xs, smalls[0]["norm_mix_g"], name="rms_fwd")
    behind = token
    for l in range(DEPTH):
        g_next = smalls[l + 1]["norm_mix_g"] if l + 1 < DEPTH else norm_final_g[None]
        (win,) = _exchange_wait(ag_handles[l][0], _gather_maps(ag_groups[0]), behind, name=f"ag_wait_l{l}_g0")
        xs, h, w_l, sv = _layer_fwd(xs, h, mem0, win, rest_of_weights(l), smalls[l], g_next, f"l{l}")
        behind = h
        wts.append(w_l)
        saved.append(sv)
    dx, dxb, head_sums = _loss_head(xs, tgt, norm_final_g[None], name="loss_head")

    rs_handles = []
    small_grads = [None] * DEPTH

    def start_scatter(grp, arrays, name):
        maps = _scatter_maps(grp)
        lands = [_place_own(k, a, SDS((N_DEV,) + _SHARD[k], BF), False, me_arr, name=f"rs_own_{k}_{name}")
                 for k, a in zip(grp, arrays)]
        handle, tok = _exchange_start(list(arrays), lands, maps, rs_handles[-1][2] if rs_handles else head_sums,
                                      name=f"rs_start_{name}")
        rs_handles.append((grp, handle, tok, name))
        return tok

    for l in reversed(range(DEPTH)):
        dx, dxb, small_grads[l], token = _layer_bwd(dx, dxb, mem0, wts[l], smalls[l], saved[l], start_scatter,
                                                    f"l{l}")

    staged = [dict() for _ in range(DEPTH)]
    for grp, handle, _, name in rs_handles[:-1]:
        staged[int(name[-1])].update(zip(grp, _exchange_wait(handle, _scatter_maps(grp), dx, name=f"rs_wait_{name}")))

    pack = jnp.concatenate(small_grads[0] + small_grads[1] + [head_sums[1:2], head_sums[0:1]], axis=0)
    pack = jnp.pad(pack, ((0, -pack.shape[0] % 8), (0, 0)))
    total = _allreduce_small(pack, name="allreduce_small")
    per_layer = sum(_SMALL_ROWS)
    parts = []
    for l in range(DEPTH):
        at, one = l * per_layer, []
        for rows in _SMALL_ROWS:
            one.append(total[at:at + rows])
            at += rows
        parts.append(one)
    g_final = total[DEPTH * per_layer]
    loss = 0.5 / D * jnp.sum(total[DEPTH * per_layer + 1])

    def both(i):
        return jnp.stack([parts[0][i], parts[1][i]])

    g_norm_mix, g_norm_mem = both(0)[:, 0], both(1)[:, 0]
    g_b_gate = both(2).reshape(DEPTH, 3 * D)
    g_cbias, g_lng, g_lnb, g_norm_ffn = both(3)[:, 0], both(4)[:, 0], both(5)[:, 0], both(6)[:, 0]
    g_conv_a = lax.dynamic_slice_in_dim(both(7), me * R_O, R_O, axis=2)
    g_conv_b = lax.dynamic_slice_in_dim(both(8), me * R_O, R_O, axis=2)
    g_conv_f = lax.dynamic_slice_in_dim(both(9).reshape(DEPTH, K_F, 2 * FF_P), _up_slot(me) * C_UP_P, C_UP, axis=2)

    small_g = [g_norm_mix, g_norm_mem, g_b_gate, g_conv_a, g_conv_b, g_cbias, g_lng, g_lnb, g_norm_ffn, g_conv_f,
               g_final]
    small_w = [norm_mix_g, norm_mem_g, b_gate, conv_a_w, conv_b_w, conv_b_bias, ln_b_g, ln_b_b, norm_ffn_g,
               conv_ffn_w, norm_final_g]
    small_m = [m_norm_mix_g, m_norm_mem_g, m_b_gate, m_conv_a_w, m_conv_b_w, m_conv_b_bias, m_ln_b_g, m_ln_b_b,
               m_norm_ffn_g, m_conv_ffn_w, m_norm_final_g]
    small_v = [v_norm_mix_g, v_norm_mem_g, v_b_gate, v_conv_a_w, v_conv_b_w, v_conv_b_bias, v_ln_b_g, v_ln_b_b,
               v_norm_ffn_g, v_conv_ffn_w, v_norm_final_g]
    upd = _adamw_small(_pack_rows(small_g), _pack_rows(small_w), _pack_rows(small_m), _pack_rows(small_v),
                       name="adamw_small")
    s_d, s_m, s_v = (_unpack_rows(p, small_w) for p in upd)
    (d_norm_mix, d_norm_mem, d_b_gate, d_conv_a, d_conv_b, d_cbias, d_lng, d_lnb, d_norm_ffn, d_conv_f,
     d_final) = s_d
    (nm_norm_mix, nm_norm_mem, nm_b_gate, nm_conv_a, nm_conv_b, nm_cbias, nm_lng, nm_lnb, nm_norm_ffn, nm_conv_f,
     nm_final) = s_m
    (nv_norm_mix, nv_norm_mem, nv_b_gate, nv_conv_a, nv_conv_b, nv_cbias, nv_lng, nv_lnb, nv_norm_ffn, nv_conv_f,
     nv_final) = s_v

    def big_update(kind, w, m, v, name):
        return _adamw_staged(staged[0][kind], staged[1][kind], w, m, v, name=name)

    r_up = [a[:, :, 0:C_UP] for a in big_update("wup", jnp.pad(w_up, up_pad), jnp.pad(m_w_up, up_pad),
                                                jnp.pad(v_w_up, up_pad), "adamw_w_up")]
    r_kv = big_update("wkv", w_kv, m_w_kv, v_w_kv, "adamw_w_kv")
    r_dn = big_update("wdn", w_down, m_w_down, v_w_down, "adamw_w_down")

    def four(a, b, c, d_):
        return jnp.stack([a, b, c, d_], axis=1).reshape(DEPTH, 4 * R_O, D)

    r_4 = _adamw_staged(
        staged[0]["w4"].reshape(N_DEV, 4 * R_O, D), staged[1]["w4"].reshape(N_DEV, 4 * R_O, D),
        four(w_a_out, w_b_out, w_att_out, w_o), four(m_w_a_out, m_w_b_out, m_w_att_out, m_w_o),
        four(v_w_a_out, v_w_b_out, v_w_att_out, v_w_o), name="adamw_w_out")
    grp, handle, _, name = rs_handles[-1]
    staged[0].update(zip(grp, _exchange_wait(handle, _scatter_maps(grp), r_4[0], name=f"rs_wait_{name}")))
    r_in = big_update("win", w_in, m_w_in, v_w_in, "adamw_w_in")
    r_a, r_b, r_att, r_o = ([a.reshape(DEPTH, 4, R_O, D)[:, j] for a in r_4] for j in range(4))

    grads = [g_norm_mix, g_norm_mem, r_in[0], g_b_gate, g_conv_a, r_a[0], g_conv_b, g_cbias, g_lng, g_lnb, r_b[0],
             r_kv[0], r_att[0], r_o[0], g_norm_ffn, r_up[0], g_conv_f, r_dn[0], g_final]
    deltas = [d_norm_mix, d_norm_mem, r_in[1], d_b_gate, d_conv_a, r_a[1], d_conv_b, d_cbias, d_lng, d_lnb, r_b[1],
              r_kv[1], r_att[1], r_o[1], d_norm_ffn, r_up[1], d_conv_f, r_dn[1], d_final]
    new_m = [nm_norm_mix, nm_norm_mem, r_in[2], nm_b_gate, nm_conv_a, r_a[2], nm_conv_b, nm_cbias, nm_lng, nm_lnb,
             r_b[2], r_kv[2], r_att[2], r_o[2], nm_norm_ffn, r_up[2], nm_conv_f, r_dn[2], nm_final]
    new_v = [nv_norm_mix, nv_norm_mem, r_in[3], nv_b_gate, nv_conv_a, r_a[3], nv_conv_b, nv_cbias, nv_lng, nv_lnb,
             r_b[3], r_kv[3], r_att[3], r_o[3], nv_norm_ffn, r_up[3], nv_conv_f, r_dn[3], nv_final]
    return (loss, dx[None], *grads, *deltas, *new_m, *new_v)
```

```python
import functools

import jax
import jax.numpy as jnp
from jax import lax
from jax.experimental import pallas as pl
from jax.experimental.pallas import tpu as pltpu

F32 = jnp.float32
BF = jnp.bfloat16
SDS = jax.ShapeDtypeStruct
MESH = pl.DeviceIdType.MESH
ANY = pl.BlockSpec(memory_space=pl.ANY)

N_DEV = 8
DEPTH = 2
D = 1024
N_HEADS = 4
HEAD = D // N_HEADS
D_FF = 2816
K_A, K_B, K_F = 3, 31, 3
NORM_EPS = 1e-6

C_IN = 9 * D // N_DEV
C_KV = 2 * D // N_DEV
C_UP = 2 * D_FF // N_DEV
LANE = 128
C_UP_P = -(-C_UP // LANE) * LANE
FF_P = 4 * C_UP_P
R_O = D // N_DEV
R_DN = D_FF // N_DEV

VMEM_LIMIT = 56 * 1024 * 1024
TM = 512
TR = 256
H_S, H_L = 16, 32

ADAM_LR, ADAM_B1, ADAM_B2, ADAM_EPS, ADAM_WD, ADAM_STEP = 0.001, 0.9, 0.999, 1e-08, 0.01, 10

UP_ORDER = (0, 4, 1, 5, 2, 6, 3, 7)


def _pcall(body, **kw):
    return pl.pallas_call(body, **kw)


def _cp(sem=None, **kw):
    return pltpu.CompilerParams(dimension_semantics=sem, vmem_limit_bytes=VMEM_LIMIT, **kw)


def _dot(a, b):
    return jnp.dot(a, b, preferred_element_type=F32)


def _dot_nt(a, b):
    return lax.dot_general(a, b, (((1,), (1,)), ((), ())), preferred_element_type=F32)


def _dot_tn(a, b):
    return lax.dot_general(a, b, (((0,), (0,)), ((), ())), preferred_element_type=F32)


def _sigmoid(z):
    return 1.0 / (1.0 + jnp.exp(-z))


def _rms(xv):
    return lax.rsqrt(jnp.mean(xv * xv, axis=-1, keepdims=True) + NORM_EPS)


def _up_slot(idx):
    return jnp.where(idx < 4, 2 * idx, 2 * (idx - 4) + 1)


def _dn_row(idx):
    return C_UP_P * (idx // 2) + R_DN * (idx % 2)


def _mm(a, b, *, ta=False, tb=False, out_dtype=BF, tm=TM, tn=512, tk=None, name):
    m, k_dim = (a.shape[1], a.shape[0]) if ta else a.shape
    n = b.shape[0] if tb else b.shape[1]
    tm, tn = min(tm, m), min(tn, n)
    tk = k_dim if tk is None else min(tk, k_dim)
    nk = k_dim // tk
    assert m % tm == 0 and n % tn == 0 and k_dim % tk == 0
    dims = (((0 if ta else 1,), (1 if tb else 0,)), ((), ()))

    def body(a_ref, b_ref, o_ref, *scratch):
        part = lax.dot_general(a_ref[...], b_ref[...], dims, preferred_element_type=F32)
        if nk == 1:
            o_ref[...] = part.astype(o_ref.dtype)
            return
        acc = scratch[0]
        k = pl.program_id(2)

        @pl.when(k == 0)
        def _():
            acc[...] = part

        @pl.when(k > 0)
        def _():
            acc[...] += part

        @pl.when(k == nk - 1)
        def _():
            o_ref[...] = acc[...].astype(o_ref.dtype)

    a_spec = pl.BlockSpec((tk, tm), lambda i, j, k: (k, i)) if ta else pl.BlockSpec((tm, tk), lambda i, j, k: (i, k))
    b_spec = pl.BlockSpec((tn, tk), lambda i, j, k: (j, k)) if tb else pl.BlockSpec((tk, tn), lambda i, j, k: (k, j))
    return _pcall(
        body, grid=(m // tm, n // tn, nk), in_specs=[a_spec, b_spec],
        out_specs=pl.BlockSpec((tm, tn), lambda i, j, k: (i, j)),
        out_shape=SDS((m, n), out_dtype),
        scratch_shapes=[pltpu.VMEM((tm, tn), F32)] if nk > 1 else [],
        compiler_params=_cp(("parallel", "parallel", "arbitrary")), name=name)(a, b)


def _mm_res_norm(a, w, x, g, *, name):
    s, k_dim = a.shape
    tm = min(TM, s)

    def body(a_ref, w_ref, x_ref, g_ref, xo_ref, h_ref):
        xo = x_ref[...] + _dot(a_ref[...], w_ref[...])
        xo_ref[...] = xo
        h_ref[...] = ((xo * _rms(xo)) * g_ref[...]).astype(BF)

    return _pcall(
        body, grid=(s // tm,),
        in_specs=[pl.BlockSpec((tm, k_dim), lambda i: (i, 0)), pl.BlockSpec((k_dim, D), lambda i: (0, 0)),
                  pl.BlockSpec((tm, D), lambda i: (i, 0)), pl.BlockSpec((1, D), lambda i: (0, 0))],
        out_specs=[pl.BlockSpec((tm, D), lambda i: (i, 0))] * 2,
        out_shape=[SDS((s, D), F32), SDS((s, D), BF)],
        compiler_params=_cp(("parallel",)), name=name)(a, w, x, g)


def _mm_nt_normbwd(da, w, x, dres, g, *, tk, name):
    s, k_dim = da.shape
    tm = min(TM, s)
    nk = k_dim // tk
    assert k_dim % tk == 0

    def body(da_ref, w_ref, x_ref, dres_ref, g_ref, dx_ref, dxb_ref, dg_ref, acc):
        i, k = pl.program_id(0), pl.program_id(1)
        part = _dot_nt(da_ref[...], w_ref[...])

        @pl.when(k == 0)
        def _():
            acc[...] = part

        @pl.when(k > 0)
        def _():
            acc[...] += part

        @pl.when((i == 0) & (k == 0))
        def _():
            dg_ref[...] = jnp.zeros_like(dg_ref)

        @pl.when(k == nk - 1)
        def _():
            dh = acc[...]
            xv = x_ref[...]
            r = _rms(xv)
            xn = xv * r
            dg_ref[0:1, :] += jnp.sum(dh * xn, axis=0, keepdims=True)
            dxn = dh * g_ref[...]
            dx = dres_ref[...] + r * (dxn - xn * jnp.mean(dxn * xn, axis=-1, keepdims=True))
            dx_ref[...] = dx
            dxb_ref[...] = dx.astype(BF)

    row = lambda i, k: (i, 0)
    return _pcall(
        body, grid=(s // tm, nk),
        in_specs=[pl.BlockSpec((tm, tk), lambda i, k: (i, k)), pl.BlockSpec((D, tk), lambda i, k: (0, k)),
                  pl.BlockSpec((tm, D), row), pl.BlockSpec((tm, D), row), pl.BlockSpec((1, D), lambda i, k: (0, 0))],
        out_specs=[pl.BlockSpec((tm, D), row), pl.BlockSpec((tm, D), row), pl.BlockSpec((8, D), lambda i, k: (0, 0))],
        out_shape=[SDS((s, D), F32), SDS((s, D), BF), SDS((8, D), F32)],
        scratch_shapes=[pltpu.VMEM((tm, D), F32)],
        compiler_params=_cp(("arbitrary", "arbitrary")), name=name)(da, w, x, dres, g)


def _rms_fwd(x, g, *, name):
    s = x.shape[0]
    tm = min(TM, s)

    def body(x_ref, g_ref, h_ref):
        xv = x_ref[...]
        h_ref[...] = ((xv * _rms(xv)) * g_ref[...]).astype(BF)

    return _pcall(
        body, grid=(s // tm,),
        in_specs=[pl.BlockSpec((tm, D), lambda i: (i, 0)), pl.BlockSpec((1, D), lambda i: (0, 0))],
        out_specs=pl.BlockSpec((tm, D), lambda i: (i, 0)), out_shape=SDS((s, D), BF),
        compiler_params=_cp(("parallel",)), name=name)(x, g)


def _loss_head(x, tgt, g, *, name):
    s = x.shape[0]
    tm = min(TM, s)

    def body(x_ref, t_ref, g_ref, dx_ref, dxb_ref, sums_ref):
        @pl.when(pl.program_id(0) == 0)
        def _():
            sums_ref[...] = jnp.zeros_like(sums_ref)

        xv = x_ref[...]
        r = _rms(xv)
        xn = xv * r
        diff = xn * g_ref[...] - t_ref[...]
        sums_ref[0:1, :] += jnp.sum(diff * diff, axis=0, keepdims=True)
        dy = diff * (1.0 / D)
        sums_ref[1:2, :] += jnp.sum(dy * xn, axis=0, keepdims=True)
        dxn = dy * g_ref[...]
        dx = r * (dxn - xn * jnp.mean(dxn * xn, axis=-1, keepdims=True))
        dx_ref[...] = dx
        dxb_ref[...] = dx.astype(BF)

    row = lambda i: (i, 0)
    return _pcall(
        body, grid=(s // tm,),
        in_specs=[pl.BlockSpec((tm, D), row), pl.BlockSpec((tm, D), row), pl.BlockSpec((1, D), lambda i: (0, 0))],
        out_specs=[pl.BlockSpec((tm, D), row), pl.BlockSpec((tm, D), row), pl.BlockSpec((8, D), lambda i: (0, 0))],
        out_shape=[SDS((s, D), F32), SDS((s, D), BF), SDS((8, D), F32)],
        compiler_params=_cp(("arbitrary",)), name=name)(x, tgt, g)


def _halo_before(i, tr, h):
    return jnp.maximum(i * (tr // h) - 1, 0)


def _halo_after(i, tr, h, s):
    return jnp.minimum((i + 1) * (tr // h), s // h - 1)


def _taps(buf, w_ref, sl, k_w, base, rows):
    acc = None
    for k in range(k_w):
        t = w_ref[k:k + 1, sl] * buf[base + k:base + k + rows, sl]
        acc = t if acc is None else acc + t
    return acc


def _taps_rev(buf, w_ref, sl, k_w, rows):
    acc = None
    for k in range(k_w):
        t = w_ref[k:k + 1, sl] * buf[k_w - 1 - k:k_w - 1 - k + rows, sl]
        acc = t if acc is None else acc + t
    return acc


def _tap_grads(dw_ref, dc, buf, sl, k_w, base, rows):
    for k in range(k_w):
        dw_ref[k:k + 1, sl] += jnp.sum(dc * buf[base + k:base + k + rows, sl], axis=0, keepdims=True)


def _bra_fwd(proj, cw, *, name):
    s = proj.shape[0]
    tr, h = min(TR, s), H_S

    def body(cur, halo, w_ref, za_ref, cvb):
        i = pl.program_id(0)
        hv = halo[:, D:2 * D].astype(F32) * halo[:, 2 * D:3 * D].astype(F32)
        cvb[0:h, :] = jnp.where(i == 0, 0.0, hv)
        cvb[h:h + tr, :] = cur[:, D:2 * D].astype(F32) * cur[:, 2 * D:3 * D].astype(F32)
        for c in range(D // LANE):
            sl = slice(LANE * c, LANE * c + LANE)
            ca = _taps(cvb, w_ref, sl, K_A, h - (K_A - 1), tr)
            za_ref[:, sl] = (cur[:, sl].astype(F32) * ca).astype(BF)

    return _pcall(
        body, grid=(s // tr,),
        in_specs=[pl.BlockSpec((tr, 3 * D), lambda i: (i, 0)),
                  pl.BlockSpec((h, 3 * D), lambda i: (_halo_before(i, tr, h), 0)),
                  pl.BlockSpec((8, D), lambda i: (0, 0))],
        out_specs=pl.BlockSpec((tr, D), lambda i: (i, 0)), out_shape=SDS((s, D), BF),
        scratch_shapes=[pltpu.VMEM((h + tr, D), F32)],
        compiler_params=_cp(("parallel",)), name=name)(proj, proj, cw)


def _bra_bwd(proj, dza, cw, *, name):
    s = proj.shape[0]
    tr, h = min(TR, s), H_S
    n = s // tr

    def body(before, cur, after, dz_cur, dz_after, w_ref, da_ref, dw_ref, cvb, dcab):
        i = pl.program_id(0)

        @pl.when(i == 0)
        def _():
            dw_ref[...] = jnp.zeros_like(dw_ref)

        first, last = i == 0, i == n - 1
        cvb[0:h, :] = jnp.where(first, 0.0, before[:, D:2 * D].astype(F32) * before[:, 2 * D:3 * D].astype(F32))
        cvb[h:h + tr, :] = cur[:, D:2 * D].astype(F32) * cur[:, 2 * D:3 * D].astype(F32)
        dcab[0:tr, :] = dz_cur[...].astype(F32) * cur[:, 0:D].astype(F32)
        dcab[tr:tr + h, :] = jnp.where(last, 0.0, dz_after[...].astype(F32) * after[:, 0:D].astype(F32))
        for c in range(D // LANE):
            sl = slice(LANE * c, LANE * c + LANE)
            gc = cur[:, D + LANE * c:D + LANE * c + LANE].astype(F32)
            v = cur[:, 2 * D + LANE * c:2 * D + LANE * c + LANE].astype(F32)
            ca = _taps(cvb, w_ref, sl, K_A, h - (K_A - 1), tr)
            da_ref[:, sl] = (dz_cur[:, sl].astype(F32) * ca).astype(BF)
            dcv = _taps_rev(dcab, w_ref, sl, K_A, tr)
            da_ref[:, D + LANE * c:D + LANE * c + LANE] = (dcv * v).astype(BF)
            da_ref[:, 2 * D + LANE * c:2 * D + LANE * c + LANE] = (dcv * gc).astype(BF)
            _tap_grads(dw_ref, dcab[0:tr, sl], cvb, sl, K_A, h - (K_A - 1), tr)

    return _pcall(
        body, grid=(n,),
        in_specs=[pl.BlockSpec((h, 3 * D), lambda i: (_halo_before(i, tr, h), 0)),
                  pl.BlockSpec((tr, 3 * D), lambda i: (i, 0)),
                  pl.BlockSpec((h, 3 * D), lambda i: (_halo_after(i, tr, h, s), 0)),
                  pl.BlockSpec((tr, D), lambda i: (i, 0)),
                  pl.BlockSpec((h, D), lambda i: (_halo_after(i, tr, h, s), 0)),
                  pl.BlockSpec((8, D), lambda i: (0, 0))],
        out_specs=[pl.BlockSpec((tr, 3 * D), lambda i: (i, 0)), pl.BlockSpec((8, D), lambda i: (0, 0))],
        out_shape=[SDS((s, 3 * D), BF), SDS((8, D), F32)],
        scratch_shapes=[pltpu.VMEM((h + tr, D), F32), pltpu.VMEM((tr + h, D), F32)],
        compiler_params=_cp(("arbitrary",)), name=name)(proj, proj, proj, dza, dza, cw)


_U_COL, _UG_COL = 3, 4


def _brb_conv_fwd(proj, cw, bias, *, name):
    s = proj.shape[0]
    tr, h = min(TR, s), H_L

    def body(u_cur, ug_cur, u_halo, ug_halo, w_ref, b_ref, cb_ref, glb):
        i = pl.program_id(0)
        glb[0:h, :] = jnp.where(i == 0, 0.0, u_halo[...].astype(F32) * _sigmoid(ug_halo[...].astype(F32)))
        glb[h:h + tr, :] = u_cur[...].astype(F32) * _sigmoid(ug_cur[...].astype(F32))
        for c in range(D // LANE):
            sl = slice(LANE * c, LANE * c + LANE)
            cb_ref[:, sl] = (_taps(glb, w_ref, sl, K_B, h - (K_B - 1), tr) + b_ref[:, sl]).astype(BF)

    return _pcall(
        body, grid=(s // tr,),
        in_specs=[pl.BlockSpec((tr, D), lambda i: (i, _U_COL)), pl.BlockSpec((tr, D), lambda i: (i, _UG_COL)),
                  pl.BlockSpec((h, D), lambda i: (_halo_before(i, tr, h), _U_COL)),
                  pl.BlockSpec((h, D), lambda i: (_halo_before(i, tr, h), _UG_COL)),
                  pl.BlockSpec((32, D), lambda i: (0, 0)), pl.BlockSpec((1, D), lambda i: (0, 0))],
        out_specs=pl.BlockSpec((tr, D), lambda i: (i, 0)), out_shape=SDS((s, D), BF),
        scratch_shapes=[pltpu.VMEM((h + tr, D), F32)],
        compiler_params=_cp(("parallel",)), name=name)(proj, proj, proj, proj, cw, bias)


def _brb_conv_bwd(proj, dcb, cw, *, name):
    s = proj.shape[0]
    tr, h = min(TR, s), H_L
    n = s // tr

    def body(u_before, ug_before, u_cur, ug_cur, d_cur, d_after, w_ref, db_ref, dw_ref, glb, dcbb):
        i = pl.program_id(0)

        @pl.when(i == 0)
        def _():
            dw_ref[...] = jnp.zeros_like(dw_ref)

        glb[0:h, :] = jnp.where(i == 0, 0.0, u_before[...].astype(F32) * _sigmoid(ug_before[...].astype(F32)))
        glb[h:h + tr, :] = u_cur[...].astype(F32) * _sigmoid(ug_cur[...].astype(F32))
        dcbb[0:tr, :] = d_cur[...].astype(F32)
        dcbb[tr:tr + h, :] = jnp.where(i == n - 1, 0.0, d_after[...].astype(F32))
        for c in range(D // LANE):
            sl = slice(LANE * c, LANE * c + LANE)
            dglu = _taps_rev(dcbb, w_ref, sl, K_B, tr)
            u = u_cur[:, sl].astype(F32)
            sg = _sigmoid(ug_cur[:, sl].astype(F32))
            db_ref[:, sl] = (dglu * sg).astype(BF)
            db_ref[:, D + LANE * c:D + LANE * c + LANE] = (dglu * u * sg * (1.0 - sg)).astype(BF)
            _tap_grads(dw_ref, dcbb[0:tr, sl], glb, sl, K_B, h - (K_B - 1), tr)

    return _pcall(
        body, grid=(n,),
        in_specs=[pl.BlockSpec((h, D), lambda i: (_halo_before(i, tr, h), _U_COL)),
                  pl.BlockSpec((h, D), lambda i: (_halo_before(i, tr, h), _UG_COL)),
                  pl.BlockSpec((tr, D), lambda i: (i, _U_COL)), pl.BlockSpec((tr, D), lambda i: (i, _UG_COL)),
                  pl.BlockSpec((tr, D), lambda i: (i, 0)),
                  pl.BlockSpec((h, D), lambda i: (_halo_after(i, tr, h, s), 0)),
                  pl.BlockSpec((32, D), lambda i: (0, 0))],
        out_specs=[pl.BlockSpec((tr, 2 * D), lambda i: (i, 0)), pl.BlockSpec((32, D), lambda i: (0, 0))],
        out_shape=[SDS((s, 2 * D), BF), SDS((32, D), F32)],
        scratch_shapes=[pltpu.VMEM((h + tr, D), F32), pltpu.VMEM((tr + h, D), F32)],
        compiler_params=_cp(("arbitrary",)), name=name)(proj, proj, proj, proj, dcb, dcb, cw)


def _ln_silu_fwd(cb, g, b, *, name):
    s = cb.shape[0]
    tm = min(TM, s)

    def body(cb_ref, g_ref, b_ref, sb_ref):
        z = cb_ref[...].astype(F32)
        zc = z - jnp.mean(z, axis=-1, keepdims=True)
        ln = (zc * lax.rsqrt(jnp.mean(zc * zc, axis=-1, keepdims=True) + NORM_EPS)) * g_ref[...] + b_ref[...]
        sb_ref[...] = (ln * _sigmoid(ln)).astype(BF)

    row = lambda i: (i, 0)
    vec = pl.BlockSpec((1, D), lambda i: (0, 0))
    return _pcall(
        body, grid=(s // tm,), in_specs=[pl.BlockSpec((tm, D), row), vec, vec],
        out_specs=pl.BlockSpec((tm, D), row), out_shape=SDS((s, D), BF),
        compiler_params=_cp(("parallel",)), name=name)(cb, g, b)


def _ln_silu_bwd(cb, dsb, g, b, *, name):
    s = cb.shape[0]
    tm = min(TM, s)

    def body(cb_ref, dsb_ref, g_ref, b_ref, dcb_ref, sums_ref):
        @pl.when(pl.program_id(0) == 0)
        def _():
            sums_ref[...] = jnp.zeros_like(sums_ref)

        z = cb_ref[...].astype(F32)
        zc = z - jnp.mean(z, axis=-1, keepdims=True)
        rstd = lax.rsqrt(jnp.mean(zc * zc, axis=-1, keepdims=True) + NORM_EPS)
        lnh = zc * rstd
        ln = lnh * g_ref[...] + b_ref[...]
        sg = _sigmoid(ln)
        dln = dsb_ref[...].astype(F32) * (sg * (1.0 + ln * (1.0 - sg)))
        sums_ref[0:1, :] += jnp.sum(dln * lnh, axis=0, keepdims=True)
        sums_ref[1:2, :] += jnp.sum(dln, axis=0, keepdims=True)
        dlnh = dln * g_ref[...]
        dz = rstd * (dlnh - jnp.mean(dlnh, axis=-1, keepdims=True)
                     - lnh * jnp.mean(dlnh * lnh, axis=-1, keepdims=True))
        sums_ref[2:3, :] += jnp.sum(dz, axis=0, keepdims=True)
        dcb_ref[...] = dz.astype(BF)

    row = lambda i: (i, 0)
    vec = pl.BlockSpec((1, D), lambda i: (0, 0))
    return _pcall(
        body, grid=(s // tm,), in_specs=[pl.BlockSpec((tm, D), row), pl.BlockSpec((tm, D), row), vec, vec],
        out_specs=[pl.BlockSpec((tm, D), row), pl.BlockSpec((8, D), lambda i: (0, 0))],
        out_shape=[SDS((s, D), BF), SDS((8, D), F32)],
        compiler_params=_cp(("arbitrary",)), name=name)(cb, dsb, g, b)


_Q_COL = 5 * D // HEAD


def _kv_prep(mem, g, wkv, *, name):
    m = mem.shape[0]

    def body(mem_ref, g_ref, w_ref, memn_ref, kv_ref):
        mv = mem_ref[...]
        memn = ((mv * _rms(mv)) * g_ref[...]).astype(BF)
        memn_ref[...] = memn
        for dev in range(N_DEV):
            kv_ref[:, dev * C_KV:(dev + 1) * C_KV] = _dot(memn, w_ref[dev]).astype(BF)

    return _pcall(body, out_shape=[SDS((m, D), BF), SDS((m, 2 * D), BF)],
                  compiler_params=_cp(), name=name)(mem, g, wkv)


def _softmax_rows(q, k):
    sc = _dot_nt(q, k) * (1.0 / (HEAD ** 0.5))
    e = jnp.exp(sc - jnp.max(sc, axis=-1, keepdims=True))
    return e / jnp.sum(e, axis=-1, keepdims=True)


def _attn_fwd(proj, kv, *, name):
    s, m = proj.shape[0], kv.shape[0]
    tm = min(TM, s)

    def body(q_ref, k_ref, v_ref, o_ref):
        p = _softmax_rows(q_ref[...], k_ref[...])
        o_ref[...] = _dot(p.astype(BF), v_ref[...]).astype(BF)

    return _pcall(
        body, grid=(s // tm, N_HEADS),
        in_specs=[pl.BlockSpec((tm, HEAD), lambda i, hd: (i, _Q_COL + hd)),
                  pl.BlockSpec((m, HEAD), lambda i, hd: (0, hd)),
                  pl.BlockSpec((m, HEAD), lambda i, hd: (0, N_HEADS + hd))],
        out_specs=pl.BlockSpec((tm, HEAD), lambda i, hd: (i, hd)), out_shape=SDS((s, D), BF),
        compiler_params=_cp(("parallel", "parallel")), name=name)(proj, kv, kv)


def _attn_bwd(proj, kv, do, *, name):
    s, m = proj.shape[0], kv.shape[0]
    tm = min(TM, s)

    def body(q_ref, k_ref, v_ref, do_ref, dq_ref, dk_ref, dv_ref):
        @pl.when(pl.program_id(1) == 0)
        def _():
            dk_ref[...] = jnp.zeros_like(dk_ref)
            dv_ref[...] = jnp.zeros_like(dv_ref)

        q, k, dov = q_ref[...], k_ref[...], do_ref[...]
        p = _softmax_rows(q, k)
        dp = _dot_nt(dov, v_ref[...])
        dv_ref[...] += _dot_tn(p.astype(BF), dov)
        ds = (p * (dp - jnp.sum(dp * p, axis=-1, keepdims=True)) * (1.0 / (HEAD ** 0.5))).astype(BF)
        dq_ref[...] = _dot(ds, k).astype(BF)
        dk_ref[...] += _dot_tn(ds, q)

    return _pcall(
        body, grid=(N_HEADS, s // tm),
        in_specs=[pl.BlockSpec((tm, HEAD), lambda hd, i: (i, _Q_COL + hd)),
                  pl.BlockSpec((m, HEAD), lambda hd, i: (0, hd)),
                  pl.BlockSpec((m, HEAD), lambda hd, i: (0, N_HEADS + hd)),
                  pl.BlockSpec((tm, HEAD), lambda hd, i: (i, hd))],
        out_specs=[pl.BlockSpec((tm, HEAD), lambda hd, i: (i, hd)),
                   pl.BlockSpec((m, HEAD), lambda hd, i: (0, hd)),
                   pl.BlockSpec((m, HEAD), lambda hd, i: (0, hd))],
        out_shape=[SDS((s, D), BF), SDS((m, D), F32), SDS((m, D), F32)],
        compiler_params=_cp(("parallel", "arbitrary")), name=name)(proj, kv, kv, do)


def _kv_bwd(mem, g, memn, dk, dv, wkv, *, name):
    def body(mem_ref, g_ref, memn_ref, dk_ref, dv_ref, w_ref, dw_ref, dg_ref):
        memn = memn_ref[...]
        dmemn = None
        for dev in range(N_DEV):
            d_ref, col = (dk_ref, dev) if dev < N_HEADS else (dv_ref, dev - N_HEADS)
            dslab = d_ref[:, col * C_KV:(col + 1) * C_KV].astype(BF)
            dw_ref[dev] = _dot_tn(memn, dslab).astype(BF)
            part = _dot_nt(dslab, w_ref[dev])
            dmemn = part if dmemn is None else dmemn + part
        mv = mem_ref[...]
        dg_ref[...] = jnp.zeros_like(dg_ref)
        dg_ref[0:1, :] = jnp.sum(dmemn * (mv * _rms(mv)), axis=0, keepdims=True)

    assert C_KV == HEAD
    return _pcall(body, out_shape=[SDS((N_DEV, D, C_KV), BF), SDS((8, D), F32)],
                  compiler_params=_cp(), name=name)(mem, g, memn, dk, dv, wkv)


_TM_MIX = 256


def _mix_out(x, za, sb, o, proj, w4, bg, g_next, *, name):
    s = x.shape[0]
    tm = min(_TM_MIX, s)

    def body(x_ref, za_ref, sb_ref, o_ref, pg_ref, w4_ref, bg_ref, gn_ref,
             ya_ref, yb_ref, yc_ref, mg_ref, x1_ref, h_ref):
        ys = (_dot(za_ref[...], w4_ref[0]), _dot(sb_ref[...], w4_ref[1]), _dot(o_ref[...], w4_ref[2]))
        merged = None
        for j, (y, y_ref) in enumerate(zip(ys, (ya_ref, yb_ref, yc_ref))):
            y_ref[...] = y.astype(BF)
            gate = _sigmoid(pg_ref[:, j * D:(j + 1) * D].astype(F32) + bg_ref[:, j * D:(j + 1) * D])
            merged = gate * y if merged is None else merged + gate * y
        mg = merged.astype(BF)
        mg_ref[...] = mg
        x1 = x_ref[...] + _dot(mg, w4_ref[3])
        x1_ref[...] = x1
        h_ref[...] = ((x1 * _rms(x1)) * gn_ref[...]).astype(BF)

    row = lambda i: (i, 0)
    act = pl.BlockSpec((tm, D), row)
    return _pcall(
        body, grid=(s // tm,),
        in_specs=[act, act, act, act, pl.BlockSpec((tm, 3 * D), lambda i: (i, 2)),
                  pl.BlockSpec((4, D, D), lambda i: (0, 0, 0)), pl.BlockSpec((1, 3 * D), lambda i: (0, 0)),
                  pl.BlockSpec((1, D), lambda i: (0, 0))],
        out_specs=[act] * 6,
        out_shape=[SDS((s, D), BF)] * 4 + [SDS((s, D), F32), SDS((s, D), BF)],
        compiler_params=_cp(("parallel",)), name=name)(x, za, sb, o, proj, w4, bg, g_next)


def _mix_bwd(dxb, ya, yb, yc, proj, w4, bg, *, name):
    s = dxb.shape[0]
    tm = min(_TM_MIX, s)

    def body(dx_ref, ya_ref, yb_ref, yc_ref, pg_ref, w4_ref, bg_ref,
             dya_ref, dyb_ref, dyc_ref, dza_ref, dsb_ref, do_ref, dgt_ref, dbg_ref):
        @pl.when(pl.program_id(0) == 0)
        def _():
            dbg_ref[...] = jnp.zeros_like(dbg_ref)

        dm = _dot_nt(dx_ref[...], w4_ref[3])
        for j, (y_ref, dy_ref, din_ref) in enumerate(zip((ya_ref, yb_ref, yc_ref), (dya_ref, dyb_ref, dyc_ref),
                                                         (dza_ref, dsb_ref, do_ref))):
            cols = slice(j * D, (j + 1) * D)
            gate = _sigmoid(pg_ref[:, cols].astype(F32) + bg_ref[:, cols])
            dy = (dm * gate).astype(BF)
            dy_ref[...] = dy
            din_ref[...] = _dot_nt(dy, w4_ref[j]).astype(BF)
            dpre = dm * y_ref[...].astype(F32) * gate * (1.0 - gate)
            dgt_ref[:, cols] = dpre.astype(BF)
            dbg_ref[0:1, cols] += jnp.sum(dpre, axis=0, keepdims=True)

    row = lambda i: (i, 0)
    act = pl.BlockSpec((tm, D), row)
    return _pcall(
        body, grid=(s // tm,),
        in_specs=[act, act, act, act, pl.BlockSpec((tm, 3 * D), lambda i: (i, 2)),
                  pl.BlockSpec((4, D, D), lambda i: (0, 0, 0)), pl.BlockSpec((1, 3 * D), lambda i: (0, 0))],
        out_specs=[act] * 6 + [pl.BlockSpec((tm, 3 * D), row), pl.BlockSpec((8, 3 * D), lambda i: (0, 0))],
        out_shape=[SDS((s, D), BF)] * 6 + [SDS((s, 3 * D), BF), SDS((8, 3 * D), F32)],
        compiler_params=_cp(("arbitrary",)), name=name)(dxb, ya, yb, yc, proj, w4, bg)


_PAIR = 2 * C_UP_P


def _ffn_act(u2, cw, *, name):
    s = u2.shape[0]
    tr, h = min(TR, s), H_S

    def body(cur, halo, w_ref, act_ref, ub):
        i = pl.program_id(1)
        ub[0:h, :] = jnp.where(i == 0, 0.0, halo[...].astype(F32))
        ub[h:h + tr, :] = cur[...].astype(F32)
        for c in range(C_UP_P // LANE):
            gl = slice(LANE * c, LANE * c + LANE)
            ul = slice(C_UP_P + LANE * c, C_UP_P + LANE * c + LANE)
            gt = _taps(ub, w_ref, gl, K_F, h - (K_F - 1), tr)
            up = _taps(ub, w_ref, ul, K_F, h - (K_F - 1), tr)
            act_ref[:, gl] = (gt * _sigmoid(gt) * up).astype(BF)

    return _pcall(
        body, grid=(4, s // tr),
        in_specs=[pl.BlockSpec((tr, _PAIR), lambda p, i: (i, p)),
                  pl.BlockSpec((h, _PAIR), lambda p, i: (_halo_before(i, tr, h), p)),
                  pl.BlockSpec((8, _PAIR), lambda p, i: (0, p))],
        out_specs=pl.BlockSpec((tr, C_UP_P), lambda p, i: (i, p)), out_shape=SDS((s, FF_P), BF),
        scratch_shapes=[pltpu.VMEM((h + tr, _PAIR), F32)],
        compiler_params=_cp(("parallel", "parallel")), name=name)(u2, u2, cw)


def _ffn_bwd(u2, dact, cw, *, name):
    s = u2.shape[0]
    tr, h = min(TR, s), H_S
    n = s // tr
    ext = tr + h

    def body(before, cur, after, da_cur, da_after, w_ref, du_ref, dw_ref, ub, dcb):
        i = pl.program_id(1)

        @pl.when(i == 0)
        def _():
            dw_ref[...] = jnp.zeros_like(dw_ref)

        ub[0:h, :] = jnp.where(i == 0, 0.0, before[...].astype(F32))
        ub[h:h + tr, :] = cur[...].astype(F32)
        ub[h + tr:h + tr + h, :] = jnp.where(i == n - 1, 0.0, after[...].astype(F32))
        for c in range(C_UP_P // LANE):
            gl = slice(LANE * c, LANE * c + LANE)
            ul = slice(C_UP_P + LANE * c, C_UP_P + LANE * c + LANE)
            gt = _taps(ub, w_ref, gl, K_F, h - (K_F - 1), ext)
            up = _taps(ub, w_ref, ul, K_F, h - (K_F - 1), ext)
            da = jnp.concatenate([da_cur[:, gl].astype(F32),
                                  jnp.where(i == n - 1, 0.0, da_after[:, gl].astype(F32))], axis=0)
            sg = _sigmoid(gt)
            dcb[:, gl] = da * up * (sg * (1.0 + gt * (1.0 - sg)))
            dcb[:, ul] = da * (gt * sg)
        for c in range(_PAIR // LANE):
            sl = slice(LANE * c, LANE * c + LANE)
            du_ref[:, sl] = _taps_rev(dcb, w_ref, sl, K_F, tr).astype(BF)
            _tap_grads(dw_ref, dcb[0:tr, sl], ub, sl, K_F, h - (K_F - 1), tr)

    return _pcall(
        body, grid=(4, n),
        in_specs=[pl.BlockSpec((h, _PAIR), lambda p, i: (_halo_before(i, tr, h), p)),
                  pl.BlockSpec((tr, _PAIR), lambda p, i: (i, p)),
                  pl.BlockSpec((h, _PAIR), lambda p, i: (_halo_after(i, tr, h, s), p)),
                  pl.BlockSpec((tr, C_UP_P), lambda p, i: (i, p)),
                  pl.BlockSpec((h, C_UP_P), lambda p, i: (_halo_after(i, tr, h, s), p)),
                  pl.BlockSpec((8, _PAIR), lambda p, i: (0, p))],
        out_specs=[pl.BlockSpec((tr, _PAIR), lambda p, i: (i, p)), pl.BlockSpec((8, _PAIR), lambda p, i: (0, p))],
        out_shape=[SDS((s, 2 * FF_P), BF), SDS((8, 2 * FF_P), F32)],
        scratch_shapes=[pltpu.VMEM((h + tr + h, _PAIR), F32), pltpu.VMEM((ext, _PAIR), F32)],
        compiler_params=_cp(("parallel", "arbitrary")), name=name)(u2, u2, u2, dact, dact, cw)


def _relations():
    x, y, c = lax.axis_index("x"), lax.axis_index("y"), lax.axis_index("c")
    out = []
    for r in range(1, N_DEV):
        rx, ry, rc = (r >> 2) & 1, (r >> 1) & 1, r & 1
        out.append((r, (x ^ rx, y ^ ry, c ^ rc)))
    return out


def _my_index():
    return 4 * lax.axis_index("x") + 2 * lax.axis_index("y") + lax.axis_index("c")


def _exchange(n_arrays, src_of, dst_of, refs):
    ssem, rsem, lsem = refs
    me = _my_index()
    local = []
    for a in range(n_arrays):
        loc = pltpu.make_async_copy(src_of(a, me), dst_of(a, me), lsem.at[a])
        loc.start()
        local.append(loc)

    def copy(a, r, peer, src_idx, dst_idx):
        return pltpu.make_async_remote_copy(
            src_ref=src_of(a, src_idx), dst_ref=dst_of(a, dst_idx), send_sem=ssem.at[a, r - 1],
            recv_sem=rsem.at[a, r - 1], device_id=peer, device_id_type=MESH)

    peers = [(r, peer, 4 * peer[0] + 2 * peer[1] + peer[2]) for r, peer in _relations()]
    for r, peer, p_idx in peers:
        for a in range(n_arrays):
            copy(a, r, peer, p_idx, me).start()
    for r, peer, p_idx in peers:
        for a in range(n_arrays):
            copy(a, r, peer, p_idx, me).wait_send()
            copy(a, r, peer, me, p_idx).wait_recv()
    for loc in local:
        loc.wait()


def _sem_scratch(n_arrays):
    return [pltpu.SemaphoreType.DMA((n_arrays, N_DEV - 1)), pltpu.SemaphoreType.DMA((n_arrays, N_DEV - 1)),
            pltpu.SemaphoreType.DMA((n_arrays,))]


def _slab(kind, ref, idx):
    if kind == "win":
        return ref.at[:, pl.ds(pl.multiple_of(idx * C_IN, LANE), C_IN)]
    if kind == "wup":
        return ref.at[:, pl.ds(pl.multiple_of(_up_slot(idx) * C_UP_P, LANE), C_UP_P)]
    if kind == "wkv":
        return ref.at[idx]
    if kind == "w4":
        return ref.at[:, pl.ds(pl.multiple_of(idx * R_O, 16), R_O), :]
    if kind == "wdn":
        return ref.at[pl.ds(pl.multiple_of(_dn_row(idx), 16), R_DN), :]
    assert kind == "cv"
    return ref.at[idx]


_WHOLE = {"win": ((D, 9 * D), BF), "wup": ((D, 2 * FF_P), BF), "wkv": ((N_DEV, D, C_KV), BF),
          "w4": ((4, D, D), BF), "wdn": ((FF_P, D), BF)}
_SHARD = {"win": (D, C_IN), "wup": (D, C_UP_P), "wkv": (D, C_KV), "w4": (4, R_O, D), "wdn": (R_DN, D)}
HBM_SPEC = pl.BlockSpec(memory_space=pltpu.HBM)
SEM_SPEC = pl.BlockSpec(memory_space=pltpu.SEMAPHORE)
_DATAFLOW = pltpu.SideEffectType.DATAFLOW_SIDE_EFFECTING


def _gather_maps(kinds):
    return ((lambda srcs, lands, a, idx: _slab(kinds[a], lands[a], _my_index())),
            (lambda lands, a, idx: _slab(kinds[a], lands[a], idx)))


def _scatter_maps(kinds):
    return ((lambda srcs, lands, a, idx: _slab(kinds[a], srcs[a], idx)),
            (lambda lands, a, idx: lands[a].at[idx]))


_SLOTTED = ("wkv", "cv")


def _own_slab_blocks(kind, shard_shape):
    if kind in ("win", "wup"):
        rows, slot = 256, (_up_slot if kind == "wup" else (lambda m: m))
        return (shard_shape[0] // rows, (rows, shard_shape[1]), (lambda i, me: (i, slot(me[0]))),
                (lambda i, me: (i, 0)), (lambda i, me: (me[0], i, 0)))
    if kind == "w4":
        return (1, shard_shape, (lambda i, me: (0, me[0], 0)), (lambda i, me: (0, 0, 0)),
                (lambda i, me: (me[0], 0, 0, 0)))
    if kind == "wdn":
        rows = 32
        return (R_DN // rows, (rows, D), (lambda i, me: (_dn_row(me[0]) // rows + i, 0)), (lambda i, me: (i, 0)),
                (lambda i, me: (me[0], i, 0)))
    assert kind in _SLOTTED
    rows = min(256, shard_shape[0])
    return (shard_shape[0] // rows, (rows, shard_shape[1]), (lambda i, me: (me[0], i, 0)),
            (lambda i, me: (i, 0)), (lambda i, me: (me[0], i, 0)))


def _place_own(kind, src, out_sds, gather, me_arr, *, name):
    shard_shape = src.shape if gather else out_sds.shape[1:]
    steps, blk, whole_idx, shard_idx, staging_idx = _own_slab_blocks(kind, shard_shape)
    slotted = kind in _SLOTTED
    whole_spec = pl.BlockSpec(((None,) if slotted else ()) + tuple(blk), whole_idx)
    if gather:
        in_spec, out_spec = pl.BlockSpec(tuple(blk), shard_idx), whole_spec
    else:
        in_spec, out_spec = whole_spec, pl.BlockSpec((None,) + tuple(blk), staging_idx)
    zero_init = gather and kind == "wdn"

    def body(me_ref, src_ref, *rest):
        rest[-1][...] = src_ref[...].astype(rest[-1].dtype)

    operands = (me_arr, src) + ((jnp.zeros(out_sds.shape, out_sds.dtype),) if zero_init else ())
    return _pcall(
        body,
        grid_spec=pltpu.PrefetchScalarGridSpec(
            num_scalar_prefetch=1, grid=(steps,), in_specs=[in_spec] + ([ANY] if zero_init else []),
            out_specs=out_spec),
        out_shape=out_sds, input_output_aliases={2: 0} if zero_init else {},
        compiler_params=_cp(("arbitrary",)), name=name)(*operands)


def _peer_copies(n, src_of, dst_of, src_r, land_r, ssem, rsem):
    me = _my_index()
    out = []
    for r, peer in _relations():
        p_idx = 4 * peer[0] + 2 * peer[1] + peer[2]
        for a in range(n):
            def copy(src_idx, dst_idx, a=a, r=r, peer=peer):
                sem = a * (N_DEV - 1) + r - 1
                return pltpu.make_async_remote_copy(
                    src_ref=src_of(src_r, land_r, a, src_idx), dst_ref=dst_of(land_r, a, dst_idx),
                    send_sem=ssem.at[sem], recv_sem=rsem.at[sem], device_id=peer, device_id_type=MESH)
            out.append((functools.partial(copy, p_idx, me), functools.partial(copy, me, p_idx)))
    return out


def _exchange_start(srcs, lands, maps, after, *, name):
    n, ns = len(lands), len(srcs)
    src_of, dst_of = maps

    def body(*refs):
        src_r, land_r = refs[:ns], refs[ns:ns + n]
        ssem, rsem, token = refs[ns + n + 1], refs[ns + n + 2], refs[-1]
        for send, _ in _peer_copies(n, src_of, dst_of, src_r, land_r, ssem, rsem):
            send().start()
        token[...] = jnp.zeros_like(token)

    flight = list(srcs) + list(lands)
    outs = pl.pallas_call(
        body, name=name,
        out_shape=(pltpu.SemaphoreType.DMA((n * (N_DEV - 1),)), pltpu.SemaphoreType.DMA((n * (N_DEV - 1),)),
                   *[pltpu.HBM(a.shape, a.dtype) for a in flight], SDS((8, LANE), F32)),
        in_specs=[HBM_SPEC] * (ns + n) + [ANY],
        out_specs=(SEM_SPEC, SEM_SPEC, *[HBM_SPEC] * (ns + n), pl.BlockSpec(memory_space=pltpu.VMEM)),
        input_output_aliases={i: 2 + i for i in range(ns + n)},
        compiler_params=pltpu.CompilerParams(has_side_effects=_DATAFLOW),
    )(*[pltpu.with_memory_space_constraint(a, pltpu.HBM) for a in flight], after)
    return (outs[0], outs[1], list(outs[2:2 + ns + n]), ns), outs[-1]


def _exchange_wait(handle, maps, after, *, name):
    ssem, rsem, flight, ns = handle
    n = len(flight) - ns
    src_of, dst_of = maps

    def body(*refs):
        src_r, land_r, ssem_r, rsem_r = refs[:ns], refs[ns:ns + n], refs[ns + n], refs[ns + n + 1]
        for send, arrival in _peer_copies(n, src_of, dst_of, src_r, land_r, ssem_r, rsem_r):
            send().wait_send()
            arrival().wait_recv()

    outs = pl.pallas_call(
        body, name=name, out_shape=[pltpu.HBM(a.shape, a.dtype) for a in flight],
        in_specs=[HBM_SPEC] * (ns + n) + [SEM_SPEC, SEM_SPEC, ANY], out_specs=[HBM_SPEC] * (ns + n),
        input_output_aliases={i: i for i in range(ns + n)},
        compiler_params=pltpu.CompilerParams(has_side_effects=_DATAFLOW),
    )(*flight, ssem, rsem, after)
    return list(outs[ns:])


def _allreduce_small(pack, *, name):
    rows = pack.shape[0]

    def body(p_ref, out_ref, gath, ssem, rsem, lsem):
        _exchange(1, lambda a, idx: p_ref, lambda a, idx: gath.at[idx], (ssem, rsem, lsem))
        total = gath[0]
        for d in range(1, N_DEV):
            total = total + gath[d]
        out_ref[...] = total

    vm = pl.BlockSpec(memory_space=pltpu.VMEM)
    return _pcall(
        body, in_specs=[vm], out_specs=vm, out_shape=SDS(pack.shape, F32),
        scratch_shapes=[pltpu.VMEM((N_DEV, rows, pack.shape[1]), F32)] + _sem_scratch(1),
        compiler_params=_cp(has_side_effects=True), name=name)(pack)


def _adam(g, w, m, v):
    nm = ADAM_B1 * m + (1.0 - ADAM_B1) * g
    nv = ADAM_B2 * v + (1.0 - ADAM_B2) * (g * g)
    m_hat = nm / (1.0 - ADAM_B1 ** ADAM_STEP)
    v_hat = nv / (1.0 - ADAM_B2 ** ADAM_STEP)
    return -ADAM_LR * (m_hat / (jnp.sqrt(v_hat) + ADAM_EPS) + ADAM_WD * w), nm, nv


def _adamw_staged(st0, st1, w, m, v, *, name):
    _, rows, cols = w.shape
    tr = max(t for t in range(16, 129, 16) if rows % t == 0)
    nr = rows // tr

    def body(s0_ref, s1_ref, w_ref, m_ref, v_ref, g_ref, d_ref, nm_ref, nv_ref):
        for layer, s_ref in enumerate((s0_ref, s1_ref)):
            @pl.when(pl.program_id(0) == layer)
            def _(s_ref=s_ref):
                total = s_ref[0].astype(F32)
                for dev in range(1, N_DEV):
                    total = total + s_ref[dev].astype(F32)
                g_ref[0] = total

        d_ref[0], nm_ref[0], nv_ref[0] = _adam(g_ref[0], w_ref[0], m_ref[0], v_ref[0])

    st_spec = lambda layer: pl.BlockSpec(
        (N_DEV, tr, cols), lambda l, i: (0, jnp.where(l == layer, i, (nr - 1) * (1 - layer)), 0))
    par = pl.BlockSpec((1, tr, cols), lambda l, i: (l, i, 0))
    return _pcall(
        body, grid=(DEPTH, nr), in_specs=[st_spec(0), st_spec(1), par, par, par], out_specs=[par] * 4,
        out_shape=[SDS(w.shape, F32)] * 4,
        compiler_params=_cp(("arbitrary", "arbitrary")), name=name)(st0, st1, w, m, v)


def _adamw_small(g, w, m, v, *, name):
    def body(g_ref, w_ref, m_ref, v_ref, d_ref, nm_ref, nv_ref):
        d_ref[...], nm_ref[...], nv_ref[...] = _adam(g_ref[...], w_ref[...], m_ref[...], v_ref[...])

    return _pcall(body, out_shape=[SDS(g.shape, F32)] * 3, compiler_params=_cp(), name=name)(g, w, m, v)


def _pack_rows(arrays):
    flat = jnp.concatenate([a.reshape(-1).astype(F32) for a in arrays])
    rows = -(-flat.shape[0] // (8 * D)) * 8
    return jnp.pad(flat, (0, rows * D - flat.shape[0])).reshape(rows, D)


def _unpack_rows(pack, like):
    flat = pack.reshape(-1)
    out, at = [], 0
    for a in like:
        out.append(flat[at:at + a.size].reshape(a.shape))
        at += a.size
    return out


def _layer_fwd(x, h, mem, win, rest_of_weights, small, g_next, tag):
    proj = _mm(h, win, tm=1024, tn=1536, name=f"proj_{tag}")
    wup, wkv, w4, wdn, cw_a, cw_b, cw_f = rest_of_weights(proj)
    za = _bra_fwd(proj, cw_a, name=f"bra_fwd_{tag}")
    cb = _brb_conv_fwd(proj, cw_b, small["conv_b_bias"], name=f"brb_conv_fwd_{tag}")
    sb = _ln_silu_fwd(cb, small["ln_b_g"], small["ln_b_b"], name=f"ln_silu_fwd_{tag}")
    memn, kv = _kv_prep(mem, small["norm_mem_g"], wkv, name=f"kv_prep_{tag}")
    o = _attn_fwd(proj, kv, name=f"attn_fwd_{tag}")
    ya, yb, yc, mg, x1, h2 = _mix_out(x, za, sb, o, proj, w4, small["b_gate"], small["norm_ffn_g"],
                                      name=f"mix_out_{tag}")
    u2 = _mm(h2, wup, tm=1024, tn=1536, name=f"up_{tag}")
    act = _ffn_act(u2, cw_f, name=f"ffn_act_{tag}")
    x2, h_next = _mm_res_norm(act, wdn, x1, g_next, name=f"down_{tag}")
    saved = dict(x=x, h=h, proj=proj, za=za, cb=cb, sb=sb, memn=memn, kv=kv, o=o, ya=ya, yb=yb, yc=yc,
                 mg=mg, x1=x1, h2=h2, u2=u2, act=act)
    return x2, h_next, (win, wup, wkv, w4, wdn, cw_a, cw_b, cw_f), saved


def _behind(operand, token):
    return operand + token[0:1, 0:1]


def _layer_bwd(dx2, dx2b, mem, wts, small, sv, start, tag):
    win, wup, wkv, w4, wdn, cw_a, cw_b, cw_f = wts
    dact = _mm(dx2b, wdn, tb=True, tm=1024, tn=768, name=f"d_act_{tag}")
    dwdn = _mm(sv["act"], dx2b, ta=True, tm=768, tn=1024, name=f"dw_down_{tag}")
    du2, dcw_f = _ffn_bwd(sv["u2"], dact, cw_f, name=f"ffn_bwd_{tag}")
    dwup = _mm(sv["h2"], du2, ta=True, tm=1024, tn=768, name=f"dw_up_{tag}")
    token = start(("wdn", "wup"), (dwdn, dwup), f"ffn_{tag}")
    dx1, dx1b, dg_ffn = _mm_nt_normbwd(du2, wup, sv["x1"], dx2, _behind(small["norm_ffn_g"], token),
                                       tk=3072, name=f"d_h2_{tag}")

    dya, dyb, dyc, dza, dsb, do, dgate, dbg = _mix_bwd(dx1b, sv["ya"], sv["yb"], sv["yc"], sv["proj"], w4,
                                                      small["b_gate"], name=f"mix_bwd_{tag}")
    dw4 = jnp.stack([
        _mm(a, b, ta=True, tm=1024, tn=512, name=f"dw_{nm}_{tag}")
        for nm, a, b in (("a_out", sv["za"], dya), ("b_out", sv["sb"], dyb), ("att_out", sv["o"], dyc),
                         ("o", sv["mg"], dx1b))])
    dq, dk, dv = _attn_bwd(sv["proj"], sv["kv"], do, name=f"attn_bwd_{tag}")
    dwkv, dg_mem = _kv_bwd(mem, small["norm_mem_g"], sv["memn"], dk, dv, wkv, name=f"kv_bwd_{tag}")
    token = start(("w4", "wkv"), (dw4, dwkv), f"mix_{tag}")
    d_a, dcw_a = _bra_bwd(sv["proj"], dza, _behind(cw_a, token), name=f"bra_bwd_{tag}")
    dcb, ln_sums = _ln_silu_bwd(sv["cb"], dsb, small["ln_b_g"], small["ln_b_b"], name=f"ln_silu_bwd_{tag}")
    d_b, dcw_b = _brb_conv_bwd(sv["proj"], dcb, cw_b, name=f"brb_conv_bwd_{tag}")
    dproj = jnp.concatenate([d_a, d_b, dq, dgate], axis=1)
    dwin = _mm(sv["h"], dproj, ta=True, tm=1024, tn=768, name=f"dw_in_{tag}")
    token = start(("win",), (dwin,), f"in_{tag}")
    dx, dxb, dg_mix = _mm_nt_normbwd(dproj, win, sv["x"], dx1, _behind(small["norm_mix_g"], token),
                                     tk=3072, name=f"d_h_{tag}")

    small_grads = [dg_mix[0:1], dg_mem[0:1], dbg[0:1].reshape(3, D), ln_sums[2:3], ln_sums[0:1], ln_sums[1:2],
                   dg_ffn[0:1], dcw_a[0:K_A], dcw_b[0:K_B], dcw_f[0:K_F].reshape(K_F * 2 * FF_P // D, D)]
    return dx, dxb, small_grads, token


_SMALL_ROWS = (1, 1, 3, 1, 1, 1, 1, K_A, K_B, K_F * 2 * FF_P // D)
_CV_ROWS = 48


def kernel(x, mem, norm_mix_g, norm_mem_g, w_in, b_gate, conv_a_w, w_a_out, conv_b_w, conv_b_bias, ln_b_g, ln_b_b, w_b_out, w_kv, w_att_out, w_o, norm_ffn_g, w_up, conv_ffn_w, w_down, norm_final_g, loss_target, m_norm_mix_g, m_norm_mem_g, m_w_in, m_b_gate, m_conv_a_w, m_w_a_out, m_conv_b_w, m_conv_b_bias, m_ln_b_g, m_ln_b_b, m_w_b_out, m_w_kv, m_w_att_out, m_w_o, m_norm_ffn_g, m_w_up, m_conv_ffn_w, m_w_down, m_norm_final_g, v_norm_mix_g, v_norm_mem_g, v_w_in, v_b_gate, v_conv_a_w, v_w_a_out, v_conv_b_w, v_conv_b_bias, v_ln_b_g, v_ln_b_b, v_w_b_out, v_w_kv, v_w_att_out, v_w_o, v_norm_ffn_g, v_w_up, v_conv_ffn_w, v_w_down, v_norm_final_g):
    me = _my_index()
    me_arr = me.astype(jnp.int32).reshape(1)
    x0, mem0, tgt = x.reshape(x.shape[1:]), mem.reshape(mem.shape[1:]), loss_target.reshape(x.shape[1:])
    up_pad = ((0, 0), (0, 0), (0, C_UP_P - C_UP))

    ag_groups = (("win",), ("wup", "wkv", "w4", "wdn", "cv"))
    kinds = ag_groups[0] + ag_groups[1]
    smalls, ag_handles = [], []
    token = jnp.zeros((8, LANE), F32)
    for l in range(DEPTH):
        cv = jnp.zeros((_CV_ROWS, C_UP_P), F32)
        cv = cv.at[0:K_F, 0:C_UP].set(conv_ffn_w[l]).at[3:3 + K_A, 0:R_O].set(conv_a_w[l])
        cv = cv.at[8:8 + K_B, 0:R_O].set(conv_b_w[l])
        shards = dict(
            win=w_in[l], wup=jnp.pad(w_up[l], up_pad[1:]), wkv=w_kv[l],
            w4=jnp.stack([w_a_out[l], w_b_out[l], w_att_out[l], w_o[l]]), wdn=w_down[l], cv=cv)
        whole = dict({k: SDS(*_WHOLE[k]) for k in kinds[:-1]}, cv=SDS((N_DEV,) + cv.shape, F32))
        lands = {k: _place_own(k, shards[k], whole[k], True, me_arr, name=f"ag_own_{k}_l{l}") for k in kinds}
        per_layer = []
        for g, grp in enumerate(ag_groups):
            handle, token = _exchange_start([], [lands[k] for k in grp], _gather_maps(grp), token,
                                            name=f"ag_start_l{l}_g{g}")
            per_layer.append(handle)
        ag_handles.append(per_layer)
        smalls.append(dict(
            norm_mix_g=norm_mix_g[l][None], norm_mem_g=norm_mem_g[l][None], b_gate=b_gate[l][None],
            conv_b_bias=conv_b_bias[l][None], ln_b_g=ln_b_g[l][None], ln_b_b=ln_b_b[l][None],
            norm_ffn_g=norm_ffn_g[l][None]))

    def rest_of_weights(l):
        def wait(after):
            wup, wkv, w4, wdn, cvg = _exchange_wait(ag_handles[l][1], _gather_maps(ag_groups[1]), after,
                                                    name=f"ag_wait_l{l}_g1")
            cw_f = jnp.stack([cvg[d, 0:K_F, :] for d in UP_ORDER], axis=1).reshape(K_F, 2 * FF_P)
            cw_a = cvg[:, 3:3 + K_A, 0:R_O].transpose(1, 0, 2).reshape(K_A, D)
            cw_b = cvg[:, 8:8 + K_B, 0:R_O].transpose(1, 0, 2).reshape(K_B, D)
            return (wup, wkv, w4, wdn, jnp.pad(cw_a, ((0, 8 - K_A), (0, 0))),
                    jnp.pad(cw_b, ((0, 32 - K_B), (0, 0))), jnp.pad(cw_f, ((0, 8 - K_F), (0, 0))))
        return wait

    wts, saved = [], []
    xs = x0
    h = _rms_fwd(xs, smalls[0]["norm_mix_g"], name="rms_fwd")
    behind = token
    for l in range(DEPTH):
        g_next = smalls[l + 1]["norm_mix_g"] if l + 1 < DEPTH else norm_final_g[None]
        (win,) = _exchange_wait(ag_handles[l][0], _gather_maps(ag_groups[0]), behind, name=f"ag_wait_l{l}_g0")
        xs, h, w_l, sv = _layer_fwd(xs, h, mem0, win, rest_of_weights(l), smalls[l], g_next, f"l{l}")
        behind = h
        wts.append(w_l)
        saved.append(sv)
    dx, dxb, head_sums = _loss_head(xs, tgt, norm_final_g[None], name="loss_head")

    rs_handles = []
    small_grads = [None] * DEPTH

    def start_scatter(grp, arrays, name):
        maps = _scatter_maps(grp)
        lands = [_place_own(k, a, SDS((N_DEV,) + _SHARD[k], BF), False, me_arr, name=f"rs_own_{k}_{name}")
                 for k, a in zip(grp, arrays)]
        handle, tok = _exchange_start(list(arrays), lands, maps, rs_handles[-1][2] if rs_handles else head_sums,
                                      name=f"rs_start_{name}")
        rs_handles.append((grp, handle, tok, name))
        return tok

    for l in reversed(range(DEPTH)):
        dx, dxb, small_grads[l], token = _layer_bwd(dx, dxb, mem0, wts[l], smalls[l], saved[l], start_scatter,
                                                    f"l{l}")

    staged = [dict() for _ in range(DEPTH)]
    for grp, handle, _, name in rs_handles[:-1]:
        staged[int(name[-1])].update(zip(grp, _exchange_wait(handle, _scatter_maps(grp), dx, name=f"rs_wait_{name}")))

    pack = jnp.concatenate(small_grads[0] + small_grads[1] + [head_sums[1:2], head_sums[0:1]], axis=0)
    pack = jnp.pad(pack, ((0, -pack.shape[0] % 8), (0, 0)))
    total = _allreduce_small(pack, name="allreduce_small")
    per_layer = sum(_SMALL_ROWS)
    parts = []
    for l in range(DEPTH):
        at, one = l * per_layer, []
        for rows in _SMALL_ROWS:
            one.append(total[at:at + rows])
            at += rows
        parts.append(one)
    g_final = total[DEPTH * per_layer]
    loss = 0.5 / D * jnp.sum(total[DEPTH * per_layer + 1])

    def both(i):
        return jnp.stack([parts[0][i], parts[1][i]])

    g_norm_mix, g_norm_mem = both(0)[:, 0], both(1)[:, 0]
    g_b_gate = both(2).reshape(DEPTH, 3 * D)
    g_cbias, g_lng, g_lnb, g_norm_ffn = both(3)[:, 0], both(4)[:, 0], both(5)[:, 0], both(6)[:, 0]
    g_conv_a = lax.dynamic_slice_in_dim(both(7), me * R_O, R_O, axis=2)
    g_conv_b = lax.dynamic_slice_in_dim(both(8), me * R_O, R_O, axis=2)
    g_conv_f = lax.dynamic_slice_in_dim(both(9).reshape(DEPTH, K_F, 2 * FF_P), _up_slot(me) * C_UP_P, C_UP, axis=2)

    small_g = [g_norm_mix, g_norm_mem, g_b_gate, g_conv_a, g_conv_b, g_cbias, g_lng, g_lnb, g_norm_ffn, g_conv_f,
               g_final]
    small_w = [norm_mix_g, norm_mem_g, b_gate, conv_a_w, conv_b_w, conv_b_bias, ln_b_g, ln_b_b, norm_ffn_g,
               conv_ffn_w, norm_final_g]
    small_m = [m_norm_mix_g, m_norm_mem_g, m_b_gate, m_conv_a_w, m_conv_b_w, m_conv_b_bias, m_ln_b_g, m_ln_b_b,
               m_norm_ffn_g, m_conv_ffn_w, m_norm_final_g]
    small_v = [v_norm_mix_g, v_norm_mem_g, v_b_gate, v_conv_a_w, v_conv_b_w, v_conv_b_bias, v_ln_b_g, v_ln_b_b,
               v_norm_ffn_g, v_conv_ffn_w, v_norm_final_g]
    upd = _adamw_small(_pack_rows(small_g), _pack_rows(small_w), _pack_rows(small_m), _pack_rows(small_v),
                       name="adamw_small")
    s_d, s_m, s_v = (_unpack_rows(p, small_w) for p in upd)
    (d_norm_mix, d_norm_mem, d_b_gate, d_conv_a, d_conv_b, d_cbias, d_lng, d_lnb, d_norm_ffn, d_conv_f,
     d_final) = s_d
    (nm_norm_mix, nm_norm_mem, nm_b_gate, nm_conv_a, nm_conv_b, nm_cbias, nm_lng, nm_lnb, nm_norm_ffn, nm_conv_f,
     nm_final) = s_m
    (nv_norm_mix, nv_norm_mem, nv_b_gate, nv_conv_a, nv_conv_b, nv_cbias, nv_lng, nv_lnb, nv_norm_ffn, nv_conv_f,
     nv_final) = s_v

    def big_update(kind, w, m, v, name):
        return _adamw_staged(staged[0][kind], staged[1][kind], w, m, v, name=name)

    r_up = [a[:, :, 0:C_UP] for a in big_update("wup", jnp.pad(w_up, up_pad), jnp.pad(m_w_up, up_pad),
                                                jnp.pad(v_w_up, up_pad), "adamw_w_up")]
    r_kv = big_update("wkv", w_kv, m_w_kv, v_w_kv, "adamw_w_kv")
    r_dn = big_update("wdn", w_down, m_w_down, v_w_down, "adamw_w_down")

    def four(a, b, c, d_):
        return jnp.stack([a, b, c, d_], axis=1).reshape(DEPTH, 4 * R_O, D)

    r_4 = _adamw_staged(
        staged[0]["w4"].reshape(N_DEV, 4 * R_O, D), staged[1]["w4"].reshape(N_DEV, 4 * R_O, D),
        four(w_a_out, w_b_out, w_att_out, w_o), four(m_w_a_out, m_w_b_out, m_w_att_out, m_w_o),
        four(v_w_a_out, v_w_b_out, v_w_att_out, v_w_o), name="adamw_w_out")
    grp, handle, _, name = rs_handles[-1]
    staged[0].update(zip(grp, _exchange_wait(handle, _scatter_maps(grp), r_4[0], name=f"rs_wait_{name}")))
    r_in = big_update("win", w_in, m_w_in, v_w_in, "adamw_w_in")
    r_a, r_b, r_att, r_o = ([a.reshape(DEPTH, 4, R_O, D)[:, j] for a in r_4] for j in range(4))

    grads = [g_norm_mix, g_norm_mem, r_in[0], g_b_gate, g_conv_a, r_a[0], g_conv_b, g_cbias, g_lng, g_lnb, r_b[0],
             r_kv[0], r_att[0], r_o[0], g_norm_ffn, r_up[0], g_conv_f, r_dn[0], g_final]
    deltas = [d_norm_mix, d_norm_mem, r_in[1], d_b_gate, d_conv_a, r_a[1], d_conv_b, d_cbias, d_lng, d_lnb, r_b[1],
              r_kv[1], r_att[1], r_o[1], d_norm_ffn, r_up[1], d_conv_f, r_dn[1], d_final]
    new_m = [nm_norm_mix, nm_norm_mem, r_in[2], nm_b_gate, nm_conv_a, r_a[2], nm_conv_b, nm_cbias, nm_lng, nm_lnb,
             r_b[2], r_kv[2], r_att[2], r_o[2], nm_norm_ffn, r_up[2], nm_conv_f, r_dn[2], nm_final]
    new_v = [nv_norm_mix, nv_norm_mem, r_in[3], nv_b_gate, nv_conv_a, r_a[3], nv_conv_b, nv_cbias, nv_lng, nv_lnb,
             r_b[3], r_kv[3], r_att[3], r_o[3], nv_norm_ffn, r_up[3], nv_conv_f, r_dn[3], nv_final]
    return (loss, dx[None], *grads, *deltas, *new_m, *new_v)
```

```python
import functools

import jax
import jax.numpy as jnp
from jax import lax
from jax.experimental import pallas as pl
from jax.experimental.pallas import tpu as pltpu

F32 = jnp.float32
BF = jnp.bfloat16
SDS = jax.ShapeDtypeStruct
MESH = pl.DeviceIdType.MESH
ANY = pl.BlockSpec(memory_space=pl.ANY)

N_DEV = 8
DEPTH = 2
D = 1024
N_HEADS = 4
HEAD = D // N_HEADS
D_FF = 2816
K_A, K_B, K_F = 3, 31, 3
NORM_EPS = 1e-6

C_IN = 9 * D // N_DEV
C_KV = 2 * D // N_DEV
C_UP = 2 * D_FF // N_DEV
LANE = 128
C_UP_P = -(-C_UP // LANE) * LANE
FF_P = 4 * C_UP_P
R_O = D // N_DEV
R_DN = D_FF // N_DEV

VMEM_LIMIT = 56 * 1024 * 1024
TM = 512
TR = 256
H_S, H_L = 16, 32

ADAM_LR, ADAM_B1, ADAM_B2, ADAM_EPS, ADAM_WD, ADAM_STEP = 0.001, 0.9, 0.999, 1e-08, 0.01, 10

UP_ORDER = (0, 4, 1, 5, 2, 6, 3, 7)


def _pcall(body, **kw):
    return pl.pallas_call(body, **kw)


def _cp(sem=None, **kw):
    return pltpu.CompilerParams(dimension_semantics=sem, vmem_limit_bytes=VMEM_LIMIT, **kw)


def _dot(a, b):
    return jnp.dot(a, b, preferred_element_type=F32)


def _dot_nt(a, b):
    return lax.dot_general(a, b, (((1,), (1,)), ((), ())), preferred_element_type=F32)


def _dot_tn(a, b):
    return lax.dot_general(a, b, (((0,), (0,)), ((), ())), preferred_element_type=F32)


def _sigmoid(z):
    return 1.0 / (1.0 + jnp.exp(-z))


def _rms(xv):
    return lax.rsqrt(jnp.mean(xv * xv, axis=-1, keepdims=True) + NORM_EPS)


def _up_slot(idx):
    return jnp.where(idx < 4, 2 * idx, 2 * (idx - 4) + 1)


def _dn_row(idx):
    return C_UP_P * (idx // 2) + R_DN * (idx % 2)


def _mm(a, b, *, ta=False, tb=False, out_dtype=BF, tm=TM, tn=512, tk=None, name):
    m, k_dim = (a.shape[1], a.shape[0]) if ta else a.shape
    n = b.shape[0] if tb else b.shape[1]
    tm, tn = min(tm, m), min(tn, n)
    tk = k_dim if tk is None else min(tk, k_dim)
    nk = k_dim // tk
    assert m % tm == 0 and n % tn == 0 and k_dim % tk == 0
    dims = (((0 if ta else 1,), (1 if tb else 0,)), ((), ()))

    def body(a_ref, b_ref, o_ref, *scratch):
        part = lax.dot_general(a_ref[...], b_ref[...], dims, preferred_element_type=F32)
        if nk == 1:
            o_ref[...] = part.astype(o_ref.dtype)
            return
        acc = scratch[0]
        k = pl.program_id(2)

        @pl.when(k == 0)
        def _():
            acc[...] = part

        @pl.when(k > 0)
        def _():
            acc[...] += part

        @pl.when(k == nk - 1)
        def _():
            o_ref[...] = acc[...].astype(o_ref.dtype)

    a_spec = pl.BlockSpec((tk, tm), lambda i, j, k: (k, i)) if ta else pl.BlockSpec((tm, tk), lambda i, j, k: (i, k))
    b_spec = pl.BlockSpec((tn, tk), lambda i, j, k: (j, k)) if tb else pl.BlockSpec((tk, tn), lambda i, j, k: (k, j))
    return _pcall(
        body, grid=(m // tm, n // tn, nk), in_specs=[a_spec, b_spec],
        out_specs=pl.BlockSpec((tm, tn), lambda i, j, k: (i, j)),
        out_shape=SDS((m, n), out_dtype),
        scratch_shapes=[pltpu.VMEM((tm, tn), F32)] if nk > 1 else [],
        compiler_params=_cp(("parallel", "parallel", "arbitrary")), name=name)(a, b)


def _mm_res_norm(a, w, x, g, *, name):
    s, k_dim = a.shape
    tm = min(TM, s)

    def body(a_ref, w_ref, x_ref, g_ref, xo_ref, h_ref):
        xo = x_ref[...] + _dot(a_ref[...], w_ref[...])
        xo_ref[...] = xo
        h_ref[...] = ((xo * _rms(xo)) * g_ref[...]).astype(BF)

    return _pcall(
        body, grid=(s // tm,),
        in_specs=[pl.BlockSpec((tm, k_dim), lambda i: (i, 0)), pl.BlockSpec((k_dim, D), lambda i: (0, 0)),
                  pl.BlockSpec((tm, D), lambda i: (i, 0)), pl.BlockSpec((1, D), lambda i: (0, 0))],
        out_specs=[pl.BlockSpec((tm, D), lambda i: (i, 0))] * 2,
        out_shape=[SDS((s, D), F32), SDS((s, D), BF)],
        compiler_params=_cp(("parallel",)), name=name)(a, w, x, g)


def _mm_nt_normbwd(da, w, x, dres, g, *, tk, name):
    s, k_dim = da.shape
    tm = min(TM, s)
    nk = k_dim // tk
    assert k_dim % tk == 0

    def body(da_ref, w_ref, x_ref, dres_ref, g_ref, dx_ref, dxb_ref, dg_ref, acc):
        i, k = pl.program_id(0), pl.program_id(1)
        part = _dot_nt(da_ref[...], w_ref[...])

        @pl.when(k == 0)
        def _():
            acc[...] = part

        @pl.when(k > 0)
        def _():
            acc[...] += part

        @pl.when((i == 0) & (k == 0))
        def _():
            dg_ref[...] = jnp.zeros_like(dg_ref)

        @pl.when(k == nk - 1)
        def _():
            dh = acc[...]
            xv = x_ref[...]
            r = _rms(xv)
            xn = xv * r
            dg_ref[0:1, :] += jnp.sum(dh * xn, axis=0, keepdims=True)
            dxn = dh * g_ref[...]
            dx = dres_ref[...] + r * (dxn - xn * jnp.mean(dxn * xn, axis=-1, keepdims=True))
            dx_ref[...] = dx
            dxb_ref[...] = dx.astype(BF)

    row = lambda i, k: (i, 0)
    return _pcall(
        body, grid=(s // tm, nk),
        in_specs=[pl.BlockSpec((tm, tk), lambda i, k: (i, k)), pl.BlockSpec((D, tk), lambda i, k: (0, k)),
                  pl.BlockSpec((tm, D), row), pl.BlockSpec((tm, D), row), pl.BlockSpec((1, D), lambda i, k: (0, 0))],
        out_specs=[pl.BlockSpec((tm, D), row), pl.BlockSpec((tm, D), row), pl.BlockSpec((8, D), lambda i, k: (0, 0))],
        out_shape=[SDS((s, D), F32), SDS((s, D), BF), SDS((8, D), F32)],
        scratch_shapes=[pltpu.VMEM((tm, D), F32)],
        compiler_params=_cp(("arbitrary", "arbitrary")), name=name)(da, w, x, dres, g)


def _rms_fwd(x, g, *, name):
    s = x.shape[0]
    tm = min(TM, s)

    def body(x_ref, g_ref, h_ref):
        xv = x_ref[...]
        h_ref[...] = ((xv * _rms(xv)) * g_ref[...]).astype(BF)

    return _pcall(
        body, grid=(s // tm,),
        in_specs=[pl.BlockSpec((tm, D), lambda i: (i, 0)), pl.BlockSpec((1, D), lambda i: (0, 0))],
        out_specs=pl.BlockSpec((tm, D), lambda i: (i, 0)), out_shape=SDS((s, D), BF),
        compiler_params=_cp(("parallel",)), name=name)(x, g)


def _loss_head(x, tgt, g, *, name):
    s = x.shape[0]
    tm = min(TM, s)

    def body(x_ref, t_ref, g_ref, dx_ref, dxb_ref, sums_ref):
        @pl.when(pl.program_id(0) == 0)
        def _():
            sums_ref[...] = jnp.zeros_like(sums_ref)

        xv = x_ref[...]
        r = _rms(xv)
        xn = xv * r
        diff = xn * g_ref[...] - t_ref[...]
        sums_ref[0:1, :] += jnp.sum(diff * diff, axis=0, keepdims=True)
        dy = diff * (1.0 / D)
        sums_ref[1:2, :] += jnp.sum(dy * xn, axis=0, keepdims=True)
        dxn = dy * g_ref[...]
        dx = r * (dxn - xn * jnp.mean(dxn * xn, axis=-1, keepdims=True))
        dx_ref[...] = dx
        dxb_ref[...] = dx.astype(BF)

    row = lambda i: (i, 0)
    return _pcall(
        body, grid=(s // tm,),
        in_specs=[pl.BlockSpec((tm, D), row), pl.BlockSpec((tm, D), row), pl.BlockSpec((1, D), lambda i: (0, 0))],
        out_specs=[pl.BlockSpec((tm, D), row), pl.BlockSpec((tm, D), row), pl.BlockSpec((8, D), lambda i: (0, 0))],
        out_shape=[SDS((s, D), F32), SDS((s, D), BF), SDS((8, D), F32)],
        compiler_params=_cp(("arbitrary",)), name=name)(x, tgt, g)


def _halo_before(i, tr, h):
    return jnp.maximum(i * (tr // h) - 1, 0)


def _halo_after(i, tr, h, s):
    return jnp.minimum((i + 1) * (tr // h), s // h - 1)


def _taps(buf, w_ref, sl, k_w, base, rows):
    acc = None
    for k in range(k_w):
        t = w_ref[k:k + 1, sl] * buf[base + k:base + k + rows, sl]
        acc = t if acc is None else acc + t
    return acc


def _taps_rev(buf, w_ref, sl, k_w, rows):
    acc = None
    for k in range(k_w):
        t = w_ref[k:k + 1, sl] * buf[k_w - 1 - k:k_w - 1 - k + rows, sl]
        acc = t if acc is None else acc + t
    return acc


def _tap_grads(dw_ref, dc, buf, sl, k_w, base, rows):
    for k in range(k_w):
        dw_ref[k:k + 1, sl] += jnp.sum(dc * buf[base + k:base + k + rows, sl], axis=0, keepdims=True)


def _bra_fwd(proj, cw, *, name):
    s = proj.shape[0]
    tr, h = min(TR, s), H_S

    def body(cur, halo, w_ref, za_ref, cvb):
        i = pl.program_id(0)
        hv = halo[:, D:2 * D].astype(F32) * halo[:, 2 * D:3 * D].astype(F32)
        cvb[0:h, :] = jnp.where(i == 0, 0.0, hv)
        cvb[h:h + tr, :] = cur[:, D:2 * D].astype(F32) * cur[:, 2 * D:3 * D].astype(F32)
        for c in range(D // LANE):
            sl = slice(LANE * c, LANE * c + LANE)
            ca = _taps(cvb, w_ref, sl, K_A, h - (K_A - 1), tr)
            za_ref[:, sl] = (cur[:, sl].astype(F32) * ca).astype(BF)

    return _pcall(
        body, grid=(s // tr,),
        in_specs=[pl.BlockSpec((tr, 3 * D), lambda i: (i, 0)),
                  pl.BlockSpec((h, 3 * D), lambda i: (_halo_before(i, tr, h), 0)),
                  pl.BlockSpec((8, D), lambda i: (0, 0))],
        out_specs=pl.BlockSpec((tr, D), lambda i: (i, 0)), out_shape=SDS((s, D), BF),
        scratch_shapes=[pltpu.VMEM((h + tr, D), F32)],
        compiler_params=_cp(("parallel",)), name=name)(proj, proj, cw)


def _bra_bwd(proj, dza, cw, dproj, *, name):
    s = proj.shape[0]
    tr, h = min(TR, s), H_S
    n = s // tr

    def body(before, cur, after, dz_cur, dz_after, w_ref, dproj_in, da_ref, dw_ref, cvb, dcab):
        del dproj_in
        i = pl.program_id(0)

        @pl.when(i == 0)
        def _():
            dw_ref[...] = jnp.zeros_like(dw_ref)

        first, last = i == 0, i == n - 1
        cvb[0:h, :] = jnp.where(first, 0.0, before[:, D:2 * D].astype(F32) * before[:, 2 * D:3 * D].astype(F32))
        cvb[h:h + tr, :] = cur[:, D:2 * D].astype(F32) * cur[:, 2 * D:3 * D].astype(F32)
        dcab[0:tr, :] = dz_cur[...].astype(F32) * cur[:, 0:D].astype(F32)
        dcab[tr:tr + h, :] = jnp.where(last, 0.0, dz_after[...].astype(F32) * after[:, 0:D].astype(F32))
        for c in range(D // LANE):
            sl = slice(LANE * c, LANE * c + LANE)
            gc = cur[:, D + LANE * c:D + LANE * c + LANE].astype(F32)
            v = cur[:, 2 * D + LANE * c:2 * D + LANE * c + LANE].astype(F32)
            ca = _taps(cvb, w_ref, sl, K_A, h - (K_A - 1), tr)
            da_ref[:, sl] = (dz_cur[:, sl].astype(F32) * ca).astype(BF)
            dcv = _taps_rev(dcab, w_ref, sl, K_A, tr)
            da_ref[:, D + LANE * c:D + LANE * c + LANE] = (dcv * v).astype(BF)
            da_ref[:, 2 * D + LANE * c:2 * D + LANE * c + LANE] = (dcv * gc).astype(BF)
            _tap_grads(dw_ref, dcab[0:tr, sl], cvb, sl, K_A, h - (K_A - 1), tr)

    return _pcall(
        body, grid=(n,),
        in_specs=[pl.BlockSpec((h, 3 * D), lambda i: (_halo_before(i, tr, h), 0)),
                  pl.BlockSpec((tr, 3 * D), lambda i: (i, 0)),
                  pl.BlockSpec((h, 3 * D), lambda i: (_halo_after(i, tr, h, s), 0)),
                  pl.BlockSpec((tr, D), lambda i: (i, 0)),
                  pl.BlockSpec((h, D), lambda i: (_halo_after(i, tr, h, s), 0)),
                  pl.BlockSpec((8, D), lambda i: (0, 0)), ANY],
        out_specs=[pl.BlockSpec((tr, 3 * D), lambda i: (i, 0)), pl.BlockSpec((8, D), lambda i: (0, 0))],
        out_shape=[SDS(dproj.shape, BF), SDS((8, D), F32)], input_output_aliases={6: 0},
        scratch_shapes=[pltpu.VMEM((h + tr, D), F32), pltpu.VMEM((tr + h, D), F32)],
        compiler_params=_cp(("arbitrary",)), name=name)(proj, proj, proj, dza, dza, cw, dproj)


_U_COL, _UG_COL = 3, 4


def _brb_conv_fwd(proj, cw, bias, *, name):
    s = proj.shape[0]
    tr, h = min(TR, s), H_L

    def body(u_cur, ug_cur, u_halo, ug_halo, w_ref, b_ref, cb_ref, glb, shifted):
        i = pl.program_id(0)
        glb[0:h, :] = jnp.where(i == 0, 0.0, u_halo[...].astype(F32) * _sigmoid(ug_halo[...].astype(F32)))
        glb[h:h + tr, :] = u_cur[...].astype(F32) * _sigmoid(ug_cur[...].astype(F32))
        for c in range(D // LANE):
            sl = slice(LANE * c, LANE * c + LANE)
            for r in range(1, 8):
                shifted[r] = glb[8 - r:8 - r + tr + 24, sl]
            acc = None
            for k in range(K_B):
                q, r = divmod(K_B - 1 - k, 8)
                win = shifted[r, 24 - 8 * q:24 - 8 * q + tr, :] if r else glb[h - 8 * q:h - 8 * q + tr, sl]
                term = w_ref[k:k + 1, sl] * win
                acc = term if acc is None else acc + term
            cb_ref[:, sl] = (acc + b_ref[:, sl]).astype(BF)

    return _pcall(
        body, grid=(s // tr,),
        in_specs=[pl.BlockSpec((tr, D), lambda i: (i, _U_COL)), pl.BlockSpec((tr, D), lambda i: (i, _UG_COL)),
                  pl.BlockSpec((h, D), lambda i: (_halo_before(i, tr, h), _U_COL)),
                  pl.BlockSpec((h, D), lambda i: (_halo_before(i, tr, h), _UG_COL)),
                  pl.BlockSpec((32, D), lambda i: (0, 0)), pl.BlockSpec((1, D), lambda i: (0, 0))],
        out_specs=pl.BlockSpec((tr, D), lambda i: (i, 0)), out_shape=SDS((s, D), BF),
        scratch_shapes=[pltpu.VMEM((h + tr, D), F32), pltpu.VMEM((8, tr + 24, LANE), F32)],
        compiler_params=_cp(("parallel",)), name=name)(proj, proj, proj, proj, cw, bias)


def _brb_conv_bwd(proj, dcb, dq, cw, dproj, *, name):
    s = proj.shape[0]
    tr, h = min(TR, s), H_L
    n = s // tr

    def body(u_cur, ug_cur, d_cur, d_after, dq_ref, w_ref, dproj_in, db_ref, dw_ref, dcbb, shifted):
        del dproj_in
        i = pl.program_id(0)

        @pl.when(i == 0)
        def _():
            dw_ref[...] = jnp.zeros_like(dw_ref)

        db_ref[:, 2 * D:3 * D] = dq_ref[...]
        dcbb[0:tr, :] = d_cur[...].astype(F32)
        dcbb[tr:tr + h, :] = jnp.where(i == n - 1, 0.0, d_after[...].astype(F32))
        for c in range(D // LANE):
            sl = slice(LANE * c, LANE * c + LANE)
            u = u_cur[:, sl].astype(F32)
            sg = _sigmoid(ug_cur[:, sl].astype(F32))
            glu = u * sg
            for r in range(1, 8):
                shifted[r] = dcbb[r:r + tr + 24, sl]
            dglu = None
            for k in range(K_B):
                q, r = divmod(K_B - 1 - k, 8)
                win = shifted[r, 8 * q:8 * q + tr, :] if r else dcbb[8 * q:8 * q + tr, sl]
                term = w_ref[k:k + 1, sl] * win
                dglu = term if dglu is None else dglu + term
                dw_ref[k:k + 1, sl] += jnp.sum(win * glu, axis=0, keepdims=True)
            db_ref[:, sl] = (dglu * sg).astype(BF)
            db_ref[:, D + LANE * c:D + LANE * c + LANE] = (dglu * u * sg * (1.0 - sg)).astype(BF)

    return _pcall(
        body, grid=(n,),
        in_specs=[pl.BlockSpec((tr, D), lambda i: (i, _U_COL)), pl.BlockSpec((tr, D), lambda i: (i, _UG_COL)),
                  pl.BlockSpec((tr, D), lambda i: (i, 0)),
                  pl.BlockSpec((h, D), lambda i: (_halo_after(i, tr, h, s), 0)),
                  pl.BlockSpec((tr, D), lambda i: (i, 0)),
                  pl.BlockSpec((32, D), lambda i: (0, 0)), ANY],
        out_specs=[pl.BlockSpec((tr, 3 * D), lambda i: (i, 1)), pl.BlockSpec((32, D), lambda i: (0, 0))],
        out_shape=[SDS(dproj.shape, BF), SDS((32, D), F32)], input_output_aliases={6: 0},
        scratch_shapes=[pltpu.VMEM((tr + h, D), F32), pltpu.VMEM((8, tr + 24, LANE), F32)],
        compiler_params=_cp(("arbitrary",)), name=name)(proj, proj, dcb, dcb, dq, cw, dproj)


def _ln_silu_fwd(cb, g, b, *, name):
    s = cb.shape[0]
    tm = min(TM, s)

    def body(cb_ref, g_ref, b_ref, sb_ref):
        z = cb_ref[...].astype(F32)
        zc = z - jnp.mean(z, axis=-1, keepdims=True)
        ln = (zc * lax.rsqrt(jnp.mean(zc * zc, axis=-1, keepdims=True) + NORM_EPS)) * g_ref[...] + b_ref[...]
        sb_ref[...] = (ln * _sigmoid(ln)).astype(BF)

    row = lambda i: (i, 0)
    vec = pl.BlockSpec((1, D), lambda i: (0, 0))
    return _pcall(
        body, grid=(s // tm,), in_specs=[pl.BlockSpec((tm, D), row), vec, vec],
        out_specs=pl.BlockSpec((tm, D), row), out_shape=SDS((s, D), BF),
        compiler_params=_cp(("parallel",)), name=name)(cb, g, b)


def _ln_silu_bwd(cb, dsb, g, b, *, name):
    s = cb.shape[0]
    tm = min(TM, s)

    def body(cb_ref, dsb_ref, g_ref, b_ref, dcb_ref, sums_ref):
        @pl.when(pl.program_id(0) == 0)
        def _():
            sums_ref[...] = jnp.zeros_like(sums_ref)

        z = cb_ref[...].astype(F32)
        zc = z - jnp.mean(z, axis=-1, keepdims=True)
        rstd = lax.rsqrt(jnp.mean(zc * zc, axis=-1, keepdims=True) + NORM_EPS)
        lnh = zc * rstd
        ln = lnh * g_ref[...] + b_ref[...]
        sg = _sigmoid(ln)
        dln = dsb_ref[...].astype(F32) * (sg * (1.0 + ln * (1.0 - sg)))
        sums_ref[0:1, :] += jnp.sum(dln * lnh, axis=0, keepdims=True)
        sums_ref[1:2, :] += jnp.sum(dln, axis=0, keepdims=True)
        dlnh = dln * g_ref[...]
        dz = rstd * (dlnh - jnp.mean(dlnh, axis=-1, keepdims=True)
                     - lnh * jnp.mean(dlnh * lnh, axis=-1, keepdims=True))
        sums_ref[2:3, :] += jnp.sum(dz, axis=0, keepdims=True)
        dcb_ref[...] = dz.astype(BF)

    row = lambda i: (i, 0)
    vec = pl.BlockSpec((1, D), lambda i: (0, 0))
    return _pcall(
        body, grid=(s // tm,), in_specs=[pl.BlockSpec((tm, D), row), pl.BlockSpec((tm, D), row), vec, vec],
        out_specs=[pl.BlockSpec((tm, D), row), pl.BlockSpec((8, D), lambda i: (0, 0))],
        out_shape=[SDS((s, D), BF), SDS((8, D), F32)],
        compiler_params=_cp(("arbitrary",)), name=name)(cb, dsb, g, b)


_Q_COL = 5 * D // HEAD


def _kv_prep(mem, g, wkv, *, name):
    m = mem.shape[0]

    def body(mem_ref, g_ref, w_ref, memn_ref, kv_ref):
        mv = mem_ref[...]
        memn = ((mv * _rms(mv)) * g_ref[...]).astype(BF)
        memn_ref[...] = memn
        for dev in range(N_DEV):
            kv_ref[:, dev * C_KV:(dev + 1) * C_KV] = _dot(memn, w_ref[dev]).astype(BF)

    return _pcall(body, out_shape=[SDS((m, D), BF), SDS((m, 2 * D), BF)],
                  compiler_params=_cp(), name=name)(mem, g, wkv)


def _softmax_rows(q, k):
    sc = _dot_nt(q, k) * (1.0 / (HEAD ** 0.5))
    e = jnp.exp(sc - jnp.max(sc, axis=-1, keepdims=True))
    return e / jnp.sum(e, axis=-1, keepdims=True)


def _attn_fwd(proj, kv, *, name):
    s, m = proj.shape[0], kv.shape[0]
    tm = min(TM, s)

    def body(q_ref, k_ref, v_ref, o_ref):
        p = _softmax_rows(q_ref[...], k_ref[...])
        o_ref[...] = _dot(p.astype(BF), v_ref[...]).astype(BF)

    return _pcall(
        body, grid=(s // tm, N_HEADS),
        in_specs=[pl.BlockSpec((tm, HEAD), lambda i, hd: (i, _Q_COL + hd)),
                  pl.BlockSpec((m, HEAD), lambda i, hd: (0, hd)),
                  pl.BlockSpec((m, HEAD), lambda i, hd: (0, N_HEADS + hd))],
        out_specs=pl.BlockSpec((tm, HEAD), lambda i, hd: (i, hd)), out_shape=SDS((s, D), BF),
        compiler_params=_cp(("parallel", "parallel")), name=name)(proj, kv, kv)


def _attn_bwd(proj, kv, do, *, name):
    s, m = proj.shape[0], kv.shape[0]
    tm = min(TM, s)

    def body(q_ref, k_ref, v_ref, do_ref, dq_ref, dk_ref, dv_ref):
        @pl.when(pl.program_id(1) == 0)
        def _():
            dk_ref[...] = jnp.zeros_like(dk_ref)
            dv_ref[...] = jnp.zeros_like(dv_ref)

        q, k, dov = q_ref[...], k_ref[...], do_ref[...]
        p = _softmax_rows(q, k)
        dp = _dot_nt(dov, v_ref[...])
        dv_ref[...] += _dot_tn(p.astype(BF), dov)
        ds = (p * (dp - jnp.sum(dp * p, axis=-1, keepdims=True)) * (1.0 / (HEAD ** 0.5))).astype(BF)
        dq_ref[...] = _dot(ds, k).astype(BF)
        dk_ref[...] += _dot_tn(ds, q)

    return _pcall(
        body, grid=(N_HEADS, s // tm),
        in_specs=[pl.BlockSpec((tm, HEAD), lambda hd, i: (i, _Q_COL + hd)),
                  pl.BlockSpec((m, HEAD), lambda hd, i: (0, hd)),
                  pl.BlockSpec((m, HEAD), lambda hd, i: (0, N_HEADS + hd)),
                  pl.BlockSpec((tm, HEAD), lambda hd, i: (i, hd))],
        out_specs=[pl.BlockSpec((tm, HEAD), lambda hd, i: (i, hd)),
                   pl.BlockSpec((m, HEAD), lambda hd, i: (0, hd)),
                   pl.BlockSpec((m, HEAD), lambda hd, i: (0, hd))],
        out_shape=[SDS((s, D), BF), SDS((m, D), F32), SDS((m, D), F32)],
        compiler_params=_cp(("parallel", "arbitrary")), name=name)(proj, kv, kv, do)


def _kv_bwd(mem, g, memn, dk, dv, wkv, *, name):
    def body(mem_ref, g_ref, memn_ref, dk_ref, dv_ref, w_ref, dw_ref, dg_ref):
        memn = memn_ref[...]
        dmemn = None
        for dev in range(N_DEV):
            d_ref, col = (dk_ref, dev) if dev < N_HEADS else (dv_ref, dev - N_HEADS)
            dslab = d_ref[:, col * C_KV:(col + 1) * C_KV].astype(BF)
            dw_ref[dev] = _dot_tn(memn, dslab).astype(BF)
            part = _dot_nt(dslab, w_ref[dev])
            dmemn = part if dmemn is None else dmemn + part
        mv = mem_ref[...]
        dg_ref[...] = jnp.zeros_like(dg_ref)
        dg_ref[0:1, :] = jnp.sum(dmemn * (mv * _rms(mv)), axis=0, keepdims=True)

    assert C_KV == HEAD
    return _pcall(body, out_shape=[SDS((N_DEV, D, C_KV), BF), SDS((8, D), F32)],
                  compiler_params=_cp(), name=name)(mem, g, memn, dk, dv, wkv)


_TM_MIX = 256


def _mix_out(x, za, sb, o, proj, w4, bg, g_next, *, name):
    s = x.shape[0]
    tm = min(_TM_MIX, s)

    def body(x_ref, za_ref, sb_ref, o_ref, pg_ref, w4_ref, bg_ref, gn_ref,
             ya_ref, yb_ref, yc_ref, mg_ref, x1_ref, h_ref):
        ys = (_dot(za_ref[...], w4_ref[0]), _dot(sb_ref[...], w4_ref[1]), _dot(o_ref[...], w4_ref[2]))
        merged = None
        for j, (y, y_ref) in enumerate(zip(ys, (ya_ref, yb_ref, yc_ref))):
            y_ref[...] = y.astype(BF)
            gate = _sigmoid(pg_ref[:, j * D:(j + 1) * D].astype(F32) + bg_ref[:, j * D:(j + 1) * D])
            merged = gate * y if merged is None else merged + gate * y
        mg = merged.astype(BF)
        mg_ref[...] = mg
        x1 = x_ref[...] + _dot(mg, w4_ref[3])
        x1_ref[...] = x1
        h_ref[...] = ((x1 * _rms(x1)) * gn_ref[...]).astype(BF)

    row = lambda i: (i, 0)
    act = pl.BlockSpec((tm, D), row)
    return _pcall(
        body, grid=(s // tm,),
        in_specs=[act, act, act, act, pl.BlockSpec((tm, 3 * D), lambda i: (i, 2)),
                  pl.BlockSpec((4, D, D), lambda i: (0, 0, 0)), pl.BlockSpec((1, 3 * D), lambda i: (0, 0)),
                  pl.BlockSpec((1, D), lambda i: (0, 0))],
        out_specs=[act] * 6,
        out_shape=[SDS((s, D), BF)] * 4 + [SDS((s, D), F32), SDS((s, D), BF)],
        compiler_params=_cp(("parallel",)), name=name)(x, za, sb, o, proj, w4, bg, g_next)


def _mix_bwd(dxb, ya, yb, yc, proj, w4, bg, *, name):
    s = dxb.shape[0]
    tm = min(_TM_MIX, s)

    def body(dx_ref, ya_ref, yb_ref, yc_ref, pg_ref, w4_ref, bg_ref,
             dya_ref, dyb_ref, dyc_ref, dza_ref, dsb_ref, do_ref, dgt_ref, dbg_ref):
        @pl.when(pl.program_id(0) == 0)
        def _():
            dbg_ref[...] = jnp.zeros_like(dbg_ref)

        dm = _dot_nt(dx_ref[...], w4_ref[3])
        for j, (y_ref, dy_ref, din_ref) in enumerate(zip((ya_ref, yb_ref, yc_ref), (dya_ref, dyb_ref, dyc_ref),
                                                         (dza_ref, dsb_ref, do_ref))):
            cols = slice(j * D, (j + 1) * D)
            gate = _sigmoid(pg_ref[:, cols].astype(F32) + bg_ref[:, cols])
            dy = (dm * gate).astype(BF)
            dy_ref[...] = dy
            din_ref[...] = _dot_nt(dy, w4_ref[j]).astype(BF)
            dpre = dm * y_ref[...].astype(F32) * gate * (1.0 - gate)
            dgt_ref[:, cols] = dpre.astype(BF)
            dbg_ref[0:1, cols] += jnp.sum(dpre, axis=0, keepdims=True)

    row = lambda i: (i, 0)
    act = pl.BlockSpec((tm, D), row)
    return _pcall(
        body, grid=(s // tm,),
        in_specs=[act, act, act, act, pl.BlockSpec((tm, 3 * D), lambda i: (i, 2)),
                  pl.BlockSpec((4, D, D), lambda i: (0, 0, 0)), pl.BlockSpec((1, 3 * D), lambda i: (0, 0))],
        out_specs=[act] * 6 + [pl.BlockSpec((tm, 3 * D), lambda i: (i, 2)),
                               pl.BlockSpec((8, 3 * D), lambda i: (0, 0))],
        out_shape=[SDS((s, D), BF)] * 6 + [SDS((s, 9 * D), BF), SDS((8, 3 * D), F32)],
        compiler_params=_cp(("arbitrary",)), name=name)(dxb, ya, yb, yc, proj, w4, bg)


_PAIR = 2 * C_UP_P


def _ffn_act(u2, cw, *, name):
    s = u2.shape[0]
    tr, h = min(TR, s), H_S

    def body(cur, halo, w_ref, act_ref, c2_ref, ub):
        i = pl.program_id(1)
        ub[0:h, :] = jnp.where(i == 0, 0.0, halo[...].astype(F32))
        ub[h:h + tr, :] = cur[...].astype(F32)
        for c in range(C_UP_P // LANE):
            gl = slice(LANE * c, LANE * c + LANE)
            ul = slice(C_UP_P + LANE * c, C_UP_P + LANE * c + LANE)
            gt = _taps(ub, w_ref, gl, K_F, h - (K_F - 1), tr)
            up = _taps(ub, w_ref, ul, K_F, h - (K_F - 1), tr)
            c2_ref[:, gl] = gt.astype(BF)
            c2_ref[:, ul] = up.astype(BF)
            act_ref[:, gl] = (gt * _sigmoid(gt) * up).astype(BF)

    return _pcall(
        body, grid=(4, s // tr),
        in_specs=[pl.BlockSpec((tr, _PAIR), lambda p, i: (i, p)),
                  pl.BlockSpec((h, _PAIR), lambda p, i: (_halo_before(i, tr, h), p)),
                  pl.BlockSpec((8, _PAIR), lambda p, i: (0, p))],
        out_specs=[pl.BlockSpec((tr, C_UP_P), lambda p, i: (i, p)), pl.BlockSpec((tr, _PAIR), lambda p, i: (i, p))],
        out_shape=[SDS((s, FF_P), BF), SDS((s, 2 * FF_P), BF)],
        scratch_shapes=[pltpu.VMEM((h + tr, _PAIR), F32)],
        compiler_params=_cp(("parallel", "parallel")), name=name)(u2, u2, cw)


def _ffn_bwd(u2, c2, dact, cw, *, name):
    s = u2.shape[0]
    tr, h = min(TR, s), H_S
    n = s // tr
    ext = tr + h

    def body(u_cur, c_cur, c_after, da_cur, da_after, w_ref, du_ref, dw_ref, dcb):
        i = pl.program_id(1)
        last = i == n - 1

        @pl.when(i == 0)
        def _():
            dw_ref[...] = jnp.zeros_like(dw_ref)

        def rows(cur_ref, after_ref, sl):
            return jnp.concatenate([cur_ref[:, sl].astype(F32),
                                    jnp.where(last, 0.0, after_ref[:, sl].astype(F32))], axis=0)

        for c in range(C_UP_P // LANE):
            gl = slice(LANE * c, LANE * c + LANE)
            ul = slice(C_UP_P + LANE * c, C_UP_P + LANE * c + LANE)
            gt, up, da = rows(c_cur, c_after, gl), rows(c_cur, c_after, ul), rows(da_cur, da_after, gl)
            sg = _sigmoid(gt)
            dcb[:, gl] = da * up * (sg * (1.0 + gt * (1.0 - sg)))
            dcb[:, ul] = da * (gt * sg)
        for c in range(_PAIR // LANE):
            sl = slice(LANE * c, LANE * c + LANE)
            u = u_cur[:, sl].astype(F32)
            du = None
            for k in range(K_F):
                win = dcb[K_F - 1 - k:K_F - 1 - k + tr, sl]
                term = w_ref[k:k + 1, sl] * win
                du = term if du is None else du + term
                dw_ref[k:k + 1, sl] += jnp.sum(win * u, axis=0, keepdims=True)
            du_ref[:, sl] = du.astype(BF)

    return _pcall(
        body, grid=(4, n),
        in_specs=[pl.BlockSpec((tr, _PAIR), lambda p, i: (i, p)),
                  pl.BlockSpec((tr, _PAIR), lambda p, i: (i, p)),
                  pl.BlockSpec((h, _PAIR), lambda p, i: (_halo_after(i, tr, h, s), p)),
                  pl.BlockSpec((tr, C_UP_P), lambda p, i: (i, p)),
                  pl.BlockSpec((h, C_UP_P), lambda p, i: (_halo_after(i, tr, h, s), p)),
                  pl.BlockSpec((8, _PAIR), lambda p, i: (0, p))],
        out_specs=[pl.BlockSpec((tr, _PAIR), lambda p, i: (i, p)), pl.BlockSpec((8, _PAIR), lambda p, i: (0, p))],
        out_shape=[SDS((s, 2 * FF_P), BF), SDS((8, 2 * FF_P), F32)],
        scratch_shapes=[pltpu.VMEM((ext, _PAIR), F32)],
        compiler_params=_cp(("parallel", "arbitrary")), name=name)(u2, c2, c2, dact, dact, cw)


def _relations():
    x, y, c = lax.axis_index("x"), lax.axis_index("y"), lax.axis_index("c")
    out = []
    for r in range(1, N_DEV):
        rx, ry, rc = (r >> 2) & 1, (r >> 1) & 1, r & 1
        out.append((r, (x ^ rx, y ^ ry, c ^ rc)))
    return out


def _my_index():
    return 4 * lax.axis_index("x") + 2 * lax.axis_index("y") + lax.axis_index("c")


def _exchange(n_arrays, src_of, dst_of, refs):
    ssem, rsem, lsem = refs
    me = _my_index()
    local = []
    for a in range(n_arrays):
        loc = pltpu.make_async_copy(src_of(a, me), dst_of(a, me), lsem.at[a])
        loc.start()
        local.append(loc)

    def copy(a, r, peer, src_idx, dst_idx):
        return pltpu.make_async_remote_copy(
            src_ref=src_of(a, src_idx), dst_ref=dst_of(a, dst_idx), send_sem=ssem.at[a, r - 1],
            recv_sem=rsem.at[a, r - 1], device_id=peer, device_id_type=MESH)

    peers = [(r, peer, 4 * peer[0] + 2 * peer[1] + peer[2]) for r, peer in _relations()]
    for r, peer, p_idx in peers:
        for a in range(n_arrays):
            copy(a, r, peer, p_idx, me).start()
    for r, peer, p_idx in peers:
        for a in range(n_arrays):
            copy(a, r, peer, p_idx, me).wait_send()
            copy(a, r, peer, me, p_idx).wait_recv()
    for loc in local:
        loc.wait()


def _sem_scratch(n_arrays):
    return [pltpu.SemaphoreType.DMA((n_arrays, N_DEV - 1)), pltpu.SemaphoreType.DMA((n_arrays, N_DEV - 1)),
            pltpu.SemaphoreType.DMA((n_arrays,))]


def _slab(kind, ref, idx):
    if kind == "win":
        return ref.at[:, pl.ds(pl.multiple_of(idx * C_IN, LANE), C_IN)]
    if kind == "wup":
        return ref.at[:, pl.ds(pl.multiple_of(_up_slot(idx) * C_UP_P, LANE), C_UP_P)]
    if kind == "wkv":
        return ref.at[idx]
    if kind == "w4":
        return ref.at[:, pl.ds(pl.multiple_of(idx * R_O, 16), R_O), :]
    if kind == "wdn":
        return ref.at[pl.ds(pl.multiple_of(_dn_row(idx), 16), R_DN), :]
    assert kind == "cv"
    return ref.at[idx]


_WHOLE = {"win": ((D, 9 * D), BF), "wup": ((D, 2 * FF_P), BF), "wkv": ((N_DEV, D, C_KV), BF),
          "w4": ((4, D, D), BF), "wdn": ((FF_P, D), BF)}
_SHARD = {"win": (D, C_IN), "wup": (D, C_UP_P), "wkv": (D, C_KV), "w4": (4, R_O, D), "wdn": (R_DN, D)}
HBM_SPEC = pl.BlockSpec(memory_space=pltpu.HBM)
SEM_SPEC = pl.BlockSpec(memory_space=pltpu.SEMAPHORE)
_DATAFLOW = pltpu.SideEffectType.DATAFLOW_SIDE_EFFECTING


def _gather_maps(kinds):
    return ((lambda srcs, lands, a, idx: _slab(kinds[a], lands[a], _my_index())),
            (lambda lands, a, idx: _slab(kinds[a], lands[a], idx)))


def _scatter_maps(kinds):
    return ((lambda srcs, lands, a, idx: _slab(kinds[a], srcs[a], idx)),
            (lambda lands, a, idx: lands[a].at[idx]))


_SLOTTED = ("wkv", "cv")


def _own_slab_blocks(kind, shard_shape):
    if kind in ("win", "wup"):
        rows, slot = 256, (_up_slot if kind == "wup" else (lambda m: m))
        return (shard_shape[0] // rows, (rows, shard_shape[1]), (lambda i, me: (i, slot(me[0]))),
                (lambda i, me: (i, 0)), (lambda i, me: (me[0], i, 0)))
    if kind == "w4":
        return (1, shard_shape, (lambda i, me: (0, me[0], 0)), (lambda i, me: (0, 0, 0)),
                (lambda i, me: (me[0], 0, 0, 0)))
    if kind == "wdn":
        rows = 32
        return (R_DN // rows, (rows, D), (lambda i, me: (_dn_row(me[0]) // rows + i, 0)), (lambda i, me: (i, 0)),
                (lambda i, me: (me[0], i, 0)))
    assert kind in _SLOTTED
    rows = min(256, shard_shape[0])
    return (shard_shape[0] // rows, (rows, shard_shape[1]), (lambda i, me: (me[0], i, 0)),
            (lambda i, me: (i, 0)), (lambda i, me: (me[0], i, 0)))


def _place_own(kind, src, out_sds, gather, me_arr, *, name):
    shard_shape = src.shape if gather else out_sds.shape[1:]
    steps, blk, whole_idx, shard_idx, staging_idx = _own_slab_blocks(kind, shard_shape)
    slotted = kind in _SLOTTED
    whole_spec = pl.BlockSpec(((None,) if slotted else ()) + tuple(blk), whole_idx)
    if gather:
        in_spec, out_spec = pl.BlockSpec(tuple(blk), shard_idx), whole_spec
    else:
        in_spec, out_spec = whole_spec, pl.BlockSpec((None,) + tuple(blk), staging_idx)
    zero_init = gather and kind == "wdn"

    def body(me_ref, src_ref, *rest):
        rest[-1][...] = src_ref[...].astype(rest[-1].dtype)

    operands = (me_arr, src) + ((jnp.zeros(out_sds.shape, out_sds.dtype),) if zero_init else ())
    return _pcall(
        body,
        grid_spec=pltpu.PrefetchScalarGridSpec(
            num_scalar_prefetch=1, grid=(steps,), in_specs=[in_spec] + ([ANY] if zero_init else []),
            out_specs=out_spec),
        out_shape=out_sds, input_output_aliases={2: 0} if zero_init else {},
        compiler_params=_cp(("arbitrary",)), name=name)(*operands)


def _peer_copies(n, src_of, dst_of, src_r, land_r, ssem, rsem):
    me = _my_index()
    out = []
    for r, peer in _relations():
        p_idx = 4 * peer[0] + 2 * peer[1] + peer[2]
        for a in range(n):
            def copy(src_idx, dst_idx, a=a, r=r, peer=peer):
                sem = a * (N_DEV - 1) + r - 1
                return pltpu.make_async_remote_copy(
                    src_ref=src_of(src_r, land_r, a, src_idx), dst_ref=dst_of(land_r, a, dst_idx),
                    send_sem=ssem.at[sem], recv_sem=rsem.at[sem], device_id=peer, device_id_type=MESH)
            out.append((functools.partial(copy, p_idx, me), functools.partial(copy, me, p_idx)))
    return out


def _exchange_start(srcs, lands, maps, after, *, name):
    n, ns = len(lands), len(srcs)
    src_of, dst_of = maps

    def body(*refs):
        src_r, land_r = refs[:ns], refs[ns:ns + n]
        ssem, rsem, token = refs[ns + n + 1], refs[ns + n + 2], refs[-1]
        for send, _ in _peer_copies(n, src_of, dst_of, src_r, land_r, ssem, rsem):
            send().start()
        token[...] = jnp.zeros_like(token)

    flight = list(srcs) + list(lands)
    outs = pl.pallas_call(
        body, name=name,
        out_shape=(pltpu.SemaphoreType.DMA((n * (N_DEV - 1),)), pltpu.SemaphoreType.DMA((n * (N_DEV - 1),)),
                   *[pltpu.HBM(a.shape, a.dtype) for a in flight], SDS((8, LANE), F32)),
        in_specs=[HBM_SPEC] * (ns + n) + [ANY],
        out_specs=(SEM_SPEC, SEM_SPEC, *[HBM_SPEC] * (ns + n), pl.BlockSpec(memory_space=pltpu.VMEM)),
        input_output_aliases={i: 2 + i for i in range(ns + n)},
        compiler_params=pltpu.CompilerParams(has_side_effects=_DATAFLOW),
    )(*[pltpu.with_memory_space_constraint(a, pltpu.HBM) for a in flight], after)
    return (outs[0], outs[1], list(outs[2:2 + ns + n]), ns), outs[-1]


def _exchange_wait(handle, maps, after, *, name):
    ssem, rsem, flight, ns = handle
    n = len(flight) - ns
    src_of, dst_of = maps

    def body(*refs):
        src_r, land_r, ssem_r, rsem_r = refs[:ns], refs[ns:ns + n], refs[ns + n], refs[ns + n + 1]
        for send, arrival in _peer_copies(n, src_of, dst_of, src_r, land_r, ssem_r, rsem_r):
            send().wait_send()
            arrival().wait_recv()

    outs = pl.pallas_call(
        body, name=name, out_shape=[pltpu.HBM(a.shape, a.dtype) for a in flight],
        in_specs=[HBM_SPEC] * (ns + n) + [SEM_SPEC, SEM_SPEC, ANY], out_specs=[HBM_SPEC] * (ns + n),
        input_output_aliases={i: i for i in range(ns + n)},
        compiler_params=pltpu.CompilerParams(has_side_effects=_DATAFLOW),
    )(*flight, ssem, rsem, after)
    return list(outs[ns:])


def _allreduce_small(pack, *, name):
    rows = pack.shape[0]

    def body(p_ref, out_ref, gath, ssem, rsem, lsem):
        _exchange(1, lambda a, idx: p_ref, lambda a, idx: gath.at[idx], (ssem, rsem, lsem))
        total = gath[0]
        for d in range(1, N_DEV):
            total = total + gath[d]
        out_ref[...] = total

    vm = pl.BlockSpec(memory_space=pltpu.VMEM)
    return _pcall(
        body, in_specs=[vm], out_specs=vm, out_shape=SDS(pack.shape, F32),
        scratch_shapes=[pltpu.VMEM((N_DEV, rows, pack.shape[1]), F32)] + _sem_scratch(1),
        compiler_params=_cp(has_side_effects=True), name=name)(pack)


def _adam(g, w, m, v):
    nm = ADAM_B1 * m + (1.0 - ADAM_B1) * g
    nv = ADAM_B2 * v + (1.0 - ADAM_B2) * (g * g)
    m_hat = nm / (1.0 - ADAM_B1 ** ADAM_STEP)
    v_hat = nv / (1.0 - ADAM_B2 ** ADAM_STEP)
    return -ADAM_LR * (m_hat / (jnp.sqrt(v_hat) + ADAM_EPS) + ADAM_WD * w), nm, nv


def _adamw_staged(st0, st1, w, m, v, *, name):
    _, rows, cols = w.shape
    tr = max(t for t in range(16, 129, 16) if rows % t == 0)
    nr = rows // tr

    def body(s0_ref, s1_ref, w_ref, m_ref, v_ref, g_ref, d_ref, nm_ref, nv_ref):
        for layer, s_ref in enumerate((s0_ref, s1_ref)):
            @pl.when(pl.program_id(0) == layer)
            def _(s_ref=s_ref):
                total = s_ref[0].astype(F32)
                for dev in range(1, N_DEV):
                    total = total + s_ref[dev].astype(F32)
                g_ref[0] = total

        d_ref[0], nm_ref[0], nv_ref[0] = _adam(g_ref[0], w_ref[0], m_ref[0], v_ref[0])

    st_spec = lambda layer: pl.BlockSpec(
        (N_DEV, tr, cols), lambda l, i: (0, jnp.where(l == layer, i, (nr - 1) * (1 - layer)), 0))
    par = pl.BlockSpec((1, tr, cols), lambda l, i: (l, i, 0))
    return _pcall(
        body, grid=(DEPTH, nr), in_specs=[st_spec(0), st_spec(1), par, par, par], out_specs=[par] * 4,
        out_shape=[SDS(w.shape, F32)] * 4,
        compiler_params=_cp(("arbitrary", "arbitrary")), name=name)(st0, st1, w, m, v)


def _adamw_small(g, w, m, v, *, name):
    def body(g_ref, w_ref, m_ref, v_ref, d_ref, nm_ref, nv_ref):
        d_ref[...], nm_ref[...], nv_ref[...] = _adam(g_ref[...], w_ref[...], m_ref[...], v_ref[...])

    return _pcall(body, out_shape=[SDS(g.shape, F32)] * 3, compiler_params=_cp(), name=name)(g, w, m, v)


def _pack_rows(arrays):
    flat = jnp.concatenate([a.reshape(-1).astype(F32) for a in arrays])
    rows = -(-flat.shape[0] // (8 * D)) * 8
    return jnp.pad(flat, (0, rows * D - flat.shape[0])).reshape(rows, D)


def _unpack_rows(pack, like):
    flat = pack.reshape(-1)
    out, at = [], 0
    for a in like:
        out.append(flat[at:at + a.size].reshape(a.shape))
        at += a.size
    return out


def _layer_fwd(x, h, mem, win, rest_of_weights, small, g_next, tag):
    proj = _mm(h, win, tm=1024, tn=1536, name=f"proj_{tag}")
    wup, wkv, w4, wdn, cw_a, cw_b, cw_f = rest_of_weights(proj)
    za = _bra_fwd(proj, cw_a, name=f"bra_fwd_{tag}")
    cb = _brb_conv_fwd(proj, cw_b, small["conv_b_bias"], name=f"brb_conv_fwd_{tag}")
    sb = _ln_silu_fwd(cb, small["ln_b_g"], small["ln_b_b"], name=f"ln_silu_fwd_{tag}")
    memn, kv = _kv_prep(mem, small["norm_mem_g"], wkv, name=f"kv_prep_{tag}")
    o = _attn_fwd(proj, kv, name=f"attn_fwd_{tag}")
    ya, yb, yc, mg, x1, h2 = _mix_out(x, za, sb, o, proj, w4, small["b_gate"], small["norm_ffn_g"],
                                      name=f"mix_out_{tag}")
    u2 = _mm(h2, wup, tm=1024, tn=1536, name=f"up_{tag}")
    act, c2 = _ffn_act(u2, cw_f, name=f"ffn_act_{tag}")
    x2, h_next = _mm_res_norm(act, wdn, x1, g_next, name=f"down_{tag}")
    saved = dict(x=x, h=h, proj=proj, za=za, cb=cb, sb=sb, memn=memn, kv=kv, o=o, ya=ya, yb=yb, yc=yc,
                 mg=mg, x1=x1, h2=h2, u2=u2, c2=c2, act=act)
    return x2, h_next, (win, wup, wkv, w4, wdn, cw_a, cw_b, cw_f), saved


def _behind(operand, token):
    return operand + token[0:1, 0:1]


def _layer_bwd(dx2, dx2b, mem, wts, small, sv, start, tag):
    win, wup, wkv, w4, wdn, cw_a, cw_b, cw_f = wts
    dact = _mm(dx2b, wdn, tb=True, tm=1024, tn=768, name=f"d_act_{tag}")
    dwdn = _mm(sv["act"], dx2b, ta=True, tm=768, tn=1024, name=f"dw_down_{tag}")
    du2, dcw_f = _ffn_bwd(sv["u2"], sv["c2"], dact, cw_f, name=f"ffn_bwd_{tag}")
    dwup = _mm(sv["h2"], du2, ta=True, tm=1024, tn=768, name=f"dw_up_{tag}")
    token = start(("wdn", "wup"), (dwdn, dwup), f"ffn_{tag}")
    dx1, dx1b, dg_ffn = _mm_nt_normbwd(du2, wup, sv["x1"], dx2, _behind(small["norm_ffn_g"], token),
                                       tk=3072, name=f"d_h2_{tag}")

    dya, dyb, dyc, dza, dsb, do, dproj, dbg = _mix_bwd(dx1b, sv["ya"], sv["yb"], sv["yc"], sv["proj"], w4,
                                                      small["b_gate"], name=f"mix_bwd_{tag}")
    dw4 = jnp.stack([
        _mm(a, b, ta=True, tm=1024, tn=512, name=f"dw_{nm}_{tag}")
        for nm, a, b in (("a_out", sv["za"], dya), ("b_out", sv["sb"], dyb), ("att_out", sv["o"], dyc),
                         ("o", sv["mg"], dx1b))])
    dq, dk, dv = _attn_bwd(sv["proj"], sv["kv"], do, name=f"attn_bwd_{tag}")
    dwkv, dg_mem = _kv_bwd(mem, small["norm_mem_g"], sv["memn"], dk, dv, wkv, name=f"kv_bwd_{tag}")
    token = start(("w4", "wkv"), (dw4, dwkv), f"mix_{tag}")
    dproj, dcw_a = _bra_bwd(sv["proj"], dza, _behind(cw_a, token), dproj, name=f"bra_bwd_{tag}")
    dcb, ln_sums = _ln_silu_bwd(sv["cb"], dsb, small["ln_b_g"], small["ln_b_b"], name=f"ln_silu_bwd_{tag}")
    dproj, dcw_b = _brb_conv_bwd(sv["proj"], dcb, dq, cw_b, dproj, name=f"brb_conv_bwd_{tag}")
    dwin = _mm(sv["h"], dproj, ta=True, tm=1024, tn=768, name=f"dw_in_{tag}")
    token = start(("win",), (dwin,), f"in_{tag}")
    dx, dxb, dg_mix = _mm_nt_normbwd(dproj, win, sv["x"], dx1, _behind(small["norm_mix_g"], token),
                                     tk=3072, name=f"d_h_{tag}")

    small_grads = [dg_mix[0:1], dg_mem[0:1], dbg[0:1].reshape(3, D), ln_sums[2:3], ln_sums[0:1], ln_sums[1:2],
                   dg_ffn[0:1], dcw_a[0:K_A], dcw_b[0:K_B], dcw_f[0:K_F].reshape(K_F * 2 * FF_P // D, D)]
    return dx, dxb, small_grads, token


_SMALL_ROWS = (1, 1, 3, 1, 1, 1, 1, K_A, K_B, K_F * 2 * FF_P // D)
_CV_ROWS = 48


def kernel(x, mem, norm_mix_g, norm_mem_g, w_in, b_gate, conv_a_w, w_a_out, conv_b_w, conv_b_bias, ln_b_g, ln_b_b, w_b_out, w_kv, w_att_out, w_o, norm_ffn_g, w_up, conv_ffn_w, w_down, norm_final_g, loss_target, m_norm_mix_g, m_norm_mem_g, m_w_in, m_b_gate, m_conv_a_w, m_w_a_out, m_conv_b_w, m_conv_b_bias, m_ln_b_g, m_ln_b_b, m_w_b_out, m_w_kv, m_w_att_out, m_w_o, m_norm_ffn_g, m_w_up, m_conv_ffn_w, m_w_down, m_norm_final_g, v_norm_mix_g, v_norm_mem_g, v_w_in, v_b_gate, v_conv_a_w, v_w_a_out, v_conv_b_w, v_conv_b_bias, v_ln_b_g, v_ln_b_b, v_w_b_out, v_w_kv, v_w_att_out, v_w_o, v_norm_ffn_g, v_w_up, v_conv_ffn_w, v_w_down, v_norm_final_g):
    me = _my_index()
    me_arr = me.astype(jnp.int32).reshape(1)
    x0, mem0, tgt = x.reshape(x.shape[1:]), mem.reshape(mem.shape[1:]), loss_target.reshape(x.shape[1:])
    up_pad = ((0, 0), (0, 0), (0, C_UP_P - C_UP))

    ag_groups = (("win",), ("wup", "wkv", "w4", "wdn", "cv"))
    kinds = ag_groups[0] + ag_groups[1]
    smalls, ag_handles = [], []
    token = jnp.zeros((8, LANE), F32)
    for l in range(DEPTH):
        cv = jnp.zeros((_CV_ROWS, C_UP_P), F32)
        cv = cv.at[0:K_F, 0:C_UP].set(conv_ffn_w[l]).at[3:3 + K_A, 0:R_O].set(conv_a_w[l])
        cv = cv.at[8:8 + K_B, 0:R_O].set(conv_b_w[l])
        shards = dict(
            win=w_in[l], wup=jnp.pad(w_up[l], up_pad[1:]), wkv=w_kv[l],
            w4=jnp.stack([w_a_out[l], w_b_out[l], w_att_out[l], w_o[l]]), wdn=w_down[l], cv=cv)
        whole = dict({k: SDS(*_WHOLE[k]) for k in kinds[:-1]}, cv=SDS((N_DEV,) + cv.shape, F32))
        lands = {k: _place_own(k, shards[k], whole[k], True, me_arr, name=f"ag_own_{k}_l{l}") for k in kinds}
        per_layer = []
        for g, grp in enumerate(ag_groups):
            handle, token = _exchange_start([], [lands[k] for k in grp], _gather_maps(grp), token,
                                            name=f"ag_start_l{l}_g{g}")
            per_layer.append(handle)
        ag_handles.append(per_layer)
        smalls.append(dict(
            norm_mix_g=norm_mix_g[l][None], norm_mem_g=norm_mem_g[l][None], b_gate=b_gate[l][None],
            conv_b_bias=conv_b_bias[l][None], ln_b_g=ln_b_g[l][None], ln_b_b=ln_b_b[l][None],
            norm_ffn_g=norm_ffn_g[l][None]))

    def rest_of_weights(l):
        def wait(after):
            wup, wkv, w4, wdn, cvg = _exchange_wait(ag_handles[l][1], _gather_maps(ag_groups[1]), after,
                                                    name=f"ag_wait_l{l}_g1")
            cw_f = jnp.stack([cvg[d, 0:K_F, :] for d in UP_ORDER], axis=1).reshape(K_F, 2 * FF_P)
            cw_a = cvg[:, 3:3 + K_A, 0:R_O].transpose(1, 0, 2).reshape(K_A, D)
            cw_b = cvg[:, 8:8 + K_B, 0:R_O].transpose(1, 0, 2).reshape(K_B, D)
            return (wup, wkv, w4, wdn, jnp.pad(cw_a, ((0, 8 - K_A), (0, 0))),
                    jnp.pad(cw_b, ((0, 32 - K_B), (0, 0))), jnp.pad(cw_f, ((0, 8 - K_F), (0, 0))))
        return wait

    wts, saved = [], []
    xs = x0
    h = _rms_fwd(xs, smalls[0]["norm_mix_g"], name="rms_fwd")
    behind = token
    for l in range(DEPTH):
        g_next = smalls[l + 1]["norm_mix_g"] if l + 1 < DEPTH else norm_final_g[None]
        (win,) = _exchange_wait(ag_handles[l][0], _gather_maps(ag_groups[0]), behind, name=f"ag_wait_l{l}_g0")
        xs, h, w_l, sv = _layer_fwd(xs, h, mem0, win, rest_of_weights(l), smalls[l], g_next, f"l{l}")
        behind = h
        wts.append(w_l)
        saved.append(sv)
    dx, dxb, head_sums = _loss_head(xs, tgt, norm_final_g[None], name="loss_head")

    rs_handles = []
    small_grads = [None] * DEPTH

    def start_scatter(grp, arrays, name):
        maps = _scatter_maps(grp)
        lands = [_place_own(k, a, SDS((N_DEV,) + _SHARD[k], BF), False, me_arr, name=f"rs_own_{k}_{name}")
                 for k, a in zip(grp, arrays)]
        handle, tok = _exchange_start(list(arrays), lands, maps, rs_handles[-1][2] if rs_handles else head_sums,
                                      name=f"rs_start_{name}")
        rs_handles.append((grp, handle, tok, name))
        return tok

    for l in reversed(range(DEPTH)):
        dx, dxb, small_grads[l], token = _layer_bwd(dx, dxb, mem0, wts[l], smalls[l], saved[l], start_scatter,
                                                    f"l{l}")

    staged = [dict() for _ in range(DEPTH)]
    for grp, handle, _, name in rs_handles[:-1]:
        staged[int(name[-1])].update(zip(grp, _exchange_wait(handle, _scatter_maps(grp), dx, name=f"rs_wait_{name}")))

    pack = jnp.concatenate(small_grads[0] + small_grads[1] + [head_sums[1:2], head_sums[0:1]], axis=0)
    pack = jnp.pad(pack, ((0, -pack.shape[0] % 8), (0, 0)))
    total = _allreduce_small(pack, name="allreduce_small")
    per_layer = sum(_SMALL_ROWS)
    parts = []
    for l in range(DEPTH):
        at, one = l * per_layer, []
        for rows in _SMALL_ROWS:
            one.append(total[at:at + rows])
            at += rows
        parts.append(one)
    g_final = total[DEPTH * per_layer]
    loss = 0.5 / D * jnp.sum(total[DEPTH * per_layer + 1])

    def both(i):
        return jnp.stack([parts[0][i], parts[1][i]])

    g_norm_mix, g_norm_mem = both(0)[:, 0], both(1)[:, 0]
    g_b_gate = both(2).reshape(DEPTH, 3 * D)
    g_cbias, g_lng, g_lnb, g_norm_ffn = both(3)[:, 0], both(4)[:, 0], both(5)[:, 0], both(6)[:, 0]
    g_conv_a = lax.dynamic_slice_in_dim(both(7), me * R_O, R_O, axis=2)
    g_conv_b = lax.dynamic_slice_in_dim(both(8), me * R_O, R_O, axis=2)
    g_conv_f = lax.dynamic_slice_in_dim(both(9).reshape(DEPTH, K_F, 2 * FF_P), _up_slot(me) * C_UP_P, C_UP, axis=2)

    small_g = [g_norm_mix, g_norm_mem, g_b_gate, g_conv_a, g_conv_b, g_cbias, g_lng, g_lnb, g_norm_ffn, g_conv_f,
               g_final]
    small_w = [norm_mix_g, norm_mem_g, b_gate, conv_a_w, conv_b_w, conv_b_bias, ln_b_g, ln_b_b, norm_ffn_g,
               conv_ffn_w, norm_final_g]
    small_m = [m_norm_mix_g, m_norm_mem_g, m_b_gate, m_conv_a_w, m_conv_b_w, m_conv_b_bias, m_ln_b_g, m_ln_b_b,
               m_norm_ffn_g, m_conv_ffn_w, m_norm_final_g]
    small_v = [v_norm_mix_g, v_norm_mem_g, v_b_gate, v_conv_a_w, v_conv_b_w, v_conv_b_bias, v_ln_b_g, v_ln_b_b,
               v_norm_ffn_g, v_conv_ffn_w, v_norm_final_g]
    upd = _adamw_small(_pack_rows(small_g), _pack_rows(small_w), _pack_rows(small_m), _pack_rows(small_v),
                       name="adamw_small")
    s_d, s_m, s_v = (_unpack_rows(p, small_w) for p in upd)
    (d_norm_mix, d_norm_mem, d_b_gate, d_conv_a, d_conv_b, d_cbias, d_lng, d_lnb, d_norm_ffn, d_conv_f,
     d_final) = s_d
    (nm_norm_mix, nm_norm_mem, nm_b_gate, nm_conv_a, nm_conv_b, nm_cbias, nm_lng, nm_lnb, nm_norm_ffn, nm_conv_f,
     nm_final) = s_m
    (nv_norm_mix, nv_norm_mem, nv_b_gate, nv_conv_a, nv_conv_b, nv_cbias, nv_lng, nv_lnb, nv_norm_ffn, nv_conv_f,
     nv_final) = s_v

    def big_update(kind, w, m, v, name):
        return _adamw_staged(staged[0][kind], staged[1][kind], w, m, v, name=name)

    r_up = [a[:, :, 0:C_UP] for a in big_update("wup", jnp.pad(w_up, up_pad), jnp.pad(m_w_up, up_pad),
                                                jnp.pad(v_w_up, up_pad), "adamw_w_up")]
    r_kv = big_update("wkv", w_kv, m_w_kv, v_w_kv, "adamw_w_kv")
    r_dn = big_update("wdn", w_down, m_w_down, v_w_down, "adamw_w_down")

    def four(a, b, c, d_):
        return jnp.stack([a, b, c, d_], axis=1).reshape(DEPTH, 4 * R_O, D)

    r_4 = _adamw_staged(
        staged[0]["w4"].reshape(N_DEV, 4 * R_O, D), staged[1]["w4"].reshape(N_DEV, 4 * R_O, D),
        four(w_a_out, w_b_out, w_att_out, w_o), four(m_w_a_out, m_w_b_out, m_w_att_out, m_w_o),
        four(v_w_a_out, v_w_b_out, v_w_att_out, v_w_o), name="adamw_w_out")
    grp, handle, _, name = rs_handles[-1]
    staged[0].update(zip(grp, _exchange_wait(handle, _scatter_maps(grp), r_4[0], name=f"rs_wait_{name}")))
    r_in = big_update("win", w_in, m_w_in, v_w_in, "adamw_w_in")
    r_a, r_b, r_att, r_o = ([a.reshape(DEPTH, 4, R_O, D)[:, j] for a in r_4] for j in range(4))

    grads = [g_norm_mix, g_norm_mem, r_in[0], g_b_gate, g_conv_a, r_a[0], g_conv_b, g_cbias, g_lng, g_lnb, r_b[0],
             r_kv[0], r_att[0], r_o[0], g_norm_ffn, r_up[0], g_conv_f, r_dn[0], g_final]
    deltas = [d_norm_mix, d_norm_mem, r_in[1], d_b_gate, d_conv_a, r_a[1], d_conv_b, d_cbias, d_lng, d_lnb, r_b[1],
              r_kv[1], r_att[1], r_o[1], d_norm_ffn, r_up[1], d_conv_f, r_dn[1], d_final]
    new_m = [nm_norm_mix, nm_norm_mem, r_in[2], nm_b_gate, nm_conv_a, r_a[2], nm_conv_b, nm_cbias, nm_lng, nm_lnb,
             r_b[2], r_kv[2], r_att[2], r_o[2], nm_norm_ffn, r_up[2], nm_conv_f, r_dn[2], nm_final]
    new_v = [nv_norm_mix, nv_norm_mem, r_in[3], nv_b_gate, nv_conv_a, r_a[3], nv_conv_b, nv_cbias, nv_lng, nv_lnb,
             r_b[3], r_kv[3], r_att[3], r_o[3], nv_norm_ffn, r_up[3], nv_conv_f, r_dn[3], nv_final]
    return (loss, dx[None], *grads, *deltas, *new_m, *new_v)
```

```python
import functools

import jax
import jax.numpy as jnp
from jax import lax
from jax.experimental import pallas as pl
from jax.experimental.pallas import tpu as pltpu

F32 = jnp.float32
BF = jnp.bfloat16
SDS = jax.ShapeDtypeStruct
MESH = pl.DeviceIdType.MESH
ANY = pl.BlockSpec(memory_space=pl.ANY)

N_DEV = 8
DEPTH = 2
D = 1024
N_HEADS = 4
HEAD = D // N_HEADS
D_FF = 2816
K_A, K_B, K_F = 3, 31, 3
NORM_EPS = 1e-6

C_IN = 9 * D // N_DEV
C_KV = 2 * D // N_DEV
C_UP = 2 * D_FF // N_DEV
LANE = 128
C_UP_P = -(-C_UP // LANE) * LANE
FF_P = 4 * C_UP_P
R_O = D // N_DEV
R_DN = D_FF // N_DEV

VMEM_LIMIT = 56 * 1024 * 1024
TM = 512
TR = 256
H_S, H_L = 16, 32

ADAM_LR, ADAM_B1, ADAM_B2, ADAM_EPS, ADAM_WD, ADAM_STEP = 0.001, 0.9, 0.999, 1e-08, 0.01, 10

UP_ORDER = (0, 4, 1, 5, 2, 6, 3, 7)


def _pcall(body, **kw):
    return pl.pallas_call(body, **kw)


def _cp(sem=None, **kw):
    return pltpu.CompilerParams(dimension_semantics=sem, vmem_limit_bytes=VMEM_LIMIT, **kw)


def _dot(a, b):
    return jnp.dot(a, b, preferred_element_type=F32)


def _dot_nt(a, b):
    return lax.dot_general(a, b, (((1,), (1,)), ((), ())), preferred_element_type=F32)


def _dot_tn(a, b):
    return lax.dot_general(a, b, (((0,), (0,)), ((), ())), preferred_element_type=F32)


def _sigmoid(z):
    return 1.0 / (1.0 + jnp.exp(-z))


def _rms(xv):
    return lax.rsqrt(jnp.mean(xv * xv, axis=-1, keepdims=True) + NORM_EPS)


def _up_slot(idx):
    return jnp.where(idx < 4, 2 * idx, 2 * (idx - 4) + 1)


def _dn_row(idx):
    return C_UP_P * (idx // 2) + R_DN * (idx % 2)


def _mm(a, b, *, ta=False, tb=False, out_dtype=BF, tm=TM, tn=512, tk=None, name):
    m, k_dim = (a.shape[1], a.shape[0]) if ta else a.shape
    n = b.shape[0] if tb else b.shape[1]
    tm, tn = min(tm, m), min(tn, n)
    tk = k_dim if tk is None else min(tk, k_dim)
    nk = k_dim // tk
    assert m % tm == 0 and n % tn == 0 and k_dim % tk == 0
    dims = (((0 if ta else 1,), (1 if tb else 0,)), ((), ()))

    def body(a_ref, b_ref, o_ref, *scratch):
        part = lax.dot_general(a_ref[...], b_ref[...], dims, preferred_element_type=F32)
        if nk == 1:
            o_ref[...] = part.astype(o_ref.dtype)
            return
        acc = scratch[0]
        k = pl.program_id(2)

        @pl.when(k == 0)
        def _():
            acc[...] = part

        @pl.when(k > 0)
        def _():
            acc[...] += part

        @pl.when(k == nk - 1)
        def _():
            o_ref[...] = acc[...].astype(o_ref.dtype)

    a_spec = pl.BlockSpec((tk, tm), lambda i, j, k: (k, i)) if ta else pl.BlockSpec((tm, tk), lambda i, j, k: (i, k))
    b_spec = pl.BlockSpec((tn, tk), lambda i, j, k: (j, k)) if tb else pl.BlockSpec((tk, tn), lambda i, j, k: (k, j))
    return _pcall(
        body, grid=(m // tm, n // tn, nk), in_specs=[a_spec, b_spec],
        out_specs=pl.BlockSpec((tm, tn), lambda i, j, k: (i, j)),
        out_shape=SDS((m, n), out_dtype),
        scratch_shapes=[pltpu.VMEM((tm, tn), F32)] if nk > 1 else [],
        compiler_params=_cp(("parallel", "parallel", "arbitrary")), name=name)(a, b)


def _mm_res_norm(a, w, x, g, *, name):
    s, k_dim = a.shape
    tm = min(TM, s)

    def body(a_ref, w_ref, x_ref, g_ref, xo_ref, h_ref):
        xo = x_ref[...] + _dot(a_ref[...], w_ref[...])
        xo_ref[...] = xo
        h_ref[...] = ((xo * _rms(xo)) * g_ref[...]).astype(BF)

    return _pcall(
        body, grid=(s // tm,),
        in_specs=[pl.BlockSpec((tm, k_dim), lambda i: (i, 0)), pl.BlockSpec((k_dim, D), lambda i: (0, 0)),
                  pl.BlockSpec((tm, D), lambda i: (i, 0)), pl.BlockSpec((1, D), lambda i: (0, 0))],
        out_specs=[pl.BlockSpec((tm, D), lambda i: (i, 0))] * 2,
        out_shape=[SDS((s, D), F32), SDS((s, D), BF)],
        compiler_params=_cp(("parallel",)), name=name)(a, w, x, g)


def _mm_nt_normbwd(da, w, x, dres, g, *, tk, name):
    s, k_dim = da.shape
    tm = min(TM, s)
    nk = k_dim // tk
    assert k_dim % tk == 0

    def body(da_ref, w_ref, x_ref, dres_ref, g_ref, dx_ref, dxb_ref, dg_ref, acc):
        i, k = pl.program_id(0), pl.program_id(1)
        part = _dot_nt(da_ref[...], w_ref[...])

        @pl.when(k == 0)
        def _():
            acc[...] = part

        @pl.when(k > 0)
        def _():
            acc[...] += part

        @pl.when((i == 0) & (k == 0))
        def _():
            dg_ref[...] = jnp.zeros_like(dg_ref)

        @pl.when(k == nk - 1)
        def _():
            dh = acc[...]
            xv = x_ref[...]
            r = _rms(xv)
            xn = xv * r
            dg_ref[0:1, :] += jnp.sum(dh * xn, axis=0, keepdims=True)
            dxn = dh * g_ref[...]
            dx = dres_ref[...] + r * (dxn - xn * jnp.mean(dxn * xn, axis=-1, keepdims=True))
            dx_ref[...] = dx
            dxb_ref[...] = dx.astype(BF)

    row = lambda i, k: (i, 0)
    return _pcall(
        body, grid=(s // tm, nk),
        in_specs=[pl.BlockSpec((tm, tk), lambda i, k: (i, k)), pl.BlockSpec((D, tk), lambda i, k: (0, k)),
                  pl.BlockSpec((tm, D), row), pl.BlockSpec((tm, D), row), pl.BlockSpec((1, D), lambda i, k: (0, 0))],
        out_specs=[pl.BlockSpec((tm, D), row), pl.BlockSpec((tm, D), row), pl.BlockSpec((8, D), lambda i, k: (0, 0))],
        out_shape=[SDS((s, D), F32), SDS((s, D), BF), SDS((8, D), F32)],
        scratch_shapes=[pltpu.VMEM((tm, D), F32)],
        compiler_params=_cp(("arbitrary", "arbitrary")), name=name)(da, w, x, dres, g)


def _rms_fwd(x, g, *, name):
    s = x.shape[0]
    tm = min(TM, s)

    def body(x_ref, g_ref, h_ref):
        xv = x_ref[...]
        h_ref[...] = ((xv * _rms(xv)) * g_ref[...]).astype(BF)

    return _pcall(
        body, grid=(s // tm,),
        in_specs=[pl.BlockSpec((tm, D), lambda i: (i, 0)), pl.BlockSpec((1, D), lambda i: (0, 0))],
        out_specs=pl.BlockSpec((tm, D), lambda i: (i, 0)), out_shape=SDS((s, D), BF),
        compiler_params=_cp(("parallel",)), name=name)(x, g)


def _loss_head(x, tgt, g, *, name):
    s = x.shape[0]
    tm = min(TM, s)

    def body(x_ref, t_ref, g_ref, dx_ref, dxb_ref, sums_ref):
        @pl.when(pl.program_id(0) == 0)
        def _():
            sums_ref[...] = jnp.zeros_like(sums_ref)

        xv = x_ref[...]
        r = _rms(xv)
        xn = xv * r
        diff = xn * g_ref[...] - t_ref[...]
        sums_ref[0:1, :] += jnp.sum(diff * diff, axis=0, keepdims=True)
        dy = diff * (1.0 / D)
        sums_ref[1:2, :] += jnp.sum(dy * xn, axis=0, keepdims=True)
        dxn = dy * g_ref[...]
        dx = r * (dxn - xn * jnp.mean(dxn * xn, axis=-1, keepdims=True))
        dx_ref[...] = dx
        dxb_ref[...] = dx.astype(BF)

    row = lambda i: (i, 0)
    return _pcall(
        body, grid=(s // tm,),
        in_specs=[pl.BlockSpec((tm, D), row), pl.BlockSpec((tm, D), row), pl.BlockSpec((1, D), lambda i: (0, 0))],
        out_specs=[pl.BlockSpec((tm, D), row), pl.BlockSpec((tm, D), row), pl.BlockSpec((8, D), lambda i: (0, 0))],
        out_shape=[SDS((s, D), F32), SDS((s, D), BF), SDS((8, D), F32)],
        compiler_params=_cp(("arbitrary",)), name=name)(x, tgt, g)


def _halo_before(i, tr, h):
    return jnp.maximum(i * (tr // h) - 1, 0)


def _halo_after(i, tr, h, s):
    return jnp.minimum((i + 1) * (tr // h), s // h - 1)


def _taps(buf, w_ref, sl, k_w, base, rows):
    acc = None
    for k in range(k_w):
        t = w_ref[k:k + 1, sl] * buf[base + k:base + k + rows, sl]
        acc = t if acc is None else acc + t
    return acc


def _taps_rev(buf, w_ref, sl, k_w, rows):
    acc = None
    for k in range(k_w):
        t = w_ref[k:k + 1, sl] * buf[k_w - 1 - k:k_w - 1 - k + rows, sl]
        acc = t if acc is None else acc + t
    return acc


def _tap_grads(dw_ref, dc, buf, sl, k_w, base, rows):
    for k in range(k_w):
        dw_ref[k:k + 1, sl] += jnp.sum(dc * buf[base + k:base + k + rows, sl], axis=0, keepdims=True)


def _bra_fwd(proj, cw, *, name):
    s = proj.shape[0]
    tr, h = min(TR, s), H_S

    def body(cur, halo, w_ref, za_ref, cvb):
        i = pl.program_id(0)
        hv = halo[:, D:2 * D].astype(F32) * halo[:, 2 * D:3 * D].astype(F32)
        cvb[0:h, :] = jnp.where(i == 0, 0.0, hv)
        cvb[h:h + tr, :] = cur[:, D:2 * D].astype(F32) * cur[:, 2 * D:3 * D].astype(F32)
        for c in range(D // LANE):
            sl = slice(LANE * c, LANE * c + LANE)
            ca = _taps(cvb, w_ref, sl, K_A, h - (K_A - 1), tr)
            za_ref[:, sl] = (cur[:, sl].astype(F32) * ca).astype(BF)

    return _pcall(
        body, grid=(s // tr,),
        in_specs=[pl.BlockSpec((tr, 3 * D), lambda i: (i, 0)),
                  pl.BlockSpec((h, 3 * D), lambda i: (_halo_before(i, tr, h), 0)),
                  pl.BlockSpec((8, D), lambda i: (0, 0))],
        out_specs=pl.BlockSpec((tr, D), lambda i: (i, 0)), out_shape=SDS((s, D), BF),
        scratch_shapes=[pltpu.VMEM((h + tr, D), F32)],
        compiler_params=_cp(("parallel",)), name=name)(proj, proj, cw)


def _bra_bwd(proj, dza, cw, dproj, *, name):
    s = proj.shape[0]
    tr, h = min(TR, s), H_S
    n = s // tr

    def body(before, cur, after, dz_cur, dz_after, w_ref, dproj_in, da_ref, dw_ref, cvb, dcab):
        del dproj_in
        i = pl.program_id(0)

        @pl.when(i == 0)
        def _():
            dw_ref[...] = jnp.zeros_like(dw_ref)

        first, last = i == 0, i == n - 1
        cvb[0:h, :] = jnp.where(first, 0.0, before[:, D:2 * D].astype(F32) * before[:, 2 * D:3 * D].astype(F32))
        cvb[h:h + tr, :] = cur[:, D:2 * D].astype(F32) * cur[:, 2 * D:3 * D].astype(F32)
        dcab[0:tr, :] = dz_cur[...].astype(F32) * cur[:, 0:D].astype(F32)
        dcab[tr:tr + h, :] = jnp.where(last, 0.0, dz_after[...].astype(F32) * after[:, 0:D].astype(F32))
        for c in range(D // LANE):
            sl = slice(LANE * c, LANE * c + LANE)
            gc = cur[:, D + LANE * c:D + LANE * c + LANE].astype(F32)
            v = cur[:, 2 * D + LANE * c:2 * D + LANE * c + LANE].astype(F32)
            ca = _taps(cvb, w_ref, sl, K_A, h - (K_A - 1), tr)
            da_ref[:, sl] = (dz_cur[:, sl].astype(F32) * ca).astype(BF)
            dcv = _taps_rev(dcab, w_ref, sl, K_A, tr)
            da_ref[:, D + LANE * c:D + LANE * c + LANE] = (dcv * v).astype(BF)
            da_ref[:, 2 * D + LANE * c:2 * D + LANE * c + LANE] = (dcv * gc).astype(BF)
            _tap_grads(dw_ref, dcab[0:tr, sl], cvb, sl, K_A, h - (K_A - 1), tr)

    return _pcall(
        body, grid=(n,),
        in_specs=[pl.BlockSpec((h, 3 * D), lambda i: (_halo_before(i, tr, h), 0)),
                  pl.BlockSpec((tr, 3 * D), lambda i: (i, 0)),
                  pl.BlockSpec((h, 3 * D), lambda i: (_halo_after(i, tr, h, s), 0)),
                  pl.BlockSpec((tr, D), lambda i: (i, 0)),
                  pl.BlockSpec((h, D), lambda i: (_halo_after(i, tr, h, s), 0)),
                  pl.BlockSpec((8, D), lambda i: (0, 0)), ANY],
        out_specs=[pl.BlockSpec((tr, 3 * D), lambda i: (i, 0)), pl.BlockSpec((8, D), lambda i: (0, 0))],
        out_shape=[SDS(dproj.shape, BF), SDS((8, D), F32)], input_output_aliases={6: 0},
        scratch_shapes=[pltpu.VMEM((h + tr, D), F32), pltpu.VMEM((tr + h, D), F32)],
        compiler_params=_cp(("arbitrary",)), name=name)(proj, proj, proj, dza, dza, cw, dproj)


_U_COL, _UG_COL = 3, 4


def _brb_conv_fwd(proj, cw, bias, *, name):
    s = proj.shape[0]
    tr, h = min(TR, s), H_L

    def body(u_cur, ug_cur, u_halo, ug_halo, w_ref, b_ref, cb_ref, glb, shifted):
        i = pl.program_id(0)
        glb[0:h, :] = jnp.where(i == 0, 0.0, u_halo[...].astype(F32) * _sigmoid(ug_halo[...].astype(F32)))
        glb[h:h + tr, :] = u_cur[...].astype(F32) * _sigmoid(ug_cur[...].astype(F32))
        for c in range(D // LANE):
            sl = slice(LANE * c, LANE * c + LANE)
            for r in range(1, 8):
                shifted[r] = glb[8 - r:8 - r + tr + 24, sl]
            acc = None
            for k in range(K_B):
                q, r = divmod(K_B - 1 - k, 8)
                win = shifted[r, 24 - 8 * q:24 - 8 * q + tr, :] if r else glb[h - 8 * q:h - 8 * q + tr, sl]
                term = w_ref[k:k + 1, sl] * win
                acc = term if acc is None else acc + term
            cb_ref[:, sl] = (acc + b_ref[:, sl]).astype(BF)

    return _pcall(
        body, grid=(s // tr,),
        in_specs=[pl.BlockSpec((tr, D), lambda i: (i, _U_COL)), pl.BlockSpec((tr, D), lambda i: (i, _UG_COL)),
                  pl.BlockSpec((h, D), lambda i: (_halo_before(i, tr, h), _U_COL)),
                  pl.BlockSpec((h, D), lambda i: (_halo_before(i, tr, h), _UG_COL)),
                  pl.BlockSpec((32, D), lambda i: (0, 0)), pl.BlockSpec((1, D), lambda i: (0, 0))],
        out_specs=pl.BlockSpec((tr, D), lambda i: (i, 0)), out_shape=SDS((s, D), BF),
        scratch_shapes=[pltpu.VMEM((h + tr, D), F32), pltpu.VMEM((8, tr + 24, LANE), F32)],
        compiler_params=_cp(("parallel",)), name=name)(proj, proj, proj, proj, cw, bias)


def _brb_conv_bwd(proj, dcb, dq, cw, dproj, *, name):
    s = proj.shape[0]
    tr, h = min(TR, s), H_L
    n = s // tr

    def body(u_cur, ug_cur, d_cur, d_after, dq_ref, w_ref, dproj_in, db_ref, dw_ref, dcbb, shifted):
        del dproj_in
        i = pl.program_id(0)

        @pl.when(i == 0)
        def _():
            dw_ref[...] = jnp.zeros_like(dw_ref)

        db_ref[:, 2 * D:3 * D] = dq_ref[...]
        dcbb[0:tr, :] = d_cur[...].astype(F32)
        dcbb[tr:tr + h, :] = jnp.where(i == n - 1, 0.0, d_after[...].astype(F32))
        for c in range(D // LANE):
            sl = slice(LANE * c, LANE * c + LANE)
            u = u_cur[:, sl].astype(F32)
            sg = _sigmoid(ug_cur[:, sl].astype(F32))
            glu = u * sg
            for r in range(1, 8):
                shifted[r] = dcbb[r:r + tr + 24, sl]
            dglu = None
            for k in range(K_B):
                q, r = divmod(K_B - 1 - k, 8)
                win = shifted[r, 8 * q:8 * q + tr, :] if r else dcbb[8 * q:8 * q + tr, sl]
                term = w_ref[k:k + 1, sl] * win
                dglu = term if dglu is None else dglu + term
                dw_ref[k:k + 1, sl] += jnp.sum(win * glu, axis=0, keepdims=True)
            db_ref[:, sl] = (dglu * sg).astype(BF)
            db_ref[:, D + LANE * c:D + LANE * c + LANE] = (dglu * u * sg * (1.0 - sg)).astype(BF)

    return _pcall(
        body, grid=(n,),
        in_specs=[pl.BlockSpec((tr, D), lambda i: (i, _U_COL)), pl.BlockSpec((tr, D), lambda i: (i, _UG_COL)),
                  pl.BlockSpec((tr, D), lambda i: (i, 0)),
                  pl.BlockSpec((h, D), lambda i: (_halo_after(i, tr, h, s), 0)),
                  pl.BlockSpec((tr, D), lambda i: (i, 0)),
                  pl.BlockSpec((32, D), lambda i: (0, 0)), ANY],
        out_specs=[pl.BlockSpec((tr, 3 * D), lambda i: (i, 1)), pl.BlockSpec((32, D), lambda i: (0, 0))],
        out_shape=[SDS(dproj.shape, BF), SDS((32, D), F32)], input_output_aliases={6: 0},
        scratch_shapes=[pltpu.VMEM((tr + h, D), F32), pltpu.VMEM((8, tr + 24, LANE), F32)],
        compiler_params=_cp(("arbitrary",)), name=name)(proj, proj, dcb, dcb, dq, cw, dproj)


def _ln_silu_fwd(cb, g, b, *, name):
    s = cb.shape[0]
    tm = min(TM, s)

    def body(cb_ref, g_ref, b_ref, sb_ref):
        z = cb_ref[...].astype(F32)
        zc = z - jnp.mean(z, axis=-1, keepdims=True)
        ln = (zc * lax.rsqrt(jnp.mean(zc * zc, axis=-1, keepdims=True) + NORM_EPS)) * g_ref[...] + b_ref[...]
        sb_ref[...] = (ln * _sigmoid(ln)).astype(BF)

    row = lambda i: (i, 0)
    vec = pl.BlockSpec((1, D), lambda i: (0, 0))
    return _pcall(
        body, grid=(s // tm,), in_specs=[pl.BlockSpec((tm, D), row), vec, vec],
        out_specs=pl.BlockSpec((tm, D), row), out_shape=SDS((s, D), BF),
        compiler_params=_cp(("parallel",)), name=name)(cb, g, b)


def _ln_silu_bwd(cb, dsb, g, b, *, name):
    s = cb.shape[0]
    tm = min(TM, s)

    def body(cb_ref, dsb_ref, g_ref, b_ref, dcb_ref, sums_ref):
        @pl.when(pl.program_id(0) == 0)
        def _():
            sums_ref[...] = jnp.zeros_like(sums_ref)

        z = cb_ref[...].astype(F32)
        zc = z - jnp.mean(z, axis=-1, keepdims=True)
        rstd = lax.rsqrt(jnp.mean(zc * zc, axis=-1, keepdims=True) + NORM_EPS)
        lnh = zc * rstd
        ln = lnh * g_ref[...] + b_ref[...]
        sg = _sigmoid(ln)
        dln = dsb_ref[...].astype(F32) * (sg * (1.0 + ln * (1.0 - sg)))
        sums_ref[0:1, :] += jnp.sum(dln * lnh, axis=0, keepdims=True)
        sums_ref[1:2, :] += jnp.sum(dln, axis=0, keepdims=True)
        dlnh = dln * g_ref[...]
        dz = rstd * (dlnh - jnp.mean(dlnh, axis=-1, keepdims=True)
                     - lnh * jnp.mean(dlnh * lnh, axis=-1, keepdims=True))
        sums_ref[2:3, :] += jnp.sum(dz, axis=0, keepdims=True)
        dcb_ref[...] = dz.astype(BF)

    row = lambda i: (i, 0)
    vec = pl.BlockSpec((1, D), lambda i: (0, 0))
    return _pcall(
        body, grid=(s // tm,), in_specs=[pl.BlockSpec((tm, D), row), pl.BlockSpec((tm, D), row), vec, vec],
        out_specs=[pl.BlockSpec((tm, D), row), pl.BlockSpec((8, D), lambda i: (0, 0))],
        out_shape=[SDS((s, D), BF), SDS((8, D), F32)],
        compiler_params=_cp(("arbitrary",)), name=name)(cb, dsb, g, b)


_Q_COL = 5 * D // HEAD


def _kv_prep(mem, g, wkv, *, name):
    m = mem.shape[0]

    def body(mem_ref, g_ref, w_ref, memn_ref, kv_ref):
        mv = mem_ref[...]
        memn = ((mv * _rms(mv)) * g_ref[...]).astype(BF)
        memn_ref[...] = memn
        for dev in range(N_DEV):
            kv_ref[:, dev * C_KV:(dev + 1) * C_KV] = _dot(memn, w_ref[dev]).astype(BF)

    return _pcall(body, out_shape=[SDS((m, D), BF), SDS((m, 2 * D), BF)],
                  compiler_params=_cp(), name=name)(mem, g, wkv)


def _softmax_rows(q, k):
    sc = _dot_nt(q, k) * (1.0 / (HEAD ** 0.5))
    e = jnp.exp(sc - jnp.max(sc, axis=-1, keepdims=True))
    return e / jnp.sum(e, axis=-1, keepdims=True)


def _attn_fwd(proj, kv, *, name):
    s, m = proj.shape[0], kv.shape[0]
    tm = min(TM, s)

    def body(q_ref, k_ref, v_ref, o_ref):
        p = _softmax_rows(q_ref[...], k_ref[...])
        o_ref[...] = _dot(p.astype(BF), v_ref[...]).astype(BF)

    return _pcall(
        body, grid=(s // tm, N_HEADS),
        in_specs=[pl.BlockSpec((tm, HEAD), lambda i, hd: (i, _Q_COL + hd)),
                  pl.BlockSpec((m, HEAD), lambda i, hd: (0, hd)),
                  pl.BlockSpec((m, HEAD), lambda i, hd: (0, N_HEADS + hd))],
        out_specs=pl.BlockSpec((tm, HEAD), lambda i, hd: (i, hd)), out_shape=SDS((s, D), BF),
        compiler_params=_cp(("parallel", "parallel")), name=name)(proj, kv, kv)


def _attn_bwd(proj, kv, do, *, name):
    s, m = proj.shape[0], kv.shape[0]
    tm = min(TM, s)

    def body(q_ref, k_ref, v_ref, do_ref, dq_ref, dk_ref, dv_ref):
        @pl.when(pl.program_id(1) == 0)
        def _():
            dk_ref[...] = jnp.zeros_like(dk_ref)
            dv_ref[...] = jnp.zeros_like(dv_ref)

        q, k, dov = q_ref[...], k_ref[...], do_ref[...]
        p = _softmax_rows(q, k)
        dp = _dot_nt(dov, v_ref[...])
        dv_ref[...] += _dot_tn(p.astype(BF), dov)
        ds = (p * (dp - jnp.sum(dp * p, axis=-1, keepdims=True)) * (1.0 / (HEAD ** 0.5))).astype(BF)
        dq_ref[...] = _dot(ds, k).astype(BF)
        dk_ref[...] += _dot_tn(ds, q)

    return _pcall(
        body, grid=(N_HEADS, s // tm),
        in_specs=[pl.BlockSpec((tm, HEAD), lambda hd, i: (i, _Q_COL + hd)),
                  pl.BlockSpec((m, HEAD), lambda hd, i: (0, hd)),
                  pl.BlockSpec((m, HEAD), lambda hd, i: (0, N_HEADS + hd)),
                  pl.BlockSpec((tm, HEAD), lambda hd, i: (i, hd))],
        out_specs=[pl.BlockSpec((tm, HEAD), lambda hd, i: (i, hd)),
                   pl.BlockSpec((m, HEAD), lambda hd, i: (0, hd)),
                   pl.BlockSpec((m, HEAD), lambda hd, i: (0, hd))],
        out_shape=[SDS((s, D), BF), SDS((m, D), F32), SDS((m, D), F32)],
        compiler_params=_cp(("parallel", "arbitrary")), name=name)(proj, kv, kv, do)


def _kv_bwd(mem, g, memn, dk, dv, wkv, *, name):
    def body(mem_ref, g_ref, memn_ref, dk_ref, dv_ref, w_ref, dw_ref, dg_ref):
        memn = memn_ref[...]
        dmemn = None
        for dev in range(N_DEV):
            d_ref, col = (dk_ref, dev) if dev < N_HEADS else (dv_ref, dev - N_HEADS)
            dslab = d_ref[:, col * C_KV:(col + 1) * C_KV].astype(BF)
            dw_ref[dev] = _dot_tn(memn, dslab).astype(BF)
            part = _dot_nt(dslab, w_ref[dev])
            dmemn = part if dmemn is None else dmemn + part
        mv = mem_ref[...]
        dg_ref[...] = jnp.zeros_like(dg_ref)
        dg_ref[0:1, :] = jnp.sum(dmemn * (mv * _rms(mv)), axis=0, keepdims=True)

    assert C_KV == HEAD
    return _pcall(body, out_shape=[SDS((N_DEV, D, C_KV), BF), SDS((8, D), F32)],
                  compiler_params=_cp(), name=name)(mem, g, memn, dk, dv, wkv)


_TM_MIX = 256


def _mix_out(x, za, sb, o, proj, w4, bg, g_next, *, name):
    s = x.shape[0]
    tm = min(_TM_MIX, s)

    def body(x_ref, za_ref, sb_ref, o_ref, pg_ref, w4_ref, bg_ref, gn_ref,
             ya_ref, yb_ref, yc_ref, mg_ref, x1_ref, h_ref):
        ys = (_dot(za_ref[...], w4_ref[0]), _dot(sb_ref[...], w4_ref[1]), _dot(o_ref[...], w4_ref[2]))
        merged = None
        for j, (y, y_ref) in enumerate(zip(ys, (ya_ref, yb_ref, yc_ref))):
            y_ref[...] = y.astype(BF)
            gate = _sigmoid(pg_ref[:, j * D:(j + 1) * D].astype(F32) + bg_ref[:, j * D:(j + 1) * D])
            merged = gate * y if merged is None else merged + gate * y
        mg = merged.astype(BF)
        mg_ref[...] = mg
        x1 = x_ref[...] + _dot(mg, w4_ref[3])
        x1_ref[...] = x1
        h_ref[...] = ((x1 * _rms(x1)) * gn_ref[...]).astype(BF)

    row = lambda i: (i, 0)
    act = pl.BlockSpec((tm, D), row)
    return _pcall(
        body, grid=(s // tm,),
        in_specs=[act, act, act, act, pl.BlockSpec((tm, 3 * D), lambda i: (i, 2)),
                  pl.BlockSpec((4, D, D), lambda i: (0, 0, 0)), pl.BlockSpec((1, 3 * D), lambda i: (0, 0)),
                  pl.BlockSpec((1, D), lambda i: (0, 0))],
        out_specs=[act] * 6,
        out_shape=[SDS((s, D), BF)] * 4 + [SDS((s, D), F32), SDS((s, D), BF)],
        compiler_params=_cp(("parallel",)), name=name)(x, za, sb, o, proj, w4, bg, g_next)


def _mix_bwd(dxb, ya, yb, yc, proj, w4, bg, *, name):
    s = dxb.shape[0]
    tm = min(_TM_MIX, s)

    def body(dx_ref, ya_ref, yb_ref, yc_ref, pg_ref, w4_ref, bg_ref,
             dya_ref, dyb_ref, dyc_ref, dza_ref, dsb_ref, do_ref, dgt_ref, dbg_ref):
        @pl.when(pl.program_id(0) == 0)
        def _():
            dbg_ref[...] = jnp.zeros_like(dbg_ref)

        dm = _dot_nt(dx_ref[...], w4_ref[3])
        for j, (y_ref, dy_ref, din_ref) in enumerate(zip((ya_ref, yb_ref, yc_ref), (dya_ref, dyb_ref, dyc_ref),
                                                         (dza_ref, dsb_ref, do_ref))):
            cols = slice(j * D, (j + 1) * D)
            gate = _sigmoid(pg_ref[:, cols].astype(F32) + bg_ref[:, cols])
            dy = (dm * gate).astype(BF)
            dy_ref[...] = dy
            din_ref[...] = _dot_nt(dy, w4_ref[j]).astype(BF)
            dpre = dm * y_ref[...].astype(F32) * gate * (1.0 - gate)
            dgt_ref[:, cols] = dpre.astype(BF)
            dbg_ref[0:1, cols] += jnp.sum(dpre, axis=0, keepdims=True)

    row = lambda i: (i, 0)
    act = pl.BlockSpec((tm, D), row)
    return _pcall(
        body, grid=(s // tm,),
        in_specs=[act, act, act, act, pl.BlockSpec((tm, 3 * D), lambda i: (i, 2)),
                  pl.BlockSpec((4, D, D), lambda i: (0, 0, 0)), pl.BlockSpec((1, 3 * D), lambda i: (0, 0))],
        out_specs=[act] * 6 + [pl.BlockSpec((tm, 3 * D), lambda i: (i, 2)),
                               pl.BlockSpec((8, 3 * D), lambda i: (0, 0))],
        out_shape=[SDS((s, D), BF)] * 6 + [SDS((s, 9 * D), BF), SDS((8, 3 * D), F32)],
        compiler_params=_cp(("arbitrary",)), name=name)(dxb, ya, yb, yc, proj, w4, bg)


_PAIR = 2 * C_UP_P


def _ffn_act(u2, cw, *, name):
    s = u2.shape[0]
    tr, h = min(TR, s), H_S

    def body(cur, halo, w_ref, act_ref, c2_ref, ub):
        i = pl.program_id(1)
        ub[0:h, :] = jnp.where(i == 0, 0.0, halo[...].astype(F32))
        ub[h:h + tr, :] = cur[...].astype(F32)
        for c in range(C_UP_P // LANE):
            gl = slice(LANE * c, LANE * c + LANE)
            ul = slice(C_UP_P + LANE * c, C_UP_P + LANE * c + LANE)
            gt = _taps(ub, w_ref, gl, K_F, h - (K_F - 1), tr)
            up = _taps(ub, w_ref, ul, K_F, h - (K_F - 1), tr)
            c2_ref[:, gl] = gt.astype(BF)
            c2_ref[:, ul] = up.astype(BF)
            act_ref[:, gl] = (gt * _sigmoid(gt) * up).astype(BF)

    return _pcall(
        body, grid=(4, s // tr),
        in_specs=[pl.BlockSpec((tr, _PAIR), lambda p, i: (i, p)),
                  pl.BlockSpec((h, _PAIR), lambda p, i: (_halo_before(i, tr, h), p)),
                  pl.BlockSpec((8, _PAIR), lambda p, i: (0, p))],
        out_specs=[pl.BlockSpec((tr, C_UP_P), lambda p, i: (i, p)), pl.BlockSpec((tr, _PAIR), lambda p, i: (i, p))],
        out_shape=[SDS((s, FF_P), BF), SDS((s, 2 * FF_P), BF)],
        scratch_shapes=[pltpu.VMEM((h + tr, _PAIR), F32)],
        compiler_params=_cp(("parallel", "parallel")), name=name)(u2, u2, cw)


def _ffn_bwd(u2, c2, dact, cw, *, name):
    s = u2.shape[0]
    tr, h = min(TR, s), H_S
    n = s // tr
    ext = tr + h

    def body(u_cur, c_cur, c_after, da_cur, da_after, w_ref, du_ref, dw_ref, dcb):
        i = pl.program_id(1)
        last = i == n - 1

        @pl.when(i == 0)
        def _():
            dw_ref[...] = jnp.zeros_like(dw_ref)

        def rows(cur_ref, after_ref, sl):
            return jnp.concatenate([cur_ref[:, sl].astype(F32),
                                    jnp.where(last, 0.0, after_ref[:, sl].astype(F32))], axis=0)

        for c in range(C_UP_P // LANE):
            gl = slice(LANE * c, LANE * c + LANE)
            ul = slice(C_UP_P + LANE * c, C_UP_P + LANE * c + LANE)
            gt, up, da = rows(c_cur, c_after, gl), rows(c_cur, c_after, ul), rows(da_cur, da_after, gl)
            sg = _sigmoid(gt)
            dcb[:, gl] = da * up * (sg * (1.0 + gt * (1.0 - sg)))
            dcb[:, ul] = da * (gt * sg)
        for c in range(_PAIR // LANE):
            sl = slice(LANE * c, LANE * c + LANE)
            u = u_cur[:, sl].astype(F32)
            du = None
            for k in range(K_F):
                win = dcb[K_F - 1 - k:K_F - 1 - k + tr, sl]
                term = w_ref[k:k + 1, sl] * win
                du = term if du is None else du + term
                dw_ref[k:k + 1, sl] += jnp.sum(win * u, axis=0, keepdims=True)
            du_ref[:, sl] = du.astype(BF)

    return _pcall(
        body, grid=(4, n),
        in_specs=[pl.BlockSpec((tr, _PAIR), lambda p, i: (i, p)),
                  pl.BlockSpec((tr, _PAIR), lambda p, i: (i, p)),
                  pl.BlockSpec((h, _PAIR), lambda p, i: (_halo_after(i, tr, h, s), p)),
                  pl.BlockSpec((tr, C_UP_P), lambda p, i: (i, p)),
                  pl.BlockSpec((h, C_UP_P), lambda p, i: (_halo_after(i, tr, h, s), p)),
                  pl.BlockSpec((8, _PAIR), lambda p, i: (0, p))],
        out_specs=[pl.BlockSpec((tr, _PAIR), lambda p, i: (i, p)), pl.BlockSpec((8, _PAIR), lambda p, i: (0, p))],
        out_shape=[SDS((s, 2 * FF_P), BF), SDS((8, 2 * FF_P), F32)],
        scratch_shapes=[pltpu.VMEM((ext, _PAIR), F32)],
        compiler_params=_cp(("parallel", "arbitrary")), name=name)(u2, c2, c2, dact, dact, cw)


def _relations():
    x, y, c = lax.axis_index("x"), lax.axis_index("y"), lax.axis_index("c")
    out = []
    for r in range(1, N_DEV):
        rx, ry, rc = (r >> 2) & 1, (r >> 1) & 1, r & 1
        out.append((r, (x ^ rx, y ^ ry, c ^ rc)))
    return out


def _my_index():
    return 4 * lax.axis_index("x") + 2 * lax.axis_index("y") + lax.axis_index("c")


def _exchange(n_arrays, src_of, dst_of, refs):
    ssem, rsem, lsem = refs
    me = _my_index()
    local = []
    for a in range(n_arrays):
        loc = pltpu.make_async_copy(src_of(a, me), dst_of(a, me), lsem.at[a])
        loc.start()
        local.append(loc)

    def copy(a, r, peer, src_idx, dst_idx):
        return pltpu.make_async_remote_copy(
            src_ref=src_of(a, src_idx), dst_ref=dst_of(a, dst_idx), send_sem=ssem.at[a, r - 1],
            recv_sem=rsem.at[a, r - 1], device_id=peer, device_id_type=MESH)

    peers = [(r, peer, 4 * peer[0] + 2 * peer[1] + peer[2]) for r, peer in _relations()]
    for r, peer, p_idx in peers:
        for a in range(n_arrays):
            copy(a, r, peer, p_idx, me).start()
    for r, peer, p_idx in peers:
        for a in range(n_arrays):
            copy(a, r, peer, p_idx, me).wait_send()
            copy(a, r, peer, me, p_idx).wait_recv()
    for loc in local:
        loc.wait()


def _sem_scratch(n_arrays):
    return [pltpu.SemaphoreType.DMA((n_arrays, N_DEV - 1)), pltpu.SemaphoreType.DMA((n_arrays, N_DEV - 1)),
            pltpu.SemaphoreType.DMA((n_arrays,))]


def _slab(kind, ref, idx):
    if kind == "win":
        return ref.at[:, pl.ds(pl.multiple_of(idx * C_IN, LANE), C_IN)]
    if kind == "wup":
        return ref.at[:, pl.ds(pl.multiple_of(_up_slot(idx) * C_UP_P, LANE), C_UP_P)]
    if kind == "wkv":
        return ref.at[idx]
    if kind == "w4":
        return ref.at[:, pl.ds(pl.multiple_of(idx * R_O, 16), R_O), :]
    if kind == "wdn":
        return ref.at[pl.ds(pl.multiple_of(_dn_row(idx), 16), R_DN), :]
    assert kind == "cv"
    return ref.at[idx]


_WHOLE = {"win": ((D, 9 * D), BF), "wup": ((D, 2 * FF_P), BF), "wkv": ((N_DEV, D, C_KV), BF),
          "w4": ((4, D, D), BF), "wdn": ((FF_P, D), BF)}
_SHARD = {"win": (D, C_IN), "wup": (D, C_UP_P), "wkv": (D, C_KV), "w4": (4, R_O, D), "wdn": (R_DN, D)}
HBM_SPEC = pl.BlockSpec(memory_space=pltpu.HBM)
SEM_SPEC = pl.BlockSpec(memory_space=pltpu.SEMAPHORE)
_DATAFLOW = pltpu.SideEffectType.DATAFLOW_SIDE_EFFECTING


def _gather_maps(kinds):
    return ((lambda srcs, lands, a, idx: _slab(kinds[a], lands[a], _my_index())),
            (lambda lands, a, idx: _slab(kinds[a], lands[a], idx)))


def _scatter_maps(kinds):
    return ((lambda srcs, lands, a, idx: _slab(kinds[a], srcs[a], idx)),
            (lambda lands, a, idx: lands[a].at[idx]))


_SLOTTED = ("wkv", "cv")


def _own_slab_blocks(kind, shard_shape):
    if kind in ("win", "wup"):
        rows, slot = 256, (_up_slot if kind == "wup" else (lambda m: m))
        return (shard_shape[0] // rows, (rows, shard_shape[1]), (lambda i, me: (i, slot(me[0]))),
                (lambda i, me: (i, 0)), (lambda i, me: (me[0], i, 0)))
    if kind == "w4":
        return (1, shard_shape, (lambda i, me: (0, me[0], 0)), (lambda i, me: (0, 0, 0)),
                (lambda i, me: (me[0], 0, 0, 0)))
    if kind == "wdn":
        rows = 32
        return (R_DN // rows, (rows, D), (lambda i, me: (_dn_row(me[0]) // rows + i, 0)), (lambda i, me: (i, 0)),
                (lambda i, me: (me[0], i, 0)))
    assert kind in _SLOTTED
    rows = min(256, shard_shape[0])
    return (shard_shape[0] // rows, (rows, shard_shape[1]), (lambda i, me: (me[0], i, 0)),
            (lambda i, me: (i, 0)), (lambda i, me: (me[0], i, 0)))


def _place_own(kind, src, out_sds, gather, me_arr, *, name):
    shard_shape = src.shape if gather else out_sds.shape[1:]
    steps, blk, whole_idx, shard_idx, staging_idx = _own_slab_blocks(kind, shard_shape)
    slotted = kind in _SLOTTED
    whole_spec = pl.BlockSpec(((None,) if slotted else ()) + tuple(blk), whole_idx)
    if gather:
        in_spec, out_spec = pl.BlockSpec(tuple(blk), shard_idx), whole_spec
    else:
        in_spec, out_spec = whole_spec, pl.BlockSpec((None,) + tuple(blk), staging_idx)
    zero_init = gather and kind == "wdn"

    def body(me_ref, src_ref, *rest):
        rest[-1][...] = src_ref[...].astype(rest[-1].dtype)

    operands = (me_arr, src) + ((jnp.zeros(out_sds.shape, out_sds.dtype),) if zero_init else ())
    return _pcall(
        body,
        grid_spec=pltpu.PrefetchScalarGridSpec(
            num_scalar_prefetch=1, grid=(steps,), in_specs=[in_spec] + ([ANY] if zero_init else []),
            out_specs=out_spec),
        out_shape=out_sds, input_output_aliases={2: 0} if zero_init else {},
        compiler_params=_cp(("arbitrary",)), name=name)(*operands)


def _peer_copies(n, src_of, dst_of, src_r, land_r, ssem, rsem):
    me = _my_index()
    out = []
    for r, peer in _relations():
        p_idx = 4 * peer[0] + 2 * peer[1] + peer[2]
        for a in range(n):
            def copy(src_idx, dst_idx, a=a, r=r, peer=peer):
                sem = a * (N_DEV - 1) + r - 1
                return pltpu.make_async_remote_copy(
                    src_ref=src_of(src_r, land_r, a, src_idx), dst_ref=dst_of(land_r, a, dst_idx),
                    send_sem=ssem.at[sem], recv_sem=rsem.at[sem], device_id=peer, device_id_type=MESH)
            out.append((functools.partial(copy, p_idx, me), functools.partial(copy, me, p_idx)))
    return out


def _exchange_start(srcs, lands, maps, after, *, name):
    n, ns = len(lands), len(srcs)
    src_of, dst_of = maps

    def body(*refs):
        src_r, land_r = refs[:ns], refs[ns:ns + n]
        ssem, rsem, token = refs[ns + n + 1], refs[ns + n + 2], refs[-1]
        for send, _ in _peer_copies(n, src_of, dst_of, src_r, land_r, ssem, rsem):
            send().start()
        token[...] = jnp.zeros_like(token)

    flight = list(srcs) + list(lands)
    outs = pl.pallas_call(
        body, name=name,
        out_shape=(pltpu.SemaphoreType.DMA((n * (N_DEV - 1),)), pltpu.SemaphoreType.DMA((n * (N_DEV - 1),)),
                   *[pltpu.HBM(a.shape, a.dtype) for a in flight], SDS((8, LANE), F32)),
        in_specs=[HBM_SPEC] * (ns + n) + [ANY],
        out_specs=(SEM_SPEC, SEM_SPEC, *[HBM_SPEC] * (ns + n), pl.BlockSpec(memory_space=pltpu.VMEM)),
        input_output_aliases={i: 2 + i for i in range(ns + n)},
        compiler_params=pltpu.CompilerParams(has_side_effects=_DATAFLOW),
    )(*[pltpu.with_memory_space_constraint(a, pltpu.HBM) for a in flight], after)
    return (outs[0], outs[1], list(outs[2:2 + ns + n]), ns), outs[-1]


def _exchange_wait(handle, maps, after, *, name):
    ssem, rsem, flight, ns = handle
    n = len(flight) - ns
    src_of, dst_of = maps

    def body(*refs):
        src_r, land_r, ssem_r, rsem_r = refs[:ns], refs[ns:ns + n], refs[ns + n], refs[ns + n + 1]
        for send, arrival in _peer_copies(n, src_of, dst_of, src_r, land_r, ssem_r, rsem_r):
            send().wait_send()
            arrival().wait_recv()

    outs = pl.pallas_call(
        body, name=name, out_shape=[pltpu.HBM(a.shape, a.dtype) for a in flight],
        in_specs=[HBM_SPEC] * (ns + n) + [SEM_SPEC, SEM_SPEC, ANY], out_specs=[HBM_SPEC] * (ns + n),
        input_output_aliases={i: i for i in range(ns + n)},
        compiler_params=pltpu.CompilerParams(has_side_effects=_DATAFLOW),
    )(*flight, ssem, rsem, after)
    return list(outs[ns:])


_SIBLING = 1
_ICI = (2, 4, 6)


def _rel_peer(r):
    x, y, c = lax.axis_index("x"), lax.axis_index("y"), lax.axis_index("c")
    peer = (x ^ ((r >> 2) & 1), y ^ ((r >> 1) & 1), c ^ (r & 1))
    return peer, 4 * peer[0] + 2 * peer[1] + peer[2]


def _rcopy(ref, ssem, rsem, peer):
    return pltpu.make_async_remote_copy(src_ref=ref, dst_ref=ref, send_sem=ssem, recv_sem=rsem, device_id=peer,
                                        device_id_type=MESH)


def _gather2_start(lands, kinds, after, *, name):
    n = len(lands)

    def body(*refs):
        land_r, (send1, recv_sib, recv_ici), token = refs[:n], refs[n + 1:n + 4], refs[-1]
        me = _my_index()
        for a in range(n):
            own = _slab(kinds[a], land_r[a], me)
            for j, r in enumerate((_SIBLING,) + _ICI):
                rsem = recv_sib.at[a] if r == _SIBLING else recv_ici.at[3 * a + j - 1]
                _rcopy(own, send1.at[4 * a + j], rsem, _rel_peer(r)[0]).start()
        token[...] = jnp.zeros_like(token)

    sems = [pltpu.SemaphoreType.DMA((4 * n,)), pltpu.SemaphoreType.DMA((n,)), pltpu.SemaphoreType.DMA((3 * n,))]
    outs = pl.pallas_call(
        body, name=name, out_shape=(*sems, *[pltpu.HBM(a.shape, a.dtype) for a in lands], SDS((8, LANE), F32)),
        in_specs=[HBM_SPEC] * n + [ANY],
        out_specs=(SEM_SPEC,) * 3 + (HBM_SPEC,) * n + (pl.BlockSpec(memory_space=pltpu.VMEM),),
        input_output_aliases={i: 3 + i for i in range(n)},
        compiler_params=pltpu.CompilerParams(has_side_effects=_DATAFLOW),
    )(*[pltpu.with_memory_space_constraint(a, pltpu.HBM) for a in lands], after)
    return dict(send1=outs[0], recv_sib=outs[1], recv_ici=outs[2], lands=list(outs[3:3 + n])), outs[-1]


def _gather2_forward(handle, kinds, after, *, name):
    lands = handle["lands"]
    n = len(lands)

    def body(*refs):
        land_r, recv_ici, (fwd_send, fwd_recv), token = refs[:n], refs[n], refs[n + 2:n + 4], refs[-1]
        sibling = _rel_peer(_SIBLING)[0]
        for a in range(n):
            for j, r in enumerate(_ICI):
                got = _slab(kinds[a], land_r[a], _rel_peer(r)[1])
                _rcopy(got, fwd_send.at[3 * a + j], recv_ici.at[3 * a + j], sibling).wait_recv()
                _rcopy(got, fwd_send.at[3 * a + j], fwd_recv.at[3 * a + j], sibling).start()
        token[...] = jnp.zeros_like(token)

    sems = [pltpu.SemaphoreType.DMA((3 * n,)), pltpu.SemaphoreType.DMA((3 * n,))]
    outs = pl.pallas_call(
        body, name=name, out_shape=(*sems, *[pltpu.HBM(a.shape, a.dtype) for a in lands], SDS((8, LANE), F32)),
        in_specs=[HBM_SPEC] * n + [SEM_SPEC, ANY],
        out_specs=(SEM_SPEC,) * 2 + (HBM_SPEC,) * n + (pl.BlockSpec(memory_space=pltpu.VMEM),),
        input_output_aliases={i: 2 + i for i in range(n)},
        compiler_params=pltpu.CompilerParams(has_side_effects=_DATAFLOW),
    )(*lands, handle["recv_ici"], after)
    return dict(handle, fwd_send=outs[0], fwd_recv=outs[1], lands=list(outs[2:2 + n])), outs[-1]


def _gather2_wait(handle, kinds, after, *, name):
    lands = handle["lands"]
    n = len(lands)

    def body(*refs):
        land_r, (send1, recv_sib, fwd_send, fwd_recv) = refs[:n], refs[n:n + 4]
        me = _my_index()
        sibling, sib_idx = _rel_peer(_SIBLING)
        for a in range(n):
            own = _slab(kinds[a], land_r[a], me)
            for j, r in enumerate((_SIBLING,) + _ICI):
                _rcopy(own, send1.at[4 * a + j], recv_sib.at[a], _rel_peer(r)[0]).wait_send()
            theirs = _slab(kinds[a], land_r[a], sib_idx)
            _rcopy(theirs, send1.at[4 * a], recv_sib.at[a], sibling).wait_recv()
            for j, r in enumerate(_ICI):
                passed_on = _slab(kinds[a], land_r[a], _rel_peer(r)[1])
                _rcopy(passed_on, fwd_send.at[3 * a + j], fwd_recv.at[3 * a + j], sibling).wait_send()
                arrived = _slab(kinds[a], land_r[a], _rel_peer(r ^ _SIBLING)[1])
                _rcopy(arrived, fwd_send.at[3 * a + j], fwd_recv.at[3 * a + j], sibling).wait_recv()

    outs = pl.pallas_call(
        body, name=name, out_shape=[pltpu.HBM(a.shape, a.dtype) for a in lands],
        in_specs=[HBM_SPEC] * n + [SEM_SPEC] * 4 + [ANY], out_specs=[HBM_SPEC] * n,
        input_output_aliases={i: i for i in range(n)},
        compiler_params=pltpu.CompilerParams(has_side_effects=_DATAFLOW),
    )(*lands, handle["send1"], handle["recv_sib"], handle["fwd_send"], handle["fwd_recv"], after)
    return list(outs)


def _allreduce_small(pack, *, name):
    rows = pack.shape[0]

    def body(p_ref, out_ref, gath, ssem, rsem, lsem):
        _exchange(1, lambda a, idx: p_ref, lambda a, idx: gath.at[idx], (ssem, rsem, lsem))
        total = gath[0]
        for d in range(1, N_DEV):
            total = total + gath[d]
        out_ref[...] = total

    vm = pl.BlockSpec(memory_space=pltpu.VMEM)
    return _pcall(
        body, in_specs=[vm], out_specs=vm, out_shape=SDS(pack.shape, F32),
        scratch_shapes=[pltpu.VMEM((N_DEV, rows, pack.shape[1]), F32)] + _sem_scratch(1),
        compiler_params=_cp(has_side_effects=True), name=name)(pack)


def _adam(g, w, m, v):
    nm = ADAM_B1 * m + (1.0 - ADAM_B1) * g
    nv = ADAM_B2 * v + (1.0 - ADAM_B2) * (g * g)
    m_hat = nm / (1.0 - ADAM_B1 ** ADAM_STEP)
    v_hat = nv / (1.0 - ADAM_B2 ** ADAM_STEP)
    return -ADAM_LR * (m_hat / (jnp.sqrt(v_hat) + ADAM_EPS) + ADAM_WD * w), nm, nv


def _adamw_staged(st0, st1, w, m, v, *, name):
    _, rows, cols = w.shape
    tr = max(t for t in range(16, 129, 16) if rows % t == 0)
    nr = rows // tr

    def body(s0_ref, s1_ref, w_ref, m_ref, v_ref, g_ref, d_ref, nm_ref, nv_ref):
        for layer, s_ref in enumerate((s0_ref, s1_ref)):
            @pl.when(pl.program_id(0) == layer)
            def _(s_ref=s_ref):
                total = s_ref[0].astype(F32)
                for dev in range(1, N_DEV):
                    total = total + s_ref[dev].astype(F32)
                g_ref[0] = total

        d_ref[0], nm_ref[0], nv_ref[0] = _adam(g_ref[0], w_ref[0], m_ref[0], v_ref[0])

    st_spec = lambda layer: pl.BlockSpec(
        (N_DEV, tr, cols), lambda l, i: (0, jnp.where(l == layer, i, (nr - 1) * (1 - layer)), 0))
    par = pl.BlockSpec((1, tr, cols), lambda l, i: (l, i, 0))
    return _pcall(
        body, grid=(DEPTH, nr), in_specs=[st_spec(0), st_spec(1), par, par, par], out_specs=[par] * 4,
        out_shape=[SDS(w.shape, F32)] * 4,
        compiler_params=_cp(("arbitrary", "arbitrary")), name=name)(st0, st1, w, m, v)


def _adamw_small(g, w, m, v, *, name):
    def body(g_ref, w_ref, m_ref, v_ref, d_ref, nm_ref, nv_ref):
        d_ref[...], nm_ref[...], nv_ref[...] = _adam(g_ref[...], w_ref[...], m_ref[...], v_ref[...])

    return _pcall(body, out_shape=[SDS(g.shape, F32)] * 3, compiler_params=_cp(), name=name)(g, w, m, v)


def _pack_rows(arrays):
    flat = jnp.concatenate([a.reshape(-1).astype(F32) for a in arrays])
    rows = -(-flat.shape[0] // (8 * D)) * 8
    return jnp.pad(flat, (0, rows * D - flat.shape[0])).reshape(rows, D)


def _unpack_rows(pack, like):
    flat = pack.reshape(-1)
    out, at = [], 0
    for a in like:
        out.append(flat[at:at + a.size].reshape(a.shape))
        at += a.size
    return out


def _layer_fwd(x, h, mem, win, rest_of_weights, after_up, small, g_next, tag):
    proj = _mm(h, win, tm=1024, tn=1536, name=f"proj_{tag}")
    wup, wkv, w4, wdn, cw_a, cw_b, cw_f = rest_of_weights(proj)
    za = _bra_fwd(proj, cw_a, name=f"bra_fwd_{tag}")
    cb = _brb_conv_fwd(proj, cw_b, small["conv_b_bias"], name=f"brb_conv_fwd_{tag}")
    sb = _ln_silu_fwd(cb, small["ln_b_g"], small["ln_b_b"], name=f"ln_silu_fwd_{tag}")
    memn, kv = _kv_prep(mem, small["norm_mem_g"], wkv, name=f"kv_prep_{tag}")
    o = _attn_fwd(proj, kv, name=f"attn_fwd_{tag}")
    ya, yb, yc, mg, x1, h2 = _mix_out(x, za, sb, o, proj, w4, small["b_gate"], small["norm_ffn_g"],
                                      name=f"mix_out_{tag}")
    u2 = _mm(h2, wup, tm=1024, tn=1536, name=f"up_{tag}")
    token = after_up(u2)
    act, c2 = _ffn_act(u2, cw_f if token is None else _behind(cw_f, token), name=f"ffn_act_{tag}")
    x2, h_next = _mm_res_norm(act, wdn, x1, g_next, name=f"down_{tag}")
    saved = dict(x=x, h=h, proj=proj, za=za, cb=cb, sb=sb, memn=memn, kv=kv, o=o, ya=ya, yb=yb, yc=yc,
                 mg=mg, x1=x1, h2=h2, u2=u2, c2=c2, act=act)
    return x2, h_next, (win, wup, wkv, w4, wdn, cw_a, cw_b, cw_f), saved


def _behind(operand, token):
    return operand + token[0:1, 0:1]


def _layer_bwd(dx2, dx2b, mem, wts, small, sv, start, tag):
    win, wup, wkv, w4, wdn, cw_a, cw_b, cw_f = wts
    dact = _mm(dx2b, wdn, tb=True, tm=1024, tn=768, name=f"d_act_{tag}")
    dwdn = _mm(sv["act"], dx2b, ta=True, tm=768, tn=1024, name=f"dw_down_{tag}")
    du2, dcw_f = _ffn_bwd(sv["u2"], sv["c2"], dact, cw_f, name=f"ffn_bwd_{tag}")
    dwup = _mm(sv["h2"], du2, ta=True, tm=1024, tn=768, name=f"dw_up_{tag}")
    token = start(("wdn", "wup"), (dwdn, dwup), f"ffn_{tag}")
    dx1, dx1b, dg_ffn = _mm_nt_normbwd(du2, wup, sv["x1"], dx2, _behind(small["norm_ffn_g"], token),
                                       tk=3072, name=f"d_h2_{tag}")

    dya, dyb, dyc, dza, dsb, do, dproj, dbg = _mix_bwd(dx1b, sv["ya"], sv["yb"], sv["yc"], sv["proj"], w4,
                                                      small["b_gate"], name=f"mix_bwd_{tag}")
    dw4 = jnp.stack([
        _mm(a, b, ta=True, tm=1024, tn=512, name=f"dw_{nm}_{tag}")
        for nm, a, b in (("a_out", sv["za"], dya), ("b_out", sv["sb"], dyb), ("att_out", sv["o"], dyc),
                         ("o", sv["mg"], dx1b))])
    dq, dk, dv = _attn_bwd(sv["proj"], sv["kv"], do, name=f"attn_bwd_{tag}")
    dwkv, dg_mem = _kv_bwd(mem, small["norm_mem_g"], sv["memn"], dk, dv, wkv, name=f"kv_bwd_{tag}")
    token = start(("w4", "wkv"), (dw4, dwkv), f"mix_{tag}")
    dproj, dcw_a = _bra_bwd(sv["proj"], dza, _behind(cw_a, token), dproj, name=f"bra_bwd_{tag}")
    dcb, ln_sums = _ln_silu_bwd(sv["cb"], dsb, small["ln_b_g"], small["ln_b_b"], name=f"ln_silu_bwd_{tag}")
    dproj, dcw_b = _brb_conv_bwd(sv["proj"], dcb, dq, cw_b, dproj, name=f"brb_conv_bwd_{tag}")
    dwin = _mm(sv["h"], dproj, ta=True, tm=1024, tn=768, name=f"dw_in_{tag}")
    token = start(("win",), (dwin,), f"in_{tag}")
    dx, dxb, dg_mix = _mm_nt_normbwd(dproj, win, sv["x"], dx1, _behind(small["norm_mix_g"], token),
                                     tk=3072, name=f"d_h_{tag}")

    small_grads = [dg_mix[0:1], dg_mem[0:1], dbg[0:1].reshape(3, D), ln_sums[2:3], ln_sums[0:1], ln_sums[1:2],
                   dg_ffn[0:1], dcw_a[0:K_A], dcw_b[0:K_B], dcw_f[0:K_F].reshape(K_F * 2 * FF_P // D, D)]
    return dx, dxb, small_grads, token


_SMALL_ROWS = (1, 1, 3, 1, 1, 1, 1, K_A, K_B, K_F * 2 * FF_P // D)
_CV_ROWS = 48


def kernel(x, mem, norm_mix_g, norm_mem_g, w_in, b_gate, conv_a_w, w_a_out, conv_b_w, conv_b_bias, ln_b_g, ln_b_b, w_b_out, w_kv, w_att_out, w_o, norm_ffn_g, w_up, conv_ffn_w, w_down, norm_final_g, loss_target, m_norm_mix_g, m_norm_mem_g, m_w_in, m_b_gate, m_conv_a_w, m_w_a_out, m_conv_b_w, m_conv_b_bias, m_ln_b_g, m_ln_b_b, m_w_b_out, m_w_kv, m_w_att_out, m_w_o, m_norm_ffn_g, m_w_up, m_conv_ffn_w, m_w_down, m_norm_final_g, v_norm_mix_g, v_norm_mem_g, v_w_in, v_b_gate, v_conv_a_w, v_w_a_out, v_conv_b_w, v_conv_b_bias, v_ln_b_g, v_ln_b_b, v_w_b_out, v_w_kv, v_w_att_out, v_w_o, v_norm_ffn_g, v_w_up, v_conv_ffn_w, v_w_down, v_norm_final_g):
    me = _my_index()
    me_arr = me.astype(jnp.int32).reshape(1)
    x0, mem0, tgt = x.reshape(x.shape[1:]), mem.reshape(mem.shape[1:]), loss_target.reshape(x.shape[1:])
    up_pad = ((0, 0), (0, 0), (0, C_UP_P - C_UP))

    ag_groups = (("win",), ("wup", "wkv", "w4", "wdn", "cv"))
    kinds = ag_groups[0] + ag_groups[1]
    smalls, ag_handles = [], []
    token = jnp.zeros((8, LANE), F32)
    for l in range(DEPTH):
        cv = jnp.zeros((_CV_ROWS, C_UP_P), F32)
        cv = cv.at[0:K_F, 0:C_UP].set(conv_ffn_w[l]).at[3:3 + K_A, 0:R_O].set(conv_a_w[l])
        cv = cv.at[8:8 + K_B, 0:R_O].set(conv_b_w[l])
        shards = dict(
            win=w_in[l], wup=jnp.pad(w_up[l], up_pad[1:]), wkv=w_kv[l],
            w4=jnp.stack([w_a_out[l], w_b_out[l], w_att_out[l], w_o[l]]), wdn=w_down[l], cv=cv)
        whole = dict({k: SDS(*_WHOLE[k]) for k in kinds[:-1]}, cv=SDS((N_DEV,) + cv.shape, F32))
        lands = {k: _place_own(k, shards[k], whole[k], True, me_arr, name=f"ag_own_{k}_l{l}") for k in kinds}
        per_layer = []
        for g, grp in enumerate(ag_groups):
            handle, token = _gather2_start([lands[k] for k in grp], grp, token, name=f"ag_start_l{l}_g{g}")
            per_layer.append(handle)
        ag_handles.append(per_layer)
        smalls.append(dict(
            norm_mix_g=norm_mix_g[l][None], norm_mem_g=norm_mem_g[l][None], b_gate=b_gate[l][None],
            conv_b_bias=conv_b_bias[l][None], ln_b_g=ln_b_g[l][None], ln_b_b=ln_b_b[l][None],
            norm_ffn_g=norm_ffn_g[l][None]))

    def forward_group(l, g, after):
        ag_handles[l][g], tok = _gather2_forward(ag_handles[l][g], ag_groups[g], after, name=f"ag_forward_l{l}_g{g}")
        return tok

    def rest_of_weights(l):
        def wait(after):
            if l == 0:
                after = forward_group(0, 1, after)
            wup, wkv, w4, wdn, cvg = _gather2_wait(ag_handles[l][1], ag_groups[1], after, name=f"ag_wait_l{l}_g1")
            cw_f = jnp.stack([cvg[d, 0:K_F, :] for d in UP_ORDER], axis=1).reshape(K_F, 2 * FF_P)
            cw_a = cvg[:, 3:3 + K_A, 0:R_O].transpose(1, 0, 2).reshape(K_A, D)
            cw_b = cvg[:, 8:8 + K_B, 0:R_O].transpose(1, 0, 2).reshape(K_B, D)
            return (wup, wkv, w4, wdn, jnp.pad(cw_a, ((0, 8 - K_A), (0, 0))),
                    jnp.pad(cw_b, ((0, 32 - K_B), (0, 0))), jnp.pad(cw_f, ((0, 8 - K_F), (0, 0))))
        return wait

    wts, saved = [], []
    xs = x0
    h = _rms_fwd(xs, smalls[0]["norm_mix_g"], name="rms_fwd")
    behind = forward_group(0, 0, token)

    def next_layer_forwarding(l):
        def hook(after):
            if l + 1 == DEPTH:
                return None
            return forward_group(l + 1, 1, forward_group(l + 1, 0, after))
        return hook

    for l in range(DEPTH):
        g_next = smalls[l + 1]["norm_mix_g"] if l + 1 < DEPTH else norm_final_g[None]
        (win,) = _gather2_wait(ag_handles[l][0], ag_groups[0], behind, name=f"ag_wait_l{l}_g0")
        xs, h, w_l, sv = _layer_fwd(xs, h, mem0, win, rest_of_weights(l), next_layer_forwarding(l), smalls[l],
                                    g_next, f"l{l}")
        behind = h
        wts.append(w_l)
        saved.append(sv)
    dx, dxb, head_sums = _loss_head(xs, tgt, norm_final_g[None], name="loss_head")

    rs_handles = []
    small_grads = [None] * DEPTH

    def start_scatter(grp, arrays, name):
        maps = _scatter_maps(grp)
        lands = [_place_own(k, a, SDS((N_DEV,) + _SHARD[k], BF), False, me_arr, name=f"rs_own_{k}_{name}")
                 for k, a in zip(grp, arrays)]
        handle, tok = _exchange_start(list(arrays), lands, maps, rs_handles[-1][2] if rs_handles else head_sums,
                                      name=f"rs_start_{name}")
        rs_handles.append((grp, handle, tok, name))
        return tok

    for l in reversed(range(DEPTH)):
        dx, dxb, small_grads[l], token = _layer_bwd(dx, dxb, mem0, wts[l], smalls[l], saved[l], start_scatter,
                                                    f"l{l}")

    staged = [dict() for _ in range(DEPTH)]
    for grp, handle, _, name in rs_handles[:-1]:
        staged[int(name[-1])].update(zip(grp, _exchange_wait(handle, _scatter_maps(grp), dx, name=f"rs_wait_{name}")))

    pack = jnp.concatenate(small_grads[0] + small_grads[1] + [head_sums[1:2], head_sums[0:1]], axis=0)
    pack = jnp.pad(pack, ((0, -pack.shape[0] % 8), (0, 0)))
    total = _allreduce_small(pack, name="allreduce_small")
    per_layer = sum(_SMALL_ROWS)
    parts = []
    for l in range(DEPTH):
        at, one = l * per_layer, []
        for rows in _SMALL_ROWS:
            one.append(total[at:at + rows])
            at += rows
        parts.append(one)
    g_final = total[DEPTH * per_layer]
    loss = 0.5 / D * jnp.sum(total[DEPTH * per_layer + 1])

    def both(i):
        return jnp.stack([parts[0][i], parts[1][i]])

    g_norm_mix, g_norm_mem = both(0)[:, 0], both(1)[:, 0]
    g_b_gate = both(2).reshape(DEPTH, 3 * D)
    g_cbias, g_lng, g_lnb, g_norm_ffn = both(3)[:, 0], both(4)[:, 0], both(5)[:, 0], both(6)[:, 0]
    g_conv_a = lax.dynamic_slice_in_dim(both(7), me * R_O, R_O, axis=2)
    g_conv_b = lax.dynamic_slice_in_dim(both(8), me * R_O, R_O, axis=2)
    g_conv_f = lax.dynamic_slice_in_dim(both(9).reshape(DEPTH, K_F, 2 * FF_P), _up_slot(me) * C_UP_P, C_UP, axis=2)

    small_g = [g_norm_mix, g_norm_mem, g_b_gate, g_conv_a, g_conv_b, g_cbias, g_lng, g_lnb, g_norm_ffn, g_conv_f,
               g_final]
    small_w = [norm_mix_g, norm_mem_g, b_gate, conv_a_w, conv_b_w, conv_b_bias, ln_b_g, ln_b_b, norm_ffn_g,
               conv_ffn_w, norm_final_g]
    small_m = [m_norm_mix_g, m_norm_mem_g, m_b_gate, m_conv_a_w, m_conv_b_w, m_conv_b_bias, m_ln_b_g, m_ln_b_b,
               m_norm_ffn_g, m_conv_ffn_w, m_norm_final_g]
    small_v = [v_norm_mix_g, v_norm_mem_g, v_b_gate, v_conv_a_w, v_conv_b_w, v_conv_b_bias, v_ln_b_g, v_ln_b_b,
               v_norm_ffn_g, v_conv_ffn_w, v_norm_final_g]
    upd = _adamw_small(_pack_rows(small_g), _pack_rows(small_w), _pack_rows(small_m), _pack_rows(small_v),
                       name="adamw_small")
    s_d, s_m, s_v = (_unpack_rows(p, small_w) for p in upd)
    (d_norm_mix, d_norm_mem, d_b_gate, d_conv_a, d_conv_b, d_cbias, d_lng, d_lnb, d_norm_ffn, d_conv_f,
     d_final) = s_d
    (nm_norm_mix, nm_norm_mem, nm_b_gate, nm_conv_a, nm_conv_b, nm_cbias, nm_lng, nm_lnb, nm_norm_ffn, nm_conv_f,
     nm_final) = s_m
    (nv_norm_mix, nv_norm_mem, nv_b_gate, nv_conv_a, nv_conv_b, nv_cbias, nv_lng, nv_lnb, nv_norm_ffn, nv_conv_f,
     nv_final) = s_v

    def big_update(kind, w, m, v, name):
        return _adamw_staged(staged[0][kind], staged[1][kind], w, m, v, name=name)

    r_up = [a[:, :, 0:C_UP] for a in big_update("wup", jnp.pad(w_up, up_pad), jnp.pad(m_w_up, up_pad),
                                                jnp.pad(v_w_up, up_pad), "adamw_w_up")]
    r_kv = big_update("wkv", w_kv, m_w_kv, v_w_kv, "adamw_w_kv")
    r_dn = big_update("wdn", w_down, m_w_down, v_w_down, "adamw_w_down")

    def four(a, b, c, d_):
        return jnp.stack([a, b, c, d_], axis=1).reshape(DEPTH, 4 * R_O, D)

    r_4 = _adamw_staged(
        staged[0]["w4"].reshape(N_DEV, 4 * R_O, D), staged[1]["w4"].reshape(N_DEV, 4 * R_O, D),
        four(w_a_out, w_b_out, w_att_out, w_o), four(m_w_a_out, m_w_b_out, m_w_att_out, m_w_o),
        four(v_w_a_out, v_w_b_out, v_w_att_out, v_w_o), name="adamw_w_out")
    grp, handle, _, name = rs_handles[-1]
    staged[0].update(zip(grp, _exchange_wait(handle, _scatter_maps(grp), r_4[0], name=f"rs_wait_{name}")))
    r_in = big_update("win", w_in, m_w_in, v_w_in, "adamw_w_in")
    r_a, r_b, r_att, r_o = ([a.reshape(DEPTH, 4, R_O, D)[:, j] for a in r_4] for j in range(4))

    grads = [g_norm_mix, g_norm_mem, r_in[0], g_b_gate, g_conv_a, r_a[0], g_conv_b, g_cbias, g_lng, g_lnb, r_b[0],
             r_kv[0], r_att[0], r_o[0], g_norm_ffn, r_up[0], g_conv_f, r_dn[0], g_final]
    deltas = [d_norm_mix, d_norm_mem, r_in[1], d_b_gate, d_conv_a, r_a[1], d_conv_b, d_cbias, d_lng, d_lnb, r_b[1],
              r_kv[1], r_att[1], r_o[1], d_norm_ffn, r_up[1], d_conv_f, r_dn[1], d_final]
    new_m = [nm_norm_mix, nm_norm_mem, r_in[2], nm_b_gate, nm_conv_a, r_a[2], nm_conv_b, nm_cbias, nm_lng, nm_lnb,
             r_b[2], r_kv[2], r_att[2], r_o[2], nm_norm_ffn, r_up[2], nm_conv_f, r_dn[2], nm_final]
    new_v = [nv_norm_mix, nv_norm_mem, r_in[3], nv_b_gate, nv_conv_a, r_a[3], nv_conv_b, nv_cbias, nv_lng, nv_lnb,
             r_b[3], r_kv[3], r_att[3], r_o[3], nv_norm_ffn, r_up[3], nv_conv_f, r_dn[3], nv_final]
    return (loss, dx[None], *grads, *deltas, *new_m, *new_v)
```

```python
import functools

import jax
import jax.numpy as jnp
from jax import lax
from jax.experimental import pallas as pl
from jax.experimental.pallas import tpu as pltpu

F32 = jnp.float32
BF = jnp.bfloat16
SDS = jax.ShapeDtypeStruct
MESH = pl.DeviceIdType.MESH
ANY = pl.BlockSpec(memory_space=pl.ANY)

N_DEV = 8
DEPTH = 2
D = 1024
N_HEADS = 4
HEAD = D // N_HEADS
D_FF = 2816
K_A, K_B, K_F = 3, 31, 3
NORM_EPS = 1e-6

C_IN = 9 * D // N_DEV
C_KV = 2 * D // N_DEV
C_UP = 2 * D_FF // N_DEV
LANE = 128
C_UP_P = -(-C_UP // LANE) * LANE
FF_P = 4 * C_UP_P
R_O = D // N_DEV
R_DN = D_FF // N_DEV

VMEM_LIMIT = 56 * 1024 * 1024
TM = 512
TR = 256
SUB = 128
H_S, H_L = 16, 32

ADAM_LR, ADAM_B1, ADAM_B2, ADAM_EPS, ADAM_WD, ADAM_STEP = 0.001, 0.9, 0.999, 1e-08, 0.01, 10

UP_ORDER = (0, 4, 1, 5, 2, 6, 3, 7)


def _pcall(body, **kw):
    return pl.pallas_call(body, **kw)


def _cp(sem=None, **kw):
    return pltpu.CompilerParams(dimension_semantics=sem, vmem_limit_bytes=VMEM_LIMIT, **kw)


def _dot(a, b):
    return jnp.dot(a, b, preferred_element_type=F32)


def _dot_nt(a, b):
    return lax.dot_general(a, b, (((1,), (1,)), ((), ())), preferred_element_type=F32)


def _dot_tn(a, b):
    return lax.dot_general(a, b, (((0,), (0,)), ((), ())), preferred_element_type=F32)


def _sigmoid(z):
    return 1.0 / (1.0 + jnp.exp(-z))


def _rms(xv):
    return lax.rsqrt(jnp.mean(xv * xv, axis=-1, keepdims=True) + NORM_EPS)


def _up_slot(idx):
    return jnp.where(idx < 4, 2 * idx, 2 * (idx - 4) + 1)


def _dn_row(idx):
    return C_UP_P * (idx // 2) + R_DN * (idx % 2)


def _mm(a, b, *, ta=False, tb=False, out_dtype=BF, tm=TM, tn=512, tk=None, name):
    m, k_dim = (a.shape[1], a.shape[0]) if ta else a.shape
    n = b.shape[0] if tb else b.shape[1]
    tm, tn = min(tm, m), min(tn, n)
    tk = k_dim if tk is None else min(tk, k_dim)
    nk = k_dim // tk
    assert m % tm == 0 and n % tn == 0 and k_dim % tk == 0
    dims = (((0 if ta else 1,), (1 if tb else 0,)), ((), ()))

    def body(a_ref, b_ref, o_ref, *scratch):
        part = lax.dot_general(a_ref[...], b_ref[...], dims, preferred_element_type=F32)
        if nk == 1:
            o_ref[...] = part.astype(o_ref.dtype)
            return
        acc = scratch[0]
        k = pl.program_id(2)

        @pl.when(k == 0)
        def _():
            acc[...] = part

        @pl.when(k > 0)
        def _():
            acc[...] += part

        @pl.when(k == nk - 1)
        def _():
            o_ref[...] = acc[...].astype(o_ref.dtype)

    a_spec = pl.BlockSpec((tk, tm), lambda i, j, k: (k, i)) if ta else pl.BlockSpec((tm, tk), lambda i, j, k: (i, k))
    b_spec = pl.BlockSpec((tn, tk), lambda i, j, k: (j, k)) if tb else pl.BlockSpec((tk, tn), lambda i, j, k: (k, j))
    return _pcall(
        body, grid=(m // tm, n // tn, nk), in_specs=[a_spec, b_spec],
        out_specs=pl.BlockSpec((tm, tn), lambda i, j, k: (i, j)),
        out_shape=SDS((m, n), out_dtype),
        scratch_shapes=[pltpu.VMEM((tm, tn), F32)] if nk > 1 else [],
        compiler_params=_cp(("parallel", "parallel", "arbitrary")), name=name)(a, b)


def _mm_res_norm(a, w, x, g, *, name):
    s, k_dim = a.shape
    tm = min(TM, s)

    def body(a_ref, w_ref, x_ref, g_ref, xo_ref, h_ref):
        xo = x_ref[...] + _dot(a_ref[...], w_ref[...])
        xo_ref[...] = xo
        h_ref[...] = ((xo * _rms(xo)) * g_ref[...]).astype(BF)

    return _pcall(
        body, grid=(s // tm,),
        in_specs=[pl.BlockSpec((tm, k_dim), lambda i: (i, 0)), pl.BlockSpec((k_dim, D), lambda i: (0, 0)),
                  pl.BlockSpec((tm, D), lambda i: (i, 0)), pl.BlockSpec((1, D), lambda i: (0, 0))],
        out_specs=[pl.BlockSpec((tm, D), lambda i: (i, 0))] * 2,
        out_shape=[SDS((s, D), F32), SDS((s, D), BF)],
        compiler_params=_cp(("parallel",)), name=name)(a, w, x, g)


def _mm_nt_normbwd(da, w, x, dres, g, *, tk, name):
    s, k_dim = da.shape
    tm = min(TM, s)
    nk = k_dim // tk
    assert k_dim % tk == 0

    def body(da_ref, w_ref, x_ref, dres_ref, g_ref, dx_ref, dxb_ref, dg_ref, acc):
        i, k = pl.program_id(0), pl.program_id(1)
        part = _dot_nt(da_ref[...], w_ref[...])

        @pl.when(k == 0)
        def _():
            acc[...] = part

        @pl.when(k > 0)
        def _():
            acc[...] += part

        @pl.when((i == 0) & (k == 0))
        def _():
            dg_ref[...] = jnp.zeros_like(dg_ref)

        @pl.when(k == nk - 1)
        def _():
            dh = acc[...]
            xv = x_ref[...]
            r = _rms(xv)
            xn = xv * r
            dg_ref[0:1, :] += jnp.sum(dh * xn, axis=0, keepdims=True)
            dxn = dh * g_ref[...]
            dx = dres_ref[...] + r * (dxn - xn * jnp.mean(dxn * xn, axis=-1, keepdims=True))
            dx_ref[...] = dx
            dxb_ref[...] = dx.astype(BF)

    row = lambda i, k: (i, 0)
    return _pcall(
        body, grid=(s // tm, nk),
        in_specs=[pl.BlockSpec((tm, tk), lambda i, k: (i, k)), pl.BlockSpec((D, tk), lambda i, k: (0, k)),
                  pl.BlockSpec((tm, D), row), pl.BlockSpec((tm, D), row), pl.BlockSpec((1, D), lambda i, k: (0, 0))],
        out_specs=[pl.BlockSpec((tm, D), row), pl.BlockSpec((tm, D), row), pl.BlockSpec((8, D), lambda i, k: (0, 0))],
        out_shape=[SDS((s, D), F32), SDS((s, D), BF), SDS((8, D), F32)],
        scratch_shapes=[pltpu.VMEM((tm, D), F32)],
        compiler_params=_cp(("arbitrary", "arbitrary")), name=name)(da, w, x, dres, g)


def _rms_fwd(x, g, *, name):
    s = x.shape[0]
    tm = min(TM, s)

    def body(x_ref, g_ref, h_ref):
        xv = x_ref[...]
        h_ref[...] = ((xv * _rms(xv)) * g_ref[...]).astype(BF)

    return _pcall(
        body, grid=(s // tm,),
        in_specs=[pl.BlockSpec((tm, D), lambda i: (i, 0)), pl.BlockSpec((1, D), lambda i: (0, 0))],
        out_specs=pl.BlockSpec((tm, D), lambda i: (i, 0)), out_shape=SDS((s, D), BF),
        compiler_params=_cp(("parallel",)), name=name)(x, g)


def _loss_head(x, tgt, g, *, name):
    s = x.shape[0]
    tm = min(TM, s)

    def body(x_ref, t_ref, g_ref, dx_ref, dxb_ref, sums_ref):
        @pl.when(pl.program_id(0) == 0)
        def _():
            sums_ref[...] = jnp.zeros_like(sums_ref)

        xv = x_ref[...]
        r = _rms(xv)
        xn = xv * r
        diff = xn * g_ref[...] - t_ref[...]
        sums_ref[0:1, :] += jnp.sum(diff * diff, axis=0, keepdims=True)
        dy = diff * (1.0 / D)
        sums_ref[1:2, :] += jnp.sum(dy * xn, axis=0, keepdims=True)
        dxn = dy * g_ref[...]
        dx = r * (dxn - xn * jnp.mean(dxn * xn, axis=-1, keepdims=True))
        dx_ref[...] = dx
        dxb_ref[...] = dx.astype(BF)

    row = lambda i: (i, 0)
    return _pcall(
        body, grid=(s // tm,),
        in_specs=[pl.BlockSpec((tm, D), row), pl.BlockSpec((tm, D), row), pl.BlockSpec((1, D), lambda i: (0, 0))],
        out_specs=[pl.BlockSpec((tm, D), row), pl.BlockSpec((tm, D), row), pl.BlockSpec((8, D), lambda i: (0, 0))],
        out_shape=[SDS((s, D), F32), SDS((s, D), BF), SDS((8, D), F32)],
        compiler_params=_cp(("arbitrary",)), name=name)(x, tgt, g)


def _halo_before(i, tr, h):
    return jnp.maximum(i * (tr // h) - 1, 0)


def _halo_after(i, tr, h, s):
    return jnp.minimum((i + 1) * (tr // h), s // h - 1)


def _taps(buf, w_ref, sl, k_w, base, rows):
    acc = None
    for k in range(k_w):
        t = w_ref[k:k + 1, sl] * buf[base + k:base + k + rows, sl]
        acc = t if acc is None else acc + t
    return acc


def _taps_rev(buf, w_ref, sl, k_w, rows):
    acc = None
    for k in range(k_w):
        t = w_ref[k:k + 1, sl] * buf[k_w - 1 - k:k_w - 1 - k + rows, sl]
        acc = t if acc is None else acc + t
    return acc


def _tap_grads(dw_ref, dc, buf, sl, k_w, base, rows):
    for k in range(k_w):
        dw_ref[k:k + 1, sl] += jnp.sum(dc * buf[base + k:base + k + rows, sl], axis=0, keepdims=True)


def _bra_fwd(proj, cw, *, name):
    s = proj.shape[0]
    tr, h = min(TR, s), H_S
    sub = min(SUB, tr)

    def body(cur, halo, w_ref, za_ref, cvb):
        i = pl.program_id(0)
        hv = halo[:, D:2 * D].astype(F32) * halo[:, 2 * D:3 * D].astype(F32)
        cvb[0:h, :] = jnp.where(i == 0, 0.0, hv)
        cvb[h:h + tr, :] = cur[:, D:2 * D].astype(F32) * cur[:, 2 * D:3 * D].astype(F32)
        for c in range(D // LANE):
            sl = slice(LANE * c, LANE * c + LANE)
            ca = _taps(cvb, w_ref, sl, K_A, h - (K_A - 1), tr)
            za_ref[:, sl] = (cur[:, sl].astype(F32) * ca).astype(BF)

    return _pcall(
        body, grid=(s // tr,),
        in_specs=[pl.BlockSpec((tr, 3 * D), lambda i: (i, 0)),
                  pl.BlockSpec((h, 3 * D), lambda i: (_halo_before(i, tr, h), 0)),
                  pl.BlockSpec((8, D), lambda i: (0, 0))],
        out_specs=pl.BlockSpec((tr, D), lambda i: (i, 0)), out_shape=SDS((s, D), BF),
        scratch_shapes=[pltpu.VMEM((h + tr, D), F32)],
        compiler_params=_cp(("parallel",)), name=name)(proj, proj, cw)


def _bra_bwd(proj, dza, cw, dproj, *, name):
    s = proj.shape[0]
    tr, h = min(TR, s), H_S
    sub = min(SUB, tr)
    n = s // tr

    def body(before, cur, after, dz_cur, dz_after, w_ref, dproj_in, da_ref, dw_ref, cvb, dcab):
        del dproj_in
        i = pl.program_id(0)

        @pl.when(i == 0)
        def _():
            dw_ref[...] = jnp.zeros_like(dw_ref)

        first, last = i == 0, i == n - 1
        cvb[0:h, :] = jnp.where(first, 0.0, before[:, D:2 * D].astype(F32) * before[:, 2 * D:3 * D].astype(F32))
        cvb[h:h + tr, :] = cur[:, D:2 * D].astype(F32) * cur[:, 2 * D:3 * D].astype(F32)
        dcab[0:tr, :] = dz_cur[...].astype(F32) * cur[:, 0:D].astype(F32)
        dcab[tr:tr + h, :] = jnp.where(last, 0.0, dz_after[...].astype(F32) * after[:, 0:D].astype(F32))
        for c in range(D // LANE):
            sl = slice(LANE * c, LANE * c + LANE)
            gc = cur[:, D + LANE * c:D + LANE * c + LANE].astype(F32)
            v = cur[:, 2 * D + LANE * c:2 * D + LANE * c + LANE].astype(F32)
            ca = _taps(cvb, w_ref, sl, K_A, h - (K_A - 1), tr)
            da_ref[:, sl] = (dz_cur[:, sl].astype(F32) * ca).astype(BF)
            dcv = _taps_rev(dcab, w_ref, sl, K_A, tr)
            da_ref[:, D + LANE * c:D + LANE * c + LANE] = (dcv * v).astype(BF)
            da_ref[:, 2 * D + LANE * c:2 * D + LANE * c + LANE] = (dcv * gc).astype(BF)
            _tap_grads(dw_ref, dcab[0:tr, sl], cvb, sl, K_A, h - (K_A - 1), tr)

    return _pcall(
        body, grid=(n,),
        in_specs=[pl.BlockSpec((h, 3 * D), lambda i: (_halo_before(i, tr, h), 0)),
                  pl.BlockSpec((tr, 3 * D), lambda i: (i, 0)),
                  pl.BlockSpec((h, 3 * D), lambda i: (_halo_after(i, tr, h, s), 0)),
                  pl.BlockSpec((tr, D), lambda i: (i, 0)),
                  pl.BlockSpec((h, D), lambda i: (_halo_after(i, tr, h, s), 0)),
                  pl.BlockSpec((8, D), lambda i: (0, 0)), ANY],
        out_specs=[pl.BlockSpec((tr, 3 * D), lambda i: (i, 0)), pl.BlockSpec((8, D), lambda i: (0, 0))],
        out_shape=[SDS(dproj.shape, BF), SDS((8, D), F32)], input_output_aliases={6: 0},
        scratch_shapes=[pltpu.VMEM((h + tr, D), F32), pltpu.VMEM((tr + h, D), F32)],
        compiler_params=_cp(("arbitrary",)), name=name)(proj, proj, proj, dza, dza, cw, dproj)


_U_COL, _UG_COL = 3, 4


def _brb_conv_fwd(proj, cw, bias, *, name):
    s = proj.shape[0]
    tr, h = min(TR, s), H_L
    sub = min(SUB, tr)

    def body(u_cur, ug_cur, u_halo, ug_halo, w_ref, b_ref, cb_ref, glb, shifted):
        i = pl.program_id(0)
        glb[0:h, :] = jnp.where(i == 0, 0.0, u_halo[...].astype(F32) * _sigmoid(ug_halo[...].astype(F32)))
        glb[h:h + tr, :] = u_cur[...].astype(F32) * _sigmoid(ug_cur[...].astype(F32))
        for c in range(D // LANE):
            sl = slice(LANE * c, LANE * c + LANE)
            for r in range(1, 8):
                shifted[r] = glb[8 - r:8 - r + tr + 24, sl]
            for r0 in range(0, tr, sub):
                acc = None
                for k in range(K_B):
                    q, r = divmod(K_B - 1 - k, 8)
                    at = r0 - 8 * q
                    win = shifted[r, 24 + at:24 + at + sub, :] if r else glb[h + at:h + at + sub, sl]
                    term = w_ref[k:k + 1, sl] * win
                    acc = term if acc is None else acc + term
                cb_ref[r0:r0 + sub, sl] = (acc + b_ref[:, sl]).astype(BF)

    return _pcall(
        body, grid=(s // tr,),
        in_specs=[pl.BlockSpec((tr, D), lambda i: (i, _U_COL)), pl.BlockSpec((tr, D), lambda i: (i, _UG_COL)),
                  pl.BlockSpec((h, D), lambda i: (_halo_before(i, tr, h), _U_COL)),
                  pl.BlockSpec((h, D), lambda i: (_halo_before(i, tr, h), _UG_COL)),
                  pl.BlockSpec((32, D), lambda i: (0, 0)), pl.BlockSpec((1, D), lambda i: (0, 0))],
        out_specs=pl.BlockSpec((tr, D), lambda i: (i, 0)), out_shape=SDS((s, D), BF),
        scratch_shapes=[pltpu.VMEM((h + tr, D), F32), pltpu.VMEM((8, tr + 24, LANE), F32)],
        compiler_params=_cp(("parallel",)), name=name)(proj, proj, proj, proj, cw, bias)


def _brb_conv_bwd(proj, dcb, dq, cw, dproj, *, name):
    s = proj.shape[0]
    tr, h = min(TR, s), H_L
    sub = min(SUB, tr)
    n = s // tr

    def body(u_cur, ug_cur, d_cur, d_after, dq_ref, w_ref, dproj_in, db_ref, dw_ref, dcbb, shifted):
        del dproj_in
        i = pl.program_id(0)

        @pl.when(i == 0)
        def _():
            dw_ref[...] = jnp.zeros_like(dw_ref)

        db_ref[:, 2 * D:3 * D] = dq_ref[...]
        dcbb[0:tr, :] = d_cur[...].astype(F32)
        dcbb[tr:tr + h, :] = jnp.where(i == n - 1, 0.0, d_after[...].astype(F32))
        for c in range(D // LANE):
            sl = slice(LANE * c, LANE * c + LANE)
            for r in range(1, 8):
                shifted[r] = dcbb[r:r + tr + 24, sl]
            for r0 in range(0, tr, sub):
                u = u_cur[r0:r0 + sub, sl].astype(F32)
                sg = _sigmoid(ug_cur[r0:r0 + sub, sl].astype(F32))
                glu = u * sg
                dglu = None
                for k in range(K_B):
                    q, r = divmod(K_B - 1 - k, 8)
                    at = r0 + 8 * q
                    win = shifted[r, at:at + sub, :] if r else dcbb[at:at + sub, sl]
                    term = w_ref[k:k + 1, sl] * win
                    dglu = term if dglu is None else dglu + term
                    dw_ref[k:k + 1, sl] += jnp.sum(win * glu, axis=0, keepdims=True)
                db_ref[r0:r0 + sub, sl] = (dglu * sg).astype(BF)
                db_ref[r0:r0 + sub, D + LANE * c:D + LANE * c + LANE] = (dglu * u * sg * (1.0 - sg)).astype(BF)

    return _pcall(
        body, grid=(n,),
        in_specs=[pl.BlockSpec((tr, D), lambda i: (i, _U_COL)), pl.BlockSpec((tr, D), lambda i: (i, _UG_COL)),
                  pl.BlockSpec((tr, D), lambda i: (i, 0)),
                  pl.BlockSpec((h, D), lambda i: (_halo_after(i, tr, h, s), 0)),
                  pl.BlockSpec((tr, D), lambda i: (i, 0)),
                  pl.BlockSpec((32, D), lambda i: (0, 0)), ANY],
        out_specs=[pl.BlockSpec((tr, 3 * D), lambda i: (i, 1)), pl.BlockSpec((32, D), lambda i: (0, 0))],
        out_shape=[SDS(dproj.shape, BF), SDS((32, D), F32)], input_output_aliases={6: 0},
        scratch_shapes=[pltpu.VMEM((tr + h, D), F32), pltpu.VMEM((8, tr + 24, LANE), F32)],
        compiler_params=_cp(("arbitrary",)), name=name)(proj, proj, dcb, dcb, dq, cw, dproj)


def _ln_silu_fwd(cb, g, b, *, name):
    s = cb.shape[0]
    tm = min(TM, s)

    def body(cb_ref, g_ref, b_ref, sb_ref):
        z = cb_ref[...].astype(F32)
        zc = z - jnp.mean(z, axis=-1, keepdims=True)
        ln = (zc * lax.rsqrt(jnp.mean(zc * zc, axis=-1, keepdims=True) + NORM_EPS)) * g_ref[...] + b_ref[...]
        sb_ref[...] = (ln * _sigmoid(ln)).astype(BF)

    row = lambda i: (i, 0)
    vec = pl.BlockSpec((1, D), lambda i: (0, 0))
    return _pcall(
        body, grid=(s // tm,), in_specs=[pl.BlockSpec((tm, D), row), vec, vec],
        out_specs=pl.BlockSpec((tm, D), row), out_shape=SDS((s, D), BF),
        compiler_params=_cp(("parallel",)), name=name)(cb, g, b)


def _ln_silu_bwd(cb, dsb, g, b, *, name):
    s = cb.shape[0]
    tm = min(TM, s)

    def body(cb_ref, dsb_ref, g_ref, b_ref, dcb_ref, sums_ref):
        @pl.when(pl.program_id(0) == 0)
        def _():
            sums_ref[...] = jnp.zeros_like(sums_ref)

        z = cb_ref[...].astype(F32)
        zc = z - jnp.mean(z, axis=-1, keepdims=True)
        rstd = lax.rsqrt(jnp.mean(zc * zc, axis=-1, keepdims=True) + NORM_EPS)
        lnh = zc * rstd
        ln = lnh * g_ref[...] + b_ref[...]
        sg = _sigmoid(ln)
        dln = dsb_ref[...].astype(F32) * (sg * (1.0 + ln * (1.0 - sg)))
        sums_ref[0:1, :] += jnp.sum(dln * lnh, axis=0, keepdims=True)
        sums_ref[1:2, :] += jnp.sum(dln, axis=0, keepdims=True)
        dlnh = dln * g_ref[...]
        dz = rstd * (dlnh - jnp.mean(dlnh, axis=-1, keepdims=True)
                     - lnh * jnp.mean(dlnh * lnh, axis=-1, keepdims=True))
        sums_ref[2:3, :] += jnp.sum(dz, axis=0, keepdims=True)
        dcb_ref[...] = dz.astype(BF)

    row = lambda i: (i, 0)
    vec = pl.BlockSpec((1, D), lambda i: (0, 0))
    return _pcall(
        body, grid=(s // tm,), in_specs=[pl.BlockSpec((tm, D), row), pl.BlockSpec((tm, D), row), vec, vec],
        out_specs=[pl.BlockSpec((tm, D), row), pl.BlockSpec((8, D), lambda i: (0, 0))],
        out_shape=[SDS((s, D), BF), SDS((8, D), F32)],
        compiler_params=_cp(("arbitrary",)), name=name)(cb, dsb, g, b)


_Q_COL = 5 * D // HEAD


def _kv_prep(mem, g, wkv, *, name):
    m = mem.shape[0]

    def body(mem_ref, g_ref, w_ref, memn_ref, kv_ref):
        mv = mem_ref[...]
        memn = ((mv * _rms(mv)) * g_ref[...]).astype(BF)
        memn_ref[...] = memn
        for dev in range(N_DEV):
            kv_ref[:, dev * C_KV:(dev + 1) * C_KV] = _dot(memn, w_ref[dev]).astype(BF)

    return _pcall(body, out_shape=[SDS((m, D), BF), SDS((m, 2 * D), BF)],
                  compiler_params=_cp(), name=name)(mem, g, wkv)


def _softmax_rows(q, k):
    sc = _dot_nt(q, k) * (1.0 / (HEAD ** 0.5))
    e = jnp.exp(sc - jnp.max(sc, axis=-1, keepdims=True))
    return e / jnp.sum(e, axis=-1, keepdims=True)


def _attn_fwd(proj, kv, *, name):
    s, m = proj.shape[0], kv.shape[0]
    tm = min(TM, s)

    def body(q_ref, k_ref, v_ref, o_ref):
        p = _softmax_rows(q_ref[...], k_ref[...])
        o_ref[...] = _dot(p.astype(BF), v_ref[...]).astype(BF)

    return _pcall(
        body, grid=(s // tm, N_HEADS),
        in_specs=[pl.BlockSpec((tm, HEAD), lambda i, hd: (i, _Q_COL + hd)),
                  pl.BlockSpec((m, HEAD), lambda i, hd: (0, hd)),
                  pl.BlockSpec((m, HEAD), lambda i, hd: (0, N_HEADS + hd))],
        out_specs=pl.BlockSpec((tm, HEAD), lambda i, hd: (i, hd)), out_shape=SDS((s, D), BF),
        compiler_params=_cp(("parallel", "parallel")), name=name)(proj, kv, kv)


def _attn_bwd(proj, kv, do, *, name):
    s, m = proj.shape[0], kv.shape[0]
    tm = min(TM, s)

    def body(q_ref, k_ref, v_ref, do_ref, dq_ref, dk_ref, dv_ref):
        @pl.when(pl.program_id(1) == 0)
        def _():
            dk_ref[...] = jnp.zeros_like(dk_ref)
            dv_ref[...] = jnp.zeros_like(dv_ref)

        q, k, dov = q_ref[...], k_ref[...], do_ref[...]
        p = _softmax_rows(q, k)
        dp = _dot_nt(dov, v_ref[...])
        dv_ref[...] += _dot_tn(p.astype(BF), dov)
        ds = (p * (dp - jnp.sum(dp * p, axis=-1, keepdims=True)) * (1.0 / (HEAD ** 0.5))).astype(BF)
        dq_ref[...] = _dot(ds, k).astype(BF)
        dk_ref[...] += _dot_tn(ds, q)

    return _pcall(
        body, grid=(N_HEADS, s // tm),
        in_specs=[pl.BlockSpec((tm, HEAD), lambda hd, i: (i, _Q_COL + hd)),
                  pl.BlockSpec((m, HEAD), lambda hd, i: (0, hd)),
                  pl.BlockSpec((m, HEAD), lambda hd, i: (0, N_HEADS + hd)),
                  pl.BlockSpec((tm, HEAD), lambda hd, i: (i, hd))],
        out_specs=[pl.BlockSpec((tm, HEAD), lambda hd, i: (i, hd)),
                   pl.BlockSpec((m, HEAD), lambda hd, i: (0, hd)),
                   pl.BlockSpec((m, HEAD), lambda hd, i: (0, hd))],
        out_shape=[SDS((s, D), BF), SDS((m, D), F32), SDS((m, D), F32)],
        compiler_params=_cp(("parallel", "arbitrary")), name=name)(proj, kv, kv, do)


def _kv_bwd(mem, g, memn, dk, dv, wkv, *, name):
    def body(mem_ref, g_ref, memn_ref, dk_ref, dv_ref, w_ref, dw_ref, dg_ref):
        memn = memn_ref[...]
        dmemn = None
        for dev in range(N_DEV):
            d_ref, col = (dk_ref, dev) if dev < N_HEADS else (dv_ref, dev - N_HEADS)
            dslab = d_ref[:, col * C_KV:(col + 1) * C_KV].astype(BF)
            dw_ref[dev] = _dot_tn(memn, dslab).astype(BF)
            part = _dot_nt(dslab, w_ref[dev])
            dmemn = part if dmemn is None else dmemn + part
        mv = mem_ref[...]
        dg_ref[...] = jnp.zeros_like(dg_ref)
        dg_ref[0:1, :] = jnp.sum(dmemn * (mv * _rms(mv)), axis=0, keepdims=True)

    assert C_KV == HEAD
    return _pcall(body, out_shape=[SDS((N_DEV, D, C_KV), BF), SDS((8, D), F32)],
                  compiler_params=_cp(), name=name)(mem, g, memn, dk, dv, wkv)


_TM_MIX = 256


def _mix_out(x, za, sb, o, proj, w4, bg, g_next, *, name):
    s = x.shape[0]
    tm = min(_TM_MIX, s)

    def body(x_ref, za_ref, sb_ref, o_ref, pg_ref, w4_ref, bg_ref, gn_ref,
             ya_ref, yb_ref, yc_ref, mg_ref, x1_ref, h_ref):
        ys = (_dot(za_ref[...], w4_ref[0]), _dot(sb_ref[...], w4_ref[1]), _dot(o_ref[...], w4_ref[2]))
        merged = None
        for j, (y, y_ref) in enumerate(zip(ys, (ya_ref, yb_ref, yc_ref))):
            y_ref[...] = y.astype(BF)
            gate = _sigmoid(pg_ref[:, j * D:(j + 1) * D].astype(F32) + bg_ref[:, j * D:(j + 1) * D])
            merged = gate * y if merged is None else merged + gate * y
        mg = merged.astype(BF)
        mg_ref[...] = mg
        x1 = x_ref[...] + _dot(mg, w4_ref[3])
        x1_ref[...] = x1
        h_ref[...] = ((x1 * _rms(x1)) * gn_ref[...]).astype(BF)

    row = lambda i: (i, 0)
    act = pl.BlockSpec((tm, D), row)
    return _pcall(
        body, grid=(s // tm,),
        in_specs=[act, act, act, act, pl.BlockSpec((tm, 3 * D), lambda i: (i, 2)),
                  pl.BlockSpec((4, D, D), lambda i: (0, 0, 0)), pl.BlockSpec((1, 3 * D), lambda i: (0, 0)),
                  pl.BlockSpec((1, D), lambda i: (0, 0))],
        out_specs=[act] * 6,
        out_shape=[SDS((s, D), BF)] * 4 + [SDS((s, D), F32), SDS((s, D), BF)],
        compiler_params=_cp(("parallel",)), name=name)(x, za, sb, o, proj, w4, bg, g_next)


def _mix_bwd(dxb, ya, yb, yc, proj, w4, bg, *, name):
    s = dxb.shape[0]
    tm = min(_TM_MIX, s)

    def body(dx_ref, ya_ref, yb_ref, yc_ref, pg_ref, w4_ref, bg_ref,
             dya_ref, dyb_ref, dyc_ref, dza_ref, dsb_ref, do_ref, dgt_ref, dbg_ref):
        @pl.when(pl.program_id(0) == 0)
        def _():
            dbg_ref[...] = jnp.zeros_like(dbg_ref)

        dm = _dot_nt(dx_ref[...], w4_ref[3])
        for j, (y_ref, dy_ref, din_ref) in enumerate(zip((ya_ref, yb_ref, yc_ref), (dya_ref, dyb_ref, dyc_ref),
                                                         (dza_ref, dsb_ref, do_ref))):
            cols = slice(j * D, (j + 1) * D)
            gate = _sigmoid(pg_ref[:, cols].astype(F32) + bg_ref[:, cols])
            dy = (dm * gate).astype(BF)
            dy_ref[...] = dy
            din_ref[...] = _dot_nt(dy, w4_ref[j]).astype(BF)
            dpre = dm * y_ref[...].astype(F32) * gate * (1.0 - gate)
            dgt_ref[:, cols] = dpre.astype(BF)
            dbg_ref[0:1, cols] += jnp.sum(dpre, axis=0, keepdims=True)

    row = lambda i: (i, 0)
    act = pl.BlockSpec((tm, D), row)
    return _pcall(
        body, grid=(s // tm,),
        in_specs=[act, act, act, act, pl.BlockSpec((tm, 3 * D), lambda i: (i, 2)),
                  pl.BlockSpec((4, D, D), lambda i: (0, 0, 0)), pl.BlockSpec((1, 3 * D), lambda i: (0, 0))],
        out_specs=[act] * 6 + [pl.BlockSpec((tm, 3 * D), lambda i: (i, 2)),
                               pl.BlockSpec((8, 3 * D), lambda i: (0, 0))],
        out_shape=[SDS((s, D), BF)] * 6 + [SDS((s, 9 * D), BF), SDS((8, 3 * D), F32)],
        compiler_params=_cp(("arbitrary",)), name=name)(dxb, ya, yb, yc, proj, w4, bg)


_PAIR = 2 * C_UP_P


def _ffn_act(u2, cw, *, name):
    s = u2.shape[0]
    tr, h = min(TR, s), H_S
    sub = min(SUB, tr)

    def body(cur, halo, w_ref, act_ref, c2_ref, ub):
        i = pl.program_id(1)
        ub[0:h, :] = jnp.where(i == 0, 0.0, halo[...].astype(F32))
        ub[h:h + tr, :] = cur[...].astype(F32)
        for c in range(C_UP_P // LANE):
            gl = slice(LANE * c, LANE * c + LANE)
            ul = slice(C_UP_P + LANE * c, C_UP_P + LANE * c + LANE)
            for r0 in range(0, tr, sub):
                gt = _taps(ub, w_ref, gl, K_F, h - (K_F - 1) + r0, sub)
                up = _taps(ub, w_ref, ul, K_F, h - (K_F - 1) + r0, sub)
                c2_ref[r0:r0 + sub, gl] = gt.astype(BF)
                c2_ref[r0:r0 + sub, ul] = up.astype(BF)
                act_ref[r0:r0 + sub, gl] = (gt * _sigmoid(gt) * up).astype(BF)

    return _pcall(
        body, grid=(4, s // tr),
        in_specs=[pl.BlockSpec((tr, _PAIR), lambda p, i: (i, p)),
                  pl.BlockSpec((h, _PAIR), lambda p, i: (_halo_before(i, tr, h), p)),
                  pl.BlockSpec((8, _PAIR), lambda p, i: (0, p))],
        out_specs=[pl.BlockSpec((tr, C_UP_P), lambda p, i: (i, p)), pl.BlockSpec((tr, _PAIR), lambda p, i: (i, p))],
        out_shape=[SDS((s, FF_P), BF), SDS((s, 2 * FF_P), BF)],
        scratch_shapes=[pltpu.VMEM((h + tr, _PAIR), F32)],
        compiler_params=_cp(("parallel", "parallel")), name=name)(u2, u2, cw)


def _ffn_bwd(u2, c2, dact, cw, *, name):
    s = u2.shape[0]
    tr, h = min(TR, s), H_S
    sub = min(SUB, tr)
    n = s // tr
    ext = tr + h

    def body(u_cur, c_cur, c_after, da_cur, da_after, w_ref, du_ref, dw_ref, dcb):
        i = pl.program_id(1)
        last = i == n - 1

        @pl.when(i == 0)
        def _():
            dw_ref[...] = jnp.zeros_like(dw_ref)

        def conv_grad(gt, up, da):
            gt, up, da = gt.astype(F32), up.astype(F32), da.astype(F32)
            sg = _sigmoid(gt)
            return da * up * (sg * (1.0 + gt * (1.0 - sg))), da * (gt * sg)

        for c in range(C_UP_P // LANE):
            gl = slice(LANE * c, LANE * c + LANE)
            ul = slice(C_UP_P + LANE * c, C_UP_P + LANE * c + LANE)
            for r0 in range(0, tr, sub):
                rows = slice(r0, r0 + sub)
                dcb[rows, gl], dcb[rows, ul] = conv_grad(c_cur[rows, gl], c_cur[rows, ul], da_cur[rows, gl])
            dg, du_ = conv_grad(c_after[:, gl], c_after[:, ul], da_after[:, gl])
            dcb[tr:ext, gl], dcb[tr:ext, ul] = jnp.where(last, 0.0, dg), jnp.where(last, 0.0, du_)
        for c in range(_PAIR // LANE):
            sl = slice(LANE * c, LANE * c + LANE)
            for r0 in range(0, tr, sub):
                u = u_cur[r0:r0 + sub, sl].astype(F32)
                du = None
                for k in range(K_F):
                    at = r0 + K_F - 1 - k
                    win = dcb[at:at + sub, sl]
                    term = w_ref[k:k + 1, sl] * win
                    du = term if du is None else du + term
                    dw_ref[k:k + 1, sl] += jnp.sum(win * u, axis=0, keepdims=True)
                du_ref[r0:r0 + sub, sl] = du.astype(BF)

    return _pcall(
        body, grid=(4, n),
        in_specs=[pl.BlockSpec((tr, _PAIR), lambda p, i: (i, p)),
                  pl.BlockSpec((tr, _PAIR), lambda p, i: (i, p)),
                  pl.BlockSpec((h, _PAIR), lambda p, i: (_halo_after(i, tr, h, s), p)),
                  pl.BlockSpec((tr, C_UP_P), lambda p, i: (i, p)),
                  pl.BlockSpec((h, C_UP_P), lambda p, i: (_halo_after(i, tr, h, s), p)),
                  pl.BlockSpec((8, _PAIR), lambda p, i: (0, p))],
        out_specs=[pl.BlockSpec((tr, _PAIR), lambda p, i: (i, p)), pl.BlockSpec((8, _PAIR), lambda p, i: (0, p))],
        out_shape=[SDS((s, 2 * FF_P), BF), SDS((8, 2 * FF_P), F32)],
        scratch_shapes=[pltpu.VMEM((ext, _PAIR), F32)],
        compiler_params=_cp(("parallel", "arbitrary")), name=name)(u2, c2, c2, dact, dact, cw)


def _relations():
    x, y, c = lax.axis_index("x"), lax.axis_index("y"), lax.axis_index("c")
    out = []
    for r in range(1, N_DEV):
        rx, ry, rc = (r >> 2) & 1, (r >> 1) & 1, r & 1
        out.append((r, (x ^ rx, y ^ ry, c ^ rc)))
    return out


def _my_index():
    return 4 * lax.axis_index("x") + 2 * lax.axis_index("y") + lax.axis_index("c")


def _exchange(n_arrays, src_of, dst_of, refs):
    ssem, rsem, lsem = refs
    me = _my_index()
    local = []
    for a in range(n_arrays):
        loc = pltpu.make_async_copy(src_of(a, me), dst_of(a, me), lsem.at[a])
        loc.start()
        local.append(loc)

    def copy(a, r, peer, src_idx, dst_idx):
        return pltpu.make_async_remote_copy(
            src_ref=src_of(a, src_idx), dst_ref=dst_of(a, dst_idx), send_sem=ssem.at[a, r - 1],
            recv_sem=rsem.at[a, r - 1], device_id=peer, device_id_type=MESH)

    peers = [(r, peer, 4 * peer[0] + 2 * peer[1] + peer[2]) for r, peer in _relations()]
    for r, peer, p_idx in peers:
        for a in range(n_arrays):
            copy(a, r, peer, p_idx, me).start()
    for r, peer, p_idx in peers:
        for a in range(n_arrays):
            copy(a, r, peer, p_idx, me).wait_send()
            copy(a, r, peer, me, p_idx).wait_recv()
    for loc in local:
        loc.wait()


def _sem_scratch(n_arrays):
    return [pltpu.SemaphoreType.DMA((n_arrays, N_DEV - 1)), pltpu.SemaphoreType.DMA((n_arrays, N_DEV - 1)),
            pltpu.SemaphoreType.DMA((n_arrays,))]


def _slab(kind, ref, idx):
    if kind == "win":
        return ref.at[:, pl.ds(pl.multiple_of(idx * C_IN, LANE), C_IN)]
    if kind == "wup":
        return ref.at[:, pl.ds(pl.multiple_of(_up_slot(idx) * C_UP_P, LANE), C_UP_P)]
    if kind == "wkv":
        return ref.at[idx]
    if kind == "w4":
        return ref.at[:, pl.ds(pl.multiple_of(idx * R_O, 16), R_O), :]
    if kind == "wdn":
        return ref.at[pl.ds(pl.multiple_of(_dn_row(idx), 16), R_DN), :]
    assert kind == "cv"
    return ref.at[idx]


_WHOLE = {"win": ((D, 9 * D), BF), "wup": ((D, 2 * FF_P), BF), "wkv": ((N_DEV, D, C_KV), BF),
          "w4": ((4, D, D), BF), "wdn": ((FF_P, D), BF)}
_SHARD = {"win": (D, C_IN), "wup": (D, C_UP_P), "wkv": (D, C_KV), "w4": (4, R_O, D), "wdn": (R_DN, D)}
HBM_SPEC = pl.BlockSpec(memory_space=pltpu.HBM)
SEM_SPEC = pl.BlockSpec(memory_space=pltpu.SEMAPHORE)
_DATAFLOW = pltpu.SideEffectType.DATAFLOW_SIDE_EFFECTING


def _scatter_maps(kinds):
    return ((lambda srcs, lands, a, idx: _slab(kinds[a], srcs[a], idx)),
            (lambda lands, a, idx: lands[a].at[idx]))


_SLOTTED = ("wkv", "cv")


def _own_slab_blocks(kind, shard_shape):
    if kind in ("win", "wup"):
        rows, slot = 256, (_up_slot if kind == "wup" else (lambda m: m))
        return (shard_shape[0] // rows, (rows, shard_shape[1]), (lambda i, me: (i, slot(me[0]))),
                (lambda i, me: (i, 0)), (lambda i, me: (me[0], i, 0)))
    if kind == "w4":
        return (1, shard_shape, (lambda i, me: (0, me[0], 0)), (lambda i, me: (0, 0, 0)),
                (lambda i, me: (me[0], 0, 0, 0)))
    if kind == "wdn":
        rows = 32
        return (R_DN // rows, (rows, D), (lambda i, me: (_dn_row(me[0]) // rows + i, 0)), (lambda i, me: (i, 0)),
                (lambda i, me: (me[0], i, 0)))
    assert kind in _SLOTTED
    rows = min(256, shard_shape[0])
    return (shard_shape[0] // rows, (rows, shard_shape[1]), (lambda i, me: (me[0], i, 0)),
            (lambda i, me: (i, 0)), (lambda i, me: (me[0], i, 0)))


def _place_own(kind, src, out_sds, gather, me_arr, *, name):
    shard_shape = src.shape if gather else out_sds.shape[1:]
    steps, blk, whole_idx, shard_idx, staging_idx = _own_slab_blocks(kind, shard_shape)
    slotted = kind in _SLOTTED
    whole_spec = pl.BlockSpec(((None,) if slotted else ()) + tuple(blk), whole_idx)
    if gather:
        in_spec, out_spec = pl.BlockSpec(tuple(blk), shard_idx), whole_spec
    else:
        in_spec, out_spec = whole_spec, pl.BlockSpec((None,) + tuple(blk), staging_idx)
    zero_init = gather and kind == "wdn"

    def body(me_ref, src_ref, *rest):
        rest[-1][...] = src_ref[...].astype(rest[-1].dtype)

    operands = (me_arr, src) + ((jnp.zeros(out_sds.shape, out_sds.dtype),) if zero_init else ())
    return _pcall(
        body,
        grid_spec=pltpu.PrefetchScalarGridSpec(
            num_scalar_prefetch=1, grid=(steps,), in_specs=[in_spec] + ([ANY] if zero_init else []),
            out_specs=out_spec),
        out_shape=out_sds, input_output_aliases={2: 0} if zero_init else {},
        compiler_params=_cp(("arbitrary",)), name=name)(*operands)


def _peer_copies(n, src_of, dst_of, src_r, land_r, ssem, rsem):
    me = _my_index()
    out = []
    for r, peer in _relations():
        p_idx = 4 * peer[0] + 2 * peer[1] + peer[2]
        for a in range(n):
            def copy(src_idx, dst_idx, a=a, r=r, peer=peer):
                sem = a * (N_DEV - 1) + r - 1
                return pltpu.make_async_remote_copy(
                    src_ref=src_of(src_r, land_r, a, src_idx), dst_ref=dst_of(land_r, a, dst_idx),
                    send_sem=ssem.at[sem], recv_sem=rsem.at[sem], device_id=peer, device_id_type=MESH)
            out.append((functools.partial(copy, p_idx, me), functools.partial(copy, me, p_idx)))
    return out


def _exchange_start(srcs, lands, maps, after, *, name):
    n, ns = len(lands), len(srcs)
    src_of, dst_of = maps

    def body(*refs):
        src_r, land_r = refs[:ns], refs[ns:ns + n]
        ssem, rsem, token = refs[ns + n + 1], refs[ns + n + 2], refs[-1]
        for send, _ in _peer_copies(n, src_of, dst_of, src_r, land_r, ssem, rsem):
            send().start()
        token[...] = jnp.zeros_like(token)

    flight = list(srcs) + list(lands)
    outs = pl.pallas_call(
        body, name=name,
        out_shape=(pltpu.SemaphoreType.DMA((n * (N_DEV - 1),)), pltpu.SemaphoreType.DMA((n * (N_DEV - 1),)),
                   *[pltpu.HBM(a.shape, a.dtype) for a in flight], SDS((8, LANE), F32)),
        in_specs=[HBM_SPEC] * (ns + n) + [ANY],
        out_specs=(SEM_SPEC, SEM_SPEC, *[HBM_SPEC] * (ns + n), pl.BlockSpec(memory_space=pltpu.VMEM)),
        input_output_aliases={i: 2 + i for i in range(ns + n)},
        compiler_params=pltpu.CompilerParams(has_side_effects=_DATAFLOW),
    )(*[pltpu.with_memory_space_constraint(a, pltpu.HBM) for a in flight], after)
    return (outs[0], outs[1], list(outs[2:2 + ns + n]), ns), outs[-1]


def _exchange_wait(handle, maps, after, *, name):
    ssem, rsem, flight, ns = handle
    n = len(flight) - ns
    src_of, dst_of = maps

    def body(*refs):
        src_r, land_r, ssem_r, rsem_r = refs[:ns], refs[ns:ns + n], refs[ns + n], refs[ns + n + 1]
        for send, arrival in _peer_copies(n, src_of, dst_of, src_r, land_r, ssem_r, rsem_r):
            send().wait_send()
            arrival().wait_recv()

    outs = pl.pallas_call(
        body, name=name, out_shape=[pltpu.HBM(a.shape, a.dtype) for a in flight],
        in_specs=[HBM_SPEC] * (ns + n) + [SEM_SPEC, SEM_SPEC, ANY], out_specs=[HBM_SPEC] * (ns + n),
        input_output_aliases={i: i for i in range(ns + n)},
        compiler_params=pltpu.CompilerParams(has_side_effects=_DATAFLOW),
    )(*flight, ssem, rsem, after)
    return list(outs[ns:])


_SIBLING = 1
_ICI = (2, 4, 6)


def _rel_peer(r):
    x, y, c = lax.axis_index("x"), lax.axis_index("y"), lax.axis_index("c")
    peer = (x ^ ((r >> 2) & 1), y ^ ((r >> 1) & 1), c ^ (r & 1))
    return peer, 4 * peer[0] + 2 * peer[1] + peer[2]


def _rcopy(ref, ssem, rsem, peer):
    return pltpu.make_async_remote_copy(src_ref=ref, dst_ref=ref, send_sem=ssem, recv_sem=rsem, device_id=peer,
                                        device_id_type=MESH)


def _gather2_start(lands, kinds, after, *, name):
    n = len(lands)

    def body(*refs):
        land_r, (send1, recv_sib, recv_ici), token = refs[:n], refs[n + 1:n + 4], refs[-1]
        me = _my_index()
        for a in range(n):
            own = _slab(kinds[a], land_r[a], me)
            for j, r in enumerate((_SIBLING,) + _ICI):
                rsem = recv_sib.at[a] if r == _SIBLING else recv_ici.at[3 * a + j - 1]
                _rcopy(own, send1.at[4 * a + j], rsem, _rel_peer(r)[0]).start()
        token[...] = jnp.zeros_like(token)

    sems = [pltpu.SemaphoreType.DMA((4 * n,)), pltpu.SemaphoreType.DMA((n,)), pltpu.SemaphoreType.DMA((3 * n,))]
    outs = pl.pallas_call(
        body, name=name, out_shape=(*sems, *[pltpu.HBM(a.shape, a.dtype) for a in lands], SDS((8, LANE), F32)),
        in_specs=[HBM_SPEC] * n + [ANY],
        out_specs=(SEM_SPEC,) * 3 + (HBM_SPEC,) * n + (pl.BlockSpec(memory_space=pltpu.VMEM),),
        input_output_aliases={i: 3 + i for i in range(n)},
        compiler_params=pltpu.CompilerParams(has_side_effects=_DATAFLOW),
    )(*[pltpu.with_memory_space_constraint(a, pltpu.HBM) for a in lands], after)
    return dict(send1=outs[0], recv_sib=outs[1], recv_ici=outs[2], lands=list(outs[3:3 + n])), outs[-1]


def _gather2_forward(handle, kinds, after, *, name):
    lands = handle["lands"]
    n = len(lands)

    def body(*refs):
        land_r, recv_ici, (fwd_send, fwd_recv), token = refs[:n], refs[n], refs[n + 2:n + 4], refs[-1]
        sibling = _rel_peer(_SIBLING)[0]
        for a in range(n):
            for j, r in enumerate(_ICI):
                got = _slab(kinds[a], land_r[a], _rel_peer(r)[1])
                _rcopy(got, fwd_send.at[3 * a + j], recv_ici.at[3 * a + j], sibling).wait_recv()
                _rcopy(got, fwd_send.at[3 * a + j], fwd_recv.at[3 * a + j], sibling).start()
        token[...] = jnp.zeros_like(token)

    sems = [pltpu.SemaphoreType.DMA((3 * n,)), pltpu.SemaphoreType.DMA((3 * n,))]
    outs = pl.pallas_call(
        body, name=name, out_shape=(*sems, *[pltpu.HBM(a.shape, a.dtype) for a in lands], SDS((8, LANE), F32)),
        in_specs=[HBM_SPEC] * n + [SEM_SPEC, ANY],
        out_specs=(SEM_SPEC,) * 2 + (HBM_SPEC,) * n + (pl.BlockSpec(memory_space=pltpu.VMEM),),
        input_output_aliases={i: 2 + i for i in range(n)},
        compiler_params=pltpu.CompilerParams(has_side_effects=_DATAFLOW),
    )(*lands, handle["recv_ici"], after)
    return dict(handle, fwd_send=outs[0], fwd_recv=outs[1], lands=list(outs[2:2 + n])), outs[-1]


def _gather2_wait(handle, kinds, after, *, name):
    lands = handle["lands"]
    n = len(lands)

    def body(*refs):
        land_r, (send1, recv_sib, fwd_send, fwd_recv) = refs[:n], refs[n:n + 4]
        me = _my_index()
        sibling, sib_idx = _rel_peer(_SIBLING)
        for a in range(n):
            own = _slab(kinds[a], land_r[a], me)
            for j, r in enumerate((_SIBLING,) + _ICI):
                _rcopy(own, send1.at[4 * a + j], recv_sib.at[a], _rel_peer(r)[0]).wait_send()
            theirs = _slab(kinds[a], land_r[a], sib_idx)
            _rcopy(theirs, send1.at[4 * a], recv_sib.at[a], sibling).wait_recv()
            for j, r in enumerate(_ICI):
                passed_on = _slab(kinds[a], land_r[a], _rel_peer(r)[1])
                _rcopy(passed_on, fwd_send.at[3 * a + j], fwd_recv.at[3 * a + j], sibling).wait_send()
                arrived = _slab(kinds[a], land_r[a], _rel_peer(r ^ _SIBLING)[1])
                _rcopy(arrived, fwd_send.at[3 * a + j], fwd_recv.at[3 * a + j], sibling).wait_recv()

    outs = pl.pallas_call(
        body, name=name, out_shape=[pltpu.HBM(a.shape, a.dtype) for a in lands],
        in_specs=[HBM_SPEC] * n + [SEM_SPEC] * 4 + [ANY], out_specs=[HBM_SPEC] * n,
        input_output_aliases={i: i for i in range(n)},
        compiler_params=pltpu.CompilerParams(has_side_effects=_DATAFLOW),
    )(*lands, handle["send1"], handle["recv_sib"], handle["fwd_send"], handle["fwd_recv"], after)
    return list(outs)


def _allreduce_small(pack, *, name):
    rows = pack.shape[0]

    def body(p_ref, out_ref, gath, ssem, rsem, lsem):
        _exchange(1, lambda a, idx: p_ref, lambda a, idx: gath.at[idx], (ssem, rsem, lsem))
        total = gath[0]
        for d in range(1, N_DEV):
            total = total + gath[d]
        out_ref[...] = total

    vm = pl.BlockSpec(memory_space=pltpu.VMEM)
    return _pcall(
        body, in_specs=[vm], out_specs=vm, out_shape=SDS(pack.shape, F32),
        scratch_shapes=[pltpu.VMEM((N_DEV, rows, pack.shape[1]), F32)] + _sem_scratch(1),
        compiler_params=_cp(has_side_effects=True), name=name)(pack)


def _adam(g, w, m, v):
    nm = ADAM_B1 * m + (1.0 - ADAM_B1) * g
    nv = ADAM_B2 * v + (1.0 - ADAM_B2) * (g * g)
    m_hat = nm / (1.0 - ADAM_B1 ** ADAM_STEP)
    v_hat = nv / (1.0 - ADAM_B2 ** ADAM_STEP)
    return -ADAM_LR * (m_hat / (jnp.sqrt(v_hat) + ADAM_EPS) + ADAM_WD * w), nm, nv


def _adamw_staged(st0, st1, w, m, v, *, name):
    _, rows, cols = w.shape
    tr = max(t for t in range(16, 129, 16) if rows % t == 0)
    nr = rows // tr

    def body(s0_ref, s1_ref, w_ref, m_ref, v_ref, g_ref, d_ref, nm_ref, nv_ref):
        for layer, s_ref in enumerate((s0_ref, s1_ref)):
            @pl.when(pl.program_id(0) == layer)
            def _(s_ref=s_ref):
                total = s_ref[0].astype(F32)
                for dev in range(1, N_DEV):
                    total = total + s_ref[dev].astype(F32)
                g_ref[0] = total

        d_ref[0], nm_ref[0], nv_ref[0] = _adam(g_ref[0], w_ref[0], m_ref[0], v_ref[0])

    st_spec = lambda layer: pl.BlockSpec(
        (N_DEV, tr, cols), lambda l, i: (0, jnp.where(l == layer, i, (nr - 1) * (1 - layer)), 0))
    par = pl.BlockSpec((1, tr, cols), lambda l, i: (l, i, 0))
    return _pcall(
        body, grid=(DEPTH, nr), in_specs=[st_spec(0), st_spec(1), par, par, par], out_specs=[par] * 4,
        out_shape=[SDS(w.shape, F32)] * 4,
        compiler_params=_cp(("arbitrary", "arbitrary")), name=name)(st0, st1, w, m, v)


def _adamw_small(g, w, m, v, *, name):
    def body(g_ref, w_ref, m_ref, v_ref, d_ref, nm_ref, nv_ref):
        d_ref[...], nm_ref[...], nv_ref[...] = _adam(g_ref[...], w_ref[...], m_ref[...], v_ref[...])

    return _pcall(body, out_shape=[SDS(g.shape, F32)] * 3, compiler_params=_cp(), name=name)(g, w, m, v)


def _pack_rows(arrays):
    flat = jnp.concatenate([a.reshape(-1).astype(F32) for a in arrays])
    rows = -(-flat.shape[0] // (8 * D)) * 8
    return jnp.pad(flat, (0, rows * D - flat.shape[0])).reshape(rows, D)


def _unpack_rows(pack, like):
    flat = pack.reshape(-1)
    out, at = [], 0
    for a in like:
        out.append(flat[at:at + a.size].reshape(a.shape))
        at += a.size
    return out


def _layer_fwd(x, h, mem, win, rest_of_weights, after_up, small, g_next, tag):
    proj = _mm(h, win, tm=1024, tn=1536, name=f"proj_{tag}")
    wup, wkv, w4, wdn, cw_a, cw_b, cw_f = rest_of_weights(proj)
    za = _bra_fwd(proj, cw_a, name=f"bra_fwd_{tag}")
    cb = _brb_conv_fwd(proj, cw_b, small["conv_b_bias"], name=f"brb_conv_fwd_{tag}")
    sb = _ln_silu_fwd(cb, small["ln_b_g"], small["ln_b_b"], name=f"ln_silu_fwd_{tag}")
    memn, kv = _kv_prep(mem, small["norm_mem_g"], wkv, name=f"kv_prep_{tag}")
    o = _attn_fwd(proj, kv, name=f"attn_fwd_{tag}")
    ya, yb, yc, mg, x1, h2 = _mix_out(x, za, sb, o, proj, w4, small["b_gate"], small["norm_ffn_g"],
                                      name=f"mix_out_{tag}")
    u2 = _mm(h2, wup, tm=1024, tn=1536, name=f"up_{tag}")
    token = after_up(u2)
    act, c2 = _ffn_act(u2, cw_f if token is None else _behind(cw_f, token), name=f"ffn_act_{tag}")
    x2, h_next = _mm_res_norm(act, wdn, x1, g_next, name=f"down_{tag}")
    saved = dict(x=x, h=h, proj=proj, za=za, cb=cb, sb=sb, memn=memn, kv=kv, o=o, ya=ya, yb=yb, yc=yc,
                 mg=mg, x1=x1, h2=h2, u2=u2, c2=c2, act=act)
    return x2, h_next, (win, wup, wkv, w4, wdn, cw_a, cw_b, cw_f), saved


def _behind(operand, token):
    return operand + token[0:1, 0:1]


def _layer_bwd(dx2, dx2b, mem, wts, small, sv, start, tag):
    win, wup, wkv, w4, wdn, cw_a, cw_b, cw_f = wts
    dact = _mm(dx2b, wdn, tb=True, tm=1024, tn=768, name=f"d_act_{tag}")
    dwdn = _mm(sv["act"], dx2b, ta=True, tm=768, tn=1024, name=f"dw_down_{tag}")
    du2, dcw_f = _ffn_bwd(sv["u2"], sv["c2"], dact, cw_f, name=f"ffn_bwd_{tag}")
    dwup = _mm(sv["h2"], du2, ta=True, tm=1024, tn=768, name=f"dw_up_{tag}")
    token = start(("wdn", "wup"), (dwdn, dwup), f"ffn_{tag}")
    dx1, dx1b, dg_ffn = _mm_nt_normbwd(du2, wup, sv["x1"], dx2, _behind(small["norm_ffn_g"], token),
                                       tk=3072, name=f"d_h2_{tag}")

    dya, dyb, dyc, dza, dsb, do, dproj, dbg = _mix_bwd(dx1b, sv["ya"], sv["yb"], sv["yc"], sv["proj"], w4,
                                                      small["b_gate"], name=f"mix_bwd_{tag}")
    dw4 = jnp.stack([
        _mm(a, b, ta=True, tm=1024, tn=512, name=f"dw_{nm}_{tag}")
        for nm, a, b in (("a_out", sv["za"], dya), ("b_out", sv["sb"], dyb), ("att_out", sv["o"], dyc),
                         ("o", sv["mg"], dx1b))])
    dq, dk, dv = _attn_bwd(sv["proj"], sv["kv"], do, name=f"attn_bwd_{tag}")
    dwkv, dg_mem = _kv_bwd(mem, small["norm_mem_g"], sv["memn"], dk, dv, wkv, name=f"kv_bwd_{tag}")
    token = start(("w4", "wkv"), (dw4, dwkv), f"mix_{tag}")
    dproj, dcw_a = _bra_bwd(sv["proj"], dza, _behind(cw_a, token), dproj, name=f"bra_bwd_{tag}")
    dcb, ln_sums = _ln_silu_bwd(sv["cb"], dsb, small["ln_b_g"], small["ln_b_b"], name=f"ln_silu_bwd_{tag}")
    dproj, dcw_b = _brb_conv_bwd(sv["proj"], dcb, dq, cw_b, dproj, name=f"brb_conv_bwd_{tag}")
    dwin = _mm(sv["h"], dproj, ta=True, tm=1024, tn=768, name=f"dw_in_{tag}")
    token = start(("win",), (dwin,), f"in_{tag}")
    dx, dxb, dg_mix = _mm_nt_normbwd(dproj, win, sv["x"], dx1, _behind(small["norm_mix_g"], token),
                                     tk=4608, name=f"d_h_{tag}")

    small_grads = [dg_mix[0:1], dg_mem[0:1], dbg[0:1].reshape(3, D), ln_sums[2:3], ln_sums[0:1], ln_sums[1:2],
                   dg_ffn[0:1], dcw_a[0:K_A], dcw_b[0:K_B], dcw_f[0:K_F].reshape(K_F * 2 * FF_P // D, D)]
    return dx, dxb, small_grads, token


_SMALL_ROWS = (1, 1, 3, 1, 1, 1, 1, K_A, K_B, K_F * 2 * FF_P // D)
_CV_ROWS = 48


def kernel(x, mem, norm_mix_g, norm_mem_g, w_in, b_gate, conv_a_w, w_a_out, conv_b_w, conv_b_bias, ln_b_g, ln_b_b, w_b_out, w_kv, w_att_out, w_o, norm_ffn_g, w_up, conv_ffn_w, w_down, norm_final_g, loss_target, m_norm_mix_g, m_norm_mem_g, m_w_in, m_b_gate, m_conv_a_w, m_w_a_out, m_conv_b_w, m_conv_b_bias, m_ln_b_g, m_ln_b_b, m_w_b_out, m_w_kv, m_w_att_out, m_w_o, m_norm_ffn_g, m_w_up, m_conv_ffn_w, m_w_down, m_norm_final_g, v_norm_mix_g, v_norm_mem_g, v_w_in, v_b_gate, v_conv_a_w, v_w_a_out, v_conv_b_w, v_conv_b_bias, v_ln_b_g, v_ln_b_b, v_w_b_out, v_w_kv, v_w_att_out, v_w_o, v_norm_ffn_g, v_w_up, v_conv_ffn_w, v_w_down, v_norm_final_g):
    me = _my_index()
    me_arr = me.astype(jnp.int32).reshape(1)
    x0, mem0, tgt = x.reshape(x.shape[1:]), mem.reshape(mem.shape[1:]), loss_target.reshape(x.shape[1:])
    up_pad = ((0, 0), (0, 0), (0, C_UP_P - C_UP))

    ag_groups = (("win",), ("wup", "wkv", "w4", "wdn", "cv"))
    kinds = ag_groups[0] + ag_groups[1]
    smalls, ag_handles = [], []
    token = jnp.zeros((8, LANE), F32)
    for l in range(DEPTH):
        cv = jnp.zeros((_CV_ROWS, C_UP_P), F32)
        cv = cv.at[0:K_F, 0:C_UP].set(conv_ffn_w[l]).at[3:3 + K_A, 0:R_O].set(conv_a_w[l])
        cv = cv.at[8:8 + K_B, 0:R_O].set(conv_b_w[l])
        shards = dict(
            win=w_in[l], wup=jnp.pad(w_up[l], up_pad[1:]), wkv=w_kv[l],
            w4=jnp.stack([w_a_out[l], w_b_out[l], w_att_out[l], w_o[l]]), wdn=w_down[l], cv=cv)
        whole = dict({k: SDS(*_WHOLE[k]) for k in kinds[:-1]}, cv=SDS((N_DEV,) + cv.shape, F32))
        lands = {k: _place_own(k, shards[k], whole[k], True, me_arr, name=f"ag_own_{k}_l{l}") for k in kinds}
        per_layer = []
        for g, grp in enumerate(ag_groups):
            handle, token = _gather2_start([lands[k] for k in grp], grp, token, name=f"ag_start_l{l}_g{g}")
            per_layer.append(handle)
        ag_handles.append(per_layer)
        smalls.append(dict(
            norm_mix_g=norm_mix_g[l][None], norm_mem_g=norm_mem_g[l][None], b_gate=b_gate[l][None],
            conv_b_bias=conv_b_bias[l][None], ln_b_g=ln_b_g[l][None], ln_b_b=ln_b_b[l][None],
            norm_ffn_g=norm_ffn_g[l][None]))

    def forward_group(l, g, after):
        ag_handles[l][g], tok = _gather2_forward(ag_handles[l][g], ag_groups[g], after, name=f"ag_forward_l{l}_g{g}")
        return tok

    def rest_of_weights(l):
        def wait(after):
            if l == 0:
                after = forward_group(0, 1, after)
            wup, wkv, w4, wdn, cvg = _gather2_wait(ag_handles[l][1], ag_groups[1], after, name=f"ag_wait_l{l}_g1")
            cw_f = jnp.stack([cvg[d, 0:K_F, :] for d in UP_ORDER], axis=1).reshape(K_F, 2 * FF_P)
            cw_a = cvg[:, 3:3 + K_A, 0:R_O].transpose(1, 0, 2).reshape(K_A, D)
            cw_b = cvg[:, 8:8 + K_B, 0:R_O].transpose(1, 0, 2).reshape(K_B, D)
            return (wup, wkv, w4, wdn, jnp.pad(cw_a, ((0, 8 - K_A), (0, 0))),
                    jnp.pad(cw_b, ((0, 32 - K_B), (0, 0))), jnp.pad(cw_f, ((0, 8 - K_F), (0, 0))))
        return wait

    wts, saved = [], []
    xs = x0
    h = _rms_fwd(xs, smalls[0]["norm_mix_g"], name="rms_fwd")
    behind = forward_group(0, 0, token)

    def next_layer_forwarding(l):
        def hook(after):
            if l + 1 == DEPTH:
                return None
            return forward_group(l + 1, 1, forward_group(l + 1, 0, after))
        return hook

    for l in range(DEPTH):
        g_next = smalls[l + 1]["norm_mix_g"] if l + 1 < DEPTH else norm_final_g[None]
        (win,) = _gather2_wait(ag_handles[l][0], ag_groups[0], behind, name=f"ag_wait_l{l}_g0")
        xs, h, w_l, sv = _layer_fwd(xs, h, mem0, win, rest_of_weights(l), next_layer_forwarding(l), smalls[l],
                                    g_next, f"l{l}")
        behind = h
        wts.append(w_l)
        saved.append(sv)
    dx, dxb, head_sums = _loss_head(xs, tgt, norm_final_g[None], name="loss_head")

    rs_handles = []
    small_grads = [None] * DEPTH

    def start_scatter(grp, arrays, name):
        maps = _scatter_maps(grp)
        lands = [_place_own(k, a, SDS((N_DEV,) + _SHARD[k], BF), False, me_arr, name=f"rs_own_{k}_{name}")
                 for k, a in zip(grp, arrays)]
        handle, tok = _exchange_start(list(arrays), lands, maps, rs_handles[-1][2] if rs_handles else head_sums,
                                      name=f"rs_start_{name}")
        rs_handles.append((grp, handle, tok, name))
        return tok

    for l in reversed(range(DEPTH)):
        dx, dxb, small_grads[l], token = _layer_bwd(dx, dxb, mem0, wts[l], smalls[l], saved[l], start_scatter,
                                                    f"l{l}")

    staged = [dict() for _ in range(DEPTH)]
    for grp, handle, _, name in rs_handles[:-1]:
        staged[int(name[-1])].update(zip(grp, _exchange_wait(handle, _scatter_maps(grp), dx, name=f"rs_wait_{name}")))

    pack = jnp.concatenate(small_grads[0] + small_grads[1] + [head_sums[1:2], head_sums[0:1]], axis=0)
    pack = jnp.pad(pack, ((0, -pack.shape[0] % 8), (0, 0)))
    total = _allreduce_small(pack, name="allreduce_small")
    per_layer = sum(_SMALL_ROWS)
    parts = []
    for l in range(DEPTH):
        at, one = l * per_layer, []
        for rows in _SMALL_ROWS:
            one.append(total[at:at + rows])
            at += rows
        parts.append(one)
    g_final = total[DEPTH * per_layer]
    loss = 0.5 / D * jnp.sum(total[DEPTH * per_layer + 1])

    def both(i):
        return jnp.stack([parts[0][i], parts[1][i]])

    g_norm_mix, g_norm_mem = both(0)[:, 0], both(1)[:, 0]
    g_b_gate = both(2).reshape(DEPTH, 3 * D)
    g_cbias, g_lng, g_lnb, g_norm_ffn = both(3)[:, 0], both(4)[:, 0], both(5)[:, 0], both(6)[:, 0]
    g_conv_a = lax.dynamic_slice_in_dim(both(7), me * R_O, R_O, axis=2)
    g_conv_b = lax.dynamic_slice_in_dim(both(8), me * R_O, R_O, axis=2)
    g_conv_f = lax.dynamic_slice_in_dim(both(9).reshape(DEPTH, K_F, 2 * FF_P), _up_slot(me) * C_UP_P, C_UP, axis=2)

    small_g = [g_norm_mix, g_norm_mem, g_b_gate, g_conv_a, g_conv_b, g_cbias, g_lng, g_lnb, g_norm_ffn, g_conv_f,
               g_final]
    small_w = [norm_mix_g, norm_mem_g, b_gate, conv_a_w, conv_b_w, conv_b_bias, ln_b_g, ln_b_b, norm_ffn_g,
               conv_ffn_w, norm_final_g]
    small_m = [m_norm_mix_g, m_norm_mem_g, m_b_gate, m_conv_a_w, m_conv_b_w, m_conv_b_bias, m_ln_b_g, m_ln_b_b,
               m_norm_ffn_g, m_conv_ffn_w, m_norm_final_g]
    small_v = [v_norm_mix_g, v_norm_mem_g, v_b_gate, v_conv_a_w, v_conv_b_w, v_conv_b_bias, v_ln_b_g, v_ln_b_b,
               v_norm_ffn_g, v_conv_ffn_w, v_norm_final_g]
    upd = _adamw_small(_pack_rows(small_g), _pack_rows(small_w), _pack_rows(small_m), _pack_rows(small_v),
                       name="adamw_small")
    s_d, s_m, s_v = (_unpack_rows(p, small_w) for p in upd)
    (d_norm_mix, d_norm_mem, d_b_gate, d_conv_a, d_conv_b, d_cbias, d_lng, d_lnb, d_norm_ffn, d_conv_f,
     d_final) = s_d
    (nm_norm_mix, nm_norm_mem, nm_b_gate, nm_conv_a, nm_conv_b, nm_cbias, nm_lng, nm_lnb, nm_norm_ffn, nm_conv_f,
     nm_final) = s_m
    (nv_norm_mix, nv_norm_mem, nv_b_gate, nv_conv_a, nv_conv_b, nv_cbias, nv_lng, nv_lnb, nv_norm_ffn, nv_conv_f,
     nv_final) = s_v

    def big_update(kind, w, m, v, name):
        return _adamw_staged(staged[0][kind], staged[1][kind], w, m, v, name=name)

    r_up = [a[:, :, 0:C_UP] for a in big_update("wup", jnp.pad(w_up, up_pad), jnp.pad(m_w_up, up_pad),
                                                jnp.pad(v_w_up, up_pad), "adamw_w_up")]
    r_kv = big_update("wkv", w_kv, m_w_kv, v_w_kv, "adamw_w_kv")
    r_dn = big_update("wdn", w_down, m_w_down, v_w_down, "adamw_w_down")

    def four(a, b, c, d_):
        return jnp.stack([a, b, c, d_], axis=1).reshape(DEPTH, 4 * R_O, D)

    r_4 = _adamw_staged(
        staged[0]["w4"].reshape(N_DEV, 4 * R_O, D), staged[1]["w4"].reshape(N_DEV, 4 * R_O, D),
        four(w_a_out, w_b_out, w_att_out, w_o), four(m_w_a_out, m_w_b_out, m_w_att_out, m_w_o),
        four(v_w_a_out, v_w_b_out, v_w_att_out, v_w_o), name="adamw_w_out")
    grp, handle, _, name = rs_handles[-1]
    staged[0].update(zip(grp, _exchange_wait(handle, _scatter_maps(grp), r_4[0], name=f"rs_wait_{name}")))
    r_in = big_update("win", w_in, m_w_in, v_w_in, "adamw_w_in")
    r_a, r_b, r_att, r_o = ([a.reshape(DEPTH, 4, R_O, D)[:, j] for a in r_4] for j in range(4))

    grads = [g_norm_mix, g_norm_mem, r_in[0], g_b_gate, g_conv_a, r_a[0], g_conv_b, g_cbias, g_lng, g_lnb, r_b[0],
             r_kv[0], r_att[0], r_o[0], g_norm_ffn, r_up[0], g_conv_f, r_dn[0], g_final]
    deltas = [d_norm_mix, d_norm_mem, r_in[1], d_b_gate, d_conv_a, r_a[1], d_conv_b, d_cbias, d_lng, d_lnb, r_b[1],
              r_kv[1], r_att[1], r_o[1], d_norm_ffn, r_up[1], d_conv_f, r_dn[1], d_final]
    new_m = [nm_norm_mix, nm_norm_mem, r_in[2], nm_b_gate, nm_conv_a, r_a[2], nm_conv_b, nm_cbias, nm_lng, nm_lnb,
             r_b[2], r_kv[2], r_att[2], r_o[2], nm_norm_ffn, r_up[2], nm_conv_f, r_dn[2], nm_final]
    new_v = [nv_norm_mix, nv_norm_mem, r_in[3], nv_b_gate, nv_conv_a, r_a[3], nv_conv_b, nv_cbias, nv_lng, nv_lnb,
             r_b[3], r_kv[3], r_att[3], r_o[3], nv_norm_ffn, r_up[3], nv_conv_f, r_dn[3], nv_final]
    return (loss, dx[None], *grads, *deltas, *new_m, *new_v)
```

```python
import functools

import jax
import jax.numpy as jnp
from jax import lax
from jax.experimental import pallas as pl
from jax.experimental.pallas import tpu as pltpu

F32 = jnp.float32
BF = jnp.bfloat16
SDS = jax.ShapeDtypeStruct
MESH = pl.DeviceIdType.MESH
ANY = pl.BlockSpec(memory_space=pl.ANY)

N_DEV = 8
DEPTH = 2
D = 1024
N_HEADS = 4
HEAD = D // N_HEADS
D_FF = 2816
K_A, K_B, K_F = 3, 31, 3
NORM_EPS = 1e-6

C_IN = 9 * D // N_DEV
C_KV = 2 * D // N_DEV
C_UP = 2 * D_FF // N_DEV
LANE = 128
C_UP_P = -(-C_UP // LANE) * LANE
FF_P = 4 * C_UP_P
R_O = D // N_DEV
R_DN = D_FF // N_DEV

VMEM_LIMIT = 56 * 1024 * 1024
TM = 512
TR = 256
SUB = 128
H_S, H_L = 16, 32

ADAM_LR, ADAM_B1, ADAM_B2, ADAM_EPS, ADAM_WD, ADAM_STEP = 0.001, 0.9, 0.999, 1e-08, 0.01, 10

UP_ORDER = (0, 4, 1, 5, 2, 6, 3, 7)


def _pcall(body, **kw):
    return pl.pallas_call(body, **kw)


def _cp(sem=None, **kw):
    return pltpu.CompilerParams(dimension_semantics=sem, vmem_limit_bytes=VMEM_LIMIT, **kw)


def _dot(a, b):
    return jnp.dot(a, b, preferred_element_type=F32)


def _dot_nt(a, b):
    return lax.dot_general(a, b, (((1,), (1,)), ((), ())), preferred_element_type=F32)


def _dot_tn(a, b):
    return lax.dot_general(a, b, (((0,), (0,)), ((), ())), preferred_element_type=F32)


def _sigmoid(z):
    return 1.0 / (1.0 + jnp.exp(-z))


def _rms(xv):
    return lax.rsqrt(jnp.mean(xv * xv, axis=-1, keepdims=True) + NORM_EPS)


def _up_slot(idx):
    return jnp.where(idx < 4, 2 * idx, 2 * (idx - 4) + 1)


def _dn_row(idx):
    return C_UP_P * (idx // 2) + R_DN * (idx % 2)


def _mm(a, b, *, ta=False, tb=False, out_dtype=BF, tm=TM, tn=512, tk=None, name):
    m, k_dim = (a.shape[1], a.shape[0]) if ta else a.shape
    n = b.shape[0] if tb else b.shape[1]
    tm, tn = min(tm, m), min(tn, n)
    tk = k_dim if tk is None else min(tk, k_dim)
    nk = k_dim // tk
    assert m % tm == 0 and n % tn == 0 and k_dim % tk == 0
    dims = (((0 if ta else 1,), (1 if tb else 0,)), ((), ()))

    def body(a_ref, b_ref, o_ref, *scratch):
        part = lax.dot_general(a_ref[...], b_ref[...], dims, preferred_element_type=F32)
        if nk == 1:
            o_ref[...] = part.astype(o_ref.dtype)
            return
        acc = scratch[0]
        k = pl.program_id(2)

        @pl.when(k == 0)
        def _():
            acc[...] = part

        @pl.when(k > 0)
        def _():
            acc[...] += part

        @pl.when(k == nk - 1)
        def _():
            o_ref[...] = acc[...].astype(o_ref.dtype)

    a_spec = pl.BlockSpec((tk, tm), lambda i, j, k: (k, i)) if ta else pl.BlockSpec((tm, tk), lambda i, j, k: (i, k))
    b_spec = pl.BlockSpec((tn, tk), lambda i, j, k: (j, k)) if tb else pl.BlockSpec((tk, tn), lambda i, j, k: (k, j))
    return _pcall(
        body, grid=(m // tm, n // tn, nk), in_specs=[a_spec, b_spec],
        out_specs=pl.BlockSpec((tm, tn), lambda i, j, k: (i, j)),
        out_shape=SDS((m, n), out_dtype),
        scratch_shapes=[pltpu.VMEM((tm, tn), F32)] if nk > 1 else [],
        compiler_params=_cp(("parallel", "parallel", "arbitrary")), name=name)(a, b)


def _mm_res_norm(a, w, x, g, *, name):
    s, k_dim = a.shape
    tm = min(TM, s)

    def body(a_ref, w_ref, x_ref, g_ref, xo_ref, h_ref):
        xo = x_ref[...] + _dot(a_ref[...], w_ref[...])
        xo_ref[...] = xo
        h_ref[...] = ((xo * _rms(xo)) * g_ref[...]).astype(BF)

    return _pcall(
        body, grid=(s // tm,),
        in_specs=[pl.BlockSpec((tm, k_dim), lambda i: (i, 0)),
                  pl.BlockSpec((k_dim, D), lambda i: (0, 0), pipeline_mode=pl.Buffered(1)),
                  pl.BlockSpec((tm, D), lambda i: (i, 0)), pl.BlockSpec((1, D), lambda i: (0, 0))],
        out_specs=[pl.BlockSpec((tm, D), lambda i: (i, 0))] * 2,
        out_shape=[SDS((s, D), F32), SDS((s, D), BF)],
        compiler_params=_cp(("parallel",)), name=name)(a, w, x, g)


def _mm_nt_normbwd(da, w, x, dres, g, *, tk=None, name):
    s, k_dim = da.shape
    tm = min(TM, s)
    tk = k_dim if tk is None else tk
    nk = k_dim // tk
    assert k_dim % tk == 0

    def body(da_ref, w_ref, x_ref, dres_ref, g_ref, dx_ref, dxb_ref, dg_ref, *scratch):
        i, k = pl.program_id(0), pl.program_id(1)
        part = _dot_nt(da_ref[...], w_ref[...])
        if nk > 1:
            acc = scratch[0]

            @pl.when(k == 0)
            def _():
                acc[...] = part

            @pl.when(k > 0)
            def _():
                acc[...] += part

        @pl.when((i == 0) & (k == 0))
        def _():
            dg_ref[...] = jnp.zeros_like(dg_ref)

        @pl.when(k == nk - 1)
        def _():
            dh = acc[...] if nk > 1 else part
            xv = x_ref[...]
            r = _rms(xv)
            xn = xv * r
            dg_ref[0:1, :] += jnp.sum(dh * xn, axis=0, keepdims=True)
            dxn = dh * g_ref[...]
            dx = dres_ref[...] + r * (dxn - xn * jnp.mean(dxn * xn, axis=-1, keepdims=True))
            dx_ref[...] = dx
            dxb_ref[...] = dx.astype(BF)

    row = lambda i, k: (i, 0)
    w_spec = (pl.BlockSpec((D, tk), lambda i, k: (0, k)) if nk > 1 else
              pl.BlockSpec((D, tk), lambda i, k: (0, 0), pipeline_mode=pl.Buffered(1)))
    return _pcall(
        body, grid=(s // tm, nk),
        in_specs=[pl.BlockSpec((tm, tk), lambda i, k: (i, k)), w_spec,
                  pl.BlockSpec((tm, D), row), pl.BlockSpec((tm, D), row), pl.BlockSpec((1, D), lambda i, k: (0, 0))],
        out_specs=[pl.BlockSpec((tm, D), row), pl.BlockSpec((tm, D), row), pl.BlockSpec((8, D), lambda i, k: (0, 0))],
        out_shape=[SDS((s, D), F32), SDS((s, D), BF), SDS((8, D), F32)],
        scratch_shapes=[pltpu.VMEM((tm, D), F32)] if nk > 1 else [],
        compiler_params=_cp(("arbitrary", "arbitrary")), name=name)(da, w, x, dres, g)


def _rms_fwd(x, g, *, name):
    s = x.shape[0]
    tm = min(TM, s)

    def body(x_ref, g_ref, h_ref):
        xv = x_ref[...]
        h_ref[...] = ((xv * _rms(xv)) * g_ref[...]).astype(BF)

    return _pcall(
        body, grid=(s // tm,),
        in_specs=[pl.BlockSpec((tm, D), lambda i: (i, 0)), pl.BlockSpec((1, D), lambda i: (0, 0))],
        out_specs=pl.BlockSpec((tm, D), lambda i: (i, 0)), out_shape=SDS((s, D), BF),
        compiler_params=_cp(("parallel",)), name=name)(x, g)


def _loss_head(x, tgt, g, *, name):
    s = x.shape[0]
    tm = min(TM, s)

    def body(x_ref, t_ref, g_ref, dx_ref, dxb_ref, sums_ref):
        @pl.when(pl.program_id(0) == 0)
        def _():
            sums_ref[...] = jnp.zeros_like(sums_ref)

        xv = x_ref[...]
        r = _rms(xv)
        xn = xv * r
        diff = xn * g_ref[...] - t_ref[...]
        sums_ref[0:1, :] += jnp.sum(diff * diff, axis=0, keepdims=True)
        dy = diff * (1.0 / D)
        sums_ref[1:2, :] += jnp.sum(dy * xn, axis=0, keepdims=True)
        dxn = dy * g_ref[...]
        dx = r * (dxn - xn * jnp.mean(dxn * xn, axis=-1, keepdims=True))
        dx_ref[...] = dx
        dxb_ref[...] = dx.astype(BF)

    row = lambda i: (i, 0)
    return _pcall(
        body, grid=(s // tm,),
        in_specs=[pl.BlockSpec((tm, D), row), pl.BlockSpec((tm, D), row), pl.BlockSpec((1, D), lambda i: (0, 0))],
        out_specs=[pl.BlockSpec((tm, D), row), pl.BlockSpec((tm, D), row), pl.BlockSpec((8, D), lambda i: (0, 0))],
        out_shape=[SDS((s, D), F32), SDS((s, D), BF), SDS((8, D), F32)],
        compiler_params=_cp(("arbitrary",)), name=name)(x, tgt, g)


def _halo_before(i, tr, h):
    return jnp.maximum(i * (tr // h) - 1, 0)


def _halo_after(i, tr, h, s):
    return jnp.minimum((i + 1) * (tr // h), s // h - 1)


def _taps(buf, w_ref, sl, k_w, base, rows):
    acc = None
    for k in range(k_w):
        t = w_ref[k:k + 1, sl] * buf[base + k:base + k + rows, sl]
        acc = t if acc is None else acc + t
    return acc


def _taps_rev(buf, w_ref, sl, k_w, base, rows):
    acc = None
    for k in range(k_w):
        t = w_ref[k:k + 1, sl] * buf[base + k_w - 1 - k:base + k_w - 1 - k + rows, sl]
        acc = t if acc is None else acc + t
    return acc


def _tap_grads(dw_ref, dc, buf, sl, k_w, base, rows):
    for k in range(k_w):
        dw_ref[k:k + 1, sl] += jnp.sum(dc * buf[base + k:base + k + rows, sl], axis=0, keepdims=True)


def _bra_fwd(proj, cw, *, name):
    s = proj.shape[0]
    tr, h = min(TR, s), H_S
    sub = min(SUB, tr)

    def body(cur, halo, w_ref, za_ref, cvb):
        i = pl.program_id(0)
        hv = halo[:, D:2 * D].astype(F32) * halo[:, 2 * D:3 * D].astype(F32)
        cvb[0:h, :] = jnp.where(i == 0, 0.0, hv)
        cvb[h:h + tr, :] = cur[:, D:2 * D].astype(F32) * cur[:, 2 * D:3 * D].astype(F32)
        for c in range(D // LANE):
            sl = slice(LANE * c, LANE * c + LANE)
            ca = _taps(cvb, w_ref, sl, K_A, h - (K_A - 1), tr)
            za_ref[:, sl] = (cur[:, sl].astype(F32) * ca).astype(BF)

    return _pcall(
        body, grid=(s // tr,),
        in_specs=[pl.BlockSpec((tr, 3 * D), lambda i: (i, 0)),
                  pl.BlockSpec((h, 3 * D), lambda i: (_halo_before(i, tr, h), 0)),
                  pl.BlockSpec((8, D), lambda i: (0, 0))],
        out_specs=pl.BlockSpec((tr, D), lambda i: (i, 0)), out_shape=SDS((s, D), BF),
        scratch_shapes=[pltpu.VMEM((h + tr, D), F32)],
        compiler_params=_cp(("parallel",)), name=name)(proj, proj, cw)


def _bra_bwd(proj, dza, cw, dproj, *, name):
    s = proj.shape[0]
    tr, h = min(TR, s), H_S
    sub = min(SUB, tr)
    n = s // tr

    def body(before, cur, after, dz_cur, dz_after, w_ref, dproj_in, da_ref, dw_ref, cvb, dcab):
        del dproj_in
        i = pl.program_id(0)

        @pl.when(i == 0)
        def _():
            dw_ref[...] = jnp.zeros_like(dw_ref)

        first, last = i == 0, i == n - 1
        cvb[0:h, :] = jnp.where(first, 0.0, before[:, D:2 * D].astype(F32) * before[:, 2 * D:3 * D].astype(F32))
        cvb[h:h + tr, :] = cur[:, D:2 * D].astype(F32) * cur[:, 2 * D:3 * D].astype(F32)
        dcab[0:tr, :] = dz_cur[...].astype(F32) * cur[:, 0:D].astype(F32)
        dcab[tr:tr + h, :] = jnp.where(last, 0.0, dz_after[...].astype(F32) * after[:, 0:D].astype(F32))
        for c in range(D // LANE):
            sl = slice(LANE * c, LANE * c + LANE)
            gl, vl = slice(D + LANE * c, D + LANE * c + LANE), slice(2 * D + LANE * c, 2 * D + LANE * c + LANE)
            for r0 in range(0, tr, sub):
                rows = slice(r0, r0 + sub)
                ca = _taps(cvb, w_ref, sl, K_A, h - (K_A - 1) + r0, sub)
                da_ref[rows, sl] = (dz_cur[rows, sl].astype(F32) * ca).astype(BF)
                dcv = _taps_rev(dcab, w_ref, sl, K_A, r0, sub)
                da_ref[rows, gl] = (dcv * cur[rows, vl].astype(F32)).astype(BF)
                da_ref[rows, vl] = (dcv * cur[rows, gl].astype(F32)).astype(BF)
                _tap_grads(dw_ref, dcab[rows, sl], cvb, sl, K_A, h - (K_A - 1) + r0, sub)

    return _pcall(
        body, grid=(n,),
        in_specs=[pl.BlockSpec((h, 3 * D), lambda i: (_halo_before(i, tr, h), 0)),
                  pl.BlockSpec((tr, 3 * D), lambda i: (i, 0)),
                  pl.BlockSpec((h, 3 * D), lambda i: (_halo_after(i, tr, h, s), 0)),
                  pl.BlockSpec((tr, D), lambda i: (i, 0)),
                  pl.BlockSpec((h, D), lambda i: (_halo_after(i, tr, h, s), 0)),
                  pl.BlockSpec((8, D), lambda i: (0, 0)), ANY],
        out_specs=[pl.BlockSpec((tr, 3 * D), lambda i: (i, 0)), pl.BlockSpec((8, D), lambda i: (0, 0))],
        out_shape=[SDS(dproj.shape, BF), SDS((8, D), F32)], input_output_aliases={6: 0},
        scratch_shapes=[pltpu.VMEM((h + tr, D), F32), pltpu.VMEM((tr + h, D), F32)],
        compiler_params=_cp(("arbitrary",)), name=name)(proj, proj, proj, dza, dza, cw, dproj)


_U_COL, _UG_COL = 3, 4


def _brb_conv_fwd(proj, cw, bias, *, name):
    s = proj.shape[0]
    tr, h = min(TR, s), H_L
    sub = min(SUB, tr)

    def body(u_cur, ug_cur, u_halo, ug_halo, w_ref, b_ref, cb_ref, glb, shifted):
        i = pl.program_id(0)
        glb[0:h, :] = jnp.where(i == 0, 0.0, u_halo[...].astype(F32) * _sigmoid(ug_halo[...].astype(F32)))
        glb[h:h + tr, :] = u_cur[...].astype(F32) * _sigmoid(ug_cur[...].astype(F32))
        for c in range(D // LANE):
            sl = slice(LANE * c, LANE * c + LANE)
            for r in range(1, 8):
                shifted[r] = glb[8 - r:8 - r + tr + 24, sl]
            for r0 in range(0, tr, sub):
                acc = None
                for k in range(K_B):
                    q, r = divmod(K_B - 1 - k, 8)
                    at = r0 - 8 * q
                    win = shifted[r, 24 + at:24 + at + sub, :] if r else glb[h + at:h + at + sub, sl]
                    term = w_ref[k:k + 1, sl] * win
                    acc = term if acc is None else acc + term
                cb_ref[r0:r0 + sub, sl] = (acc + b_ref[:, sl]).astype(BF)

    return _pcall(
        body, grid=(s // tr,),
        in_specs=[pl.BlockSpec((tr, D), lambda i: (i, _U_COL)), pl.BlockSpec((tr, D), lambda i: (i, _UG_COL)),
                  pl.BlockSpec((h, D), lambda i: (_halo_before(i, tr, h), _U_COL)),
                  pl.BlockSpec((h, D), lambda i: (_halo_before(i, tr, h), _UG_COL)),
                  pl.BlockSpec((32, D), lambda i: (0, 0)), pl.BlockSpec((1, D), lambda i: (0, 0))],
        out_specs=pl.BlockSpec((tr, D), lambda i: (i, 0)), out_shape=SDS((s, D), BF),
        scratch_shapes=[pltpu.VMEM((h + tr, D), F32), pltpu.VMEM((8, tr + 24, LANE), F32)],
        compiler_params=_cp(("parallel",)), name=name)(proj, proj, proj, proj, cw, bias)


def _brb_conv_bwd(proj, dcb, dq, cw, dproj, *, name):
    s = proj.shape[0]
    tr, h = min(TR, s), H_L
    sub = min(SUB, tr)
    n = s // tr

    def body(u_cur, ug_cur, d_cur, d_after, dq_ref, w_ref, dproj_in, db_ref, dw_ref, dcbb, shifted):
        del dproj_in
        i = pl.program_id(0)

        @pl.when(i == 0)
        def _():
            dw_ref[...] = jnp.zeros_like(dw_ref)

        db_ref[:, 2 * D:3 * D] = dq_ref[...]
        dcbb[0:tr, :] = d_cur[...].astype(F32)
        dcbb[tr:tr + h, :] = jnp.where(i == n - 1, 0.0, d_after[...].astype(F32))
        for c in range(D // LANE):
            sl = slice(LANE * c, LANE * c + LANE)
            for r in range(1, 8):
                shifted[r] = dcbb[r:r + tr + 24, sl]
            for r0 in range(0, tr, sub):
                u = u_cur[r0:r0 + sub, sl].astype(F32)
                sg = _sigmoid(ug_cur[r0:r0 + sub, sl].astype(F32))
                glu = u * sg
                dglu = None
                for k in range(K_B):
                    q, r = divmod(K_B - 1 - k, 8)
                    at = r0 + 8 * q
                    win = shifted[r, at:at + sub, :] if r else dcbb[at:at + sub, sl]
                    term = w_ref[k:k + 1, sl] * win
                    dglu = term if dglu is None else dglu + term
                    dw_ref[k:k + 1, sl] += jnp.sum(win * glu, axis=0, keepdims=True)
                db_ref[r0:r0 + sub, sl] = (dglu * sg).astype(BF)
                db_ref[r0:r0 + sub, D + LANE * c:D + LANE * c + LANE] = (dglu * u * sg * (1.0 - sg)).astype(BF)

    return _pcall(
        body, grid=(n,),
        in_specs=[pl.BlockSpec((tr, D), lambda i: (i, _U_COL)), pl.BlockSpec((tr, D), lambda i: (i, _UG_COL)),
                  pl.BlockSpec((tr, D), lambda i: (i, 0)),
                  pl.BlockSpec((h, D), lambda i: (_halo_after(i, tr, h, s), 0)),
                  pl.BlockSpec((tr, D), lambda i: (i, 0)),
                  pl.BlockSpec((32, D), lambda i: (0, 0)), ANY],
        out_specs=[pl.BlockSpec((tr, 3 * D), lambda i: (i, 1)), pl.BlockSpec((32, D), lambda i: (0, 0))],
        out_shape=[SDS(dproj.shape, BF), SDS((32, D), F32)], input_output_aliases={6: 0},
        scratch_shapes=[pltpu.VMEM((tr + h, D), F32), pltpu.VMEM((8, tr + 24, LANE), F32)],
        compiler_params=_cp(("arbitrary",)), name=name)(proj, proj, dcb, dcb, dq, cw, dproj)


def _ln_silu_fwd(cb, g, b, *, name):
    s = cb.shape[0]
    tm = min(TM, s)

    def body(cb_ref, g_ref, b_ref, sb_ref):
        z = cb_ref[...].astype(F32)
        zc = z - jnp.mean(z, axis=-1, keepdims=True)
        ln = (zc * lax.rsqrt(jnp.mean(zc * zc, axis=-1, keepdims=True) + NORM_EPS)) * g_ref[...] + b_ref[...]
        sb_ref[...] = (ln * _sigmoid(ln)).astype(BF)

    row = lambda i: (i, 0)
    vec = pl.BlockSpec((1, D), lambda i: (0, 0))
    return _pcall(
        body, grid=(s // tm,), in_specs=[pl.BlockSpec((tm, D), row), vec, vec],
        out_specs=pl.BlockSpec((tm, D), row), out_shape=SDS((s, D), BF),
        compiler_params=_cp(("parallel",)), name=name)(cb, g, b)


def _ln_silu_bwd(cb, dsb, g, b, *, name):
    s = cb.shape[0]
    tm = min(TM, s)

    def body(cb_ref, dsb_ref, g_ref, b_ref, dcb_ref, sums_ref):
        @pl.when(pl.program_id(0) == 0)
        def _():
            sums_ref[...] = jnp.zeros_like(sums_ref)

        z = cb_ref[...].astype(F32)
        zc = z - jnp.mean(z, axis=-1, keepdims=True)
        rstd = lax.rsqrt(jnp.mean(zc * zc, axis=-1, keepdims=True) + NORM_EPS)
        lnh = zc * rstd
        ln = lnh * g_ref[...] + b_ref[...]
        sg = _sigmoid(ln)
        dln = dsb_ref[...].astype(F32) * (sg * (1.0 + ln * (1.0 - sg)))
        sums_ref[0:1, :] += jnp.sum(dln * lnh, axis=0, keepdims=True)
        sums_ref[1:2, :] += jnp.sum(dln, axis=0, keepdims=True)
        dlnh = dln * g_ref[...]
        dz = rstd * (dlnh - jnp.mean(dlnh, axis=-1, keepdims=True)
                     - lnh * jnp.mean(dlnh * lnh, axis=-1, keepdims=True))
        sums_ref[2:3, :] += jnp.sum(dz, axis=0, keepdims=True)
        dcb_ref[...] = dz.astype(BF)

    row = lambda i: (i, 0)
    vec = pl.BlockSpec((1, D), lambda i: (0, 0))
    return _pcall(
        body, grid=(s // tm,), in_specs=[pl.BlockSpec((tm, D), row), pl.BlockSpec((tm, D), row), vec, vec],
        out_specs=[pl.BlockSpec((tm, D), row), pl.BlockSpec((8, D), lambda i: (0, 0))],
        out_shape=[SDS((s, D), BF), SDS((8, D), F32)],
        compiler_params=_cp(("arbitrary",)), name=name)(cb, dsb, g, b)


_Q_COL = 5 * D // HEAD


def _kv_prep(mem, g, wkv, *, name):
    m = mem.shape[0]

    def body(mem_ref, g_ref, w_ref, memn_ref, kv_ref):
        mv = mem_ref[...]
        memn = ((mv * _rms(mv)) * g_ref[...]).astype(BF)
        memn_ref[...] = memn
        for dev in range(N_DEV):
            kv_ref[:, dev * C_KV:(dev + 1) * C_KV] = _dot(memn, w_ref[dev]).astype(BF)

    return _pcall(body, out_shape=[SDS((m, D), BF), SDS((m, 2 * D), BF)],
                  compiler_params=_cp(), name=name)(mem, g, wkv)


def _softmax_rows(q, k):
    sc = _dot_nt(q, k) * (1.0 / (HEAD ** 0.5))
    e = jnp.exp(sc - jnp.max(sc, axis=-1, keepdims=True))
    return e / jnp.sum(e, axis=-1, keepdims=True)


def _attn_fwd(proj, kv, *, name):
    s, m = proj.shape[0], kv.shape[0]
    tm = min(TM, s)

    def body(q_ref, k_ref, v_ref, o_ref):
        p = _softmax_rows(q_ref[...], k_ref[...])
        o_ref[...] = _dot(p.astype(BF), v_ref[...]).astype(BF)

    return _pcall(
        body, grid=(s // tm, N_HEADS),
        in_specs=[pl.BlockSpec((tm, HEAD), lambda i, hd: (i, _Q_COL + hd)),
                  pl.BlockSpec((m, HEAD), lambda i, hd: (0, hd)),
                  pl.BlockSpec((m, HEAD), lambda i, hd: (0, N_HEADS + hd))],
        out_specs=pl.BlockSpec((tm, HEAD), lambda i, hd: (i, hd)), out_shape=SDS((s, D), BF),
        compiler_params=_cp(("parallel", "parallel")), name=name)(proj, kv, kv)


def _attn_bwd(proj, kv, do, *, name):
    s, m = proj.shape[0], kv.shape[0]
    tm = min(TM, s)

    def body(q_ref, k_ref, v_ref, do_ref, dq_ref, dk_ref, dv_ref):
        @pl.when(pl.program_id(1) == 0)
        def _():
            dk_ref[...] = jnp.zeros_like(dk_ref)
            dv_ref[...] = jnp.zeros_like(dv_ref)

        q, k, dov = q_ref[...], k_ref[...], do_ref[...]
        p = _softmax_rows(q, k)
        dp = _dot_nt(dov, v_ref[...])
        dv_ref[...] += _dot_tn(p.astype(BF), dov)
        ds = (p * (dp - jnp.sum(dp * p, axis=-1, keepdims=True)) * (1.0 / (HEAD ** 0.5))).astype(BF)
        dq_ref[...] = _dot(ds, k).astype(BF)
        dk_ref[...] += _dot_tn(ds, q)

    return _pcall(
        body, grid=(N_HEADS, s // tm),
        in_specs=[pl.BlockSpec((tm, HEAD), lambda hd, i: (i, _Q_COL + hd)),
                  pl.BlockSpec((m, HEAD), lambda hd, i: (0, hd)),
                  pl.BlockSpec((m, HEAD), lambda hd, i: (0, N_HEADS + hd)),
                  pl.BlockSpec((tm, HEAD), lambda hd, i: (i, hd))],
        out_specs=[pl.BlockSpec((tm, HEAD), lambda hd, i: (i, hd)),
                   pl.BlockSpec((m, HEAD), lambda hd, i: (0, hd)),
                   pl.BlockSpec((m, HEAD), lambda hd, i: (0, hd))],
        out_shape=[SDS((s, D), BF), SDS((m, D), F32), SDS((m, D), F32)],
        compiler_params=_cp(("parallel", "arbitrary")), name=name)(proj, kv, kv, do)


def _kv_bwd(mem, g, memn, dk, dv, wkv, *, name):
    def body(mem_ref, g_ref, memn_ref, dk_ref, dv_ref, w_ref, dw_ref, dg_ref):
        memn = memn_ref[...]
        dmemn = None
        for dev in range(N_DEV):
            d_ref, col = (dk_ref, dev) if dev < N_HEADS else (dv_ref, dev - N_HEADS)
            dslab = d_ref[:, col * C_KV:(col + 1) * C_KV].astype(BF)
            dw_ref[dev] = _dot_tn(memn, dslab).astype(BF)
            part = _dot_nt(dslab, w_ref[dev])
            dmemn = part if dmemn is None else dmemn + part
        mv = mem_ref[...]
        dg_ref[...] = jnp.zeros_like(dg_ref)
        dg_ref[0:1, :] = jnp.sum(dmemn * (mv * _rms(mv)), axis=0, keepdims=True)

    assert C_KV == HEAD
    return _pcall(body, out_shape=[SDS((N_DEV, D, C_KV), BF), SDS((8, D), F32)],
                  compiler_params=_cp(), name=name)(mem, g, memn, dk, dv, wkv)


_TM_MIX = 512


def _mix_out(x, za, sb, o, proj, w4, bg, g_next, *, name):
    s = x.shape[0]
    tm = min(_TM_MIX, s)

    def body(x_ref, za_ref, sb_ref, o_ref, pg_ref, w4_ref, bg_ref, gn_ref,
             ya_ref, yb_ref, yc_ref, mg_ref, x1_ref, h_ref):
        ys = (_dot(za_ref[...], w4_ref[0]), _dot(sb_ref[...], w4_ref[1]), _dot(o_ref[...], w4_ref[2]))
        merged = None
        for j, (y, y_ref) in enumerate(zip(ys, (ya_ref, yb_ref, yc_ref))):
            y_ref[...] = y.astype(BF)
            gate = _sigmoid(pg_ref[:, j * D:(j + 1) * D].astype(F32) + bg_ref[:, j * D:(j + 1) * D])
            merged = gate * y if merged is None else merged + gate * y
        mg = merged.astype(BF)
        mg_ref[...] = mg
        x1 = x_ref[...] + _dot(mg, w4_ref[3])
        x1_ref[...] = x1
        h_ref[...] = ((x1 * _rms(x1)) * gn_ref[...]).astype(BF)

    row = lambda i: (i, 0)
    act = pl.BlockSpec((tm, D), row)
    return _pcall(
        body, grid=(s // tm,),
        in_specs=[act, act, act, act, pl.BlockSpec((tm, 3 * D), lambda i: (i, 2)),
                  pl.BlockSpec((4, D, D), lambda i: (0, 0, 0), pipeline_mode=pl.Buffered(1)), pl.BlockSpec((1, 3 * D), lambda i: (0, 0)),
                  pl.BlockSpec((1, D), lambda i: (0, 0))],
        out_specs=[act] * 6,
        out_shape=[SDS((s, D), BF)] * 4 + [SDS((s, D), F32), SDS((s, D), BF)],
        compiler_params=_cp(("parallel",)), name=name)(x, za, sb, o, proj, w4, bg, g_next)


def _mix_bwd(dxb, ya, yb, yc, proj, w4, bg, *, name):
    s = dxb.shape[0]
    tm = min(_TM_MIX, s)

    def body(dx_ref, ya_ref, yb_ref, yc_ref, pg_ref, w4_ref, bg_ref,
             dya_ref, dyb_ref, dyc_ref, dza_ref, dsb_ref, do_ref, dgt_ref, dbg_ref):
        @pl.when(pl.program_id(0) == 0)
        def _():
            dbg_ref[...] = jnp.zeros_like(dbg_ref)

        dm = _dot_nt(dx_ref[...], w4_ref[3])
        for j, (y_ref, dy_ref, din_ref) in enumerate(zip((ya_ref, yb_ref, yc_ref), (dya_ref, dyb_ref, dyc_ref),
                                                         (dza_ref, dsb_ref, do_ref))):
            cols = slice(j * D, (j + 1) * D)
            gate = _sigmoid(pg_ref[:, cols].astype(F32) + bg_ref[:, cols])
            dy = (dm * gate).astype(BF)
            dy_ref[...] = dy
            din_ref[...] = _dot_nt(dy, w4_ref[j]).astype(BF)
            dpre = dm * y_ref[...].astype(F32) * gate * (1.0 - gate)
            dgt_ref[:, cols] = dpre.astype(BF)
            dbg_ref[0:1, cols] += jnp.sum(dpre, axis=0, keepdims=True)

    row = lambda i: (i, 0)
    act = pl.BlockSpec((tm, D), row)
    return _pcall(
        body, grid=(s // tm,),
        in_specs=[act, act, act, act, pl.BlockSpec((tm, 3 * D), lambda i: (i, 2)),
                  pl.BlockSpec((4, D, D), lambda i: (0, 0, 0), pipeline_mode=pl.Buffered(1)), pl.BlockSpec((1, 3 * D), lambda i: (0, 0))],
        out_specs=[act] * 6 + [pl.BlockSpec((tm, 3 * D), lambda i: (i, 2)),
                               pl.BlockSpec((8, 3 * D), lambda i: (0, 0))],
        out_shape=[SDS((s, D), BF)] * 6 + [SDS((s, 9 * D), BF), SDS((8, 3 * D), F32)],
        compiler_params=_cp(("arbitrary",)), name=name)(dxb, ya, yb, yc, proj, w4, bg)


_PAIR = 2 * C_UP_P


def _ffn_act(u2, cw, *, name):
    s = u2.shape[0]
    tr, h = min(TR, s), H_S
    sub = min(SUB, tr)

    def body(cur, halo, w_ref, act_ref, c2_ref, ub):
        i = pl.program_id(1)
        ub[0:h, :] = jnp.where(i == 0, 0.0, halo[...].astype(F32))
        ub[h:h + tr, :] = cur[...].astype(F32)
        for c in range(C_UP_P // LANE):
            gl = slice(LANE * c, LANE * c + LANE)
            ul = slice(C_UP_P + LANE * c, C_UP_P + LANE * c + LANE)
            for r0 in range(0, tr, sub):
                gt = _taps(ub, w_ref, gl, K_F, h - (K_F - 1) + r0, sub)
                up = _taps(ub, w_ref, ul, K_F, h - (K_F - 1) + r0, sub)
                c2_ref[r0:r0 + sub, gl] = gt.astype(BF)
                c2_ref[r0:r0 + sub, ul] = up.astype(BF)
                act_ref[r0:r0 + sub, gl] = (gt * _sigmoid(gt) * up).astype(BF)

    return _pcall(
        body, grid=(4, s // tr),
        in_specs=[pl.BlockSpec((tr, _PAIR), lambda p, i: (i, p)),
                  pl.BlockSpec((h, _PAIR), lambda p, i: (_halo_before(i, tr, h), p)),
                  pl.BlockSpec((8, _PAIR), lambda p, i: (0, p))],
        out_specs=[pl.BlockSpec((tr, C_UP_P), lambda p, i: (i, p)), pl.BlockSpec((tr, _PAIR), lambda p, i: (i, p))],
        out_shape=[SDS((s, FF_P), BF), SDS((s, 2 * FF_P), BF)],
        scratch_shapes=[pltpu.VMEM((h + tr, _PAIR), F32)],
        compiler_params=_cp(("parallel", "parallel")), name=name)(u2, u2, cw)


def _ffn_bwd(u2, c2, dact, cw, *, name):
    s = u2.shape[0]
    tr, h = min(TR, s), H_S
    sub = min(SUB, tr)
    n = s // tr
    ext = tr + h

    def body(u_cur, c_cur, c_after, da_cur, da_after, w_ref, du_ref, dw_ref, dcb):
        i = pl.program_id(1)
        last = i == n - 1

        @pl.when(i == 0)
        def _():
            dw_ref[...] = jnp.zeros_like(dw_ref)

        def conv_grad(gt, up, da):
            gt, up, da = gt.astype(F32), up.astype(F32), da.astype(F32)
            sg = _sigmoid(gt)
            return da * up * (sg * (1.0 + gt * (1.0 - sg))), da * (gt * sg)

        for c in range(C_UP_P // LANE):
            gl = slice(LANE * c, LANE * c + LANE)
            ul = slice(C_UP_P + LANE * c, C_UP_P + LANE * c + LANE)
            for r0 in range(0, tr, sub):
                rows = slice(r0, r0 + sub)
                dcb[rows, gl], dcb[rows, ul] = conv_grad(c_cur[rows, gl], c_cur[rows, ul], da_cur[rows, gl])
            dg, du_ = conv_grad(c_after[:, gl], c_after[:, ul], da_after[:, gl])
            dcb[tr:ext, gl], dcb[tr:ext, ul] = jnp.where(last, 0.0, dg), jnp.where(last, 0.0, du_)
        for c in range(_PAIR // LANE):
            sl = slice(LANE * c, LANE * c + LANE)
            for r0 in range(0, tr, sub):
                u = u_cur[r0:r0 + sub, sl].astype(F32)
                du = None
                for k in range(K_F):
                    at = r0 + K_F - 1 - k
                    win = dcb[at:at + sub, sl]
                    term = w_ref[k:k + 1, sl] * win
                    du = term if du is None else du + term
                    dw_ref[k:k + 1, sl] += jnp.sum(win * u, axis=0, keepdims=True)
                du_ref[r0:r0 + sub, sl] = du.astype(BF)

    return _pcall(
        body, grid=(4, n),
        in_specs=[pl.BlockSpec((tr, _PAIR), lambda p, i: (i, p)),
                  pl.BlockSpec((tr, _PAIR), lambda p, i: (i, p)),
                  pl.BlockSpec((h, _PAIR), lambda p, i: (_halo_after(i, tr, h, s), p)),
                  pl.BlockSpec((tr, C_UP_P), lambda p, i: (i, p)),
                  pl.BlockSpec((h, C_UP_P), lambda p, i: (_halo_after(i, tr, h, s), p)),
                  pl.BlockSpec((8, _PAIR), lambda p, i: (0, p))],
        out_specs=[pl.BlockSpec((tr, _PAIR), lambda p, i: (i, p)), pl.BlockSpec((8, _PAIR), lambda p, i: (0, p))],
        out_shape=[SDS((s, 2 * FF_P), BF), SDS((8, 2 * FF_P), F32)],
        scratch_shapes=[pltpu.VMEM((ext, _PAIR), F32)],
        compiler_params=_cp(("parallel", "arbitrary")), name=name)(u2, c2, c2, dact, dact, cw)


def _relations():
    x, y, c = lax.axis_index("x"), lax.axis_index("y"), lax.axis_index("c")
    out = []
    for r in range(1, N_DEV):
        rx, ry, rc = (r >> 2) & 1, (r >> 1) & 1, r & 1
        out.append((r, (x ^ rx, y ^ ry, c ^ rc)))
    return out


def _my_index():
    return 4 * lax.axis_index("x") + 2 * lax.axis_index("y") + lax.axis_index("c")


def _exchange(n_arrays, src_of, dst_of, refs):
    ssem, rsem, lsem = refs
    me = _my_index()
    local = []
    for a in range(n_arrays):
        loc = pltpu.make_async_copy(src_of(a, me), dst_of(a, me), lsem.at[a])
        loc.start()
        local.append(loc)

    def copy(a, r, peer, src_idx, dst_idx):
        return pltpu.make_async_remote_copy(
            src_ref=src_of(a, src_idx), dst_ref=dst_of(a, dst_idx), send_sem=ssem.at[a, r - 1],
            recv_sem=rsem.at[a, r - 1], device_id=peer, device_id_type=MESH)

    peers = [(r, peer, 4 * peer[0] + 2 * peer[1] + peer[2]) for r, peer in _relations()]
    for r, peer, p_idx in peers:
        for a in range(n_arrays):
            copy(a, r, peer, p_idx, me).start()
    for r, peer, p_idx in peers:
        for a in range(n_arrays):
            copy(a, r, peer, p_idx, me).wait_send()
            copy(a, r, peer, me, p_idx).wait_recv()
    for loc in local:
        loc.wait()


def _sem_scratch(n_arrays):
    return [pltpu.SemaphoreType.DMA((n_arrays, N_DEV - 1)), pltpu.SemaphoreType.DMA((n_arrays, N_DEV - 1)),
            pltpu.SemaphoreType.DMA((n_arrays,))]


def _slab(kind, ref, idx):
    if kind == "win":
        return ref.at[:, pl.ds(pl.multiple_of(idx * C_IN, LANE), C_IN)]
    if kind == "wup":
        return ref.at[:, pl.ds(pl.multiple_of(_up_slot(idx) * C_UP_P, LANE), C_UP_P)]
    if kind == "wkv":
        return ref.at[idx]
    if kind == "w4":
        return ref.at[:, pl.ds(pl.multiple_of(idx * R_O, 16), R_O), :]
    if kind == "wdn":
        return ref.at[pl.ds(pl.multiple_of(_dn_row(idx), 16), R_DN), :]
    assert kind == "cv"
    return ref.at[idx]


_WHOLE = {"win": ((D, 9 * D), BF), "wup": ((D, 2 * FF_P), BF), "wkv": ((N_DEV, D, C_KV), BF),
          "w4": ((4, D, D), BF), "wdn": ((FF_P, D), BF)}
_SHARD = {"win": (D, C_IN), "wup": (D, C_UP_P), "wkv": (D, C_KV), "w4": (4, R_O, D), "wdn": (R_DN, D)}
HBM_SPEC = pl.BlockSpec(memory_space=pltpu.HBM)
SEM_SPEC = pl.BlockSpec(memory_space=pltpu.SEMAPHORE)
_DATAFLOW = pltpu.SideEffectType.DATAFLOW_SIDE_EFFECTING


def _scatter_maps(kinds):
    return ((lambda srcs, lands, a, idx: _slab(kinds[a], srcs[a], idx)),
            (lambda lands, a, idx: lands[a].at[idx]))


_SLOTTED = ("wkv", "cv")


def _own_slab_blocks(kind, shard_shape):
    if kind in ("win", "wup"):
        rows, slot = 256, (_up_slot if kind == "wup" else (lambda m: m))
        return (shard_shape[0] // rows, (rows, shard_shape[1]), (lambda i, me: (i, slot(me[0]))),
                (lambda i, me: (i, 0)), (lambda i, me: (me[0], i, 0)))
    if kind == "w4":
        return (1, shard_shape, (lambda i, me: (0, me[0], 0)), (lambda i, me: (0, 0, 0)),
                (lambda i, me: (me[0], 0, 0, 0)))
    if kind == "wdn":
        rows = 32
        return (R_DN // rows, (rows, D), (lambda i, me: (_dn_row(me[0]) // rows + i, 0)), (lambda i, me: (i, 0)),
                (lambda i, me: (me[0], i, 0)))
    assert kind in _SLOTTED
    rows = min(256, shard_shape[0])
    return (shard_shape[0] // rows, (rows, shard_shape[1]), (lambda i, me: (me[0], i, 0)),
            (lambda i, me: (i, 0)), (lambda i, me: (me[0], i, 0)))


def _place_own(kind, src, out_sds, gather, me_arr, *, name):
    shard_shape = src.shape if gather else out_sds.shape[1:]
    steps, blk, whole_idx, shard_idx, staging_idx = _own_slab_blocks(kind, shard_shape)
    slotted = kind in _SLOTTED
    whole_spec = pl.BlockSpec(((None,) if slotted else ()) + tuple(blk), whole_idx)
    if gather:
        in_spec, out_spec = pl.BlockSpec(tuple(blk), shard_idx), whole_spec
    else:
        in_spec, out_spec = whole_spec, pl.BlockSpec((None,) + tuple(blk), staging_idx)
    zero_init = gather and kind == "wdn"

    def body(me_ref, src_ref, *rest):
        rest[-1][...] = src_ref[...].astype(rest[-1].dtype)

    operands = (me_arr, src) + ((jnp.zeros(out_sds.shape, out_sds.dtype),) if zero_init else ())
    return _pcall(
        body,
        grid_spec=pltpu.PrefetchScalarGridSpec(
            num_scalar_prefetch=1, grid=(steps,), in_specs=[in_spec] + ([ANY] if zero_init else []),
            out_specs=out_spec),
        out_shape=out_sds, input_output_aliases={2: 0} if zero_init else {},
        compiler_params=_cp(("arbitrary",)), name=name)(*operands)


def _peer_copies(n, src_of, dst_of, src_r, land_r, ssem, rsem):
    me = _my_index()
    out = []
    for r, peer in _relations():
        p_idx = 4 * peer[0] + 2 * peer[1] + peer[2]
        for a in range(n):
            def copy(src_idx, dst_idx, a=a, r=r, peer=peer):
                sem = a * (N_DEV - 1) + r - 1
                return pltpu.make_async_remote_copy(
                    src_ref=src_of(src_r, land_r, a, src_idx), dst_ref=dst_of(land_r, a, dst_idx),
                    send_sem=ssem.at[sem], recv_sem=rsem.at[sem], device_id=peer, device_id_type=MESH)
            out.append((functools.partial(copy, p_idx, me), functools.partial(copy, me, p_idx)))
    return out


def _exchange_start(srcs, lands, maps, after, *, name):
    n, ns = len(lands), len(srcs)
    src_of, dst_of = maps

    def body(*refs):
        src_r, land_r = refs[:ns], refs[ns:ns + n]
        ssem, rsem, token = refs[ns + n + 1], refs[ns + n + 2], refs[-1]
        for send, _ in _peer_copies(n, src_of, dst_of, src_r, land_r, ssem, rsem):
            send().start()
        token[...] = jnp.zeros_like(token)

    flight = list(srcs) + list(lands)
    outs = pl.pallas_call(
        body, name=name,
        out_shape=(pltpu.SemaphoreType.DMA((n * (N_DEV - 1),)), pltpu.SemaphoreType.DMA((n * (N_DEV - 1),)),
                   *[pltpu.HBM(a.shape, a.dtype) for a in flight], SDS((8, LANE), F32)),
        in_specs=[HBM_SPEC] * (ns + n) + [ANY],
        out_specs=(SEM_SPEC, SEM_SPEC, *[HBM_SPEC] * (ns + n), pl.BlockSpec(memory_space=pltpu.VMEM)),
        input_output_aliases={i: 2 + i for i in range(ns + n)},
        compiler_params=pltpu.CompilerParams(has_side_effects=_DATAFLOW),
    )(*[pltpu.with_memory_space_constraint(a, pltpu.HBM) for a in flight], after)
    return (outs[0], outs[1], list(outs[2:2 + ns + n]), ns), outs[-1]


def _exchange_wait(handle, maps, after, *, name):
    ssem, rsem, flight, ns = handle
    n = len(flight) - ns
    src_of, dst_of = maps

    def body(*refs):
        src_r, land_r, ssem_r, rsem_r = refs[:ns], refs[ns:ns + n], refs[ns + n], refs[ns + n + 1]
        for send, arrival in _peer_copies(n, src_of, dst_of, src_r, land_r, ssem_r, rsem_r):
            send().wait_send()
            arrival().wait_recv()

    outs = pl.pallas_call(
        body, name=name, out_shape=[pltpu.HBM(a.shape, a.dtype) for a in flight],
        in_specs=[HBM_SPEC] * (ns + n) + [SEM_SPEC, SEM_SPEC, ANY], out_specs=[HBM_SPEC] * (ns + n),
        input_output_aliases={i: i for i in range(ns + n)},
        compiler_params=pltpu.CompilerParams(has_side_effects=_DATAFLOW),
    )(*flight, ssem, rsem, after)
    return list(outs[ns:])


_SIBLING = 1
_ICI = (2, 4, 6)


def _rel_peer(r):
    x, y, c = lax.axis_index("x"), lax.axis_index("y"), lax.axis_index("c")
    peer = (x ^ ((r >> 2) & 1), y ^ ((r >> 1) & 1), c ^ (r & 1))
    return peer, 4 * peer[0] + 2 * peer[1] + peer[2]


def _rcopy(ref, ssem, rsem, peer):
    return pltpu.make_async_remote_copy(src_ref=ref, dst_ref=ref, send_sem=ssem, recv_sem=rsem, device_id=peer,
                                        device_id_type=MESH)


def _gather2_start(lands, kinds, after, *, name):
    n = len(lands)

    def body(*refs):
        land_r, (send1, recv_sib, recv_ici), token = refs[:n], refs[n + 1:n + 4], refs[-1]
        me = _my_index()
        for a in range(n):
            own = _slab(kinds[a], land_r[a], me)
            for j, r in enumerate((_SIBLING,) + _ICI):
                rsem = recv_sib.at[a] if r == _SIBLING else recv_ici.at[3 * a + j - 1]
                _rcopy(own, send1.at[4 * a + j], rsem, _rel_peer(r)[0]).start()
        token[...] = jnp.zeros_like(token)

    sems = [pltpu.SemaphoreType.DMA((4 * n,)), pltpu.SemaphoreType.DMA((n,)), pltpu.SemaphoreType.DMA((3 * n,))]
    outs = pl.pallas_call(
        body, name=name, out_shape=(*sems, *[pltpu.HBM(a.shape, a.dtype) for a in lands], SDS((8, LANE), F32)),
        in_specs=[HBM_SPEC] * n + [ANY],
        out_specs=(SEM_SPEC,) * 3 + (HBM_SPEC,) * n + (pl.BlockSpec(memory_space=pltpu.VMEM),),
        input_output_aliases={i: 3 + i for i in range(n)},
        compiler_params=pltpu.CompilerParams(has_side_effects=_DATAFLOW),
    )(*[pltpu.with_memory_space_constraint(a, pltpu.HBM) for a in lands], after)
    return dict(send1=outs[0], recv_sib=outs[1], recv_ici=outs[2], lands=list(outs[3:3 + n])), outs[-1]


def _gather2_forward(handle, kinds, after, *, name):
    lands = handle["lands"]
    n = len(lands)

    def body(*refs):
        land_r, recv_ici, (fwd_send, fwd_recv), token = refs[:n], refs[n], refs[n + 2:n + 4], refs[-1]
        sibling = _rel_peer(_SIBLING)[0]
        for a in range(n):
            for j, r in enumerate(_ICI):
                got = _slab(kinds[a], land_r[a], _rel_peer(r)[1])
                _rcopy(got, fwd_send.at[3 * a + j], recv_ici.at[3 * a + j], sibling).wait_recv()
                _rcopy(got, fwd_send.at[3 * a + j], fwd_recv.at[3 * a + j], sibling).start()
        token[...] = jnp.zeros_like(token)

    sems = [pltpu.SemaphoreType.DMA((3 * n,)), pltpu.SemaphoreType.DMA((3 * n,))]
    outs = pl.pallas_call(
        body, name=name, out_shape=(*sems, *[pltpu.HBM(a.shape, a.dtype) for a in lands], SDS((8, LANE), F32)),
        in_specs=[HBM_SPEC] * n + [SEM_SPEC, ANY],
        out_specs=(SEM_SPEC,) * 2 + (HBM_SPEC,) * n + (pl.BlockSpec(memory_space=pltpu.VMEM),),
        input_output_aliases={i: 2 + i for i in range(n)},
        compiler_params=pltpu.CompilerParams(has_side_effects=_DATAFLOW),
    )(*lands, handle["recv_ici"], after)
    return dict(handle, fwd_send=outs[0], fwd_recv=outs[1], lands=list(outs[2:2 + n])), outs[-1]


def _gather2_wait(handle, kinds, after, *, name):
    lands = handle["lands"]
    n = len(lands)

    def body(*refs):
        land_r, (send1, recv_sib, fwd_send, fwd_recv) = refs[:n], refs[n:n + 4]
        me = _my_index()
        sibling, sib_idx = _rel_peer(_SIBLING)
        for a in range(n):
            own = _slab(kinds[a], land_r[a], me)
            for j, r in enumerate((_SIBLING,) + _ICI):
                _rcopy(own, send1.at[4 * a + j], recv_sib.at[a], _rel_peer(r)[0]).wait_send()
            theirs = _slab(kinds[a], land_r[a], sib_idx)
            _rcopy(theirs, send1.at[4 * a], recv_sib.at[a], sibling).wait_recv()
            for j, r in enumerate(_ICI):
                passed_on = _slab(kinds[a], land_r[a], _rel_peer(r)[1])
                _rcopy(passed_on, fwd_send.at[3 * a + j], fwd_recv.at[3 * a + j], sibling).wait_send()
                arrived = _slab(kinds[a], land_r[a], _rel_peer(r ^ _SIBLING)[1])
                _rcopy(arrived, fwd_send.at[3 * a + j], fwd_recv.at[3 * a + j], sibling).wait_recv()

    outs = pl.pallas_call(
        body, name=name, out_shape=[pltpu.HBM(a.shape, a.dtype) for a in lands],
        in_specs=[HBM_SPEC] * n + [SEM_SPEC] * 4 + [ANY], out_specs=[HBM_SPEC] * n,
        input_output_aliases={i: i for i in range(n)},
        compiler_params=pltpu.CompilerParams(has_side_effects=_DATAFLOW),
    )(*lands, handle["send1"], handle["recv_sib"], handle["fwd_send"], handle["fwd_recv"], after)
    return list(outs)


def _allreduce_small(pack, *, name):
    rows = pack.shape[0]

    def body(p_ref, out_ref, gath, ssem, rsem, lsem):
        _exchange(1, lambda a, idx: p_ref, lambda a, idx: gath.at[idx], (ssem, rsem, lsem))
        total = gath[0]
        for d in range(1, N_DEV):
            total = total + gath[d]
        out_ref[...] = total

    vm = pl.BlockSpec(memory_space=pltpu.VMEM)
    return _pcall(
        body, in_specs=[vm], out_specs=vm, out_shape=SDS(pack.shape, F32),
        scratch_shapes=[pltpu.VMEM((N_DEV, rows, pack.shape[1]), F32)] + _sem_scratch(1),
        compiler_params=_cp(has_side_effects=True), name=name)(pack)


def _adam(g, w, m, v):
    nm = ADAM_B1 * m + (1.0 - ADAM_B1) * g
    nv = ADAM_B2 * v + (1.0 - ADAM_B2) * (g * g)
    m_hat = nm / (1.0 - ADAM_B1 ** ADAM_STEP)
    v_hat = nv / (1.0 - ADAM_B2 ** ADAM_STEP)
    return -ADAM_LR * (m_hat / (jnp.sqrt(v_hat) + ADAM_EPS) + ADAM_WD * w), nm, nv


def _adamw_staged(st0, st1, w, m, v, *, name):
    _, rows, cols = w.shape
    st_cols = st0.shape[2]
    tr = max(t for t in range(16, 129, 16) if rows % t == 0)
    nr = rows // tr

    def body(s0_ref, s1_ref, w_ref, m_ref, v_ref, g_ref, d_ref, nm_ref, nv_ref):
        for layer, s_ref in enumerate((s0_ref, s1_ref)):
            @pl.when(pl.program_id(0) == layer)
            def _(s_ref=s_ref):
                total = s_ref[0, :, 0:cols].astype(F32)
                for dev in range(1, N_DEV):
                    total = total + s_ref[dev, :, 0:cols].astype(F32)
                g_ref[0] = total

        d_ref[0], nm_ref[0], nv_ref[0] = _adam(g_ref[0], w_ref[0], m_ref[0], v_ref[0])

    st_spec = lambda layer: pl.BlockSpec(
        (N_DEV, tr, st_cols), lambda l, i: (0, jnp.where(l == layer, i, (nr - 1) * (1 - layer)), 0))
    par = pl.BlockSpec((1, tr, cols), lambda l, i: (l, i, 0))
    return _pcall(
        body, grid=(DEPTH, nr), in_specs=[st_spec(0), st_spec(1), par, par, par], out_specs=[par] * 4,
        out_shape=[SDS(w.shape, F32)] * 4,
        compiler_params=_cp(("arbitrary", "arbitrary")), name=name)(st0, st1, w, m, v)


def _adamw_small(g, w, m, v, *, name):
    def body(g_ref, w_ref, m_ref, v_ref, d_ref, nm_ref, nv_ref):
        d_ref[...], nm_ref[...], nv_ref[...] = _adam(g_ref[...], w_ref[...], m_ref[...], v_ref[...])

    return _pcall(body, out_shape=[SDS(g.shape, F32)] * 3, compiler_params=_cp(), name=name)(g, w, m, v)


def _pack_rows(arrays):
    flat = jnp.concatenate([a.reshape(-1).astype(F32) for a in arrays])
    rows = -(-flat.shape[0] // (8 * D)) * 8
    return jnp.pad(flat, (0, rows * D - flat.shape[0])).reshape(rows, D)


def _unpack_rows(pack, like):
    flat = pack.reshape(-1)
    out, at = [], 0
    for a in like:
        out.append(flat[at:at + a.size].reshape(a.shape))
        at += a.size
    return out


def _layer_fwd(x, h, mem, win, rest_of_weights, after_up, small, g_next, tag):
    proj = _mm(h, win, tm=1024, tn=1536, name=f"proj_{tag}")
    wup, wkv, w4, wdn, cw_a, cw_b, cw_f = rest_of_weights(proj)
    za = _bra_fwd(proj, cw_a, name=f"bra_fwd_{tag}")
    cb = _brb_conv_fwd(proj, cw_b, small["conv_b_bias"], name=f"brb_conv_fwd_{tag}")
    sb = _ln_silu_fwd(cb, small["ln_b_g"], small["ln_b_b"], name=f"ln_silu_fwd_{tag}")
    memn, kv = _kv_prep(mem, small["norm_mem_g"], wkv, name=f"kv_prep_{tag}")
    o = _attn_fwd(proj, kv, name=f"attn_fwd_{tag}")
    ya, yb, yc, mg, x1, h2 = _mix_out(x, za, sb, o, proj, w4, small["b_gate"], small["norm_ffn_g"],
                                      name=f"mix_out_{tag}")
    u2 = _mm(h2, wup, tm=1024, tn=1536, name=f"up_{tag}")
    token = after_up(u2)
    act, c2 = _ffn_act(u2, cw_f if token is None else _behind(cw_f, token), name=f"ffn_act_{tag}")
    x2, h_next = _mm_res_norm(act, wdn, x1, g_next, name=f"down_{tag}")
    saved = dict(x=x, h=h, proj=proj, za=za, cb=cb, sb=sb, memn=memn, kv=kv, o=o, ya=ya, yb=yb, yc=yc,
                 mg=mg, x1=x1, h2=h2, u2=u2, c2=c2, act=act)
    return x2, h_next, (win, wup, wkv, w4, wdn, cw_a, cw_b, cw_f), saved


def _behind(operand, token):
    return operand + token[0:1, 0:1]


def _layer_bwd(dx2, dx2b, mem, wts, small, sv, start, tag):
    win, wup, wkv, w4, wdn, cw_a, cw_b, cw_f = wts
    dact = _mm(dx2b, wdn, tb=True, tm=1024, tn=768, name=f"d_act_{tag}")
    dwdn = _mm(sv["act"], dx2b, ta=True, tm=768, tn=1024, name=f"dw_down_{tag}")
    du2, dcw_f = _ffn_bwd(sv["u2"], sv["c2"], dact, cw_f, name=f"ffn_bwd_{tag}")
    dwup = _mm(sv["h2"], du2, ta=True, tm=1024, tn=768, name=f"dw_up_{tag}")
    token = start(("wdn", "wup"), (dwdn, dwup), f"ffn_{tag}")
    dx1, dx1b, dg_ffn = _mm_nt_normbwd(du2, wup, sv["x1"], dx2, _behind(small["norm_ffn_g"], token),
                                       name=f"d_h2_{tag}")

    dya, dyb, dyc, dza, dsb, do, dproj, dbg = _mix_bwd(dx1b, sv["ya"], sv["yb"], sv["yc"], sv["proj"], w4,
                                                      small["b_gate"], name=f"mix_bwd_{tag}")
    dw4 = jnp.stack([
        _mm(a, b, ta=True, tm=1024, tn=512, name=f"dw_{nm}_{tag}")
        for nm, a, b in (("a_out", sv["za"], dya), ("b_out", sv["sb"], dyb), ("att_out", sv["o"], dyc),
                         ("o", sv["mg"], dx1b))])
    dq, dk, dv = _attn_bwd(sv["proj"], sv["kv"], do, name=f"attn_bwd_{tag}")
    dwkv, dg_mem = _kv_bwd(mem, small["norm_mem_g"], sv["memn"], dk, dv, wkv, name=f"kv_bwd_{tag}")
    token = start(("w4", "wkv"), (dw4, dwkv), f"mix_{tag}")
    dproj, dcw_a = _bra_bwd(sv["proj"], dza, _behind(cw_a, token), dproj, name=f"bra_bwd_{tag}")
    dcb, ln_sums = _ln_silu_bwd(sv["cb"], dsb, small["ln_b_g"], small["ln_b_b"], name=f"ln_silu_bwd_{tag}")
    dproj, dcw_b = _brb_conv_bwd(sv["proj"], dcb, dq, cw_b, dproj, name=f"brb_conv_bwd_{tag}")
    dwin = _mm(sv["h"], dproj, ta=True, tm=1024, tn=768, name=f"dw_in_{tag}")
    token = start(("win",), (dwin,), f"in_{tag}")
    dx, dxb, dg_mix = _mm_nt_normbwd(dproj, win, sv["x"], dx1, _behind(small["norm_mix_g"], token),
                                     tk=4608, name=f"d_h_{tag}")

    small_grads = [dg_mix[0:1], dg_mem[0:1], dbg[0:1].reshape(3, D), ln_sums[2:3], ln_sums[0:1], ln_sums[1:2],
                   dg_ffn[0:1], dcw_a[0:K_A], dcw_b[0:K_B], dcw_f[0:K_F].reshape(K_F * 2 * FF_P // D, D)]
    return dx, dxb, small_grads, token


_SMALL_ROWS = (1, 1, 3, 1, 1, 1, 1, K_A, K_B, K_F * 2 * FF_P // D)
_CV_ROWS = 48


def kernel(x, mem, norm_mix_g, norm_mem_g, w_in, b_gate, conv_a_w, w_a_out, conv_b_w, conv_b_bias, ln_b_g, ln_b_b, w_b_out, w_kv, w_att_out, w_o, norm_ffn_g, w_up, conv_ffn_w, w_down, norm_final_g, loss_target, m_norm_mix_g, m_norm_mem_g, m_w_in, m_b_gate, m_conv_a_w, m_w_a_out, m_conv_b_w, m_conv_b_bias, m_ln_b_g, m_ln_b_b, m_w_b_out, m_w_kv, m_w_att_out, m_w_o, m_norm_ffn_g, m_w_up, m_conv_ffn_w, m_w_down, m_norm_final_g, v_norm_mix_g, v_norm_mem_g, v_w_in, v_b_gate, v_conv_a_w, v_w_a_out, v_conv_b_w, v_conv_b_bias, v_ln_b_g, v_ln_b_b, v_w_b_out, v_w_kv, v_w_att_out, v_w_o, v_norm_ffn_g, v_w_up, v_conv_ffn_w, v_w_down, v_norm_final_g):
    me = _my_index()
    me_arr = me.astype(jnp.int32).reshape(1)
    x0, mem0, tgt = x.reshape(x.shape[1:]), mem.reshape(mem.shape[1:]), loss_target.reshape(x.shape[1:])
    up_pad = ((0, 0), (0, 0), (0, C_UP_P - C_UP))

    ag_groups = (("win",), ("wup", "wkv", "w4", "wdn", "cv"))
    kinds = ag_groups[0] + ag_groups[1]
    smalls, ag_handles = [], []
    token = jnp.zeros((8, LANE), F32)
    for l in range(DEPTH):
        cv = jnp.zeros((_CV_ROWS, C_UP_P), F32)
        cv = cv.at[0:K_F, 0:C_UP].set(conv_ffn_w[l]).at[3:3 + K_A, 0:R_O].set(conv_a_w[l])
        cv = cv.at[8:8 + K_B, 0:R_O].set(conv_b_w[l])
        shards = dict(
            win=w_in[l], wup=jnp.pad(w_up[l], up_pad[1:]), wkv=w_kv[l],
            w4=jnp.stack([w_a_out[l], w_b_out[l], w_att_out[l], w_o[l]]), wdn=w_down[l], cv=cv)
        whole = dict({k: SDS(*_WHOLE[k]) for k in kinds[:-1]}, cv=SDS((N_DEV,) + cv.shape, F32))
        lands = {k: _place_own(k, shards[k], whole[k], True, me_arr, name=f"ag_own_{k}_l{l}") for k in kinds}
        per_layer = []
        for g, grp in enumerate(ag_groups):
            handle, token = _gather2_start([lands[k] for k in grp], grp, token, name=f"ag_start_l{l}_g{g}")
            per_layer.append(handle)
        ag_handles.append(per_layer)
        smalls.append(dict(
            norm_mix_g=norm_mix_g[l][None], norm_mem_g=norm_mem_g[l][None], b_gate=b_gate[l][None],
            conv_b_bias=conv_b_bias[l][None], ln_b_g=ln_b_g[l][None], ln_b_b=ln_b_b[l][None],
            norm_ffn_g=norm_ffn_g[l][None]))

    def forward_group(l, g, after):
        ag_handles[l][g], tok = _gather2_forward(ag_handles[l][g], ag_groups[g], after, name=f"ag_forward_l{l}_g{g}")
        return tok

    def rest_of_weights(l):
        def wait(after):
            if l == 0:
                after = forward_group(0, 1, after)
            wup, wkv, w4, wdn, cvg = _gather2_wait(ag_handles[l][1], ag_groups[1], after, name=f"ag_wait_l{l}_g1")
            cw_f = jnp.stack([cvg[d, 0:K_F, :] for d in UP_ORDER], axis=1).reshape(K_F, 2 * FF_P)
            cw_a = cvg[:, 3:3 + K_A, 0:R_O].transpose(1, 0, 2).reshape(K_A, D)
            cw_b = cvg[:, 8:8 + K_B, 0:R_O].transpose(1, 0, 2).reshape(K_B, D)
            return (wup, wkv, w4, wdn, jnp.pad(cw_a, ((0, 8 - K_A), (0, 0))),
                    jnp.pad(cw_b, ((0, 32 - K_B), (0, 0))), jnp.pad(cw_f, ((0, 8 - K_F), (0, 0))))
        return wait

    wts, saved = [], []
    xs = x0
    h = _rms_fwd(xs, smalls[0]["norm_mix_g"], name="rms_fwd")
    behind = forward_group(0, 0, token)

    def next_layer_forwarding(l):
        def hook(after):
            if l + 1 == DEPTH:
                return None
            return forward_group(l + 1, 1, forward_group(l + 1, 0, after))
        return hook

    for l in range(DEPTH):
        g_next = smalls[l + 1]["norm_mix_g"] if l + 1 < DEPTH else norm_final_g[None]
        (win,) = _gather2_wait(ag_handles[l][0], ag_groups[0], behind, name=f"ag_wait_l{l}_g0")
        xs, h, w_l, sv = _layer_fwd(xs, h, mem0, win, rest_of_weights(l), next_layer_forwarding(l), smalls[l],
                                    g_next, f"l{l}")
        behind = h
        wts.append(w_l)
        saved.append(sv)
    dx, dxb, head_sums = _loss_head(xs, tgt, norm_final_g[None], name="loss_head")

    rs_handles = []
    small_grads = [None] * DEPTH

    def start_scatter(grp, arrays, name):
        maps = _scatter_maps(grp)
        lands = [_place_own(k, a, SDS((N_DEV,) + _SHARD[k], BF), False, me_arr, name=f"rs_own_{k}_{name}")
                 for k, a in zip(grp, arrays)]
        handle, tok = _exchange_start(list(arrays), lands, maps, rs_handles[-1][2] if rs_handles else head_sums,
                                      name=f"rs_start_{name}")
        rs_handles.append((grp, handle, tok, name))
        return tok

    for l in reversed(range(DEPTH)):
        dx, dxb, small_grads[l], token = _layer_bwd(dx, dxb, mem0, wts[l], smalls[l], saved[l], start_scatter,
                                                    f"l{l}")

    staged = [dict() for _ in range(DEPTH)]
    for grp, handle, _, name in rs_handles[:-1]:
        staged[int(name[-1])].update(zip(grp, _exchange_wait(handle, _scatter_maps(grp), dx, name=f"rs_wait_{name}")))

    pack = jnp.concatenate(small_grads[0] + small_grads[1] + [head_sums[1:2], head_sums[0:1]], axis=0)
    pack = jnp.pad(pack, ((0, -pack.shape[0] % 8), (0, 0)))
    total = _allreduce_small(pack, name="allreduce_small")
    per_layer = sum(_SMALL_ROWS)
    parts = []
    for l in range(DEPTH):
        at, one = l * per_layer, []
        for rows in _SMALL_ROWS:
            one.append(total[at:at + rows])
            at += rows
        parts.append(one)
    g_final = total[DEPTH * per_layer]
    loss = 0.5 / D * jnp.sum(total[DEPTH * per_layer + 1])

    def both(i):
        return jnp.stack([parts[0][i], parts[1][i]])

    g_norm_mix, g_norm_mem = both(0)[:, 0], both(1)[:, 0]
    g_b_gate = both(2).reshape(DEPTH, 3 * D)
    g_cbias, g_lng, g_lnb, g_norm_ffn = both(3)[:, 0], both(4)[:, 0], both(5)[:, 0], both(6)[:, 0]
    g_conv_a = lax.dynamic_slice_in_dim(both(7), me * R_O, R_O, axis=2)
    g_conv_b = lax.dynamic_slice_in_dim(both(8), me * R_O, R_O, axis=2)
    g_conv_f = lax.dynamic_slice_in_dim(both(9).reshape(DEPTH, K_F, 2 * FF_P), _up_slot(me) * C_UP_P, C_UP, axis=2)

    small_g = [g_norm_mix, g_norm_mem, g_b_gate, g_conv_a, g_conv_b, g_cbias, g_lng, g_lnb, g_norm_ffn, g_conv_f,
               g_final]
    small_w = [norm_mix_g, norm_mem_g, b_gate, conv_a_w, conv_b_w, conv_b_bias, ln_b_g, ln_b_b, norm_ffn_g,
               conv_ffn_w, norm_final_g]
    small_m = [m_norm_mix_g, m_norm_mem_g, m_b_gate, m_conv_a_w, m_conv_b_w, m_conv_b_bias, m_ln_b_g, m_ln_b_b,
               m_norm_ffn_g, m_conv_ffn_w, m_norm_final_g]
    small_v = [v_norm_mix_g, v_norm_mem_g, v_b_gate, v_conv_a_w, v_conv_b_w, v_conv_b_bias, v_ln_b_g, v_ln_b_b,
               v_norm_ffn_g, v_conv_ffn_w, v_norm_final_g]
    upd = _adamw_small(_pack_rows(small_g), _pack_rows(small_w), _pack_rows(small_m), _pack_rows(small_v),
                       name="adamw_small")
    s_d, s_m, s_v = (_unpack_rows(p, small_w) for p in upd)
    (d_norm_mix, d_norm_mem, d_b_gate, d_conv_a, d_conv_b, d_cbias, d_lng, d_lnb, d_norm_ffn, d_conv_f,
     d_final) = s_d
    (nm_norm_mix, nm_norm_mem, nm_b_gate, nm_conv_a, nm_conv_b, nm_cbias, nm_lng, nm_lnb, nm_norm_ffn, nm_conv_f,
     nm_final) = s_m
    (nv_norm_mix, nv_norm_mem, nv_b_gate, nv_conv_a, nv_conv_b, nv_cbias, nv_lng, nv_lnb, nv_norm_ffn, nv_conv_f,
     nv_final) = s_v

    def big_update(kind, w, m, v, name):
        return _adamw_staged(staged[0][kind], staged[1][kind], w, m, v, name=name)

    r_up = big_update("wup", w_up, m_w_up, v_w_up, "adamw_w_up")
    r_kv = big_update("wkv", w_kv, m_w_kv, v_w_kv, "adamw_w_kv")
    r_dn = big_update("wdn", w_down, m_w_down, v_w_down, "adamw_w_down")

    def four(a, b, c, d_):
        return jnp.stack([a, b, c, d_], axis=1).reshape(DEPTH, 4 * R_O, D)

    r_4 = _adamw_staged(
        staged[0]["w4"].reshape(N_DEV, 4 * R_O, D), staged[1]["w4"].reshape(N_DEV, 4 * R_O, D),
        four(w_a_out, w_b_out, w_att_out, w_o), four(m_w_a_out, m_w_b_out, m_w_att_out, m_w_o),
        four(v_w_a_out, v_w_b_out, v_w_att_out, v_w_o), name="adamw_w_out")
    grp, handle, _, name = rs_handles[-1]
    staged[0].update(zip(grp, _exchange_wait(handle, _scatter_maps(grp), r_4[0], name=f"rs_wait_{name}")))
    r_in = big_update("win", w_in, m_w_in, v_w_in, "adamw_w_in")
    r_a, r_b, r_att, r_o = ([a.reshape(DEPTH, 4, R_O, D)[:, j] for a in r_4] for j in range(4))

    grads = [g_norm_mix, g_norm_mem, r_in[0], g_b_gate, g_conv_a, r_a[0], g_conv_b, g_cbias, g_lng, g_lnb, r_b[0],
             r_kv[0], r_att[0], r_o[0], g_norm_ffn, r_up[0], g_conv_f, r_dn[0], g_final]
    deltas = [d_norm_mix, d_norm_mem, r_in[1], d_b_gate, d_conv_a, r_a[1], d_conv_b, d_cbias, d_lng, d_lnb, r_b[1],
              r_kv[1], r_att[1], r_o[1], d_norm_ffn, r_up[1], d_conv_f, r_dn[1], d_final]
    new_m = [nm_norm_mix, nm_norm_mem, r_in[2], nm_b_gate, nm_conv_a, r_a[2], nm_conv_b, nm_cbias, nm_lng, nm_lnb,
             r_b[2], r_kv[2], r_att[2], r_o[2], nm_norm_ffn, r_up[2], nm_conv_f, r_dn[2], nm_final]
    new_v = [nv_norm_mix, nv_norm_mem, r_in[3], nv_b_gate, nv_conv_a, r_a[3], nv_conv_b, nv_cbias, nv_lng, nv_lnb,
             r_b[3], r_kv[3], r_att[3], r_o[3], nv_norm_ffn, r_up[3], nv_conv_f, r_dn[3], nv_final]
    return (loss, dx[None], *grads, *deltas, *new_m, *new_v)
```

```python
import functools

import jax
import jax.numpy as jnp
import numpy as np
from jax import lax
from jax.experimental import pallas as pl
from jax.experimental.pallas import tpu as pltpu

F32 = jnp.float32
BF = jnp.bfloat16
SDS = jax.ShapeDtypeStruct
MESH = pl.DeviceIdType.MESH
ANY = pl.BlockSpec(memory_space=pl.ANY)

N_DEV = 8
DEPTH = 2
D = 1024
N_HEADS = 4
HEAD = D // N_HEADS
D_FF = 2816
K_A, K_B, K_F = 3, 31, 3
NORM_EPS = 1e-6

C_IN = 9 * D // N_DEV
C_KV = 2 * D // N_DEV
C_UP = 2 * D_FF // N_DEV
LANE = 128
C_UP_P = -(-C_UP // LANE) * LANE
FF_P = 4 * C_UP_P
R_O = D // N_DEV
R_DN = D_FF // N_DEV

VMEM_LIMIT = 56 * 1024 * 1024
TM = 512
TR = 256
SUB = 128
H_S, H_L = 16, 32

ADAM_LR, ADAM_B1, ADAM_B2, ADAM_EPS, ADAM_WD, ADAM_STEP = 0.001, 0.9, 0.999, 1e-08, 0.01, 10

UP_ORDER = (0, 4, 1, 5, 2, 6, 3, 7)


def _pcall(body, **kw):
    return pl.pallas_call(body, **kw)


def _cp(sem=None, **kw):
    return pltpu.CompilerParams(dimension_semantics=sem, vmem_limit_bytes=VMEM_LIMIT, **kw)


def _dot(a, b):
    return jnp.dot(a, b, preferred_element_type=F32)


def _dot_nt(a, b):
    return lax.dot_general(a, b, (((1,), (1,)), ((), ())), preferred_element_type=F32)


def _dot_tn(a, b):
    return lax.dot_general(a, b, (((0,), (0,)), ((), ())), preferred_element_type=F32)


def _sigmoid(z):
    return 1.0 / (1.0 + jnp.exp(-z))


def _rms(xv):
    return lax.rsqrt(jnp.mean(xv * xv, axis=-1, keepdims=True) + NORM_EPS)


def _up_slot(idx):
    return jnp.where(idx < 4, 2 * idx, 2 * (idx - 4) + 1)


def _dn_row(idx):
    return C_UP_P * (idx // 2) + R_DN * (idx % 2)


def _mm(a, b, *, ta=False, tb=False, out_dtype=BF, tm=TM, tn=512, tk=None, name):
    m, k_dim = (a.shape[1], a.shape[0]) if ta else a.shape
    n = b.shape[0] if tb else b.shape[1]
    tm, tn = min(tm, m), min(tn, n)
    tk = k_dim if tk is None else min(tk, k_dim)
    nk = k_dim // tk
    assert m % tm == 0 and n % tn == 0 and k_dim % tk == 0
    dims = (((0 if ta else 1,), (1 if tb else 0,)), ((), ()))

    def body(a_ref, b_ref, o_ref, *scratch):
        part = lax.dot_general(a_ref[...], b_ref[...], dims, preferred_element_type=F32)
        if nk == 1:
            o_ref[...] = part.astype(o_ref.dtype)
            return
        acc = scratch[0]
        k = pl.program_id(2)

        @pl.when(k == 0)
        def _():
            acc[...] = part

        @pl.when(k > 0)
        def _():
            acc[...] += part

        @pl.when(k == nk - 1)
        def _():
            o_ref[...] = acc[...].astype(o_ref.dtype)

    a_spec = pl.BlockSpec((tk, tm), lambda i, j, k: (k, i)) if ta else pl.BlockSpec((tm, tk), lambda i, j, k: (i, k))
    b_spec = pl.BlockSpec((tn, tk), lambda i, j, k: (j, k)) if tb else pl.BlockSpec((tk, tn), lambda i, j, k: (k, j))
    return _pcall(
        body, grid=(m // tm, n // tn, nk), in_specs=[a_spec, b_spec],
        out_specs=pl.BlockSpec((tm, tn), lambda i, j, k: (i, j)),
        out_shape=SDS((m, n), out_dtype),
        scratch_shapes=[pltpu.VMEM((tm, tn), F32)] if nk > 1 else [],
        compiler_params=_cp(("parallel", "parallel", "arbitrary")), name=name)(a, b)


def _mm_res_norm(a, w, x, g, *, name):
    s, k_dim = a.shape
    tm = min(TM, s)

    def body(a_ref, w_ref, x_ref, g_ref, xo_ref, h_ref):
        xo = x_ref[...] + _dot(a_ref[...], w_ref[...])
        xo_ref[...] = xo
        h_ref[...] = ((xo * _rms(xo)) * g_ref[...]).astype(BF)

    return _pcall(
        body, grid=(s // tm,),
        in_specs=[pl.BlockSpec((tm, k_dim), lambda i: (i, 0)),
                  pl.BlockSpec((k_dim, D), lambda i: (0, 0), pipeline_mode=pl.Buffered(1)),
                  pl.BlockSpec((tm, D), lambda i: (i, 0)), pl.BlockSpec((1, D), lambda i: (0, 0))],
        out_specs=[pl.BlockSpec((tm, D), lambda i: (i, 0))] * 2,
        out_shape=[SDS((s, D), F32), SDS((s, D), BF)],
        compiler_params=_cp(("parallel",)), name=name)(a, w, x, g)


def _mm_nt_normbwd(da, w, x, dres, g, *, tk=None, name):
    s, k_dim = da.shape
    tm = min(TM, s)
    tk = k_dim if tk is None else tk
    nk = k_dim // tk
    assert k_dim % tk == 0

    def body(da_ref, w_ref, x_ref, dres_ref, g_ref, dx_ref, dxb_ref, dg_ref, *scratch):
        i, k = pl.program_id(0), pl.program_id(1)
        part = _dot_nt(da_ref[...], w_ref[...])
        if nk > 1:
            acc = scratch[0]

            @pl.when(k == 0)
            def _():
                acc[...] = part

            @pl.when(k > 0)
            def _():
                acc[...] += part

        @pl.when((i == 0) & (k == 0))
        def _():
            dg_ref[...] = jnp.zeros_like(dg_ref)

        @pl.when(k == nk - 1)
        def _():
            dh = acc[...] if nk > 1 else part
            xv = x_ref[...]
            r = _rms(xv)
            xn = xv * r
            dg_ref[0:1, :] += jnp.sum(dh * xn, axis=0, keepdims=True)
            dxn = dh * g_ref[...]
            dx = dres_ref[...] + r * (dxn - xn * jnp.mean(dxn * xn, axis=-1, keepdims=True))
            dx_ref[...] = dx
            dxb_ref[...] = dx.astype(BF)

    row = lambda i, k: (i, 0)
    w_spec = (pl.BlockSpec((D, tk), lambda i, k: (0, k)) if nk > 1 else
              pl.BlockSpec((D, tk), lambda i, k: (0, 0), pipeline_mode=pl.Buffered(1)))
    return _pcall(
        body, grid=(s // tm, nk),
        in_specs=[pl.BlockSpec((tm, tk), lambda i, k: (i, k)), w_spec,
                  pl.BlockSpec((tm, D), row), pl.BlockSpec((tm, D), row), pl.BlockSpec((1, D), lambda i, k: (0, 0))],
        out_specs=[pl.BlockSpec((tm, D), row), pl.BlockSpec((tm, D), row), pl.BlockSpec((8, D), lambda i, k: (0, 0))],
        out_shape=[SDS((s, D), F32), SDS((s, D), BF), SDS((8, D), F32)],
        scratch_shapes=[pltpu.VMEM((tm, D), F32)] if nk > 1 else [],
        compiler_params=_cp(("arbitrary", "arbitrary")), name=name)(da, w, x, dres, g)


def _rms_fwd(x, g, *, name):
    s = x.shape[0]
    tm = min(TM, s)

    def body(x_ref, g_ref, h_ref):
        xv = x_ref[...]
        h_ref[...] = ((xv * _rms(xv)) * g_ref[...]).astype(BF)

    return _pcall(
        body, grid=(s // tm,),
        in_specs=[pl.BlockSpec((tm, D), lambda i: (i, 0)), pl.BlockSpec((1, D), lambda i: (0, 0))],
        out_specs=pl.BlockSpec((tm, D), lambda i: (i, 0)), out_shape=SDS((s, D), BF),
        compiler_params=_cp(("parallel",)), name=name)(x, g)


def _loss_head(x, tgt, g, *, name):
    s = x.shape[0]
    tm = min(TM, s)

    def body(x_ref, t_ref, g_ref, dx_ref, dxb_ref, sums_ref):
        @pl.when(pl.program_id(0) == 0)
        def _():
            sums_ref[...] = jnp.zeros_like(sums_ref)

        xv = x_ref[...]
        r = _rms(xv)
        xn = xv * r
        diff = xn * g_ref[...] - t_ref[...]
        sums_ref[0:1, :] += jnp.sum(diff * diff, axis=0, keepdims=True)
        dy = diff * (1.0 / D)
        sums_ref[1:2, :] += jnp.sum(dy * xn, axis=0, keepdims=True)
        dxn = dy * g_ref[...]
        dx = r * (dxn - xn * jnp.mean(dxn * xn, axis=-1, keepdims=True))
        dx_ref[...] = dx
        dxb_ref[...] = dx.astype(BF)

    row = lambda i: (i, 0)
    return _pcall(
        body, grid=(s // tm,),
        in_specs=[pl.BlockSpec((tm, D), row), pl.BlockSpec((tm, D), row), pl.BlockSpec((1, D), lambda i: (0, 0))],
        out_specs=[pl.BlockSpec((tm, D), row), pl.BlockSpec((tm, D), row), pl.BlockSpec((8, D), lambda i: (0, 0))],
        out_shape=[SDS((s, D), F32), SDS((s, D), BF), SDS((8, D), F32)],
        compiler_params=_cp(("arbitrary",)), name=name)(x, tgt, g)


def _halo_before(i, tr, h):
    return jnp.maximum(i * (tr // h) - 1, 0)


def _halo_after(i, tr, h, s):
    return jnp.minimum((i + 1) * (tr // h), s // h - 1)


def _taps(buf, w_ref, sl, k_w, base, rows):
    acc = None
    for k in range(k_w):
        t = w_ref[k:k + 1, sl] * buf[base + k:base + k + rows, sl]
        acc = t if acc is None else acc + t
    return acc


def _taps_rev(buf, w_ref, sl, k_w, base, rows):
    acc = None
    for k in range(k_w):
        t = w_ref[k:k + 1, sl] * buf[base + k_w - 1 - k:base + k_w - 1 - k + rows, sl]
        acc = t if acc is None else acc + t
    return acc


def _tap_grads(dw_ref, dc, buf, sl, k_w, base, rows):
    for k in range(k_w):
        dw_ref[k:k + 1, sl] += jnp.sum(dc * buf[base + k:base + k + rows, sl], axis=0, keepdims=True)


def _bra_fwd(proj, cw, *, name):
    s = proj.shape[0]
    tr, h = min(TR, s), H_S
    sub = min(SUB, tr)

    def body(cur, halo, w_ref, za_ref, cvb):
        i = pl.program_id(0)
        hv = halo[:, D:2 * D].astype(F32) * halo[:, 2 * D:3 * D].astype(F32)
        cvb[0:h, :] = jnp.where(i == 0, 0.0, hv)
        cvb[h:h + tr, :] = cur[:, D:2 * D].astype(F32) * cur[:, 2 * D:3 * D].astype(F32)
        for c in range(D // LANE):
            sl = slice(LANE * c, LANE * c + LANE)
            ca = _taps(cvb, w_ref, sl, K_A, h - (K_A - 1), tr)
            za_ref[:, sl] = (cur[:, sl].astype(F32) * ca).astype(BF)

    return _pcall(
        body, grid=(s // tr,),
        in_specs=[pl.BlockSpec((tr, 3 * D), lambda i: (i, 0)),
                  pl.BlockSpec((h, 3 * D), lambda i: (_halo_before(i, tr, h), 0)),
                  pl.BlockSpec((8, D), lambda i: (0, 0))],
        out_specs=pl.BlockSpec((tr, D), lambda i: (i, 0)), out_shape=SDS((s, D), BF),
        scratch_shapes=[pltpu.VMEM((h + tr, D), F32)],
        compiler_params=_cp(("parallel",)), name=name)(proj, proj, cw)


def _bra_bwd(proj, dza, cw, dproj, *, name):
    s = proj.shape[0]
    tr, h = min(TR, s), H_S
    sub = min(SUB, tr)
    n = s // tr

    def body(before, cur, after, dz_cur, dz_after, w_ref, dproj_in, da_ref, dw_ref, cvb, dcab):
        del dproj_in
        i = pl.program_id(0)

        @pl.when(i == 0)
        def _():
            dw_ref[...] = jnp.zeros_like(dw_ref)

        first, last = i == 0, i == n - 1
        cvb[0:h, :] = jnp.where(first, 0.0, before[:, D:2 * D].astype(F32) * before[:, 2 * D:3 * D].astype(F32))
        cvb[h:h + tr, :] = cur[:, D:2 * D].astype(F32) * cur[:, 2 * D:3 * D].astype(F32)
        dcab[0:tr, :] = dz_cur[...].astype(F32) * cur[:, 0:D].astype(F32)
        dcab[tr:tr + h, :] = jnp.where(last, 0.0, dz_after[...].astype(F32) * after[:, 0:D].astype(F32))
        for c in range(D // LANE):
            sl = slice(LANE * c, LANE * c + LANE)
            gl, vl = slice(D + LANE * c, D + LANE * c + LANE), slice(2 * D + LANE * c, 2 * D + LANE * c + LANE)
            for r0 in range(0, tr, sub):
                rows = slice(r0, r0 + sub)
                ca = _taps(cvb, w_ref, sl, K_A, h - (K_A - 1) + r0, sub)
                da_ref[rows, sl] = (dz_cur[rows, sl].astype(F32) * ca).astype(BF)
                dcv = _taps_rev(dcab, w_ref, sl, K_A, r0, sub)
                da_ref[rows, gl] = (dcv * cur[rows, vl].astype(F32)).astype(BF)
                da_ref[rows, vl] = (dcv * cur[rows, gl].astype(F32)).astype(BF)
                _tap_grads(dw_ref, dcab[rows, sl], cvb, sl, K_A, h - (K_A - 1) + r0, sub)

    return _pcall(
        body, grid=(n,),
        in_specs=[pl.BlockSpec((h, 3 * D), lambda i: (_halo_before(i, tr, h), 0)),
                  pl.BlockSpec((tr, 3 * D), lambda i: (i, 0)),
                  pl.BlockSpec((h, 3 * D), lambda i: (_halo_after(i, tr, h, s), 0)),
                  pl.BlockSpec((tr, D), lambda i: (i, 0)),
                  pl.BlockSpec((h, D), lambda i: (_halo_after(i, tr, h, s), 0)),
                  pl.BlockSpec((8, D), lambda i: (0, 0)), ANY],
        out_specs=[pl.BlockSpec((tr, 3 * D), lambda i: (i, 0)), pl.BlockSpec((8, D), lambda i: (0, 0))],
        out_shape=[SDS(dproj.shape, BF), SDS((8, D), F32)], input_output_aliases={6: 0},
        scratch_shapes=[pltpu.VMEM((h + tr, D), F32), pltpu.VMEM((tr + h, D), F32)],
        compiler_params=_cp(("arbitrary",)), name=name)(proj, proj, proj, dza, dza, cw, dproj)


_U_COL, _UG_COL = 3, 4


def _brb_conv_fwd(proj, cw, bias, *, name):
    s = proj.shape[0]
    tr, h = min(TR, s), H_L
    sub = min(SUB, tr)

    def body(u_cur, ug_cur, u_halo, ug_halo, w_ref, b_ref, cb_ref, glb, shifted):
        i = pl.program_id(0)
        glb[0:h, :] = jnp.where(i == 0, 0.0, u_halo[...].astype(F32) * _sigmoid(ug_halo[...].astype(F32)))
        glb[h:h + tr, :] = u_cur[...].astype(F32) * _sigmoid(ug_cur[...].astype(F32))
        for c in range(D // LANE):
            sl = slice(LANE * c, LANE * c + LANE)
            for r in range(1, 8):
                shifted[r] = glb[8 - r:8 - r + tr + 24, sl]
            for r0 in range(0, tr, sub):
                acc = None
                for k in range(K_B):
                    q, r = divmod(K_B - 1 - k, 8)
                    at = r0 - 8 * q
                    win = shifted[r, 24 + at:24 + at + sub, :] if r else glb[h + at:h + at + sub, sl]
                    term = w_ref[k:k + 1, sl] * win
                    acc = term if acc is None else acc + term
                cb_ref[r0:r0 + sub, sl] = (acc + b_ref[:, sl]).astype(BF)

    return _pcall(
        body, grid=(s // tr,),
        in_specs=[pl.BlockSpec((tr, D), lambda i: (i, _U_COL)), pl.BlockSpec((tr, D), lambda i: (i, _UG_COL)),
                  pl.BlockSpec((h, D), lambda i: (_halo_before(i, tr, h), _U_COL)),
                  pl.BlockSpec((h, D), lambda i: (_halo_before(i, tr, h), _UG_COL)),
                  pl.BlockSpec((32, D), lambda i: (0, 0)), pl.BlockSpec((1, D), lambda i: (0, 0))],
        out_specs=pl.BlockSpec((tr, D), lambda i: (i, 0)), out_shape=SDS((s, D), BF),
        scratch_shapes=[pltpu.VMEM((h + tr, D), F32), pltpu.VMEM((8, tr + 24, LANE), F32)],
        compiler_params=_cp(("parallel",)), name=name)(proj, proj, proj, proj, cw, bias)


def _brb_conv_bwd(proj, dcb, dq, cw, dproj, *, name):
    s = proj.shape[0]
    tr, h = min(TR, s), H_L
    sub = min(SUB, tr)
    n = s // tr

    def body(u_cur, ug_cur, d_cur, d_after, dq_ref, w_ref, dproj_in, db_ref, dw_ref, dcbb, shifted):
        del dproj_in
        i = pl.program_id(0)

        @pl.when(i == 0)
        def _():
            dw_ref[...] = jnp.zeros_like(dw_ref)

        db_ref[:, 2 * D:3 * D] = dq_ref[...]
        dcbb[0:tr, :] = d_cur[...].astype(F32)
        dcbb[tr:tr + h, :] = jnp.where(i == n - 1, 0.0, d_after[...].astype(F32))
        for c in range(D // LANE):
            sl = slice(LANE * c, LANE * c + LANE)
            for r in range(1, 8):
                shifted[r] = dcbb[r:r + tr + 24, sl]
            for r0 in range(0, tr, sub):
                u = u_cur[r0:r0 + sub, sl].astype(F32)
                sg = _sigmoid(ug_cur[r0:r0 + sub, sl].astype(F32))
                glu = u * sg
                dglu = None
                for k in range(K_B):
                    q, r = divmod(K_B - 1 - k, 8)
                    at = r0 + 8 * q
                    win = shifted[r, at:at + sub, :] if r else dcbb[at:at + sub, sl]
                    term = w_ref[k:k + 1, sl] * win
                    dglu = term if dglu is None else dglu + term
                    dw_ref[k:k + 1, sl] += jnp.sum(win * glu, axis=0, keepdims=True)
                db_ref[r0:r0 + sub, sl] = (dglu * sg).astype(BF)
                db_ref[r0:r0 + sub, D + LANE * c:D + LANE * c + LANE] = (dglu * u * sg * (1.0 - sg)).astype(BF)

    return _pcall(
        body, grid=(n,),
        in_specs=[pl.BlockSpec((tr, D), lambda i: (i, _U_COL)), pl.BlockSpec((tr, D), lambda i: (i, _UG_COL)),
                  pl.BlockSpec((tr, D), lambda i: (i, 0)),
                  pl.BlockSpec((h, D), lambda i: (_halo_after(i, tr, h, s), 0)),
                  pl.BlockSpec((tr, D), lambda i: (i, 0)),
                  pl.BlockSpec((32, D), lambda i: (0, 0)), ANY],
        out_specs=[pl.BlockSpec((tr, 3 * D), lambda i: (i, 1)), pl.BlockSpec((32, D), lambda i: (0, 0))],
        out_shape=[SDS(dproj.shape, BF), SDS((32, D), F32)], input_output_aliases={6: 0},
        scratch_shapes=[pltpu.VMEM((tr + h, D), F32), pltpu.VMEM((8, tr + 24, LANE), F32)],
        compiler_params=_cp(("arbitrary",)), name=name)(proj, proj, dcb, dcb, dq, cw, dproj)


def _ln_silu_fwd(cb, g, b, *, name):
    s = cb.shape[0]
    tm = min(TM, s)

    def body(cb_ref, g_ref, b_ref, sb_ref):
        z = cb_ref[...].astype(F32)
        zc = z - jnp.mean(z, axis=-1, keepdims=True)
        ln = (zc * lax.rsqrt(jnp.mean(zc * zc, axis=-1, keepdims=True) + NORM_EPS)) * g_ref[...] + b_ref[...]
        sb_ref[...] = (ln * _sigmoid(ln)).astype(BF)

    row = lambda i: (i, 0)
    vec = pl.BlockSpec((1, D), lambda i: (0, 0))
    return _pcall(
        body, grid=(s // tm,), in_specs=[pl.BlockSpec((tm, D), row), vec, vec],
        out_specs=pl.BlockSpec((tm, D), row), out_shape=SDS((s, D), BF),
        compiler_params=_cp(("parallel",)), name=name)(cb, g, b)


def _ln_silu_bwd(cb, dsb, g, b, *, name):
    s = cb.shape[0]
    tm = min(TM, s)

    def body(cb_ref, dsb_ref, g_ref, b_ref, dcb_ref, sums_ref):
        @pl.when(pl.program_id(0) == 0)
        def _():
            sums_ref[...] = jnp.zeros_like(sums_ref)

        z = cb_ref[...].astype(F32)
        zc = z - jnp.mean(z, axis=-1, keepdims=True)
        rstd = lax.rsqrt(jnp.mean(zc * zc, axis=-1, keepdims=True) + NORM_EPS)
        lnh = zc * rstd
        ln = lnh * g_ref[...] + b_ref[...]
        sg = _sigmoid(ln)
        dln = dsb_ref[...].astype(F32) * (sg * (1.0 + ln * (1.0 - sg)))
        sums_ref[0:1, :] += jnp.sum(dln * lnh, axis=0, keepdims=True)
        sums_ref[1:2, :] += jnp.sum(dln, axis=0, keepdims=True)
        dlnh = dln * g_ref[...]
        dz = rstd * (dlnh - jnp.mean(dlnh, axis=-1, keepdims=True)
                     - lnh * jnp.mean(dlnh * lnh, axis=-1, keepdims=True))
        sums_ref[2:3, :] += jnp.sum(dz, axis=0, keepdims=True)
        dcb_ref[...] = dz.astype(BF)

    row = lambda i: (i, 0)
    vec = pl.BlockSpec((1, D), lambda i: (0, 0))
    return _pcall(
        body, grid=(s // tm,), in_specs=[pl.BlockSpec((tm, D), row), pl.BlockSpec((tm, D), row), vec, vec],
        out_specs=[pl.BlockSpec((tm, D), row), pl.BlockSpec((8, D), lambda i: (0, 0))],
        out_shape=[SDS((s, D), BF), SDS((8, D), F32)],
        compiler_params=_cp(("arbitrary",)), name=name)(cb, dsb, g, b)


_Q_COL = 5 * D // HEAD


def _kv_prep(mem, g, wkv, *, name):
    m = mem.shape[0]

    def body(mem_ref, g_ref, w_ref, memn_ref, kv_ref):
        mv = mem_ref[...]
        memn = ((mv * _rms(mv)) * g_ref[...]).astype(BF)
        memn_ref[...] = memn
        for dev in range(N_DEV):
            kv_ref[:, dev * C_KV:(dev + 1) * C_KV] = _dot(memn, w_ref[dev]).astype(BF)

    return _pcall(body, out_shape=[SDS((m, D), BF), SDS((m, 2 * D), BF)],
                  compiler_params=_cp(), name=name)(mem, g, wkv)


def _softmax_rows(q, k):
    sc = _dot_nt(q, k) * (1.0 / (HEAD ** 0.5))
    e = jnp.exp(sc - jnp.max(sc, axis=-1, keepdims=True))
    return e / jnp.sum(e, axis=-1, keepdims=True)


def _attn_fwd(proj, kv, *, name):
    s, m = proj.shape[0], kv.shape[0]
    tm = min(TM, s)

    def body(q_ref, k_ref, v_ref, o_ref):
        p = _softmax_rows(q_ref[...], k_ref[...])
        o_ref[...] = _dot(p.astype(BF), v_ref[...]).astype(BF)

    return _pcall(
        body, grid=(s // tm, N_HEADS),
        in_specs=[pl.BlockSpec((tm, HEAD), lambda i, hd: (i, _Q_COL + hd)),
                  pl.BlockSpec((m, HEAD), lambda i, hd: (0, hd)),
                  pl.BlockSpec((m, HEAD), lambda i, hd: (0, N_HEADS + hd))],
        out_specs=pl.BlockSpec((tm, HEAD), lambda i, hd: (i, hd)), out_shape=SDS((s, D), BF),
        compiler_params=_cp(("parallel", "parallel")), name=name)(proj, kv, kv)


def _attn_bwd(proj, kv, do, *, name):
    s, m = proj.shape[0], kv.shape[0]
    tm = min(TM, s)

    def body(q_ref, k_ref, v_ref, do_ref, dq_ref, dk_ref, dv_ref):
        @pl.when(pl.program_id(1) == 0)
        def _():
            dk_ref[...] = jnp.zeros_like(dk_ref)
            dv_ref[...] = jnp.zeros_like(dv_ref)

        q, k, dov = q_ref[...], k_ref[...], do_ref[...]
        p = _softmax_rows(q, k)
        dp = _dot_nt(dov, v_ref[...])
        dv_ref[...] += _dot_tn(p.astype(BF), dov)
        ds = (p * (dp - jnp.sum(dp * p, axis=-1, keepdims=True)) * (1.0 / (HEAD ** 0.5))).astype(BF)
        dq_ref[...] = _dot(ds, k).astype(BF)
        dk_ref[...] += _dot_tn(ds, q)

    return _pcall(
        body, grid=(N_HEADS, s // tm),
        in_specs=[pl.BlockSpec((tm, HEAD), lambda hd, i: (i, _Q_COL + hd)),
                  pl.BlockSpec((m, HEAD), lambda hd, i: (0, hd)),
                  pl.BlockSpec((m, HEAD), lambda hd, i: (0, N_HEADS + hd)),
                  pl.BlockSpec((tm, HEAD), lambda hd, i: (i, hd))],
        out_specs=[pl.BlockSpec((tm, HEAD), lambda hd, i: (i, hd)),
                   pl.BlockSpec((m, HEAD), lambda hd, i: (0, hd)),
                   pl.BlockSpec((m, HEAD), lambda hd, i: (0, hd))],
        out_shape=[SDS((s, D), BF), SDS((m, D), F32), SDS((m, D), F32)],
        compiler_params=_cp(("parallel", "arbitrary")), name=name)(proj, kv, kv, do)


def _kv_bwd(mem, g, memn, dk, dv, wkv, *, name):
    def body(mem_ref, g_ref, memn_ref, dk_ref, dv_ref, w_ref, dw_ref, dg_ref):
        memn = memn_ref[...]
        dmemn = None
        for dev in range(N_DEV):
            d_ref, col = (dk_ref, dev) if dev < N_HEADS else (dv_ref, dev - N_HEADS)
            dslab = d_ref[:, col * C_KV:(col + 1) * C_KV].astype(BF)
            dw_ref[dev] = _dot_tn(memn, dslab).astype(BF)
            part = _dot_nt(dslab, w_ref[dev])
            dmemn = part if dmemn is None else dmemn + part
        mv = mem_ref[...]
        dg_ref[...] = jnp.zeros_like(dg_ref)
        dg_ref[0:1, :] = jnp.sum(dmemn * (mv * _rms(mv)), axis=0, keepdims=True)

    assert C_KV == HEAD
    return _pcall(body, out_shape=[SDS((N_DEV, D, C_KV), BF), SDS((8, D), F32)],
                  compiler_params=_cp(), name=name)(mem, g, memn, dk, dv, wkv)


_TM_MIX = 512


def _mix_out(x, za, sb, o, proj, w4, bg, g_next, *, name):
    s = x.shape[0]
    tm = min(_TM_MIX, s)

    def body(x_ref, za_ref, sb_ref, o_ref, pg_ref, w4_ref, bg_ref, gn_ref,
             ya_ref, yb_ref, yc_ref, mg_ref, x1_ref, h_ref):
        ys = (_dot(za_ref[...], w4_ref[0]), _dot(sb_ref[...], w4_ref[1]), _dot(o_ref[...], w4_ref[2]))
        merged = None
        for j, (y, y_ref) in enumerate(zip(ys, (ya_ref, yb_ref, yc_ref))):
            y_ref[...] = y.astype(BF)
            gate = _sigmoid(pg_ref[:, j * D:(j + 1) * D].astype(F32) + bg_ref[:, j * D:(j + 1) * D])
            merged = gate * y if merged is None else merged + gate * y
        mg = merged.astype(BF)
        mg_ref[...] = mg
        x1 = x_ref[...] + _dot(mg, w4_ref[3])
        x1_ref[...] = x1
        h_ref[...] = ((x1 * _rms(x1)) * gn_ref[...]).astype(BF)

    row = lambda i: (i, 0)
    act = pl.BlockSpec((tm, D), row)
    return _pcall(
        body, grid=(s // tm,),
        in_specs=[act, act, act, act, pl.BlockSpec((tm, 3 * D), lambda i: (i, 2)),
                  pl.BlockSpec((4, D, D), lambda i: (0, 0, 0), pipeline_mode=pl.Buffered(1)), pl.BlockSpec((1, 3 * D), lambda i: (0, 0)),
                  pl.BlockSpec((1, D), lambda i: (0, 0))],
        out_specs=[act] * 6,
        out_shape=[SDS((s, D), BF)] * 4 + [SDS((s, D), F32), SDS((s, D), BF)],
        compiler_params=_cp(("parallel",)), name=name)(x, za, sb, o, proj, w4, bg, g_next)


def _mix_bwd(dxb, ya, yb, yc, proj, w4, bg, *, name):
    s = dxb.shape[0]
    tm = min(_TM_MIX, s)

    def body(dx_ref, ya_ref, yb_ref, yc_ref, pg_ref, w4_ref, bg_ref,
             dya_ref, dyb_ref, dyc_ref, dza_ref, dsb_ref, do_ref, dgt_ref, dbg_ref):
        @pl.when(pl.program_id(0) == 0)
        def _():
            dbg_ref[...] = jnp.zeros_like(dbg_ref)

        dm = _dot_nt(dx_ref[...], w4_ref[3])
        for j, (y_ref, dy_ref, din_ref) in enumerate(zip((ya_ref, yb_ref, yc_ref), (dya_ref, dyb_ref, dyc_ref),
                                                         (dza_ref, dsb_ref, do_ref))):
            cols = slice(j * D, (j + 1) * D)
            gate = _sigmoid(pg_ref[:, cols].astype(F32) + bg_ref[:, cols])
            dy = (dm * gate).astype(BF)
            dy_ref[...] = dy
            din_ref[...] = _dot_nt(dy, w4_ref[j]).astype(BF)
            dpre = dm * y_ref[...].astype(F32) * gate * (1.0 - gate)
            dgt_ref[:, cols] = dpre.astype(BF)
            dbg_ref[0:1, cols] += jnp.sum(dpre, axis=0, keepdims=True)

    row = lambda i: (i, 0)
    act = pl.BlockSpec((tm, D), row)
    return _pcall(
        body, grid=(s // tm,),
        in_specs=[act, act, act, act, pl.BlockSpec((tm, 3 * D), lambda i: (i, 2)),
                  pl.BlockSpec((4, D, D), lambda i: (0, 0, 0), pipeline_mode=pl.Buffered(1)), pl.BlockSpec((1, 3 * D), lambda i: (0, 0))],
        out_specs=[act] * 6 + [pl.BlockSpec((tm, 3 * D), lambda i: (i, 2)),
                               pl.BlockSpec((8, 3 * D), lambda i: (0, 0))],
        out_shape=[SDS((s, D), BF)] * 6 + [SDS((s, 9 * D), BF), SDS((8, 3 * D), F32)],
        compiler_params=_cp(("arbitrary",)), name=name)(dxb, ya, yb, yc, proj, w4, bg)


_PAIR = 2 * C_UP_P


def _row_selector(sub, first_cols):
    rows = np.arange(len(first_cols) * sub)
    col = np.asarray(first_cols)[rows // sub] + rows % sub
    return jnp.asarray(np.arange(2 * sub)[None, :] == col[:, None], BF)


def _ffn_act(u2, cw, *, name):
    s = u2.shape[0]
    tr = min(TR, s)
    sub = min(SUB, tr)
    sel = _row_selector(sub, [sub - (K_F - 1 - k) for k in range(K_F)])

    def body(cur, prev, sel_ref, w_ref, act_ref, c2_ref, xb, win):
        i = pl.program_id(1)
        before = prev[...]
        xb[0:sub, :] = jnp.where(i == 0, jnp.zeros_like(before), before)
        xb[sub:sub + tr, :] = cur[...]
        for r0 in range(0, tr, sub):
            win[...] = _dot(sel_ref[...], xb[r0:r0 + 2 * sub, :])
            for c in range(C_UP_P // LANE):
                gl = slice(LANE * c, LANE * c + LANE)
                ul = slice(C_UP_P + LANE * c, C_UP_P + LANE * c + LANE)
                gt = sum(w_ref[k:k + 1, gl] * win[k * sub:(k + 1) * sub, gl] for k in range(K_F))
                up = sum(w_ref[k:k + 1, ul] * win[k * sub:(k + 1) * sub, ul] for k in range(K_F))
                c2_ref[r0:r0 + sub, gl] = gt.astype(BF)
                c2_ref[r0:r0 + sub, ul] = up.astype(BF)
                act_ref[r0:r0 + sub, gl] = (gt * _sigmoid(gt) * up).astype(BF)

    return _pcall(
        body, grid=(4, s // tr),
        in_specs=[pl.BlockSpec((tr, _PAIR), lambda p, i: (i, p)),
                  pl.BlockSpec((sub, _PAIR), lambda p, i: (_halo_before(i, tr, sub), p)),
                  pl.BlockSpec(sel.shape, lambda p, i: (0, 0)),
                  pl.BlockSpec((8, _PAIR), lambda p, i: (0, p))],
        out_specs=[pl.BlockSpec((tr, C_UP_P), lambda p, i: (i, p)), pl.BlockSpec((tr, _PAIR), lambda p, i: (i, p))],
        out_shape=[SDS((s, FF_P), BF), SDS((s, 2 * FF_P), BF)],
        scratch_shapes=[pltpu.VMEM((sub + tr, _PAIR), BF), pltpu.VMEM((K_F * sub, _PAIR), F32)],
        compiler_params=_cp(("parallel", "parallel")), name=name)(u2, u2, sel, cw)


def _ffn_bwd(u2, c2, dact, cw, *, name):
    s = u2.shape[0]
    tr, h = min(TR, s), H_S
    sub = min(SUB, tr)
    n = s // tr
    sel = _row_selector(sub, [K_F - 1 - k for k in range(K_F)])

    def body(u_cur, c_cur, c_after, da_cur, da_after, sel_ref, w_ref, du_ref, dw_ref, dcb, win):
        i = pl.program_id(1)
        last = i == n - 1

        @pl.when(i == 0)
        def _():
            dw_ref[...] = jnp.zeros_like(dw_ref)

        def conv_grad(gt, up, da):
            gt, up, da = gt.astype(F32), up.astype(F32), da.astype(F32)
            sg = _sigmoid(gt)
            return (da * up * (sg * (1.0 + gt * (1.0 - sg)))).astype(BF), (da * (gt * sg)).astype(BF)

        for c in range(C_UP_P // LANE):
            gl = slice(LANE * c, LANE * c + LANE)
            ul = slice(C_UP_P + LANE * c, C_UP_P + LANE * c + LANE)
            for r0 in range(0, tr, sub):
                rows = slice(r0, r0 + sub)
                dcb[rows, gl], dcb[rows, ul] = conv_grad(c_cur[rows, gl], c_cur[rows, ul], da_cur[rows, gl])
            dg, du_ = conv_grad(c_after[:, gl], c_after[:, ul], da_after[:, gl])
            dcb[tr:tr + h, gl] = jnp.where(last, jnp.zeros_like(dg), dg)
            dcb[tr:tr + h, ul] = jnp.where(last, jnp.zeros_like(du_), du_)
        dcb[tr + h:tr + sub, :] = jnp.zeros((sub - h, _PAIR), BF)
        for r0 in range(0, tr, sub):
            win[...] = _dot(sel_ref[...], dcb[r0:r0 + 2 * sub, :])
            for c in range(_PAIR // LANE):
                sl = slice(LANE * c, LANE * c + LANE)
                u = u_cur[r0:r0 + sub, sl].astype(F32)
                du = None
                for k in range(K_F):
                    wk = win[k * sub:(k + 1) * sub, sl]
                    term = w_ref[k:k + 1, sl] * wk
                    du = term if du is None else du + term
                    dw_ref[k:k + 1, sl] += jnp.sum(wk * u, axis=0, keepdims=True)
                du_ref[r0:r0 + sub, sl] = du.astype(BF)

    return _pcall(
        body, grid=(4, n),
        in_specs=[pl.BlockSpec((tr, _PAIR), lambda p, i: (i, p)),
                  pl.BlockSpec((tr, _PAIR), lambda p, i: (i, p)),
                  pl.BlockSpec((h, _PAIR), lambda p, i: (_halo_after(i, tr, h, s), p)),
                  pl.BlockSpec((tr, C_UP_P), lambda p, i: (i, p)),
                  pl.BlockSpec((h, C_UP_P), lambda p, i: (_halo_after(i, tr, h, s), p)),
                  pl.BlockSpec(sel.shape, lambda p, i: (0, 0)),
                  pl.BlockSpec((8, _PAIR), lambda p, i: (0, p))],
        out_specs=[pl.BlockSpec((tr, _PAIR), lambda p, i: (i, p)), pl.BlockSpec((8, _PAIR), lambda p, i: (0, p))],
        out_shape=[SDS((s, 2 * FF_P), BF), SDS((8, 2 * FF_P), F32)],
        scratch_shapes=[pltpu.VMEM((tr + sub, _PAIR), BF), pltpu.VMEM((K_F * sub, _PAIR), F32)],
        compiler_params=_cp(("parallel", "arbitrary")), name=name)(u2, c2, c2, dact, dact, sel, cw)


def _relations():
    x, y, c = lax.axis_index("x"), lax.axis_index("y"), lax.axis_index("c")
    out = []
    for r in range(1, N_DEV):
        rx, ry, rc = (r >> 2) & 1, (r >> 1) & 1, r & 1
        out.append((r, (x ^ rx, y ^ ry, c ^ rc)))
    return out


def _my_index():
    return 4 * lax.axis_index("x") + 2 * lax.axis_index("y") + lax.axis_index("c")


def _exchange(n_arrays, src_of, dst_of, refs):
    ssem, rsem, lsem = refs
    me = _my_index()
    local = []
    for a in range(n_arrays):
        loc = pltpu.make_async_copy(src_of(a, me), dst_of(a, me), lsem.at[a])
        loc.start()
        local.append(loc)

    def copy(a, r, peer, src_idx, dst_idx):
        return pltpu.make_async_remote_copy(
            src_ref=src_of(a, src_idx), dst_ref=dst_of(a, dst_idx), send_sem=ssem.at[a, r - 1],
            recv_sem=rsem.at[a, r - 1], device_id=peer, device_id_type=MESH)

    peers = [(r, peer, 4 * peer[0] + 2 * peer[1] + peer[2]) for r, peer in _relations()]
    for r, peer, p_idx in peers:
        for a in range(n_arrays):
            copy(a, r, peer, p_idx, me).start()
    for r, peer, p_idx in peers:
        for a in range(n_arrays):
            copy(a, r, peer, p_idx, me).wait_send()
            copy(a, r, peer, me, p_idx).wait_recv()
    for loc in local:
        loc.wait()


def _sem_scratch(n_arrays):
    return [pltpu.SemaphoreType.DMA((n_arrays, N_DEV - 1)), pltpu.SemaphoreType.DMA((n_arrays, N_DEV - 1)),
            pltpu.SemaphoreType.DMA((n_arrays,))]


def _slab(kind, ref, idx):
    if kind == "win":
        return ref.at[:, pl.ds(pl.multiple_of(idx * C_IN, LANE), C_IN)]
    if kind == "wup":
        return ref.at[:, pl.ds(pl.multiple_of(_up_slot(idx) * C_UP_P, LANE), C_UP_P)]
    if kind == "wkv":
        return ref.at[idx]
    if kind == "w4":
        return ref.at[:, pl.ds(pl.multiple_of(idx * R_O, 16), R_O), :]
    if kind == "wdn":
        return ref.at[pl.ds(pl.multiple_of(_dn_row(idx), 16), R_DN), :]
    assert kind == "cv"
    return ref.at[idx]


_WHOLE = {"win": ((D, 9 * D), BF), "wup": ((D, 2 * FF_P), BF), "wkv": ((N_DEV, D, C_KV), BF),
          "w4": ((4, D, D), BF), "wdn": ((FF_P, D), BF)}
_SHARD = {"win": (D, C_IN), "wup": (D, C_UP_P), "wkv": (D, C_KV), "w4": (4, R_O, D), "wdn": (R_DN, D)}
HBM_SPEC = pl.BlockSpec(memory_space=pltpu.HBM)
SEM_SPEC = pl.BlockSpec(memory_space=pltpu.SEMAPHORE)
_DATAFLOW = pltpu.SideEffectType.DATAFLOW_SIDE_EFFECTING


def _scatter_maps(kinds):
    return ((lambda srcs, lands, a, idx: _slab(kinds[a], srcs[a], idx)),
            (lambda lands, a, idx: lands[a].at[idx]))


_SLOTTED = ("wkv", "cv")


def _own_slab_blocks(kind, shard_shape):
    if kind in ("win", "wup"):
        rows, slot = 256, (_up_slot if kind == "wup" else (lambda m: m))
        return (shard_shape[0] // rows, (rows, shard_shape[1]), (lambda i, me: (i, slot(me[0]))),
                (lambda i, me: (i, 0)), (lambda i, me: (me[0], i, 0)))
    if kind == "w4":
        return (1, shard_shape, (lambda i, me: (0, me[0], 0)), (lambda i, me: (0, 0, 0)),
                (lambda i, me: (me[0], 0, 0, 0)))
    if kind == "wdn":
        rows = 32
        return (R_DN // rows, (rows, D), (lambda i, me: (_dn_row(me[0]) // rows + i, 0)), (lambda i, me: (i, 0)),
                (lambda i, me: (me[0], i, 0)))
    assert kind in _SLOTTED
    rows = min(256, shard_shape[0])
    return (shard_shape[0] // rows, (rows, shard_shape[1]), (lambda i, me: (me[0], i, 0)),
            (lambda i, me: (i, 0)), (lambda i, me: (me[0], i, 0)))


def _place_own(kind, src, out_sds, gather, me_arr, *, name):
    shard_shape = src.shape if gather else out_sds.shape[1:]
    steps, blk, whole_idx, shard_idx, staging_idx = _own_slab_blocks(kind, shard_shape)
    slotted = kind in _SLOTTED
    whole_spec = pl.BlockSpec(((None,) if slotted else ()) + tuple(blk), whole_idx)
    if gather:
        in_spec, out_spec = pl.BlockSpec(tuple(blk), shard_idx), whole_spec
    else:
        in_spec, out_spec = whole_spec, pl.BlockSpec((None,) + tuple(blk), staging_idx)
    zero_init = gather and kind == "wdn"

    def body(me_ref, src_ref, *rest):
        rest[-1][...] = src_ref[...].astype(rest[-1].dtype)

    operands = (me_arr, src) + ((jnp.zeros(out_sds.shape, out_sds.dtype),) if zero_init else ())
    return _pcall(
        body,
        grid_spec=pltpu.PrefetchScalarGridSpec(
            num_scalar_prefetch=1, grid=(steps,), in_specs=[in_spec] + ([ANY] if zero_init else []),
            out_specs=out_spec),
        out_shape=out_sds, input_output_aliases={2: 0} if zero_init else {},
        compiler_params=_cp(("arbitrary",)), name=name)(*operands)


def _peer_copies(n, src_of, dst_of, src_r, land_r, ssem, rsem):
    me = _my_index()
    out = []
    for r, peer in _relations():
        p_idx = 4 * peer[0] + 2 * peer[1] + peer[2]
        for a in range(n):
            def copy(src_idx, dst_idx, a=a, r=r, peer=peer):
                sem = a * (N_DEV - 1) + r - 1
                return pltpu.make_async_remote_copy(
                    src_ref=src_of(src_r, land_r, a, src_idx), dst_ref=dst_of(land_r, a, dst_idx),
                    send_sem=ssem.at[sem], recv_sem=rsem.at[sem], device_id=peer, device_id_type=MESH)
            out.append((functools.partial(copy, p_idx, me), functools.partial(copy, me, p_idx)))
    return out


def _exchange_start(srcs, lands, maps, after, *, name):
    n, ns = len(lands), len(srcs)
    src_of, dst_of = maps

    def body(*refs):
        src_r, land_r = refs[:ns], refs[ns:ns + n]
        ssem, rsem, token = refs[ns + n + 1], refs[ns + n + 2], refs[-1]
        for send, _ in _peer_copies(n, src_of, dst_of, src_r, land_r, ssem, rsem):
            send().start()
        token[...] = jnp.zeros_like(token)

    flight = list(srcs) + list(lands)
    outs = pl.pallas_call(
        body, name=name,
        out_shape=(pltpu.SemaphoreType.DMA((n * (N_DEV - 1),)), pltpu.SemaphoreType.DMA((n * (N_DEV - 1),)),
                   *[pltpu.HBM(a.shape, a.dtype) for a in flight], SDS((8, LANE), F32)),
        in_specs=[HBM_SPEC] * (ns + n) + [ANY],
        out_specs=(SEM_SPEC, SEM_SPEC, *[HBM_SPEC] * (ns + n), pl.BlockSpec(memory_space=pltpu.VMEM)),
        input_output_aliases={i: 2 + i for i in range(ns + n)},
        compiler_params=pltpu.CompilerParams(has_side_effects=_DATAFLOW),
    )(*[pltpu.with_memory_space_constraint(a, pltpu.HBM) for a in flight], after)
    return (outs[0], outs[1], list(outs[2:2 + ns + n]), ns), outs[-1]


def _exchange_wait(handle, maps, after, *, name):
    ssem, rsem, flight, ns = handle
    n = len(flight) - ns
    src_of, dst_of = maps

    def body(*refs):
        src_r, land_r, ssem_r, rsem_r = refs[:ns], refs[ns:ns + n], refs[ns + n], refs[ns + n + 1]
        for send, arrival in _peer_copies(n, src_of, dst_of, src_r, land_r, ssem_r, rsem_r):
            send().wait_send()
            arrival().wait_recv()

    outs = pl.pallas_call(
        body, name=name, out_shape=[pltpu.HBM(a.shape, a.dtype) for a in flight],
        in_specs=[HBM_SPEC] * (ns + n) + [SEM_SPEC, SEM_SPEC, ANY], out_specs=[HBM_SPEC] * (ns + n),
        input_output_aliases={i: i for i in range(ns + n)},
        compiler_params=pltpu.CompilerParams(has_side_effects=_DATAFLOW),
    )(*flight, ssem, rsem, after)
    return list(outs[ns:])


_SIBLING = 1
_ICI = (2, 4, 6)


def _rel_peer(r):
    x, y, c = lax.axis_index("x"), lax.axis_index("y"), lax.axis_index("c")
    peer = (x ^ ((r >> 2) & 1), y ^ ((r >> 1) & 1), c ^ (r & 1))
    return peer, 4 * peer[0] + 2 * peer[1] + peer[2]


def _rcopy(ref, ssem, rsem, peer):
    return pltpu.make_async_remote_copy(src_ref=ref, dst_ref=ref, send_sem=ssem, recv_sem=rsem, device_id=peer,
                                        device_id_type=MESH)


def _gather2_start(lands, kinds, after, *, name):
    n = len(lands)

    def body(*refs):
        land_r, (send1, recv_sib, recv_ici), token = refs[:n], refs[n + 1:n + 4], refs[-1]
        me = _my_index()
        for a in range(n):
            own = _slab(kinds[a], land_r[a], me)
            for j, r in enumerate((_SIBLING,) + _ICI):
                rsem = recv_sib.at[a] if r == _SIBLING else recv_ici.at[3 * a + j - 1]
                _rcopy(own, send1.at[4 * a + j], rsem, _rel_peer(r)[0]).start()
        token[...] = jnp.zeros_like(token)

    sems = [pltpu.SemaphoreType.DMA((4 * n,)), pltpu.SemaphoreType.DMA((n,)), pltpu.SemaphoreType.DMA((3 * n,))]
    outs = pl.pallas_call(
        body, name=name, out_shape=(*sems, *[pltpu.HBM(a.shape, a.dtype) for a in lands], SDS((8, LANE), F32)),
        in_specs=[HBM_SPEC] * n + [ANY],
        out_specs=(SEM_SPEC,) * 3 + (HBM_SPEC,) * n + (pl.BlockSpec(memory_space=pltpu.VMEM),),
        input_output_aliases={i: 3 + i for i in range(n)},
        compiler_params=pltpu.CompilerParams(has_side_effects=_DATAFLOW),
    )(*[pltpu.with_memory_space_constraint(a, pltpu.HBM) for a in lands], after)
    return dict(send1=outs[0], recv_sib=outs[1], recv_ici=outs[2], lands=list(outs[3:3 + n])), outs[-1]


def _gather2_forward(handle, kinds, after, *, name):
    lands = handle["lands"]
    n = len(lands)

    def body(*refs):
        land_r, recv_ici, (fwd_send, fwd_recv), token = refs[:n], refs[n], refs[n + 2:n + 4], refs[-1]
        sibling = _rel_peer(_SIBLING)[0]
        for a in range(n):
            for j, r in enumerate(_ICI):
                got = _slab(kinds[a], land_r[a], _rel_peer(r)[1])
                _rcopy(got, fwd_send.at[3 * a + j], recv_ici.at[3 * a + j], sibling).wait_recv()
                _rcopy(got, fwd_send.at[3 * a + j], fwd_recv.at[3 * a + j], sibling).start()
        token[...] = jnp.zeros_like(token)

    sems = [pltpu.SemaphoreType.DMA((3 * n,)), pltpu.SemaphoreType.DMA((3 * n,))]
    outs = pl.pallas_call(
        body, name=name, out_shape=(*sems, *[pltpu.HBM(a.shape, a.dtype) for a in lands], SDS((8, LANE), F32)),
        in_specs=[HBM_SPEC] * n + [SEM_SPEC, ANY],
        out_specs=(SEM_SPEC,) * 2 + (HBM_SPEC,) * n + (pl.BlockSpec(memory_space=pltpu.VMEM),),
        input_output_aliases={i: 2 + i for i in range(n)},
        compiler_params=pltpu.CompilerParams(has_side_effects=_DATAFLOW),
    )(*lands, handle["recv_ici"], after)
    return dict(handle, fwd_send=outs[0], fwd_recv=outs[1], lands=list(outs[2:2 + n])), outs[-1]


def _gather2_wait(handle, kinds, after, *, name):
    lands = handle["lands"]
    n = len(lands)

    def body(*refs):
        land_r, (send1, recv_sib, fwd_send, fwd_recv) = refs[:n], refs[n:n + 4]
        me = _my_index()
        sibling, sib_idx = _rel_peer(_SIBLING)
        for a in range(n):
            own = _slab(kinds[a], land_r[a], me)
            for j, r in enumerate((_SIBLING,) + _ICI):
                _rcopy(own, send1.at[4 * a + j], recv_sib.at[a], _rel_peer(r)[0]).wait_send()
            theirs = _slab(kinds[a], land_r[a], sib_idx)
            _rcopy(theirs, send1.at[4 * a], recv_sib.at[a], sibling).wait_recv()
            for j, r in enumerate(_ICI):
                passed_on = _slab(kinds[a], land_r[a], _rel_peer(r)[1])
                _rcopy(passed_on, fwd_send.at[3 * a + j], fwd_recv.at[3 * a + j], sibling).wait_send()
                arrived = _slab(kinds[a], land_r[a], _rel_peer(r ^ _SIBLING)[1])
                _rcopy(arrived, fwd_send.at[3 * a + j], fwd_recv.at[3 * a + j], sibling).wait_recv()

    outs = pl.pallas_call(
        body, name=name, out_shape=[pltpu.HBM(a.shape, a.dtype) for a in lands],
        in_specs=[HBM_SPEC] * n + [SEM_SPEC] * 4 + [ANY], out_specs=[HBM_SPEC] * n,
        input_output_aliases={i: i for i in range(n)},
        compiler_params=pltpu.CompilerParams(has_side_effects=_DATAFLOW),
    )(*lands, handle["send1"], handle["recv_sib"], handle["fwd_send"], handle["fwd_recv"], after)
    return list(outs)


def _allreduce_small(pack, *, name):
    rows = pack.shape[0]

    def body(p_ref, out_ref, gath, ssem, rsem, lsem):
        _exchange(1, lambda a, idx: p_ref, lambda a, idx: gath.at[idx], (ssem, rsem, lsem))
        total = gath[0]
        for d in range(1, N_DEV):
            total = total + gath[d]
        out_ref[...] = total

    vm = pl.BlockSpec(memory_space=pltpu.VMEM)
    return _pcall(
        body, in_specs=[vm], out_specs=vm, out_shape=SDS(pack.shape, F32),
        scratch_shapes=[pltpu.VMEM((N_DEV, rows, pack.shape[1]), F32)] + _sem_scratch(1),
        compiler_params=_cp(has_side_effects=True), name=name)(pack)


def _adam(g, w, m, v):
    nm = ADAM_B1 * m + (1.0 - ADAM_B1) * g
    nv = ADAM_B2 * v + (1.0 - ADAM_B2) * (g * g)
    m_hat = nm / (1.0 - ADAM_B1 ** ADAM_STEP)
    v_hat = nv / (1.0 - ADAM_B2 ** ADAM_STEP)
    return -ADAM_LR * (m_hat / (jnp.sqrt(v_hat) + ADAM_EPS) + ADAM_WD * w), nm, nv


def _adamw_staged(st0, st1, w, m, v, *, name):
    _, rows, cols = w.shape
    st_cols = st0.shape[2]
    tr = max(t for t in range(16, 129, 16) if rows % t == 0)
    nr = rows // tr

    def body(s0_ref, s1_ref, w_ref, m_ref, v_ref, g_ref, d_ref, nm_ref, nv_ref):
        for layer, s_ref in enumerate((s0_ref, s1_ref)):
            @pl.when(pl.program_id(0) == layer)
            def _(s_ref=s_ref):
                total = s_ref[0, :, 0:cols].astype(F32)
                for dev in range(1, N_DEV):
                    total = total + s_ref[dev, :, 0:cols].astype(F32)
                g_ref[0] = total

        d_ref[0], nm_ref[0], nv_ref[0] = _adam(g_ref[0], w_ref[0], m_ref[0], v_ref[0])

    st_spec = lambda layer: pl.BlockSpec(
        (N_DEV, tr, st_cols), lambda l, i: (0, jnp.where(l == layer, i, (nr - 1) * (1 - layer)), 0))
    par = pl.BlockSpec((1, tr, cols), lambda l, i: (l, i, 0))
    return _pcall(
        body, grid=(DEPTH, nr), in_specs=[st_spec(0), st_spec(1), par, par, par], out_specs=[par] * 4,
        out_shape=[SDS(w.shape, F32)] * 4,
        compiler_params=_cp(("arbitrary", "arbitrary")), name=name)(st0, st1, w, m, v)


def _adamw_small(g, w, m, v, *, name):
    def body(g_ref, w_ref, m_ref, v_ref, d_ref, nm_ref, nv_ref):
        d_ref[...], nm_ref[...], nv_ref[...] = _adam(g_ref[...], w_ref[...], m_ref[...], v_ref[...])

    return _pcall(body, out_shape=[SDS(g.shape, F32)] * 3, compiler_params=_cp(), name=name)(g, w, m, v)


def _pack_rows(arrays):
    flat = jnp.concatenate([a.reshape(-1).astype(F32) for a in arrays])
    rows = -(-flat.shape[0] // (8 * D)) * 8
    return jnp.pad(flat, (0, rows * D - flat.shape[0])).reshape(rows, D)


def _unpack_rows(pack, like):
    flat = pack.reshape(-1)
    out, at = [], 0
    for a in like:
        out.append(flat[at:at + a.size].reshape(a.shape))
        at += a.size
    return out


def _layer_fwd(x, h, mem, win, rest_of_weights, after_up, small, g_next, tag):
    proj = _mm(h, win, tm=1024, tn=1536, name=f"proj_{tag}")
    wup, wkv, w4, wdn, cw_a, cw_b, cw_f = rest_of_weights(proj)
    za = _bra_fwd(proj, cw_a, name=f"bra_fwd_{tag}")
    cb = _brb_conv_fwd(proj, cw_b, small["conv_b_bias"], name=f"brb_conv_fwd_{tag}")
    sb = _ln_silu_fwd(cb, small["ln_b_g"], small["ln_b_b"], name=f"ln_silu_fwd_{tag}")
    memn, kv = _kv_prep(mem, small["norm_mem_g"], wkv, name=f"kv_prep_{tag}")
    o = _attn_fwd(proj, kv, name=f"attn_fwd_{tag}")
    ya, yb, yc, mg, x1, h2 = _mix_out(x, za, sb, o, proj, w4, small["b_gate"], small["norm_ffn_g"],
                                      name=f"mix_out_{tag}")
    u2 = _mm(h2, wup, tm=1024, tn=1536, name=f"up_{tag}")
    token = after_up(u2)
    act, c2 = _ffn_act(u2, cw_f if token is None else _behind(cw_f, token), name=f"ffn_act_{tag}")
    x2, h_next = _mm_res_norm(act, wdn, x1, g_next, name=f"down_{tag}")
    saved = dict(x=x, h=h, proj=proj, za=za, cb=cb, sb=sb, memn=memn, kv=kv, o=o, ya=ya, yb=yb, yc=yc,
                 mg=mg, x1=x1, h2=h2, u2=u2, c2=c2, act=act)
    return x2, h_next, (win, wup, wkv, w4, wdn, cw_a, cw_b, cw_f), saved


def _behind(operand, token):
    return operand + token[0:1, 0:1]


def _layer_bwd(dx2, dx2b, mem, wts, small, sv, start, tag):
    win, wup, wkv, w4, wdn, cw_a, cw_b, cw_f = wts
    dact = _mm(dx2b, wdn, tb=True, tm=1024, tn=768, name=f"d_act_{tag}")
    dwdn = _mm(sv["act"], dx2b, ta=True, tm=768, tn=1024, name=f"dw_down_{tag}")
    du2, dcw_f = _ffn_bwd(sv["u2"], sv["c2"], dact, cw_f, name=f"ffn_bwd_{tag}")
    dwup = _mm(sv["h2"], du2, ta=True, tm=1024, tn=768, name=f"dw_up_{tag}")
    token = start(("wdn", "wup"), (dwdn, dwup), f"ffn_{tag}")
    dx1, dx1b, dg_ffn = _mm_nt_normbwd(du2, wup, sv["x1"], dx2, _behind(small["norm_ffn_g"], token),
                                       name=f"d_h2_{tag}")

    dya, dyb, dyc, dza, dsb, do, dproj, dbg = _mix_bwd(dx1b, sv["ya"], sv["yb"], sv["yc"], sv["proj"], w4,
                                                      small["b_gate"], name=f"mix_bwd_{tag}")
    dw4 = jnp.stack([
        _mm(a, b, ta=True, tm=1024, tn=512, name=f"dw_{nm}_{tag}")
        for nm, a, b in (("a_out", sv["za"], dya), ("b_out", sv["sb"], dyb), ("att_out", sv["o"], dyc),
                         ("o", sv["mg"], dx1b))])
    dq, dk, dv = _attn_bwd(sv["proj"], sv["kv"], do, name=f"attn_bwd_{tag}")
    dwkv, dg_mem = _kv_bwd(mem, small["norm_mem_g"], sv["memn"], dk, dv, wkv, name=f"kv_bwd_{tag}")
    token = start(("w4", "wkv"), (dw4, dwkv), f"mix_{tag}")
    dproj, dcw_a = _bra_bwd(sv["proj"], dza, _behind(cw_a, token), dproj, name=f"bra_bwd_{tag}")
    dcb, ln_sums = _ln_silu_bwd(sv["cb"], dsb, small["ln_b_g"], small["ln_b_b"], name=f"ln_silu_bwd_{tag}")
    dproj, dcw_b = _brb_conv_bwd(sv["proj"], dcb, dq, cw_b, dproj, name=f"brb_conv_bwd_{tag}")
    dwin = _mm(sv["h"], dproj, ta=True, tm=1024, tn=768, name=f"dw_in_{tag}")
    token = start(("win",), (dwin,), f"in_{tag}")
    dx, dxb, dg_mix = _mm_nt_normbwd(dproj, win, sv["x"], dx1, _behind(small["norm_mix_g"], token),
                                     tk=4608, name=f"d_h_{tag}")

    small_grads = [dg_mix[0:1], dg_mem[0:1], dbg[0:1].reshape(3, D), ln_sums[2:3], ln_sums[0:1], ln_sums[1:2],
                   dg_ffn[0:1], dcw_a[0:K_A], dcw_b[0:K_B], dcw_f[0:K_F].reshape(K_F * 2 * FF_P // D, D)]
    return dx, dxb, small_grads, token


_SMALL_ROWS = (1, 1, 3, 1, 1, 1, 1, K_A, K_B, K_F * 2 * FF_P // D)
_CV_ROWS = 48


def kernel(x, mem, norm_mix_g, norm_mem_g, w_in, b_gate, conv_a_w, w_a_out, conv_b_w, conv_b_bias, ln_b_g, ln_b_b, w_b_out, w_kv, w_att_out, w_o, norm_ffn_g, w_up, conv_ffn_w, w_down, norm_final_g, loss_target, m_norm_mix_g, m_norm_mem_g, m_w_in, m_b_gate, m_conv_a_w, m_w_a_out, m_conv_b_w, m_conv_b_bias, m_ln_b_g, m_ln_b_b, m_w_b_out, m_w_kv, m_w_att_out, m_w_o, m_norm_ffn_g, m_w_up, m_conv_ffn_w, m_w_down, m_norm_final_g, v_norm_mix_g, v_norm_mem_g, v_w_in, v_b_gate, v_conv_a_w, v_w_a_out, v_conv_b_w, v_conv_b_bias, v_ln_b_g, v_ln_b_b, v_w_b_out, v_w_kv, v_w_att_out, v_w_o, v_norm_ffn_g, v_w_up, v_conv_ffn_w, v_w_down, v_norm_final_g):
    me = _my_index()
    me_arr = me.astype(jnp.int32).reshape(1)
    x0, mem0, tgt = x.reshape(x.shape[1:]), mem.reshape(mem.shape[1:]), loss_target.reshape(x.shape[1:])
    up_pad = ((0, 0), (0, 0), (0, C_UP_P - C_UP))

    ag_groups = (("win",), ("wup", "wkv", "w4", "wdn", "cv"))
    kinds = ag_groups[0] + ag_groups[1]
    smalls, ag_handles = [], []
    token = jnp.zeros((8, LANE), F32)
    for l in range(DEPTH):
        cv = jnp.zeros((_CV_ROWS, C_UP_P), F32)
        cv = cv.at[0:K_F, 0:C_UP].set(conv_ffn_w[l]).at[3:3 + K_A, 0:R_O].set(conv_a_w[l])
        cv = cv.at[8:8 + K_B, 0:R_O].set(conv_b_w[l])
        shards = dict(
            win=w_in[l], wup=jnp.pad(w_up[l], up_pad[1:]), wkv=w_kv[l],
            w4=jnp.stack([w_a_out[l], w_b_out[l], w_att_out[l], w_o[l]]), wdn=w_down[l], cv=cv)
        whole = dict({k: SDS(*_WHOLE[k]) for k in kinds[:-1]}, cv=SDS((N_DEV,) + cv.shape, F32))
        lands = {k: _place_own(k, shards[k], whole[k], True, me_arr, name=f"ag_own_{k}_l{l}") for k in kinds}
        per_layer = []
        for g, grp in enumerate(ag_groups):
            handle, token = _gather2_start([lands[k] for k in grp], grp, token, name=f"ag_start_l{l}_g{g}")
            per_layer.append(handle)
        ag_handles.append(per_layer)
        smalls.append(dict(
            norm_mix_g=norm_mix_g[l][None], norm_mem_g=norm_mem_g[l][None], b_gate=b_gate[l][None],
            conv_b_bias=conv_b_bias[l][None], ln_b_g=ln_b_g[l][None], ln_b_b=ln_b_b[l][None],
            norm_ffn_g=norm_ffn_g[l][None]))

    def forward_group(l, g, after):
        ag_handles[l][g], tok = _gather2_forward(ag_handles[l][g], ag_groups[g], after, name=f"ag_forward_l{l}_g{g}")
        return tok

    def rest_of_weights(l):
        def wait(after):
            if l == 0:
                after = forward_group(0, 1, after)
            wup, wkv, w4, wdn, cvg = _gather2_wait(ag_handles[l][1], ag_groups[1], after, name=f"ag_wait_l{l}_g1")
            cw_f = jnp.stack([cvg[d, 0:K_F, :] for d in UP_ORDER], axis=1).reshape(K_F, 2 * FF_P)
            cw_a = cvg[:, 3:3 + K_A, 0:R_O].transpose(1, 0, 2).reshape(K_A, D)
            cw_b = cvg[:, 8:8 + K_B, 0:R_O].transpose(1, 0, 2).reshape(K_B, D)
            return (wup, wkv, w4, wdn, jnp.pad(cw_a, ((0, 8 - K_A), (0, 0))),
                    jnp.pad(cw_b, ((0, 32 - K_B), (0, 0))), jnp.pad(cw_f, ((0, 8 - K_F), (0, 0))))
        return wait

    wts, saved = [], []
    xs = x0
    h = _rms_fwd(xs, smalls[0]["norm_mix_g"], name="rms_fwd")
    behind = forward_group(0, 0, token)

    def next_layer_forwarding(l):
        def hook(after):
            if l + 1 == DEPTH:
                return None
            return forward_group(l + 1, 1, forward_group(l + 1, 0, after))
        return hook

    for l in range(DEPTH):
        g_next = smalls[l + 1]["norm_mix_g"] if l + 1 < DEPTH else norm_final_g[None]
        (win,) = _gather2_wait(ag_handles[l][0], ag_groups[0], behind, name=f"ag_wait_l{l}_g0")
        xs, h, w_l, sv = _layer_fwd(xs, h, mem0, win, rest_of_weights(l), next_layer_forwarding(l), smalls[l],
                                    g_next, f"l{l}")
        behind = h
        wts.append(w_l)
        saved.append(sv)
    dx, dxb, head_sums = _loss_head(xs, tgt, norm_final_g[None], name="loss_head")

    rs_handles = []
    small_grads = [None] * DEPTH

    def start_scatter(grp, arrays, name):
        maps = _scatter_maps(grp)
        lands = [_place_own(k, a, SDS((N_DEV,) + _SHARD[k], BF), False, me_arr, name=f"rs_own_{k}_{name}")
                 for k, a in zip(grp, arrays)]
        handle, tok = _exchange_start(list(arrays), lands, maps, rs_handles[-1][2] if rs_handles else head_sums,
                                      name=f"rs_start_{name}")
        rs_handles.append((grp, handle, tok, name))
        return tok

    for l in reversed(range(DEPTH)):
        dx, dxb, small_grads[l], token = _layer_bwd(dx, dxb, mem0, wts[l], smalls[l], saved[l], start_scatter,
                                                    f"l{l}")

    staged = [dict() for _ in range(DEPTH)]
    for grp, handle, _, name in rs_handles[:-1]:
        staged[int(name[-1])].update(zip(grp, _exchange_wait(handle, _scatter_maps(grp), dx, name=f"rs_wait_{name}")))

    pack = jnp.concatenate(small_grads[0] + small_grads[1] + [head_sums[1:2], head_sums[0:1]], axis=0)
    pack = jnp.pad(pack, ((0, -pack.shape[0] % 8), (0, 0)))
    total = _allreduce_small(pack, name="allreduce_small")
    per_layer = sum(_SMALL_ROWS)
    parts = []
    for l in range(DEPTH):
        at, one = l * per_layer, []
        for rows in _SMALL_ROWS:
            one.append(total[at:at + rows])
            at += rows
        parts.append(one)
    g_final = total[DEPTH * per_layer]
    loss = 0.5 / D * jnp.sum(total[DEPTH * per_layer + 1])

    def both(i):
        return jnp.stack([parts[0][i], parts[1][i]])

    g_norm_mix, g_norm_mem = both(0)[:, 0], both(1)[:, 0]
    g_b_gate = both(2).reshape(DEPTH, 3 * D)
    g_cbias, g_lng, g_lnb, g_norm_ffn = both(3)[:, 0], both(4)[:, 0], both(5)[:, 0], both(6)[:, 0]
    g_conv_a = lax.dynamic_slice_in_dim(both(7), me * R_O, R_O, axis=2)
    g_conv_b = lax.dynamic_slice_in_dim(both(8), me * R_O, R_O, axis=2)
    g_conv_f = lax.dynamic_slice_in_dim(both(9).reshape(DEPTH, K_F, 2 * FF_P), _up_slot(me) * C_UP_P, C_UP, axis=2)

    small_g = [g_norm_mix, g_norm_mem, g_b_gate, g_conv_a, g_conv_b, g_cbias, g_lng, g_lnb, g_norm_ffn, g_conv_f,
               g_final]
    small_w = [norm_mix_g, norm_mem_g, b_gate, conv_a_w, conv_b_w, conv_b_bias, ln_b_g, ln_b_b, norm_ffn_g,
               conv_ffn_w, norm_final_g]
    small_m = [m_norm_mix_g, m_norm_mem_g, m_b_gate, m_conv_a_w, m_conv_b_w, m_conv_b_bias, m_ln_b_g, m_ln_b_b,
               m_norm_ffn_g, m_conv_ffn_w, m_norm_final_g]
    small_v = [v_norm_mix_g, v_norm_mem_g, v_b_gate, v_conv_a_w, v_conv_b_w, v_conv_b_bias, v_ln_b_g, v_ln_b_b,
               v_norm_ffn_g, v_conv_ffn_w, v_norm_final_g]
    upd = _adamw_small(_pack_rows(small_g), _pack_rows(small_w), _pack_rows(small_m), _pack_rows(small_v),
                       name="adamw_small")
    s_d, s_m, s_v = (_unpack_rows(p, small_w) for p in upd)
    (d_norm_mix, d_norm_mem, d_b_gate, d_conv_a, d_conv_b, d_cbias, d_lng, d_lnb, d_norm_ffn, d_conv_f,
     d_final) = s_d
    (nm_norm_mix, nm_norm_mem, nm_b_gate, nm_conv_a, nm_conv_b, nm_cbias, nm_lng, nm_lnb, nm_norm_ffn, nm_conv_f,
     nm_final) = s_m
    (nv_norm_mix, nv_norm_mem, nv_b_gate, nv_conv_a, nv_conv_b, nv_cbias, nv_lng, nv_lnb, nv_norm_ffn, nv_conv_f,
     nv_final) = s_v

    def big_update(kind, w, m, v, name):
        return _adamw_staged(staged[0][kind], staged[1][kind], w, m, v, name=name)

    r_up = big_update("wup", w_up, m_w_up, v_w_up, "adamw_w_up")
    r_kv = big_update("wkv", w_kv, m_w_kv, v_w_kv, "adamw_w_kv")
    r_dn = big_update("wdn", w_down, m_w_down, v_w_down, "adamw_w_down")

    def four(a, b, c, d_):
        return jnp.stack([a, b, c, d_], axis=1).reshape(DEPTH, 4 * R_O, D)

    r_4 = _adamw_staged(
        staged[0]["w4"].reshape(N_DEV, 4 * R_O, D), staged[1]["w4"].reshape(N_DEV, 4 * R_O, D),
        four(w_a_out, w_b_out, w_att_out, w_o), four(m_w_a_out, m_w_b_out, m_w_att_out, m_w_o),
        four(v_w_a_out, v_w_b_out, v_w_att_out, v_w_o), name="adamw_w_out")
    grp, handle, _, name = rs_handles[-1]
    staged[0].update(zip(grp, _exchange_wait(handle, _scatter_maps(grp), r_4[0], name=f"rs_wait_{name}")))
    r_in = big_update("win", w_in, m_w_in, v_w_in, "adamw_w_in")
    r_a, r_b, r_att, r_o = ([a.reshape(DEPTH, 4, R_O, D)[:, j] for a in r_4] for j in range(4))

    grads = [g_norm_mix, g_norm_mem, r_in[0], g_b_gate, g_conv_a, r_a[0], g_conv_b, g_cbias, g_lng, g_lnb, r_b[0],
             r_kv[0], r_att[0], r_o[0], g_norm_ffn, r_up[0], g_conv_f, r_dn[0], g_final]
    deltas = [d_norm_mix, d_norm_mem, r_in[1], d_b_gate, d_conv_a, r_a[1], d_conv_b, d_cbias, d_lng, d_lnb, r_b[1],
              r_kv[1], r_att[1], r_o[1], d_norm_ffn, r_up[1], d_conv_f, r_dn[1], d_final]
    new_m = [nm_norm_mix, nm_norm_mem, r_in[2], nm_b_gate, nm_conv_a, r_a[2], nm_conv_b, nm_cbias, nm_lng, nm_lnb,
             r_b[2], r_kv[2], r_att[2], r_o[2], nm_norm_ffn, r_up[2], nm_conv_f, r_dn[2], nm_final]
    new_v = [nv_norm_mix, nv_norm_mem, r_in[3], nv_b_gate, nv_conv_a, r_a[3], nv_conv_b, nv_cbias, nv_lng, nv_lnb,
             r_b[3], r_kv[3], r_att[3], r_o[3], nv_norm_ffn, r_up[3], nv_conv_f, r_dn[3], nv_final]
    return (loss, dx[None], *grads, *deltas, *new_m, *new_v)
```

```python
import functools

import jax
import jax.numpy as jnp
import numpy as np
from jax import lax
from jax.experimental import pallas as pl
from jax.experimental.pallas import tpu as pltpu

F32 = jnp.float32
BF = jnp.bfloat16
SDS = jax.ShapeDtypeStruct
MESH = pl.DeviceIdType.MESH
ANY = pl.BlockSpec(memory_space=pl.ANY)

N_DEV = 8
DEPTH = 2
D = 1024
N_HEADS = 4
HEAD = D // N_HEADS
D_FF = 2816
K_A, K_B, K_F = 3, 31, 3
NORM_EPS = 1e-6

C_IN = 9 * D // N_DEV
C_KV = 2 * D // N_DEV
C_UP = 2 * D_FF // N_DEV
LANE = 128
C_UP_P = -(-C_UP // LANE) * LANE
FF_P = 4 * C_UP_P
R_O = D // N_DEV
R_DN = D_FF // N_DEV

VMEM_LIMIT = 56 * 1024 * 1024
TM = 512
TR = 256
SUB = 128
H_S, H_L = 16, 32

ADAM_LR, ADAM_B1, ADAM_B2, ADAM_EPS, ADAM_WD, ADAM_STEP = 0.001, 0.9, 0.999, 1e-08, 0.01, 10

UP_ORDER = (0, 4, 1, 5, 2, 6, 3, 7)


def _pcall(body, **kw):
    return pl.pallas_call(body, **kw)


def _cp(sem=None, **kw):
    return pltpu.CompilerParams(dimension_semantics=sem, vmem_limit_bytes=VMEM_LIMIT, **kw)


def _dot(a, b):
    return jnp.dot(a, b, preferred_element_type=F32)


def _dot_nt(a, b):
    return lax.dot_general(a, b, (((1,), (1,)), ((), ())), preferred_element_type=F32)


def _dot_tn(a, b):
    return lax.dot_general(a, b, (((0,), (0,)), ((), ())), preferred_element_type=F32)


def _sigmoid(z):
    return 1.0 / (1.0 + jnp.exp(-z))


def _rms(xv):
    return lax.rsqrt(jnp.mean(xv * xv, axis=-1, keepdims=True) + NORM_EPS)


def _up_slot(idx):
    return jnp.where(idx < 4, 2 * idx, 2 * (idx - 4) + 1)


def _dn_row(idx):
    return C_UP_P * (idx // 2) + R_DN * (idx % 2)


def _mm(a, b, *, ta=False, tb=False, out_dtype=BF, tm=TM, tn=512, tk=None, name):
    m, k_dim = (a.shape[1], a.shape[0]) if ta else a.shape
    n = b.shape[0] if tb else b.shape[1]
    tm, tn = min(tm, m), min(tn, n)
    tk = k_dim if tk is None else min(tk, k_dim)
    nk = k_dim // tk
    assert m % tm == 0 and n % tn == 0 and k_dim % tk == 0
    dims = (((0 if ta else 1,), (1 if tb else 0,)), ((), ()))

    def body(a_ref, b_ref, o_ref, *scratch):
        part = lax.dot_general(a_ref[...], b_ref[...], dims, preferred_element_type=F32)
        if nk == 1:
            o_ref[...] = part.astype(o_ref.dtype)
            return
        acc = scratch[0]
        k = pl.program_id(2)

        @pl.when(k == 0)
        def _():
            acc[...] = part

        @pl.when(k > 0)
        def _():
            acc[...] += part

        @pl.when(k == nk - 1)
        def _():
            o_ref[...] = acc[...].astype(o_ref.dtype)

    a_spec = pl.BlockSpec((tk, tm), lambda i, j, k: (k, i)) if ta else pl.BlockSpec((tm, tk), lambda i, j, k: (i, k))
    b_spec = pl.BlockSpec((tn, tk), lambda i, j, k: (j, k)) if tb else pl.BlockSpec((tk, tn), lambda i, j, k: (k, j))
    return _pcall(
        body, grid=(m // tm, n // tn, nk), in_specs=[a_spec, b_spec],
        out_specs=pl.BlockSpec((tm, tn), lambda i, j, k: (i, j)),
        out_shape=SDS((m, n), out_dtype),
        scratch_shapes=[pltpu.VMEM((tm, tn), F32)] if nk > 1 else [],
        compiler_params=_cp(("parallel", "parallel", "arbitrary")), name=name)(a, b)


def _mm_res_norm(a, w, x, g, *, name):
    s, k_dim = a.shape
    tm = min(TM, s)

    def body(a_ref, w_ref, x_ref, g_ref, xo_ref, h_ref):
        xo = x_ref[...] + _dot(a_ref[...], w_ref[...])
        xo_ref[...] = xo
        h_ref[...] = ((xo * _rms(xo)) * g_ref[...]).astype(BF)

    return _pcall(
        body, grid=(s // tm,),
        in_specs=[pl.BlockSpec((tm, k_dim), lambda i: (i, 0)),
                  pl.BlockSpec((k_dim, D), lambda i: (0, 0), pipeline_mode=pl.Buffered(1)),
                  pl.BlockSpec((tm, D), lambda i: (i, 0)), pl.BlockSpec((1, D), lambda i: (0, 0))],
        out_specs=[pl.BlockSpec((tm, D), lambda i: (i, 0))] * 2,
        out_shape=[SDS((s, D), F32), SDS((s, D), BF)],
        compiler_params=_cp(("parallel",)), name=name)(a, w, x, g)


def _mm_nt_normbwd(da, w, x, dres, g, *, tk=None, name):
    s, k_dim = da.shape
    tm = min(TM, s)
    tk = k_dim if tk is None else tk
    nk = k_dim // tk
    assert k_dim % tk == 0

    def body(da_ref, w_ref, x_ref, dres_ref, g_ref, dx_ref, dxb_ref, dg_ref, *scratch):
        i, k = pl.program_id(0), pl.program_id(1)
        part = _dot_nt(da_ref[...], w_ref[...])
        if nk > 1:
            acc = scratch[0]

            @pl.when(k == 0)
            def _():
                acc[...] = part

            @pl.when(k > 0)
            def _():
                acc[...] += part

        @pl.when((i == 0) & (k == 0))
        def _():
            dg_ref[...] = jnp.zeros_like(dg_ref)

        @pl.when(k == nk - 1)
        def _():
            dh = acc[...] if nk > 1 else part
            xv = x_ref[...]
            r = _rms(xv)
            xn = xv * r
            dg_ref[0:1, :] += jnp.sum(dh * xn, axis=0, keepdims=True)
            dxn = dh * g_ref[...]
            dx = dres_ref[...] + r * (dxn - xn * jnp.mean(dxn * xn, axis=-1, keepdims=True))
            dx_ref[...] = dx
            dxb_ref[...] = dx.astype(BF)

    row = lambda i, k: (i, 0)
    w_spec = (pl.BlockSpec((D, tk), lambda i, k: (0, k)) if nk > 1 else
              pl.BlockSpec((D, tk), lambda i, k: (0, 0), pipeline_mode=pl.Buffered(1)))
    return _pcall(
        body, grid=(s // tm, nk),
        in_specs=[pl.BlockSpec((tm, tk), lambda i, k: (i, k)), w_spec,
                  pl.BlockSpec((tm, D), row), pl.BlockSpec((tm, D), row), pl.BlockSpec((1, D), lambda i, k: (0, 0))],
        out_specs=[pl.BlockSpec((tm, D), row), pl.BlockSpec((tm, D), row), pl.BlockSpec((8, D), lambda i, k: (0, 0))],
        out_shape=[SDS((s, D), F32), SDS((s, D), BF), SDS((8, D), F32)],
        scratch_shapes=[pltpu.VMEM((tm, D), F32)] if nk > 1 else [],
        compiler_params=_cp(("arbitrary", "arbitrary")), name=name)(da, w, x, dres, g)


def _rms_fwd(x, g, *, name):
    s = x.shape[0]
    tm = min(TM, s)

    def body(x_ref, g_ref, h_ref):
        xv = x_ref[...]
        h_ref[...] = ((xv * _rms(xv)) * g_ref[...]).astype(BF)

    return _pcall(
        body, grid=(s // tm,),
        in_specs=[pl.BlockSpec((tm, D), lambda i: (i, 0)), pl.BlockSpec((1, D), lambda i: (0, 0))],
        out_specs=pl.BlockSpec((tm, D), lambda i: (i, 0)), out_shape=SDS((s, D), BF),
        compiler_params=_cp(("parallel",)), name=name)(x, g)


def _loss_head(x, tgt, g, *, name):
    s = x.shape[0]
    tm = min(TM, s)

    def body(x_ref, t_ref, g_ref, dx_ref, dxb_ref, sums_ref):
        @pl.when(pl.program_id(0) == 0)
        def _():
            sums_ref[...] = jnp.zeros_like(sums_ref)

        xv = x_ref[...]
        r = _rms(xv)
        xn = xv * r
        diff = xn * g_ref[...] - t_ref[...]
        sums_ref[0:1, :] += jnp.sum(diff * diff, axis=0, keepdims=True)
        dy = diff * (1.0 / D)
        sums_ref[1:2, :] += jnp.sum(dy * xn, axis=0, keepdims=True)
        dxn = dy * g_ref[...]
        dx = r * (dxn - xn * jnp.mean(dxn * xn, axis=-1, keepdims=True))
        dx_ref[...] = dx
        dxb_ref[...] = dx.astype(BF)

    row = lambda i: (i, 0)
    return _pcall(
        body, grid=(s // tm,),
        in_specs=[pl.BlockSpec((tm, D), row), pl.BlockSpec((tm, D), row), pl.BlockSpec((1, D), lambda i: (0, 0))],
        out_specs=[pl.BlockSpec((tm, D), row), pl.BlockSpec((tm, D), row), pl.BlockSpec((8, D), lambda i: (0, 0))],
        out_shape=[SDS((s, D), F32), SDS((s, D), BF), SDS((8, D), F32)],
        compiler_params=_cp(("arbitrary",)), name=name)(x, tgt, g)


def _halo_before(i, tr, h):
    return jnp.maximum(i * (tr // h) - 1, 0)


def _halo_after(i, tr, h, s):
    return jnp.minimum((i + 1) * (tr // h), s // h - 1)


def _taps(buf, w_ref, sl, k_w, base, rows):
    acc = None
    for k in range(k_w):
        t = w_ref[k:k + 1, sl] * buf[base + k:base + k + rows, sl]
        acc = t if acc is None else acc + t
    return acc


def _taps_rev(buf, w_ref, sl, k_w, base, rows):
    acc = None
    for k in range(k_w):
        t = w_ref[k:k + 1, sl] * buf[base + k_w - 1 - k:base + k_w - 1 - k + rows, sl]
        acc = t if acc is None else acc + t
    return acc


def _tap_grads(dw_ref, dc, buf, sl, k_w, base, rows):
    for k in range(k_w):
        dw_ref[k:k + 1, sl] += jnp.sum(dc * buf[base + k:base + k + rows, sl], axis=0, keepdims=True)


def _bra_fwd(proj, cw, *, name):
    s = proj.shape[0]
    tr, h = min(TR, s), H_S
    sub = min(SUB, tr)

    def body(cur, halo, w_ref, za_ref, cvb):
        i = pl.program_id(0)
        hv = halo[:, D:2 * D].astype(F32) * halo[:, 2 * D:3 * D].astype(F32)
        cvb[0:h, :] = jnp.where(i == 0, 0.0, hv)
        cvb[h:h + tr, :] = cur[:, D:2 * D].astype(F32) * cur[:, 2 * D:3 * D].astype(F32)
        for c in range(D // LANE):
            sl = slice(LANE * c, LANE * c + LANE)
            ca = _taps(cvb, w_ref, sl, K_A, h - (K_A - 1), tr)
            za_ref[:, sl] = (cur[:, sl].astype(F32) * ca).astype(BF)

    return _pcall(
        body, grid=(s // tr,),
        in_specs=[pl.BlockSpec((tr, 3 * D), lambda i: (i, 0)),
                  pl.BlockSpec((h, 3 * D), lambda i: (_halo_before(i, tr, h), 0)),
                  pl.BlockSpec((8, D), lambda i: (0, 0))],
        out_specs=pl.BlockSpec((tr, D), lambda i: (i, 0)), out_shape=SDS((s, D), BF),
        scratch_shapes=[pltpu.VMEM((h + tr, D), F32)],
        compiler_params=_cp(("parallel",)), name=name)(proj, proj, cw)


def _bra_bwd(proj, dza, cw, dproj, *, name):
    s = proj.shape[0]
    tr, h = min(TR, s), H_S
    sub = min(SUB, tr)
    n = s // tr

    def body(before, cur, after, dz_cur, dz_after, w_ref, dproj_in, da_ref, dw_ref, cvb, dcab):
        del dproj_in
        i = pl.program_id(0)

        @pl.when(i == 0)
        def _():
            dw_ref[...] = jnp.zeros_like(dw_ref)

        first, last = i == 0, i == n - 1
        cvb[0:h, :] = jnp.where(first, 0.0, before[:, D:2 * D].astype(F32) * before[:, 2 * D:3 * D].astype(F32))
        cvb[h:h + tr, :] = cur[:, D:2 * D].astype(F32) * cur[:, 2 * D:3 * D].astype(F32)
        dcab[0:tr, :] = dz_cur[...].astype(F32) * cur[:, 0:D].astype(F32)
        dcab[tr:tr + h, :] = jnp.where(last, 0.0, dz_after[...].astype(F32) * after[:, 0:D].astype(F32))
        for c in range(D // LANE):
            sl = slice(LANE * c, LANE * c + LANE)
            gl, vl = slice(D + LANE * c, D + LANE * c + LANE), slice(2 * D + LANE * c, 2 * D + LANE * c + LANE)
            for r0 in range(0, tr, sub):
                rows = slice(r0, r0 + sub)
                ca = _taps(cvb, w_ref, sl, K_A, h - (K_A - 1) + r0, sub)
                da_ref[rows, sl] = (dz_cur[rows, sl].astype(F32) * ca).astype(BF)
                dcv = _taps_rev(dcab, w_ref, sl, K_A, r0, sub)
                da_ref[rows, gl] = (dcv * cur[rows, vl].astype(F32)).astype(BF)
                da_ref[rows, vl] = (dcv * cur[rows, gl].astype(F32)).astype(BF)
                _tap_grads(dw_ref, dcab[rows, sl], cvb, sl, K_A, h - (K_A - 1) + r0, sub)

    return _pcall(
        body, grid=(n,),
        in_specs=[pl.BlockSpec((h, 3 * D), lambda i: (_halo_before(i, tr, h), 0)),
                  pl.BlockSpec((tr, 3 * D), lambda i: (i, 0)),
                  pl.BlockSpec((h, 3 * D), lambda i: (_halo_after(i, tr, h, s), 0)),
                  pl.BlockSpec((tr, D), lambda i: (i, 0)),
                  pl.BlockSpec((h, D), lambda i: (_halo_after(i, tr, h, s), 0)),
                  pl.BlockSpec((8, D), lambda i: (0, 0)), ANY],
        out_specs=[pl.BlockSpec((tr, 3 * D), lambda i: (i, 0)), pl.BlockSpec((8, D), lambda i: (0, 0))],
        out_shape=[SDS(dproj.shape, BF), SDS((8, D), F32)], input_output_aliases={6: 0},
        scratch_shapes=[pltpu.VMEM((h + tr, D), F32), pltpu.VMEM((tr + h, D), F32)],
        compiler_params=_cp(("arbitrary",)), name=name)(proj, proj, proj, dza, dza, cw, dproj)


_U_COL, _UG_COL = 3, 4


def _brb_conv_fwd(proj, cw, bias, *, name):
    s = proj.shape[0]
    tr, h = min(TR, s), H_L
    sub = min(SUB, tr)
    nb = -(-(tr + 24) // sub)
    sel = _row_selector(sub, [8 - r for r in range(8)])

    def body(u_cur, ug_cur, u_halo, ug_halo, sel_ref, w_ref, b_ref, cb_ref, glb, shifted):
        i = pl.program_id(0)
        before = (u_halo[...].astype(F32) * _sigmoid(ug_halo[...].astype(F32))).astype(BF)
        glb[0:h, :] = jnp.where(i == 0, jnp.zeros_like(before), before)
        glb[h:h + tr, :] = (u_cur[...].astype(F32) * _sigmoid(ug_cur[...].astype(F32))).astype(BF)
        glb[h + tr:(nb + 1) * sub, :] = jnp.zeros(((nb + 1) * sub - h - tr, D), BF)
        for c in range(D // LANE):
            sl = slice(LANE * c, LANE * c + LANE)
            for blk in range(nb):
                res = _dot(sel_ref[...], glb[blk * sub:(blk + 2) * sub, sl])
                for r in range(8):
                    shifted[r, blk * sub:(blk + 1) * sub, :] = res[r * sub:(r + 1) * sub]
            for r0 in range(0, tr, sub):
                acc = None
                for k in range(K_B):
                    q, r = divmod(K_B - 1 - k, 8)
                    at = 24 + r0 - 8 * q
                    term = w_ref[k:k + 1, sl] * shifted[r, at:at + sub, :]
                    acc = term if acc is None else acc + term
                cb_ref[r0:r0 + sub, sl] = (acc + b_ref[:, sl]).astype(BF)

    return _pcall(
        body, grid=(s // tr,),
        in_specs=[pl.BlockSpec((tr, D), lambda i: (i, _U_COL)), pl.BlockSpec((tr, D), lambda i: (i, _UG_COL)),
                  pl.BlockSpec((h, D), lambda i: (_halo_before(i, tr, h), _U_COL)),
                  pl.BlockSpec((h, D), lambda i: (_halo_before(i, tr, h), _UG_COL)),
                  pl.BlockSpec(sel.shape, lambda i: (0, 0)),
                  pl.BlockSpec((32, D), lambda i: (0, 0)), pl.BlockSpec((1, D), lambda i: (0, 0))],
        out_specs=pl.BlockSpec((tr, D), lambda i: (i, 0)), out_shape=SDS((s, D), BF),
        scratch_shapes=[pltpu.VMEM(((nb + 1) * sub, D), BF), pltpu.VMEM((8, nb * sub, LANE), F32)],
        compiler_params=_cp(("parallel",)), name=name)(proj, proj, proj, proj, sel, cw, bias)


def _brb_conv_bwd(proj, dcb, dq, cw, dproj, *, name):
    s = proj.shape[0]
    tr, h = min(TR, s), H_L
    sub = min(SUB, tr)
    n = s // tr
    nb = -(-(tr + 24) // sub)
    sel = _row_selector(sub, list(range(8)))

    def body(u_cur, ug_cur, d_cur, d_after, dq_ref, sel_ref, w_ref, dproj_in, db_ref, dw_ref, dcbb, shifted):
        del dproj_in
        i = pl.program_id(0)

        @pl.when(i == 0)
        def _():
            dw_ref[...] = jnp.zeros_like(dw_ref)

        db_ref[:, 2 * D:3 * D] = dq_ref[...]
        after = d_after[...]
        dcbb[0:tr, :] = d_cur[...]
        dcbb[tr:tr + h, :] = jnp.where(i == n - 1, jnp.zeros_like(after), after)
        dcbb[tr + h:(nb + 1) * sub, :] = jnp.zeros(((nb + 1) * sub - h - tr, D), BF)
        for c in range(D // LANE):
            sl = slice(LANE * c, LANE * c + LANE)
            for blk in range(nb):
                res = _dot(sel_ref[...], dcbb[blk * sub:(blk + 2) * sub, sl])
                for r in range(8):
                    shifted[r, blk * sub:(blk + 1) * sub, :] = res[r * sub:(r + 1) * sub]
            for r0 in range(0, tr, sub):
                u = u_cur[r0:r0 + sub, sl].astype(F32)
                sg = _sigmoid(ug_cur[r0:r0 + sub, sl].astype(F32))
                glu = u * sg
                dglu = None
                for k in range(K_B):
                    q, r = divmod(K_B - 1 - k, 8)
                    at = r0 + 8 * q
                    win = shifted[r, at:at + sub, :]
                    term = w_ref[k:k + 1, sl] * win
                    dglu = term if dglu is None else dglu + term
                    dw_ref[k:k + 1, sl] += jnp.sum(win * glu, axis=0, keepdims=True)
                db_ref[r0:r0 + sub, sl] = (dglu * sg).astype(BF)
                db_ref[r0:r0 + sub, D + LANE * c:D + LANE * c + LANE] = (dglu * u * sg * (1.0 - sg)).astype(BF)

    return _pcall(
        body, grid=(n,),
        in_specs=[pl.BlockSpec((tr, D), lambda i: (i, _U_COL)), pl.BlockSpec((tr, D), lambda i: (i, _UG_COL)),
                  pl.BlockSpec((tr, D), lambda i: (i, 0)),
                  pl.BlockSpec((h, D), lambda i: (_halo_after(i, tr, h, s), 0)),
                  pl.BlockSpec((tr, D), lambda i: (i, 0)),
                  pl.BlockSpec(sel.shape, lambda i: (0, 0)),
                  pl.BlockSpec((32, D), lambda i: (0, 0)), ANY],
        out_specs=[pl.BlockSpec((tr, 3 * D), lambda i: (i, 1)), pl.BlockSpec((32, D), lambda i: (0, 0))],
        out_shape=[SDS(dproj.shape, BF), SDS((32, D), F32)], input_output_aliases={7: 0},
        scratch_shapes=[pltpu.VMEM(((nb + 1) * sub, D), BF), pltpu.VMEM((8, nb * sub, LANE), F32)],
        compiler_params=_cp(("arbitrary",)), name=name)(proj, proj, dcb, dcb, dq, sel, cw, dproj)


def _ln_silu_fwd(cb, g, b, *, name):
    s = cb.shape[0]
    tm = min(TM, s)

    def body(cb_ref, g_ref, b_ref, sb_ref):
        z = cb_ref[...].astype(F32)
        zc = z - jnp.mean(z, axis=-1, keepdims=True)
        ln = (zc * lax.rsqrt(jnp.mean(zc * zc, axis=-1, keepdims=True) + NORM_EPS)) * g_ref[...] + b_ref[...]
        sb_ref[...] = (ln * _sigmoid(ln)).astype(BF)

    row = lambda i: (i, 0)
    vec = pl.BlockSpec((1, D), lambda i: (0, 0))
    return _pcall(
        body, grid=(s // tm,), in_specs=[pl.BlockSpec((tm, D), row), vec, vec],
        out_specs=pl.BlockSpec((tm, D), row), out_shape=SDS((s, D), BF),
        compiler_params=_cp(("parallel",)), name=name)(cb, g, b)


def _ln_silu_bwd(cb, dsb, g, b, *, name):
    s = cb.shape[0]
    tm = min(TM, s)

    def body(cb_ref, dsb_ref, g_ref, b_ref, dcb_ref, sums_ref):
        @pl.when(pl.program_id(0) == 0)
        def _():
            sums_ref[...] = jnp.zeros_like(sums_ref)

        z = cb_ref[...].astype(F32)
        zc = z - jnp.mean(z, axis=-1, keepdims=True)
        rstd = lax.rsqrt(jnp.mean(zc * zc, axis=-1, keepdims=True) + NORM_EPS)
        lnh = zc * rstd
        ln = lnh * g_ref[...] + b_ref[...]
        sg = _sigmoid(ln)
        dln = dsb_ref[...].astype(F32) * (sg * (1.0 + ln * (1.0 - sg)))
        sums_ref[0:1, :] += jnp.sum(dln * lnh, axis=0, keepdims=True)
        sums_ref[1:2, :] += jnp.sum(dln, axis=0, keepdims=True)
        dlnh = dln * g_ref[...]
        dz = rstd * (dlnh - jnp.mean(dlnh, axis=-1, keepdims=True)
                     - lnh * jnp.mean(dlnh * lnh, axis=-1, keepdims=True))
        sums_ref[2:3, :] += jnp.sum(dz, axis=0, keepdims=True)
        dcb_ref[...] = dz.astype(BF)

    row = lambda i: (i, 0)
    vec = pl.BlockSpec((1, D), lambda i: (0, 0))
    return _pcall(
        body, grid=(s // tm,), in_specs=[pl.BlockSpec((tm, D), row), pl.BlockSpec((tm, D), row), vec, vec],
        out_specs=[pl.BlockSpec((tm, D), row), pl.BlockSpec((8, D), lambda i: (0, 0))],
        out_shape=[SDS((s, D), BF), SDS((8, D), F32)],
        compiler_params=_cp(("arbitrary",)), name=name)(cb, dsb, g, b)


_Q_COL = 5 * D // HEAD


def _kv_prep(mem, g, wkv, *, name):
    m = mem.shape[0]

    def body(mem_ref, g_ref, w_ref, memn_ref, kv_ref):
        mv = mem_ref[...]
        memn = ((mv * _rms(mv)) * g_ref[...]).astype(BF)
        memn_ref[...] = memn
        for dev in range(N_DEV):
            kv_ref[:, dev * C_KV:(dev + 1) * C_KV] = _dot(memn, w_ref[dev]).astype(BF)

    return _pcall(body, out_shape=[SDS((m, D), BF), SDS((m, 2 * D), BF)],
                  compiler_params=_cp(), name=name)(mem, g, wkv)


def _softmax_rows(q, k):
    sc = _dot_nt(q, k) * (1.0 / (HEAD ** 0.5))
    e = jnp.exp(sc - jnp.max(sc, axis=-1, keepdims=True))
    return e / jnp.sum(e, axis=-1, keepdims=True)


def _attn_fwd(proj, kv, *, name):
    s, m = proj.shape[0], kv.shape[0]
    tm = min(TM, s)

    def body(q_ref, k_ref, v_ref, o_ref):
        p = _softmax_rows(q_ref[...], k_ref[...])
        o_ref[...] = _dot(p.astype(BF), v_ref[...]).astype(BF)

    return _pcall(
        body, grid=(s // tm, N_HEADS),
        in_specs=[pl.BlockSpec((tm, HEAD), lambda i, hd: (i, _Q_COL + hd)),
                  pl.BlockSpec((m, HEAD), lambda i, hd: (0, hd)),
                  pl.BlockSpec((m, HEAD), lambda i, hd: (0, N_HEADS + hd))],
        out_specs=pl.BlockSpec((tm, HEAD), lambda i, hd: (i, hd)), out_shape=SDS((s, D), BF),
        compiler_params=_cp(("parallel", "parallel")), name=name)(proj, kv, kv)


def _attn_bwd(proj, kv, do, *, name):
    s, m = proj.shape[0], kv.shape[0]
    tm = min(TM, s)

    def body(q_ref, k_ref, v_ref, do_ref, dq_ref, dk_ref, dv_ref):
        @pl.when(pl.program_id(1) == 0)
        def _():
            dk_ref[...] = jnp.zeros_like(dk_ref)
            dv_ref[...] = jnp.zeros_like(dv_ref)

        q, k, dov = q_ref[...], k_ref[...], do_ref[...]
        p = _softmax_rows(q, k)
        dp = _dot_nt(dov, v_ref[...])
        dv_ref[...] += _dot_tn(p.astype(BF), dov)
        ds = (p * (dp - jnp.sum(dp * p, axis=-1, keepdims=True)) * (1.0 / (HEAD ** 0.5))).astype(BF)
        dq_ref[...] = _dot(ds, k).astype(BF)
        dk_ref[...] += _dot_tn(ds, q)

    return _pcall(
        body, grid=(N_HEADS, s // tm),
        in_specs=[pl.BlockSpec((tm, HEAD), lambda hd, i: (i, _Q_COL + hd)),
                  pl.BlockSpec((m, HEAD), lambda hd, i: (0, hd)),
                  pl.BlockSpec((m, HEAD), lambda hd, i: (0, N_HEADS + hd)),
                  pl.BlockSpec((tm, HEAD), lambda hd, i: (i, hd))],
        out_specs=[pl.BlockSpec((tm, HEAD), lambda hd, i: (i, hd)),
                   pl.BlockSpec((m, HEAD), lambda hd, i: (0, hd)),
                   pl.BlockSpec((m, HEAD), lambda hd, i: (0, hd))],
        out_shape=[SDS((s, D), BF), SDS((m, D), F32), SDS((m, D), F32)],
        compiler_params=_cp(("parallel", "arbitrary")), name=name)(proj, kv, kv, do)


def _kv_bwd(mem, g, memn, dk, dv, wkv, *, name):
    def body(mem_ref, g_ref, memn_ref, dk_ref, dv_ref, w_ref, dw_ref, dg_ref):
        memn = memn_ref[...]
        dmemn = None
        for dev in range(N_DEV):
            d_ref, col = (dk_ref, dev) if dev < N_HEADS else (dv_ref, dev - N_HEADS)
            dslab = d_ref[:, col * C_KV:(col + 1) * C_KV].astype(BF)
            dw_ref[dev] = _dot_tn(memn, dslab).astype(BF)
            part = _dot_nt(dslab, w_ref[dev])
            dmemn = part if dmemn is None else dmemn + part
        mv = mem_ref[...]
        dg_ref[...] = jnp.zeros_like(dg_ref)
        dg_ref[0:1, :] = jnp.sum(dmemn * (mv * _rms(mv)), axis=0, keepdims=True)

    assert C_KV == HEAD
    return _pcall(body, out_shape=[SDS((N_DEV, D, C_KV), BF), SDS((8, D), F32)],
                  compiler_params=_cp(), name=name)(mem, g, memn, dk, dv, wkv)


_TM_MIX = 512


def _mix_out(x, za, sb, o, proj, w4, bg, g_next, *, name):
    s = x.shape[0]
    tm = min(_TM_MIX, s)

    def body(x_ref, za_ref, sb_ref, o_ref, pg_ref, w4_ref, bg_ref, gn_ref,
             ya_ref, yb_ref, yc_ref, mg_ref, x1_ref, h_ref):
        ys = (_dot(za_ref[...], w4_ref[0]), _dot(sb_ref[...], w4_ref[1]), _dot(o_ref[...], w4_ref[2]))
        merged = None
        for j, (y, y_ref) in enumerate(zip(ys, (ya_ref, yb_ref, yc_ref))):
            y_ref[...] = y.astype(BF)
            gate = _sigmoid(pg_ref[:, j * D:(j + 1) * D].astype(F32) + bg_ref[:, j * D:(j + 1) * D])
            merged = gate * y if merged is None else merged + gate * y
        mg = merged.astype(BF)
        mg_ref[...] = mg
        x1 = x_ref[...] + _dot(mg, w4_ref[3])
        x1_ref[...] = x1
        h_ref[...] = ((x1 * _rms(x1)) * gn_ref[...]).astype(BF)

    row = lambda i: (i, 0)
    act = pl.BlockSpec((tm, D), row)
    return _pcall(
        body, grid=(s // tm,),
        in_specs=[act, act, act, act, pl.BlockSpec((tm, 3 * D), lambda i: (i, 2)),
                  pl.BlockSpec((4, D, D), lambda i: (0, 0, 0), pipeline_mode=pl.Buffered(1)), pl.BlockSpec((1, 3 * D), lambda i: (0, 0)),
                  pl.BlockSpec((1, D), lambda i: (0, 0))],
        out_specs=[act] * 6,
        out_shape=[SDS((s, D), BF)] * 4 + [SDS((s, D), F32), SDS((s, D), BF)],
        compiler_params=_cp(("parallel",)), name=name)(x, za, sb, o, proj, w4, bg, g_next)


def _mix_bwd(dxb, ya, yb, yc, proj, w4, bg, *, name):
    s = dxb.shape[0]
    tm = min(_TM_MIX, s)

    def body(dx_ref, ya_ref, yb_ref, yc_ref, pg_ref, w4_ref, bg_ref,
             dya_ref, dyb_ref, dyc_ref, dza_ref, dsb_ref, do_ref, dgt_ref, dbg_ref):
        @pl.when(pl.program_id(0) == 0)
        def _():
            dbg_ref[...] = jnp.zeros_like(dbg_ref)

        dm = _dot_nt(dx_ref[...], w4_ref[3])
        for j, (y_ref, dy_ref, din_ref) in enumerate(zip((ya_ref, yb_ref, yc_ref), (dya_ref, dyb_ref, dyc_ref),
                                                         (dza_ref, dsb_ref, do_ref))):
            cols = slice(j * D, (j + 1) * D)
            gate = _sigmoid(pg_ref[:, cols].astype(F32) + bg_ref[:, cols])
            dy = (dm * gate).astype(BF)
            dy_ref[...] = dy
            din_ref[...] = _dot_nt(dy, w4_ref[j]).astype(BF)
            dpre = dm * y_ref[...].astype(F32) * gate * (1.0 - gate)
            dgt_ref[:, cols] = dpre.astype(BF)
            dbg_ref[0:1, cols] += jnp.sum(dpre, axis=0, keepdims=True)

    row = lambda i: (i, 0)
    act = pl.BlockSpec((tm, D), row)
    return _pcall(
        body, grid=(s // tm,),
        in_specs=[act, act, act, act, pl.BlockSpec((tm, 3 * D), lambda i: (i, 2)),
                  pl.BlockSpec((4, D, D), lambda i: (0, 0, 0), pipeline_mode=pl.Buffered(1)), pl.BlockSpec((1, 3 * D), lambda i: (0, 0))],
        out_specs=[act] * 6 + [pl.BlockSpec((tm, 3 * D), lambda i: (i, 2)),
                               pl.BlockSpec((8, 3 * D), lambda i: (0, 0))],
        out_shape=[SDS((s, D), BF)] * 6 + [SDS((s, 9 * D), BF), SDS((8, 3 * D), F32)],
        compiler_params=_cp(("arbitrary",)), name=name)(dxb, ya, yb, yc, proj, w4, bg)


_PAIR = 2 * C_UP_P


def _row_selector(sub, first_cols):
    rows = np.arange(len(first_cols) * sub)
    col = np.asarray(first_cols)[rows // sub] + rows % sub
    return jnp.asarray(np.arange(2 * sub)[None, :] == col[:, None], BF)


def _ffn_act(u2, cw, *, name):
    s = u2.shape[0]
    tr = min(TR, s)
    sub = min(SUB, tr)
    sel = _row_selector(sub, [sub - (K_F - 1 - k) for k in range(K_F)])

    def body(cur, prev, sel_ref, w_ref, act_ref, c2_ref, xb, win):
        i = pl.program_id(1)
        before = prev[...]
        xb[0:sub, :] = jnp.where(i == 0, jnp.zeros_like(before), before)
        xb[sub:sub + tr, :] = cur[...]
        for r0 in range(0, tr, sub):
            win[...] = _dot(sel_ref[...], xb[r0:r0 + 2 * sub, :])
            for c in range(C_UP_P // LANE):
                gl = slice(LANE * c, LANE * c + LANE)
                ul = slice(C_UP_P + LANE * c, C_UP_P + LANE * c + LANE)
                gt = sum(w_ref[k:k + 1, gl] * win[k * sub:(k + 1) * sub, gl] for k in range(K_F))
                up = sum(w_ref[k:k + 1, ul] * win[k * sub:(k + 1) * sub, ul] for k in range(K_F))
                c2_ref[r0:r0 + sub, gl] = gt.astype(BF)
                c2_ref[r0:r0 + sub, ul] = up.astype(BF)
                act_ref[r0:r0 + sub, gl] = (gt * _sigmoid(gt) * up).astype(BF)

    return _pcall(
        body, grid=(4, s // tr),
        in_specs=[pl.BlockSpec((tr, _PAIR), lambda p, i: (i, p)),
                  pl.BlockSpec((sub, _PAIR), lambda p, i: (_halo_before(i, tr, sub), p)),
                  pl.BlockSpec(sel.shape, lambda p, i: (0, 0)),
                  pl.BlockSpec((8, _PAIR), lambda p, i: (0, p))],
        out_specs=[pl.BlockSpec((tr, C_UP_P), lambda p, i: (i, p)), pl.BlockSpec((tr, _PAIR), lambda p, i: (i, p))],
        out_shape=[SDS((s, FF_P), BF), SDS((s, 2 * FF_P), BF)],
        scratch_shapes=[pltpu.VMEM((sub + tr, _PAIR), BF), pltpu.VMEM((K_F * sub, _PAIR), F32)],
        compiler_params=_cp(("parallel", "parallel")), name=name)(u2, u2, sel, cw)


def _ffn_bwd(u2, c2, dact, cw, *, name):
    s = u2.shape[0]
    tr, h = min(TR, s), H_S
    sub = min(SUB, tr)
    n = s // tr
    sel = _row_selector(sub, [K_F - 1 - k for k in range(K_F)])

    def body(u_cur, c_cur, c_after, da_cur, da_after, sel_ref, w_ref, du_ref, dw_ref, dcb, win):
        i = pl.program_id(1)
        last = i == n - 1

        @pl.when(i == 0)
        def _():
            dw_ref[...] = jnp.zeros_like(dw_ref)

        def conv_grad(gt, up, da):
            gt, up, da = gt.astype(F32), up.astype(F32), da.astype(F32)
            sg = _sigmoid(gt)
            return (da * up * (sg * (1.0 + gt * (1.0 - sg)))).astype(BF), (da * (gt * sg)).astype(BF)

        for c in range(C_UP_P // LANE):
            gl = slice(LANE * c, LANE * c + LANE)
            ul = slice(C_UP_P + LANE * c, C_UP_P + LANE * c + LANE)
            for r0 in range(0, tr, sub):
                rows = slice(r0, r0 + sub)
                dcb[rows, gl], dcb[rows, ul] = conv_grad(c_cur[rows, gl], c_cur[rows, ul], da_cur[rows, gl])
            dg, du_ = conv_grad(c_after[:, gl], c_after[:, ul], da_after[:, gl])
            dcb[tr:tr + h, gl] = jnp.where(last, jnp.zeros_like(dg), dg)
            dcb[tr:tr + h, ul] = jnp.where(last, jnp.zeros_like(du_), du_)
        dcb[tr + h:tr + sub, :] = jnp.zeros((sub - h, _PAIR), BF)
        for r0 in range(0, tr, sub):
            win[...] = _dot(sel_ref[...], dcb[r0:r0 + 2 * sub, :])
            for c in range(_PAIR // LANE):
                sl = slice(LANE * c, LANE * c + LANE)
                u = u_cur[r0:r0 + sub, sl].astype(F32)
                du = None
                for k in range(K_F):
                    wk = win[k * sub:(k + 1) * sub, sl]
                    term = w_ref[k:k + 1, sl] * wk
                    du = term if du is None else du + term
                    dw_ref[k:k + 1, sl] += jnp.sum(wk * u, axis=0, keepdims=True)
                du_ref[r0:r0 + sub, sl] = du.astype(BF)

    return _pcall(
        body, grid=(4, n),
        in_specs=[pl.BlockSpec((tr, _PAIR), lambda p, i: (i, p)),
                  pl.BlockSpec((tr, _PAIR), lambda p, i: (i, p)),
                  pl.BlockSpec((h, _PAIR), lambda p, i: (_halo_after(i, tr, h, s), p)),
                  pl.BlockSpec((tr, C_UP_P), lambda p, i: (i, p)),
                  pl.BlockSpec((h, C_UP_P), lambda p, i: (_halo_after(i, tr, h, s), p)),
                  pl.BlockSpec(sel.shape, lambda p, i: (0, 0)),
                  pl.BlockSpec((8, _PAIR), lambda p, i: (0, p))],
        out_specs=[pl.BlockSpec((tr, _PAIR), lambda p, i: (i, p)), pl.BlockSpec((8, _PAIR), lambda p, i: (0, p))],
        out_shape=[SDS((s, 2 * FF_P), BF), SDS((8, 2 * FF_P), F32)],
        scratch_shapes=[pltpu.VMEM((tr + sub, _PAIR), BF), pltpu.VMEM((K_F * sub, _PAIR), F32)],
        compiler_params=_cp(("parallel", "arbitrary")), name=name)(u2, c2, c2, dact, dact, sel, cw)


def _relations():
    x, y, c = lax.axis_index("x"), lax.axis_index("y"), lax.axis_index("c")
    out = []
    for r in range(1, N_DEV):
        rx, ry, rc = (r >> 2) & 1, (r >> 1) & 1, r & 1
        out.append((r, (x ^ rx, y ^ ry, c ^ rc)))
    return out


def _my_index():
    return 4 * lax.axis_index("x") + 2 * lax.axis_index("y") + lax.axis_index("c")


def _slab(kind, ref, idx):
    if kind == "win":
        return ref.at[:, pl.ds(pl.multiple_of(idx * C_IN, LANE), C_IN)]
    if kind == "wup":
        return ref.at[:, pl.ds(pl.multiple_of(_up_slot(idx) * C_UP_P, LANE), C_UP_P)]
    if kind == "wkv":
        return ref.at[idx]
    if kind == "w4":
        return ref.at[:, pl.ds(pl.multiple_of(idx * R_O, 16), R_O), :]
    if kind == "wdn":
        return ref.at[pl.ds(pl.multiple_of(_dn_row(idx), 16), R_DN), :]
    assert kind == "cv"
    return ref.at[idx]


_WHOLE = {"win": ((D, 9 * D), BF), "wup": ((D, 2 * FF_P), BF), "wkv": ((N_DEV, D, C_KV), BF),
          "w4": ((4, D, D), BF), "wdn": ((FF_P, D), BF)}
_SHARD = {"win": (D, C_IN), "wup": (D, C_UP_P), "wkv": (D, C_KV), "w4": (4, R_O, D), "wdn": (R_DN, D)}
HBM_SPEC = pl.BlockSpec(memory_space=pltpu.HBM)
SEM_SPEC = pl.BlockSpec(memory_space=pltpu.SEMAPHORE)
_DATAFLOW = pltpu.SideEffectType.DATAFLOW_SIDE_EFFECTING


def _scatter_maps(kinds):
    return ((lambda srcs, lands, a, idx: _slab(kinds[a], srcs[a], idx)),
            (lambda lands, a, idx: lands[a].at[idx]))


_SLOTTED = ("wkv", "cv")


def _own_slab_blocks(kind, shard_shape):
    if kind in ("win", "wup"):
        rows, slot = 256, (_up_slot if kind == "wup" else (lambda m: m))
        return (shard_shape[0] // rows, (rows, shard_shape[1]), (lambda i, me: (i, slot(me[0]))),
                (lambda i, me: (i, 0)), (lambda i, me: (me[0], i, 0)))
    if kind == "w4":
        return (1, shard_shape, (lambda i, me: (0, me[0], 0)), (lambda i, me: (0, 0, 0)),
                (lambda i, me: (me[0], 0, 0, 0)))
    if kind == "wdn":
        rows = 32
        return (R_DN // rows, (rows, D), (lambda i, me: (_dn_row(me[0]) // rows + i, 0)), (lambda i, me: (i, 0)),
                (lambda i, me: (me[0], i, 0)))
    assert kind in _SLOTTED
    rows = min(256, shard_shape[0])
    return (shard_shape[0] // rows, (rows, shard_shape[1]), (lambda i, me: (me[0], i, 0)),
            (lambda i, me: (i, 0)), (lambda i, me: (me[0], i, 0)))


def _place_own(kind, src, out_sds, gather, me_arr, *, name):
    shard_shape = src.shape if gather else out_sds.shape[1:]
    steps, blk, whole_idx, shard_idx, staging_idx = _own_slab_blocks(kind, shard_shape)
    slotted = kind in _SLOTTED
    whole_spec = pl.BlockSpec(((None,) if slotted else ()) + tuple(blk), whole_idx)
    if gather:
        in_spec, out_spec = pl.BlockSpec(tuple(blk), shard_idx), whole_spec
    else:
        in_spec, out_spec = whole_spec, pl.BlockSpec((None,) + tuple(blk), staging_idx)
    zero_init = gather and kind == "wdn"

    def body(me_ref, src_ref, *rest):
        rest[-1][...] = src_ref[...].astype(rest[-1].dtype)

    operands = (me_arr, src) + ((jnp.zeros(out_sds.shape, out_sds.dtype),) if zero_init else ())
    return _pcall(
        body,
        grid_spec=pltpu.PrefetchScalarGridSpec(
            num_scalar_prefetch=1, grid=(steps,), in_specs=[in_spec] + ([ANY] if zero_init else []),
            out_specs=out_spec),
        out_shape=out_sds, input_output_aliases={2: 0} if zero_init else {},
        compiler_params=_cp(("arbitrary",)), name=name)(*operands)


def _peer_copies(n, src_of, dst_of, src_r, land_r, ssem, rsem):
    me = _my_index()
    out = []
    for r, peer in _relations():
        p_idx = 4 * peer[0] + 2 * peer[1] + peer[2]
        for a in range(n):
            def copy(src_idx, dst_idx, a=a, r=r, peer=peer):
                sem = a * (N_DEV - 1) + r - 1
                return pltpu.make_async_remote_copy(
                    src_ref=src_of(src_r, land_r, a, src_idx), dst_ref=dst_of(land_r, a, dst_idx),
                    send_sem=ssem.at[sem], recv_sem=rsem.at[sem], device_id=peer, device_id_type=MESH)
            out.append((functools.partial(copy, p_idx, me), functools.partial(copy, me, p_idx)))
    return out


def _exchange_start(srcs, lands, maps, after, *, name):
    n, ns = len(lands), len(srcs)
    src_of, dst_of = maps

    def body(*refs):
        src_r, land_r = refs[:ns], refs[ns:ns + n]
        ssem, rsem, token = refs[ns + n + 1], refs[ns + n + 2], refs[-1]
        for send, _ in _peer_copies(n, src_of, dst_of, src_r, land_r, ssem, rsem):
            send().start()
        token[...] = jnp.zeros_like(token)

    flight = list(srcs) + list(lands)
    outs = pl.pallas_call(
        body, name=name,
        out_shape=(pltpu.SemaphoreType.DMA((n * (N_DEV - 1),)), pltpu.SemaphoreType.DMA((n * (N_DEV - 1),)),
                   *[pltpu.HBM(a.shape, a.dtype) for a in flight], SDS((8, LANE), F32)),
        in_specs=[HBM_SPEC] * (ns + n) + [ANY],
        out_specs=(SEM_SPEC, SEM_SPEC, *[HBM_SPEC] * (ns + n), pl.BlockSpec(memory_space=pltpu.VMEM)),
        input_output_aliases={i: 2 + i for i in range(ns + n)},
        compiler_params=pltpu.CompilerParams(has_side_effects=_DATAFLOW),
    )(*[pltpu.with_memory_space_constraint(a, pltpu.HBM) for a in flight], after)
    return (outs[0], outs[1], list(outs[2:2 + ns + n]), ns), outs[-1]


def _exchange_wait(handle, maps, after, *, name):
    ssem, rsem, flight, ns = handle
    n = len(flight) - ns
    src_of, dst_of = maps

    def body(*refs):
        src_r, land_r, ssem_r, rsem_r = refs[:ns], refs[ns:ns + n], refs[ns + n], refs[ns + n + 1]
        for send, arrival in _peer_copies(n, src_of, dst_of, src_r, land_r, ssem_r, rsem_r):
            send().wait_send()
            arrival().wait_recv()

    outs = pl.pallas_call(
        body, name=name, out_shape=[pltpu.HBM(a.shape, a.dtype) for a in flight],
        in_specs=[HBM_SPEC] * (ns + n) + [SEM_SPEC, SEM_SPEC, ANY], out_specs=[HBM_SPEC] * (ns + n),
        input_output_aliases={i: i for i in range(ns + n)},
        compiler_params=pltpu.CompilerParams(has_side_effects=_DATAFLOW),
    )(*flight, ssem, rsem, after)
    return list(outs[ns:])


_SIBLING = 1
_ICI = (2, 4, 6)


def _rel_peer(r):
    x, y, c = lax.axis_index("x"), lax.axis_index("y"), lax.axis_index("c")
    peer = (x ^ ((r >> 2) & 1), y ^ ((r >> 1) & 1), c ^ (r & 1))
    return peer, 4 * peer[0] + 2 * peer[1] + peer[2]


def _rcopy(ref, ssem, rsem, peer):
    return pltpu.make_async_remote_copy(src_ref=ref, dst_ref=ref, send_sem=ssem, recv_sem=rsem, device_id=peer,
                                        device_id_type=MESH)


def _gather2_start(lands, kinds, after, *, name):
    n = len(lands)

    def body(*refs):
        land_r, (send1, recv_sib, recv_ici), token = refs[:n], refs[n + 1:n + 4], refs[-1]
        me = _my_index()
        for a in range(n):
            own = _slab(kinds[a], land_r[a], me)
            for j, r in enumerate((_SIBLING,) + _ICI):
                rsem = recv_sib.at[a] if r == _SIBLING else recv_ici.at[3 * a + j - 1]
                _rcopy(own, send1.at[4 * a + j], rsem, _rel_peer(r)[0]).start()
        token[...] = jnp.zeros_like(token)

    sems = [pltpu.SemaphoreType.DMA((4 * n,)), pltpu.SemaphoreType.DMA((n,)), pltpu.SemaphoreType.DMA((3 * n,))]
    outs = pl.pallas_call(
        body, name=name, out_shape=(*sems, *[pltpu.HBM(a.shape, a.dtype) for a in lands], SDS((8, LANE), F32)),
        in_specs=[HBM_SPEC] * n + [ANY],
        out_specs=(SEM_SPEC,) * 3 + (HBM_SPEC,) * n + (pl.BlockSpec(memory_space=pltpu.VMEM),),
        input_output_aliases={i: 3 + i for i in range(n)},
        compiler_params=pltpu.CompilerParams(has_side_effects=_DATAFLOW),
    )(*[pltpu.with_memory_space_constraint(a, pltpu.HBM) for a in lands], after)
    return dict(send1=outs[0], recv_sib=outs[1], recv_ici=outs[2], lands=list(outs[3:3 + n])), outs[-1]


def _gather2_forward(handle, kinds, after, *, name):
    lands = handle["lands"]
    n = len(lands)

    def body(*refs):
        land_r, recv_ici, (fwd_send, fwd_recv), token = refs[:n], refs[n], refs[n + 2:n + 4], refs[-1]
        sibling = _rel_peer(_SIBLING)[0]
        for a in range(n):
            for j, r in enumerate(_ICI):
                got = _slab(kinds[a], land_r[a], _rel_peer(r)[1])
                _rcopy(got, fwd_send.at[3 * a + j], recv_ici.at[3 * a + j], sibling).wait_recv()
                _rcopy(got, fwd_send.at[3 * a + j], fwd_recv.at[3 * a + j], sibling).start()
        token[...] = jnp.zeros_like(token)

    sems = [pltpu.SemaphoreType.DMA((3 * n,)), pltpu.SemaphoreType.DMA((3 * n,))]
    outs = pl.pallas_call(
        body, name=name, out_shape=(*sems, *[pltpu.HBM(a.shape, a.dtype) for a in lands], SDS((8, LANE), F32)),
        in_specs=[HBM_SPEC] * n + [SEM_SPEC, ANY],
        out_specs=(SEM_SPEC,) * 2 + (HBM_SPEC,) * n + (pl.BlockSpec(memory_space=pltpu.VMEM),),
        input_output_aliases={i: 2 + i for i in range(n)},
        compiler_params=pltpu.CompilerParams(has_side_effects=_DATAFLOW),
    )(*lands, handle["recv_ici"], after)
    return dict(handle, fwd_send=outs[0], fwd_recv=outs[1], lands=list(outs[2:2 + n])), outs[-1]


def _gather2_wait(handle, kinds, after, *, name):
    lands = handle["lands"]
    n = len(lands)

    def body(*refs):
        land_r, (send1, recv_sib, fwd_send, fwd_recv) = refs[:n], refs[n:n + 4]
        me = _my_index()
        sibling, sib_idx = _rel_peer(_SIBLING)
        for a in range(n):
            own = _slab(kinds[a], land_r[a], me)
            for j, r in enumerate((_SIBLING,) + _ICI):
                _rcopy(own, send1.at[4 * a + j], recv_sib.at[a], _rel_peer(r)[0]).wait_send()
            theirs = _slab(kinds[a], land_r[a], sib_idx)
            _rcopy(theirs, send1.at[4 * a], recv_sib.at[a], sibling).wait_recv()
            for j, r in enumerate(_ICI):
                passed_on = _slab(kinds[a], land_r[a], _rel_peer(r)[1])
                _rcopy(passed_on, fwd_send.at[3 * a + j], fwd_recv.at[3 * a + j], sibling).wait_send()
                arrived = _slab(kinds[a], land_r[a], _rel_peer(r ^ _SIBLING)[1])
                _rcopy(arrived, fwd_send.at[3 * a + j], fwd_recv.at[3 * a + j], sibling).wait_recv()

    outs = pl.pallas_call(
        body, name=name, out_shape=[pltpu.HBM(a.shape, a.dtype) for a in lands],
        in_specs=[HBM_SPEC] * n + [SEM_SPEC] * 4 + [ANY], out_specs=[HBM_SPEC] * n,
        input_output_aliases={i: i for i in range(n)},
        compiler_params=pltpu.CompilerParams(has_side_effects=_DATAFLOW),
    )(*lands, handle["send1"], handle["recv_sib"], handle["fwd_send"], handle["fwd_recv"], after)
    return list(outs)


def _sum_slots(gathered, *, name):
    def body(g_ref, out_ref):
        total = g_ref[0]
        for dev in range(1, N_DEV):
            total = total + g_ref[dev]
        out_ref[...] = total

    return _pcall(body, out_shape=SDS(gathered.shape[1:], F32), compiler_params=_cp(), name=name)(gathered)


def _adam(g, w, m, v):
    nm = ADAM_B1 * m + (1.0 - ADAM_B1) * g
    nv = ADAM_B2 * v + (1.0 - ADAM_B2) * (g * g)
    m_hat = nm / (1.0 - ADAM_B1 ** ADAM_STEP)
    v_hat = nv / (1.0 - ADAM_B2 ** ADAM_STEP)
    return -ADAM_LR * (m_hat / (jnp.sqrt(v_hat) + ADAM_EPS) + ADAM_WD * w), nm, nv


def _adamw_staged(st0, st1, w, m, v, *, name):
    _, rows, cols = w.shape
    st_cols = st0.shape[2]
    tr = max(t for t in range(16, 129, 16) if rows % t == 0)
    nr = rows // tr

    def body(s0_ref, s1_ref, w_ref, m_ref, v_ref, g_ref, d_ref, nm_ref, nv_ref):
        for layer, s_ref in enumerate((s0_ref, s1_ref)):
            @pl.when(pl.program_id(0) == layer)
            def _(s_ref=s_ref):
                total = s_ref[0, :, 0:cols].astype(F32)
                for dev in range(1, N_DEV):
                    total = total + s_ref[dev, :, 0:cols].astype(F32)
                g_ref[0] = total

        d_ref[0], nm_ref[0], nv_ref[0] = _adam(g_ref[0], w_ref[0], m_ref[0], v_ref[0])

    st_spec = lambda layer: pl.BlockSpec(
        (N_DEV, tr, st_cols), lambda l, i: (0, jnp.where(l == layer, i, (nr - 1) * (1 - layer)), 0))
    par = pl.BlockSpec((1, tr, cols), lambda l, i: (l, i, 0))
    return _pcall(
        body, grid=(DEPTH, nr), in_specs=[st_spec(0), st_spec(1), par, par, par], out_specs=[par] * 4,
        out_shape=[SDS(w.shape, F32)] * 4,
        compiler_params=_cp(("arbitrary", "arbitrary")), name=name)(st0, st1, w, m, v)


def _adamw_small(g, w, m, v, *, name):
    def body(g_ref, w_ref, m_ref, v_ref, d_ref, nm_ref, nv_ref):
        d_ref[...], nm_ref[...], nv_ref[...] = _adam(g_ref[...], w_ref[...], m_ref[...], v_ref[...])

    return _pcall(body, out_shape=[SDS(g.shape, F32)] * 3, compiler_params=_cp(), name=name)(g, w, m, v)


def _pack_rows(arrays):
    flat = jnp.concatenate([a.reshape(-1).astype(F32) for a in arrays])
    rows = -(-flat.shape[0] // (8 * D)) * 8
    return jnp.pad(flat, (0, rows * D - flat.shape[0])).reshape(rows, D)


def _unpack_rows(pack, like):
    flat = pack.reshape(-1)
    out, at = [], 0
    for a in like:
        out.append(flat[at:at + a.size].reshape(a.shape))
        at += a.size
    return out


def _layer_fwd(x, h, mem, win, rest_of_weights, after_up, small, g_next, tag):
    proj = _mm(h, win, tm=1024, tn=1536, name=f"proj_{tag}")
    wup, wkv, w4, wdn, cw_a, cw_b, cw_f = rest_of_weights(proj)
    za = _bra_fwd(proj, cw_a, name=f"bra_fwd_{tag}")
    cb = _brb_conv_fwd(proj, cw_b, small["conv_b_bias"], name=f"brb_conv_fwd_{tag}")
    sb = _ln_silu_fwd(cb, small["ln_b_g"], small["ln_b_b"], name=f"ln_silu_fwd_{tag}")
    memn, kv = _kv_prep(mem, small["norm_mem_g"], wkv, name=f"kv_prep_{tag}")
    o = _attn_fwd(proj, kv, name=f"attn_fwd_{tag}")
    ya, yb, yc, mg, x1, h2 = _mix_out(x, za, sb, o, proj, w4, small["b_gate"], small["norm_ffn_g"],
                                      name=f"mix_out_{tag}")
    u2 = _mm(h2, wup, tm=1024, tn=1536, name=f"up_{tag}")
    token = after_up(u2)
    act, c2 = _ffn_act(u2, cw_f if token is None else _behind(cw_f, token), name=f"ffn_act_{tag}")
    x2, h_next = _mm_res_norm(act, wdn, x1, g_next, name=f"down_{tag}")
    saved = dict(x=x, h=h, proj=proj, za=za, cb=cb, sb=sb, memn=memn, kv=kv, o=o, ya=ya, yb=yb, yc=yc,
                 mg=mg, x1=x1, h2=h2, u2=u2, c2=c2, act=act)
    return x2, h_next, (win, wup, wkv, w4, wdn, cw_a, cw_b, cw_f), saved


def _behind(operand, token):
    return operand + token[0:1, 0:1]


def _layer_bwd(dx2, dx2b, mem, wts, small, sv, start, tag):
    win, wup, wkv, w4, wdn, cw_a, cw_b, cw_f = wts
    dact = _mm(dx2b, wdn, tb=True, tm=1024, tn=768, name=f"d_act_{tag}")
    dwdn = _mm(sv["act"], dx2b, ta=True, tm=768, tn=1024, name=f"dw_down_{tag}")
    du2, dcw_f = _ffn_bwd(sv["u2"], sv["c2"], dact, cw_f, name=f"ffn_bwd_{tag}")
    dwup = _mm(sv["h2"], du2, ta=True, tm=1024, tn=768, name=f"dw_up_{tag}")
    token = start(("wdn", "wup"), (dwdn, dwup), f"ffn_{tag}")
    dx1, dx1b, dg_ffn = _mm_nt_normbwd(du2, wup, sv["x1"], dx2, _behind(small["norm_ffn_g"], token),
                                       name=f"d_h2_{tag}")

    dya, dyb, dyc, dza, dsb, do, dproj, dbg = _mix_bwd(dx1b, sv["ya"], sv["yb"], sv["yc"], sv["proj"], w4,
                                                      small["b_gate"], name=f"mix_bwd_{tag}")
    dw4 = jnp.stack([
        _mm(a, b, ta=True, tm=1024, tn=512, name=f"dw_{nm}_{tag}")
        for nm, a, b in (("a_out", sv["za"], dya), ("b_out", sv["sb"], dyb), ("att_out", sv["o"], dyc),
                         ("o", sv["mg"], dx1b))])
    dq, dk, dv = _attn_bwd(sv["proj"], sv["kv"], do, name=f"attn_bwd_{tag}")
    dwkv, dg_mem = _kv_bwd(mem, small["norm_mem_g"], sv["memn"], dk, dv, wkv, name=f"kv_bwd_{tag}")
    token = start(("w4", "wkv"), (dw4, dwkv), f"mix_{tag}")
    dproj, dcw_a = _bra_bwd(sv["proj"], dza, _behind(cw_a, token), dproj, name=f"bra_bwd_{tag}")
    dcb, ln_sums = _ln_silu_bwd(sv["cb"], dsb, small["ln_b_g"], small["ln_b_b"], name=f"ln_silu_bwd_{tag}")
    dproj, dcw_b = _brb_conv_bwd(sv["proj"], dcb, dq, cw_b, dproj, name=f"brb_conv_bwd_{tag}")
    dwin = _mm(sv["h"], dproj, ta=True, tm=1024, tn=768, name=f"dw_in_{tag}")
    token = start(("win",), (dwin,), f"in_{tag}")
    dx, dxb, dg_mix = _mm_nt_normbwd(dproj, win, sv["x"], dx1, _behind(small["norm_mix_g"], token),
                                     tk=4608, name=f"d_h_{tag}")

    small_grads = [dg_mix[0:1], dg_mem[0:1], dbg[0:1].reshape(3, D), ln_sums[2:3], ln_sums[0:1], ln_sums[1:2],
                   dg_ffn[0:1], dcw_a[0:K_A], dcw_b[0:K_B], dcw_f[0:K_F].reshape(K_F * 2 * FF_P // D, D)]
    return dx, dxb, small_grads, token


_SMALL_ROWS = (1, 1, 3, 1, 1, 1, 1, K_A, K_B, K_F * 2 * FF_P // D)
_CV_ROWS = 48


def kernel(x, mem, norm_mix_g, norm_mem_g, w_in, b_gate, conv_a_w, w_a_out, conv_b_w, conv_b_bias, ln_b_g, ln_b_b, w_b_out, w_kv, w_att_out, w_o, norm_ffn_g, w_up, conv_ffn_w, w_down, norm_final_g, loss_target, m_norm_mix_g, m_norm_mem_g, m_w_in, m_b_gate, m_conv_a_w, m_w_a_out, m_conv_b_w, m_conv_b_bias, m_ln_b_g, m_ln_b_b, m_w_b_out, m_w_kv, m_w_att_out, m_w_o, m_norm_ffn_g, m_w_up, m_conv_ffn_w, m_w_down, m_norm_final_g, v_norm_mix_g, v_norm_mem_g, v_w_in, v_b_gate, v_conv_a_w, v_w_a_out, v_conv_b_w, v_conv_b_bias, v_ln_b_g, v_ln_b_b, v_w_b_out, v_w_kv, v_w_att_out, v_w_o, v_norm_ffn_g, v_w_up, v_conv_ffn_w, v_w_down, v_norm_final_g):
    me = _my_index()
    me_arr = me.astype(jnp.int32).reshape(1)
    x0, mem0, tgt = x.reshape(x.shape[1:]), mem.reshape(mem.shape[1:]), loss_target.reshape(x.shape[1:])
    up_pad = ((0, 0), (0, 0), (0, C_UP_P - C_UP))

    ag_groups = (("win",), ("wup", "wkv", "w4", "wdn", "cv"))
    kinds = ag_groups[0] + ag_groups[1]
    smalls, ag_handles = [], []
    token = jnp.zeros((8, LANE), F32)
    for l in range(DEPTH):
        cv = jnp.zeros((_CV_ROWS, C_UP_P), F32)
        cv = cv.at[0:K_F, 0:C_UP].set(conv_ffn_w[l]).at[3:3 + K_A, 0:R_O].set(conv_a_w[l])
        cv = cv.at[8:8 + K_B, 0:R_O].set(conv_b_w[l])
        shards = dict(
            win=w_in[l], wup=jnp.pad(w_up[l], up_pad[1:]), wkv=w_kv[l],
            w4=jnp.stack([w_a_out[l], w_b_out[l], w_att_out[l], w_o[l]]), wdn=w_down[l], cv=cv)
        whole = dict({k: SDS(*_WHOLE[k]) for k in kinds[:-1]}, cv=SDS((N_DEV,) + cv.shape, F32))
        lands = {k: _place_own(k, shards[k], whole[k], True, me_arr, name=f"ag_own_{k}_l{l}") for k in kinds}
        per_layer = []
        for g, grp in enumerate(ag_groups):
            handle, token = _gather2_start([lands[k] for k in grp], grp, token, name=f"ag_start_l{l}_g{g}")
            per_layer.append(handle)
        ag_handles.append(per_layer)
        smalls.append(dict(
            norm_mix_g=norm_mix_g[l][None], norm_mem_g=norm_mem_g[l][None], b_gate=b_gate[l][None],
            conv_b_bias=conv_b_bias[l][None], ln_b_g=ln_b_g[l][None], ln_b_b=ln_b_b[l][None],
            norm_ffn_g=norm_ffn_g[l][None]))

    def forward_group(l, g, after):
        ag_handles[l][g], tok = _gather2_forward(ag_handles[l][g], ag_groups[g], after, name=f"ag_forward_l{l}_g{g}")
        return tok

    def rest_of_weights(l):
        def wait(after):
            if l == 0:
                after = forward_group(0, 1, after)
            wup, wkv, w4, wdn, cvg = _gather2_wait(ag_handles[l][1], ag_groups[1], after, name=f"ag_wait_l{l}_g1")
            cw_f = jnp.stack([cvg[d, 0:K_F, :] for d in UP_ORDER], axis=1).reshape(K_F, 2 * FF_P)
            cw_a = cvg[:, 3:3 + K_A, 0:R_O].transpose(1, 0, 2).reshape(K_A, D)
            cw_b = cvg[:, 8:8 + K_B, 0:R_O].transpose(1, 0, 2).reshape(K_B, D)
            return (wup, wkv, w4, wdn, jnp.pad(cw_a, ((0, 8 - K_A), (0, 0))),
                    jnp.pad(cw_b, ((0, 32 - K_B), (0, 0))), jnp.pad(cw_f, ((0, 8 - K_F), (0, 0))))
        return wait

    wts, saved = [], []
    xs = x0
    h = _rms_fwd(xs, smalls[0]["norm_mix_g"], name="rms_fwd")
    behind = forward_group(0, 0, token)

    def next_layer_forwarding(l):
        def hook(after):
            if l + 1 == DEPTH:
                return None
            return forward_group(l + 1, 1, forward_group(l + 1, 0, after))
        return hook

    for l in range(DEPTH):
        g_next = smalls[l + 1]["norm_mix_g"] if l + 1 < DEPTH else norm_final_g[None]
        (win,) = _gather2_wait(ag_handles[l][0], ag_groups[0], behind, name=f"ag_wait_l{l}_g0")
        xs, h, w_l, sv = _layer_fwd(xs, h, mem0, win, rest_of_weights(l), next_layer_forwarding(l), smalls[l],
                                    g_next, f"l{l}")
        behind = h
        wts.append(w_l)
        saved.append(sv)
    dx, dxb, head_sums = _loss_head(xs, tgt, norm_final_g[None], name="loss_head")

    rs_handles = []
    small_grads = [None] * DEPTH

    def start_scatter(grp, arrays, name):
        maps = _scatter_maps(grp)
        lands = [_place_own(k, a, SDS((N_DEV,) + _SHARD[k], BF), False, me_arr, name=f"rs_own_{k}_{name}")
                 for k, a in zip(grp, arrays)]
        handle, tok = _exchange_start(list(arrays), lands, maps, rs_handles[-1][2] if rs_handles else head_sums,
                                      name=f"rs_start_{name}")
        rs_handles.append((grp, handle, tok, name))
        return tok

    for l in reversed(range(DEPTH)):
        dx, dxb, small_grads[l], token = _layer_bwd(dx, dxb, mem0, wts[l], smalls[l], saved[l], start_scatter,
                                                    f"l{l}")

    pack = jnp.concatenate(small_grads[0] + small_grads[1] + [head_sums[1:2], head_sums[0:1]], axis=0)
    pack = jnp.pad(pack, ((0, -pack.shape[0] % 8), (0, 0)))
    small_maps = (lambda srcs, lands, a, idx: srcs[a]), (lambda lands, a, idx: lands[a].at[idx])
    small_land = _place_own("cv", pack, SDS((N_DEV,) + pack.shape, F32), True, me_arr, name="small_own")
    small_handle, small_token = _exchange_start([pack], [small_land], small_maps, dx, name="small_start")

    staged = [dict() for _ in range(DEPTH)]
    for grp, handle, _, name in rs_handles[:-1]:
        staged[int(name[-1])].update(zip(grp, _exchange_wait(handle, _scatter_maps(grp), small_token,
                                                             name=f"rs_wait_{name}")))

    def big_update(kind, w, m, v, name):
        return _adamw_staged(staged[0][kind], staged[1][kind], w, m, v, name=name)

    r_up = big_update("wup", w_up, m_w_up, v_w_up, "adamw_w_up")
    r_kv = big_update("wkv", w_kv, m_w_kv, v_w_kv, "adamw_w_kv")
    r_dn = big_update("wdn", w_down, m_w_down, v_w_down, "adamw_w_down")

    def four(a, b, c, d_):
        return jnp.stack([a, b, c, d_], axis=1).reshape(DEPTH, 4 * R_O, D)

    r_4 = _adamw_staged(
        staged[0]["w4"].reshape(N_DEV, 4 * R_O, D), staged[1]["w4"].reshape(N_DEV, 4 * R_O, D),
        four(w_a_out, w_b_out, w_att_out, w_o), four(m_w_a_out, m_w_b_out, m_w_att_out, m_w_o),
        four(v_w_a_out, v_w_b_out, v_w_att_out, v_w_o), name="adamw_w_out")
    grp, handle, _, name = rs_handles[-1]
    staged[0].update(zip(grp, _exchange_wait(handle, _scatter_maps(grp), r_4[0], name=f"rs_wait_{name}")))
    r_in = big_update("win", w_in, m_w_in, v_w_in, "adamw_w_in")
    r_a, r_b, r_att, r_o = ([a.reshape(DEPTH, 4, R_O, D)[:, j] for a in r_4] for j in range(4))

    (gathered,) = _exchange_wait(small_handle, small_maps, r_in[0], name="small_wait")
    total = _sum_slots(gathered, name="small_sum")
    per_layer = sum(_SMALL_ROWS)
    parts = []
    for l in range(DEPTH):
        at, one = l * per_layer, []
        for rows in _SMALL_ROWS:
            one.append(total[at:at + rows])
            at += rows
        parts.append(one)
    g_final = total[DEPTH * per_layer]
    loss = 0.5 / D * jnp.sum(total[DEPTH * per_layer + 1])

    def both(i):
        return jnp.stack([parts[0][i], parts[1][i]])

    g_norm_mix, g_norm_mem = both(0)[:, 0], both(1)[:, 0]
    g_b_gate = both(2).reshape(DEPTH, 3 * D)
    g_cbias, g_lng, g_lnb, g_norm_ffn = both(3)[:, 0], both(4)[:, 0], both(5)[:, 0], both(6)[:, 0]
    g_conv_a = lax.dynamic_slice_in_dim(both(7), me * R_O, R_O, axis=2)
    g_conv_b = lax.dynamic_slice_in_dim(both(8), me * R_O, R_O, axis=2)
    g_conv_f = lax.dynamic_slice_in_dim(both(9).reshape(DEPTH, K_F, 2 * FF_P), _up_slot(me) * C_UP_P, C_UP, axis=2)

    small_g = [g_norm_mix, g_norm_mem, g_b_gate, g_conv_a, g_conv_b, g_cbias, g_lng, g_lnb, g_norm_ffn, g_conv_f,
               g_final]
    small_w = [norm_mix_g, norm_mem_g, b_gate, conv_a_w, conv_b_w, conv_b_bias, ln_b_g, ln_b_b, norm_ffn_g,
               conv_ffn_w, norm_final_g]
    small_m = [m_norm_mix_g, m_norm_mem_g, m_b_gate, m_conv_a_w, m_conv_b_w, m_conv_b_bias, m_ln_b_g, m_ln_b_b,
               m_norm_ffn_g, m_conv_ffn_w, m_norm_final_g]
    small_v = [v_norm_mix_g, v_norm_mem_g, v_b_gate, v_conv_a_w, v_conv_b_w, v_conv_b_bias, v_ln_b_g, v_ln_b_b,
               v_norm_ffn_g, v_conv_ffn_w, v_norm_final_g]
    upd = _adamw_small(_pack_rows(small_g), _pack_rows(small_w), _pack_rows(small_m), _pack_rows(small_v),
                       name="adamw_small")
    s_d, s_m, s_v = (_unpack_rows(p, small_w) for p in upd)
    (d_norm_mix, d_norm_mem, d_b_gate, d_conv_a, d_conv_b, d_cbias, d_lng, d_lnb, d_norm_ffn, d_conv_f,
     d_final) = s_d
    (nm_norm_mix, nm_norm_mem, nm_b_gate, nm_conv_a, nm_conv_b, nm_cbias, nm_lng, nm_lnb, nm_norm_ffn, nm_conv_f,
     nm_final) = s_m
    (nv_norm_mix, nv_norm_mem, nv_b_gate, nv_conv_a, nv_conv_b, nv_cbias, nv_lng, nv_lnb, nv_norm_ffn, nv_conv_f,
     nv_final) = s_v

    grads = [g_norm_mix, g_norm_mem, r_in[0], g_b_gate, g_conv_a, r_a[0], g_conv_b, g_cbias, g_lng, g_lnb, r_b[0],
             r_kv[0], r_att[0], r_o[0], g_norm_ffn, r_up[0], g_conv_f, r_dn[0], g_final]
    deltas = [d_norm_mix, d_norm_mem, r_in[1], d_b_gate, d_conv_a, r_a[1], d_conv_b, d_cbias, d_lng, d_lnb, r_b[1],
              r_kv[1], r_att[1], r_o[1], d_norm_ffn, r_up[1], d_conv_f, r_dn[1], d_final]
    new_m = [nm_norm_mix, nm_norm_mem, r_in[2], nm_b_gate, nm_conv_a, r_a[2], nm_conv_b, nm_cbias, nm_lng, nm_lnb,
             r_b[2], r_kv[2], r_att[2], r_o[2], nm_norm_ffn, r_up[2], nm_conv_f, r_dn[2], nm_final]
    new_v = [nv_norm_mix, nv_norm_mem, r_in[3], nv_b_gate, nv_conv_a, r_a[3], nv_conv_b, nv_cbias, nv_lng, nv_lnb,
             r_b[3], r_kv[3], r_att[3], r_o[3], nv_norm_ffn, r_up[3], nv_conv_f, r_dn[3], nv_final]
    return (loss, dx[None], *grads, *deltas, *new_m, *new_v)
```

```python
import functools

import jax
import jax.numpy as jnp
import numpy as np
from jax import lax
from jax.experimental import pallas as pl
from jax.experimental.pallas import tpu as pltpu

F32 = jnp.float32
BF = jnp.bfloat16
SDS = jax.ShapeDtypeStruct
MESH = pl.DeviceIdType.MESH
ANY = pl.BlockSpec(memory_space=pl.ANY)

N_DEV = 8
DEPTH = 2
D = 1024
N_HEADS = 4
HEAD = D // N_HEADS
D_FF = 2816
K_A, K_B, K_F = 3, 31, 3
NORM_EPS = 1e-6

C_IN = 9 * D // N_DEV
C_KV = 2 * D // N_DEV
C_UP = 2 * D_FF // N_DEV
LANE = 128
C_UP_P = -(-C_UP // LANE) * LANE
FF_P = 4 * C_UP_P
R_O = D // N_DEV
R_DN = D_FF // N_DEV

VMEM_LIMIT = 56 * 1024 * 1024
TM = 512
TR = 256
SUB = 128
H_S, H_L = 16, 32

ADAM_LR, ADAM_B1, ADAM_B2, ADAM_EPS, ADAM_WD, ADAM_STEP = 0.001, 0.9, 0.999, 1e-08, 0.01, 10

UP_ORDER = (0, 4, 1, 5, 2, 6, 3, 7)


def _pcall(body, **kw):
    return pl.pallas_call(body, **kw)


def _cp(sem=None, **kw):
    return pltpu.CompilerParams(dimension_semantics=sem, vmem_limit_bytes=VMEM_LIMIT, **kw)


def _dot(a, b):
    return jnp.dot(a, b, preferred_element_type=F32)


def _dot_nt(a, b):
    return lax.dot_general(a, b, (((1,), (1,)), ((), ())), preferred_element_type=F32)


def _dot_tn(a, b):
    return lax.dot_general(a, b, (((0,), (0,)), ((), ())), preferred_element_type=F32)


def _sigmoid(z):
    return 1.0 / (1.0 + jnp.exp(-z))


def _rms(xv):
    return lax.rsqrt(jnp.mean(xv * xv, axis=-1, keepdims=True) + NORM_EPS)


def _up_slot(idx):
    return jnp.where(idx < 4, 2 * idx, 2 * (idx - 4) + 1)


def _dn_row(idx):
    return C_UP_P * (idx // 2) + R_DN * (idx % 2)


def _mm(a, b, *, ta=False, tb=False, out_dtype=BF, tm=TM, tn=512, tk=None, name):
    m, k_dim = (a.shape[1], a.shape[0]) if ta else a.shape
    n = b.shape[0] if tb else b.shape[1]
    tm, tn = min(tm, m), min(tn, n)
    tk = k_dim if tk is None else min(tk, k_dim)
    nk = k_dim // tk
    assert m % tm == 0 and n % tn == 0 and k_dim % tk == 0
    dims = (((0 if ta else 1,), (1 if tb else 0,)), ((), ()))

    def body(a_ref, b_ref, o_ref, *scratch):
        part = lax.dot_general(a_ref[...], b_ref[...], dims, preferred_element_type=F32)
        if nk == 1:
            o_ref[...] = part.astype(o_ref.dtype)
            return
        acc = scratch[0]
        k = pl.program_id(2)

        @pl.when(k == 0)
        def _():
            acc[...] = part

        @pl.when(k > 0)
        def _():
            acc[...] += part

        @pl.when(k == nk - 1)
        def _():
            o_ref[...] = acc[...].astype(o_ref.dtype)

    a_spec = pl.BlockSpec((tk, tm), lambda i, j, k: (k, i)) if ta else pl.BlockSpec((tm, tk), lambda i, j, k: (i, k))
    b_spec = pl.BlockSpec((tn, tk), lambda i, j, k: (j, k)) if tb else pl.BlockSpec((tk, tn), lambda i, j, k: (k, j))
    return _pcall(
        body, grid=(m // tm, n // tn, nk), in_specs=[a_spec, b_spec],
        out_specs=pl.BlockSpec((tm, tn), lambda i, j, k: (i, j)),
        out_shape=SDS((m, n), out_dtype),
        scratch_shapes=[pltpu.VMEM((tm, tn), F32)] if nk > 1 else [],
        compiler_params=_cp(("parallel", "parallel", "arbitrary")), name=name)(a, b)


def _mm_res_norm(a, w, x, g, *, name):
    s, k_dim = a.shape
    tm = min(TM, s)

    def body(a_ref, w_ref, x_ref, g_ref, xo_ref, h_ref):
        xo = x_ref[...] + _dot(a_ref[...], w_ref[...])
        xo_ref[...] = xo
        h_ref[...] = ((xo * _rms(xo)) * g_ref[...]).astype(BF)

    return _pcall(
        body, grid=(s // tm,),
        in_specs=[pl.BlockSpec((tm, k_dim), lambda i: (i, 0)),
                  pl.BlockSpec((k_dim, D), lambda i: (0, 0), pipeline_mode=pl.Buffered(1)),
                  pl.BlockSpec((tm, D), lambda i: (i, 0)), pl.BlockSpec((1, D), lambda i: (0, 0))],
        out_specs=[pl.BlockSpec((tm, D), lambda i: (i, 0))] * 2,
        out_shape=[SDS((s, D), F32), SDS((s, D), BF)],
        compiler_params=_cp(("parallel",)), name=name)(a, w, x, g)


def _mm_nt_normbwd(da, w, x, dres, g, *, tk=None, name):
    s, k_dim = da.shape
    tm = min(TM, s)
    tk = k_dim if tk is None else tk
    nk = k_dim // tk
    assert k_dim % tk == 0

    def body(da_ref, w_ref, x_ref, dres_ref, g_ref, dx_ref, dxb_ref, dg_ref, *scratch):
        i, k = pl.program_id(0), pl.program_id(1)
        part = _dot_nt(da_ref[...], w_ref[...])
        if nk > 1:
            acc = scratch[0]

            @pl.when(k == 0)
            def _():
                acc[...] = part

            @pl.when(k > 0)
            def _():
                acc[...] += part

        @pl.when((i == 0) & (k == 0))
        def _():
            dg_ref[...] = jnp.zeros_like(dg_ref)

        @pl.when(k == nk - 1)
        def _():
            dh = acc[...] if nk > 1 else part
            xv = x_ref[...]
            r = _rms(xv)
            xn = xv * r
            dg_ref[0:1, :] += jnp.sum(dh * xn, axis=0, keepdims=True)
            dxn = dh * g_ref[...]
            dx = dres_ref[...] + r * (dxn - xn * jnp.mean(dxn * xn, axis=-1, keepdims=True))
            dx_ref[...] = dx
            dxb_ref[...] = dx.astype(BF)

    row = lambda i, k: (i, 0)
    w_spec = (pl.BlockSpec((D, tk), lambda i, k: (0, k)) if nk > 1 else
              pl.BlockSpec((D, tk), lambda i, k: (0, 0), pipeline_mode=pl.Buffered(1)))
    return _pcall(
        body, grid=(s // tm, nk),
        in_specs=[pl.BlockSpec((tm, tk), lambda i, k: (i, k)), w_spec,
                  pl.BlockSpec((tm, D), row), pl.BlockSpec((tm, D), row), pl.BlockSpec((1, D), lambda i, k: (0, 0))],
        out_specs=[pl.BlockSpec((tm, D), row), pl.BlockSpec((tm, D), row), pl.BlockSpec((8, D), lambda i, k: (0, 0))],
        out_shape=[SDS((s, D), F32), SDS((s, D), BF), SDS((8, D), F32)],
        scratch_shapes=[pltpu.VMEM((tm, D), F32)] if nk > 1 else [],
        compiler_params=_cp(("arbitrary", "arbitrary")), name=name)(da, w, x, dres, g)


def _rms_fwd(x, g, *, name):
    s = x.shape[0]
    tm = min(TM, s)

    def body(x_ref, g_ref, h_ref):
        xv = x_ref[...]
        h_ref[...] = ((xv * _rms(xv)) * g_ref[...]).astype(BF)

    return _pcall(
        body, grid=(s // tm,),
        in_specs=[pl.BlockSpec((tm, D), lambda i: (i, 0)), pl.BlockSpec((1, D), lambda i: (0, 0))],
        out_specs=pl.BlockSpec((tm, D), lambda i: (i, 0)), out_shape=SDS((s, D), BF),
        compiler_params=_cp(("parallel",)), name=name)(x, g)


def _loss_head(x, tgt, g, *, name):
    s = x.shape[0]
    tm = min(TM, s)

    def body(x_ref, t_ref, g_ref, dx_ref, dxb_ref, sums_ref):
        @pl.when(pl.program_id(0) == 0)
        def _():
            sums_ref[...] = jnp.zeros_like(sums_ref)

        xv = x_ref[...]
        r = _rms(xv)
        xn = xv * r
        diff = xn * g_ref[...] - t_ref[...]
        sums_ref[0:1, :] += jnp.sum(diff * diff, axis=0, keepdims=True)
        dy = diff * (1.0 / D)
        sums_ref[1:2, :] += jnp.sum(dy * xn, axis=0, keepdims=True)
        dxn = dy * g_ref[...]
        dx = r * (dxn - xn * jnp.mean(dxn * xn, axis=-1, keepdims=True))
        dx_ref[...] = dx
        dxb_ref[...] = dx.astype(BF)

    row = lambda i: (i, 0)
    return _pcall(
        body, grid=(s // tm,),
        in_specs=[pl.BlockSpec((tm, D), row), pl.BlockSpec((tm, D), row), pl.BlockSpec((1, D), lambda i: (0, 0))],
        out_specs=[pl.BlockSpec((tm, D), row), pl.BlockSpec((tm, D), row), pl.BlockSpec((8, D), lambda i: (0, 0))],
        out_shape=[SDS((s, D), F32), SDS((s, D), BF), SDS((8, D), F32)],
        compiler_params=_cp(("arbitrary",)), name=name)(x, tgt, g)


def _halo_before(i, tr, h):
    return jnp.maximum(i * (tr // h) - 1, 0)


def _halo_after(i, tr, h, s):
    return jnp.minimum((i + 1) * (tr // h), s // h - 1)


def _taps(buf, w_ref, sl, k_w, base, rows):
    acc = None
    for k in range(k_w):
        t = w_ref[k:k + 1, sl] * buf[base + k:base + k + rows, sl]
        acc = t if acc is None else acc + t
    return acc


def _taps_rev(buf, w_ref, sl, k_w, base, rows):
    acc = None
    for k in range(k_w):
        t = w_ref[k:k + 1, sl] * buf[base + k_w - 1 - k:base + k_w - 1 - k + rows, sl]
        acc = t if acc is None else acc + t
    return acc


def _tap_grads(dw_ref, dc, buf, sl, k_w, base, rows):
    for k in range(k_w):
        dw_ref[k:k + 1, sl] += jnp.sum(dc * buf[base + k:base + k + rows, sl], axis=0, keepdims=True)


def _bra_fwd(proj, cw, *, name):
    s = proj.shape[0]
    tr, h = min(TR, s), H_S
    sub = min(SUB, tr)

    def body(cur, halo, w_ref, za_ref, cvb):
        i = pl.program_id(0)
        hv = halo[:, D:2 * D].astype(F32) * halo[:, 2 * D:3 * D].astype(F32)
        cvb[0:h, :] = jnp.where(i == 0, 0.0, hv)
        cvb[h:h + tr, :] = cur[:, D:2 * D].astype(F32) * cur[:, 2 * D:3 * D].astype(F32)
        for c in range(D // LANE):
            sl = slice(LANE * c, LANE * c + LANE)
            ca = _taps(cvb, w_ref, sl, K_A, h - (K_A - 1), tr)
            za_ref[:, sl] = (cur[:, sl].astype(F32) * ca).astype(BF)

    return _pcall(
        body, grid=(s // tr,),
        in_specs=[pl.BlockSpec((tr, 3 * D), lambda i: (i, 0)),
                  pl.BlockSpec((h, 3 * D), lambda i: (_halo_before(i, tr, h), 0)),
                  pl.BlockSpec((8, D), lambda i: (0, 0))],
        out_specs=pl.BlockSpec((tr, D), lambda i: (i, 0)), out_shape=SDS((s, D), BF),
        scratch_shapes=[pltpu.VMEM((h + tr, D), F32)],
        compiler_params=_cp(("parallel",)), name=name)(proj, proj, cw)


def _bra_bwd(proj, dza, cw, dproj, *, name):
    s = proj.shape[0]
    tr, h = min(TR, s), H_S
    sub = min(SUB, tr)
    n = s // tr

    def body(before, cur, after, dz_cur, dz_after, w_ref, dproj_in, da_ref, dw_ref, cvb, dcab):
        del dproj_in
        i = pl.program_id(0)

        @pl.when(i == 0)
        def _():
            dw_ref[...] = jnp.zeros_like(dw_ref)

        first, last = i == 0, i == n - 1
        cvb[0:h, :] = jnp.where(first, 0.0, before[:, D:2 * D].astype(F32) * before[:, 2 * D:3 * D].astype(F32))
        cvb[h:h + tr, :] = cur[:, D:2 * D].astype(F32) * cur[:, 2 * D:3 * D].astype(F32)
        dcab[0:tr, :] = dz_cur[...].astype(F32) * cur[:, 0:D].astype(F32)
        dcab[tr:tr + h, :] = jnp.where(last, 0.0, dz_after[...].astype(F32) * after[:, 0:D].astype(F32))
        for c in range(D // LANE):
            sl = slice(LANE * c, LANE * c + LANE)
            gl, vl = slice(D + LANE * c, D + LANE * c + LANE), slice(2 * D + LANE * c, 2 * D + LANE * c + LANE)
            for r0 in range(0, tr, sub):
                rows = slice(r0, r0 + sub)
                ca = _taps(cvb, w_ref, sl, K_A, h - (K_A - 1) + r0, sub)
                da_ref[rows, sl] = (dz_cur[rows, sl].astype(F32) * ca).astype(BF)
                dcv = _taps_rev(dcab, w_ref, sl, K_A, r0, sub)
                da_ref[rows, gl] = (dcv * cur[rows, vl].astype(F32)).astype(BF)
                da_ref[rows, vl] = (dcv * cur[rows, gl].astype(F32)).astype(BF)
                _tap_grads(dw_ref, dcab[rows, sl], cvb, sl, K_A, h - (K_A - 1) + r0, sub)

    return _pcall(
        body, grid=(n,),
        in_specs=[pl.BlockSpec((h, 3 * D), lambda i: (_halo_before(i, tr, h), 0)),
                  pl.BlockSpec((tr, 3 * D), lambda i: (i, 0)),
                  pl.BlockSpec((h, 3 * D), lambda i: (_halo_after(i, tr, h, s), 0)),
                  pl.BlockSpec((tr, D), lambda i: (i, 0)),
                  pl.BlockSpec((h, D), lambda i: (_halo_after(i, tr, h, s), 0)),
                  pl.BlockSpec((8, D), lambda i: (0, 0)), ANY],
        out_specs=[pl.BlockSpec((tr, 3 * D), lambda i: (i, 0)), pl.BlockSpec((8, D), lambda i: (0, 0))],
        out_shape=[SDS(dproj.shape, BF), SDS((8, D), F32)], input_output_aliases={6: 0},
        scratch_shapes=[pltpu.VMEM((h + tr, D), F32), pltpu.VMEM((tr + h, D), F32)],
        compiler_params=_cp(("arbitrary",)), name=name)(proj, proj, proj, dza, dza, cw, dproj)


_U_COL, _UG_COL = 3, 4


def _brb_conv_fwd(proj, cw, bias, *, name):
    s = proj.shape[0]
    tr, h = min(TR, s), H_L
    sub = min(SUB, tr)

    def body(u_cur, ug_cur, u_halo, ug_halo, w_ref, b_ref, cb_ref, glb, shifted):
        i = pl.program_id(0)
        glb[0:h, :] = jnp.where(i == 0, 0.0, u_halo[...].astype(F32) * _sigmoid(ug_halo[...].astype(F32)))
        glb[h:h + tr, :] = u_cur[...].astype(F32) * _sigmoid(ug_cur[...].astype(F32))
        for c in range(D // LANE):
            sl = slice(LANE * c, LANE * c + LANE)
            for r in range(1, 8):
                shifted[r] = glb[8 - r:8 - r + tr + 24, sl]
            for r0 in range(0, tr, sub):
                acc = None
                for k in range(K_B):
                    q, r = divmod(K_B - 1 - k, 8)
                    at = r0 - 8 * q
                    win = shifted[r, 24 + at:24 + at + sub, :] if r else glb[h + at:h + at + sub, sl]
                    term = w_ref[k:k + 1, sl] * win
                    acc = term if acc is None else acc + term
                cb_ref[r0:r0 + sub, sl] = (acc + b_ref[:, sl]).astype(BF)

    return _pcall(
        body, grid=(s // tr,),
        in_specs=[pl.BlockSpec((tr, D), lambda i: (i, _U_COL)), pl.BlockSpec((tr, D), lambda i: (i, _UG_COL)),
                  pl.BlockSpec((h, D), lambda i: (_halo_before(i, tr, h), _U_COL)),
                  pl.BlockSpec((h, D), lambda i: (_halo_before(i, tr, h), _UG_COL)),
                  pl.BlockSpec((32, D), lambda i: (0, 0)), pl.BlockSpec((1, D), lambda i: (0, 0))],
        out_specs=pl.BlockSpec((tr, D), lambda i: (i, 0)), out_shape=SDS((s, D), BF),
        scratch_shapes=[pltpu.VMEM((h + tr, D), F32), pltpu.VMEM((8, tr + 24, LANE), F32)],
        compiler_params=_cp(("parallel",)), name=name)(proj, proj, proj, proj, cw, bias)


def _brb_conv_bwd(proj, dcb, dq, cw, dproj, *, name):
    s = proj.shape[0]
    tr, h = min(TR, s), H_L
    sub = min(SUB, tr)
    n = s // tr
    nb = -(-(tr + 24) // sub)
    sel = _row_selector(sub, list(range(8)))

    def body(u_cur, ug_cur, d_cur, d_after, dq_ref, sel_ref, w_ref, dproj_in, db_ref, dw_ref, dcbb, shifted):
        del dproj_in
        i = pl.program_id(0)

        @pl.when(i == 0)
        def _():
            dw_ref[...] = jnp.zeros_like(dw_ref)

        db_ref[:, 2 * D:3 * D] = dq_ref[...]
        after = d_after[...]
        dcbb[0:tr, :] = d_cur[...]
        dcbb[tr:tr + h, :] = jnp.where(i == n - 1, jnp.zeros_like(after), after)
        dcbb[tr + h:(nb + 1) * sub, :] = jnp.zeros(((nb + 1) * sub - h - tr, D), BF)
        for c in range(D // LANE):
            sl = slice(LANE * c, LANE * c + LANE)
            for blk in range(nb):
                res = _dot(sel_ref[...], dcbb[blk * sub:(blk + 2) * sub, sl])
                for r in range(8):
                    shifted[r, blk * sub:(blk + 1) * sub, :] = res[r * sub:(r + 1) * sub]
            for r0 in range(0, tr, sub):
                u = u_cur[r0:r0 + sub, sl].astype(F32)
                sg = _sigmoid(ug_cur[r0:r0 + sub, sl].astype(F32))
                glu = u * sg
                dglu = None
                for k in range(K_B):
                    q, r = divmod(K_B - 1 - k, 8)
                    at = r0 + 8 * q
                    win = shifted[r, at:at + sub, :]
                    term = w_ref[k:k + 1, sl] * win
                    dglu = term if dglu is None else dglu + term
                    dw_ref[k:k + 1, sl] += jnp.sum(win * glu, axis=0, keepdims=True)
                db_ref[r0:r0 + sub, sl] = (dglu * sg).astype(BF)
                db_ref[r0:r0 + sub, D + LANE * c:D + LANE * c + LANE] = (dglu * u * sg * (1.0 - sg)).astype(BF)

    return _pcall(
        body, grid=(n,),
        in_specs=[pl.BlockSpec((tr, D), lambda i: (i, _U_COL)), pl.BlockSpec((tr, D), lambda i: (i, _UG_COL)),
                  pl.BlockSpec((tr, D), lambda i: (i, 0)),
                  pl.BlockSpec((h, D), lambda i: (_halo_after(i, tr, h, s), 0)),
                  pl.BlockSpec((tr, D), lambda i: (i, 0)),
                  pl.BlockSpec(sel.shape, lambda i: (0, 0)),
                  pl.BlockSpec((32, D), lambda i: (0, 0)), ANY],
        out_specs=[pl.BlockSpec((tr, 3 * D), lambda i: (i, 1)), pl.BlockSpec((32, D), lambda i: (0, 0))],
        out_shape=[SDS(dproj.shape, BF), SDS((32, D), F32)], input_output_aliases={7: 0},
        scratch_shapes=[pltpu.VMEM(((nb + 1) * sub, D), BF), pltpu.VMEM((8, nb * sub, LANE), F32)],
        compiler_params=_cp(("arbitrary",)), name=name)(proj, proj, dcb, dcb, dq, sel, cw, dproj)


def _ln_silu_fwd(cb, g, b, *, name):
    s = cb.shape[0]
    tm = min(TM, s)

    def body(cb_ref, g_ref, b_ref, sb_ref):
        z = cb_ref[...].astype(F32)
        zc = z - jnp.mean(z, axis=-1, keepdims=True)
        ln = (zc * lax.rsqrt(jnp.mean(zc * zc, axis=-1, keepdims=True) + NORM_EPS)) * g_ref[...] + b_ref[...]
        sb_ref[...] = (ln * _sigmoid(ln)).astype(BF)

    row = lambda i: (i, 0)
    vec = pl.BlockSpec((1, D), lambda i: (0, 0))
    return _pcall(
        body, grid=(s // tm,), in_specs=[pl.BlockSpec((tm, D), row), vec, vec],
        out_specs=pl.BlockSpec((tm, D), row), out_shape=SDS((s, D), BF),
        compiler_params=_cp(("parallel",)), name=name)(cb, g, b)


def _ln_silu_bwd(cb, dsb, g, b, *, name):
    s = cb.shape[0]
    tm = min(TM, s)

    def body(cb_ref, dsb_ref, g_ref, b_ref, dcb_ref, sums_ref):
        @pl.when(pl.program_id(0) == 0)
        def _():
            sums_ref[...] = jnp.zeros_like(sums_ref)

        z = cb_ref[...].astype(F32)
        zc = z - jnp.mean(z, axis=-1, keepdims=True)
        rstd = lax.rsqrt(jnp.mean(zc * zc, axis=-1, keepdims=True) + NORM_EPS)
        lnh = zc * rstd
        ln = lnh * g_ref[...] + b_ref[...]
        sg = _sigmoid(ln)
        dln = dsb_ref[...].astype(F32) * (sg * (1.0 + ln * (1.0 - sg)))
        sums_ref[0:1, :] += jnp.sum(dln * lnh, axis=0, keepdims=True)
        sums_ref[1:2, :] += jnp.sum(dln, axis=0, keepdims=True)
        dlnh = dln * g_ref[...]
        dz = rstd * (dlnh - jnp.mean(dlnh, axis=-1, keepdims=True)
                     - lnh * jnp.mean(dlnh * lnh, axis=-1, keepdims=True))
        sums_ref[2:3, :] += jnp.sum(dz, axis=0, keepdims=True)
        dcb_ref[...] = dz.astype(BF)

    row = lambda i: (i, 0)
    vec = pl.BlockSpec((1, D), lambda i: (0, 0))
    return _pcall(
        body, grid=(s // tm,), in_specs=[pl.BlockSpec((tm, D), row), pl.BlockSpec((tm, D), row), vec, vec],
        out_specs=[pl.BlockSpec((tm, D), row), pl.BlockSpec((8, D), lambda i: (0, 0))],
        out_shape=[SDS((s, D), BF), SDS((8, D), F32)],
        compiler_params=_cp(("arbitrary",)), name=name)(cb, dsb, g, b)


_Q_COL = 5 * D // HEAD


def _kv_prep(mem, g, wkv, *, name):
    m = mem.shape[0]

    def body(mem_ref, g_ref, w_ref, memn_ref, kv_ref):
        mv = mem_ref[...]
        memn = ((mv * _rms(mv)) * g_ref[...]).astype(BF)
        memn_ref[...] = memn
        for dev in range(N_DEV):
            kv_ref[:, dev * C_KV:(dev + 1) * C_KV] = _dot(memn, w_ref[dev]).astype(BF)

    return _pcall(body, out_shape=[SDS((m, D), BF), SDS((m, 2 * D), BF)],
                  compiler_params=_cp(), name=name)(mem, g, wkv)


def _softmax_rows(q, k):
    sc = _dot_nt(q, k) * (1.0 / (HEAD ** 0.5))
    e = jnp.exp(sc - jnp.max(sc, axis=-1, keepdims=True))
    return e / jnp.sum(e, axis=-1, keepdims=True)


def _attn_fwd(proj, kv, *, name):
    s, m = proj.shape[0], kv.shape[0]
    tm = min(TM, s)

    def body(q_ref, kv_ref, o_ref):
        for hd in range(N_HEADS):
            cols = slice(hd * HEAD, (hd + 1) * HEAD)
            p = _softmax_rows(q_ref[:, cols], kv_ref[:, cols])
            o_ref[:, cols] = _dot(p.astype(BF), kv_ref[:, D + hd * HEAD:D + (hd + 1) * HEAD]).astype(BF)

    return _pcall(
        body, grid=(s // tm,),
        in_specs=[pl.BlockSpec((tm, D), lambda i: (i, _Q_COL // N_HEADS)),
                  pl.BlockSpec((m, 2 * D), lambda i: (0, 0))],
        out_specs=pl.BlockSpec((tm, D), lambda i: (i, 0)), out_shape=SDS((s, D), BF),
        compiler_params=_cp(("parallel",)), name=name)(proj, kv)


def _attn_bwd(proj, kv, do, *, name):
    s, m = proj.shape[0], kv.shape[0]
    tm = min(TM, s)

    def body(q_ref, kv_ref, do_ref, dq_ref, dk_ref, dv_ref):
        @pl.when(pl.program_id(0) == 0)
        def _():
            dk_ref[...] = jnp.zeros_like(dk_ref)
            dv_ref[...] = jnp.zeros_like(dv_ref)

        for hd in range(N_HEADS):
            cols = slice(hd * HEAD, (hd + 1) * HEAD)
            q, k, dov = q_ref[:, cols], kv_ref[:, cols], do_ref[:, cols]
            p = _softmax_rows(q, k)
            dp = _dot_nt(dov, kv_ref[:, D + hd * HEAD:D + (hd + 1) * HEAD])
            dv_ref[:, cols] += _dot_tn(p.astype(BF), dov)
            ds = (p * (dp - jnp.sum(dp * p, axis=-1, keepdims=True)) * (1.0 / (HEAD ** 0.5))).astype(BF)
            dq_ref[:, cols] = _dot(ds, k).astype(BF)
            dk_ref[:, cols] += _dot_tn(ds, q)

    return _pcall(
        body, grid=(s // tm,),
        in_specs=[pl.BlockSpec((tm, D), lambda i: (i, _Q_COL // N_HEADS)),
                  pl.BlockSpec((m, 2 * D), lambda i: (0, 0)),
                  pl.BlockSpec((tm, D), lambda i: (i, 0))],
        out_specs=[pl.BlockSpec((tm, D), lambda i: (i, 0)),
                   pl.BlockSpec((m, D), lambda i: (0, 0)),
                   pl.BlockSpec((m, D), lambda i: (0, 0))],
        out_shape=[SDS((s, D), BF), SDS((m, D), F32), SDS((m, D), F32)],
        compiler_params=_cp(("arbitrary",)), name=name)(proj, kv, do)


def _kv_bwd(mem, g, memn, dk, dv, wkv, *, name):
    def body(mem_ref, g_ref, memn_ref, dk_ref, dv_ref, w_ref, dw_ref, dg_ref):
        memn = memn_ref[...]
        dmemn = None
        for dev in range(N_DEV):
            d_ref, col = (dk_ref, dev) if dev < N_HEADS else (dv_ref, dev - N_HEADS)
            dslab = d_ref[:, col * C_KV:(col + 1) * C_KV].astype(BF)
            dw_ref[dev] = _dot_tn(memn, dslab).astype(BF)
            part = _dot_nt(dslab, w_ref[dev])
            dmemn = part if dmemn is None else dmemn + part
        mv = mem_ref[...]
        dg_ref[...] = jnp.zeros_like(dg_ref)
        dg_ref[0:1, :] = jnp.sum(dmemn * (mv * _rms(mv)), axis=0, keepdims=True)

    assert C_KV == HEAD
    return _pcall(body, out_shape=[SDS((N_DEV, D, C_KV), BF), SDS((8, D), F32)],
                  compiler_params=_cp(), name=name)(mem, g, memn, dk, dv, wkv)


_TM_MIX = 512


def _mix_out(x, za, sb, o, proj, w4, bg, g_next, *, name):
    s = x.shape[0]
    tm = min(_TM_MIX, s)

    def body(x_ref, za_ref, sb_ref, o_ref, pg_ref, w4_ref, bg_ref, gn_ref,
             ya_ref, yb_ref, yc_ref, mg_ref, x1_ref, h_ref):
        ys = (_dot(za_ref[...], w4_ref[0]), _dot(sb_ref[...], w4_ref[1]), _dot(o_ref[...], w4_ref[2]))
        merged = None
        for j, (y, y_ref) in enumerate(zip(ys, (ya_ref, yb_ref, yc_ref))):
            y_ref[...] = y.astype(BF)
            gate = _sigmoid(pg_ref[:, j * D:(j + 1) * D].astype(F32) + bg_ref[:, j * D:(j + 1) * D])
            merged = gate * y if merged is None else merged + gate * y
        mg = merged.astype(BF)
        mg_ref[...] = mg
        x1 = x_ref[...] + _dot(mg, w4_ref[3])
        x1_ref[...] = x1
        h_ref[...] = ((x1 * _rms(x1)) * gn_ref[...]).astype(BF)

    row = lambda i: (i, 0)
    act = pl.BlockSpec((tm, D), row)
    return _pcall(
        body, grid=(s // tm,),
        in_specs=[act, act, act, act, pl.BlockSpec((tm, 3 * D), lambda i: (i, 2)),
                  pl.BlockSpec((4, D, D), lambda i: (0, 0, 0), pipeline_mode=pl.Buffered(1)), pl.BlockSpec((1, 3 * D), lambda i: (0, 0)),
                  pl.BlockSpec((1, D), lambda i: (0, 0))],
        out_specs=[act] * 6,
        out_shape=[SDS((s, D), BF)] * 4 + [SDS((s, D), F32), SDS((s, D), BF)],
        compiler_params=_cp(("parallel",)), name=name)(x, za, sb, o, proj, w4, bg, g_next)


def _mix_bwd(dxb, ya, yb, yc, proj, w4, bg, *, name):
    s = dxb.shape[0]
    tm = min(_TM_MIX, s)

    def body(dx_ref, ya_ref, yb_ref, yc_ref, pg_ref, w4_ref, bg_ref,
             dya_ref, dyb_ref, dyc_ref, dza_ref, dsb_ref, do_ref, dgt_ref, dbg_ref):
        @pl.when(pl.program_id(0) == 0)
        def _():
            dbg_ref[...] = jnp.zeros_like(dbg_ref)

        dm = _dot_nt(dx_ref[...], w4_ref[3])
        for j, (y_ref, dy_ref, din_ref) in enumerate(zip((ya_ref, yb_ref, yc_ref), (dya_ref, dyb_ref, dyc_ref),
                                                         (dza_ref, dsb_ref, do_ref))):
            cols = slice(j * D, (j + 1) * D)
            gate = _sigmoid(pg_ref[:, cols].astype(F32) + bg_ref[:, cols])
            dy = (dm * gate).astype(BF)
            dy_ref[...] = dy
            din_ref[...] = _dot_nt(dy, w4_ref[j]).astype(BF)
            dpre = dm * y_ref[...].astype(F32) * gate * (1.0 - gate)
            dgt_ref[:, cols] = dpre.astype(BF)
            dbg_ref[0:1, cols] += jnp.sum(dpre, axis=0, keepdims=True)

    row = lambda i: (i, 0)
    act = pl.BlockSpec((tm, D), row)
    return _pcall(
        body, grid=(s // tm,),
        in_specs=[act, act, act, act, pl.BlockSpec((tm, 3 * D), lambda i: (i, 2)),
                  pl.BlockSpec((4, D, D), lambda i: (0, 0, 0), pipeline_mode=pl.Buffered(1)), pl.BlockSpec((1, 3 * D), lambda i: (0, 0))],
        out_specs=[act] * 6 + [pl.BlockSpec((tm, 3 * D), lambda i: (i, 2)),
                               pl.BlockSpec((8, 3 * D), lambda i: (0, 0))],
        out_shape=[SDS((s, D), BF)] * 6 + [SDS((s, 9 * D), BF), SDS((8, 3 * D), F32)],
        compiler_params=_cp(("arbitrary",)), name=name)(dxb, ya, yb, yc, proj, w4, bg)


_PAIR = 2 * C_UP_P


def _row_selector(sub, first_cols):
    rows = np.arange(len(first_cols) * sub)
    col = np.asarray(first_cols)[rows // sub] + rows % sub
    return jnp.asarray(np.arange(2 * sub)[None, :] == col[:, None], BF)


def _ffn_act(u2, cw, *, name):
    s = u2.shape[0]
    tr = min(TR, s)
    sub = min(SUB, tr)
    sel = _row_selector(sub, [sub - (K_F - 1 - k) for k in range(K_F)])

    def body(cur, prev, sel_ref, w_ref, act_ref, c2_ref, xb, win):
        i = pl.program_id(1)
        before = prev[...]
        xb[0:sub, :] = jnp.where(i == 0, jnp.zeros_like(before), before)
        xb[sub:sub + tr, :] = cur[...]
        for r0 in range(0, tr, sub):
            win[...] = _dot(sel_ref[...], xb[r0:r0 + 2 * sub, :])
            for c in range(C_UP_P // LANE):
                gl = slice(LANE * c, LANE * c + LANE)
                ul = slice(C_UP_P + LANE * c, C_UP_P + LANE * c + LANE)
                gt = sum(w_ref[k:k + 1, gl] * win[k * sub:(k + 1) * sub, gl] for k in range(K_F))
                up = sum(w_ref[k:k + 1, ul] * win[k * sub:(k + 1) * sub, ul] for k in range(K_F))
                c2_ref[r0:r0 + sub, gl] = gt.astype(BF)
                c2_ref[r0:r0 + sub, ul] = up.astype(BF)
                act_ref[r0:r0 + sub, gl] = (gt * _sigmoid(gt) * up).astype(BF)

    return _pcall(
        body, grid=(4, s // tr),
        in_specs=[pl.BlockSpec((tr, _PAIR), lambda p, i: (i, p)),
                  pl.BlockSpec((sub, _PAIR), lambda p, i: (_halo_before(i, tr, sub), p)),
                  pl.BlockSpec(sel.shape, lambda p, i: (0, 0)),
                  pl.BlockSpec((8, _PAIR), lambda p, i: (0, p))],
        out_specs=[pl.BlockSpec((tr, C_UP_P), lambda p, i: (i, p)), pl.BlockSpec((tr, _PAIR), lambda p, i: (i, p))],
        out_shape=[SDS((s, FF_P), BF), SDS((s, 2 * FF_P), BF)],
        scratch_shapes=[pltpu.VMEM((sub + tr, _PAIR), BF), pltpu.VMEM((K_F * sub, _PAIR), F32)],
        compiler_params=_cp(("parallel", "parallel")), name=name)(u2, u2, sel, cw)


def _ffn_bwd(u2, c2, dact, cw, *, name):
    s = u2.shape[0]
    tr, h = min(TR, s), H_S
    sub = min(SUB, tr)
    n = s // tr
    sel = _row_selector(sub, [K_F - 1 - k for k in range(K_F)])

    def body(u_cur, c_cur, c_after, da_cur, da_after, sel_ref, w_ref, du_ref, dw_ref, dcb, win):
        i = pl.program_id(1)
        last = i == n - 1

        @pl.when(i == 0)
        def _():
            dw_ref[...] = jnp.zeros_like(dw_ref)

        def conv_grad(gt, up, da):
            gt, up, da = gt.astype(F32), up.astype(F32), da.astype(F32)
            sg = _sigmoid(gt)
            return (da * up * (sg * (1.0 + gt * (1.0 - sg)))).astype(BF), (da * (gt * sg)).astype(BF)

        for c in range(C_UP_P // LANE):
            gl = slice(LANE * c, LANE * c + LANE)
            ul = slice(C_UP_P + LANE * c, C_UP_P + LANE * c + LANE)
            for r0 in range(0, tr, sub):
                rows = slice(r0, r0 + sub)
                dcb[rows, gl], dcb[rows, ul] = conv_grad(c_cur[rows, gl], c_cur[rows, ul], da_cur[rows, gl])
            dg, du_ = conv_grad(c_after[:, gl], c_after[:, ul], da_after[:, gl])
            dcb[tr:tr + h, gl] = jnp.where(last, jnp.zeros_like(dg), dg)
            dcb[tr:tr + h, ul] = jnp.where(last, jnp.zeros_like(du_), du_)
        dcb[tr + h:tr + sub, :] = jnp.zeros((sub - h, _PAIR), BF)
        for r0 in range(0, tr, sub):
            win[...] = _dot(sel_ref[...], dcb[r0:r0 + 2 * sub, :])
            for c in range(_PAIR // LANE):
                sl = slice(LANE * c, LANE * c + LANE)
                u = u_cur[r0:r0 + sub, sl].astype(F32)
                du = None
                for k in range(K_F):
                    wk = win[k * sub:(k + 1) * sub, sl]
                    term = w_ref[k:k + 1, sl] * wk
                    du = term if du is None else du + term
                    dw_ref[k:k + 1, sl] += jnp.sum(wk * u, axis=0, keepdims=True)
                du_ref[r0:r0 + sub, sl] = du.astype(BF)

    return _pcall(
        body, grid=(4, n),
        in_specs=[pl.BlockSpec((tr, _PAIR), lambda p, i: (i, p)),
                  pl.BlockSpec((tr, _PAIR), lambda p, i: (i, p)),
                  pl.BlockSpec((h, _PAIR), lambda p, i: (_halo_after(i, tr, h, s), p)),
                  pl.BlockSpec((tr, C_UP_P), lambda p, i: (i, p)),
                  pl.BlockSpec((h, C_UP_P), lambda p, i: (_halo_after(i, tr, h, s), p)),
                  pl.BlockSpec(sel.shape, lambda p, i: (0, 0)),
                  pl.BlockSpec((8, _PAIR), lambda p, i: (0, p))],
        out_specs=[pl.BlockSpec((tr, _PAIR), lambda p, i: (i, p)), pl.BlockSpec((8, _PAIR), lambda p, i: (0, p))],
        out_shape=[SDS((s, 2 * FF_P), BF), SDS((8, 2 * FF_P), F32)],
        scratch_shapes=[pltpu.VMEM((tr + sub, _PAIR), BF), pltpu.VMEM((K_F * sub, _PAIR), F32)],
        compiler_params=_cp(("parallel", "arbitrary")), name=name)(u2, c2, c2, dact, dact, sel, cw)


def _relations():
    x, y, c = lax.axis_index("x"), lax.axis_index("y"), lax.axis_index("c")
    out = []
    for r in range(1, N_DEV):
        rx, ry, rc = (r >> 2) & 1, (r >> 1) & 1, r & 1
        out.append((r, (x ^ rx, y ^ ry, c ^ rc)))
    return out


def _my_index():
    return 4 * lax.axis_index("x") + 2 * lax.axis_index("y") + lax.axis_index("c")


def _slab(kind, ref, idx):
    if kind == "win":
        return ref.at[:, pl.ds(pl.multiple_of(idx * C_IN, LANE), C_IN)]
    if kind == "wup":
        return ref.at[:, pl.ds(pl.multiple_of(_up_slot(idx) * C_UP_P, LANE), C_UP_P)]
    if kind == "wkv":
        return ref.at[idx]
    if kind == "w4":
        return ref.at[:, pl.ds(pl.multiple_of(idx * R_O, 16), R_O), :]
    if kind == "wdn":
        return ref.at[pl.ds(pl.multiple_of(_dn_row(idx), 16), R_DN), :]
    assert kind == "cv"
    return ref.at[idx]


_WHOLE = {"win": ((D, 9 * D), BF), "wup": ((D, 2 * FF_P), BF), "wkv": ((N_DEV, D, C_KV), BF),
          "w4": ((4, D, D), BF), "wdn": ((FF_P, D), BF)}
_SHARD = {"win": (D, C_IN), "wup": (D, C_UP_P), "wkv": (D, C_KV), "w4": (4, R_O, D), "wdn": (R_DN, D)}
HBM_SPEC = pl.BlockSpec(memory_space=pltpu.HBM)
SEM_SPEC = pl.BlockSpec(memory_space=pltpu.SEMAPHORE)
_DATAFLOW = pltpu.SideEffectType.DATAFLOW_SIDE_EFFECTING


def _scatter_maps(kinds):
    return ((lambda srcs, lands, a, idx: _slab(kinds[a], srcs[a], idx)),
            (lambda lands, a, idx: lands[a].at[idx]))


_SLOTTED = ("wkv", "cv")


def _own_slab_blocks(kind, shard_shape):
    if kind in ("win", "wup"):
        rows, slot = 256, (_up_slot if kind == "wup" else (lambda m: m))
        return (shard_shape[0] // rows, (rows, shard_shape[1]), (lambda i, me: (i, slot(me[0]))),
                (lambda i, me: (i, 0)), (lambda i, me: (me[0], i, 0)))
    if kind == "w4":
        return (1, shard_shape, (lambda i, me: (0, me[0], 0)), (lambda i, me: (0, 0, 0)),
                (lambda i, me: (me[0], 0, 0, 0)))
    if kind == "wdn":
        rows = 32
        return (R_DN // rows, (rows, D), (lambda i, me: (_dn_row(me[0]) // rows + i, 0)), (lambda i, me: (i, 0)),
                (lambda i, me: (me[0], i, 0)))
    assert kind in _SLOTTED
    rows = min(256, shard_shape[0])
    return (shard_shape[0] // rows, (rows, shard_shape[1]), (lambda i, me: (me[0], i, 0)),
            (lambda i, me: (i, 0)), (lambda i, me: (me[0], i, 0)))


def _place_own(kind, src, out_sds, gather, me_arr, *, name):
    shard_shape = src.shape if gather else out_sds.shape[1:]
    steps, blk, whole_idx, shard_idx, staging_idx = _own_slab_blocks(kind, shard_shape)
    slotted = kind in _SLOTTED
    whole_spec = pl.BlockSpec(((None,) if slotted else ()) + tuple(blk), whole_idx)
    if gather:
        in_spec, out_spec = pl.BlockSpec(tuple(blk), shard_idx), whole_spec
    else:
        in_spec, out_spec = whole_spec, pl.BlockSpec((None,) + tuple(blk), staging_idx)
    zero_init = gather and kind == "wdn"

    def body(me_ref, src_ref, *rest):
        rest[-1][...] = src_ref[...].astype(rest[-1].dtype)

    operands = (me_arr, src) + ((jnp.zeros(out_sds.shape, out_sds.dtype),) if zero_init else ())
    return _pcall(
        body,
        grid_spec=pltpu.PrefetchScalarGridSpec(
            num_scalar_prefetch=1, grid=(steps,), in_specs=[in_spec] + ([ANY] if zero_init else []),
            out_specs=out_spec),
        out_shape=out_sds, input_output_aliases={2: 0} if zero_init else {},
        compiler_params=_cp(("arbitrary",)), name=name)(*operands)


def _peer_copies(n, src_of, dst_of, src_r, land_r, ssem, rsem):
    me = _my_index()
    out = []
    for r, peer in _relations():
        p_idx = 4 * peer[0] + 2 * peer[1] + peer[2]
        for a in range(n):
            def copy(src_idx, dst_idx, a=a, r=r, peer=peer):
                sem = a * (N_DEV - 1) + r - 1
                return pltpu.make_async_remote_copy(
                    src_ref=src_of(src_r, land_r, a, src_idx), dst_ref=dst_of(land_r, a, dst_idx),
                    send_sem=ssem.at[sem], recv_sem=rsem.at[sem], device_id=peer, device_id_type=MESH)
            out.append((functools.partial(copy, p_idx, me), functools.partial(copy, me, p_idx)))
    return out


def _exchange_start(srcs, lands, maps, after, *, name):
    n, ns = len(lands), len(srcs)
    src_of, dst_of = maps

    def body(*refs):
        src_r, land_r = refs[:ns], refs[ns:ns + n]
        ssem, rsem, token = refs[ns + n + 1], refs[ns + n + 2], refs[-1]
        for send, _ in _peer_copies(n, src_of, dst_of, src_r, land_r, ssem, rsem):
            send().start()
        token[...] = jnp.zeros_like(token)

    flight = list(srcs) + list(lands)
    outs = pl.pallas_call(
        body, name=name,
        out_shape=(pltpu.SemaphoreType.DMA((n * (N_DEV - 1),)), pltpu.SemaphoreType.DMA((n * (N_DEV - 1),)),
                   *[pltpu.HBM(a.shape, a.dtype) for a in flight], SDS((8, LANE), F32)),
        in_specs=[HBM_SPEC] * (ns + n) + [ANY],
        out_specs=(SEM_SPEC, SEM_SPEC, *[HBM_SPEC] * (ns + n), pl.BlockSpec(memory_space=pltpu.VMEM)),
        input_output_aliases={i: 2 + i for i in range(ns + n)},
        compiler_params=pltpu.CompilerParams(has_side_effects=_DATAFLOW),
    )(*[pltpu.with_memory_space_constraint(a, pltpu.HBM) for a in flight], after)
    return (outs[0], outs[1], list(outs[2:2 + ns + n]), ns), outs[-1]


def _exchange_wait(handle, maps, after, *, name):
    ssem, rsem, flight, ns = handle
    n = len(flight) - ns
    src_of, dst_of = maps

    def body(*refs):
        src_r, land_r, ssem_r, rsem_r = refs[:ns], refs[ns:ns + n], refs[ns + n], refs[ns + n + 1]
        for send, arrival in _peer_copies(n, src_of, dst_of, src_r, land_r, ssem_r, rsem_r):
            send().wait_send()
            arrival().wait_recv()

    outs = pl.pallas_call(
        body, name=name, out_shape=[pltpu.HBM(a.shape, a.dtype) for a in flight],
        in_specs=[HBM_SPEC] * (ns + n) + [SEM_SPEC, SEM_SPEC, ANY], out_specs=[HBM_SPEC] * (ns + n),
        input_output_aliases={i: i for i in range(ns + n)},
        compiler_params=pltpu.CompilerParams(has_side_effects=_DATAFLOW),
    )(*flight, ssem, rsem, after)
    return list(outs[ns:])


_SIBLING = 1
_ICI = (2, 4, 6)


def _rel_peer(r):
    x, y, c = lax.axis_index("x"), lax.axis_index("y"), lax.axis_index("c")
    peer = (x ^ ((r >> 2) & 1), y ^ ((r >> 1) & 1), c ^ (r & 1))
    return peer, 4 * peer[0] + 2 * peer[1] + peer[2]


def _rcopy(ref, ssem, rsem, peer):
    return pltpu.make_async_remote_copy(src_ref=ref, dst_ref=ref, send_sem=ssem, recv_sem=rsem, device_id=peer,
                                        device_id_type=MESH)


def _gather2_start(lands, kinds, after, *, name):
    n = len(lands)

    def body(*refs):
        land_r, (send1, recv_sib, recv_ici), token = refs[:n], refs[n + 1:n + 4], refs[-1]
        me = _my_index()
        for a in range(n):
            own = _slab(kinds[a], land_r[a], me)
            for j, r in enumerate((_SIBLING,) + _ICI):
                rsem = recv_sib.at[a] if r == _SIBLING else recv_ici.at[3 * a + j - 1]
                _rcopy(own, send1.at[4 * a + j], rsem, _rel_peer(r)[0]).start()
        token[...] = jnp.zeros_like(token)

    sems = [pltpu.SemaphoreType.DMA((4 * n,)), pltpu.SemaphoreType.DMA((n,)), pltpu.SemaphoreType.DMA((3 * n,))]
    outs = pl.pallas_call(
        body, name=name, out_shape=(*sems, *[pltpu.HBM(a.shape, a.dtype) for a in lands], SDS((8, LANE), F32)),
        in_specs=[HBM_SPEC] * n + [ANY],
        out_specs=(SEM_SPEC,) * 3 + (HBM_SPEC,) * n + (pl.BlockSpec(memory_space=pltpu.VMEM),),
        input_output_aliases={i: 3 + i for i in range(n)},
        compiler_params=pltpu.CompilerParams(has_side_effects=_DATAFLOW),
    )(*[pltpu.with_memory_space_constraint(a, pltpu.HBM) for a in lands], after)
    return dict(send1=outs[0], recv_sib=outs[1], recv_ici=outs[2], lands=list(outs[3:3 + n])), outs[-1]


def _gather2_forward(handle, kinds, after, *, name):
    lands = handle["lands"]
    n = len(lands)

    def body(*refs):
        land_r, recv_ici, (fwd_send, fwd_recv), token = refs[:n], refs[n], refs[n + 2:n + 4], refs[-1]
        sibling = _rel_peer(_SIBLING)[0]
        for a in range(n):
            for j, r in enumerate(_ICI):
                got = _slab(kinds[a], land_r[a], _rel_peer(r)[1])
                _rcopy(got, fwd_send.at[3 * a + j], recv_ici.at[3 * a + j], sibling).wait_recv()
                _rcopy(got, fwd_send.at[3 * a + j], fwd_recv.at[3 * a + j], sibling).start()
        token[...] = jnp.zeros_like(token)

    sems = [pltpu.SemaphoreType.DMA((3 * n,)), pltpu.SemaphoreType.DMA((3 * n,))]
    outs = pl.pallas_call(
        body, name=name, out_shape=(*sems, *[pltpu.HBM(a.shape, a.dtype) for a in lands], SDS((8, LANE), F32)),
        in_specs=[HBM_SPEC] * n + [SEM_SPEC, ANY],
        out_specs=(SEM_SPEC,) * 2 + (HBM_SPEC,) * n + (pl.BlockSpec(memory_space=pltpu.VMEM),),
        input_output_aliases={i: 2 + i for i in range(n)},
        compiler_params=pltpu.CompilerParams(has_side_effects=_DATAFLOW),
    )(*lands, handle["recv_ici"], after)
    return dict(handle, fwd_send=outs[0], fwd_recv=outs[1], lands=list(outs[2:2 + n])), outs[-1]


def _gather2_wait(handle, kinds, after, *, name):
    lands = handle["lands"]
    n = len(lands)

    def body(*refs):
        land_r, (send1, recv_sib, fwd_send, fwd_recv) = refs[:n], refs[n:n + 4]
        me = _my_index()
        sibling, sib_idx = _rel_peer(_SIBLING)
        for a in range(n):
            own = _slab(kinds[a], land_r[a], me)
            for j, r in enumerate((_SIBLING,) + _ICI):
                _rcopy(own, send1.at[4 * a + j], recv_sib.at[a], _rel_peer(r)[0]).wait_send()
            theirs = _slab(kinds[a], land_r[a], sib_idx)
            _rcopy(theirs, send1.at[4 * a], recv_sib.at[a], sibling).wait_recv()
            for j, r in enumerate(_ICI):
                passed_on = _slab(kinds[a], land_r[a], _rel_peer(r)[1])
                _rcopy(passed_on, fwd_send.at[3 * a + j], fwd_recv.at[3 * a + j], sibling).wait_send()
                arrived = _slab(kinds[a], land_r[a], _rel_peer(r ^ _SIBLING)[1])
                _rcopy(arrived, fwd_send.at[3 * a + j], fwd_recv.at[3 * a + j], sibling).wait_recv()

    outs = pl.pallas_call(
        body, name=name, out_shape=[pltpu.HBM(a.shape, a.dtype) for a in lands],
        in_specs=[HBM_SPEC] * n + [SEM_SPEC] * 4 + [ANY], out_specs=[HBM_SPEC] * n,
        input_output_aliases={i: i for i in range(n)},
        compiler_params=pltpu.CompilerParams(has_side_effects=_DATAFLOW),
    )(*lands, handle["send1"], handle["recv_sib"], handle["fwd_send"], handle["fwd_recv"], after)
    return list(outs)


def _sum_slots(gathered, *, name):
    def body(g_ref, out_ref):
        total = g_ref[0]
        for dev in range(1, N_DEV):
            total = total + g_ref[dev]
        out_ref[...] = total

    return _pcall(body, out_shape=SDS(gathered.shape[1:], F32), compiler_params=_cp(), name=name)(gathered)


def _adam(g, w, m, v):
    nm = ADAM_B1 * m + (1.0 - ADAM_B1) * g
    nv = ADAM_B2 * v + (1.0 - ADAM_B2) * (g * g)
    m_hat = nm / (1.0 - ADAM_B1 ** ADAM_STEP)
    v_hat = nv / (1.0 - ADAM_B2 ** ADAM_STEP)
    return -ADAM_LR * (m_hat / (jnp.sqrt(v_hat) + ADAM_EPS) + ADAM_WD * w), nm, nv


def _adamw_staged(st0, st1, w, m, v, *, name):
    _, rows, cols = w.shape
    st_cols = st0.shape[2]
    tr = max(t for t in range(16, 129, 16) if rows % t == 0)
    nr = rows // tr

    def body(s0_ref, s1_ref, w_ref, m_ref, v_ref, g_ref, d_ref, nm_ref, nv_ref):
        for layer, s_ref in enumerate((s0_ref, s1_ref)):
            @pl.when(pl.program_id(0) == layer)
            def _(s_ref=s_ref):
                total = s_ref[0, :, 0:cols].astype(F32)
                for dev in range(1, N_DEV):
                    total = total + s_ref[dev, :, 0:cols].astype(F32)
                g_ref[0] = total

        d_ref[0], nm_ref[0], nv_ref[0] = _adam(g_ref[0], w_ref[0], m_ref[0], v_ref[0])

    st_spec = lambda layer: pl.BlockSpec(
        (N_DEV, tr, st_cols), lambda l, i: (0, jnp.where(l == layer, i, (nr - 1) * (1 - layer)), 0))
    par = pl.BlockSpec((1, tr, cols), lambda l, i: (l, i, 0))
    return _pcall(
        body, grid=(DEPTH, nr), in_specs=[st_spec(0), st_spec(1), par, par, par], out_specs=[par] * 4,
        out_shape=[SDS(w.shape, F32)] * 4,
        compiler_params=_cp(("arbitrary", "arbitrary")), name=name)(st0, st1, w, m, v)


def _adamw_small(g, w, m, v, *, name):
    def body(g_ref, w_ref, m_ref, v_ref, d_ref, nm_ref, nv_ref):
        d_ref[...], nm_ref[...], nv_ref[...] = _adam(g_ref[...], w_ref[...], m_ref[...], v_ref[...])

    return _pcall(body, out_shape=[SDS(g.shape, F32)] * 3, compiler_params=_cp(), name=name)(g, w, m, v)


def _pack_rows(arrays):
    flat = jnp.concatenate([a.reshape(-1).astype(F32) for a in arrays])
    rows = -(-flat.shape[0] // (8 * D)) * 8
    return jnp.pad(flat, (0, rows * D - flat.shape[0])).reshape(rows, D)


def _unpack_rows(pack, like):
    flat = pack.reshape(-1)
    out, at = [], 0
    for a in like:
        out.append(flat[at:at + a.size].reshape(a.shape))
        at += a.size
    return out


def _layer_fwd(x, h, mem, win, rest_of_weights, after_up, small, g_next, tag):
    proj = _mm(h, win, tm=1024, tn=1536, name=f"proj_{tag}")
    wup, wkv, w4, wdn, cw_a, cw_b, cw_f = rest_of_weights(proj)
    za = _bra_fwd(proj, cw_a, name=f"bra_fwd_{tag}")
    cb = _brb_conv_fwd(proj, cw_b, small["conv_b_bias"], name=f"brb_conv_fwd_{tag}")
    sb = _ln_silu_fwd(cb, small["ln_b_g"], small["ln_b_b"], name=f"ln_silu_fwd_{tag}")
    memn, kv = _kv_prep(mem, small["norm_mem_g"], wkv, name=f"kv_prep_{tag}")
    o = _attn_fwd(proj, kv, name=f"attn_fwd_{tag}")
    ya, yb, yc, mg, x1, h2 = _mix_out(x, za, sb, o, proj, w4, small["b_gate"], small["norm_ffn_g"],
                                      name=f"mix_out_{tag}")
    u2 = _mm(h2, wup, tm=1024, tn=1536, name=f"up_{tag}")
    token = after_up(u2)
    act, c2 = _ffn_act(u2, cw_f if token is None else _behind(cw_f, token), name=f"ffn_act_{tag}")
    x2, h_next = _mm_res_norm(act, wdn, x1, g_next, name=f"down_{tag}")
    saved = dict(x=x, h=h, proj=proj, za=za, cb=cb, sb=sb, memn=memn, kv=kv, o=o, ya=ya, yb=yb, yc=yc,
                 mg=mg, x1=x1, h2=h2, u2=u2, c2=c2, act=act)
    return x2, h_next, (win, wup, wkv, w4, wdn, cw_a, cw_b, cw_f), saved


def _behind(operand, token):
    return operand + token[0:1, 0:1]


def _layer_bwd(dx2, dx2b, mem, wts, small, sv, start, tag):
    win, wup, wkv, w4, wdn, cw_a, cw_b, cw_f = wts
    dact = _mm(dx2b, wdn, tb=True, tm=1024, tn=768, name=f"d_act_{tag}")
    dwdn = _mm(sv["act"], dx2b, ta=True, tm=768, tn=1024, name=f"dw_down_{tag}")
    du2, dcw_f = _ffn_bwd(sv["u2"], sv["c2"], dact, cw_f, name=f"ffn_bwd_{tag}")
    dwup = _mm(sv["h2"], du2, ta=True, tm=1024, tn=768, name=f"dw_up_{tag}")
    token = start(("wdn", "wup"), (dwdn, dwup), f"ffn_{tag}")
    dx1, dx1b, dg_ffn = _mm_nt_normbwd(du2, wup, sv["x1"], dx2, _behind(small["norm_ffn_g"], token),
                                       name=f"d_h2_{tag}")

    dya, dyb, dyc, dza, dsb, do, dproj, dbg = _mix_bwd(dx1b, sv["ya"], sv["yb"], sv["yc"], sv["proj"], w4,
                                                      small["b_gate"], name=f"mix_bwd_{tag}")
    dw4 = jnp.stack([
        _mm(a, b, ta=True, tm=1024, tn=512, name=f"dw_{nm}_{tag}")
        for nm, a, b in (("a_out", sv["za"], dya), ("b_out", sv["sb"], dyb), ("att_out", sv["o"], dyc),
                         ("o", sv["mg"], dx1b))])
    dq, dk, dv = _attn_bwd(sv["proj"], sv["kv"], do, name=f"attn_bwd_{tag}")
    dwkv, dg_mem = _kv_bwd(mem, small["norm_mem_g"], sv["memn"], dk, dv, wkv, name=f"kv_bwd_{tag}")
    token = start(("w4", "wkv"), (dw4, dwkv), f"mix_{tag}")
    dproj, dcw_a = _bra_bwd(sv["proj"], dza, _behind(cw_a, token), dproj, name=f"bra_bwd_{tag}")
    dcb, ln_sums = _ln_silu_bwd(sv["cb"], dsb, small["ln_b_g"], small["ln_b_b"], name=f"ln_silu_bwd_{tag}")
    dproj, dcw_b = _brb_conv_bwd(sv["proj"], dcb, dq, cw_b, dproj, name=f"brb_conv_bwd_{tag}")
    dwin = _mm(sv["h"], dproj, ta=True, tm=1024, tn=768, name=f"dw_in_{tag}")
    token = start(("win",), (dwin,), f"in_{tag}")
    dx, dxb, dg_mix = _mm_nt_normbwd(dproj, win, sv["x"], dx1, _behind(small["norm_mix_g"], token),
                                     tk=4608, name=f"d_h_{tag}")

    small_grads = [dg_mix[0:1], dg_mem[0:1], dbg[0:1].reshape(3, D), ln_sums[2:3], ln_sums[0:1], ln_sums[1:2],
                   dg_ffn[0:1], dcw_a[0:K_A], dcw_b[0:K_B], dcw_f[0:K_F].reshape(K_F * 2 * FF_P // D, D)]
    return dx, dxb, small_grads, token


_SMALL_ROWS = (1, 1, 3, 1, 1, 1, 1, K_A, K_B, K_F * 2 * FF_P // D)
_CV_ROWS = 48


def kernel(x, mem, norm_mix_g, norm_mem_g, w_in, b_gate, conv_a_w, w_a_out, conv_b_w, conv_b_bias, ln_b_g, ln_b_b, w_b_out, w_kv, w_att_out, w_o, norm_ffn_g, w_up, conv_ffn_w, w_down, norm_final_g, loss_target, m_norm_mix_g, m_norm_mem_g, m_w_in, m_b_gate, m_conv_a_w, m_w_a_out, m_conv_b_w, m_conv_b_bias, m_ln_b_g, m_ln_b_b, m_w_b_out, m_w_kv, m_w_att_out, m_w_o, m_norm_ffn_g, m_w_up, m_conv_ffn_w, m_w_down, m_norm_final_g, v_norm_mix_g, v_norm_mem_g, v_w_in, v_b_gate, v_conv_a_w, v_w_a_out, v_conv_b_w, v_conv_b_bias, v_ln_b_g, v_ln_b_b, v_w_b_out, v_w_kv, v_w_att_out, v_w_o, v_norm_ffn_g, v_w_up, v_conv_ffn_w, v_w_down, v_norm_final_g):
    me = _my_index()
    me_arr = me.astype(jnp.int32).reshape(1)
    x0, mem0, tgt = x.reshape(x.shape[1:]), mem.reshape(mem.shape[1:]), loss_target.reshape(x.shape[1:])
    up_pad = ((0, 0), (0, 0), (0, C_UP_P - C_UP))

    ag_groups = (("win",), ("wup", "wkv", "w4", "wdn", "cv"))
    kinds = ag_groups[0] + ag_groups[1]
    smalls, ag_handles = [], []
    token = jnp.zeros((8, LANE), F32)
    for l in range(DEPTH):
        cv = jnp.zeros((_CV_ROWS, C_UP_P), F32)
        cv = cv.at[0:K_F, 0:C_UP].set(conv_ffn_w[l]).at[3:3 + K_A, 0:R_O].set(conv_a_w[l])
        cv = cv.at[8:8 + K_B, 0:R_O].set(conv_b_w[l])
        shards = dict(
            win=w_in[l], wup=jnp.pad(w_up[l], up_pad[1:]), wkv=w_kv[l],
            w4=jnp.stack([w_a_out[l], w_b_out[l], w_att_out[l], w_o[l]]), wdn=w_down[l], cv=cv)
        whole = dict({k: SDS(*_WHOLE[k]) for k in kinds[:-1]}, cv=SDS((N_DEV,) + cv.shape, F32))
        lands = {k: _place_own(k, shards[k], whole[k], True, me_arr, name=f"ag_own_{k}_l{l}") for k in kinds}
        per_layer = []
        for g, grp in enumerate(ag_groups):
            handle, token = _gather2_start([lands[k] for k in grp], grp, token, name=f"ag_start_l{l}_g{g}")
            per_layer.append(handle)
        ag_handles.append(per_layer)
        smalls.append(dict(
            norm_mix_g=norm_mix_g[l][None], norm_mem_g=norm_mem_g[l][None], b_gate=b_gate[l][None],
            conv_b_bias=conv_b_bias[l][None], ln_b_g=ln_b_g[l][None], ln_b_b=ln_b_b[l][None],
            norm_ffn_g=norm_ffn_g[l][None]))

    def forward_group(l, g, after):
        ag_handles[l][g], tok = _gather2_forward(ag_handles[l][g], ag_groups[g], after, name=f"ag_forward_l{l}_g{g}")
        return tok

    def rest_of_weights(l):
        def wait(after):
            if l == 0:
                after = forward_group(0, 1, after)
            wup, wkv, w4, wdn, cvg = _gather2_wait(ag_handles[l][1], ag_groups[1], after, name=f"ag_wait_l{l}_g1")
            cw_f = jnp.stack([cvg[d, 0:K_F, :] for d in UP_ORDER], axis=1).reshape(K_F, 2 * FF_P)
            cw_a = cvg[:, 3:3 + K_A, 0:R_O].transpose(1, 0, 2).reshape(K_A, D)
            cw_b = cvg[:, 8:8 + K_B, 0:R_O].transpose(1, 0, 2).reshape(K_B, D)
            return (wup, wkv, w4, wdn, jnp.pad(cw_a, ((0, 8 - K_A), (0, 0))),
                    jnp.pad(cw_b, ((0, 32 - K_B), (0, 0))), jnp.pad(cw_f, ((0, 8 - K_F), (0, 0))))
        return wait

    wts, saved = [], []
    xs = x0
    h = _rms_fwd(xs, smalls[0]["norm_mix_g"], name="rms_fwd")
    behind = forward_group(0, 0, token)

    def next_layer_forwarding(l):
        def hook(after):
            if l + 1 == DEPTH:
                return None
            return forward_group(l + 1, 1, forward_group(l + 1, 0, after))
        return hook

    for l in range(DEPTH):
        g_next = smalls[l + 1]["norm_mix_g"] if l + 1 < DEPTH else norm_final_g[None]
        (win,) = _gather2_wait(ag_handles[l][0], ag_groups[0], behind, name=f"ag_wait_l{l}_g0")
        xs, h, w_l, sv = _layer_fwd(xs, h, mem0, win, rest_of_weights(l), next_layer_forwarding(l), smalls[l],
                                    g_next, f"l{l}")
        behind = h
        wts.append(w_l)
        saved.append(sv)
    dx, dxb, head_sums = _loss_head(xs, tgt, norm_final_g[None], name="loss_head")

    rs_handles = []
    small_grads = [None] * DEPTH

    def start_scatter(grp, arrays, name):
        maps = _scatter_maps(grp)
        lands = [_place_own(k, a, SDS((N_DEV,) + _SHARD[k], BF), False, me_arr, name=f"rs_own_{k}_{name}")
                 for k, a in zip(grp, arrays)]
        handle, tok = _exchange_start(list(arrays), lands, maps, rs_handles[-1][2] if rs_handles else head_sums,
                                      name=f"rs_start_{name}")
        rs_handles.append((grp, handle, tok, name))
        return tok

    for l in reversed(range(DEPTH)):
        dx, dxb, small_grads[l], token = _layer_bwd(dx, dxb, mem0, wts[l], smalls[l], saved[l], start_scatter,
                                                    f"l{l}")

    pack = jnp.concatenate(small_grads[0] + small_grads[1] + [head_sums[1:2], head_sums[0:1]], axis=0)
    pack = jnp.pad(pack, ((0, -pack.shape[0] % 8), (0, 0)))
    small_maps = (lambda srcs, lands, a, idx: srcs[a]), (lambda lands, a, idx: lands[a].at[idx])
    small_land = _place_own("cv", pack, SDS((N_DEV,) + pack.shape, F32), True, me_arr, name="small_own")
    small_handle, small_token = _exchange_start([pack], [small_land], small_maps, dx, name="small_start")

    staged = [dict() for _ in range(DEPTH)]
    for grp, handle, _, name in rs_handles[:-1]:
        staged[int(name[-1])].update(zip(grp, _exchange_wait(handle, _scatter_maps(grp), small_token,
                                                             name=f"rs_wait_{name}")))

    def big_update(kind, w, m, v, name):
        return _adamw_staged(staged[0][kind], staged[1][kind], w, m, v, name=name)

    r_up = big_update("wup", w_up, m_w_up, v_w_up, "adamw_w_up")
    r_kv = big_update("wkv", w_kv, m_w_kv, v_w_kv, "adamw_w_kv")
    r_dn = big_update("wdn", w_down, m_w_down, v_w_down, "adamw_w_down")

    def four(a, b, c, d_):
        return jnp.stack([a, b, c, d_], axis=1).reshape(DEPTH, 4 * R_O, D)

    r_4 = _adamw_staged(
        staged[0]["w4"].reshape(N_DEV, 4 * R_O, D), staged[1]["w4"].reshape(N_DEV, 4 * R_O, D),
        four(w_a_out, w_b_out, w_att_out, w_o), four(m_w_a_out, m_w_b_out, m_w_att_out, m_w_o),
        four(v_w_a_out, v_w_b_out, v_w_att_out, v_w_o), name="adamw_w_out")
    grp, handle, _, name = rs_handles[-1]
    staged[0].update(zip(grp, _exchange_wait(handle, _scatter_maps(grp), r_4[0], name=f"rs_wait_{name}")))
    r_in = big_update("win", w_in, m_w_in, v_w_in, "adamw_w_in")
    r_a, r_b, r_att, r_o = ([a.reshape(DEPTH, 4, R_O, D)[:, j] for a in r_4] for j in range(4))

    (gathered,) = _exchange_wait(small_handle, small_maps, r_in[0], name="small_wait")
    total = _sum_slots(gathered, name="small_sum")
    per_layer = sum(_SMALL_ROWS)
    parts = []
    for l in range(DEPTH):
        at, one = l * per_layer, []
        for rows in _SMALL_ROWS:
            one.append(total[at:at + rows])
            at += rows
        parts.append(one)
    g_final = total[DEPTH * per_layer]
    loss = 0.5 / D * jnp.sum(total[DEPTH * per_layer + 1])

    def both(i):
        return jnp.stack([parts[0][i], parts[1][i]])

    g_norm_mix, g_norm_mem = both(0)[:, 0], both(1)[:, 0]
    g_b_gate = both(2).reshape(DEPTH, 3 * D)
    g_cbias, g_lng, g_lnb, g_norm_ffn = both(3)[:, 0], both(4)[:, 0], both(5)[:, 0], both(6)[:, 0]
    g_conv_a = lax.dynamic_slice_in_dim(both(7), me * R_O, R_O, axis=2)
    g_conv_b = lax.dynamic_slice_in_dim(both(8), me * R_O, R_O, axis=2)
    g_conv_f = lax.dynamic_slice_in_dim(both(9).reshape(DEPTH, K_F, 2 * FF_P), _up_slot(me) * C_UP_P, C_UP, axis=2)

    small_g = [g_norm_mix, g_norm_mem, g_b_gate, g_conv_a, g_conv_b, g_cbias, g_lng, g_lnb, g_norm_ffn, g_conv_f,
               g_final]
    small_w = [norm_mix_g, norm_mem_g, b_gate, conv_a_w, conv_b_w, conv_b_bias, ln_b_g, ln_b_b, norm_ffn_g,
               conv_ffn_w, norm_final_g]
    small_m = [m_norm_mix_g, m_norm_mem_g, m_b_gate, m_conv_a_w, m_conv_b_w, m_conv_b_bias, m_ln_b_g, m_ln_b_b,
               m_norm_ffn_g, m_conv_ffn_w, m_norm_final_g]
    small_v = [v_norm_mix_g, v_norm_mem_g, v_b_gate, v_conv_a_w, v_conv_b_w, v_conv_b_bias, v_ln_b_g, v_ln_b_b,
               v_norm_ffn_g, v_conv_ffn_w, v_norm_final_g]
    upd = _adamw_small(_pack_rows(small_g), _pack_rows(small_w), _pack_rows(small_m), _pack_rows(small_v),
                       name="adamw_small")
    s_d, s_m, s_v = (_unpack_rows(p, small_w) for p in upd)
    (d_norm_mix, d_norm_mem, d_b_gate, d_conv_a, d_conv_b, d_cbias, d_lng, d_lnb, d_norm_ffn, d_conv_f,
     d_final) = s_d
    (nm_norm_mix, nm_norm_mem, nm_b_gate, nm_conv_a, nm_conv_b, nm_cbias, nm_lng, nm_lnb, nm_norm_ffn, nm_conv_f,
     nm_final) = s_m
    (nv_norm_mix, nv_norm_mem, nv_b_gate, nv_conv_a, nv_conv_b, nv_cbias, nv_lng, nv_lnb, nv_norm_ffn, nv_conv_f,
     nv_final) = s_v

    grads = [g_norm_mix, g_norm_mem, r_in[0], g_b_gate, g_conv_a, r_a[0], g_conv_b, g_cbias, g_lng, g_lnb, r_b[0],
             r_kv[0], r_att[0], r_o[0], g_norm_ffn, r_up[0], g_conv_f, r_dn[0], g_final]
    deltas = [d_norm_mix, d_norm_mem, r_in[1], d_b_gate, d_conv_a, r_a[1], d_conv_b, d_cbias, d_lng, d_lnb, r_b[1],
              r_kv[1], r_att[1], r_o[1], d_norm_ffn, r_up[1], d_conv_f, r_dn[1], d_final]
    new_m = [nm_norm_mix, nm_norm_mem, r_in[2], nm_b_gate, nm_conv_a, r_a[2], nm_conv_b, nm_cbias, nm_lng, nm_lnb,
             r_b[2], r_kv[2], r_att[2], r_o[2], nm_norm_ffn, r_up[2], nm_conv_f, r_dn[2], nm_final]
    new_v = [nv_norm_mix, nv_norm_mem, r_in[3], nv_b_gate, nv_conv_a, r_a[3], nv_conv_b, nv_cbias, nv_lng, nv_lnb,
             r_b[3], r_kv[3], r_att[3], r_o[3], nv_norm_ffn, r_up[3], nv_conv_f, r_dn[3], nv_final]
    return (loss, dx[None], *grads, *deltas, *new_m, *new_v)
```

```python
import functools

import jax
import jax.numpy as jnp
import numpy as np
from jax import lax
from jax.experimental import pallas as pl
from jax.experimental.pallas import tpu as pltpu

F32 = jnp.float32
BF = jnp.bfloat16
SDS = jax.ShapeDtypeStruct
MESH = pl.DeviceIdType.MESH
ANY = pl.BlockSpec(memory_space=pl.ANY)

N_DEV = 8
DEPTH = 2
D = 1024
N_HEADS = 4
HEAD = D // N_HEADS
D_FF = 2816
K_A, K_B, K_F = 3, 31, 3
NORM_EPS = 1e-6

C_IN = 9 * D // N_DEV
C_KV = 2 * D // N_DEV
C_UP = 2 * D_FF // N_DEV
LANE = 128
C_UP_P = -(-C_UP // LANE) * LANE
FF_P = 4 * C_UP_P
R_O = D // N_DEV
R_DN = D_FF // N_DEV

VMEM_LIMIT = 56 * 1024 * 1024
TM = 512
TR = 256
SUB = 128
H_S, H_L = 16, 32

ADAM_LR, ADAM_B1, ADAM_B2, ADAM_EPS, ADAM_WD, ADAM_STEP = 0.001, 0.9, 0.999, 1e-08, 0.01, 10

UP_ORDER = (0, 4, 1, 5, 2, 6, 3, 7)


def _pcall(body, **kw):
    return pl.pallas_call(body, **kw)


def _cp(sem=None, **kw):
    return pltpu.CompilerParams(dimension_semantics=sem, vmem_limit_bytes=VMEM_LIMIT, **kw)


def _dot(a, b):
    return jnp.dot(a, b, preferred_element_type=F32)


def _dot_nt(a, b):
    return lax.dot_general(a, b, (((1,), (1,)), ((), ())), preferred_element_type=F32)


def _dot_tn(a, b):
    return lax.dot_general(a, b, (((0,), (0,)), ((), ())), preferred_element_type=F32)


def _sigmoid(z):
    return 1.0 / (1.0 + jnp.exp(-z))


def _rms(xv):
    return lax.rsqrt(jnp.mean(xv * xv, axis=-1, keepdims=True) + NORM_EPS)


def _up_slot(idx):
    return jnp.where(idx < 4, 2 * idx, 2 * (idx - 4) + 1)


def _dn_row(idx):
    return C_UP_P * (idx // 2) + R_DN * (idx % 2)


def _mm(a, b, *, ta=False, tb=False, out_dtype=BF, tm=TM, tn=512, tk=None, name):
    m, k_dim = (a.shape[1], a.shape[0]) if ta else a.shape
    n = b.shape[0] if tb else b.shape[1]
    tm, tn = min(tm, m), min(tn, n)
    tk = k_dim if tk is None else min(tk, k_dim)
    nk = k_dim // tk
    assert m % tm == 0 and n % tn == 0 and k_dim % tk == 0
    dims = (((0 if ta else 1,), (1 if tb else 0,)), ((), ()))

    def body(a_ref, b_ref, o_ref, *scratch):
        part = lax.dot_general(a_ref[...], b_ref[...], dims, preferred_element_type=F32)
        if nk == 1:
            o_ref[...] = part.astype(o_ref.dtype)
            return
        acc = scratch[0]
        k = pl.program_id(2)

        @pl.when(k == 0)
        def _():
            acc[...] = part

        @pl.when(k > 0)
        def _():
            acc[...] += part

        @pl.when(k == nk - 1)
        def _():
            o_ref[...] = acc[...].astype(o_ref.dtype)

    a_spec = pl.BlockSpec((tk, tm), lambda i, j, k: (k, i)) if ta else pl.BlockSpec((tm, tk), lambda i, j, k: (i, k))
    b_spec = pl.BlockSpec((tn, tk), lambda i, j, k: (j, k)) if tb else pl.BlockSpec((tk, tn), lambda i, j, k: (k, j))
    return _pcall(
        body, grid=(m // tm, n // tn, nk), in_specs=[a_spec, b_spec],
        out_specs=pl.BlockSpec((tm, tn), lambda i, j, k: (i, j)),
        out_shape=SDS((m, n), out_dtype),
        scratch_shapes=[pltpu.VMEM((tm, tn), F32)] if nk > 1 else [],
        compiler_params=_cp(("parallel", "parallel", "arbitrary")), name=name)(a, b)


def _mm_res_norm(a, w, x, g, *, name):
    s, k_dim = a.shape
    tm = min(TM, s)

    def body(a_ref, w_ref, x_ref, g_ref, xo_ref, h_ref):
        xo = x_ref[...] + _dot(a_ref[...], w_ref[...])
        xo_ref[...] = xo
        h_ref[...] = ((xo * _rms(xo)) * g_ref[...]).astype(BF)

    return _pcall(
        body, grid=(s // tm,),
        in_specs=[pl.BlockSpec((tm, k_dim), lambda i: (i, 0)),
                  pl.BlockSpec((k_dim, D), lambda i: (0, 0), pipeline_mode=pl.Buffered(1)),
                  pl.BlockSpec((tm, D), lambda i: (i, 0)), pl.BlockSpec((1, D), lambda i: (0, 0))],
        out_specs=[pl.BlockSpec((tm, D), lambda i: (i, 0))] * 2,
        out_shape=[SDS((s, D), F32), SDS((s, D), BF)],
        compiler_params=_cp(("parallel",)), name=name)(a, w, x, g)


def _mm_nt_normbwd(da, w, x, dres, g, *, tk=None, name):
    s, k_dim = da.shape
    tm = min(TM, s)
    tk = k_dim if tk is None else tk
    nk = k_dim // tk
    assert k_dim % tk == 0

    def body(da_ref, w_ref, x_ref, dres_ref, g_ref, dx_ref, dxb_ref, dg_ref, *scratch):
        i, k = pl.program_id(0), pl.program_id(1)
        part = _dot_nt(da_ref[...], w_ref[...])
        if nk > 1:
            acc = scratch[0]

            @pl.when(k == 0)
            def _():
                acc[...] = part

            @pl.when(k > 0)
            def _():
                acc[...] += part

        @pl.when((i == 0) & (k == 0))
        def _():
            dg_ref[...] = jnp.zeros_like(dg_ref)

        @pl.when(k == nk - 1)
        def _():
            dh = acc[...] if nk > 1 else part
            xv = x_ref[...]
            r = _rms(xv)
            xn = xv * r
            dg_ref[0:1, :] += jnp.sum(dh * xn, axis=0, keepdims=True)
            dxn = dh * g_ref[...]
            dx = dres_ref[...] + r * (dxn - xn * jnp.mean(dxn * xn, axis=-1, keepdims=True))
            dx_ref[...] = dx
            dxb_ref[...] = dx.astype(BF)

    row = lambda i, k: (i, 0)
    w_spec = (pl.BlockSpec((D, tk), lambda i, k: (0, k)) if nk > 1 else
              pl.BlockSpec((D, tk), lambda i, k: (0, 0), pipeline_mode=pl.Buffered(1)))
    return _pcall(
        body, grid=(s // tm, nk),
        in_specs=[pl.BlockSpec((tm, tk), lambda i, k: (i, k)), w_spec,
                  pl.BlockSpec((tm, D), row), pl.BlockSpec((tm, D), row), pl.BlockSpec((1, D), lambda i, k: (0, 0))],
        out_specs=[pl.BlockSpec((tm, D), row), pl.BlockSpec((tm, D), row), pl.BlockSpec((8, D), lambda i, k: (0, 0))],
        out_shape=[SDS((s, D), F32), SDS((s, D), BF), SDS((8, D), F32)],
        scratch_shapes=[pltpu.VMEM((tm, D), F32)] if nk > 1 else [],
        compiler_params=_cp(("arbitrary", "arbitrary")), name=name)(da, w, x, dres, g)


def _rms_fwd(x, g, *, name):
    s = x.shape[0]
    tm = min(TM, s)

    def body(x_ref, g_ref, h_ref):
        xv = x_ref[...]
        h_ref[...] = ((xv * _rms(xv)) * g_ref[...]).astype(BF)

    return _pcall(
        body, grid=(s // tm,),
        in_specs=[pl.BlockSpec((tm, D), lambda i: (i, 0)), pl.BlockSpec((1, D), lambda i: (0, 0))],
        out_specs=pl.BlockSpec((tm, D), lambda i: (i, 0)), out_shape=SDS((s, D), BF),
        compiler_params=_cp(("parallel",)), name=name)(x, g)


def _loss_head(x, tgt, g, *, name):
    s = x.shape[0]
    tm = min(TM, s)

    def body(x_ref, t_ref, g_ref, dx_ref, dxb_ref, sums_ref):
        @pl.when(pl.program_id(0) == 0)
        def _():
            sums_ref[...] = jnp.zeros_like(sums_ref)

        xv = x_ref[...]
        r = _rms(xv)
        xn = xv * r
        diff = xn * g_ref[...] - t_ref[...]
        sums_ref[0:1, :] += jnp.sum(diff * diff, axis=0, keepdims=True)
        dy = diff * (1.0 / D)
        sums_ref[1:2, :] += jnp.sum(dy * xn, axis=0, keepdims=True)
        dxn = dy * g_ref[...]
        dx = r * (dxn - xn * jnp.mean(dxn * xn, axis=-1, keepdims=True))
        dx_ref[...] = dx
        dxb_ref[...] = dx.astype(BF)

    row = lambda i: (i, 0)
    return _pcall(
        body, grid=(s // tm,),
        in_specs=[pl.BlockSpec((tm, D), row), pl.BlockSpec((tm, D), row), pl.BlockSpec((1, D), lambda i: (0, 0))],
        out_specs=[pl.BlockSpec((tm, D), row), pl.BlockSpec((tm, D), row), pl.BlockSpec((8, D), lambda i: (0, 0))],
        out_shape=[SDS((s, D), F32), SDS((s, D), BF), SDS((8, D), F32)],
        compiler_params=_cp(("arbitrary",)), name=name)(x, tgt, g)


def _halo_before(i, tr, h):
    return jnp.maximum(i * (tr // h) - 1, 0)


def _halo_after(i, tr, h, s):
    return jnp.minimum((i + 1) * (tr // h), s // h - 1)


def _taps(buf, w_ref, sl, k_w, base, rows):
    acc = None
    for k in range(k_w):
        t = w_ref[k:k + 1, sl] * buf[base + k:base + k + rows, sl]
        acc = t if acc is None else acc + t
    return acc


def _taps_rev(buf, w_ref, sl, k_w, base, rows):
    acc = None
    for k in range(k_w):
        t = w_ref[k:k + 1, sl] * buf[base + k_w - 1 - k:base + k_w - 1 - k + rows, sl]
        acc = t if acc is None else acc + t
    return acc


def _tap_grads(dw_ref, dc, buf, sl, k_w, base, rows):
    for k in range(k_w):
        dw_ref[k:k + 1, sl] += jnp.sum(dc * buf[base + k:base + k + rows, sl], axis=0, keepdims=True)


def _bra_fwd(proj, cw, *, name):
    s = proj.shape[0]
    tr, h = min(TR, s), H_S
    sub = min(SUB, tr)

    def body(cur, halo, w_ref, za_ref, cvb):
        i = pl.program_id(0)
        hv = halo[:, D:2 * D].astype(F32) * halo[:, 2 * D:3 * D].astype(F32)
        cvb[0:h, :] = jnp.where(i == 0, 0.0, hv)
        cvb[h:h + tr, :] = cur[:, D:2 * D].astype(F32) * cur[:, 2 * D:3 * D].astype(F32)
        for c in range(D // LANE):
            sl = slice(LANE * c, LANE * c + LANE)
            ca = _taps(cvb, w_ref, sl, K_A, h - (K_A - 1), tr)
            za_ref[:, sl] = (cur[:, sl].astype(F32) * ca).astype(BF)

    return _pcall(
        body, grid=(s // tr,),
        in_specs=[pl.BlockSpec((tr, 3 * D), lambda i: (i, 0)),
                  pl.BlockSpec((h, 3 * D), lambda i: (_halo_before(i, tr, h), 0)),
                  pl.BlockSpec((8, D), lambda i: (0, 0))],
        out_specs=pl.BlockSpec((tr, D), lambda i: (i, 0)), out_shape=SDS((s, D), BF),
        scratch_shapes=[pltpu.VMEM((h + tr, D), F32)],
        compiler_params=_cp(("parallel",)), name=name)(proj, proj, cw)


def _bra_bwd(proj, dza, cw, dproj, *, name):
    s = proj.shape[0]
    tr, h = min(TR, s), H_S
    sub = min(SUB, tr)
    n = s // tr

    def body(before, cur, after, dz_cur, dz_after, w_ref, dproj_in, da_ref, dw_ref, cvb, dcab):
        del dproj_in
        i = pl.program_id(0)

        @pl.when(i == 0)
        def _():
            dw_ref[...] = jnp.zeros_like(dw_ref)

        first, last = i == 0, i == n - 1
        cvb[0:h, :] = jnp.where(first, 0.0, before[:, D:2 * D].astype(F32) * before[:, 2 * D:3 * D].astype(F32))
        cvb[h:h + tr, :] = cur[:, D:2 * D].astype(F32) * cur[:, 2 * D:3 * D].astype(F32)
        dcab[0:tr, :] = dz_cur[...].astype(F32) * cur[:, 0:D].astype(F32)
        dcab[tr:tr + h, :] = jnp.where(last, 0.0, dz_after[...].astype(F32) * after[:, 0:D].astype(F32))
        for c in range(D // LANE):
            sl = slice(LANE * c, LANE * c + LANE)
            gl, vl = slice(D + LANE * c, D + LANE * c + LANE), slice(2 * D + LANE * c, 2 * D + LANE * c + LANE)
            for r0 in range(0, tr, sub):
                rows = slice(r0, r0 + sub)
                ca = _taps(cvb, w_ref, sl, K_A, h - (K_A - 1) + r0, sub)
                da_ref[rows, sl] = (dz_cur[rows, sl].astype(F32) * ca).astype(BF)
                dcv = _taps_rev(dcab, w_ref, sl, K_A, r0, sub)
                da_ref[rows, gl] = (dcv * cur[rows, vl].astype(F32)).astype(BF)
                da_ref[rows, vl] = (dcv * cur[rows, gl].astype(F32)).astype(BF)
                _tap_grads(dw_ref, dcab[rows, sl], cvb, sl, K_A, h - (K_A - 1) + r0, sub)

    return _pcall(
        body, grid=(n,),
        in_specs=[pl.BlockSpec((h, 3 * D), lambda i: (_halo_before(i, tr, h), 0)),
                  pl.BlockSpec((tr, 3 * D), lambda i: (i, 0)),
                  pl.BlockSpec((h, 3 * D), lambda i: (_halo_after(i, tr, h, s), 0)),
                  pl.BlockSpec((tr, D), lambda i: (i, 0)),
                  pl.BlockSpec((h, D), lambda i: (_halo_after(i, tr, h, s), 0)),
                  pl.BlockSpec((8, D), lambda i: (0, 0)), ANY],
        out_specs=[pl.BlockSpec((tr, 3 * D), lambda i: (i, 0)), pl.BlockSpec((8, D), lambda i: (0, 0))],
        out_shape=[SDS(dproj.shape, BF), SDS((8, D), F32)], input_output_aliases={6: 0},
        scratch_shapes=[pltpu.VMEM((h + tr, D), F32), pltpu.VMEM((tr + h, D), F32)],
        compiler_params=_cp(("arbitrary",)), name=name)(proj, proj, proj, dza, dza, cw, dproj)


_U_COL, _UG_COL = 3, 4


def _brb_conv_fwd(proj, cw, bias, *, name):
    s = proj.shape[0]
    tr, h = min(TR, s), H_L
    sub = min(SUB, tr)

    def body(u_cur, ug_cur, u_halo, ug_halo, w_ref, b_ref, cb_ref, glb, shifted):
        i = pl.program_id(0)
        glb[0:h, :] = jnp.where(i == 0, 0.0, u_halo[...].astype(F32) * _sigmoid(ug_halo[...].astype(F32)))
        glb[h:h + tr, :] = u_cur[...].astype(F32) * _sigmoid(ug_cur[...].astype(F32))
        for c in range(D // LANE):
            sl = slice(LANE * c, LANE * c + LANE)
            for r in range(1, 8):
                shifted[r] = glb[8 - r:8 - r + tr + 24, sl]
            for r0 in range(0, tr, sub):
                acc = None
                for k in range(K_B):
                    q, r = divmod(K_B - 1 - k, 8)
                    at = r0 - 8 * q
                    win = shifted[r, 24 + at:24 + at + sub, :] if r else glb[h + at:h + at + sub, sl]
                    term = w_ref[k:k + 1, sl] * win
                    acc = term if acc is None else acc + term
                cb_ref[r0:r0 + sub, sl] = (acc + b_ref[:, sl]).astype(BF)

    return _pcall(
        body, grid=(s // tr,),
        in_specs=[pl.BlockSpec((tr, D), lambda i: (i, _U_COL)), pl.BlockSpec((tr, D), lambda i: (i, _UG_COL)),
                  pl.BlockSpec((h, D), lambda i: (_halo_before(i, tr, h), _U_COL)),
                  pl.BlockSpec((h, D), lambda i: (_halo_before(i, tr, h), _UG_COL)),
                  pl.BlockSpec((32, D), lambda i: (0, 0)), pl.BlockSpec((1, D), lambda i: (0, 0))],
        out_specs=pl.BlockSpec((tr, D), lambda i: (i, 0)), out_shape=SDS((s, D), BF),
        scratch_shapes=[pltpu.VMEM((h + tr, D), F32), pltpu.VMEM((8, tr + 24, LANE), F32)],
        compiler_params=_cp(("parallel",)), name=name)(proj, proj, proj, proj, cw, bias)


def _brb_conv_bwd(proj, dcb, dq, cw, dproj, *, name):
    s = proj.shape[0]
    tr, h = min(TR, s), H_L
    sub = min(SUB, tr)
    n = s // tr
    nb = -(-(tr + 24) // sub)
    sel = _row_selector(sub, list(range(8)))

    def body(u_cur, ug_cur, d_cur, d_after, dq_ref, sel_ref, w_ref, dproj_in, db_ref, dw_ref, dcbb, shifted):
        del dproj_in
        i = pl.program_id(0)

        @pl.when(i == 0)
        def _():
            dw_ref[...] = jnp.zeros_like(dw_ref)

        db_ref[:, 2 * D:3 * D] = dq_ref[...]
        after = d_after[...]
        dcbb[0:tr, :] = d_cur[...]
        dcbb[tr:tr + h, :] = jnp.where(i == n - 1, jnp.zeros_like(after), after)
        dcbb[tr + h:(nb + 1) * sub, :] = jnp.zeros(((nb + 1) * sub - h - tr, D), BF)
        for c in range(D // LANE):
            sl = slice(LANE * c, LANE * c + LANE)
            for blk in range(nb):
                res = _dot(sel_ref[...], dcbb[blk * sub:(blk + 2) * sub, sl])
                for r in range(8):
                    shifted[r, blk * sub:(blk + 1) * sub, :] = res[r * sub:(r + 1) * sub]
            for r0 in range(0, tr, sub):
                u = u_cur[r0:r0 + sub, sl].astype(F32)
                sg = _sigmoid(ug_cur[r0:r0 + sub, sl].astype(F32))
                glu = u * sg
                dglu = None
                for k in range(K_B):
                    q, r = divmod(K_B - 1 - k, 8)
                    at = r0 + 8 * q
                    win = shifted[r, at:at + sub, :]
                    term = w_ref[k:k + 1, sl] * win
                    dglu = term if dglu is None else dglu + term
                    dw_ref[k:k + 1, sl] += jnp.sum(win * glu, axis=0, keepdims=True)
                db_ref[r0:r0 + sub, sl] = (dglu * sg).astype(BF)
                db_ref[r0:r0 + sub, D + LANE * c:D + LANE * c + LANE] = (dglu * u * sg * (1.0 - sg)).astype(BF)

    return _pcall(
        body, grid=(n,),
        in_specs=[pl.BlockSpec((tr, D), lambda i: (i, _U_COL)), pl.BlockSpec((tr, D), lambda i: (i, _UG_COL)),
                  pl.BlockSpec((tr, D), lambda i: (i, 0)),
                  pl.BlockSpec((h, D), lambda i: (_halo_after(i, tr, h, s), 0)),
                  pl.BlockSpec((tr, D), lambda i: (i, 0)),
                  pl.BlockSpec(sel.shape, lambda i: (0, 0)),
                  pl.BlockSpec((32, D), lambda i: (0, 0)), ANY],
        out_specs=[pl.BlockSpec((tr, 3 * D), lambda i: (i, 1)), pl.BlockSpec((32, D), lambda i: (0, 0))],
        out_shape=[SDS(dproj.shape, BF), SDS((32, D), F32)], input_output_aliases={7: 0},
        scratch_shapes=[pltpu.VMEM(((nb + 1) * sub, D), BF), pltpu.VMEM((8, nb * sub, LANE), F32)],
        compiler_params=_cp(("arbitrary",)), name=name)(proj, proj, dcb, dcb, dq, sel, cw, dproj)


def _ln_silu_fwd(cb, g, b, *, name):
    s = cb.shape[0]
    tm = min(TM, s)

    def body(cb_ref, g_ref, b_ref, sb_ref):
        z = cb_ref[...].astype(F32)
        zc = z - jnp.mean(z, axis=-1, keepdims=True)
        ln = (zc * lax.rsqrt(jnp.mean(zc * zc, axis=-1, keepdims=True) + NORM_EPS)) * g_ref[...] + b_ref[...]
        sb_ref[...] = (ln * _sigmoid(ln)).astype(BF)

    row = lambda i: (i, 0)
    vec = pl.BlockSpec((1, D), lambda i: (0, 0))
    return _pcall(
        body, grid=(s // tm,), in_specs=[pl.BlockSpec((tm, D), row), vec, vec],
        out_specs=pl.BlockSpec((tm, D), row), out_shape=SDS((s, D), BF),
        compiler_params=_cp(("parallel",)), name=name)(cb, g, b)


def _ln_silu_bwd(cb, dsb, g, b, *, name):
    s = cb.shape[0]
    tm = min(TM, s)

    def body(cb_ref, dsb_ref, g_ref, b_ref, dcb_ref, sums_ref):
        @pl.when(pl.program_id(0) == 0)
        def _():
            sums_ref[...] = jnp.zeros_like(sums_ref)

        z = cb_ref[...].astype(F32)
        zc = z - jnp.mean(z, axis=-1, keepdims=True)
        rstd = lax.rsqrt(jnp.mean(zc * zc, axis=-1, keepdims=True) + NORM_EPS)
        lnh = zc * rstd
        ln = lnh * g_ref[...] + b_ref[...]
        sg = _sigmoid(ln)
        dln = dsb_ref[...].astype(F32) * (sg * (1.0 + ln * (1.0 - sg)))
        sums_ref[0:1, :] += jnp.sum(dln * lnh, axis=0, keepdims=True)
        sums_ref[1:2, :] += jnp.sum(dln, axis=0, keepdims=True)
        dlnh = dln * g_ref[...]
        dz = rstd * (dlnh - jnp.mean(dlnh, axis=-1, keepdims=True)
                     - lnh * jnp.mean(dlnh * lnh, axis=-1, keepdims=True))
        sums_ref[2:3, :] += jnp.sum(dz, axis=0, keepdims=True)
        dcb_ref[...] = dz.astype(BF)

    row = lambda i: (i, 0)
    vec = pl.BlockSpec((1, D), lambda i: (0, 0))
    return _pcall(
        body, grid=(s // tm,), in_specs=[pl.BlockSpec((tm, D), row), pl.BlockSpec((tm, D), row), vec, vec],
        out_specs=[pl.BlockSpec((tm, D), row), pl.BlockSpec((8, D), lambda i: (0, 0))],
        out_shape=[SDS((s, D), BF), SDS((8, D), F32)],
        compiler_params=_cp(("arbitrary",)), name=name)(cb, dsb, g, b)


_Q_COL = 5 * D // HEAD


def _kv_prep(mem, g, wkv, *, name):
    m = mem.shape[0]

    def body(mem_ref, g_ref, w_ref, memn_ref, kv_ref):
        mv = mem_ref[...]
        memn = ((mv * _rms(mv)) * g_ref[...]).astype(BF)
        memn_ref[...] = memn
        for dev in range(N_DEV):
            kv_ref[:, dev * C_KV:(dev + 1) * C_KV] = _dot(memn, w_ref[dev]).astype(BF)

    return _pcall(body, out_shape=[SDS((m, D), BF), SDS((m, 2 * D), BF)],
                  compiler_params=_cp(), name=name)(mem, g, wkv)


def _softmax_rows(q, k):
    sc = _dot_nt(q, k) * (1.0 / (HEAD ** 0.5))
    e = jnp.exp(sc - jnp.max(sc, axis=-1, keepdims=True))
    return e / jnp.sum(e, axis=-1, keepdims=True)


def _attn_fwd(proj, kv, *, name):
    s, m = proj.shape[0], kv.shape[0]
    tm = min(TM, s)

    def body(q_ref, kv_ref, o_ref):
        for hd in range(N_HEADS):
            cols = slice(hd * HEAD, (hd + 1) * HEAD)
            p = _softmax_rows(q_ref[:, cols], kv_ref[:, cols])
            o_ref[:, cols] = _dot(p.astype(BF), kv_ref[:, D + hd * HEAD:D + (hd + 1) * HEAD]).astype(BF)

    return _pcall(
        body, grid=(s // tm,),
        in_specs=[pl.BlockSpec((tm, D), lambda i: (i, _Q_COL // N_HEADS)),
                  pl.BlockSpec((m, 2 * D), lambda i: (0, 0))],
        out_specs=pl.BlockSpec((tm, D), lambda i: (i, 0)), out_shape=SDS((s, D), BF),
        compiler_params=_cp(("parallel",)), name=name)(proj, kv)


def _attn_bwd(proj, kv, do, *, name):
    s, m = proj.shape[0], kv.shape[0]
    tm = min(TM, s)

    def body(q_ref, kv_ref, do_ref, dq_ref, dk_ref, dv_ref):
        @pl.when(pl.program_id(0) == 0)
        def _():
            dk_ref[...] = jnp.zeros_like(dk_ref)
            dv_ref[...] = jnp.zeros_like(dv_ref)

        for hd in range(N_HEADS):
            cols = slice(hd * HEAD, (hd + 1) * HEAD)
            q, k, dov = q_ref[:, cols], kv_ref[:, cols], do_ref[:, cols]
            p = _softmax_rows(q, k)
            dp = _dot_nt(dov, kv_ref[:, D + hd * HEAD:D + (hd + 1) * HEAD])
            dv_ref[:, cols] += _dot_tn(p.astype(BF), dov)
            ds = (p * (dp - jnp.sum(dp * p, axis=-1, keepdims=True)) * (1.0 / (HEAD ** 0.5))).astype(BF)
            dq_ref[:, cols] = _dot(ds, k).astype(BF)
            dk_ref[:, cols] += _dot_tn(ds, q)

    return _pcall(
        body, grid=(s // tm,),
        in_specs=[pl.BlockSpec((tm, D), lambda i: (i, _Q_COL // N_HEADS)),
                  pl.BlockSpec((m, 2 * D), lambda i: (0, 0)),
                  pl.BlockSpec((tm, D), lambda i: (i, 0))],
        out_specs=[pl.BlockSpec((tm, D), lambda i: (i, 0)),
                   pl.BlockSpec((m, D), lambda i: (0, 0)),
                   pl.BlockSpec((m, D), lambda i: (0, 0))],
        out_shape=[SDS((s, D), BF), SDS((m, D), F32), SDS((m, D), F32)],
        compiler_params=_cp(("arbitrary",)), name=name)(proj, kv, do)


def _kv_bwd(mem, g, memn, dk, dv, wkv, *, name):
    def body(mem_ref, g_ref, memn_ref, dk_ref, dv_ref, w_ref, dw_ref, dg_ref):
        memn = memn_ref[...]
        dmemn = None
        for dev in range(N_DEV):
            d_ref, col = (dk_ref, dev) if dev < N_HEADS else (dv_ref, dev - N_HEADS)
            dslab = d_ref[:, col * C_KV:(col + 1) * C_KV].astype(BF)
            dw_ref[dev] = _dot_tn(memn, dslab).astype(BF)
            part = _dot_nt(dslab, w_ref[dev])
            dmemn = part if dmemn is None else dmemn + part
        mv = mem_ref[...]
        dg_ref[...] = jnp.zeros_like(dg_ref)
        dg_ref[0:1, :] = jnp.sum(dmemn * (mv * _rms(mv)), axis=0, keepdims=True)

    assert C_KV == HEAD
    return _pcall(body, out_shape=[SDS((N_DEV, D, C_KV), BF), SDS((8, D), F32)],
                  compiler_params=_cp(), name=name)(mem, g, memn, dk, dv, wkv)


_TM_MIX = 512


def _mix_out(x, za, sb, o, proj, w4, bg, g_next, *, name):
    s = x.shape[0]
    tm = min(_TM_MIX, s)

    def body(x_ref, za_ref, sb_ref, o_ref, pg_ref, w4_ref, bg_ref, gn_ref,
             ya_ref, yb_ref, yc_ref, mg_ref, x1_ref, h_ref):
        ys = (_dot(za_ref[...], w4_ref[0]), _dot(sb_ref[...], w4_ref[1]), _dot(o_ref[...], w4_ref[2]))
        merged = None
        for j, (y, y_ref) in enumerate(zip(ys, (ya_ref, yb_ref, yc_ref))):
            y_ref[...] = y.astype(BF)
            gate = _sigmoid(pg_ref[:, j * D:(j + 1) * D].astype(F32) + bg_ref[:, j * D:(j + 1) * D])
            merged = gate * y if merged is None else merged + gate * y
        mg = merged.astype(BF)
        mg_ref[...] = mg
        x1 = x_ref[...] + _dot(mg, w4_ref[3])
        x1_ref[...] = x1
        h_ref[...] = ((x1 * _rms(x1)) * gn_ref[...]).astype(BF)

    row = lambda i: (i, 0)
    act = pl.BlockSpec((tm, D), row)
    return _pcall(
        body, grid=(s // tm,),
        in_specs=[act, act, act, act, pl.BlockSpec((tm, 3 * D), lambda i: (i, 2)),
                  pl.BlockSpec((4, D, D), lambda i: (0, 0, 0), pipeline_mode=pl.Buffered(1)), pl.BlockSpec((1, 3 * D), lambda i: (0, 0)),
                  pl.BlockSpec((1, D), lambda i: (0, 0))],
        out_specs=[act] * 6,
        out_shape=[SDS((s, D), BF)] * 4 + [SDS((s, D), F32), SDS((s, D), BF)],
        compiler_params=_cp(("parallel",)), name=name)(x, za, sb, o, proj, w4, bg, g_next)


def _mix_bwd(dxb, ya, yb, yc, proj, w4, bg, *, name):
    s = dxb.shape[0]
    tm = min(_TM_MIX, s)

    def body(dx_ref, ya_ref, yb_ref, yc_ref, pg_ref, w4_ref, bg_ref,
             dya_ref, dyb_ref, dyc_ref, dza_ref, dsb_ref, do_ref, dgt_ref, dbg_ref):
        @pl.when(pl.program_id(0) == 0)
        def _():
            dbg_ref[...] = jnp.zeros_like(dbg_ref)

        dm = _dot_nt(dx_ref[...], w4_ref[3])
        for j, (y_ref, dy_ref, din_ref) in enumerate(zip((ya_ref, yb_ref, yc_ref), (dya_ref, dyb_ref, dyc_ref),
                                                         (dza_ref, dsb_ref, do_ref))):
            cols = slice(j * D, (j + 1) * D)
            gate = _sigmoid(pg_ref[:, cols].astype(F32) + bg_ref[:, cols])
            dy = (dm * gate).astype(BF)
            dy_ref[...] = dy
            din_ref[...] = _dot_nt(dy, w4_ref[j]).astype(BF)
            dpre = dm * y_ref[...].astype(F32) * gate * (1.0 - gate)
            dgt_ref[:, cols] = dpre.astype(BF)
            dbg_ref[0:1, cols] += jnp.sum(dpre, axis=0, keepdims=True)

    row = lambda i: (i, 0)
    act = pl.BlockSpec((tm, D), row)
    return _pcall(
        body, grid=(s // tm,),
        in_specs=[act, act, act, act, pl.BlockSpec((tm, 3 * D), lambda i: (i, 2)),
                  pl.BlockSpec((4, D, D), lambda i: (0, 0, 0), pipeline_mode=pl.Buffered(1)), pl.BlockSpec((1, 3 * D), lambda i: (0, 0))],
        out_specs=[act] * 6 + [pl.BlockSpec((tm, 3 * D), lambda i: (i, 2)),
                               pl.BlockSpec((8, 3 * D), lambda i: (0, 0))],
        out_shape=[SDS((s, D), BF)] * 6 + [SDS((s, 9 * D), BF), SDS((8, 3 * D), F32)],
        compiler_params=_cp(("arbitrary",)), name=name)(dxb, ya, yb, yc, proj, w4, bg)


_PAIR = 2 * C_UP_P


def _row_selector(sub, first_cols):
    rows = np.arange(len(first_cols) * sub)
    col = np.asarray(first_cols)[rows // sub] + rows % sub
    return jnp.asarray(np.arange(2 * sub)[None, :] == col[:, None], BF)


def _ffn_act(u2, cw, *, name):
    s = u2.shape[0]
    tr = min(TR, s)
    sub = min(SUB, tr)
    sel = _row_selector(sub, [sub - (K_F - 1 - k) for k in range(K_F)])

    def body(cur, prev, sel_ref, w_ref, act_ref, c2_ref, xb, win):
        i = pl.program_id(1)
        before = prev[...]
        xb[0:sub, :] = jnp.where(i == 0, jnp.zeros_like(before), before)
        xb[sub:sub + tr, :] = cur[...]
        for r0 in range(0, tr, sub):
            win[...] = _dot(sel_ref[...], xb[r0:r0 + 2 * sub, :])
            for c in range(C_UP_P // LANE):
                gl = slice(LANE * c, LANE * c + LANE)
                ul = slice(C_UP_P + LANE * c, C_UP_P + LANE * c + LANE)
                gt = sum(w_ref[k:k + 1, gl] * win[k * sub:(k + 1) * sub, gl] for k in range(K_F))
                up = sum(w_ref[k:k + 1, ul] * win[k * sub:(k + 1) * sub, ul] for k in range(K_F))
                c2_ref[r0:r0 + sub, gl] = gt.astype(BF)
                c2_ref[r0:r0 + sub, ul] = up.astype(BF)
                act_ref[r0:r0 + sub, gl] = (gt * _sigmoid(gt) * up).astype(BF)

    return _pcall(
        body, grid=(4, s // tr),
        in_specs=[pl.BlockSpec((tr, _PAIR), lambda p, i: (i, p)),
                  pl.BlockSpec((sub, _PAIR), lambda p, i: (_halo_before(i, tr, sub), p)),
                  pl.BlockSpec(sel.shape, lambda p, i: (0, 0)),
                  pl.BlockSpec((8, _PAIR), lambda p, i: (0, p))],
        out_specs=[pl.BlockSpec((tr, C_UP_P), lambda p, i: (i, p)), pl.BlockSpec((tr, _PAIR), lambda p, i: (i, p))],
        out_shape=[SDS((s, FF_P), BF), SDS((s, 2 * FF_P), BF)],
        scratch_shapes=[pltpu.VMEM((sub + tr, _PAIR), BF), pltpu.VMEM((K_F * sub, _PAIR), F32)],
        compiler_params=_cp(("parallel", "parallel")), name=name)(u2, u2, sel, cw)


def _ffn_bwd(u2, c2, dact, cw, *, name):
    s = u2.shape[0]
    tr, h = min(TR, s), H_S
    sub = min(SUB, tr)
    n = s // tr
    sel = _row_selector(sub, [K_F - 1 - k for k in range(K_F)])

    def body(u_cur, c_cur, c_after, da_cur, da_after, sel_ref, w_ref, du_ref, dw_ref, dcb, win):
        i = pl.program_id(1)
        last = i == n - 1

        @pl.when(i == 0)
        def _():
            dw_ref[...] = jnp.zeros_like(dw_ref)

        def conv_grad(gt, up, da):
            gt, up, da = gt.astype(F32), up.astype(F32), da.astype(F32)
            sg = _sigmoid(gt)
            return (da * up * (sg * (1.0 + gt * (1.0 - sg)))).astype(BF), (da * (gt * sg)).astype(BF)

        for c in range(C_UP_P // LANE):
            gl = slice(LANE * c, LANE * c + LANE)
            ul = slice(C_UP_P + LANE * c, C_UP_P + LANE * c + LANE)
            for r0 in range(0, tr, sub):
                rows = slice(r0, r0 + sub)
                dcb[rows, gl], dcb[rows, ul] = conv_grad(c_cur[rows, gl], c_cur[rows, ul], da_cur[rows, gl])
            dg, du_ = conv_grad(c_after[:, gl], c_after[:, ul], da_after[:, gl])
            dcb[tr:tr + h, gl] = jnp.where(last, jnp.zeros_like(dg), dg)
            dcb[tr:tr + h, ul] = jnp.where(last, jnp.zeros_like(du_), du_)
        dcb[tr + h:tr + sub, :] = jnp.zeros((sub - h, _PAIR), BF)
        for r0 in range(0, tr, sub):
            win[...] = _dot(sel_ref[...], dcb[r0:r0 + 2 * sub, :])
            for c in range(_PAIR // LANE):
                sl = slice(LANE * c, LANE * c + LANE)
                u = u_cur[r0:r0 + sub, sl].astype(F32)
                du = None
                for k in range(K_F):
                    wk = win[k * sub:(k + 1) * sub, sl]
                    term = w_ref[k:k + 1, sl] * wk
                    du = term if du is None else du + term
                    dw_ref[k:k + 1, sl] += jnp.sum(wk * u, axis=0, keepdims=True)
                du_ref[r0:r0 + sub, sl] = du.astype(BF)

    return _pcall(
        body, grid=(4, n),
        in_specs=[pl.BlockSpec((tr, _PAIR), lambda p, i: (i, p)),
                  pl.BlockSpec((tr, _PAIR), lambda p, i: (i, p)),
                  pl.BlockSpec((h, _PAIR), lambda p, i: (_halo_after(i, tr, h, s), p)),
                  pl.BlockSpec((tr, C_UP_P), lambda p, i: (i, p)),
                  pl.BlockSpec((h, C_UP_P), lambda p, i: (_halo_after(i, tr, h, s), p)),
                  pl.BlockSpec(sel.shape, lambda p, i: (0, 0)),
                  pl.BlockSpec((8, _PAIR), lambda p, i: (0, p))],
        out_specs=[pl.BlockSpec((tr, _PAIR), lambda p, i: (i, p)), pl.BlockSpec((8, _PAIR), lambda p, i: (0, p))],
        out_shape=[SDS((s, 2 * FF_P), BF), SDS((8, 2 * FF_P), F32)],
        scratch_shapes=[pltpu.VMEM((tr + sub, _PAIR), BF), pltpu.VMEM((K_F * sub, _PAIR), F32)],
        compiler_params=_cp(("parallel", "arbitrary")), name=name)(u2, c2, c2, dact, dact, sel, cw)


def _relations():
    x, y, c = lax.axis_index("x"), lax.axis_index("y"), lax.axis_index("c")
    out = []
    for r in range(1, N_DEV):
        rx, ry, rc = (r >> 2) & 1, (r >> 1) & 1, r & 1
        out.append((r, (x ^ rx, y ^ ry, c ^ rc)))
    return out


def _my_index():
    return 4 * lax.axis_index("x") + 2 * lax.axis_index("y") + lax.axis_index("c")


def _slab(kind, ref, idx):
    if kind == "win":
        return ref.at[:, pl.ds(pl.multiple_of(idx * C_IN, LANE), C_IN)]
    if kind == "wup":
        return ref.at[:, pl.ds(pl.multiple_of(_up_slot(idx) * C_UP_P, LANE), C_UP_P)]
    if kind == "wupT":
        return ref.at[pl.ds(pl.multiple_of(_up_slot(idx) * C_UP_P, LANE), C_UP_P), :]
    if kind == "wkv":
        return ref.at[idx]
    if kind == "w4":
        return ref.at[:, pl.ds(pl.multiple_of(idx * R_O, 16), R_O), :]
    if kind == "wdn":
        return ref.at[pl.ds(pl.multiple_of(_dn_row(idx), 16), R_DN), :]
    assert kind == "cv"
    return ref.at[idx]


_WHOLE = {"win": ((D, 9 * D), BF), "wup": ((D, 2 * FF_P), BF), "wkv": ((N_DEV, D, C_KV), BF),
          "w4": ((4, D, D), BF), "wdn": ((FF_P, D), BF)}
_SHARD = {"win": (D, C_IN), "wup": (D, C_UP_P), "wupT": (C_UP_P, D), "wkv": (D, C_KV), "w4": (4, R_O, D),
          "wdn": (R_DN, D)}
HBM_SPEC = pl.BlockSpec(memory_space=pltpu.HBM)
SEM_SPEC = pl.BlockSpec(memory_space=pltpu.SEMAPHORE)
_DATAFLOW = pltpu.SideEffectType.DATAFLOW_SIDE_EFFECTING


def _scatter_maps(kinds):
    return ((lambda srcs, lands, a, idx: _slab(kinds[a], srcs[a], idx)),
            (lambda lands, a, idx: lands[a].at[idx]))


_SLOTTED = ("wkv", "cv")


def _own_slab_blocks(kind, shard_shape):
    if kind in ("win", "wup"):
        rows, slot = 256, (_up_slot if kind == "wup" else (lambda m: m))
        return (shard_shape[0] // rows, (rows, shard_shape[1]), (lambda i, me: (i, slot(me[0]))),
                (lambda i, me: (i, 0)), (lambda i, me: (me[0], i, 0)))
    if kind == "wupT":
        rows = 256
        steps = shard_shape[0] // rows
        return (steps, (rows, D), (lambda i, me: (_up_slot(me[0]) * steps + i, 0)), (lambda i, me: (i, 0)),
                (lambda i, me: (me[0], i, 0)))
    if kind == "w4":
        return (1, shard_shape, (lambda i, me: (0, me[0], 0)), (lambda i, me: (0, 0, 0)),
                (lambda i, me: (me[0], 0, 0, 0)))
    if kind == "wdn":
        rows = 32
        return (R_DN // rows, (rows, D), (lambda i, me: (_dn_row(me[0]) // rows + i, 0)), (lambda i, me: (i, 0)),
                (lambda i, me: (me[0], i, 0)))
    assert kind in _SLOTTED
    rows = min(256, shard_shape[0])
    return (shard_shape[0] // rows, (rows, shard_shape[1]), (lambda i, me: (me[0], i, 0)),
            (lambda i, me: (i, 0)), (lambda i, me: (me[0], i, 0)))


def _place_own(kind, src, out_sds, gather, me_arr, *, name):
    shard_shape = src.shape if gather else out_sds.shape[1:]
    steps, blk, whole_idx, shard_idx, staging_idx = _own_slab_blocks(kind, shard_shape)
    slotted = kind in _SLOTTED
    whole_spec = pl.BlockSpec(((None,) if slotted else ()) + tuple(blk), whole_idx)
    if gather:
        in_spec, out_spec = pl.BlockSpec(tuple(blk), shard_idx), whole_spec
    else:
        in_spec, out_spec = whole_spec, pl.BlockSpec((None,) + tuple(blk), staging_idx)
    zero_init = gather and kind == "wdn"

    def body(me_ref, src_ref, *rest):
        rest[-1][...] = src_ref[...].astype(rest[-1].dtype)

    operands = (me_arr, src) + ((jnp.zeros(out_sds.shape, out_sds.dtype),) if zero_init else ())
    return _pcall(
        body,
        grid_spec=pltpu.PrefetchScalarGridSpec(
            num_scalar_prefetch=1, grid=(steps,), in_specs=[in_spec] + ([ANY] if zero_init else []),
            out_specs=out_spec),
        out_shape=out_sds, input_output_aliases={2: 0} if zero_init else {},
        compiler_params=_cp(("arbitrary",)), name=name)(*operands)


def _peer_copies(n, src_of, dst_of, src_r, land_r, ssem, rsem):
    me = _my_index()
    out = []
    for r, peer in _relations():
        p_idx = 4 * peer[0] + 2 * peer[1] + peer[2]
        for a in range(n):
            def copy(src_idx, dst_idx, a=a, r=r, peer=peer):
                sem = a * (N_DEV - 1) + r - 1
                return pltpu.make_async_remote_copy(
                    src_ref=src_of(src_r, land_r, a, src_idx), dst_ref=dst_of(land_r, a, dst_idx),
                    send_sem=ssem.at[sem], recv_sem=rsem.at[sem], device_id=peer, device_id_type=MESH)
            out.append((functools.partial(copy, p_idx, me), functools.partial(copy, me, p_idx)))
    return out


def _exchange_start(srcs, lands, maps, after, *, name):
    n, ns = len(lands), len(srcs)
    src_of, dst_of = maps

    def body(*refs):
        src_r, land_r = refs[:ns], refs[ns:ns + n]
        ssem, rsem, token = refs[ns + n + 1], refs[ns + n + 2], refs[-1]
        for send, _ in _peer_copies(n, src_of, dst_of, src_r, land_r, ssem, rsem):
            send().start()
        token[...] = jnp.zeros_like(token)

    flight = list(srcs) + list(lands)
    outs = pl.pallas_call(
        body, name=name,
        out_shape=(pltpu.SemaphoreType.DMA((n * (N_DEV - 1),)), pltpu.SemaphoreType.DMA((n * (N_DEV - 1),)),
                   *[pltpu.HBM(a.shape, a.dtype) for a in flight], SDS((8, LANE), F32)),
        in_specs=[HBM_SPEC] * (ns + n) + [ANY],
        out_specs=(SEM_SPEC, SEM_SPEC, *[HBM_SPEC] * (ns + n), pl.BlockSpec(memory_space=pltpu.VMEM)),
        input_output_aliases={i: 2 + i for i in range(ns + n)},
        compiler_params=pltpu.CompilerParams(has_side_effects=_DATAFLOW),
    )(*[pltpu.with_memory_space_constraint(a, pltpu.HBM) for a in flight], after)
    return (outs[0], outs[1], list(outs[2:2 + ns + n]), ns), outs[-1]


def _exchange_wait(handle, maps, after, *, name):
    ssem, rsem, flight, ns = handle
    n = len(flight) - ns
    src_of, dst_of = maps

    def body(*refs):
        src_r, land_r, ssem_r, rsem_r = refs[:ns], refs[ns:ns + n], refs[ns + n], refs[ns + n + 1]
        for send, arrival in _peer_copies(n, src_of, dst_of, src_r, land_r, ssem_r, rsem_r):
            send().wait_send()
            arrival().wait_recv()

    outs = pl.pallas_call(
        body, name=name, out_shape=[pltpu.HBM(a.shape, a.dtype) for a in flight],
        in_specs=[HBM_SPEC] * (ns + n) + [SEM_SPEC, SEM_SPEC, ANY], out_specs=[HBM_SPEC] * (ns + n),
        input_output_aliases={i: i for i in range(ns + n)},
        compiler_params=pltpu.CompilerParams(has_side_effects=_DATAFLOW),
    )(*flight, ssem, rsem, after)
    return list(outs[ns:])


_SIBLING = 1
_ICI = (2, 4, 6)


def _rel_peer(r):
    x, y, c = lax.axis_index("x"), lax.axis_index("y"), lax.axis_index("c")
    peer = (x ^ ((r >> 2) & 1), y ^ ((r >> 1) & 1), c ^ (r & 1))
    return peer, 4 * peer[0] + 2 * peer[1] + peer[2]


def _rcopy(ref, ssem, rsem, peer):
    return pltpu.make_async_remote_copy(src_ref=ref, dst_ref=ref, send_sem=ssem, recv_sem=rsem, device_id=peer,
                                        device_id_type=MESH)


def _gather2_start(lands, kinds, after, *, name):
    n = len(lands)

    def body(*refs):
        land_r, (send1, recv_sib, recv_ici), token = refs[:n], refs[n + 1:n + 4], refs[-1]
        me = _my_index()
        for a in range(n):
            own = _slab(kinds[a], land_r[a], me)
            for j, r in enumerate((_SIBLING,) + _ICI):
                rsem = recv_sib.at[a] if r == _SIBLING else recv_ici.at[3 * a + j - 1]
                _rcopy(own, send1.at[4 * a + j], rsem, _rel_peer(r)[0]).start()
        token[...] = jnp.zeros_like(token)

    sems = [pltpu.SemaphoreType.DMA((4 * n,)), pltpu.SemaphoreType.DMA((n,)), pltpu.SemaphoreType.DMA((3 * n,))]
    outs = pl.pallas_call(
        body, name=name, out_shape=(*sems, *[pltpu.HBM(a.shape, a.dtype) for a in lands], SDS((8, LANE), F32)),
        in_specs=[HBM_SPEC] * n + [ANY],
        out_specs=(SEM_SPEC,) * 3 + (HBM_SPEC,) * n + (pl.BlockSpec(memory_space=pltpu.VMEM),),
        input_output_aliases={i: 3 + i for i in range(n)},
        compiler_params=pltpu.CompilerParams(has_side_effects=_DATAFLOW),
    )(*[pltpu.with_memory_space_constraint(a, pltpu.HBM) for a in lands], after)
    return dict(send1=outs[0], recv_sib=outs[1], recv_ici=outs[2], lands=list(outs[3:3 + n])), outs[-1]


def _gather2_forward(handle, kinds, after, *, name):
    lands = handle["lands"]
    n = len(lands)

    def body(*refs):
        land_r, recv_ici, (fwd_send, fwd_recv), token = refs[:n], refs[n], refs[n + 2:n + 4], refs[-1]
        sibling = _rel_peer(_SIBLING)[0]
        for a in range(n):
            for j, r in enumerate(_ICI):
                got = _slab(kinds[a], land_r[a], _rel_peer(r)[1])
                _rcopy(got, fwd_send.at[3 * a + j], recv_ici.at[3 * a + j], sibling).wait_recv()
                _rcopy(got, fwd_send.at[3 * a + j], fwd_recv.at[3 * a + j], sibling).start()
        token[...] = jnp.zeros_like(token)

    sems = [pltpu.SemaphoreType.DMA((3 * n,)), pltpu.SemaphoreType.DMA((3 * n,))]
    outs = pl.pallas_call(
        body, name=name, out_shape=(*sems, *[pltpu.HBM(a.shape, a.dtype) for a in lands], SDS((8, LANE), F32)),
        in_specs=[HBM_SPEC] * n + [SEM_SPEC, ANY],
        out_specs=(SEM_SPEC,) * 2 + (HBM_SPEC,) * n + (pl.BlockSpec(memory_space=pltpu.VMEM),),
        input_output_aliases={i: 2 + i for i in range(n)},
        compiler_params=pltpu.CompilerParams(has_side_effects=_DATAFLOW),
    )(*lands, handle["recv_ici"], after)
    return dict(handle, fwd_send=outs[0], fwd_recv=outs[1], lands=list(outs[2:2 + n])), outs[-1]


def _gather2_wait(handle, kinds, after, *, name):
    lands = handle["lands"]
    n = len(lands)

    def body(*refs):
        land_r, (send1, recv_sib, fwd_send, fwd_recv) = refs[:n], refs[n:n + 4]
        me = _my_index()
        sibling, sib_idx = _rel_peer(_SIBLING)
        for a in range(n):
            own = _slab(kinds[a], land_r[a], me)
            for j, r in enumerate((_SIBLING,) + _ICI):
                _rcopy(own, send1.at[4 * a + j], recv_sib.at[a], _rel_peer(r)[0]).wait_send()
            theirs = _slab(kinds[a], land_r[a], sib_idx)
            _rcopy(theirs, send1.at[4 * a], recv_sib.at[a], sibling).wait_recv()
            for j, r in enumerate(_ICI):
                passed_on = _slab(kinds[a], land_r[a], _rel_peer(r)[1])
                _rcopy(passed_on, fwd_send.at[3 * a + j], fwd_recv.at[3 * a + j], sibling).wait_send()
                arrived = _slab(kinds[a], land_r[a], _rel_peer(r ^ _SIBLING)[1])
                _rcopy(arrived, fwd_send.at[3 * a + j], fwd_recv.at[3 * a + j], sibling).wait_recv()

    outs = pl.pallas_call(
        body, name=name, out_shape=[pltpu.HBM(a.shape, a.dtype) for a in lands],
        in_specs=[HBM_SPEC] * n + [SEM_SPEC] * 4 + [ANY], out_specs=[HBM_SPEC] * n,
        input_output_aliases={i: i for i in range(n)},
        compiler_params=pltpu.CompilerParams(has_side_effects=_DATAFLOW),
    )(*lands, handle["send1"], handle["recv_sib"], handle["fwd_send"], handle["fwd_recv"], after)
    return list(outs)


def _sum_slots(gathered, *, name):
    def body(g_ref, out_ref):
        total = g_ref[0]
        for dev in range(1, N_DEV):
            total = total + g_ref[dev]
        out_ref[...] = total

    return _pcall(body, out_shape=SDS(gathered.shape[1:], F32), compiler_params=_cp(), name=name)(gathered)


def _adam(g, w, m, v):
    nm = ADAM_B1 * m + (1.0 - ADAM_B1) * g
    nv = ADAM_B2 * v + (1.0 - ADAM_B2) * (g * g)
    m_hat = nm / (1.0 - ADAM_B1 ** ADAM_STEP)
    v_hat = nv / (1.0 - ADAM_B2 ** ADAM_STEP)
    return -ADAM_LR * (m_hat / (jnp.sqrt(v_hat) + ADAM_EPS) + ADAM_WD * w), nm, nv


def _adamw_staged(st0, st1, w, m, v, *, name):
    _, rows, cols = w.shape
    st_cols = st0.shape[2]
    tr = max(t for t in range(16, 129, 16) if rows % t == 0)
    nr = rows // tr

    def body(s0_ref, s1_ref, w_ref, m_ref, v_ref, g_ref, d_ref, nm_ref, nv_ref):
        for layer, s_ref in enumerate((s0_ref, s1_ref)):
            @pl.when(pl.program_id(0) == layer)
            def _(s_ref=s_ref):
                total = s_ref[0, :, 0:cols].astype(F32)
                for dev in range(1, N_DEV):
                    total = total + s_ref[dev, :, 0:cols].astype(F32)
                g_ref[0] = total

        d_ref[0], nm_ref[0], nv_ref[0] = _adam(g_ref[0], w_ref[0], m_ref[0], v_ref[0])

    st_spec = lambda layer: pl.BlockSpec(
        (N_DEV, tr, st_cols), lambda l, i: (0, jnp.where(l == layer, i, (nr - 1) * (1 - layer)), 0))
    par = pl.BlockSpec((1, tr, cols), lambda l, i: (l, i, 0))
    return _pcall(
        body, grid=(DEPTH, nr), in_specs=[st_spec(0), st_spec(1), par, par, par], out_specs=[par] * 4,
        out_shape=[SDS(w.shape, F32)] * 4,
        compiler_params=_cp(("arbitrary", "arbitrary")), name=name)(st0, st1, w, m, v)


def _adamw_small(g, w, m, v, *, name):
    def body(g_ref, w_ref, m_ref, v_ref, d_ref, nm_ref, nv_ref):
        d_ref[...], nm_ref[...], nv_ref[...] = _adam(g_ref[...], w_ref[...], m_ref[...], v_ref[...])

    return _pcall(body, out_shape=[SDS(g.shape, F32)] * 3, compiler_params=_cp(), name=name)(g, w, m, v)


def _pack_rows(arrays):
    flat = jnp.concatenate([a.reshape(-1).astype(F32) for a in arrays])
    rows = -(-flat.shape[0] // (8 * D)) * 8
    return jnp.pad(flat, (0, rows * D - flat.shape[0])).reshape(rows, D)


def _unpack_rows(pack, like):
    flat = pack.reshape(-1)
    out, at = [], 0
    for a in like:
        out.append(flat[at:at + a.size].reshape(a.shape))
        at += a.size
    return out


def _layer_fwd(x, h, mem, win, rest_of_weights, after_up, small, g_next, tag):
    proj = _mm(h, win, tm=1024, tn=1536, name=f"proj_{tag}")
    wup, wkv, w4, wdn, cw_a, cw_b, cw_f = rest_of_weights(proj)
    za = _bra_fwd(proj, cw_a, name=f"bra_fwd_{tag}")
    cb = _brb_conv_fwd(proj, cw_b, small["conv_b_bias"], name=f"brb_conv_fwd_{tag}")
    sb = _ln_silu_fwd(cb, small["ln_b_g"], small["ln_b_b"], name=f"ln_silu_fwd_{tag}")
    memn, kv = _kv_prep(mem, small["norm_mem_g"], wkv, name=f"kv_prep_{tag}")
    o = _attn_fwd(proj, kv, name=f"attn_fwd_{tag}")
    ya, yb, yc, mg, x1, h2 = _mix_out(x, za, sb, o, proj, w4, small["b_gate"], small["norm_ffn_g"],
                                      name=f"mix_out_{tag}")
    u2 = _mm(h2, wup, tm=1024, tn=1536, name=f"up_{tag}")
    token = after_up(u2)
    act, c2 = _ffn_act(u2, cw_f if token is None else _behind(cw_f, token), name=f"ffn_act_{tag}")
    x2, h_next = _mm_res_norm(act, wdn, x1, g_next, name=f"down_{tag}")
    saved = dict(x=x, h=h, proj=proj, za=za, cb=cb, sb=sb, memn=memn, kv=kv, o=o, ya=ya, yb=yb, yc=yc,
                 mg=mg, x1=x1, h2=h2, u2=u2, c2=c2, act=act)
    return x2, h_next, (win, wup, wkv, w4, wdn, cw_a, cw_b, cw_f), saved


def _behind(operand, token):
    return operand + token[0:1, 0:1]


def _layer_bwd(dx2, dx2b, mem, wts, small, sv, start, tag):
    win, wup, wkv, w4, wdn, cw_a, cw_b, cw_f = wts
    dact = _mm(dx2b, wdn, tb=True, tm=1024, tn=768, name=f"d_act_{tag}")
    dwdn = _mm(sv["act"], dx2b, ta=True, tm=768, tn=1024, name=f"dw_down_{tag}")
    du2, dcw_f = _ffn_bwd(sv["u2"], sv["c2"], dact, cw_f, name=f"ffn_bwd_{tag}")
    dwup_t = _mm(du2, sv["h2"], ta=True, tm=C_UP_P, tn=1024, name=f"dw_up_{tag}")
    token = start(("wdn", "wupT"), (dwdn, dwup_t), f"ffn_{tag}")
    dx1, dx1b, dg_ffn = _mm_nt_normbwd(du2, wup, sv["x1"], dx2, _behind(small["norm_ffn_g"], token),
                                       name=f"d_h2_{tag}")

    dya, dyb, dyc, dza, dsb, do, dproj, dbg = _mix_bwd(dx1b, sv["ya"], sv["yb"], sv["yc"], sv["proj"], w4,
                                                      small["b_gate"], name=f"mix_bwd_{tag}")
    dw4 = jnp.stack([
        _mm(a, b, ta=True, tm=1024, tn=512, name=f"dw_{nm}_{tag}")
        for nm, a, b in (("a_out", sv["za"], dya), ("b_out", sv["sb"], dyb), ("att_out", sv["o"], dyc),
                         ("o", sv["mg"], dx1b))])
    dq, dk, dv = _attn_bwd(sv["proj"], sv["kv"], do, name=f"attn_bwd_{tag}")
    dwkv, dg_mem = _kv_bwd(mem, small["norm_mem_g"], sv["memn"], dk, dv, wkv, name=f"kv_bwd_{tag}")
    token = start(("w4", "wkv"), (dw4, dwkv), f"mix_{tag}")
    dproj, dcw_a = _bra_bwd(sv["proj"], dza, _behind(cw_a, token), dproj, name=f"bra_bwd_{tag}")
    dcb, ln_sums = _ln_silu_bwd(sv["cb"], dsb, small["ln_b_g"], small["ln_b_b"], name=f"ln_silu_bwd_{tag}")
    dproj, dcw_b = _brb_conv_bwd(sv["proj"], dcb, dq, cw_b, dproj, name=f"brb_conv_bwd_{tag}")
    dwin = _mm(sv["h"], dproj, ta=True, tm=1024, tn=768, name=f"dw_in_{tag}")
    token = start(("win",), (dwin,), f"in_{tag}")
    dx, dxb, dg_mix = _mm_nt_normbwd(dproj, win, sv["x"], dx1, _behind(small["norm_mix_g"], token),
                                     tk=4608, name=f"d_h_{tag}")

    small_grads = [dg_mix[0:1], dg_mem[0:1], dbg[0:1].reshape(3, D), ln_sums[2:3], ln_sums[0:1], ln_sums[1:2],
                   dg_ffn[0:1], dcw_a[0:K_A], dcw_b[0:K_B], dcw_f[0:K_F].reshape(K_F * 2 * FF_P // D, D)]
    return dx, dxb, small_grads, token


_SMALL_ROWS = (1, 1, 3, 1, 1, 1, 1, K_A, K_B, K_F * 2 * FF_P // D)
_CV_ROWS = 48


def kernel(x, mem, norm_mix_g, norm_mem_g, w_in, b_gate, conv_a_w, w_a_out, conv_b_w, conv_b_bias, ln_b_g, ln_b_b, w_b_out, w_kv, w_att_out, w_o, norm_ffn_g, w_up, conv_ffn_w, w_down, norm_final_g, loss_target, m_norm_mix_g, m_norm_mem_g, m_w_in, m_b_gate, m_conv_a_w, m_w_a_out, m_conv_b_w, m_conv_b_bias, m_ln_b_g, m_ln_b_b, m_w_b_out, m_w_kv, m_w_att_out, m_w_o, m_norm_ffn_g, m_w_up, m_conv_ffn_w, m_w_down, m_norm_final_g, v_norm_mix_g, v_norm_mem_g, v_w_in, v_b_gate, v_conv_a_w, v_w_a_out, v_conv_b_w, v_conv_b_bias, v_ln_b_g, v_ln_b_b, v_w_b_out, v_w_kv, v_w_att_out, v_w_o, v_norm_ffn_g, v_w_up, v_conv_ffn_w, v_w_down, v_norm_final_g):
    me = _my_index()
    me_arr = me.astype(jnp.int32).reshape(1)
    x0, mem0, tgt = x.reshape(x.shape[1:]), mem.reshape(mem.shape[1:]), loss_target.reshape(x.shape[1:])
    up_pad = ((0, 0), (0, 0), (0, C_UP_P - C_UP))

    ag_groups = (("win",), ("wup", "wkv", "w4", "wdn", "cv"))
    kinds = ag_groups[0] + ag_groups[1]
    smalls, ag_handles = [], []
    token = jnp.zeros((8, LANE), F32)
    for l in range(DEPTH):
        cv = jnp.zeros((_CV_ROWS, C_UP_P), F32)
        cv = cv.at[0:K_F, 0:C_UP].set(conv_ffn_w[l]).at[3:3 + K_A, 0:R_O].set(conv_a_w[l])
        cv = cv.at[8:8 + K_B, 0:R_O].set(conv_b_w[l])
        shards = dict(
            win=w_in[l], wup=jnp.pad(w_up[l], up_pad[1:]), wkv=w_kv[l],
            w4=jnp.stack([w_a_out[l], w_b_out[l], w_att_out[l], w_o[l]]), wdn=w_down[l], cv=cv)
        whole = dict({k: SDS(*_WHOLE[k]) for k in kinds[:-1]}, cv=SDS((N_DEV,) + cv.shape, F32))
        lands = {k: _place_own(k, shards[k], whole[k], True, me_arr, name=f"ag_own_{k}_l{l}") for k in kinds}
        per_layer = []
        for g, grp in enumerate(ag_groups):
            handle, token = _gather2_start([lands[k] for k in grp], grp, token, name=f"ag_start_l{l}_g{g}")
            per_layer.append(handle)
        ag_handles.append(per_layer)
        smalls.append(dict(
            norm_mix_g=norm_mix_g[l][None], norm_mem_g=norm_mem_g[l][None], b_gate=b_gate[l][None],
            conv_b_bias=conv_b_bias[l][None], ln_b_g=ln_b_g[l][None], ln_b_b=ln_b_b[l][None],
            norm_ffn_g=norm_ffn_g[l][None]))

    def forward_group(l, g, after):
        ag_handles[l][g], tok = _gather2_forward(ag_handles[l][g], ag_groups[g], after, name=f"ag_forward_l{l}_g{g}")
        return tok

    def rest_of_weights(l):
        def wait(after):
            if l == 0:
                after = forward_group(0, 1, after)
            wup, wkv, w4, wdn, cvg = _gather2_wait(ag_handles[l][1], ag_groups[1], after, name=f"ag_wait_l{l}_g1")
            cw_f = jnp.stack([cvg[d, 0:K_F, :] for d in UP_ORDER], axis=1).reshape(K_F, 2 * FF_P)
            cw_a = cvg[:, 3:3 + K_A, 0:R_O].transpose(1, 0, 2).reshape(K_A, D)
            cw_b = cvg[:, 8:8 + K_B, 0:R_O].transpose(1, 0, 2).reshape(K_B, D)
            return (wup, wkv, w4, wdn, jnp.pad(cw_a, ((0, 8 - K_A), (0, 0))),
                    jnp.pad(cw_b, ((0, 32 - K_B), (0, 0))), jnp.pad(cw_f, ((0, 8 - K_F), (0, 0))))
        return wait

    wts, saved = [], []
    xs = x0
    h = _rms_fwd(xs, smalls[0]["norm_mix_g"], name="rms_fwd")
    behind = forward_group(0, 0, token)

    def next_layer_forwarding(l):
        def hook(after):
            if l + 1 == DEPTH:
                return None
            return forward_group(l + 1, 1, forward_group(l + 1, 0, after))
        return hook

    for l in range(DEPTH):
        g_next = smalls[l + 1]["norm_mix_g"] if l + 1 < DEPTH else norm_final_g[None]
        (win,) = _gather2_wait(ag_handles[l][0], ag_groups[0], behind, name=f"ag_wait_l{l}_g0")
        xs, h, w_l, sv = _layer_fwd(xs, h, mem0, win, rest_of_weights(l), next_layer_forwarding(l), smalls[l],
                                    g_next, f"l{l}")
        behind = h
        wts.append(w_l)
        saved.append(sv)
    dx, dxb, head_sums = _loss_head(xs, tgt, norm_final_g[None], name="loss_head")

    rs_handles = []
    small_grads = [None] * DEPTH

    def start_scatter(grp, arrays, name):
        maps = _scatter_maps(grp)
        lands = [_place_own(k, a, SDS((N_DEV,) + _SHARD[k], BF), False, me_arr, name=f"rs_own_{k}_{name}")
                 for k, a in zip(grp, arrays)]
        handle, tok = _exchange_start(list(arrays), lands, maps, rs_handles[-1][2] if rs_handles else head_sums,
                                      name=f"rs_start_{name}")
        rs_handles.append((grp, handle, tok, name))
        return tok

    for l in reversed(range(DEPTH)):
        dx, dxb, small_grads[l], token = _layer_bwd(dx, dxb, mem0, wts[l], smalls[l], saved[l], start_scatter,
                                                    f"l{l}")

    pack = jnp.concatenate(small_grads[0] + small_grads[1] + [head_sums[1:2], head_sums[0:1]], axis=0)
    pack = jnp.pad(pack, ((0, -pack.shape[0] % 8), (0, 0)))
    small_maps = (lambda srcs, lands, a, idx: srcs[a]), (lambda lands, a, idx: lands[a].at[idx])
    small_land = _place_own("cv", pack, SDS((N_DEV,) + pack.shape, F32), True, me_arr, name="small_own")
    small_handle, small_token = _exchange_start([pack], [small_land], small_maps, dx, name="small_start")

    staged = [dict() for _ in range(DEPTH)]
    for grp, handle, _, name in rs_handles[:-1]:
        staged[int(name[-1])].update(zip(grp, _exchange_wait(handle, _scatter_maps(grp), small_token,
                                                             name=f"rs_wait_{name}")))

    def big_update(kind, w, m, v, name):
        return _adamw_staged(staged[0][kind], staged[1][kind], w, m, v, name=name)

    r_up = [jnp.swapaxes(a, 1, 2) for a in big_update(
        "wupT", jnp.swapaxes(w_up, 1, 2), jnp.swapaxes(m_w_up, 1, 2), jnp.swapaxes(v_w_up, 1, 2), "adamw_w_up")]
    r_kv = big_update("wkv", w_kv, m_w_kv, v_w_kv, "adamw_w_kv")
    r_dn = big_update("wdn", w_down, m_w_down, v_w_down, "adamw_w_down")

    def four(a, b, c, d_):
        return jnp.stack([a, b, c, d_], axis=1).reshape(DEPTH, 4 * R_O, D)

    r_4 = _adamw_staged(
        staged[0]["w4"].reshape(N_DEV, 4 * R_O, D), staged[1]["w4"].reshape(N_DEV, 4 * R_O, D),
        four(w_a_out, w_b_out, w_att_out, w_o), four(m_w_a_out, m_w_b_out, m_w_att_out, m_w_o),
        four(v_w_a_out, v_w_b_out, v_w_att_out, v_w_o), name="adamw_w_out")
    grp, handle, _, name = rs_handles[-1]
    staged[0].update(zip(grp, _exchange_wait(handle, _scatter_maps(grp), r_4[0], name=f"rs_wait_{name}")))
    r_in = big_update("win", w_in, m_w_in, v_w_in, "adamw_w_in")
    r_a, r_b, r_att, r_o = ([a.reshape(DEPTH, 4, R_O, D)[:, j] for a in r_4] for j in range(4))

    (gathered,) = _exchange_wait(small_handle, small_maps, r_in[0], name="small_wait")
    total = _sum_slots(gathered, name="small_sum")
    per_layer = sum(_SMALL_ROWS)
    parts = []
    for l in range(DEPTH):
        at, one = l * per_layer, []
        for rows in _SMALL_ROWS:
            one.append(total[at:at + rows])
            at += rows
        parts.append(one)
    g_final = total[DEPTH * per_layer]
    loss = 0.5 / D * jnp.sum(total[DEPTH * per_layer + 1])

    def both(i):
        return jnp.stack([parts[0][i], parts[1][i]])

    g_norm_mix, g_norm_mem = both(0)[:, 0], both(1)[:, 0]
    g_b_gate = both(2).reshape(DEPTH, 3 * D)
    g_cbias, g_lng, g_lnb, g_norm_ffn = both(3)[:, 0], both(4)[:, 0], both(5)[:, 0], both(6)[:, 0]
    g_conv_a = lax.dynamic_slice_in_dim(both(7), me * R_O, R_O, axis=2)
    g_conv_b = lax.dynamic_slice_in_dim(both(8), me * R_O, R_O, axis=2)
    g_conv_f = lax.dynamic_slice_in_dim(both(9).reshape(DEPTH, K_F, 2 * FF_P), _up_slot(me) * C_UP_P, C_UP, axis=2)

    small_g = [g_norm_mix, g_norm_mem, g_b_gate, g_conv_a, g_conv_b, g_cbias, g_lng, g_lnb, g_norm_ffn, g_conv_f,
               g_final]
    small_w = [norm_mix_g, norm_mem_g, b_gate, conv_a_w, conv_b_w, conv_b_bias, ln_b_g, ln_b_b, norm_ffn_g,
               conv_ffn_w, norm_final_g]
    small_m = [m_norm_mix_g, m_norm_mem_g, m_b_gate, m_conv_a_w, m_conv_b_w, m_conv_b_bias, m_ln_b_g, m_ln_b_b,
               m_norm_ffn_g, m_conv_ffn_w, m_norm_final_g]
    small_v = [v_norm_mix_g, v_norm_mem_g, v_b_gate, v_conv_a_w, v_conv_b_w, v_conv_b_bias, v_ln_b_g, v_ln_b_b,
               v_norm_ffn_g, v_conv_ffn_w, v_norm_final_g]
    upd = _adamw_small(_pack_rows(small_g), _pack_rows(small_w), _pack_rows(small_m), _pack_rows(small_v),
                       name="adamw_small")
    s_d, s_m, s_v = (_unpack_rows(p, small_w) for p in upd)
    (d_norm_mix, d_norm_mem, d_b_gate, d_conv_a, d_conv_b, d_cbias, d_lng, d_lnb, d_norm_ffn, d_conv_f,
     d_final) = s_d
    (nm_norm_mix, nm_norm_mem, nm_b_gate, nm_conv_a, nm_conv_b, nm_cbias, nm_lng, nm_lnb, nm_norm_ffn, nm_conv_f,
     nm_final) = s_m
    (nv_norm_mix, nv_norm_mem, nv_b_gate, nv_conv_a, nv_conv_b, nv_cbias, nv_lng, nv_lnb, nv_norm_ffn, nv_conv_f,
     nv_final) = s_v

    grads = [g_norm_mix, g_norm_mem, r_in[0], g_b_gate, g_conv_a, r_a[0], g_conv_b, g_cbias, g_lng, g_lnb, r_b[0],
             r_kv[0], r_att[0], r_o[0], g_norm_ffn, r_up[0], g_conv_f, r_dn[0], g_final]
    deltas = [d_norm_mix, d_norm_mem, r_in[1], d_b_gate, d_conv_a, r_a[1], d_conv_b, d_cbias, d_lng, d_lnb, r_b[1],
              r_kv[1], r_att[1], r_o[1], d_norm_ffn, r_up[1], d_conv_f, r_dn[1], d_final]
    new_m = [nm_norm_mix, nm_norm_mem, r_in[2], nm_b_gate, nm_conv_a, r_a[2], nm_conv_b, nm_cbias, nm_lng, nm_lnb,
             r_b[2], r_kv[2], r_att[2], r_o[2], nm_norm_ffn, r_up[2], nm_conv_f, r_dn[2], nm_final]
    new_v = [nv_norm_mix, nv_norm_mem, r_in[3], nv_b_gate, nv_conv_a, r_a[3], nv_conv_b, nv_cbias, nv_lng, nv_lnb,
             r_b[3], r_kv[3], r_att[3], r_o[3], nv_norm_ffn, r_up[3], nv_conv_f, r_dn[3], nv_final]
    return (loss, dx[None], *grads, *deltas, *new_m, *new_v)
```

```python
import functools

import jax
import jax.numpy as jnp
import numpy as np
from jax import lax
from jax.experimental import pallas as pl
from jax.experimental.pallas import tpu as pltpu

F32 = jnp.float32
BF = jnp.bfloat16
SDS = jax.ShapeDtypeStruct
MESH = pl.DeviceIdType.MESH
ANY = pl.BlockSpec(memory_space=pl.ANY)

N_DEV = 8
DEPTH = 2
D = 1024
N_HEADS = 4
HEAD = D // N_HEADS
D_FF = 2816
K_A, K_B, K_F = 3, 31, 3
NORM_EPS = 1e-6

C_IN = 9 * D // N_DEV
C_KV = 2 * D // N_DEV
C_UP = 2 * D_FF // N_DEV
LANE = 128
C_UP_P = -(-C_UP // LANE) * LANE
FF_P = 4 * C_UP_P
R_O = D // N_DEV
R_DN = D_FF // N_DEV

VMEM_LIMIT = 56 * 1024 * 1024
TM = 512
TR = 256
SUB = 128
H_S, H_L = 16, 32

ADAM_LR, ADAM_B1, ADAM_B2, ADAM_EPS, ADAM_WD, ADAM_STEP = 0.001, 0.9, 0.999, 1e-08, 0.01, 10

UP_ORDER = (0, 4, 1, 5, 2, 6, 3, 7)


def _pcall(body, **kw):
    return pl.pallas_call(body, **kw)


def _cp(sem=None, **kw):
    return pltpu.CompilerParams(dimension_semantics=sem, vmem_limit_bytes=VMEM_LIMIT, **kw)


def _dot(a, b):
    return jnp.dot(a, b, preferred_element_type=F32)


def _dot_nt(a, b):
    return lax.dot_general(a, b, (((1,), (1,)), ((), ())), preferred_element_type=F32)


def _dot_tn(a, b):
    return lax.dot_general(a, b, (((0,), (0,)), ((), ())), preferred_element_type=F32)


def _sigmoid(z):
    return 1.0 / (1.0 + jnp.exp(-z))


def _rms(xv):
    return lax.rsqrt(jnp.mean(xv * xv, axis=-1, keepdims=True) + NORM_EPS)


def _up_slot(idx):
    return jnp.where(idx < 4, 2 * idx, 2 * (idx - 4) + 1)


def _dn_row(idx):
    return C_UP_P * (idx // 2) + R_DN * (idx % 2)


def _mm(a, b, *, ta=False, tb=False, out_dtype=BF, tm=TM, tn=512, tk=None, name):
    m, k_dim = (a.shape[1], a.shape[0]) if ta else a.shape
    n = b.shape[0] if tb else b.shape[1]
    tm, tn = min(tm, m), min(tn, n)
    tk = k_dim if tk is None else min(tk, k_dim)
    nk = k_dim // tk
    assert m % tm == 0 and n % tn == 0 and k_dim % tk == 0
    dims = (((0 if ta else 1,), (1 if tb else 0,)), ((), ()))

    def body(a_ref, b_ref, o_ref, *scratch):
        part = lax.dot_general(a_ref[...], b_ref[...], dims, preferred_element_type=F32)
        if nk == 1:
            o_ref[...] = part.astype(o_ref.dtype)
            return
        acc = scratch[0]
        k = pl.program_id(2)

        @pl.when(k == 0)
        def _():
            acc[...] = part

        @pl.when(k > 0)
        def _():
            acc[...] += part

        @pl.when(k == nk - 1)
        def _():
            o_ref[...] = acc[...].astype(o_ref.dtype)

    a_spec = pl.BlockSpec((tk, tm), lambda i, j, k: (k, i)) if ta else pl.BlockSpec((tm, tk), lambda i, j, k: (i, k))
    b_spec = pl.BlockSpec((tn, tk), lambda i, j, k: (j, k)) if tb else pl.BlockSpec((tk, tn), lambda i, j, k: (k, j))
    return _pcall(
        body, grid=(m // tm, n // tn, nk), in_specs=[a_spec, b_spec],
        out_specs=pl.BlockSpec((tm, tn), lambda i, j, k: (i, j)),
        out_shape=SDS((m, n), out_dtype),
        scratch_shapes=[pltpu.VMEM((tm, tn), F32)] if nk > 1 else [],
        compiler_params=_cp(("parallel", "parallel", "arbitrary")), name=name)(a, b)


def _mm_res_norm(a, w, x, g, *, name):
    s, k_dim = a.shape
    tm = min(TM, s)

    def body(a_ref, w_ref, x_ref, g_ref, xo_ref, h_ref):
        xo = x_ref[...] + _dot(a_ref[...], w_ref[...])
        xo_ref[...] = xo
        h_ref[...] = ((xo * _rms(xo)) * g_ref[...]).astype(BF)

    return _pcall(
        body, grid=(s // tm,),
        in_specs=[pl.BlockSpec((tm, k_dim), lambda i: (i, 0)),
                  pl.BlockSpec((k_dim, D), lambda i: (0, 0), pipeline_mode=pl.Buffered(1)),
                  pl.BlockSpec((tm, D), lambda i: (i, 0)), pl.BlockSpec((1, D), lambda i: (0, 0))],
        out_specs=[pl.BlockSpec((tm, D), lambda i: (i, 0))] * 2,
        out_shape=[SDS((s, D), F32), SDS((s, D), BF)],
        compiler_params=_cp(("parallel",)), name=name)(a, w, x, g)


def _mm_nt_normbwd(da, w, x, dres, g, *, tk=None, name):
    s, k_dim = da.shape
    tm = min(TM, s)
    tk = k_dim if tk is None else tk
    nk = k_dim // tk
    assert k_dim % tk == 0

    def body(da_ref, w_ref, x_ref, dres_ref, g_ref, dx_ref, dxb_ref, dg_ref, *scratch):
        i, k = pl.program_id(0), pl.program_id(1)
        part = _dot_nt(da_ref[...], w_ref[...])
        if nk > 1:
            acc = scratch[0]

            @pl.when(k == 0)
            def _():
                acc[...] = part

            @pl.when(k > 0)
            def _():
                acc[...] += part

        @pl.when((i == 0) & (k == 0))
        def _():
            dg_ref[...] = jnp.zeros_like(dg_ref)

        @pl.when(k == nk - 1)
        def _():
            dh = acc[...] if nk > 1 else part
            xv = x_ref[...]
            r = _rms(xv)
            xn = xv * r
            dg_ref[0:1, :] += jnp.sum(dh * xn, axis=0, keepdims=True)
            dxn = dh * g_ref[...]
            dx = dres_ref[...] + r * (dxn - xn * jnp.mean(dxn * xn, axis=-1, keepdims=True))
            dx_ref[...] = dx
            dxb_ref[...] = dx.astype(BF)

    row = lambda i, k: (i, 0)
    w_spec = (pl.BlockSpec((D, tk), lambda i, k: (0, k)) if nk > 1 else
              pl.BlockSpec((D, tk), lambda i, k: (0, 0), pipeline_mode=pl.Buffered(1)))
    return _pcall(
        body, grid=(s // tm, nk),
        in_specs=[pl.BlockSpec((tm, tk), lambda i, k: (i, k)), w_spec,
                  pl.BlockSpec((tm, D), row), pl.BlockSpec((tm, D), row), pl.BlockSpec((1, D), lambda i, k: (0, 0))],
        out_specs=[pl.BlockSpec((tm, D), row), pl.BlockSpec((tm, D), row), pl.BlockSpec((8, D), lambda i, k: (0, 0))],
        out_shape=[SDS((s, D), F32), SDS((s, D), BF), SDS((8, D), F32)],
        scratch_shapes=[pltpu.VMEM((tm, D), F32)] if nk > 1 else [],
        compiler_params=_cp(("arbitrary", "arbitrary")), name=name)(da, w, x, dres, g)


def _rms_fwd(x, g, *, name):
    s = x.shape[0]
    tm = min(TM, s)

    def body(x_ref, g_ref, h_ref):
        xv = x_ref[...]
        h_ref[...] = ((xv * _rms(xv)) * g_ref[...]).astype(BF)

    return _pcall(
        body, grid=(s // tm,),
        in_specs=[pl.BlockSpec((tm, D), lambda i: (i, 0)), pl.BlockSpec((1, D), lambda i: (0, 0))],
        out_specs=pl.BlockSpec((tm, D), lambda i: (i, 0)), out_shape=SDS((s, D), BF),
        compiler_params=_cp(("parallel",)), name=name)(x, g)


def _loss_head(x, tgt, g, *, name):
    s = x.shape[0]
    tm = min(TM, s)

    def body(x_ref, t_ref, g_ref, dx_ref, dxb_ref, sums_ref):
        @pl.when(pl.program_id(0) == 0)
        def _():
            sums_ref[...] = jnp.zeros_like(sums_ref)

        xv = x_ref[...]
        r = _rms(xv)
        xn = xv * r
        diff = xn * g_ref[...] - t_ref[...]
        sums_ref[0:1, :] += jnp.sum(diff * diff, axis=0, keepdims=True)
        dy = diff * (1.0 / D)
        sums_ref[1:2, :] += jnp.sum(dy * xn, axis=0, keepdims=True)
        dxn = dy * g_ref[...]
        dx = r * (dxn - xn * jnp.mean(dxn * xn, axis=-1, keepdims=True))
        dx_ref[...] = dx
        dxb_ref[...] = dx.astype(BF)

    row = lambda i: (i, 0)
    return _pcall(
        body, grid=(s // tm,),
        in_specs=[pl.BlockSpec((tm, D), row), pl.BlockSpec((tm, D), row), pl.BlockSpec((1, D), lambda i: (0, 0))],
        out_specs=[pl.BlockSpec((tm, D), row), pl.BlockSpec((tm, D), row), pl.BlockSpec((8, D), lambda i: (0, 0))],
        out_shape=[SDS((s, D), F32), SDS((s, D), BF), SDS((8, D), F32)],
        compiler_params=_cp(("arbitrary",)), name=name)(x, tgt, g)


def _halo_before(i, tr, h):
    return jnp.maximum(i * (tr // h) - 1, 0)


def _halo_after(i, tr, h, s):
    return jnp.minimum((i + 1) * (tr // h), s // h - 1)


def _taps(buf, w_ref, sl, k_w, base, rows):
    acc = None
    for k in range(k_w):
        t = w_ref[k:k + 1, sl] * buf[base + k:base + k + rows, sl]
        acc = t if acc is None else acc + t
    return acc


def _taps_rev(buf, w_ref, sl, k_w, base, rows):
    acc = None
    for k in range(k_w):
        t = w_ref[k:k + 1, sl] * buf[base + k_w - 1 - k:base + k_w - 1 - k + rows, sl]
        acc = t if acc is None else acc + t
    return acc


def _tap_grads(dw_ref, dc, buf, sl, k_w, base, rows):
    for k in range(k_w):
        dw_ref[k:k + 1, sl] += jnp.sum(dc * buf[base + k:base + k + rows, sl], axis=0, keepdims=True)


def _bra_fwd(proj, cw, *, name):
    s = proj.shape[0]
    tr, h = min(TR, s), H_S
    sub = min(SUB, tr)

    def body(cur, halo, w_ref, za_ref, cvb):
        i = pl.program_id(0)
        hv = halo[:, D:2 * D].astype(F32) * halo[:, 2 * D:3 * D].astype(F32)
        cvb[0:h, :] = jnp.where(i == 0, 0.0, hv)
        cvb[h:h + tr, :] = cur[:, D:2 * D].astype(F32) * cur[:, 2 * D:3 * D].astype(F32)
        for c in range(D // LANE):
            sl = slice(LANE * c, LANE * c + LANE)
            ca = _taps(cvb, w_ref, sl, K_A, h - (K_A - 1), tr)
            za_ref[:, sl] = (cur[:, sl].astype(F32) * ca).astype(BF)

    return _pcall(
        body, grid=(s // tr,),
        in_specs=[pl.BlockSpec((tr, 3 * D), lambda i: (i, 0)),
                  pl.BlockSpec((h, 3 * D), lambda i: (_halo_before(i, tr, h), 0)),
                  pl.BlockSpec((8, D), lambda i: (0, 0))],
        out_specs=pl.BlockSpec((tr, D), lambda i: (i, 0)), out_shape=SDS((s, D), BF),
        scratch_shapes=[pltpu.VMEM((h + tr, D), F32)],
        compiler_params=_cp(("parallel",)), name=name)(proj, proj, cw)


def _bra_bwd(proj, dza, cw, dproj, *, name):
    s = proj.shape[0]
    tr, h = min(TR, s), H_S
    sub = min(SUB, tr)
    n = s // tr

    def body(before, cur, after, dz_cur, dz_after, w_ref, dproj_in, da_ref, dw_ref, cvb, dcab):
        del dproj_in
        i = pl.program_id(0)

        @pl.when(i == 0)
        def _():
            dw_ref[...] = jnp.zeros_like(dw_ref)

        first, last = i == 0, i == n - 1
        cvb[0:h, :] = jnp.where(first, 0.0, before[:, D:2 * D].astype(F32) * before[:, 2 * D:3 * D].astype(F32))
        cvb[h:h + tr, :] = cur[:, D:2 * D].astype(F32) * cur[:, 2 * D:3 * D].astype(F32)
        dcab[0:tr, :] = dz_cur[...].astype(F32) * cur[:, 0:D].astype(F32)
        dcab[tr:tr + h, :] = jnp.where(last, 0.0, dz_after[...].astype(F32) * after[:, 0:D].astype(F32))
        for c in range(D // LANE):
            sl = slice(LANE * c, LANE * c + LANE)
            gl, vl = slice(D + LANE * c, D + LANE * c + LANE), slice(2 * D + LANE * c, 2 * D + LANE * c + LANE)
            for r0 in range(0, tr, sub):
                rows = slice(r0, r0 + sub)
                ca = _taps(cvb, w_ref, sl, K_A, h - (K_A - 1) + r0, sub)
                da_ref[rows, sl] = (dz_cur[rows, sl].astype(F32) * ca).astype(BF)
                dcv = _taps_rev(dcab, w_ref, sl, K_A, r0, sub)
                da_ref[rows, gl] = (dcv * cur[rows, vl].astype(F32)).astype(BF)
                da_ref[rows, vl] = (dcv * cur[rows, gl].astype(F32)).astype(BF)
                _tap_grads(dw_ref, dcab[rows, sl], cvb, sl, K_A, h - (K_A - 1) + r0, sub)

    return _pcall(
        body, grid=(n,),
        in_specs=[pl.BlockSpec((h, 3 * D), lambda i: (_halo_before(i, tr, h), 0)),
                  pl.BlockSpec((tr, 3 * D), lambda i: (i, 0)),
                  pl.BlockSpec((h, 3 * D), lambda i: (_halo_after(i, tr, h, s), 0)),
                  pl.BlockSpec((tr, D), lambda i: (i, 0)),
                  pl.BlockSpec((h, D), lambda i: (_halo_after(i, tr, h, s), 0)),
                  pl.BlockSpec((8, D), lambda i: (0, 0)), ANY],
        out_specs=[pl.BlockSpec((tr, 3 * D), lambda i: (i, 0)), pl.BlockSpec((8, D), lambda i: (0, 0))],
        out_shape=[SDS(dproj.shape, BF), SDS((8, D), F32)], input_output_aliases={6: 0},
        scratch_shapes=[pltpu.VMEM((h + tr, D), F32), pltpu.VMEM((tr + h, D), F32)],
        compiler_params=_cp(("arbitrary",)), name=name)(proj, proj, proj, dza, dza, cw, dproj)


_U_COL, _UG_COL = 3, 4


def _brb_conv_fwd(proj, cw, bias, *, name):
    s = proj.shape[0]
    tr, h = min(TR, s), H_L
    sub = min(SUB, tr)

    def body(u_cur, ug_cur, u_halo, ug_halo, w_ref, b_ref, cb_ref, glb, shifted):
        i = pl.program_id(0)
        glb[0:h, :] = jnp.where(i == 0, 0.0, u_halo[...].astype(F32) * _sigmoid(ug_halo[...].astype(F32)))
        glb[h:h + tr, :] = u_cur[...].astype(F32) * _sigmoid(ug_cur[...].astype(F32))
        for c in range(D // LANE):
            sl = slice(LANE * c, LANE * c + LANE)
            for r in range(1, 8):
                shifted[r] = glb[8 - r:8 - r + tr + 24, sl]
            for r0 in range(0, tr, sub):
                acc = None
                for k in range(K_B):
                    q, r = divmod(K_B - 1 - k, 8)
                    at = r0 - 8 * q
                    win = shifted[r, 24 + at:24 + at + sub, :] if r else glb[h + at:h + at + sub, sl]
                    term = w_ref[k:k + 1, sl] * win
                    acc = term if acc is None else acc + term
                cb_ref[r0:r0 + sub, sl] = (acc + b_ref[:, sl]).astype(BF)

    return _pcall(
        body, grid=(s // tr,),
        in_specs=[pl.BlockSpec((tr, D), lambda i: (i, _U_COL)), pl.BlockSpec((tr, D), lambda i: (i, _UG_COL)),
                  pl.BlockSpec((h, D), lambda i: (_halo_before(i, tr, h), _U_COL)),
                  pl.BlockSpec((h, D), lambda i: (_halo_before(i, tr, h), _UG_COL)),
                  pl.BlockSpec((32, D), lambda i: (0, 0)), pl.BlockSpec((1, D), lambda i: (0, 0))],
        out_specs=pl.BlockSpec((tr, D), lambda i: (i, 0)), out_shape=SDS((s, D), BF),
        scratch_shapes=[pltpu.VMEM((h + tr, D), F32), pltpu.VMEM((8, tr + 24, LANE), F32)],
        compiler_params=_cp(("parallel",)), name=name)(proj, proj, proj, proj, cw, bias)


def _brb_conv_bwd(proj, dcb, dq, cw, dproj, *, name):
    s = proj.shape[0]
    tr, h = min(TR, s), H_L
    sub = min(SUB, tr)
    n = s // tr
    nb = -(-(tr + 24) // sub)
    sel = _row_selector(sub, list(range(8)))

    def body(u_cur, ug_cur, d_cur, d_after, dq_ref, sel_ref, w_ref, dproj_in, db_ref, dw_ref, dcbb, shifted):
        del dproj_in
        i = pl.program_id(0)

        @pl.when(i == 0)
        def _():
            dw_ref[...] = jnp.zeros_like(dw_ref)

        db_ref[:, 2 * D:3 * D] = dq_ref[...]
        after = d_after[...]
        dcbb[0:tr, :] = d_cur[...]
        dcbb[tr:tr + h, :] = jnp.where(i == n - 1, jnp.zeros_like(after), after)
        dcbb[tr + h:(nb + 1) * sub, :] = jnp.zeros(((nb + 1) * sub - h - tr, D), BF)
        for c in range(D // LANE):
            sl = slice(LANE * c, LANE * c + LANE)
            for blk in range(nb):
                res = _dot(sel_ref[...], dcbb[blk * sub:(blk + 2) * sub, sl])
                for r in range(8):
                    shifted[r, blk * sub:(blk + 1) * sub, :] = res[r * sub:(r + 1) * sub]
            for r0 in range(0, tr, sub):
                u = u_cur[r0:r0 + sub, sl].astype(F32)
                sg = _sigmoid(ug_cur[r0:r0 + sub, sl].astype(F32))
                glu = u * sg
                dglu = None
                for k in range(K_B):
                    q, r = divmod(K_B - 1 - k, 8)
                    at = r0 + 8 * q
                    win = shifted[r, at:at + sub, :]
                    term = w_ref[k:k + 1, sl] * win
                    dglu = term if dglu is None else dglu + term
                    dw_ref[k:k + 1, sl] += jnp.sum(win * glu, axis=0, keepdims=True)
                db_ref[r0:r0 + sub, sl] = (dglu * sg).astype(BF)
                db_ref[r0:r0 + sub, D + LANE * c:D + LANE * c + LANE] = (dglu * u * sg * (1.0 - sg)).astype(BF)

    return _pcall(
        body, grid=(n,),
        in_specs=[pl.BlockSpec((tr, D), lambda i: (i, _U_COL)), pl.BlockSpec((tr, D), lambda i: (i, _UG_COL)),
                  pl.BlockSpec((tr, D), lambda i: (i, 0)),
                  pl.BlockSpec((h, D), lambda i: (_halo_after(i, tr, h, s), 0)),
                  pl.BlockSpec((tr, D), lambda i: (i, 0)),
                  pl.BlockSpec(sel.shape, lambda i: (0, 0)),
                  pl.BlockSpec((32, D), lambda i: (0, 0)), ANY],
        out_specs=[pl.BlockSpec((tr, 3 * D), lambda i: (i, 1)), pl.BlockSpec((32, D), lambda i: (0, 0))],
        out_shape=[SDS(dproj.shape, BF), SDS((32, D), F32)], input_output_aliases={7: 0},
        scratch_shapes=[pltpu.VMEM(((nb + 1) * sub, D), BF), pltpu.VMEM((8, nb * sub, LANE), F32)],
        compiler_params=_cp(("arbitrary",)), name=name)(proj, proj, dcb, dcb, dq, sel, cw, dproj)


def _ln_silu_fwd(cb, g, b, *, name):
    s = cb.shape[0]
    tm = min(TM, s)

    def body(cb_ref, g_ref, b_ref, sb_ref):
        z = cb_ref[...].astype(F32)
        zc = z - jnp.mean(z, axis=-1, keepdims=True)
        ln = (zc * lax.rsqrt(jnp.mean(zc * zc, axis=-1, keepdims=True) + NORM_EPS)) * g_ref[...] + b_ref[...]
        sb_ref[...] = (ln * _sigmoid(ln)).astype(BF)

    row = lambda i: (i, 0)
    vec = pl.BlockSpec((1, D), lambda i: (0, 0))
    return _pcall(
        body, grid=(s // tm,), in_specs=[pl.BlockSpec((tm, D), row), vec, vec],
        out_specs=pl.BlockSpec((tm, D), row), out_shape=SDS((s, D), BF),
        compiler_params=_cp(("parallel",)), name=name)(cb, g, b)


def _ln_silu_bwd(cb, dsb, g, b, *, name):
    s = cb.shape[0]
    tm = min(TM, s)

    def body(cb_ref, dsb_ref, g_ref, b_ref, dcb_ref, sums_ref):
        @pl.when(pl.program_id(0) == 0)
        def _():
            sums_ref[...] = jnp.zeros_like(sums_ref)

        z = cb_ref[...].astype(F32)
        zc = z - jnp.mean(z, axis=-1, keepdims=True)
        rstd = lax.rsqrt(jnp.mean(zc * zc, axis=-1, keepdims=True) + NORM_EPS)
        lnh = zc * rstd
        ln = lnh * g_ref[...] + b_ref[...]
        sg = _sigmoid(ln)
        dln = dsb_ref[...].astype(F32) * (sg * (1.0 + ln * (1.0 - sg)))
        sums_ref[0:1, :] += jnp.sum(dln * lnh, axis=0, keepdims=True)
        sums_ref[1:2, :] += jnp.sum(dln, axis=0, keepdims=True)
        dlnh = dln * g_ref[...]
        dz = rstd * (dlnh - jnp.mean(dlnh, axis=-1, keepdims=True)
                     - lnh * jnp.mean(dlnh * lnh, axis=-1, keepdims=True))
        sums_ref[2:3, :] += jnp.sum(dz, axis=0, keepdims=True)
        dcb_ref[...] = dz.astype(BF)

    row = lambda i: (i, 0)
    vec = pl.BlockSpec((1, D), lambda i: (0, 0))
    return _pcall(
        body, grid=(s // tm,), in_specs=[pl.BlockSpec((tm, D), row), pl.BlockSpec((tm, D), row), vec, vec],
        out_specs=[pl.BlockSpec((tm, D), row), pl.BlockSpec((8, D), lambda i: (0, 0))],
        out_shape=[SDS((s, D), BF), SDS((8, D), F32)],
        compiler_params=_cp(("arbitrary",)), name=name)(cb, dsb, g, b)


_Q_COL = 5 * D // HEAD


def _kv_prep(mem, g, wkv, *, name):
    m = mem.shape[0]

    def body(mem_ref, g_ref, w_ref, memn_ref, kv_ref):
        mv = mem_ref[...]
        memn = ((mv * _rms(mv)) * g_ref[...]).astype(BF)
        memn_ref[...] = memn
        for dev in range(N_DEV):
            kv_ref[:, dev * C_KV:(dev + 1) * C_KV] = _dot(memn, w_ref[dev]).astype(BF)

    return _pcall(body, out_shape=[SDS((m, D), BF), SDS((m, 2 * D), BF)],
                  compiler_params=_cp(), name=name)(mem, g, wkv)


def _softmax_rows(q, k):
    sc = _dot_nt(q, k) * (1.0 / (HEAD ** 0.5))
    e = jnp.exp(sc - jnp.max(sc, axis=-1, keepdims=True))
    return e / jnp.sum(e, axis=-1, keepdims=True)


def _attn_fwd(proj, kv, *, name):
    s, m = proj.shape[0], kv.shape[0]
    tm = min(TM, s)

    def body(q_ref, kv_ref, o_ref):
        for hd in range(N_HEADS):
            cols = slice(hd * HEAD, (hd + 1) * HEAD)
            p = _softmax_rows(q_ref[:, cols], kv_ref[:, cols])
            o_ref[:, cols] = _dot(p.astype(BF), kv_ref[:, D + hd * HEAD:D + (hd + 1) * HEAD]).astype(BF)

    return _pcall(
        body, grid=(s // tm,),
        in_specs=[pl.BlockSpec((tm, D), lambda i: (i, _Q_COL // N_HEADS)),
                  pl.BlockSpec((m, 2 * D), lambda i: (0, 0))],
        out_specs=pl.BlockSpec((tm, D), lambda i: (i, 0)), out_shape=SDS((s, D), BF),
        compiler_params=_cp(("parallel",)), name=name)(proj, kv)


def _attn_bwd(proj, kv, do, *, name):
    s, m = proj.shape[0], kv.shape[0]
    tm = min(TM, s)

    def body(q_ref, kv_ref, do_ref, dq_ref, dk_ref, dv_ref):
        @pl.when(pl.program_id(0) == 0)
        def _():
            dk_ref[...] = jnp.zeros_like(dk_ref)
            dv_ref[...] = jnp.zeros_like(dv_ref)

        for hd in range(N_HEADS):
            cols = slice(hd * HEAD, (hd + 1) * HEAD)
            q, k, dov = q_ref[:, cols], kv_ref[:, cols], do_ref[:, cols]
            p = _softmax_rows(q, k)
            dp = _dot_nt(dov, kv_ref[:, D + hd * HEAD:D + (hd + 1) * HEAD])
            dv_ref[:, cols] += _dot_tn(p.astype(BF), dov)
            ds = (p * (dp - jnp.sum(dp * p, axis=-1, keepdims=True)) * (1.0 / (HEAD ** 0.5))).astype(BF)
            dq_ref[:, cols] = _dot(ds, k).astype(BF)
            dk_ref[:, cols] += _dot_tn(ds, q)

    return _pcall(
        body, grid=(s // tm,),
        in_specs=[pl.BlockSpec((tm, D), lambda i: (i, _Q_COL // N_HEADS)),
                  pl.BlockSpec((m, 2 * D), lambda i: (0, 0)),
                  pl.BlockSpec((tm, D), lambda i: (i, 0))],
        out_specs=[pl.BlockSpec((tm, D), lambda i: (i, 0)),
                   pl.BlockSpec((m, D), lambda i: (0, 0)),
                   pl.BlockSpec((m, D), lambda i: (0, 0))],
        out_shape=[SDS((s, D), BF), SDS((m, D), F32), SDS((m, D), F32)],
        compiler_params=_cp(("arbitrary",)), name=name)(proj, kv, do)


def _kv_bwd(mem, g, memn, dk, dv, wkv, *, name):
    def body(mem_ref, g_ref, memn_ref, dk_ref, dv_ref, w_ref, dw_ref, dg_ref):
        memn = memn_ref[...]
        dmemn = None
        for dev in range(N_DEV):
            d_ref, col = (dk_ref, dev) if dev < N_HEADS else (dv_ref, dev - N_HEADS)
            dslab = d_ref[:, col * C_KV:(col + 1) * C_KV].astype(BF)
            dw_ref[dev] = _dot_tn(memn, dslab).astype(BF)
            part = _dot_nt(dslab, w_ref[dev])
            dmemn = part if dmemn is None else dmemn + part
        mv = mem_ref[...]
        dg_ref[...] = jnp.zeros_like(dg_ref)
        dg_ref[0:1, :] = jnp.sum(dmemn * (mv * _rms(mv)), axis=0, keepdims=True)

    assert C_KV == HEAD
    return _pcall(body, out_shape=[SDS((N_DEV, D, C_KV), BF), SDS((8, D), F32)],
                  compiler_params=_cp(), name=name)(mem, g, memn, dk, dv, wkv)


_TM_MIX = 512


def _mix_out(x, za, sb, o, proj, w4, bg, g_next, *, name):
    s = x.shape[0]
    tm = min(_TM_MIX, s)

    def body(x_ref, za_ref, sb_ref, o_ref, pg_ref, w4_ref, bg_ref, gn_ref,
             ya_ref, yb_ref, yc_ref, mg_ref, x1_ref, h_ref):
        ys = (_dot(za_ref[...], w4_ref[0]), _dot(sb_ref[...], w4_ref[1]), _dot(o_ref[...], w4_ref[2]))
        merged = None
        for j, (y, y_ref) in enumerate(zip(ys, (ya_ref, yb_ref, yc_ref))):
            y_ref[...] = y.astype(BF)
            gate = _sigmoid(pg_ref[:, j * D:(j + 1) * D].astype(F32) + bg_ref[:, j * D:(j + 1) * D])
            merged = gate * y if merged is None else merged + gate * y
        mg = merged.astype(BF)
        mg_ref[...] = mg
        x1 = x_ref[...] + _dot(mg, w4_ref[3])
        x1_ref[...] = x1
        h_ref[...] = ((x1 * _rms(x1)) * gn_ref[...]).astype(BF)

    row = lambda i: (i, 0)
    act = pl.BlockSpec((tm, D), row)
    return _pcall(
        body, grid=(s // tm,),
        in_specs=[act, act, act, act, pl.BlockSpec((tm, 3 * D), lambda i: (i, 2)),
                  pl.BlockSpec((4, D, D), lambda i: (0, 0, 0), pipeline_mode=pl.Buffered(1)), pl.BlockSpec((1, 3 * D), lambda i: (0, 0)),
                  pl.BlockSpec((1, D), lambda i: (0, 0))],
        out_specs=[act] * 6,
        out_shape=[SDS((s, D), BF)] * 4 + [SDS((s, D), F32), SDS((s, D), BF)],
        compiler_params=_cp(("parallel",)), name=name)(x, za, sb, o, proj, w4, bg, g_next)


def _mix_bwd(dxb, ya, yb, yc, proj, w4, bg, *, name):
    s = dxb.shape[0]
    tm = min(_TM_MIX, s)

    def body(dx_ref, ya_ref, yb_ref, yc_ref, pg_ref, w4_ref, bg_ref,
             dya_ref, dyb_ref, dyc_ref, dza_ref, dsb_ref, do_ref, dgt_ref, dbg_ref):
        @pl.when(pl.program_id(0) == 0)
        def _():
            dbg_ref[...] = jnp.zeros_like(dbg_ref)

        dm = _dot_nt(dx_ref[...], w4_ref[3])
        for j, (y_ref, dy_ref, din_ref) in enumerate(zip((ya_ref, yb_ref, yc_ref), (dya_ref, dyb_ref, dyc_ref),
                                                         (dza_ref, dsb_ref, do_ref))):
            cols = slice(j * D, (j + 1) * D)
            gate = _sigmoid(pg_ref[:, cols].astype(F32) + bg_ref[:, cols])
            dy = (dm * gate).astype(BF)
            dy_ref[...] = dy
            din_ref[...] = _dot_nt(dy, w4_ref[j]).astype(BF)
            dpre = dm * y_ref[...].astype(F32) * gate * (1.0 - gate)
            dgt_ref[:, cols] = dpre.astype(BF)
            dbg_ref[0:1, cols] += jnp.sum(dpre, axis=0, keepdims=True)

    row = lambda i: (i, 0)
    act = pl.BlockSpec((tm, D), row)
    return _pcall(
        body, grid=(s // tm,),
        in_specs=[act, act, act, act, pl.BlockSpec((tm, 3 * D), lambda i: (i, 2)),
                  pl.BlockSpec((4, D, D), lambda i: (0, 0, 0), pipeline_mode=pl.Buffered(1)), pl.BlockSpec((1, 3 * D), lambda i: (0, 0))],
        out_specs=[act] * 6 + [pl.BlockSpec((tm, 3 * D), lambda i: (i, 2)),
                               pl.BlockSpec((8, 3 * D), lambda i: (0, 0))],
        out_shape=[SDS((s, D), BF)] * 6 + [SDS((s, 9 * D), BF), SDS((8, 3 * D), F32)],
        compiler_params=_cp(("arbitrary",)), name=name)(dxb, ya, yb, yc, proj, w4, bg)


_PAIR = 2 * C_UP_P


def _row_selector(sub, first_cols):
    rows = np.arange(len(first_cols) * sub)
    col = np.asarray(first_cols)[rows // sub] + rows % sub
    return jnp.asarray(np.arange(2 * sub)[None, :] == col[:, None], BF)


def _ffn_act(u2, cw, *, name):
    s = u2.shape[0]
    tr = min(TR, s)
    sub = min(SUB, tr)
    sel = _row_selector(sub, [sub - (K_F - 1 - k) for k in range(K_F)])

    def body(cur, prev, sel_ref, w_ref, act_ref, c2_ref, xb, win):
        i = pl.program_id(1)
        before = prev[...]
        xb[0:sub, :] = jnp.where(i == 0, jnp.zeros_like(before), before)
        xb[sub:sub + tr, :] = cur[...]
        for r0 in range(0, tr, sub):
            win[...] = _dot(sel_ref[...], xb[r0:r0 + 2 * sub, :])
            for c in range(C_UP_P // LANE):
                gl = slice(LANE * c, LANE * c + LANE)
                ul = slice(C_UP_P + LANE * c, C_UP_P + LANE * c + LANE)
                gt = sum(w_ref[k:k + 1, gl] * win[k * sub:(k + 1) * sub, gl] for k in range(K_F))
                up = sum(w_ref[k:k + 1, ul] * win[k * sub:(k + 1) * sub, ul] for k in range(K_F))
                c2_ref[r0:r0 + sub, gl] = gt.astype(BF)
                c2_ref[r0:r0 + sub, ul] = up.astype(BF)
                act_ref[r0:r0 + sub, gl] = (gt * _sigmoid(gt) * up).astype(BF)

    return _pcall(
        body, grid=(4, s // tr),
        in_specs=[pl.BlockSpec((tr, _PAIR), lambda p, i: (i, p)),
                  pl.BlockSpec((sub, _PAIR), lambda p, i: (_halo_before(i, tr, sub), p)),
                  pl.BlockSpec(sel.shape, lambda p, i: (0, 0)),
                  pl.BlockSpec((8, _PAIR), lambda p, i: (0, p))],
        out_specs=[pl.BlockSpec((tr, C_UP_P), lambda p, i: (i, p)), pl.BlockSpec((tr, _PAIR), lambda p, i: (i, p))],
        out_shape=[SDS((s, FF_P), BF), SDS((s, 2 * FF_P), BF)],
        scratch_shapes=[pltpu.VMEM((sub + tr, _PAIR), BF), pltpu.VMEM((K_F * sub, _PAIR), F32)],
        compiler_params=_cp(("parallel", "parallel")), name=name)(u2, u2, sel, cw)


def _ffn_bwd(u2, c2, dact, cw, *, name):
    s = u2.shape[0]
    tr, h = min(TR, s), H_S
    sub = min(SUB, tr)
    n = s // tr
    sel = _row_selector(sub, [K_F - 1 - k for k in range(K_F)])

    def body(u_cur, c_cur, c_after, da_cur, da_after, sel_ref, w_ref, du_ref, dw_ref, dcb, win):
        i = pl.program_id(1)
        last = i == n - 1

        @pl.when(i == 0)
        def _():
            dw_ref[...] = jnp.zeros_like(dw_ref)

        def conv_grad(gt, up, da):
            gt, up, da = gt.astype(F32), up.astype(F32), da.astype(F32)
            sg = _sigmoid(gt)
            return (da * up * (sg * (1.0 + gt * (1.0 - sg)))).astype(BF), (da * (gt * sg)).astype(BF)

        for c in range(C_UP_P // LANE):
            gl = slice(LANE * c, LANE * c + LANE)
            ul = slice(C_UP_P + LANE * c, C_UP_P + LANE * c + LANE)
            for r0 in range(0, tr, sub):
                rows = slice(r0, r0 + sub)
                dcb[rows, gl], dcb[rows, ul] = conv_grad(c_cur[rows, gl], c_cur[rows, ul], da_cur[rows, gl])
            dg, du_ = conv_grad(c_after[:, gl], c_after[:, ul], da_after[:, gl])
            dcb[tr:tr + h, gl] = jnp.where(last, jnp.zeros_like(dg), dg)
            dcb[tr:tr + h, ul] = jnp.where(last, jnp.zeros_like(du_), du_)
        dcb[tr + h:tr + sub, :] = jnp.zeros((sub - h, _PAIR), BF)
        for r0 in range(0, tr, sub):
            win[...] = _dot(sel_ref[...], dcb[r0:r0 + 2 * sub, :])
            for c in range(_PAIR // LANE):
                sl = slice(LANE * c, LANE * c + LANE)
                u = u_cur[r0:r0 + sub, sl].astype(F32)
                du = None
                for k in range(K_F):
                    wk = win[k * sub:(k + 1) * sub, sl]
                    term = w_ref[k:k + 1, sl] * wk
                    du = term if du is None else du + term
                    dw_ref[k:k + 1, sl] += jnp.sum(wk * u, axis=0, keepdims=True)
                du_ref[r0:r0 + sub, sl] = du.astype(BF)

    return _pcall(
        body, grid=(4, n),
        in_specs=[pl.BlockSpec((tr, _PAIR), lambda p, i: (i, p)),
                  pl.BlockSpec((tr, _PAIR), lambda p, i: (i, p)),
                  pl.BlockSpec((h, _PAIR), lambda p, i: (_halo_after(i, tr, h, s), p)),
                  pl.BlockSpec((tr, C_UP_P), lambda p, i: (i, p)),
                  pl.BlockSpec((h, C_UP_P), lambda p, i: (_halo_after(i, tr, h, s), p)),
                  pl.BlockSpec(sel.shape, lambda p, i: (0, 0)),
                  pl.BlockSpec((8, _PAIR), lambda p, i: (0, p))],
        out_specs=[pl.BlockSpec((tr, _PAIR), lambda p, i: (i, p)), pl.BlockSpec((8, _PAIR), lambda p, i: (0, p))],
        out_shape=[SDS((s, 2 * FF_P), BF), SDS((8, 2 * FF_P), F32)],
        scratch_shapes=[pltpu.VMEM((tr + sub, _PAIR), BF), pltpu.VMEM((K_F * sub, _PAIR), F32)],
        compiler_params=_cp(("parallel", "arbitrary")), name=name)(u2, c2, c2, dact, dact, sel, cw)


def _relations():
    x, y, c = lax.axis_index("x"), lax.axis_index("y"), lax.axis_index("c")
    out = []
    for r in range(1, N_DEV):
        rx, ry, rc = (r >> 2) & 1, (r >> 1) & 1, r & 1
        out.append((r, (x ^ rx, y ^ ry, c ^ rc)))
    return out


def _my_index():
    return 4 * lax.axis_index("x") + 2 * lax.axis_index("y") + lax.axis_index("c")


def _slab(kind, ref, idx):
    if kind == "win":
        return ref.at[:, pl.ds(pl.multiple_of(idx * C_IN, LANE), C_IN)]
    if kind == "wup":
        return ref.at[:, pl.ds(pl.multiple_of(_up_slot(idx) * C_UP_P, LANE), C_UP_P)]
    if kind == "wupT":
        return ref.at[pl.ds(pl.multiple_of(_up_slot(idx) * C_UP_P, LANE), C_UP_P), :]
    if kind == "wkv":
        return ref.at[idx]
    if kind == "w4":
        return ref.at[:, pl.ds(pl.multiple_of(idx * R_O, 16), R_O), :]
    if kind == "wdn":
        return ref.at[pl.ds(pl.multiple_of(_dn_row(idx), 16), R_DN), :]
    assert kind == "cv"
    return ref.at[idx]


_WHOLE = {"win": ((D, 9 * D), BF), "wup": ((D, 2 * FF_P), BF), "wkv": ((N_DEV, D, C_KV), BF),
          "w4": ((4, D, D), BF), "wdn": ((FF_P, D), BF)}
_SHARD = {"win": (D, C_IN), "wup": (D, C_UP_P), "wupT": (C_UP_P, D), "wkv": (D, C_KV), "w4": (4, R_O, D),
          "wdn": (R_DN, D)}
HBM_SPEC = pl.BlockSpec(memory_space=pltpu.HBM)
SEM_SPEC = pl.BlockSpec(memory_space=pltpu.SEMAPHORE)
_DATAFLOW = pltpu.SideEffectType.DATAFLOW_SIDE_EFFECTING


def _scatter_maps(kinds):
    return ((lambda srcs, lands, a, idx: _slab(kinds[a], srcs[a], idx)),
            (lambda lands, a, idx: lands[a].at[idx]))


_SLOTTED = ("wkv", "cv")


def _own_slab_blocks(kind, shard_shape):
    if kind in ("win", "wup"):
        rows, slot = 256, (_up_slot if kind == "wup" else (lambda m: m))
        return (shard_shape[0] // rows, (rows, shard_shape[1]), (lambda i, me: (i, slot(me[0]))),
                (lambda i, me: (i, 0)), (lambda i, me: (me[0], i, 0)))
    if kind == "wupT":
        rows = 256
        steps = shard_shape[0] // rows
        return (steps, (rows, D), (lambda i, me: (_up_slot(me[0]) * steps + i, 0)), (lambda i, me: (i, 0)),
                (lambda i, me: (me[0], i, 0)))
    if kind == "w4":
        return (1, shard_shape, (lambda i, me: (0, me[0], 0)), (lambda i, me: (0, 0, 0)),
                (lambda i, me: (me[0], 0, 0, 0)))
    if kind == "wdn":
        rows = 32
        return (R_DN // rows, (rows, D), (lambda i, me: (_dn_row(me[0]) // rows + i, 0)), (lambda i, me: (i, 0)),
                (lambda i, me: (me[0], i, 0)))
    assert kind in _SLOTTED
    rows = min(256, shard_shape[0])
    return (shard_shape[0] // rows, (rows, shard_shape[1]), (lambda i, me: (me[0], i, 0)),
            (lambda i, me: (i, 0)), (lambda i, me: (me[0], i, 0)))


def _place_own(kind, src, out_sds, gather, me_arr, *, name):
    shard_shape = src.shape if gather else out_sds.shape[1:]
    steps, blk, whole_idx, shard_idx, staging_idx = _own_slab_blocks(kind, shard_shape)
    slotted = kind in _SLOTTED
    whole_spec = pl.BlockSpec(((None,) if slotted else ()) + tuple(blk), whole_idx)
    if gather:
        in_spec, out_spec = pl.BlockSpec(tuple(blk), shard_idx), whole_spec
    else:
        in_spec, out_spec = whole_spec, pl.BlockSpec((None,) + tuple(blk), staging_idx)
    zero_init = gather and kind == "wdn"

    def body(me_ref, src_ref, *rest):
        rest[-1][...] = src_ref[...].astype(rest[-1].dtype)

    operands = (me_arr, src) + ((jnp.zeros(out_sds.shape, out_sds.dtype),) if zero_init else ())
    return _pcall(
        body,
        grid_spec=pltpu.PrefetchScalarGridSpec(
            num_scalar_prefetch=1, grid=(steps,), in_specs=[in_spec] + ([ANY] if zero_init else []),
            out_specs=out_spec),
        out_shape=out_sds, input_output_aliases={2: 0} if zero_init else {},
        compiler_params=_cp(("arbitrary",)), name=name)(*operands)


def _peer_copies(n, src_of, dst_of, src_r, land_r, ssem, rsem):
    me = _my_index()
    out = []
    for r, peer in _relations():
        p_idx = 4 * peer[0] + 2 * peer[1] + peer[2]
        for a in range(n):
            def copy(src_idx, dst_idx, a=a, r=r, peer=peer):
                sem = a * (N_DEV - 1) + r - 1
                return pltpu.make_async_remote_copy(
                    src_ref=src_of(src_r, land_r, a, src_idx), dst_ref=dst_of(land_r, a, dst_idx),
                    send_sem=ssem.at[sem], recv_sem=rsem.at[sem], device_id=peer, device_id_type=MESH)
            out.append((functools.partial(copy, p_idx, me), functools.partial(copy, me, p_idx)))
    return out


def _exchange_start(srcs, lands, maps, after, *, name):
    n, ns = len(lands), len(srcs)
    src_of, dst_of = maps

    def body(*refs):
        src_r, land_r = refs[:ns], refs[ns:ns + n]
        ssem, rsem, token = refs[ns + n + 1], refs[ns + n + 2], refs[-1]
        for send, _ in _peer_copies(n, src_of, dst_of, src_r, land_r, ssem, rsem):
            send().start()
        token[...] = jnp.zeros_like(token)

    flight = list(srcs) + list(lands)
    outs = pl.pallas_call(
        body, name=name,
        out_shape=(pltpu.SemaphoreType.DMA((n * (N_DEV - 1),)), pltpu.SemaphoreType.DMA((n * (N_DEV - 1),)),
                   *[pltpu.HBM(a.shape, a.dtype) for a in flight], SDS((8, LANE), F32)),
        in_specs=[HBM_SPEC] * (ns + n) + [ANY],
        out_specs=(SEM_SPEC, SEM_SPEC, *[HBM_SPEC] * (ns + n), pl.BlockSpec(memory_space=pltpu.VMEM)),
        input_output_aliases={i: 2 + i for i in range(ns + n)},
        compiler_params=pltpu.CompilerParams(has_side_effects=_DATAFLOW),
    )(*[pltpu.with_memory_space_constraint(a, pltpu.HBM) for a in flight], after)
    return (outs[0], outs[1], list(outs[2:2 + ns + n]), ns), outs[-1]


def _exchange_wait(handle, maps, after, *, name):
    ssem, rsem, flight, ns = handle
    n = len(flight) - ns
    src_of, dst_of = maps

    def body(*refs):
        src_r, land_r, ssem_r, rsem_r = refs[:ns], refs[ns:ns + n], refs[ns + n], refs[ns + n + 1]
        for send, arrival in _peer_copies(n, src_of, dst_of, src_r, land_r, ssem_r, rsem_r):
            send().wait_send()
            arrival().wait_recv()

    outs = pl.pallas_call(
        body, name=name, out_shape=[pltpu.HBM(a.shape, a.dtype) for a in flight],
        in_specs=[HBM_SPEC] * (ns + n) + [SEM_SPEC, SEM_SPEC, ANY], out_specs=[HBM_SPEC] * (ns + n),
        input_output_aliases={i: i for i in range(ns + n)},
        compiler_params=pltpu.CompilerParams(has_side_effects=_DATAFLOW),
    )(*flight, ssem, rsem, after)
    return list(outs[ns:])


_SIBLING = 1
_ICI = (2, 4, 6)


def _rel_peer(r):
    x, y, c = lax.axis_index("x"), lax.axis_index("y"), lax.axis_index("c")
    peer = (x ^ ((r >> 2) & 1), y ^ ((r >> 1) & 1), c ^ (r & 1))
    return peer, 4 * peer[0] + 2 * peer[1] + peer[2]


def _rcopy(ref, ssem, rsem, peer):
    return pltpu.make_async_remote_copy(src_ref=ref, dst_ref=ref, send_sem=ssem, recv_sem=rsem, device_id=peer,
                                        device_id_type=MESH)


def _gather2_start(lands, kinds, after, *, name):
    n = len(lands)

    def body(*refs):
        land_r, (send1, recv_sib, recv_ici), token = refs[:n], refs[n + 1:n + 4], refs[-1]
        me = _my_index()
        for a in range(n):
            own = _slab(kinds[a], land_r[a], me)
            for j, r in enumerate((_SIBLING,) + _ICI):
                rsem = recv_sib.at[a] if r == _SIBLING else recv_ici.at[3 * a + j - 1]
                _rcopy(own, send1.at[4 * a + j], rsem, _rel_peer(r)[0]).start()
        token[...] = jnp.zeros_like(token)

    sems = [pltpu.SemaphoreType.DMA((4 * n,)), pltpu.SemaphoreType.DMA((n,)), pltpu.SemaphoreType.DMA((3 * n,))]
    outs = pl.pallas_call(
        body, name=name, out_shape=(*sems, *[pltpu.HBM(a.shape, a.dtype) for a in lands], SDS((8, LANE), F32)),
        in_specs=[HBM_SPEC] * n + [ANY],
        out_specs=(SEM_SPEC,) * 3 + (HBM_SPEC,) * n + (pl.BlockSpec(memory_space=pltpu.VMEM),),
        input_output_aliases={i: 3 + i for i in range(n)},
        compiler_params=pltpu.CompilerParams(has_side_effects=_DATAFLOW),
    )(*[pltpu.with_memory_space_constraint(a, pltpu.HBM) for a in lands], after)
    return dict(send1=outs[0], recv_sib=outs[1], recv_ici=outs[2], lands=list(outs[3:3 + n])), outs[-1]


def _gather2_forward(handle, kinds, after, *, name):
    lands = handle["lands"]
    n = len(lands)

    def body(*refs):
        land_r, recv_ici, (fwd_send, fwd_recv), token = refs[:n], refs[n], refs[n + 2:n + 4], refs[-1]
        sibling = _rel_peer(_SIBLING)[0]
        for a in range(n):
            for j, r in enumerate(_ICI):
                got = _slab(kinds[a], land_r[a], _rel_peer(r)[1])
                _rcopy(got, fwd_send.at[3 * a + j], recv_ici.at[3 * a + j], sibling).wait_recv()
                _rcopy(got, fwd_send.at[3 * a + j], fwd_recv.at[3 * a + j], sibling).start()
        token[...] = jnp.zeros_like(token)

    sems = [pltpu.SemaphoreType.DMA((3 * n,)), pltpu.SemaphoreType.DMA((3 * n,))]
    outs = pl.pallas_call(
        body, name=name, out_shape=(*sems, *[pltpu.HBM(a.shape, a.dtype) for a in lands], SDS((8, LANE), F32)),
        in_specs=[HBM_SPEC] * n + [SEM_SPEC, ANY],
        out_specs=(SEM_SPEC,) * 2 + (HBM_SPEC,) * n + (pl.BlockSpec(memory_space=pltpu.VMEM),),
        input_output_aliases={i: 2 + i for i in range(n)},
        compiler_params=pltpu.CompilerParams(has_side_effects=_DATAFLOW),
    )(*lands, handle["recv_ici"], after)
    return dict(handle, fwd_send=outs[0], fwd_recv=outs[1], lands=list(outs[2:2 + n])), outs[-1]


def _gather2_wait(handle, kinds, after, *, name):
    lands = handle["lands"]
    n = len(lands)

    def body(*refs):
        land_r, (send1, recv_sib, fwd_send, fwd_recv) = refs[:n], refs[n:n + 4]
        me = _my_index()
        sibling, sib_idx = _rel_peer(_SIBLING)
        for a in range(n):
            own = _slab(kinds[a], land_r[a], me)
            for j, r in enumerate((_SIBLING,) + _ICI):
                _rcopy(own, send1.at[4 * a + j], recv_sib.at[a], _rel_peer(r)[0]).wait_send()
            theirs = _slab(kinds[a], land_r[a], sib_idx)
            _rcopy(theirs, send1.at[4 * a], recv_sib.at[a], sibling).wait_recv()
            for j, r in enumerate(_ICI):
                passed_on = _slab(kinds[a], land_r[a], _rel_peer(r)[1])
                _rcopy(passed_on, fwd_send.at[3 * a + j], fwd_recv.at[3 * a + j], sibling).wait_send()
                arrived = _slab(kinds[a], land_r[a], _rel_peer(r ^ _SIBLING)[1])
                _rcopy(arrived, fwd_send.at[3 * a + j], fwd_recv.at[3 * a + j], sibling).wait_recv()

    outs = pl.pallas_call(
        body, name=name, out_shape=[pltpu.HBM(a.shape, a.dtype) for a in lands],
        in_specs=[HBM_SPEC] * n + [SEM_SPEC] * 4 + [ANY], out_specs=[HBM_SPEC] * n,
        input_output_aliases={i: i for i in range(n)},
        compiler_params=pltpu.CompilerParams(has_side_effects=_DATAFLOW),
    )(*lands, handle["send1"], handle["recv_sib"], handle["fwd_send"], handle["fwd_recv"], after)
    return list(outs)


def _sum_slots(gathered, *, name):
    def body(g_ref, out_ref):
        total = g_ref[0]
        for dev in range(1, N_DEV):
            total = total + g_ref[dev]
        out_ref[...] = total

    return _pcall(body, out_shape=SDS(gathered.shape[1:], F32), compiler_params=_cp(), name=name)(gathered)


def _adam(g, w, m, v):
    nm = ADAM_B1 * m + (1.0 - ADAM_B1) * g
    nv = ADAM_B2 * v + (1.0 - ADAM_B2) * (g * g)
    m_hat = nm / (1.0 - ADAM_B1 ** ADAM_STEP)
    v_hat = nv / (1.0 - ADAM_B2 ** ADAM_STEP)
    return -ADAM_LR * (m_hat / (jnp.sqrt(v_hat) + ADAM_EPS) + ADAM_WD * w), nm, nv


def _adamw_staged(st0, st1, w, m, v, *, name):
    _, rows, cols = w.shape
    st_cols = st0.shape[2]
    tr = max(t for t in range(16, 129, 16) if rows % t == 0)
    nr = rows // tr

    def body(s0_ref, s1_ref, w_ref, m_ref, v_ref, g_ref, d_ref, nm_ref, nv_ref):
        for layer, s_ref in enumerate((s0_ref, s1_ref)):
            @pl.when(pl.program_id(0) == layer)
            def _(s_ref=s_ref):
                total = s_ref[0, :, 0:cols].astype(F32)
                for dev in range(1, N_DEV):
                    total = total + s_ref[dev, :, 0:cols].astype(F32)
                g_ref[0] = total

        d_ref[0], nm_ref[0], nv_ref[0] = _adam(g_ref[0], w_ref[0], m_ref[0], v_ref[0])

    st_spec = lambda layer: pl.BlockSpec(
        (N_DEV, tr, st_cols), lambda l, i: (0, jnp.where(l == layer, i, (nr - 1) * (1 - layer)), 0))
    par = pl.BlockSpec((1, tr, cols), lambda l, i: (l, i, 0))
    return _pcall(
        body, grid=(DEPTH, nr), in_specs=[st_spec(0), st_spec(1), par, par, par], out_specs=[par] * 4,
        out_shape=[SDS(w.shape, F32)] * 4,
        compiler_params=_cp(("arbitrary", "arbitrary")), name=name)(st0, st1, w, m, v)


def _adamw_small(g, w, m, v, *, name):
    def body(g_ref, w_ref, m_ref, v_ref, d_ref, nm_ref, nv_ref):
        d_ref[...], nm_ref[...], nv_ref[...] = _adam(g_ref[...], w_ref[...], m_ref[...], v_ref[...])

    return _pcall(body, out_shape=[SDS(g.shape, F32)] * 3, compiler_params=_cp(), name=name)(g, w, m, v)


def _pack_rows(arrays):
    flat = jnp.concatenate([a.reshape(-1).astype(F32) for a in arrays])
    rows = -(-flat.shape[0] // (8 * D)) * 8
    return jnp.pad(flat, (0, rows * D - flat.shape[0])).reshape(rows, D)


def _unpack_rows(pack, like):
    flat = pack.reshape(-1)
    out, at = [], 0
    for a in like:
        out.append(flat[at:at + a.size].reshape(a.shape))
        at += a.size
    return out


def _layer_fwd(x, h, mem, win, mixer_weights, ffn_weights, after_up, small, g_next, tag):
    proj = _mm(h, win, tm=1024, tn=1536, name=f"proj_{tag}")
    wkv, w4, cw_a, cw_b, cw_f = mixer_weights(proj)
    za = _bra_fwd(proj, cw_a, name=f"bra_fwd_{tag}")
    cb = _brb_conv_fwd(proj, cw_b, small["conv_b_bias"], name=f"brb_conv_fwd_{tag}")
    sb = _ln_silu_fwd(cb, small["ln_b_g"], small["ln_b_b"], name=f"ln_silu_fwd_{tag}")
    memn, kv = _kv_prep(mem, small["norm_mem_g"], wkv, name=f"kv_prep_{tag}")
    o = _attn_fwd(proj, kv, name=f"attn_fwd_{tag}")
    ya, yb, yc, mg, x1, h2 = _mix_out(x, za, sb, o, proj, w4, small["b_gate"], small["norm_ffn_g"],
                                      name=f"mix_out_{tag}")
    wup, wdn = ffn_weights(h2)
    u2 = _mm(h2, wup, tm=1024, tn=1536, name=f"up_{tag}")
    token = after_up(u2)
    act, c2 = _ffn_act(u2, cw_f if token is None else _behind(cw_f, token), name=f"ffn_act_{tag}")
    x2, h_next = _mm_res_norm(act, wdn, x1, g_next, name=f"down_{tag}")
    saved = dict(x=x, h=h, proj=proj, za=za, cb=cb, sb=sb, memn=memn, kv=kv, o=o, ya=ya, yb=yb, yc=yc,
                 mg=mg, x1=x1, h2=h2, u2=u2, c2=c2, act=act)
    return x2, h_next, (win, wup, wkv, w4, wdn, cw_a, cw_b, cw_f), saved


def _behind(operand, token):
    return operand + token[0:1, 0:1]


def _layer_bwd(dx2, dx2b, mem, wts, small, sv, start, tag):
    win, wup, wkv, w4, wdn, cw_a, cw_b, cw_f = wts
    dact = _mm(dx2b, wdn, tb=True, tm=1024, tn=768, name=f"d_act_{tag}")
    dwdn = _mm(sv["act"], dx2b, ta=True, tm=768, tn=1024, name=f"dw_down_{tag}")
    du2, dcw_f = _ffn_bwd(sv["u2"], sv["c2"], dact, cw_f, name=f"ffn_bwd_{tag}")
    dwup_t = _mm(du2, sv["h2"], ta=True, tm=C_UP_P, tn=1024, name=f"dw_up_{tag}")
    token = start(("wdn", "wupT"), (dwdn, dwup_t), f"ffn_{tag}")
    dx1, dx1b, dg_ffn = _mm_nt_normbwd(du2, wup, sv["x1"], dx2, _behind(small["norm_ffn_g"], token),
                                       name=f"d_h2_{tag}")

    dya, dyb, dyc, dza, dsb, do, dproj, dbg = _mix_bwd(dx1b, sv["ya"], sv["yb"], sv["yc"], sv["proj"], w4,
                                                      small["b_gate"], name=f"mix_bwd_{tag}")
    dw4 = jnp.stack([
        _mm(a, b, ta=True, tm=1024, tn=256, name=f"dw_{nm}_{tag}")
        for nm, a, b in (("a_out", sv["za"], dya), ("b_out", sv["sb"], dyb), ("att_out", sv["o"], dyc),
                         ("o", sv["mg"], dx1b))])
    dq, dk, dv = _attn_bwd(sv["proj"], sv["kv"], do, name=f"attn_bwd_{tag}")
    dwkv, dg_mem = _kv_bwd(mem, small["norm_mem_g"], sv["memn"], dk, dv, wkv, name=f"kv_bwd_{tag}")
    token = start(("w4", "wkv"), (dw4, dwkv), f"mix_{tag}")
    dproj, dcw_a = _bra_bwd(sv["proj"], dza, _behind(cw_a, token), dproj, name=f"bra_bwd_{tag}")
    dcb, ln_sums = _ln_silu_bwd(sv["cb"], dsb, small["ln_b_g"], small["ln_b_b"], name=f"ln_silu_bwd_{tag}")
    dproj, dcw_b = _brb_conv_bwd(sv["proj"], dcb, dq, cw_b, dproj, name=f"brb_conv_bwd_{tag}")
    dwin = _mm(sv["h"], dproj, ta=True, tm=1024, tn=768, name=f"dw_in_{tag}")
    token = start(("win",), (dwin,), f"in_{tag}")
    dx, dxb, dg_mix = _mm_nt_normbwd(dproj, win, sv["x"], dx1, _behind(small["norm_mix_g"], token),
                                     tk=4608, name=f"d_h_{tag}")

    small_grads = [dg_mix[0:1], dg_mem[0:1], dbg[0:1].reshape(3, D), ln_sums[2:3], ln_sums[0:1], ln_sums[1:2],
                   dg_ffn[0:1], dcw_a[0:K_A], dcw_b[0:K_B], dcw_f[0:K_F].reshape(K_F * 2 * FF_P // D, D)]
    return dx, dxb, small_grads, token


_SMALL_ROWS = (1, 1, 3, 1, 1, 1, 1, K_A, K_B, K_F * 2 * FF_P // D)
_CV_ROWS = 48


def kernel(x, mem, norm_mix_g, norm_mem_g, w_in, b_gate, conv_a_w, w_a_out, conv_b_w, conv_b_bias, ln_b_g, ln_b_b, w_b_out, w_kv, w_att_out, w_o, norm_ffn_g, w_up, conv_ffn_w, w_down, norm_final_g, loss_target, m_norm_mix_g, m_norm_mem_g, m_w_in, m_b_gate, m_conv_a_w, m_w_a_out, m_conv_b_w, m_conv_b_bias, m_ln_b_g, m_ln_b_b, m_w_b_out, m_w_kv, m_w_att_out, m_w_o, m_norm_ffn_g, m_w_up, m_conv_ffn_w, m_w_down, m_norm_final_g, v_norm_mix_g, v_norm_mem_g, v_w_in, v_b_gate, v_conv_a_w, v_w_a_out, v_conv_b_w, v_conv_b_bias, v_ln_b_g, v_ln_b_b, v_w_b_out, v_w_kv, v_w_att_out, v_w_o, v_norm_ffn_g, v_w_up, v_conv_ffn_w, v_w_down, v_norm_final_g):
    me = _my_index()
    me_arr = me.astype(jnp.int32).reshape(1)
    x0, mem0, tgt = x.reshape(x.shape[1:]), mem.reshape(mem.shape[1:]), loss_target.reshape(x.shape[1:])
    up_pad = ((0, 0), (0, 0), (0, C_UP_P - C_UP))

    ag_groups = (("win",), ("wkv", "w4", "cv"), ("wup", "wdn"))
    kinds = ag_groups[0] + ag_groups[1] + ag_groups[2]
    smalls, ag_handles = [], []
    token = jnp.zeros((8, LANE), F32)
    for l in range(DEPTH):
        cv = jnp.zeros((_CV_ROWS, C_UP_P), F32)
        cv = cv.at[0:K_F, 0:C_UP].set(conv_ffn_w[l]).at[3:3 + K_A, 0:R_O].set(conv_a_w[l])
        cv = cv.at[8:8 + K_B, 0:R_O].set(conv_b_w[l])
        shards = dict(
            win=w_in[l], wup=jnp.pad(w_up[l], up_pad[1:]), wkv=w_kv[l],
            w4=jnp.stack([w_a_out[l], w_b_out[l], w_att_out[l], w_o[l]]), wdn=w_down[l], cv=cv)
        whole = dict({k: SDS(*_WHOLE[k]) for k in kinds if k != "cv"}, cv=SDS((N_DEV,) + cv.shape, F32))
        lands = {k: _place_own(k, shards[k], whole[k], True, me_arr, name=f"ag_own_{k}_l{l}") for k in kinds}
        per_layer = []
        for g, grp in enumerate(ag_groups):
            handle, token = _gather2_start([lands[k] for k in grp], grp, token, name=f"ag_start_l{l}_g{g}")
            per_layer.append(handle)
        ag_handles.append(per_layer)
        smalls.append(dict(
            norm_mix_g=norm_mix_g[l][None], norm_mem_g=norm_mem_g[l][None], b_gate=b_gate[l][None],
            conv_b_bias=conv_b_bias[l][None], ln_b_g=ln_b_g[l][None], ln_b_b=ln_b_b[l][None],
            norm_ffn_g=norm_ffn_g[l][None]))

    def forward_group(l, g, after):
        ag_handles[l][g], tok = _gather2_forward(ag_handles[l][g], ag_groups[g], after, name=f"ag_forward_l{l}_g{g}")
        return tok

    def group_of(l, g):
        def wait(after):
            if l == 0:
                after = forward_group(0, g, after)
            return _gather2_wait(ag_handles[l][g], ag_groups[g], after, name=f"ag_wait_l{l}_g{g}")
        return wait

    def mixer_weights(l):
        def wait(after):
            wkv, w4, cvg = group_of(l, 1)(after)
            cw_f = jnp.stack([cvg[d, 0:K_F, :] for d in UP_ORDER], axis=1).reshape(K_F, 2 * FF_P)
            cw_a = cvg[:, 3:3 + K_A, 0:R_O].transpose(1, 0, 2).reshape(K_A, D)
            cw_b = cvg[:, 8:8 + K_B, 0:R_O].transpose(1, 0, 2).reshape(K_B, D)
            return (wkv, w4, jnp.pad(cw_a, ((0, 8 - K_A), (0, 0))), jnp.pad(cw_b, ((0, 32 - K_B), (0, 0))),
                    jnp.pad(cw_f, ((0, 8 - K_F), (0, 0))))
        return wait

    wts, saved = [], []
    xs = x0
    h = _rms_fwd(xs, smalls[0]["norm_mix_g"], name="rms_fwd")
    behind = forward_group(0, 0, token)

    def next_layer_forwarding(l):
        def hook(after):
            if l + 1 == DEPTH:
                return None
            for g in range(len(ag_groups)):
                after = forward_group(l + 1, g, after)
            return after
        return hook

    for l in range(DEPTH):
        g_next = smalls[l + 1]["norm_mix_g"] if l + 1 < DEPTH else norm_final_g[None]
        (win,) = _gather2_wait(ag_handles[l][0], ag_groups[0], behind, name=f"ag_wait_l{l}_g0")
        xs, h, w_l, sv = _layer_fwd(xs, h, mem0, win, mixer_weights(l), group_of(l, 2), next_layer_forwarding(l),
                                    smalls[l], g_next, f"l{l}")
        behind = h
        wts.append(w_l)
        saved.append(sv)
    dx, dxb, head_sums = _loss_head(xs, tgt, norm_final_g[None], name="loss_head")

    rs_handles = []
    small_grads = [None] * DEPTH

    def start_scatter(grp, arrays, name):
        maps = _scatter_maps(grp)
        lands = [_place_own(k, a, SDS((N_DEV,) + _SHARD[k], BF), False, me_arr, name=f"rs_own_{k}_{name}")
                 for k, a in zip(grp, arrays)]
        handle, tok = _exchange_start(list(arrays), lands, maps, rs_handles[-1][2] if rs_handles else head_sums,
                                      name=f"rs_start_{name}")
        rs_handles.append((grp, handle, tok, name))
        return tok

    for l in reversed(range(DEPTH)):
        dx, dxb, small_grads[l], token = _layer_bwd(dx, dxb, mem0, wts[l], smalls[l], saved[l], start_scatter,
                                                    f"l{l}")

    pack = jnp.concatenate(small_grads[0] + small_grads[1] + [head_sums[1:2], head_sums[0:1]], axis=0)
    pack = jnp.pad(pack, ((0, -pack.shape[0] % 8), (0, 0)))
    small_maps = (lambda srcs, lands, a, idx: srcs[a]), (lambda lands, a, idx: lands[a].at[idx])
    small_land = _place_own("cv", pack, SDS((N_DEV,) + pack.shape, F32), True, me_arr, name="small_own")
    small_handle, small_token = _exchange_start([pack], [small_land], small_maps, dx, name="small_start")

    staged = [dict() for _ in range(DEPTH)]
    for grp, handle, _, name in rs_handles[:-1]:
        staged[int(name[-1])].update(zip(grp, _exchange_wait(handle, _scatter_maps(grp), small_token,
                                                             name=f"rs_wait_{name}")))

    def big_update(kind, w, m, v, name):
        return _adamw_staged(staged[0][kind], staged[1][kind], w, m, v, name=name)

    r_up = [jnp.swapaxes(a, 1, 2) for a in big_update(
        "wupT", jnp.swapaxes(w_up, 1, 2), jnp.swapaxes(m_w_up, 1, 2), jnp.swapaxes(v_w_up, 1, 2), "adamw_w_up")]
    r_kv = big_update("wkv", w_kv, m_w_kv, v_w_kv, "adamw_w_kv")
    r_dn = big_update("wdn", w_down, m_w_down, v_w_down, "adamw_w_down")

    def four(a, b, c, d_):
        return jnp.stack([a, b, c, d_], axis=1).reshape(DEPTH, 4 * R_O, D)

    r_4 = _adamw_staged(
        staged[0]["w4"].reshape(N_DEV, 4 * R_O, D), staged[1]["w4"].reshape(N_DEV, 4 * R_O, D),
        four(w_a_out, w_b_out, w_att_out, w_o), four(m_w_a_out, m_w_b_out, m_w_att_out, m_w_o),
        four(v_w_a_out, v_w_b_out, v_w_att_out, v_w_o), name="adamw_w_out")
    grp, handle, _, name = rs_handles[-1]
    staged[0].update(zip(grp, _exchange_wait(handle, _scatter_maps(grp), r_4[0], name=f"rs_wait_{name}")))
    r_in = big_update("win", w_in, m_w_in, v_w_in, "adamw_w_in")
    r_a, r_b, r_att, r_o = ([a.reshape(DEPTH, 4, R_O, D)[:, j] for a in r_4] for j in range(4))

    (gathered,) = _exchange_wait(small_handle, small_maps, r_in[0], name="small_wait")
    total = _sum_slots(gathered, name="small_sum")
    per_layer = sum(_SMALL_ROWS)
    parts = []
    for l in range(DEPTH):
        at, one = l * per_layer, []
        for rows in _SMALL_ROWS:
            one.append(total[at:at + rows])
            at += rows
        parts.append(one)
    g_final = total[DEPTH * per_layer]
    loss = 0.5 / D * jnp.sum(total[DEPTH * per_layer + 1])

    def both(i):
        return jnp.stack([parts[0][i], parts[1][i]])

    g_norm_mix, g_norm_mem = both(0)[:, 0], both(1)[:, 0]
    g_b_gate = both(2).reshape(DEPTH, 3 * D)
    g_cbias, g_lng, g_lnb, g_norm_ffn = both(3)[:, 0], both(4)[:, 0], both(5)[:, 0], both(6)[:, 0]
    g_conv_a = lax.dynamic_slice_in_dim(both(7), me * R_O, R_O, axis=2)
    g_conv_b = lax.dynamic_slice_in_dim(both(8), me * R_O, R_O, axis=2)
    g_conv_f = lax.dynamic_slice_in_dim(both(9).reshape(DEPTH, K_F, 2 * FF_P), _up_slot(me) * C_UP_P, C_UP, axis=2)

    small_g = [g_norm_mix, g_norm_mem, g_b_gate, g_conv_a, g_conv_b, g_cbias, g_lng, g_lnb, g_norm_ffn, g_conv_f,
               g_final]
    small_w = [norm_mix_g, norm_mem_g, b_gate, conv_a_w, conv_b_w, conv_b_bias, ln_b_g, ln_b_b, norm_ffn_g,
               conv_ffn_w, norm_final_g]
    small_m = [m_norm_mix_g, m_norm_mem_g, m_b_gate, m_conv_a_w, m_conv_b_w, m_conv_b_bias, m_ln_b_g, m_ln_b_b,
               m_norm_ffn_g, m_conv_ffn_w, m_norm_final_g]
    small_v = [v_norm_mix_g, v_norm_mem_g, v_b_gate, v_conv_a_w, v_conv_b_w, v_conv_b_bias, v_ln_b_g, v_ln_b_b,
               v_norm_ffn_g, v_conv_ffn_w, v_norm_final_g]
    upd = _adamw_small(_pack_rows(small_g), _pack_rows(small_w), _pack_rows(small_m), _pack_rows(small_v),
                       name="adamw_small")
    s_d, s_m, s_v = (_unpack_rows(p, small_w) for p in upd)
    (d_norm_mix, d_norm_mem, d_b_gate, d_conv_a, d_conv_b, d_cbias, d_lng, d_lnb, d_norm_ffn, d_conv_f,
     d_final) = s_d
    (nm_norm_mix, nm_norm_mem, nm_b_gate, nm_conv_a, nm_conv_b, nm_cbias, nm_lng, nm_lnb, nm_norm_ffn, nm_conv_f,
     nm_final) = s_m
    (nv_norm_mix, nv_norm_mem, nv_b_gate, nv_conv_a, nv_conv_b, nv_cbias, nv_lng, nv_lnb, nv_norm_ffn, nv_conv_f,
     nv_final) = s_v

    grads = [g_norm_mix, g_norm_mem, r_in[0], g_b_gate, g_conv_a, r_a[0], g_conv_b, g_cbias, g_lng, g_lnb, r_b[0],
             r_kv[0], r_att[0], r_o[0], g_norm_ffn, r_up[0], g_conv_f, r_dn[0], g_final]
    deltas = [d_norm_mix, d_norm_mem, r_in[1], d_b_gate, d_conv_a, r_a[1], d_conv_b, d_cbias, d_lng, d_lnb, r_b[1],
              r_kv[1], r_att[1], r_o[1], d_norm_ffn, r_up[1], d_conv_f, r_dn[1], d_final]
    new_m = [nm_norm_mix, nm_norm_mem, r_in[2], nm_b_gate, nm_conv_a, r_a[2], nm_conv_b, nm_cbias, nm_lng, nm_lnb,
             r_b[2], r_kv[2], r_att[2], r_o[2], nm_norm_ffn, r_up[2], nm_conv_f, r_dn[2], nm_final]
    new_v = [nv_norm_mix, nv_norm_mem, r_in[3], nv_b_gate, nv_conv_a, r_a[3], nv_conv_b, nv_cbias, nv_lng, nv_lnb,
             r_b[3], r_kv[3], r_att[3], r_o[3], nv_norm_ffn, r_up[3], nv_conv_f, r_dn[3], nv_final]
    return (loss, dx[None], *grads, *deltas, *new_m, *new_v)
```

```python
import functools

import jax
import jax.numpy as jnp
import numpy as np
from jax import lax
from jax.experimental import pallas as pl
from jax.experimental.pallas import tpu as pltpu

F32 = jnp.float32
BF = jnp.bfloat16
SDS = jax.ShapeDtypeStruct
MESH = pl.DeviceIdType.MESH
ANY = pl.BlockSpec(memory_space=pl.ANY)

N_DEV = 8
DEPTH = 2
D = 1024
N_HEADS = 4
HEAD = D // N_HEADS
D_FF = 2816
K_A, K_B, K_F = 3, 31, 3
NORM_EPS = 1e-6

C_IN = 9 * D // N_DEV
C_KV = 2 * D // N_DEV
C_UP = 2 * D_FF // N_DEV
LANE = 128
C_UP_P = -(-C_UP // LANE) * LANE
FF_P = 4 * C_UP_P
R_O = D // N_DEV
R_DN = D_FF // N_DEV

VMEM_LIMIT = 56 * 1024 * 1024
TM = 512
TR = 256
SUB = 128
H_S, H_L = 16, 32

ADAM_LR, ADAM_B1, ADAM_B2, ADAM_EPS, ADAM_WD, ADAM_STEP = 0.001, 0.9, 0.999, 1e-08, 0.01, 10

UP_ORDER = (0, 4, 1, 5, 2, 6, 3, 7)


def _pcall(body, **kw):
    return pl.pallas_call(body, **kw)


def _cp(sem=None, **kw):
    return pltpu.CompilerParams(dimension_semantics=sem, vmem_limit_bytes=VMEM_LIMIT, **kw)


def _dot(a, b):
    return jnp.dot(a, b, preferred_element_type=F32)


def _dot_nt(a, b):
    return lax.dot_general(a, b, (((1,), (1,)), ((), ())), preferred_element_type=F32)


def _dot_tn(a, b):
    return lax.dot_general(a, b, (((0,), (0,)), ((), ())), preferred_element_type=F32)


def _sigmoid(z):
    return 1.0 / (1.0 + jnp.exp(-z))


def _rms(xv):
    return lax.rsqrt(jnp.mean(xv * xv, axis=-1, keepdims=True) + NORM_EPS)


def _up_slot(idx):
    return jnp.where(idx < 4, 2 * idx, 2 * (idx - 4) + 1)


def _dn_row(idx):
    return C_UP_P * (idx // 2) + R_DN * (idx % 2)


def _mm(a, b, *, ta=False, tb=False, out_dtype=BF, tm=TM, tn=512, tk=None, name):
    m, k_dim = (a.shape[1], a.shape[0]) if ta else a.shape
    n = b.shape[0] if tb else b.shape[1]
    tm, tn = min(tm, m), min(tn, n)
    tk = k_dim if tk is None else min(tk, k_dim)
    nk = k_dim // tk
    assert m % tm == 0 and n % tn == 0 and k_dim % tk == 0
    dims = (((0 if ta else 1,), (1 if tb else 0,)), ((), ()))

    def body(a_ref, b_ref, o_ref, *scratch):
        part = lax.dot_general(a_ref[...], b_ref[...], dims, preferred_element_type=F32)
        if nk == 1:
            o_ref[...] = part.astype(o_ref.dtype)
            return
        acc = scratch[0]
        k = pl.program_id(2)

        @pl.when(k == 0)
        def _():
            acc[...] = part

        @pl.when(k > 0)
        def _():
            acc[...] += part

        @pl.when(k == nk - 1)
        def _():
            o_ref[...] = acc[...].astype(o_ref.dtype)

    a_spec = pl.BlockSpec((tk, tm), lambda i, j, k: (k, i)) if ta else pl.BlockSpec((tm, tk), lambda i, j, k: (i, k))
    b_spec = pl.BlockSpec((tn, tk), lambda i, j, k: (j, k)) if tb else pl.BlockSpec((tk, tn), lambda i, j, k: (k, j))
    return _pcall(
        body, grid=(m // tm, n // tn, nk), in_specs=[a_spec, b_spec],
        out_specs=pl.BlockSpec((tm, tn), lambda i, j, k: (i, j)),
        out_shape=SDS((m, n), out_dtype),
        scratch_shapes=[pltpu.VMEM((tm, tn), F32)] if nk > 1 else [],
        compiler_params=_cp(("parallel", "parallel", "arbitrary")), name=name)(a, b)


def _mm_res_norm(a, w, x, g, *, name):
    s, k_dim = a.shape
    tm = min(TM, s)

    def body(a_ref, w_ref, x_ref, g_ref, xo_ref, h_ref):
        xo = x_ref[...] + _dot(a_ref[...], w_ref[...])
        xo_ref[...] = xo
        h_ref[...] = ((xo * _rms(xo)) * g_ref[...]).astype(BF)

    return _pcall(
        body, grid=(s // tm,),
        in_specs=[pl.BlockSpec((tm, k_dim), lambda i: (i, 0)),
                  pl.BlockSpec((k_dim, D), lambda i: (0, 0), pipeline_mode=pl.Buffered(1)),
                  pl.BlockSpec((tm, D), lambda i: (i, 0)), pl.BlockSpec((1, D), lambda i: (0, 0))],
        out_specs=[pl.BlockSpec((tm, D), lambda i: (i, 0))] * 2,
        out_shape=[SDS((s, D), F32), SDS((s, D), BF)],
        compiler_params=_cp(("parallel",)), name=name)(a, w, x, g)


def _mm_nt_normbwd(da, w, x, dres, g, *, tk=None, name):
    s, k_dim = da.shape
    tm = min(TM, s)
    tk = k_dim if tk is None else tk
    nk = k_dim // tk
    assert k_dim % tk == 0

    def body(da_ref, w_ref, x_ref, dres_ref, g_ref, dx_ref, dxb_ref, dg_ref, *scratch):
        i, k = pl.program_id(0), pl.program_id(1)
        part = _dot_nt(da_ref[...], w_ref[...])
        if nk > 1:
            acc = scratch[0]

            @pl.when(k == 0)
            def _():
                acc[...] = part

            @pl.when(k > 0)
            def _():
                acc[...] += part

        @pl.when((i == 0) & (k == 0))
        def _():
            dg_ref[...] = jnp.zeros_like(dg_ref)

        @pl.when(k == nk - 1)
        def _():
            dh = acc[...] if nk > 1 else part
            xv = x_ref[...]
            r = _rms(xv)
            xn = xv * r
            dg_ref[0:1, :] += jnp.sum(dh * xn, axis=0, keepdims=True)
            dxn = dh * g_ref[...]
            dx = dres_ref[...] + r * (dxn - xn * jnp.mean(dxn * xn, axis=-1, keepdims=True))
            dx_ref[...] = dx
            dxb_ref[...] = dx.astype(BF)

    row = lambda i, k: (i, 0)
    w_spec = (pl.BlockSpec((D, tk), lambda i, k: (0, k)) if nk > 1 else
              pl.BlockSpec((D, tk), lambda i, k: (0, 0), pipeline_mode=pl.Buffered(1)))
    return _pcall(
        body, grid=(s // tm, nk),
        in_specs=[pl.BlockSpec((tm, tk), lambda i, k: (i, k)), w_spec,
                  pl.BlockSpec((tm, D), row), pl.BlockSpec((tm, D), row), pl.BlockSpec((1, D), lambda i, k: (0, 0))],
        out_specs=[pl.BlockSpec((tm, D), row), pl.BlockSpec((tm, D), row), pl.BlockSpec((8, D), lambda i, k: (0, 0))],
        out_shape=[SDS((s, D), F32), SDS((s, D), BF), SDS((8, D), F32)],
        scratch_shapes=[pltpu.VMEM((tm, D), F32)] if nk > 1 else [],
        compiler_params=_cp(("arbitrary", "arbitrary")), name=name)(da, w, x, dres, g)


def _rms_fwd(x, g, *, name):
    s = x.shape[0]
    tm = min(TM, s)

    def body(x_ref, g_ref, h_ref):
        xv = x_ref[...]
        h_ref[...] = ((xv * _rms(xv)) * g_ref[...]).astype(BF)

    return _pcall(
        body, grid=(s // tm,),
        in_specs=[pl.BlockSpec((tm, D), lambda i: (i, 0)), pl.BlockSpec((1, D), lambda i: (0, 0))],
        out_specs=pl.BlockSpec((tm, D), lambda i: (i, 0)), out_shape=SDS((s, D), BF),
        compiler_params=_cp(("parallel",)), name=name)(x, g)


def _loss_head(x, tgt, g, *, name):
    s = x.shape[0]
    tm = min(TM, s)

    def body(x_ref, t_ref, g_ref, dx_ref, dxb_ref, sums_ref):
        @pl.when(pl.program_id(0) == 0)
        def _():
            sums_ref[...] = jnp.zeros_like(sums_ref)

        xv = x_ref[...]
        r = _rms(xv)
        xn = xv * r
        diff = xn * g_ref[...] - t_ref[...]
        sums_ref[0:1, :] += jnp.sum(diff * diff, axis=0, keepdims=True)
        dy = diff * (1.0 / D)
        sums_ref[1:2, :] += jnp.sum(dy * xn, axis=0, keepdims=True)
        dxn = dy * g_ref[...]
        dx = r * (dxn - xn * jnp.mean(dxn * xn, axis=-1, keepdims=True))
        dx_ref[...] = dx
        dxb_ref[...] = dx.astype(BF)

    row = lambda i: (i, 0)
    return _pcall(
        body, grid=(s // tm,),
        in_specs=[pl.BlockSpec((tm, D), row), pl.BlockSpec((tm, D), row), pl.BlockSpec((1, D), lambda i: (0, 0))],
        out_specs=[pl.BlockSpec((tm, D), row), pl.BlockSpec((tm, D), row), pl.BlockSpec((8, D), lambda i: (0, 0))],
        out_shape=[SDS((s, D), F32), SDS((s, D), BF), SDS((8, D), F32)],
        compiler_params=_cp(("arbitrary",)), name=name)(x, tgt, g)


def _halo_before(i, tr, h):
    return jnp.maximum(i * (tr // h) - 1, 0)


def _halo_after(i, tr, h, s):
    return jnp.minimum((i + 1) * (tr // h), s // h - 1)


def _taps(buf, w_ref, sl, k_w, base, rows):
    acc = None
    for k in range(k_w):
        t = w_ref[k:k + 1, sl] * buf[base + k:base + k + rows, sl]
        acc = t if acc is None else acc + t
    return acc


def _taps_rev(buf, w_ref, sl, k_w, base, rows):
    acc = None
    for k in range(k_w):
        t = w_ref[k:k + 1, sl] * buf[base + k_w - 1 - k:base + k_w - 1 - k + rows, sl]
        acc = t if acc is None else acc + t
    return acc


def _tap_grads(dw_ref, dc, buf, sl, k_w, base, rows):
    for k in range(k_w):
        dw_ref[k:k + 1, sl] += jnp.sum(dc * buf[base + k:base + k + rows, sl], axis=0, keepdims=True)


def _bra_fwd(proj, cw, *, name):
    s = proj.shape[0]
    tr, h = min(TR, s), H_S
    sub = min(SUB, tr)

    def body(cur, halo, w_ref, za_ref, cvb):
        i = pl.program_id(0)
        hv = halo[:, D:2 * D].astype(F32) * halo[:, 2 * D:3 * D].astype(F32)
        cvb[0:h, :] = jnp.where(i == 0, 0.0, hv)
        cvb[h:h + tr, :] = cur[:, D:2 * D].astype(F32) * cur[:, 2 * D:3 * D].astype(F32)
        for c in range(D // LANE):
            sl = slice(LANE * c, LANE * c + LANE)
            ca = _taps(cvb, w_ref, sl, K_A, h - (K_A - 1), tr)
            za_ref[:, sl] = (cur[:, sl].astype(F32) * ca).astype(BF)

    return _pcall(
        body, grid=(s // tr,),
        in_specs=[pl.BlockSpec((tr, 3 * D), lambda i: (i, 0)),
                  pl.BlockSpec((h, 3 * D), lambda i: (_halo_before(i, tr, h), 0)),
                  pl.BlockSpec((8, D), lambda i: (0, 0))],
        out_specs=pl.BlockSpec((tr, D), lambda i: (i, 0)), out_shape=SDS((s, D), BF),
        scratch_shapes=[pltpu.VMEM((h + tr, D), F32)],
        compiler_params=_cp(("parallel",)), name=name)(proj, proj, cw)


def _bra_bwd(proj, dza, cw, dproj, *, name):
    s = proj.shape[0]
    tr, h = min(TR, s), H_S
    sub = min(SUB, tr)
    n = s // tr

    def body(before, cur, after, dz_cur, dz_after, w_ref, dproj_in, da_ref, dw_ref, cvb, dcab):
        del dproj_in
        i = pl.program_id(0)

        @pl.when(i == 0)
        def _():
            dw_ref[...] = jnp.zeros_like(dw_ref)

        first, last = i == 0, i == n - 1
        cvb[0:h, :] = jnp.where(first, 0.0, before[:, D:2 * D].astype(F32) * before[:, 2 * D:3 * D].astype(F32))
        cvb[h:h + tr, :] = cur[:, D:2 * D].astype(F32) * cur[:, 2 * D:3 * D].astype(F32)
        dcab[0:tr, :] = dz_cur[...].astype(F32) * cur[:, 0:D].astype(F32)
        dcab[tr:tr + h, :] = jnp.where(last, 0.0, dz_after[...].astype(F32) * after[:, 0:D].astype(F32))
        for c in range(D // LANE):
            sl = slice(LANE * c, LANE * c + LANE)
            gl, vl = slice(D + LANE * c, D + LANE * c + LANE), slice(2 * D + LANE * c, 2 * D + LANE * c + LANE)
            for r0 in range(0, tr, sub):
                rows = slice(r0, r0 + sub)
                ca = _taps(cvb, w_ref, sl, K_A, h - (K_A - 1) + r0, sub)
                da_ref[rows, sl] = (dz_cur[rows, sl].astype(F32) * ca).astype(BF)
                dcv = _taps_rev(dcab, w_ref, sl, K_A, r0, sub)
                da_ref[rows, gl] = (dcv * cur[rows, vl].astype(F32)).astype(BF)
                da_ref[rows, vl] = (dcv * cur[rows, gl].astype(F32)).astype(BF)
                _tap_grads(dw_ref, dcab[rows, sl], cvb, sl, K_A, h - (K_A - 1) + r0, sub)

    return _pcall(
        body, grid=(n,),
        in_specs=[pl.BlockSpec((h, 3 * D), lambda i: (_halo_before(i, tr, h), 0)),
                  pl.BlockSpec((tr, 3 * D), lambda i: (i, 0)),
                  pl.BlockSpec((h, 3 * D), lambda i: (_halo_after(i, tr, h, s), 0)),
                  pl.BlockSpec((tr, D), lambda i: (i, 0)),
                  pl.BlockSpec((h, D), lambda i: (_halo_after(i, tr, h, s), 0)),
                  pl.BlockSpec((8, D), lambda i: (0, 0)), ANY],
        out_specs=[pl.BlockSpec((tr, 3 * D), lambda i: (i, 0)), pl.BlockSpec((8, D), lambda i: (0, 0))],
        out_shape=[SDS(dproj.shape, BF), SDS((8, D), F32)], input_output_aliases={6: 0},
        scratch_shapes=[pltpu.VMEM((h + tr, D), F32), pltpu.VMEM((tr + h, D), F32)],
        compiler_params=_cp(("arbitrary",)), name=name)(proj, proj, proj, dza, dza, cw, dproj)


_U_COL, _UG_COL = 3, 4


def _brb_conv_fwd(proj, cw, bias, *, name):
    s = proj.shape[0]
    tr, h = min(TR, s), H_L
    sub = min(SUB, tr)

    def body(u_cur, ug_cur, u_halo, ug_halo, w_ref, b_ref, cb_ref, glb, shifted):
        i = pl.program_id(0)
        glb[0:h, :] = jnp.where(i == 0, 0.0, u_halo[...].astype(F32) * _sigmoid(ug_halo[...].astype(F32)))
        glb[h:h + tr, :] = u_cur[...].astype(F32) * _sigmoid(ug_cur[...].astype(F32))
        for c in range(D // LANE):
            sl = slice(LANE * c, LANE * c + LANE)
            for r in range(1, 8):
                shifted[r] = glb[8 - r:8 - r + tr + 24, sl]
            for r0 in range(0, tr, sub):
                acc = None
                for k in range(K_B):
                    q, r = divmod(K_B - 1 - k, 8)
                    at = r0 - 8 * q
                    win = shifted[r, 24 + at:24 + at + sub, :] if r else glb[h + at:h + at + sub, sl]
                    term = w_ref[k:k + 1, sl] * win
                    acc = term if acc is None else acc + term
                cb_ref[r0:r0 + sub, sl] = (acc + b_ref[:, sl]).astype(BF)

    return _pcall(
        body, grid=(s // tr,),
        in_specs=[pl.BlockSpec((tr, D), lambda i: (i, _U_COL)), pl.BlockSpec((tr, D), lambda i: (i, _UG_COL)),
                  pl.BlockSpec((h, D), lambda i: (_halo_before(i, tr, h), _U_COL)),
                  pl.BlockSpec((h, D), lambda i: (_halo_before(i, tr, h), _UG_COL)),
                  pl.BlockSpec((32, D), lambda i: (0, 0)), pl.BlockSpec((1, D), lambda i: (0, 0))],
        out_specs=pl.BlockSpec((tr, D), lambda i: (i, 0)), out_shape=SDS((s, D), BF),
        scratch_shapes=[pltpu.VMEM((h + tr, D), F32), pltpu.VMEM((8, tr + 24, LANE), F32)],
        compiler_params=_cp(("parallel",)), name=name)(proj, proj, proj, proj, cw, bias)


def _brb_conv_bwd(proj, dcb, dq, cw, dproj, *, name):
    s = proj.shape[0]
    tr, h = min(TR, s), H_L
    sub = min(SUB, tr)
    n = s // tr
    nb = -(-(tr + 24) // sub)
    sel = _row_selector(sub, list(range(8)))

    def body(u_cur, ug_cur, d_cur, d_after, dq_ref, sel_ref, w_ref, dproj_in, db_ref, dw_ref, dcbb, shifted):
        del dproj_in
        i = pl.program_id(0)

        @pl.when(i == 0)
        def _():
            dw_ref[...] = jnp.zeros_like(dw_ref)

        db_ref[:, 2 * D:3 * D] = dq_ref[...]
        after = d_after[...]
        dcbb[0:tr, :] = d_cur[...]
        dcbb[tr:tr + h, :] = jnp.where(i == n - 1, jnp.zeros_like(after), after)
        dcbb[tr + h:(nb + 1) * sub, :] = jnp.zeros(((nb + 1) * sub - h - tr, D), BF)
        for c in range(D // LANE):
            sl = slice(LANE * c, LANE * c + LANE)
            for blk in range(nb):
                res = _dot(sel_ref[...], dcbb[blk * sub:(blk + 2) * sub, sl])
                for r in range(8):
                    shifted[r, blk * sub:(blk + 1) * sub, :] = res[r * sub:(r + 1) * sub]
            for r0 in range(0, tr, sub):
                u = u_cur[r0:r0 + sub, sl].astype(F32)
                sg = _sigmoid(ug_cur[r0:r0 + sub, sl].astype(F32))
                glu = u * sg
                dglu = None
                for k in range(K_B):
                    q, r = divmod(K_B - 1 - k, 8)
                    at = r0 + 8 * q
                    win = shifted[r, at:at + sub, :]
                    term = w_ref[k:k + 1, sl] * win
                    dglu = term if dglu is None else dglu + term
                    dw_ref[k:k + 1, sl] += jnp.sum(win * glu, axis=0, keepdims=True)
                db_ref[r0:r0 + sub, sl] = (dglu * sg).astype(BF)
                db_ref[r0:r0 + sub, D + LANE * c:D + LANE * c + LANE] = (dglu * u * sg * (1.0 - sg)).astype(BF)

    return _pcall(
        body, grid=(n,),
        in_specs=[pl.BlockSpec((tr, D), lambda i: (i, _U_COL)), pl.BlockSpec((tr, D), lambda i: (i, _UG_COL)),
                  pl.BlockSpec((tr, D), lambda i: (i, 0)),
                  pl.BlockSpec((h, D), lambda i: (_halo_after(i, tr, h, s), 0)),
                  pl.BlockSpec((tr, D), lambda i: (i, 0)),
                  pl.BlockSpec(sel.shape, lambda i: (0, 0)),
                  pl.BlockSpec((32, D), lambda i: (0, 0)), ANY],
        out_specs=[pl.BlockSpec((tr, 3 * D), lambda i: (i, 1)), pl.BlockSpec((32, D), lambda i: (0, 0))],
        out_shape=[SDS(dproj.shape, BF), SDS((32, D), F32)], input_output_aliases={7: 0},
        scratch_shapes=[pltpu.VMEM(((nb + 1) * sub, D), BF), pltpu.VMEM((8, nb * sub, LANE), F32)],
        compiler_params=_cp(("arbitrary",)), name=name)(proj, proj, dcb, dcb, dq, sel, cw, dproj)


def _ln_silu_fwd(cb, g, b, *, name):
    s = cb.shape[0]
    tm = min(TM, s)

    def body(cb_ref, g_ref, b_ref, sb_ref):
        z = cb_ref[...].astype(F32)
        zc = z - jnp.mean(z, axis=-1, keepdims=True)
        ln = (zc * lax.rsqrt(jnp.mean(zc * zc, axis=-1, keepdims=True) + NORM_EPS)) * g_ref[...] + b_ref[...]
        sb_ref[...] = (ln * _sigmoid(ln)).astype(BF)

    row = lambda i: (i, 0)
    vec = pl.BlockSpec((1, D), lambda i: (0, 0))
    return _pcall(
        body, grid=(s // tm,), in_specs=[pl.BlockSpec((tm, D), row), vec, vec],
        out_specs=pl.BlockSpec((tm, D), row), out_shape=SDS((s, D), BF),
        compiler_params=_cp(("parallel",)), name=name)(cb, g, b)


def _ln_silu_bwd(cb, dsb, g, b, *, name):
    s = cb.shape[0]
    tm = min(TM, s)

    def body(cb_ref, dsb_ref, g_ref, b_ref, dcb_ref, sums_ref):
        @pl.when(pl.program_id(0) == 0)
        def _():
            sums_ref[...] = jnp.zeros_like(sums_ref)

        z = cb_ref[...].astype(F32)
        zc = z - jnp.mean(z, axis=-1, keepdims=True)
        rstd = lax.rsqrt(jnp.mean(zc * zc, axis=-1, keepdims=True) + NORM_EPS)
        lnh = zc * rstd
        ln = lnh * g_ref[...] + b_ref[...]
        sg = _sigmoid(ln)
        dln = dsb_ref[...].astype(F32) * (sg * (1.0 + ln * (1.0 - sg)))
        sums_ref[0:1, :] += jnp.sum(dln * lnh, axis=0, keepdims=True)
        sums_ref[1:2, :] += jnp.sum(dln, axis=0, keepdims=True)
        dlnh = dln * g_ref[...]
        dz = rstd * (dlnh - jnp.mean(dlnh, axis=-1, keepdims=True)
                     - lnh * jnp.mean(dlnh * lnh, axis=-1, keepdims=True))
        sums_ref[2:3, :] += jnp.sum(dz, axis=0, keepdims=True)
        dcb_ref[...] = dz.astype(BF)

    row = lambda i: (i, 0)
    vec = pl.BlockSpec((1, D), lambda i: (0, 0))
    return _pcall(
        body, grid=(s // tm,), in_specs=[pl.BlockSpec((tm, D), row), pl.BlockSpec((tm, D), row), vec, vec],
        out_specs=[pl.BlockSpec((tm, D), row), pl.BlockSpec((8, D), lambda i: (0, 0))],
        out_shape=[SDS((s, D), BF), SDS((8, D), F32)],
        compiler_params=_cp(("arbitrary",)), name=name)(cb, dsb, g, b)


_Q_COL = 5 * D // HEAD


def _kv_prep(mem, g, wkv, *, name):
    m = mem.shape[0]

    def body(mem_ref, g_ref, w_ref, memn_ref, kv_ref):
        mv = mem_ref[...]
        memn = ((mv * _rms(mv)) * g_ref[...]).astype(BF)
        memn_ref[...] = memn
        for dev in range(N_DEV):
            kv_ref[:, dev * C_KV:(dev + 1) * C_KV] = _dot(memn, w_ref[dev]).astype(BF)

    return _pcall(body, out_shape=[SDS((m, D), BF), SDS((m, 2 * D), BF)],
                  compiler_params=_cp(), name=name)(mem, g, wkv)


def _softmax_rows(q, k):
    sc = _dot_nt(q, k) * (1.0 / (HEAD ** 0.5))
    e = jnp.exp(sc - jnp.max(sc, axis=-1, keepdims=True))
    return e / jnp.sum(e, axis=-1, keepdims=True)


def _attn_fwd(proj, kv, *, name):
    s, m = proj.shape[0], kv.shape[0]
    tm = min(TM, s)

    def body(q_ref, kv_ref, o_ref):
        for hd in range(N_HEADS):
            cols = slice(hd * HEAD, (hd + 1) * HEAD)
            p = _softmax_rows(q_ref[:, cols], kv_ref[:, cols])
            o_ref[:, cols] = _dot(p.astype(BF), kv_ref[:, D + hd * HEAD:D + (hd + 1) * HEAD]).astype(BF)

    return _pcall(
        body, grid=(s // tm,),
        in_specs=[pl.BlockSpec((tm, D), lambda i: (i, _Q_COL // N_HEADS)),
                  pl.BlockSpec((m, 2 * D), lambda i: (0, 0))],
        out_specs=pl.BlockSpec((tm, D), lambda i: (i, 0)), out_shape=SDS((s, D), BF),
        compiler_params=_cp(("parallel",)), name=name)(proj, kv)


def _attn_bwd(proj, kv, do, *, name):
    s, m = proj.shape[0], kv.shape[0]
    tm = min(TM, s)

    def body(q_ref, kv_ref, do_ref, dq_ref, dk_ref, dv_ref):
        @pl.when(pl.program_id(0) == 0)
        def _():
            dk_ref[...] = jnp.zeros_like(dk_ref)
            dv_ref[...] = jnp.zeros_like(dv_ref)

        for hd in range(N_HEADS):
            cols = slice(hd * HEAD, (hd + 1) * HEAD)
            q, k, dov = q_ref[:, cols], kv_ref[:, cols], do_ref[:, cols]
            p = _softmax_rows(q, k)
            dp = _dot_nt(dov, kv_ref[:, D + hd * HEAD:D + (hd + 1) * HEAD])
            dv_ref[:, cols] += _dot_tn(p.astype(BF), dov)
            ds = (p * (dp - jnp.sum(dp * p, axis=-1, keepdims=True)) * (1.0 / (HEAD ** 0.5))).astype(BF)
            dq_ref[:, cols] = _dot(ds, k).astype(BF)
            dk_ref[:, cols] += _dot_tn(ds, q)

    return _pcall(
        body, grid=(s // tm,),
        in_specs=[pl.BlockSpec((tm, D), lambda i: (i, _Q_COL // N_HEADS)),
                  pl.BlockSpec((m, 2 * D), lambda i: (0, 0)),
                  pl.BlockSpec((tm, D), lambda i: (i, 0))],
        out_specs=[pl.BlockSpec((tm, D), lambda i: (i, 0)),
                   pl.BlockSpec((m, D), lambda i: (0, 0)),
                   pl.BlockSpec((m, D), lambda i: (0, 0))],
        out_shape=[SDS((s, D), BF), SDS((m, D), F32), SDS((m, D), F32)],
        compiler_params=_cp(("arbitrary",)), name=name)(proj, kv, do)


def _kv_bwd(mem, g, memn, dk, dv, wkv, *, name):
    def body(mem_ref, g_ref, memn_ref, dk_ref, dv_ref, w_ref, dw_ref, dg_ref):
        memn = memn_ref[...]
        dmemn = None
        for dev in range(N_DEV):
            d_ref, col = (dk_ref, dev) if dev < N_HEADS else (dv_ref, dev - N_HEADS)
            dslab = d_ref[:, col * C_KV:(col + 1) * C_KV].astype(BF)
            dw_ref[dev] = _dot_tn(memn, dslab).astype(BF)
            part = _dot_nt(dslab, w_ref[dev])
            dmemn = part if dmemn is None else dmemn + part
        mv = mem_ref[...]
        dg_ref[...] = jnp.zeros_like(dg_ref)
        dg_ref[0:1, :] = jnp.sum(dmemn * (mv * _rms(mv)), axis=0, keepdims=True)

    assert C_KV == HEAD
    return _pcall(body, out_shape=[SDS((N_DEV, D, C_KV), BF), SDS((8, D), F32)],
                  compiler_params=_cp(), name=name)(mem, g, memn, dk, dv, wkv)


_TM_MIX = 512


def _mix_out(x, za, sb, o, proj, w4, bg, g_next, *, name):
    s = x.shape[0]
    tm = min(_TM_MIX, s)

    def body(x_ref, za_ref, sb_ref, o_ref, pg_ref, w4_ref, bg_ref, gn_ref,
             ya_ref, yb_ref, yc_ref, mg_ref, x1_ref, h_ref):
        ys = (_dot(za_ref[...], w4_ref[0]), _dot(sb_ref[...], w4_ref[1]), _dot(o_ref[...], w4_ref[2]))
        merged = None
        for j, (y, y_ref) in enumerate(zip(ys, (ya_ref, yb_ref, yc_ref))):
            y_ref[...] = y.astype(BF)
            gate = _sigmoid(pg_ref[:, j * D:(j + 1) * D].astype(F32) + bg_ref[:, j * D:(j + 1) * D])
            merged = gate * y if merged is None else merged + gate * y
        mg = merged.astype(BF)
        mg_ref[...] = mg
        x1 = x_ref[...] + _dot(mg, w4_ref[3])
        x1_ref[...] = x1
        h_ref[...] = ((x1 * _rms(x1)) * gn_ref[...]).astype(BF)

    row = lambda i: (i, 0)
    act = pl.BlockSpec((tm, D), row)
    return _pcall(
        body, grid=(s // tm,),
        in_specs=[act, act, act, act, pl.BlockSpec((tm, 3 * D), lambda i: (i, 2)),
                  pl.BlockSpec((4, D, D), lambda i: (0, 0, 0), pipeline_mode=pl.Buffered(1)), pl.BlockSpec((1, 3 * D), lambda i: (0, 0)),
                  pl.BlockSpec((1, D), lambda i: (0, 0))],
        out_specs=[act] * 6,
        out_shape=[SDS((s, D), BF)] * 4 + [SDS((s, D), F32), SDS((s, D), BF)],
        compiler_params=_cp(("parallel",)), name=name)(x, za, sb, o, proj, w4, bg, g_next)


def _mix_bwd(dxb, ya, yb, yc, proj, w4, bg, *, name):
    s = dxb.shape[0]
    tm = min(_TM_MIX, s)

    def body(dx_ref, ya_ref, yb_ref, yc_ref, pg_ref, w4_ref, bg_ref,
             dya_ref, dyb_ref, dyc_ref, dza_ref, dsb_ref, do_ref, dgt_ref, dbg_ref):
        @pl.when(pl.program_id(0) == 0)
        def _():
            dbg_ref[...] = jnp.zeros_like(dbg_ref)

        dm = _dot_nt(dx_ref[...], w4_ref[3])
        for j, (y_ref, dy_ref, din_ref) in enumerate(zip((ya_ref, yb_ref, yc_ref), (dya_ref, dyb_ref, dyc_ref),
                                                         (dza_ref, dsb_ref, do_ref))):
            cols = slice(j * D, (j + 1) * D)
            gate = _sigmoid(pg_ref[:, cols].astype(F32) + bg_ref[:, cols])
            dy = (dm * gate).astype(BF)
            dy_ref[...] = dy
            din_ref[...] = _dot_nt(dy, w4_ref[j]).astype(BF)
            dpre = dm * y_ref[...].astype(F32) * gate * (1.0 - gate)
            dgt_ref[:, cols] = dpre.astype(BF)
            dbg_ref[0:1, cols] += jnp.sum(dpre, axis=0, keepdims=True)

    row = lambda i: (i, 0)
    act = pl.BlockSpec((tm, D), row)
    return _pcall(
        body, grid=(s // tm,),
        in_specs=[act, act, act, act, pl.BlockSpec((tm, 3 * D), lambda i: (i, 2)),
                  pl.BlockSpec((4, D, D), lambda i: (0, 0, 0), pipeline_mode=pl.Buffered(1)), pl.BlockSpec((1, 3 * D), lambda i: (0, 0))],
        out_specs=[act] * 6 + [pl.BlockSpec((tm, 3 * D), lambda i: (i, 2)),
                               pl.BlockSpec((8, 3 * D), lambda i: (0, 0))],
        out_shape=[SDS((s, D), BF)] * 6 + [SDS((s, 9 * D), BF), SDS((8, 3 * D), F32)],
        compiler_params=_cp(("arbitrary",)), name=name)(dxb, ya, yb, yc, proj, w4, bg)


_PAIR = 2 * C_UP_P


def _row_selector(sub, first_cols):
    rows = np.arange(len(first_cols) * sub)
    col = np.asarray(first_cols)[rows // sub] + rows % sub
    return jnp.asarray(np.arange(2 * sub)[None, :] == col[:, None], BF)


def _ffn_act(u2, cw, *, name):
    s = u2.shape[0]
    tr = min(TR, s)
    sub = min(SUB, tr)
    sel = _row_selector(sub, [sub - (K_F - 1 - k) for k in range(K_F)])

    def body(cur, prev, sel_ref, w_ref, act_ref, c2_ref, xb, win):
        i = pl.program_id(1)
        before = prev[...]
        xb[0:sub, :] = jnp.where(i == 0, jnp.zeros_like(before), before)
        xb[sub:sub + tr, :] = cur[...]
        for r0 in range(0, tr, sub):
            win[...] = _dot(sel_ref[...], xb[r0:r0 + 2 * sub, :])
            for c in range(C_UP_P // LANE):
                gl = slice(LANE * c, LANE * c + LANE)
                ul = slice(C_UP_P + LANE * c, C_UP_P + LANE * c + LANE)
                gt = sum(w_ref[k:k + 1, gl] * win[k * sub:(k + 1) * sub, gl] for k in range(K_F))
                up = sum(w_ref[k:k + 1, ul] * win[k * sub:(k + 1) * sub, ul] for k in range(K_F))
                c2_ref[r0:r0 + sub, gl] = gt.astype(BF)
                c2_ref[r0:r0 + sub, ul] = up.astype(BF)
                act_ref[r0:r0 + sub, gl] = (gt * _sigmoid(gt) * up).astype(BF)

    return _pcall(
        body, grid=(4, s // tr),
        in_specs=[pl.BlockSpec((tr, _PAIR), lambda p, i: (i, p)),
                  pl.BlockSpec((sub, _PAIR), lambda p, i: (_halo_before(i, tr, sub), p)),
                  pl.BlockSpec(sel.shape, lambda p, i: (0, 0)),
                  pl.BlockSpec((8, _PAIR), lambda p, i: (0, p))],
        out_specs=[pl.BlockSpec((tr, C_UP_P), lambda p, i: (i, p)), pl.BlockSpec((tr, _PAIR), lambda p, i: (i, p))],
        out_shape=[SDS((s, FF_P), BF), SDS((s, 2 * FF_P), BF)],
        scratch_shapes=[pltpu.VMEM((sub + tr, _PAIR), BF), pltpu.VMEM((K_F * sub, _PAIR), F32)],
        compiler_params=_cp(("parallel", "parallel")), name=name)(u2, u2, sel, cw)


def _ffn_bwd(u2, c2, dact, cw, *, name):
    s = u2.shape[0]
    tr, h = min(TR, s), H_S
    sub = min(SUB, tr)
    n = s // tr
    sel = _row_selector(sub, [K_F - 1 - k for k in range(K_F)])

    def body(u_cur, c_cur, c_after, da_cur, da_after, sel_ref, w_ref, du_ref, dw_ref, dcb, win):
        i = pl.program_id(1)
        last = i == n - 1

        @pl.when(i == 0)
        def _():
            dw_ref[...] = jnp.zeros_like(dw_ref)

        def conv_grad(gt, up, da):
            gt, up, da = gt.astype(F32), up.astype(F32), da.astype(F32)
            sg = _sigmoid(gt)
            return (da * up * (sg * (1.0 + gt * (1.0 - sg)))).astype(BF), (da * (gt * sg)).astype(BF)

        for c in range(C_UP_P // LANE):
            gl = slice(LANE * c, LANE * c + LANE)
            ul = slice(C_UP_P + LANE * c, C_UP_P + LANE * c + LANE)
            for r0 in range(0, tr, sub):
                rows = slice(r0, r0 + sub)
                dcb[rows, gl], dcb[rows, ul] = conv_grad(c_cur[rows, gl], c_cur[rows, ul], da_cur[rows, gl])
            dg, du_ = conv_grad(c_after[:, gl], c_after[:, ul], da_after[:, gl])
            dcb[tr:tr + h, gl] = jnp.where(last, jnp.zeros_like(dg), dg)
            dcb[tr:tr + h, ul] = jnp.where(last, jnp.zeros_like(du_), du_)
        dcb[tr + h:tr + sub, :] = jnp.zeros((sub - h, _PAIR), BF)
        for r0 in range(0, tr, sub):
            win[...] = _dot(sel_ref[...], dcb[r0:r0 + 2 * sub, :])
            for c in range(_PAIR // LANE):
                sl = slice(LANE * c, LANE * c + LANE)
                u = u_cur[r0:r0 + sub, sl].astype(F32)
                du = None
                for k in range(K_F):
                    wk = win[k * sub:(k + 1) * sub, sl]
                    term = w_ref[k:k + 1, sl] * wk
                    du = term if du is None else du + term
                    dw_ref[k:k + 1, sl] += jnp.sum(wk * u, axis=0, keepdims=True)
                du_ref[r0:r0 + sub, sl] = du.astype(BF)

    return _pcall(
        body, grid=(4, n),
        in_specs=[pl.BlockSpec((tr, _PAIR), lambda p, i: (i, p)),
                  pl.BlockSpec((tr, _PAIR), lambda p, i: (i, p)),
                  pl.BlockSpec((h, _PAIR), lambda p, i: (_halo_after(i, tr, h, s), p)),
                  pl.BlockSpec((tr, C_UP_P), lambda p, i: (i, p)),
                  pl.BlockSpec((h, C_UP_P), lambda p, i: (_halo_after(i, tr, h, s), p)),
                  pl.BlockSpec(sel.shape, lambda p, i: (0, 0)),
                  pl.BlockSpec((8, _PAIR), lambda p, i: (0, p))],
        out_specs=[pl.BlockSpec((tr, _PAIR), lambda p, i: (i, p)), pl.BlockSpec((8, _PAIR), lambda p, i: (0, p))],
        out_shape=[SDS((s, 2 * FF_P), BF), SDS((8, 2 * FF_P), F32)],
        scratch_shapes=[pltpu.VMEM((tr + sub, _PAIR), BF), pltpu.VMEM((K_F * sub, _PAIR), F32)],
        compiler_params=_cp(("parallel", "arbitrary")), name=name)(u2, c2, c2, dact, dact, sel, cw)


def _relations():
    x, y, c = lax.axis_index("x"), lax.axis_index("y"), lax.axis_index("c")
    out = []
    for r in range(1, N_DEV):
        rx, ry, rc = (r >> 2) & 1, (r >> 1) & 1, r & 1
        out.append((r, (x ^ rx, y ^ ry, c ^ rc)))
    return out


def _my_index():
    return 4 * lax.axis_index("x") + 2 * lax.axis_index("y") + lax.axis_index("c")


def _slab(kind, ref, idx):
    if kind == "win":
        return ref.at[:, pl.ds(pl.multiple_of(idx * C_IN, LANE), C_IN)]
    if kind == "wup":
        return ref.at[:, pl.ds(pl.multiple_of(_up_slot(idx) * C_UP_P, LANE), C_UP_P)]
    if kind == "wupT":
        return ref.at[pl.ds(pl.multiple_of(_up_slot(idx) * C_UP_P, LANE), C_UP_P), :]
    if kind == "wkv":
        return ref.at[idx]
    if kind == "w4":
        return ref.at[:, pl.ds(pl.multiple_of(idx * R_O, 16), R_O), :]
    if kind == "wdn":
        return ref.at[pl.ds(pl.multiple_of(_dn_row(idx), 16), R_DN), :]
    assert kind == "cv"
    return ref.at[idx]


_WHOLE = {"win": ((D, 9 * D), BF), "wup": ((D, 2 * FF_P), BF), "wkv": ((N_DEV, D, C_KV), BF),
          "w4": ((4, D, D), BF), "wdn": ((FF_P, D), BF)}
_SHARD = {"win": (D, C_IN), "wup": (D, C_UP_P), "wupT": (C_UP_P, D), "wkv": (D, C_KV), "w4": (4, R_O, D),
          "wdn": (R_DN, D)}
HBM_SPEC = pl.BlockSpec(memory_space=pltpu.HBM)
SEM_SPEC = pl.BlockSpec(memory_space=pltpu.SEMAPHORE)
_DATAFLOW = pltpu.SideEffectType.DATAFLOW_SIDE_EFFECTING


def _scatter_maps(kinds):
    return ((lambda srcs, lands, a, idx: _slab(kinds[a], srcs[a], idx)),
            (lambda lands, a, idx: lands[a].at[idx]))


_SLOTTED = ("wkv", "cv")


def _own_slab_blocks(kind, shard_shape):
    if kind in ("win", "wup"):
        rows, slot = 256, (_up_slot if kind == "wup" else (lambda m: m))
        return (shard_shape[0] // rows, (rows, shard_shape[1]), (lambda i, me: (i, slot(me[0]))),
                (lambda i, me: (i, 0)), (lambda i, me: (me[0], i, 0)))
    if kind == "wupT":
        rows = 256
        steps = shard_shape[0] // rows
        return (steps, (rows, D), (lambda i, me: (_up_slot(me[0]) * steps + i, 0)), (lambda i, me: (i, 0)),
                (lambda i, me: (me[0], i, 0)))
    if kind == "w4":
        return (1, shard_shape, (lambda i, me: (0, me[0], 0)), (lambda i, me: (0, 0, 0)),
                (lambda i, me: (me[0], 0, 0, 0)))
    if kind == "wdn":
        rows = 32
        return (R_DN // rows, (rows, D), (lambda i, me: (_dn_row(me[0]) // rows + i, 0)), (lambda i, me: (i, 0)),
                (lambda i, me: (me[0], i, 0)))
    assert kind in _SLOTTED
    rows = min(256, shard_shape[0])
    return (shard_shape[0] // rows, (rows, shard_shape[1]), (lambda i, me: (me[0], i, 0)),
            (lambda i, me: (i, 0)), (lambda i, me: (me[0], i, 0)))


def _place_own(kind, src, out_sds, gather, me_arr, *, name):
    shard_shape = src.shape if gather else out_sds.shape[1:]
    steps, blk, whole_idx, shard_idx, staging_idx = _own_slab_blocks(kind, shard_shape)
    slotted = kind in _SLOTTED
    whole_spec = pl.BlockSpec(((None,) if slotted else ()) + tuple(blk), whole_idx)
    if gather:
        in_spec, out_spec = pl.BlockSpec(tuple(blk), shard_idx), whole_spec
    else:
        in_spec, out_spec = whole_spec, pl.BlockSpec((None,) + tuple(blk), staging_idx)
    zero_init = gather and kind == "wdn"

    def body(me_ref, src_ref, *rest):
        rest[-1][...] = src_ref[...].astype(rest[-1].dtype)

    operands = (me_arr, src) + ((jnp.zeros(out_sds.shape, out_sds.dtype),) if zero_init else ())
    return _pcall(
        body,
        grid_spec=pltpu.PrefetchScalarGridSpec(
            num_scalar_prefetch=1, grid=(steps,), in_specs=[in_spec] + ([ANY] if zero_init else []),
            out_specs=out_spec),
        out_shape=out_sds, input_output_aliases={2: 0} if zero_init else {},
        compiler_params=_cp(("arbitrary",)), name=name)(*operands)


def _peer_copies(n, src_of, dst_of, src_r, land_r, ssem, rsem):
    me = _my_index()
    out = []
    for r, peer in _relations():
        p_idx = 4 * peer[0] + 2 * peer[1] + peer[2]
        for a in range(n):
            def copy(src_idx, dst_idx, a=a, r=r, peer=peer):
                sem = a * (N_DEV - 1) + r - 1
                return pltpu.make_async_remote_copy(
                    src_ref=src_of(src_r, land_r, a, src_idx), dst_ref=dst_of(land_r, a, dst_idx),
                    send_sem=ssem.at[sem], recv_sem=rsem.at[sem], device_id=peer, device_id_type=MESH)
            out.append((functools.partial(copy, p_idx, me), functools.partial(copy, me, p_idx)))
    return out


def _exchange_start(srcs, lands, maps, after, *, name):
    n, ns = len(lands), len(srcs)
    src_of, dst_of = maps

    def body(*refs):
        src_r, land_r = refs[:ns], refs[ns:ns + n]
        ssem, rsem, token = refs[ns + n + 1], refs[ns + n + 2], refs[-1]
        for send, _ in _peer_copies(n, src_of, dst_of, src_r, land_r, ssem, rsem):
            send().start()
        token[...] = jnp.zeros_like(token)

    flight = list(srcs) + list(lands)
    outs = pl.pallas_call(
        body, name=name,
        out_shape=(pltpu.SemaphoreType.DMA((n * (N_DEV - 1),)), pltpu.SemaphoreType.DMA((n * (N_DEV - 1),)),
                   *[pltpu.HBM(a.shape, a.dtype) for a in flight], SDS((8, LANE), F32)),
        in_specs=[HBM_SPEC] * (ns + n) + [ANY],
        out_specs=(SEM_SPEC, SEM_SPEC, *[HBM_SPEC] * (ns + n), pl.BlockSpec(memory_space=pltpu.VMEM)),
        input_output_aliases={i: 2 + i for i in range(ns + n)},
        compiler_params=pltpu.CompilerParams(has_side_effects=_DATAFLOW),
    )(*[pltpu.with_memory_space_constraint(a, pltpu.HBM) for a in flight], after)
    return (outs[0], outs[1], list(outs[2:2 + ns + n]), ns), outs[-1]


def _exchange_wait(handle, maps, after, *, name):
    ssem, rsem, flight, ns = handle
    n = len(flight) - ns
    src_of, dst_of = maps

    def body(*refs):
        src_r, land_r, ssem_r, rsem_r = refs[:ns], refs[ns:ns + n], refs[ns + n], refs[ns + n + 1]
        for send, arrival in _peer_copies(n, src_of, dst_of, src_r, land_r, ssem_r, rsem_r):
            send().wait_send()
            arrival().wait_recv()

    outs = pl.pallas_call(
        body, name=name, out_shape=[pltpu.HBM(a.shape, a.dtype) for a in flight],
        in_specs=[HBM_SPEC] * (ns + n) + [SEM_SPEC, SEM_SPEC, ANY], out_specs=[HBM_SPEC] * (ns + n),
        input_output_aliases={i: i for i in range(ns + n)},
        compiler_params=pltpu.CompilerParams(has_side_effects=_DATAFLOW),
    )(*flight, ssem, rsem, after)
    return list(outs[ns:])


_SIBLING = 1
_ICI = (2, 4, 6)


def _rel_peer(r):
    x, y, c = lax.axis_index("x"), lax.axis_index("y"), lax.axis_index("c")
    peer = (x ^ ((r >> 2) & 1), y ^ ((r >> 1) & 1), c ^ (r & 1))
    return peer, 4 * peer[0] + 2 * peer[1] + peer[2]


def _rcopy(ref, ssem, rsem, peer):
    return pltpu.make_async_remote_copy(src_ref=ref, dst_ref=ref, send_sem=ssem, recv_sem=rsem, device_id=peer,
                                        device_id_type=MESH)


def _gather2_start(lands, kinds, after, *, name):
    n = len(lands)

    def body(*refs):
        land_r, (send1, recv_sib, recv_ici), token = refs[:n], refs[n + 1:n + 4], refs[-1]
        me = _my_index()
        for a in range(n):
            own = _slab(kinds[a], land_r[a], me)
            for j, r in enumerate((_SIBLING,) + _ICI):
                rsem = recv_sib.at[a] if r == _SIBLING else recv_ici.at[3 * a + j - 1]
                _rcopy(own, send1.at[4 * a + j], rsem, _rel_peer(r)[0]).start()
        token[...] = jnp.zeros_like(token)

    sems = [pltpu.SemaphoreType.DMA((4 * n,)), pltpu.SemaphoreType.DMA((n,)), pltpu.SemaphoreType.DMA((3 * n,))]
    outs = pl.pallas_call(
        body, name=name, out_shape=(*sems, *[pltpu.HBM(a.shape, a.dtype) for a in lands], SDS((8, LANE), F32)),
        in_specs=[HBM_SPEC] * n + [ANY],
        out_specs=(SEM_SPEC,) * 3 + (HBM_SPEC,) * n + (pl.BlockSpec(memory_space=pltpu.VMEM),),
        input_output_aliases={i: 3 + i for i in range(n)},
        compiler_params=pltpu.CompilerParams(has_side_effects=_DATAFLOW),
    )(*[pltpu.with_memory_space_constraint(a, pltpu.HBM) for a in lands], after)
    return dict(send1=outs[0], recv_sib=outs[1], recv_ici=outs[2], lands=list(outs[3:3 + n])), outs[-1]


def _gather2_forward(handle, kinds, after, *, name):
    lands = handle["lands"]
    n = len(lands)

    def body(*refs):
        land_r, recv_ici, (fwd_send, fwd_recv), token = refs[:n], refs[n], refs[n + 2:n + 4], refs[-1]
        sibling = _rel_peer(_SIBLING)[0]
        for a in range(n):
            for j, r in enumerate(_ICI):
                got = _slab(kinds[a], land_r[a], _rel_peer(r)[1])
                _rcopy(got, fwd_send.at[3 * a + j], recv_ici.at[3 * a + j], sibling).wait_recv()
                _rcopy(got, fwd_send.at[3 * a + j], fwd_recv.at[3 * a + j], sibling).start()
        token[...] = jnp.zeros_like(token)

    sems = [pltpu.SemaphoreType.DMA((3 * n,)), pltpu.SemaphoreType.DMA((3 * n,))]
    outs = pl.pallas_call(
        body, name=name, out_shape=(*sems, *[pltpu.HBM(a.shape, a.dtype) for a in lands], SDS((8, LANE), F32)),
        in_specs=[HBM_SPEC] * n + [SEM_SPEC, ANY],
        out_specs=(SEM_SPEC,) * 2 + (HBM_SPEC,) * n + (pl.BlockSpec(memory_space=pltpu.VMEM),),
        input_output_aliases={i: 2 + i for i in range(n)},
        compiler_params=pltpu.CompilerParams(has_side_effects=_DATAFLOW),
    )(*lands, handle["recv_ici"], after)
    return dict(handle, fwd_send=outs[0], fwd_recv=outs[1], lands=list(outs[2:2 + n])), outs[-1]


def _gather2_wait(handle, kinds, after, *, name):
    lands = handle["lands"]
    n = len(lands)

    def body(*refs):
        land_r, (send1, recv_sib, fwd_send, fwd_recv) = refs[:n], refs[n:n + 4]
        me = _my_index()
        sibling, sib_idx = _rel_peer(_SIBLING)
        for a in range(n):
            own = _slab(kinds[a], land_r[a], me)
            for j, r in enumerate((_SIBLING,) + _ICI):
                _rcopy(own, send1.at[4 * a + j], recv_sib.at[a], _rel_peer(r)[0]).wait_send()
            theirs = _slab(kinds[a], land_r[a], sib_idx)
            _rcopy(theirs, send1.at[4 * a], recv_sib.at[a], sibling).wait_recv()
            for j, r in enumerate(_ICI):
                passed_on = _slab(kinds[a], land_r[a], _rel_peer(r)[1])
                _rcopy(passed_on, fwd_send.at[3 * a + j], fwd_recv.at[3 * a + j], sibling).wait_send()
                arrived = _slab(kinds[a], land_r[a], _rel_peer(r ^ _SIBLING)[1])
                _rcopy(arrived, fwd_send.at[3 * a + j], fwd_recv.at[3 * a + j], sibling).wait_recv()

    outs = pl.pallas_call(
        body, name=name, out_shape=[pltpu.HBM(a.shape, a.dtype) for a in lands],
        in_specs=[HBM_SPEC] * n + [SEM_SPEC] * 4 + [ANY], out_specs=[HBM_SPEC] * n,
        input_output_aliases={i: i for i in range(n)},
        compiler_params=pltpu.CompilerParams(has_side_effects=_DATAFLOW),
    )(*lands, handle["send1"], handle["recv_sib"], handle["fwd_send"], handle["fwd_recv"], after)
    return list(outs)


def _sum_slots(gathered, *, name):
    def body(g_ref, out_ref):
        total = g_ref[0]
        for dev in range(1, N_DEV):
            total = total + g_ref[dev]
        out_ref[...] = total

    return _pcall(body, out_shape=SDS(gathered.shape[1:], F32), compiler_params=_cp(), name=name)(gathered)


def _adam(g, w, m, v):
    nm = ADAM_B1 * m + (1.0 - ADAM_B1) * g
    nv = ADAM_B2 * v + (1.0 - ADAM_B2) * (g * g)
    m_hat = nm / (1.0 - ADAM_B1 ** ADAM_STEP)
    v_hat = nv / (1.0 - ADAM_B2 ** ADAM_STEP)
    return -ADAM_LR * (m_hat / (jnp.sqrt(v_hat) + ADAM_EPS) + ADAM_WD * w), nm, nv


def _adamw_staged(st0, st1, w, m, v, *, name):
    _, rows, cols = w.shape
    st_cols = st0.shape[2]
    tr = max(t for t in range(16, 129, 16) if rows % t == 0)
    nr = rows // tr

    def body(s0_ref, s1_ref, w_ref, m_ref, v_ref, g_ref, d_ref, nm_ref, nv_ref):
        for layer, s_ref in enumerate((s0_ref, s1_ref)):
            @pl.when(pl.program_id(0) == layer)
            def _(s_ref=s_ref):
                total = s_ref[0, :, 0:cols].astype(F32)
                for dev in range(1, N_DEV):
                    total = total + s_ref[dev, :, 0:cols].astype(F32)
                g_ref[0] = total

        d_ref[0], nm_ref[0], nv_ref[0] = _adam(g_ref[0], w_ref[0], m_ref[0], v_ref[0])

    st_spec = lambda layer: pl.BlockSpec(
        (N_DEV, tr, st_cols), lambda l, i: (0, jnp.where(l == layer, i, (nr - 1) * (1 - layer)), 0))
    par = pl.BlockSpec((1, tr, cols), lambda l, i: (l, i, 0))
    return _pcall(
        body, grid=(DEPTH, nr), in_specs=[st_spec(0), st_spec(1), par, par, par], out_specs=[par] * 4,
        out_shape=[SDS(w.shape, F32)] * 4,
        compiler_params=_cp(("arbitrary", "arbitrary")), name=name)(st0, st1, w, m, v)


def _adamw_small(g, w, m, v, *, name):
    def body(g_ref, w_ref, m_ref, v_ref, d_ref, nm_ref, nv_ref):
        d_ref[...], nm_ref[...], nv_ref[...] = _adam(g_ref[...], w_ref[...], m_ref[...], v_ref[...])

    return _pcall(body, out_shape=[SDS(g.shape, F32)] * 3, compiler_params=_cp(), name=name)(g, w, m, v)


def _pack_rows(arrays):
    flat = jnp.concatenate([a.reshape(-1).astype(F32) for a in arrays])
    rows = -(-flat.shape[0] // (8 * D)) * 8
    return jnp.pad(flat, (0, rows * D - flat.shape[0])).reshape(rows, D)


def _unpack_rows(pack, like):
    flat = pack.reshape(-1)
    out, at = [], 0
    for a in like:
        out.append(flat[at:at + a.size].reshape(a.shape))
        at += a.size
    return out


def _layer_fwd(x, h, mem, win, mixer_weights, ffn_weights, after_up, small, g_next, tag):
    proj = _mm(h, win, tm=1024, tn=1536, name=f"proj_{tag}")
    wkv, w4, cw_a, cw_b, cw_f = mixer_weights(proj)
    za = _bra_fwd(proj, cw_a, name=f"bra_fwd_{tag}")
    cb = _brb_conv_fwd(proj, cw_b, small["conv_b_bias"], name=f"brb_conv_fwd_{tag}")
    sb = _ln_silu_fwd(cb, small["ln_b_g"], small["ln_b_b"], name=f"ln_silu_fwd_{tag}")
    memn, kv = _kv_prep(mem, small["norm_mem_g"], wkv, name=f"kv_prep_{tag}")
    o = _attn_fwd(proj, kv, name=f"attn_fwd_{tag}")
    ya, yb, yc, mg, x1, h2 = _mix_out(x, za, sb, o, proj, w4, small["b_gate"], small["norm_ffn_g"],
                                      name=f"mix_out_{tag}")
    wup, wdn = ffn_weights(h2)
    u2 = _mm(h2, wup, tm=1024, tn=1536, name=f"up_{tag}")
    token = after_up(u2)
    act, c2 = _ffn_act(u2, cw_f if token is None else _behind(cw_f, token), name=f"ffn_act_{tag}")
    x2, h_next = _mm_res_norm(act, wdn, x1, g_next, name=f"down_{tag}")
    saved = dict(x=x, h=h, proj=proj, za=za, cb=cb, sb=sb, memn=memn, kv=kv, o=o, ya=ya, yb=yb, yc=yc,
                 mg=mg, x1=x1, h2=h2, u2=u2, c2=c2, act=act)
    return x2, h_next, (win, wup, wkv, w4, wdn, cw_a, cw_b, cw_f), saved


def _behind(operand, token):
    return operand + token[0:1, 0:1]


def _layer_bwd(dx2, dx2b, mem, wts, small, sv, start, tag):
    win, wup, wkv, w4, wdn, cw_a, cw_b, cw_f = wts
    dact = _mm(dx2b, wdn, tb=True, tm=1024, tn=768, name=f"d_act_{tag}")
    dwdn = _mm(sv["act"], dx2b, ta=True, tm=768, tn=1024, name=f"dw_down_{tag}")
    du2, dcw_f = _ffn_bwd(sv["u2"], sv["c2"], dact, cw_f, name=f"ffn_bwd_{tag}")
    dwup_t = _mm(du2, sv["h2"], ta=True, tm=C_UP_P, tn=1024, name=f"dw_up_{tag}")
    token = start(("wdn", "wupT"), (dwdn, dwup_t), f"ffn_{tag}")
    dx1, dx1b, dg_ffn = _mm_nt_normbwd(du2, wup, sv["x1"], dx2, _behind(small["norm_ffn_g"], token),
                                       name=f"d_h2_{tag}")

    dya, dyb, dyc, dza, dsb, do, dproj, dbg = _mix_bwd(dx1b, sv["ya"], sv["yb"], sv["yc"], sv["proj"], w4,
                                                      small["b_gate"], name=f"mix_bwd_{tag}")
    dw4 = jnp.stack([
        _mm(a, b, ta=True, tm=1024, tn=512, name=f"dw_{nm}_{tag}")
        for nm, a, b in (("a_out", sv["za"], dya), ("b_out", sv["sb"], dyb), ("att_out", sv["o"], dyc),
                         ("o", sv["mg"], dx1b))])
    dq, dk, dv = _attn_bwd(sv["proj"], sv["kv"], do, name=f"attn_bwd_{tag}")
    dwkv, dg_mem = _kv_bwd(mem, small["norm_mem_g"], sv["memn"], dk, dv, wkv, name=f"kv_bwd_{tag}")
    token = start(("w4", "wkv"), (dw4, dwkv), f"mix_{tag}")
    dproj, dcw_a = _bra_bwd(sv["proj"], dza, _behind(cw_a, token), dproj, name=f"bra_bwd_{tag}")
    dcb, ln_sums = _ln_silu_bwd(sv["cb"], dsb, small["ln_b_g"], small["ln_b_b"], name=f"ln_silu_bwd_{tag}")
    dproj, dcw_b = _brb_conv_bwd(sv["proj"], dcb, dq, cw_b, dproj, name=f"brb_conv_bwd_{tag}")
    dwin = _mm(sv["h"], dproj, ta=True, tm=1024, tn=768, name=f"dw_in_{tag}")
    token = start(("win",), (dwin,), f"in_{tag}")
    dx, dxb, dg_mix = _mm_nt_normbwd(dproj, win, sv["x"], dx1, _behind(small["norm_mix_g"], token),
                                     tk=4608, name=f"d_h_{tag}")

    small_grads = [dg_mix[0:1], dg_mem[0:1], dbg[0:1].reshape(3, D), ln_sums[2:3], ln_sums[0:1], ln_sums[1:2],
                   dg_ffn[0:1], dcw_a[0:K_A], dcw_b[0:K_B], dcw_f[0:K_F].reshape(K_F * 2 * FF_P // D, D)]
    return dx, dxb, small_grads, token


_SMALL_ROWS = (1, 1, 3, 1, 1, 1, 1, K_A, K_B, K_F * 2 * FF_P // D)
_CV_ROWS = 48


def kernel(x, mem, norm_mix_g, norm_mem_g, w_in, b_gate, conv_a_w, w_a_out, conv_b_w, conv_b_bias, ln_b_g, ln_b_b, w_b_out, w_kv, w_att_out, w_o, norm_ffn_g, w_up, conv_ffn_w, w_down, norm_final_g, loss_target, m_norm_mix_g, m_norm_mem_g, m_w_in, m_b_gate, m_conv_a_w, m_w_a_out, m_conv_b_w, m_conv_b_bias, m_ln_b_g, m_ln_b_b, m_w_b_out, m_w_kv, m_w_att_out, m_w_o, m_norm_ffn_g, m_w_up, m_conv_ffn_w, m_w_down, m_norm_final_g, v_norm_mix_g, v_norm_mem_g, v_w_in, v_b_gate, v_conv_a_w, v_w_a_out, v_conv_b_w, v_conv_b_bias, v_ln_b_g, v_ln_b_b, v_w_b_out, v_w_kv, v_w_att_out, v_w_o, v_norm_ffn_g, v_w_up, v_conv_ffn_w, v_w_down, v_norm_final_g):
    me = _my_index()
    me_arr = me.astype(jnp.int32).reshape(1)
    x0, mem0, tgt = x.reshape(x.shape[1:]), mem.reshape(mem.shape[1:]), loss_target.reshape(x.shape[1:])
    up_pad = ((0, 0), (0, 0), (0, C_UP_P - C_UP))

    ag_groups = (("win",), ("wkv", "w4", "cv"), ("wup", "wdn"))
    kinds = ag_groups[0] + ag_groups[1] + ag_groups[2]
    smalls, ag_handles = [], []
    token = jnp.zeros((8, LANE), F32)
    for l in range(DEPTH):
        cv = jnp.zeros((_CV_ROWS, C_UP_P), F32)
        cv = cv.at[0:K_F, 0:C_UP].set(conv_ffn_w[l]).at[3:3 + K_A, 0:R_O].set(conv_a_w[l])
        cv = cv.at[8:8 + K_B, 0:R_O].set(conv_b_w[l])
        shards = dict(
            win=w_in[l], wup=jnp.pad(w_up[l], up_pad[1:]), wkv=w_kv[l],
            w4=jnp.stack([w_a_out[l], w_b_out[l], w_att_out[l], w_o[l]]), wdn=w_down[l], cv=cv)
        whole = dict({k: SDS(*_WHOLE[k]) for k in kinds if k != "cv"}, cv=SDS((N_DEV,) + cv.shape, F32))
        lands = {k: _place_own(k, shards[k], whole[k], True, me_arr, name=f"ag_own_{k}_l{l}") for k in kinds}
        per_layer = []
        for g, grp in enumerate(ag_groups):
            handle, token = _gather2_start([lands[k] for k in grp], grp, token, name=f"ag_start_l{l}_g{g}")
            per_layer.append(handle)
        ag_handles.append(per_layer)
        smalls.append(dict(
            norm_mix_g=norm_mix_g[l][None], norm_mem_g=norm_mem_g[l][None], b_gate=b_gate[l][None],
            conv_b_bias=conv_b_bias[l][None], ln_b_g=ln_b_g[l][None], ln_b_b=ln_b_b[l][None],
            norm_ffn_g=norm_ffn_g[l][None]))

    def forward_group(l, g, after):
        ag_handles[l][g], tok = _gather2_forward(ag_handles[l][g], ag_groups[g], after, name=f"ag_forward_l{l}_g{g}")
        return tok

    def group_of(l, g):
        def wait(after):
            if l == 0:
                after = forward_group(0, g, after)
            return _gather2_wait(ag_handles[l][g], ag_groups[g], after, name=f"ag_wait_l{l}_g{g}")
        return wait

    def mixer_weights(l):
        def wait(after):
            if l > 0:
                after = forward_group(l, 2, after)
            wkv, w4, cvg = group_of(l, 1)(after)
            cw_f = jnp.stack([cvg[d, 0:K_F, :] for d in UP_ORDER], axis=1).reshape(K_F, 2 * FF_P)
            cw_a = cvg[:, 3:3 + K_A, 0:R_O].transpose(1, 0, 2).reshape(K_A, D)
            cw_b = cvg[:, 8:8 + K_B, 0:R_O].transpose(1, 0, 2).reshape(K_B, D)
            return (wkv, w4, jnp.pad(cw_a, ((0, 8 - K_A), (0, 0))), jnp.pad(cw_b, ((0, 32 - K_B), (0, 0))),
                    jnp.pad(cw_f, ((0, 8 - K_F), (0, 0))))
        return wait

    wts, saved = [], []
    xs = x0
    h = _rms_fwd(xs, smalls[0]["norm_mix_g"], name="rms_fwd")
    behind = forward_group(0, 0, token)

    def next_layer_forwarding(l):
        def hook(after):
            return None if l + 1 == DEPTH else forward_group(l + 1, 0, after)
        return hook

    for l in range(DEPTH):
        g_next = smalls[l + 1]["norm_mix_g"] if l + 1 < DEPTH else norm_final_g[None]
        if l > 0:
            behind = forward_group(l, 1, behind)
        (win,) = _gather2_wait(ag_handles[l][0], ag_groups[0], behind, name=f"ag_wait_l{l}_g0")
        xs, h, w_l, sv = _layer_fwd(xs, h, mem0, win, mixer_weights(l), group_of(l, 2), next_layer_forwarding(l),
                                    smalls[l], g_next, f"l{l}")
        behind = h
        wts.append(w_l)
        saved.append(sv)
    dx, dxb, head_sums = _loss_head(xs, tgt, norm_final_g[None], name="loss_head")

    rs_handles = []
    small_grads = [None] * DEPTH

    def start_scatter(grp, arrays, name):
        maps = _scatter_maps(grp)
        lands = [_place_own(k, a, SDS((N_DEV,) + _SHARD[k], BF), False, me_arr, name=f"rs_own_{k}_{name}")
                 for k, a in zip(grp, arrays)]
        handle, tok = _exchange_start(list(arrays), lands, maps, rs_handles[-1][2] if rs_handles else head_sums,
                                      name=f"rs_start_{name}")
        rs_handles.append((grp, handle, tok, name))
        return tok

    for l in reversed(range(DEPTH)):
        dx, dxb, small_grads[l], token = _layer_bwd(dx, dxb, mem0, wts[l], smalls[l], saved[l], start_scatter,
                                                    f"l{l}")

    pack = jnp.concatenate(small_grads[0] + small_grads[1] + [head_sums[1:2], head_sums[0:1]], axis=0)
    pack = jnp.pad(pack, ((0, -pack.shape[0] % 8), (0, 0)))
    small_maps = (lambda srcs, lands, a, idx: srcs[a]), (lambda lands, a, idx: lands[a].at[idx])
    small_land = _place_own("cv", pack, SDS((N_DEV,) + pack.shape, F32), True, me_arr, name="small_own")
    small_handle, small_token = _exchange_start([pack], [small_land], small_maps, dx, name="small_start")

    staged = [dict() for _ in range(DEPTH)]
    for grp, handle, _, name in rs_handles[:-1]:
        staged[int(name[-1])].update(zip(grp, _exchange_wait(handle, _scatter_maps(grp), small_token,
                                                             name=f"rs_wait_{name}")))

    def big_update(kind, w, m, v, name):
        return _adamw_staged(staged[0][kind], staged[1][kind], w, m, v, name=name)

    r_up = [jnp.swapaxes(a, 1, 2) for a in big_update(
        "wupT", jnp.swapaxes(w_up, 1, 2), jnp.swapaxes(m_w_up, 1, 2), jnp.swapaxes(v_w_up, 1, 2), "adamw_w_up")]
    r_kv = big_update("wkv", w_kv, m_w_kv, v_w_kv, "adamw_w_kv")
    r_dn = big_update("wdn", w_down, m_w_down, v_w_down, "adamw_w_down")

    def four(a, b, c, d_):
        return jnp.stack([a, b, c, d_], axis=1).reshape(DEPTH, 4 * R_O, D)

    r_4 = _adamw_staged(
        staged[0]["w4"].reshape(N_DEV, 4 * R_O, D), staged[1]["w4"].reshape(N_DEV, 4 * R_O, D),
        four(w_a_out, w_b_out, w_att_out, w_o), four(m_w_a_out, m_w_b_out, m_w_att_out, m_w_o),
        four(v_w_a_out, v_w_b_out, v_w_att_out, v_w_o), name="adamw_w_out")
    grp, handle, _, name = rs_handles[-1]
    staged[0].update(zip(grp, _exchange_wait(handle, _scatter_maps(grp), r_4[0], name=f"rs_wait_{name}")))
    r_in = big_update("win", w_in, m_w_in, v_w_in, "adamw_w_in")
    r_a, r_b, r_att, r_o = ([a.reshape(DEPTH, 4, R_O, D)[:, j] for a in r_4] for j in range(4))

    (gathered,) = _exchange_wait(small_handle, small_maps, r_in[0], name="small_wait")
    total = _sum_slots(gathered, name="small_sum")
    per_layer = sum(_SMALL_ROWS)
    parts = []
    for l in range(DEPTH):
        at, one = l * per_layer, []
        for rows in _SMALL_ROWS:
            one.append(total[at:at + rows])
            at += rows
        parts.append(one)
    g_final = total[DEPTH * per_layer]
    loss = 0.5 / D * jnp.sum(total[DEPTH * per_layer + 1])

    def both(i):
        return jnp.stack([parts[0][i], parts[1][i]])

    g_norm_mix, g_norm_mem = both(0)[:, 0], both(1)[:, 0]
    g_b_gate = both(2).reshape(DEPTH, 3 * D)
    g_cbias, g_lng, g_lnb, g_norm_ffn = both(3)[:, 0], both(4)[:, 0], both(5)[:, 0], both(6)[:, 0]
    g_conv_a = lax.dynamic_slice_in_dim(both(7), me * R_O, R_O, axis=2)
    g_conv_b = lax.dynamic_slice_in_dim(both(8), me * R_O, R_O, axis=2)
    g_conv_f = lax.dynamic_slice_in_dim(both(9).reshape(DEPTH, K_F, 2 * FF_P), _up_slot(me) * C_UP_P, C_UP, axis=2)

    small_g = [g_norm_mix, g_norm_mem, g_b_gate, g_conv_a, g_conv_b, g_cbias, g_lng, g_lnb, g_norm_ffn, g_conv_f,
               g_final]
    small_w = [norm_mix_g, norm_mem_g, b_gate, conv_a_w, conv_b_w, conv_b_bias, ln_b_g, ln_b_b, norm_ffn_g,
               conv_ffn_w, norm_final_g]
    small_m = [m_norm_mix_g, m_norm_mem_g, m_b_gate, m_conv_a_w, m_conv_b_w, m_conv_b_bias, m_ln_b_g, m_ln_b_b,
               m_norm_ffn_g, m_conv_ffn_w, m_norm_final_g]
    small_v = [v_norm_mix_g, v_norm_mem_g, v_b_gate, v_conv_a_w, v_conv_b_w, v_conv_b_bias, v_ln_b_g, v_ln_b_b,
               v_norm_ffn_g, v_conv_ffn_w, v_norm_final_g]
    upd = _adamw_small(_pack_rows(small_g), _pack_rows(small_w), _pack_rows(small_m), _pack_rows(small_v),
                       name="adamw_small")
    s_d, s_m, s_v = (_unpack_rows(p, small_w) for p in upd)
    (d_norm_mix, d_norm_mem, d_b_gate, d_conv_a, d_conv_b, d_cbias, d_lng, d_lnb, d_norm_ffn, d_conv_f,
     d_final) = s_d
    (nm_norm_mix, nm_norm_mem, nm_b_gate, nm_conv_a, nm_conv_b, nm_cbias, nm_lng, nm_lnb, nm_norm_ffn, nm_conv_f,
     nm_final) = s_m
    (nv_norm_mix, nv_norm_mem, nv_b_gate, nv_conv_a, nv_conv_b, nv_cbias, nv_lng, nv_lnb, nv_norm_ffn, nv_conv_f,
     nv_final) = s_v

    grads = [g_norm_mix, g_norm_mem, r_in[0], g_b_gate, g_conv_a, r_a[0], g_conv_b, g_cbias, g_lng, g_lnb, r_b[0],
             r_kv[0], r_att[0], r_o[0], g_norm_ffn, r_up[0], g_conv_f, r_dn[0], g_final]
    deltas = [d_norm_mix, d_norm_mem, r_in[1], d_b_gate, d_conv_a, r_a[1], d_conv_b, d_cbias, d_lng, d_lnb, r_b[1],
              r_kv[1], r_att[1], r_o[1], d_norm_ffn, r_up[1], d_conv_f, r_dn[1], d_final]
    new_m = [nm_norm_mix, nm_norm_mem, r_in[2], nm_b_gate, nm_conv_a, r_a[2], nm_conv_b, nm_cbias, nm_lng, nm_lnb,
             r_b[2], r_kv[2], r_att[2], r_o[2], nm_norm_ffn, r_up[2], nm_conv_f, r_dn[2], nm_final]
    new_v = [nv_norm_mix, nv_norm_mem, r_in[3], nv_b_gate, nv_conv_a, r_a[3], nv_conv_b, nv_cbias, nv_lng, nv_lnb,
             r_b[3], r_kv[3], r_att[3], r_o[3], nv_norm_ffn, r_up[3], nv_conv_f, r_dn[3], nv_final]
    return (loss, dx[None], *grads, *deltas, *new_m, *new_v)
```

```python
import functools

import jax
import jax.numpy as jnp
import numpy as np
from jax import lax
from jax.experimental import pallas as pl
from jax.experimental.pallas import tpu as pltpu

F32 = jnp.float32
BF = jnp.bfloat16
SDS = jax.ShapeDtypeStruct
MESH = pl.DeviceIdType.MESH
ANY = pl.BlockSpec(memory_space=pl.ANY)

N_DEV = 8
DEPTH = 2
D = 1024
N_HEADS = 4
HEAD = D // N_HEADS
D_FF = 2816
K_A, K_B, K_F = 3, 31, 3
NORM_EPS = 1e-6

C_IN = 9 * D // N_DEV
C_KV = 2 * D // N_DEV
C_UP = 2 * D_FF // N_DEV
LANE = 128
C_UP_P = -(-C_UP // LANE) * LANE
FF_P = 4 * C_UP_P
R_O = D // N_DEV
R_DN = D_FF // N_DEV

VMEM_LIMIT = 56 * 1024 * 1024
TM = 512
TR = 512
SUB = 128
H_S, H_L = 16, 32

ADAM_LR, ADAM_B1, ADAM_B2, ADAM_EPS, ADAM_WD, ADAM_STEP = 0.001, 0.9, 0.999, 1e-08, 0.01, 10

UP_ORDER = (0, 4, 1, 5, 2, 6, 3, 7)


def _pcall(body, **kw):
    return pl.pallas_call(body, **kw)


def _cp(sem=None, **kw):
    return pltpu.CompilerParams(dimension_semantics=sem, vmem_limit_bytes=VMEM_LIMIT, **kw)


def _dot(a, b):
    return jnp.dot(a, b, preferred_element_type=F32)


def _dot_nt(a, b):
    return lax.dot_general(a, b, (((1,), (1,)), ((), ())), preferred_element_type=F32)


def _dot_tn(a, b):
    return lax.dot_general(a, b, (((0,), (0,)), ((), ())), preferred_element_type=F32)


def _sigmoid(z):
    return 1.0 / (1.0 + jnp.exp(-z))


def _rms(xv):
    return lax.rsqrt(jnp.mean(xv * xv, axis=-1, keepdims=True) + NORM_EPS)


def _up_slot(idx):
    return jnp.where(idx < 4, 2 * idx, 2 * (idx - 4) + 1)


def _dn_row(idx):
    return C_UP_P * (idx // 2) + R_DN * (idx % 2)


def _mm(a, b, *, ta=False, tb=False, out_dtype=BF, tm=TM, tn=512, tk=None, name):
    m, k_dim = (a.shape[1], a.shape[0]) if ta else a.shape
    n = b.shape[0] if tb else b.shape[1]
    tm, tn = min(tm, m), min(tn, n)
    tk = k_dim if tk is None else min(tk, k_dim)
    nk = k_dim // tk
    assert m % tm == 0 and n % tn == 0 and k_dim % tk == 0
    dims = (((0 if ta else 1,), (1 if tb else 0,)), ((), ()))

    def body(a_ref, b_ref, o_ref, *scratch):
        part = lax.dot_general(a_ref[...], b_ref[...], dims, preferred_element_type=F32)
        if nk == 1:
            o_ref[...] = part.astype(o_ref.dtype)
            return
        acc = scratch[0]
        k = pl.program_id(2)

        @pl.when(k == 0)
        def _():
            acc[...] = part

        @pl.when(k > 0)
        def _():
            acc[...] += part

        @pl.when(k == nk - 1)
        def _():
            o_ref[...] = acc[...].astype(o_ref.dtype)

    a_spec = pl.BlockSpec((tk, tm), lambda i, j, k: (k, i)) if ta else pl.BlockSpec((tm, tk), lambda i, j, k: (i, k))
    b_spec = pl.BlockSpec((tn, tk), lambda i, j, k: (j, k)) if tb else pl.BlockSpec((tk, tn), lambda i, j, k: (k, j))
    return _pcall(
        body, grid=(m // tm, n // tn, nk), in_specs=[a_spec, b_spec],
        out_specs=pl.BlockSpec((tm, tn), lambda i, j, k: (i, j)),
        out_shape=SDS((m, n), out_dtype),
        scratch_shapes=[pltpu.VMEM((tm, tn), F32)] if nk > 1 else [],
        compiler_params=_cp(("parallel", "parallel", "arbitrary")), name=name)(a, b)


def _mm_res_norm(a, w, x, g, *, name):
    s, k_dim = a.shape
    tm = min(TM, s)

    def body(a_ref, w_ref, x_ref, g_ref, xo_ref, h_ref):
        xo = x_ref[...] + _dot(a_ref[...], w_ref[...])
        xo_ref[...] = xo
        h_ref[...] = ((xo * _rms(xo)) * g_ref[...]).astype(BF)

    return _pcall(
        body, grid=(s // tm,),
        in_specs=[pl.BlockSpec((tm, k_dim), lambda i: (i, 0)),
                  pl.BlockSpec((k_dim, D), lambda i: (0, 0), pipeline_mode=pl.Buffered(1)),
                  pl.BlockSpec((tm, D), lambda i: (i, 0)), pl.BlockSpec((1, D), lambda i: (0, 0))],
        out_specs=[pl.BlockSpec((tm, D), lambda i: (i, 0))] * 2,
        out_shape=[SDS((s, D), F32), SDS((s, D), BF)],
        compiler_params=_cp(("parallel",)), name=name)(a, w, x, g)


def _mm_nt_normbwd(da, w, x, dres, g, *, tk=None, name):
    s, k_dim = da.shape
    tm = min(TM, s)
    tk = k_dim if tk is None else tk
    nk = k_dim // tk
    assert k_dim % tk == 0

    def body(da_ref, w_ref, x_ref, dres_ref, g_ref, dx_ref, dxb_ref, dg_ref, *scratch):
        i, k = pl.program_id(0), pl.program_id(1)
        part = _dot_nt(da_ref[...], w_ref[...])
        if nk > 1:
            acc = scratch[0]

            @pl.when(k == 0)
            def _():
                acc[...] = part

            @pl.when(k > 0)
            def _():
                acc[...] += part

        @pl.when((i == 0) & (k == 0))
        def _():
            dg_ref[...] = jnp.zeros_like(dg_ref)

        @pl.when(k == nk - 1)
        def _():
            dh = acc[...] if nk > 1 else part
            xv = x_ref[...]
            r = _rms(xv)
            xn = xv * r
            dg_ref[0:1, :] += jnp.sum(dh * xn, axis=0, keepdims=True)
            dxn = dh * g_ref[...]
            dx = dres_ref[...] + r * (dxn - xn * jnp.mean(dxn * xn, axis=-1, keepdims=True))
            dx_ref[...] = dx
            dxb_ref[...] = dx.astype(BF)

    row = lambda i, k: (i, 0)
    w_spec = (pl.BlockSpec((D, tk), lambda i, k: (0, k)) if nk > 1 else
              pl.BlockSpec((D, tk), lambda i, k: (0, 0), pipeline_mode=pl.Buffered(1)))
    return _pcall(
        body, grid=(s // tm, nk),
        in_specs=[pl.BlockSpec((tm, tk), lambda i, k: (i, k)), w_spec,
                  pl.BlockSpec((tm, D), row), pl.BlockSpec((tm, D), row), pl.BlockSpec((1, D), lambda i, k: (0, 0))],
        out_specs=[pl.BlockSpec((tm, D), row), pl.BlockSpec((tm, D), row), pl.BlockSpec((8, D), lambda i, k: (0, 0))],
        out_shape=[SDS((s, D), F32), SDS((s, D), BF), SDS((8, D), F32)],
        scratch_shapes=[pltpu.VMEM((tm, D), F32)] if nk > 1 else [],
        compiler_params=_cp(("arbitrary", "arbitrary")), name=name)(da, w, x, dres, g)


def _rms_fwd(x, g, *, name):
    s = x.shape[0]
    tm = min(TM, s)

    def body(x_ref, g_ref, h_ref):
        xv = x_ref[...]
        h_ref[...] = ((xv * _rms(xv)) * g_ref[...]).astype(BF)

    return _pcall(
        body, grid=(s // tm,),
        in_specs=[pl.BlockSpec((tm, D), lambda i: (i, 0)), pl.BlockSpec((1, D), lambda i: (0, 0))],
        out_specs=pl.BlockSpec((tm, D), lambda i: (i, 0)), out_shape=SDS((s, D), BF),
        compiler_params=_cp(("parallel",)), name=name)(x, g)


def _loss_head(x, tgt, g, *, name):
    s = x.shape[0]
    tm = min(TM, s)

    def body(x_ref, t_ref, g_ref, dx_ref, dxb_ref, sums_ref):
        @pl.when(pl.program_id(0) == 0)
        def _():
            sums_ref[...] = jnp.zeros_like(sums_ref)

        xv = x_ref[...]
        r = _rms(xv)
        xn = xv * r
        diff = xn * g_ref[...] - t_ref[...]
        sums_ref[0:1, :] += jnp.sum(diff * diff, axis=0, keepdims=True)
        dy = diff * (1.0 / D)
        sums_ref[1:2, :] += jnp.sum(dy * xn, axis=0, keepdims=True)
        dxn = dy * g_ref[...]
        dx = r * (dxn - xn * jnp.mean(dxn * xn, axis=-1, keepdims=True))
        dx_ref[...] = dx
        dxb_ref[...] = dx.astype(BF)

    row = lambda i: (i, 0)
    return _pcall(
        body, grid=(s // tm,),
        in_specs=[pl.BlockSpec((tm, D), row), pl.BlockSpec((tm, D), row), pl.BlockSpec((1, D), lambda i: (0, 0))],
        out_specs=[pl.BlockSpec((tm, D), row), pl.BlockSpec((tm, D), row), pl.BlockSpec((8, D), lambda i: (0, 0))],
        out_shape=[SDS((s, D), F32), SDS((s, D), BF), SDS((8, D), F32)],
        compiler_params=_cp(("arbitrary",)), name=name)(x, tgt, g)


def _halo_before(i, tr, h):
    return jnp.maximum(i * (tr // h) - 1, 0)


def _halo_after(i, tr, h, s):
    return jnp.minimum((i + 1) * (tr // h), s // h - 1)


def _taps(buf, w_ref, sl, k_w, base, rows):
    acc = None
    for k in range(k_w):
        t = w_ref[k:k + 1, sl] * buf[base + k:base + k + rows, sl]
        acc = t if acc is None else acc + t
    return acc


def _taps_rev(buf, w_ref, sl, k_w, base, rows):
    acc = None
    for k in range(k_w):
        t = w_ref[k:k + 1, sl] * buf[base + k_w - 1 - k:base + k_w - 1 - k + rows, sl]
        acc = t if acc is None else acc + t
    return acc


def _tap_grads(dw_ref, dc, buf, sl, k_w, base, rows):
    for k in range(k_w):
        dw_ref[k:k + 1, sl] += jnp.sum(dc * buf[base + k:base + k + rows, sl], axis=0, keepdims=True)


def _bra_fwd(proj, cw, *, name):
    s = proj.shape[0]
    tr, h = min(TR, s), H_S
    sub = min(SUB, tr)

    def body(cur, halo, w_ref, za_ref, cvb):
        i = pl.program_id(0)
        hv = halo[:, D:2 * D].astype(F32) * halo[:, 2 * D:3 * D].astype(F32)
        cvb[0:h, :] = jnp.where(i == 0, 0.0, hv)
        cvb[h:h + tr, :] = cur[:, D:2 * D].astype(F32) * cur[:, 2 * D:3 * D].astype(F32)
        for c in range(D // LANE):
            sl = slice(LANE * c, LANE * c + LANE)
            ca = _taps(cvb, w_ref, sl, K_A, h - (K_A - 1), tr)
            za_ref[:, sl] = (cur[:, sl].astype(F32) * ca).astype(BF)

    return _pcall(
        body, grid=(s // tr,),
        in_specs=[pl.BlockSpec((tr, 3 * D), lambda i: (i, 0)),
                  pl.BlockSpec((h, 3 * D), lambda i: (_halo_before(i, tr, h), 0)),
                  pl.BlockSpec((8, D), lambda i: (0, 0))],
        out_specs=pl.BlockSpec((tr, D), lambda i: (i, 0)), out_shape=SDS((s, D), BF),
        scratch_shapes=[pltpu.VMEM((h + tr, D), F32)],
        compiler_params=_cp(("parallel",)), name=name)(proj, proj, cw)


def _bra_bwd(proj, dza, cw, dproj, *, name):
    s = proj.shape[0]
    tr, h = min(TR, s), H_S
    sub = min(SUB, tr)
    n = s // tr

    def body(before, cur, after, dz_cur, dz_after, w_ref, dproj_in, da_ref, dw_ref, cvb, dcab):
        del dproj_in
        i = pl.program_id(0)

        @pl.when(i == 0)
        def _():
            dw_ref[...] = jnp.zeros_like(dw_ref)

        first, last = i == 0, i == n - 1
        cvb[0:h, :] = jnp.where(first, 0.0, before[:, D:2 * D].astype(F32) * before[:, 2 * D:3 * D].astype(F32))
        cvb[h:h + tr, :] = cur[:, D:2 * D].astype(F32) * cur[:, 2 * D:3 * D].astype(F32)
        dcab[0:tr, :] = dz_cur[...].astype(F32) * cur[:, 0:D].astype(F32)
        dcab[tr:tr + h, :] = jnp.where(last, 0.0, dz_after[...].astype(F32) * after[:, 0:D].astype(F32))
        for c in range(D // LANE):
            sl = slice(LANE * c, LANE * c + LANE)
            gl, vl = slice(D + LANE * c, D + LANE * c + LANE), slice(2 * D + LANE * c, 2 * D + LANE * c + LANE)
            for r0 in range(0, tr, sub):
                rows = slice(r0, r0 + sub)
                ca = _taps(cvb, w_ref, sl, K_A, h - (K_A - 1) + r0, sub)
                da_ref[rows, sl] = (dz_cur[rows, sl].astype(F32) * ca).astype(BF)
                dcv = _taps_rev(dcab, w_ref, sl, K_A, r0, sub)
                da_ref[rows, gl] = (dcv * cur[rows, vl].astype(F32)).astype(BF)
                da_ref[rows, vl] = (dcv * cur[rows, gl].astype(F32)).astype(BF)
                _tap_grads(dw_ref, dcab[rows, sl], cvb, sl, K_A, h - (K_A - 1) + r0, sub)

    return _pcall(
        body, grid=(n,),
        in_specs=[pl.BlockSpec((h, 3 * D), lambda i: (_halo_before(i, tr, h), 0)),
                  pl.BlockSpec((tr, 3 * D), lambda i: (i, 0)),
                  pl.BlockSpec((h, 3 * D), lambda i: (_halo_after(i, tr, h, s), 0)),
                  pl.BlockSpec((tr, D), lambda i: (i, 0)),
                  pl.BlockSpec((h, D), lambda i: (_halo_after(i, tr, h, s), 0)),
                  pl.BlockSpec((8, D), lambda i: (0, 0)), ANY],
        out_specs=[pl.BlockSpec((tr, 3 * D), lambda i: (i, 0)), pl.BlockSpec((8, D), lambda i: (0, 0))],
        out_shape=[SDS(dproj.shape, BF), SDS((8, D), F32)], input_output_aliases={6: 0},
        scratch_shapes=[pltpu.VMEM((h + tr, D), F32), pltpu.VMEM((tr + h, D), F32)],
        compiler_params=_cp(("arbitrary",)), name=name)(proj, proj, proj, dza, dza, cw, dproj)


_U_COL, _UG_COL = 3, 4


def _brb_conv_fwd(proj, cw, bias, *, name):
    s = proj.shape[0]
    tr, h = min(TR, s), H_L
    sub = min(SUB, tr)

    def body(u_cur, ug_cur, u_halo, ug_halo, w_ref, b_ref, cb_ref, glb, shifted):
        i = pl.program_id(0)
        glb[0:h, :] = jnp.where(i == 0, 0.0, u_halo[...].astype(F32) * _sigmoid(ug_halo[...].astype(F32)))
        glb[h:h + tr, :] = u_cur[...].astype(F32) * _sigmoid(ug_cur[...].astype(F32))
        for c in range(D // LANE):
            sl = slice(LANE * c, LANE * c + LANE)
            for r in range(1, 8):
                shifted[r] = glb[8 - r:8 - r + tr + 24, sl]
            for r0 in range(0, tr, sub):
                acc = None
                for k in range(K_B):
                    q, r = divmod(K_B - 1 - k, 8)
                    at = r0 - 8 * q
                    win = shifted[r, 24 + at:24 + at + sub, :] if r else glb[h + at:h + at + sub, sl]
                    term = w_ref[k:k + 1, sl] * win
                    acc = term if acc is None else acc + term
                cb_ref[r0:r0 + sub, sl] = (acc + b_ref[:, sl]).astype(BF)

    return _pcall(
        body, grid=(s // tr,),
        in_specs=[pl.BlockSpec((tr, D), lambda i: (i, _U_COL)), pl.BlockSpec((tr, D), lambda i: (i, _UG_COL)),
                  pl.BlockSpec((h, D), lambda i: (_halo_before(i, tr, h), _U_COL)),
                  pl.BlockSpec((h, D), lambda i: (_halo_before(i, tr, h), _UG_COL)),
                  pl.BlockSpec((32, D), lambda i: (0, 0)), pl.BlockSpec((1, D), lambda i: (0, 0))],
        out_specs=pl.BlockSpec((tr, D), lambda i: (i, 0)), out_shape=SDS((s, D), BF),
        scratch_shapes=[pltpu.VMEM((h + tr, D), F32), pltpu.VMEM((8, tr + 24, LANE), F32)],
        compiler_params=_cp(("parallel",)), name=name)(proj, proj, proj, proj, cw, bias)


def _brb_conv_bwd(proj, dcb, dq, cw, dproj, *, name):
    s = proj.shape[0]
    tr, h = min(TR, s), H_L
    sub = min(SUB, tr)
    n = s // tr
    nb = -(-(tr + 24) // sub)
    sel = _row_selector(sub, list(range(8)))

    def body(u_cur, ug_cur, d_cur, d_after, dq_ref, sel_ref, w_ref, dproj_in, db_ref, dw_ref, dcbb, shifted):
        del dproj_in
        i = pl.program_id(0)

        @pl.when(i == 0)
        def _():
            dw_ref[...] = jnp.zeros_like(dw_ref)

        db_ref[:, 2 * D:3 * D] = dq_ref[...]
        after = d_after[...]
        dcbb[0:tr, :] = d_cur[...]
        dcbb[tr:tr + h, :] = jnp.where(i == n - 1, jnp.zeros_like(after), after)
        dcbb[tr + h:(nb + 1) * sub, :] = jnp.zeros(((nb + 1) * sub - h - tr, D), BF)
        for c in range(D // LANE):
            sl = slice(LANE * c, LANE * c + LANE)
            for blk in range(nb):
                res = _dot(sel_ref[...], dcbb[blk * sub:(blk + 2) * sub, sl])
                for r in range(8):
                    shifted[r, blk * sub:(blk + 1) * sub, :] = res[r * sub:(r + 1) * sub]
            for r0 in range(0, tr, sub):
                u = u_cur[r0:r0 + sub, sl].astype(F32)
                sg = _sigmoid(ug_cur[r0:r0 + sub, sl].astype(F32))
                glu = u * sg
                dglu = None
                for k in range(K_B):
                    q, r = divmod(K_B - 1 - k, 8)
                    at = r0 + 8 * q
                    win = shifted[r, at:at + sub, :]
                    term = w_ref[k:k + 1, sl] * win
                    dglu = term if dglu is None else dglu + term
                    dw_ref[k:k + 1, sl] += jnp.sum(win * glu, axis=0, keepdims=True)
                db_ref[r0:r0 + sub, sl] = (dglu * sg).astype(BF)
                db_ref[r0:r0 + sub, D + LANE * c:D + LANE * c + LANE] = (dglu * u * sg * (1.0 - sg)).astype(BF)

    return _pcall(
        body, grid=(n,),
        in_specs=[pl.BlockSpec((tr, D), lambda i: (i, _U_COL)), pl.BlockSpec((tr, D), lambda i: (i, _UG_COL)),
                  pl.BlockSpec((tr, D), lambda i: (i, 0)),
                  pl.BlockSpec((h, D), lambda i: (_halo_after(i, tr, h, s), 0)),
                  pl.BlockSpec((tr, D), lambda i: (i, 0)),
                  pl.BlockSpec(sel.shape, lambda i: (0, 0)),
                  pl.BlockSpec((32, D), lambda i: (0, 0)), ANY],
        out_specs=[pl.BlockSpec((tr, 3 * D), lambda i: (i, 1)), pl.BlockSpec((32, D), lambda i: (0, 0))],
        out_shape=[SDS(dproj.shape, BF), SDS((32, D), F32)], input_output_aliases={7: 0},
        scratch_shapes=[pltpu.VMEM(((nb + 1) * sub, D), BF), pltpu.VMEM((8, nb * sub, LANE), F32)],
        compiler_params=_cp(("arbitrary",)), name=name)(proj, proj, dcb, dcb, dq, sel, cw, dproj)


def _ln_silu_fwd(cb, g, b, *, name):
    s = cb.shape[0]
    tm = min(TM, s)

    def body(cb_ref, g_ref, b_ref, sb_ref):
        z = cb_ref[...].astype(F32)
        zc = z - jnp.mean(z, axis=-1, keepdims=True)
        ln = (zc * lax.rsqrt(jnp.mean(zc * zc, axis=-1, keepdims=True) + NORM_EPS)) * g_ref[...] + b_ref[...]
        sb_ref[...] = (ln * _sigmoid(ln)).astype(BF)

    row = lambda i: (i, 0)
    vec = pl.BlockSpec((1, D), lambda i: (0, 0))
    return _pcall(
        body, grid=(s // tm,), in_specs=[pl.BlockSpec((tm, D), row), vec, vec],
        out_specs=pl.BlockSpec((tm, D), row), out_shape=SDS((s, D), BF),
        compiler_params=_cp(("parallel",)), name=name)(cb, g, b)


def _ln_silu_bwd(cb, dsb, g, b, *, name):
    s = cb.shape[0]
    tm = min(TM, s)

    def body(cb_ref, dsb_ref, g_ref, b_ref, dcb_ref, sums_ref):
        @pl.when(pl.program_id(0) == 0)
        def _():
            sums_ref[...] = jnp.zeros_like(sums_ref)

        z = cb_ref[...].astype(F32)
        zc = z - jnp.mean(z, axis=-1, keepdims=True)
        rstd = lax.rsqrt(jnp.mean(zc * zc, axis=-1, keepdims=True) + NORM_EPS)
        lnh = zc * rstd
        ln = lnh * g_ref[...] + b_ref[...]
        sg = _sigmoid(ln)
        dln = dsb_ref[...].astype(F32) * (sg * (1.0 + ln * (1.0 - sg)))
        sums_ref[0:1, :] += jnp.sum(dln * lnh, axis=0, keepdims=True)
        sums_ref[1:2, :] += jnp.sum(dln, axis=0, keepdims=True)
        dlnh = dln * g_ref[...]
        dz = rstd * (dlnh - jnp.mean(dlnh, axis=-1, keepdims=True)
                     - lnh * jnp.mean(dlnh * lnh, axis=-1, keepdims=True))
        sums_ref[2:3, :] += jnp.sum(dz, axis=0, keepdims=True)
        dcb_ref[...] = dz.astype(BF)

    row = lambda i: (i, 0)
    vec = pl.BlockSpec((1, D), lambda i: (0, 0))
    return _pcall(
        body, grid=(s // tm,), in_specs=[pl.BlockSpec((tm, D), row), pl.BlockSpec((tm, D), row), vec, vec],
        out_specs=[pl.BlockSpec((tm, D), row), pl.BlockSpec((8, D), lambda i: (0, 0))],
        out_shape=[SDS((s, D), BF), SDS((8, D), F32)],
        compiler_params=_cp(("arbitrary",)), name=name)(cb, dsb, g, b)


_Q_COL = 5 * D // HEAD


def _kv_prep(mem, g, wkv, *, name):
    m = mem.shape[0]

    def body(mem_ref, g_ref, w_ref, memn_ref, kv_ref):
        mv = mem_ref[...]
        memn = ((mv * _rms(mv)) * g_ref[...]).astype(BF)
        memn_ref[...] = memn
        for dev in range(N_DEV):
            kv_ref[:, dev * C_KV:(dev + 1) * C_KV] = _dot(memn, w_ref[dev]).astype(BF)

    return _pcall(body, out_shape=[SDS((m, D), BF), SDS((m, 2 * D), BF)],
                  compiler_params=_cp(), name=name)(mem, g, wkv)


def _softmax_rows(q, k):
    sc = _dot_nt(q, k) * (1.0 / (HEAD ** 0.5))
    e = jnp.exp(sc - jnp.max(sc, axis=-1, keepdims=True))
    return e / jnp.sum(e, axis=-1, keepdims=True)


def _attn_fwd(proj, kv, *, name):
    s, m = proj.shape[0], kv.shape[0]
    tm = min(TM, s)

    def body(q_ref, kv_ref, o_ref):
        for hd in range(N_HEADS):
            cols = slice(hd * HEAD, (hd + 1) * HEAD)
            p = _softmax_rows(q_ref[:, cols], kv_ref[:, cols])
            o_ref[:, cols] = _dot(p.astype(BF), kv_ref[:, D + hd * HEAD:D + (hd + 1) * HEAD]).astype(BF)

    return _pcall(
        body, grid=(s // tm,),
        in_specs=[pl.BlockSpec((tm, D), lambda i: (i, _Q_COL // N_HEADS)),
                  pl.BlockSpec((m, 2 * D), lambda i: (0, 0))],
        out_specs=pl.BlockSpec((tm, D), lambda i: (i, 0)), out_shape=SDS((s, D), BF),
        compiler_params=_cp(("parallel",)), name=name)(proj, kv)


def _attn_bwd(proj, kv, do, *, name):
    s, m = proj.shape[0], kv.shape[0]
    tm = min(TM, s)

    def body(q_ref, kv_ref, do_ref, dq_ref, dk_ref, dv_ref):
        @pl.when(pl.program_id(0) == 0)
        def _():
            dk_ref[...] = jnp.zeros_like(dk_ref)
            dv_ref[...] = jnp.zeros_like(dv_ref)

        for hd in range(N_HEADS):
            cols = slice(hd * HEAD, (hd + 1) * HEAD)
            q, k, dov = q_ref[:, cols], kv_ref[:, cols], do_ref[:, cols]
            p = _softmax_rows(q, k)
            dp = _dot_nt(dov, kv_ref[:, D + hd * HEAD:D + (hd + 1) * HEAD])
            dv_ref[:, cols] += _dot_tn(p.astype(BF), dov)
            ds = (p * (dp - jnp.sum(dp * p, axis=-1, keepdims=True)) * (1.0 / (HEAD ** 0.5))).astype(BF)
            dq_ref[:, cols] = _dot(ds, k).astype(BF)
            dk_ref[:, cols] += _dot_tn(ds, q)

    return _pcall(
        body, grid=(s // tm,),
        in_specs=[pl.BlockSpec((tm, D), lambda i: (i, _Q_COL // N_HEADS)),
                  pl.BlockSpec((m, 2 * D), lambda i: (0, 0)),
                  pl.BlockSpec((tm, D), lambda i: (i, 0))],
        out_specs=[pl.BlockSpec((tm, D), lambda i: (i, 0)),
                   pl.BlockSpec((m, D), lambda i: (0, 0)),
                   pl.BlockSpec((m, D), lambda i: (0, 0))],
        out_shape=[SDS((s, D), BF), SDS((m, D), F32), SDS((m, D), F32)],
        compiler_params=_cp(("arbitrary",)), name=name)(proj, kv, do)


def _kv_bwd(mem, g, memn, dk, dv, wkv, *, name):
    def body(mem_ref, g_ref, memn_ref, dk_ref, dv_ref, w_ref, dw_ref, dg_ref):
        memn = memn_ref[...]
        dmemn = None
        for dev in range(N_DEV):
            d_ref, col = (dk_ref, dev) if dev < N_HEADS else (dv_ref, dev - N_HEADS)
            dslab = d_ref[:, col * C_KV:(col + 1) * C_KV].astype(BF)
            dw_ref[dev] = _dot_tn(memn, dslab).astype(BF)
            part = _dot_nt(dslab, w_ref[dev])
            dmemn = part if dmemn is None else dmemn + part
        mv = mem_ref[...]
        dg_ref[...] = jnp.zeros_like(dg_ref)
        dg_ref[0:1, :] = jnp.sum(dmemn * (mv * _rms(mv)), axis=0, keepdims=True)

    assert C_KV == HEAD
    return _pcall(body, out_shape=[SDS((N_DEV, D, C_KV), BF), SDS((8, D), F32)],
                  compiler_params=_cp(), name=name)(mem, g, memn, dk, dv, wkv)


_TM_MIX = 512


def _mix_out(x, za, sb, o, proj, w4, bg, g_next, *, name):
    s = x.shape[0]
    tm = min(_TM_MIX, s)

    def body(x_ref, za_ref, sb_ref, o_ref, pg_ref, w4_ref, bg_ref, gn_ref,
             ya_ref, yb_ref, yc_ref, mg_ref, x1_ref, h_ref):
        ys = (_dot(za_ref[...], w4_ref[0]), _dot(sb_ref[...], w4_ref[1]), _dot(o_ref[...], w4_ref[2]))
        merged = None
        for j, (y, y_ref) in enumerate(zip(ys, (ya_ref, yb_ref, yc_ref))):
            y_ref[...] = y.astype(BF)
            gate = _sigmoid(pg_ref[:, j * D:(j + 1) * D].astype(F32) + bg_ref[:, j * D:(j + 1) * D])
            merged = gate * y if merged is None else merged + gate * y
        mg = merged.astype(BF)
        mg_ref[...] = mg
        x1 = x_ref[...] + _dot(mg, w4_ref[3])
        x1_ref[...] = x1
        h_ref[...] = ((x1 * _rms(x1)) * gn_ref[...]).astype(BF)

    row = lambda i: (i, 0)
    act = pl.BlockSpec((tm, D), row)
    return _pcall(
        body, grid=(s // tm,),
        in_specs=[act, act, act, act, pl.BlockSpec((tm, 3 * D), lambda i: (i, 2)),
                  pl.BlockSpec((4, D, D), lambda i: (0, 0, 0), pipeline_mode=pl.Buffered(1)), pl.BlockSpec((1, 3 * D), lambda i: (0, 0)),
                  pl.BlockSpec((1, D), lambda i: (0, 0))],
        out_specs=[act] * 6,
        out_shape=[SDS((s, D), BF)] * 4 + [SDS((s, D), F32), SDS((s, D), BF)],
        compiler_params=_cp(("parallel",)), name=name)(x, za, sb, o, proj, w4, bg, g_next)


def _mix_bwd(dxb, ya, yb, yc, proj, w4, bg, *, name):
    s = dxb.shape[0]
    tm = min(_TM_MIX, s)

    def body(dx_ref, ya_ref, yb_ref, yc_ref, pg_ref, w4_ref, bg_ref,
             dya_ref, dyb_ref, dyc_ref, dza_ref, dsb_ref, do_ref, dgt_ref, dbg_ref):
        @pl.when(pl.program_id(0) == 0)
        def _():
            dbg_ref[...] = jnp.zeros_like(dbg_ref)

        dm = _dot_nt(dx_ref[...], w4_ref[3])
        for j, (y_ref, dy_ref, din_ref) in enumerate(zip((ya_ref, yb_ref, yc_ref), (dya_ref, dyb_ref, dyc_ref),
                                                         (dza_ref, dsb_ref, do_ref))):
            cols = slice(j * D, (j + 1) * D)
            gate = _sigmoid(pg_ref[:, cols].astype(F32) + bg_ref[:, cols])
            dy = (dm * gate).astype(BF)
            dy_ref[...] = dy
            din_ref[...] = _dot_nt(dy, w4_ref[j]).astype(BF)
            dpre = dm * y_ref[...].astype(F32) * gate * (1.0 - gate)
            dgt_ref[:, cols] = dpre.astype(BF)
            dbg_ref[0:1, cols] += jnp.sum(dpre, axis=0, keepdims=True)

    row = lambda i: (i, 0)
    act = pl.BlockSpec((tm, D), row)
    return _pcall(
        body, grid=(s // tm,),
        in_specs=[act, act, act, act, pl.BlockSpec((tm, 3 * D), lambda i: (i, 2)),
                  pl.BlockSpec((4, D, D), lambda i: (0, 0, 0), pipeline_mode=pl.Buffered(1)), pl.BlockSpec((1, 3 * D), lambda i: (0, 0))],
        out_specs=[act] * 6 + [pl.BlockSpec((tm, 3 * D), lambda i: (i, 2)),
                               pl.BlockSpec((8, 3 * D), lambda i: (0, 0))],
        out_shape=[SDS((s, D), BF)] * 6 + [SDS((s, 9 * D), BF), SDS((8, 3 * D), F32)],
        compiler_params=_cp(("arbitrary",)), name=name)(dxb, ya, yb, yc, proj, w4, bg)


_PAIR = 2 * C_UP_P


def _row_selector(sub, first_cols):
    rows = np.arange(len(first_cols) * sub)
    col = np.asarray(first_cols)[rows // sub] + rows % sub
    return jnp.asarray(np.arange(2 * sub)[None, :] == col[:, None], BF)


def _ffn_act(u2, cw, *, name):
    s = u2.shape[0]
    tr = min(TR, s)
    sub = min(SUB, tr)
    sel = _row_selector(sub, [sub - (K_F - 1 - k) for k in range(K_F)])

    def body(cur, prev, sel_ref, w_ref, act_ref, c2_ref, xb, win):
        i = pl.program_id(1)
        before = prev[...]
        xb[0:sub, :] = jnp.where(i == 0, jnp.zeros_like(before), before)
        xb[sub:sub + tr, :] = cur[...]
        for r0 in range(0, tr, sub):
            win[...] = _dot(sel_ref[...], xb[r0:r0 + 2 * sub, :])
            for c in range(C_UP_P // LANE):
                gl = slice(LANE * c, LANE * c + LANE)
                ul = slice(C_UP_P + LANE * c, C_UP_P + LANE * c + LANE)
                gt = sum(w_ref[k:k + 1, gl] * win[k * sub:(k + 1) * sub, gl] for k in range(K_F))
                up = sum(w_ref[k:k + 1, ul] * win[k * sub:(k + 1) * sub, ul] for k in range(K_F))
                c2_ref[r0:r0 + sub, gl] = gt.astype(BF)
                c2_ref[r0:r0 + sub, ul] = up.astype(BF)
                act_ref[r0:r0 + sub, gl] = (gt * _sigmoid(gt) * up).astype(BF)

    return _pcall(
        body, grid=(4, s // tr),
        in_specs=[pl.BlockSpec((tr, _PAIR), lambda p, i: (i, p)),
                  pl.BlockSpec((sub, _PAIR), lambda p, i: (_halo_before(i, tr, sub), p)),
                  pl.BlockSpec(sel.shape, lambda p, i: (0, 0)),
                  pl.BlockSpec((8, _PAIR), lambda p, i: (0, p))],
        out_specs=[pl.BlockSpec((tr, C_UP_P), lambda p, i: (i, p)), pl.BlockSpec((tr, _PAIR), lambda p, i: (i, p))],
        out_shape=[SDS((s, FF_P), BF), SDS((s, 2 * FF_P), BF)],
        scratch_shapes=[pltpu.VMEM((sub + tr, _PAIR), BF), pltpu.VMEM((K_F * sub, _PAIR), F32)],
        compiler_params=_cp(("parallel", "parallel")), name=name)(u2, u2, sel, cw)


def _ffn_bwd(u2, c2, dact, cw, *, name):
    s = u2.shape[0]
    tr, h = min(TR, s), H_S
    sub = min(SUB, tr)
    n = s // tr
    sel = _row_selector(sub, [K_F - 1 - k for k in range(K_F)])

    def body(u_cur, c_cur, c_after, da_cur, da_after, sel_ref, w_ref, du_ref, dw_ref, dcb, win):
        i = pl.program_id(1)
        last = i == n - 1

        @pl.when(i == 0)
        def _():
            dw_ref[...] = jnp.zeros_like(dw_ref)

        def conv_grad(gt, up, da):
            gt, up, da = gt.astype(F32), up.astype(F32), da.astype(F32)
            sg = _sigmoid(gt)
            return (da * up * (sg * (1.0 + gt * (1.0 - sg)))).astype(BF), (da * (gt * sg)).astype(BF)

        for c in range(C_UP_P // LANE):
            gl = slice(LANE * c, LANE * c + LANE)
            ul = slice(C_UP_P + LANE * c, C_UP_P + LANE * c + LANE)
            for r0 in range(0, tr, sub):
                rows = slice(r0, r0 + sub)
                dcb[rows, gl], dcb[rows, ul] = conv_grad(c_cur[rows, gl], c_cur[rows, ul], da_cur[rows, gl])
            dg, du_ = conv_grad(c_after[:, gl], c_after[:, ul], da_after[:, gl])
            dcb[tr:tr + h, gl] = jnp.where(last, jnp.zeros_like(dg), dg)
            dcb[tr:tr + h, ul] = jnp.where(last, jnp.zeros_like(du_), du_)
        dcb[tr + h:tr + sub, :] = jnp.zeros((sub - h, _PAIR), BF)
        for r0 in range(0, tr, sub):
            win[...] = _dot(sel_ref[...], dcb[r0:r0 + 2 * sub, :])
            for c in range(_PAIR // LANE):
                sl = slice(LANE * c, LANE * c + LANE)
                u = u_cur[r0:r0 + sub, sl].astype(F32)
                du = None
                for k in range(K_F):
                    wk = win[k * sub:(k + 1) * sub, sl]
                    term = w_ref[k:k + 1, sl] * wk
                    du = term if du is None else du + term
                    dw_ref[k:k + 1, sl] += jnp.sum(wk * u, axis=0, keepdims=True)
                du_ref[r0:r0 + sub, sl] = du.astype(BF)

    return _pcall(
        body, grid=(4, n),
        in_specs=[pl.BlockSpec((tr, _PAIR), lambda p, i: (i, p)),
                  pl.BlockSpec((tr, _PAIR), lambda p, i: (i, p)),
                  pl.BlockSpec((h, _PAIR), lambda p, i: (_halo_after(i, tr, h, s), p)),
                  pl.BlockSpec((tr, C_UP_P), lambda p, i: (i, p)),
                  pl.BlockSpec((h, C_UP_P), lambda p, i: (_halo_after(i, tr, h, s), p)),
                  pl.BlockSpec(sel.shape, lambda p, i: (0, 0)),
                  pl.BlockSpec((8, _PAIR), lambda p, i: (0, p))],
        out_specs=[pl.BlockSpec((tr, _PAIR), lambda p, i: (i, p)), pl.BlockSpec((8, _PAIR), lambda p, i: (0, p))],
        out_shape=[SDS((s, 2 * FF_P), BF), SDS((8, 2 * FF_P), F32)],
        scratch_shapes=[pltpu.VMEM((tr + sub, _PAIR), BF), pltpu.VMEM((K_F * sub, _PAIR), F32)],
        compiler_params=_cp(("parallel", "arbitrary")), name=name)(u2, c2, c2, dact, dact, sel, cw)


def _relations():
    x, y, c = lax.axis_index("x"), lax.axis_index("y"), lax.axis_index("c")
    out = []
    for r in range(1, N_DEV):
        rx, ry, rc = (r >> 2) & 1, (r >> 1) & 1, r & 1
        out.append((r, (x ^ rx, y ^ ry, c ^ rc)))
    return out


def _my_index():
    return 4 * lax.axis_index("x") + 2 * lax.axis_index("y") + lax.axis_index("c")


def _slab(kind, ref, idx):
    if kind == "win":
        return ref.at[:, pl.ds(pl.multiple_of(idx * C_IN, LANE), C_IN)]
    if kind == "wup":
        return ref.at[:, pl.ds(pl.multiple_of(_up_slot(idx) * C_UP_P, LANE), C_UP_P)]
    if kind == "wupT":
        return ref.at[pl.ds(pl.multiple_of(_up_slot(idx) * C_UP_P, LANE), C_UP_P), :]
    if kind == "wkv":
        return ref.at[idx]
    if kind == "w4":
        return ref.at[:, pl.ds(pl.multiple_of(idx * R_O, 16), R_O), :]
    if kind == "wdn":
        return ref.at[pl.ds(pl.multiple_of(_dn_row(idx), 16), R_DN), :]
    assert kind == "cv"
    return ref.at[idx]


_WHOLE = {"win": ((D, 9 * D), BF), "wup": ((D, 2 * FF_P), BF), "wkv": ((N_DEV, D, C_KV), BF),
          "w4": ((4, D, D), BF), "wdn": ((FF_P, D), BF)}
_SHARD = {"win": (D, C_IN), "wup": (D, C_UP_P), "wupT": (C_UP_P, D), "wkv": (D, C_KV), "w4": (4, R_O, D),
          "wdn": (R_DN, D)}
HBM_SPEC = pl.BlockSpec(memory_space=pltpu.HBM)
SEM_SPEC = pl.BlockSpec(memory_space=pltpu.SEMAPHORE)
_DATAFLOW = pltpu.SideEffectType.DATAFLOW_SIDE_EFFECTING


def _scatter_maps(kinds):
    return ((lambda srcs, lands, a, idx: _slab(kinds[a], srcs[a], idx)),
            (lambda lands, a, idx: lands[a].at[idx]))


_SLOTTED = ("wkv", "cv")


def _own_slab_blocks(kind, shard_shape):
    if kind in ("win", "wup"):
        rows, slot = 256, (_up_slot if kind == "wup" else (lambda m: m))
        return (shard_shape[0] // rows, (rows, shard_shape[1]), (lambda i, me: (i, slot(me[0]))),
                (lambda i, me: (i, 0)), (lambda i, me: (me[0], i, 0)))
    if kind == "wupT":
        rows = 256
        steps = shard_shape[0] // rows
        return (steps, (rows, D), (lambda i, me: (_up_slot(me[0]) * steps + i, 0)), (lambda i, me: (i, 0)),
                (lambda i, me: (me[0], i, 0)))
    if kind == "w4":
        return (1, shard_shape, (lambda i, me: (0, me[0], 0)), (lambda i, me: (0, 0, 0)),
                (lambda i, me: (me[0], 0, 0, 0)))
    if kind == "wdn":
        rows = 32
        return (R_DN // rows, (rows, D), (lambda i, me: (_dn_row(me[0]) // rows + i, 0)), (lambda i, me: (i, 0)),
                (lambda i, me: (me[0], i, 0)))
    assert kind in _SLOTTED
    rows = min(256, shard_shape[0])
    return (shard_shape[0] // rows, (rows, shard_shape[1]), (lambda i, me: (me[0], i, 0)),
            (lambda i, me: (i, 0)), (lambda i, me: (me[0], i, 0)))


def _place_own(kind, src, out_sds, gather, me_arr, *, name):
    shard_shape = src.shape if gather else out_sds.shape[1:]
    steps, blk, whole_idx, shard_idx, staging_idx = _own_slab_blocks(kind, shard_shape)
    slotted = kind in _SLOTTED
    whole_spec = pl.BlockSpec(((None,) if slotted else ()) + tuple(blk), whole_idx)
    if gather:
        in_spec, out_spec = pl.BlockSpec(tuple(blk), shard_idx), whole_spec
    else:
        in_spec, out_spec = whole_spec, pl.BlockSpec((None,) + tuple(blk), staging_idx)
    zero_init = gather and kind == "wdn"

    def body(me_ref, src_ref, *rest):
        rest[-1][...] = src_ref[...].astype(rest[-1].dtype)

    operands = (me_arr, src) + ((jnp.zeros(out_sds.shape, out_sds.dtype),) if zero_init else ())
    return _pcall(
        body,
        grid_spec=pltpu.PrefetchScalarGridSpec(
            num_scalar_prefetch=1, grid=(steps,), in_specs=[in_spec] + ([ANY] if zero_init else []),
            out_specs=out_spec),
        out_shape=out_sds, input_output_aliases={2: 0} if zero_init else {},
        compiler_params=_cp(("arbitrary",)), name=name)(*operands)


def _peer_copies(n, src_of, dst_of, src_r, land_r, ssem, rsem):
    me = _my_index()
    out = []
    for r, peer in _relations():
        p_idx = 4 * peer[0] + 2 * peer[1] + peer[2]
        for a in range(n):
            def copy(src_idx, dst_idx, a=a, r=r, peer=peer):
                sem = a * (N_DEV - 1) + r - 1
                return pltpu.make_async_remote_copy(
                    src_ref=src_of(src_r, land_r, a, src_idx), dst_ref=dst_of(land_r, a, dst_idx),
                    send_sem=ssem.at[sem], recv_sem=rsem.at[sem], device_id=peer, device_id_type=MESH)
            out.append((functools.partial(copy, p_idx, me), functools.partial(copy, me, p_idx)))
    return out


def _exchange_start(srcs, lands, maps, after, *, name):
    n, ns = len(lands), len(srcs)
    src_of, dst_of = maps

    def body(*refs):
        src_r, land_r = refs[:ns], refs[ns:ns + n]
        ssem, rsem, token = refs[ns + n + 1], refs[ns + n + 2], refs[-1]
        for send, _ in _peer_copies(n, src_of, dst_of, src_r, land_r, ssem, rsem):
            send().start()
        token[...] = jnp.zeros_like(token)

    flight = list(srcs) + list(lands)
    outs = pl.pallas_call(
        body, name=name,
        out_shape=(pltpu.SemaphoreType.DMA((n * (N_DEV - 1),)), pltpu.SemaphoreType.DMA((n * (N_DEV - 1),)),
                   *[pltpu.HBM(a.shape, a.dtype) for a in flight], SDS((8, LANE), F32)),
        in_specs=[HBM_SPEC] * (ns + n) + [ANY],
        out_specs=(SEM_SPEC, SEM_SPEC, *[HBM_SPEC] * (ns + n), pl.BlockSpec(memory_space=pltpu.VMEM)),
        input_output_aliases={i: 2 + i for i in range(ns + n)},
        compiler_params=pltpu.CompilerParams(has_side_effects=_DATAFLOW),
    )(*[pltpu.with_memory_space_constraint(a, pltpu.HBM) for a in flight], after)
    return (outs[0], outs[1], list(outs[2:2 + ns + n]), ns), outs[-1]


def _exchange_wait(handle, maps, after, *, name):
    ssem, rsem, flight, ns = handle
    n = len(flight) - ns
    src_of, dst_of = maps

    def body(*refs):
        src_r, land_r, ssem_r, rsem_r = refs[:ns], refs[ns:ns + n], refs[ns + n], refs[ns + n + 1]
        for send, arrival in _peer_copies(n, src_of, dst_of, src_r, land_r, ssem_r, rsem_r):
            send().wait_send()
            arrival().wait_recv()

    outs = pl.pallas_call(
        body, name=name, out_shape=[pltpu.HBM(a.shape, a.dtype) for a in flight],
        in_specs=[HBM_SPEC] * (ns + n) + [SEM_SPEC, SEM_SPEC, ANY], out_specs=[HBM_SPEC] * (ns + n),
        input_output_aliases={i: i for i in range(ns + n)},
        compiler_params=pltpu.CompilerParams(has_side_effects=_DATAFLOW),
    )(*flight, ssem, rsem, after)
    return list(outs[ns:])


_SIBLING = 1
_ICI = (2, 4, 6)


def _rel_peer(r):
    x, y, c = lax.axis_index("x"), lax.axis_index("y"), lax.axis_index("c")
    peer = (x ^ ((r >> 2) & 1), y ^ ((r >> 1) & 1), c ^ (r & 1))
    return peer, 4 * peer[0] + 2 * peer[1] + peer[2]


def _rcopy(ref, ssem, rsem, peer):
    return pltpu.make_async_remote_copy(src_ref=ref, dst_ref=ref, send_sem=ssem, recv_sem=rsem, device_id=peer,
                                        device_id_type=MESH)


def _gather2_start(lands, kinds, after, *, name):
    n = len(lands)

    def body(*refs):
        land_r, (send1, recv_sib, recv_ici), token = refs[:n], refs[n + 1:n + 4], refs[-1]
        me = _my_index()
        for a in range(n):
            own = _slab(kinds[a], land_r[a], me)
            for j, r in enumerate((_SIBLING,) + _ICI):
                rsem = recv_sib.at[a] if r == _SIBLING else recv_ici.at[3 * a + j - 1]
                _rcopy(own, send1.at[4 * a + j], rsem, _rel_peer(r)[0]).start()
        token[...] = jnp.zeros_like(token)

    sems = [pltpu.SemaphoreType.DMA((4 * n,)), pltpu.SemaphoreType.DMA((n,)), pltpu.SemaphoreType.DMA((3 * n,))]
    outs = pl.pallas_call(
        body, name=name, out_shape=(*sems, *[pltpu.HBM(a.shape, a.dtype) for a in lands], SDS((8, LANE), F32)),
        in_specs=[HBM_SPEC] * n + [ANY],
        out_specs=(SEM_SPEC,) * 3 + (HBM_SPEC,) * n + (pl.BlockSpec(memory_space=pltpu.VMEM),),
        input_output_aliases={i: 3 + i for i in range(n)},
        compiler_params=pltpu.CompilerParams(has_side_effects=_DATAFLOW),
    )(*[pltpu.with_memory_space_constraint(a, pltpu.HBM) for a in lands], after)
    return dict(send1=outs[0], recv_sib=outs[1], recv_ici=outs[2], lands=list(outs[3:3 + n])), outs[-1]


def _gather2_forward(handle, kinds, after, *, name):
    lands = handle["lands"]
    n = len(lands)

    def body(*refs):
        land_r, recv_ici, (fwd_send, fwd_recv), token = refs[:n], refs[n], refs[n + 2:n + 4], refs[-1]
        sibling = _rel_peer(_SIBLING)[0]
        for a in range(n):
            for j, r in enumerate(_ICI):
                got = _slab(kinds[a], land_r[a], _rel_peer(r)[1])
                _rcopy(got, fwd_send.at[3 * a + j], recv_ici.at[3 * a + j], sibling).wait_recv()
                _rcopy(got, fwd_send.at[3 * a + j], fwd_recv.at[3 * a + j], sibling).start()
        token[...] = jnp.zeros_like(token)

    sems = [pltpu.SemaphoreType.DMA((3 * n,)), pltpu.SemaphoreType.DMA((3 * n,))]
    outs = pl.pallas_call(
        body, name=name, out_shape=(*sems, *[pltpu.HBM(a.shape, a.dtype) for a in lands], SDS((8, LANE), F32)),
        in_specs=[HBM_SPEC] * n + [SEM_SPEC, ANY],
        out_specs=(SEM_SPEC,) * 2 + (HBM_SPEC,) * n + (pl.BlockSpec(memory_space=pltpu.VMEM),),
        input_output_aliases={i: 2 + i for i in range(n)},
        compiler_params=pltpu.CompilerParams(has_side_effects=_DATAFLOW),
    )(*lands, handle["recv_ici"], after)
    return dict(handle, fwd_send=outs[0], fwd_recv=outs[1], lands=list(outs[2:2 + n])), outs[-1]


def _gather2_wait(handle, kinds, after, *, name):
    lands = handle["lands"]
    n = len(lands)

    def body(*refs):
        land_r, (send1, recv_sib, fwd_send, fwd_recv) = refs[:n], refs[n:n + 4]
        me = _my_index()
        sibling, sib_idx = _rel_peer(_SIBLING)
        for a in range(n):
            own = _slab(kinds[a], land_r[a], me)
            for j, r in enumerate((_SIBLING,) + _ICI):
                _rcopy(own, send1.at[4 * a + j], recv_sib.at[a], _rel_peer(r)[0]).wait_send()
            theirs = _slab(kinds[a], land_r[a], sib_idx)
            _rcopy(theirs, send1.at[4 * a], recv_sib.at[a], sibling).wait_recv()
            for j, r in enumerate(_ICI):
                passed_on = _slab(kinds[a], land_r[a], _rel_peer(r)[1])
                _rcopy(passed_on, fwd_send.at[3 * a + j], fwd_recv.at[3 * a + j], sibling).wait_send()
                arrived = _slab(kinds[a], land_r[a], _rel_peer(r ^ _SIBLING)[1])
                _rcopy(arrived, fwd_send.at[3 * a + j], fwd_recv.at[3 * a + j], sibling).wait_recv()

    outs = pl.pallas_call(
        body, name=name, out_shape=[pltpu.HBM(a.shape, a.dtype) for a in lands],
        in_specs=[HBM_SPEC] * n + [SEM_SPEC] * 4 + [ANY], out_specs=[HBM_SPEC] * n,
        input_output_aliases={i: i for i in range(n)},
        compiler_params=pltpu.CompilerParams(has_side_effects=_DATAFLOW),
    )(*lands, handle["send1"], handle["recv_sib"], handle["fwd_send"], handle["fwd_recv"], after)
    return list(outs)


def _sum_slots(gathered, *, name):
    def body(g_ref, out_ref):
        total = g_ref[0]
        for dev in range(1, N_DEV):
            total = total + g_ref[dev]
        out_ref[...] = total

    return _pcall(body, out_shape=SDS(gathered.shape[1:], F32), compiler_params=_cp(), name=name)(gathered)


def _adam(g, w, m, v):
    nm = ADAM_B1 * m + (1.0 - ADAM_B1) * g
    nv = ADAM_B2 * v + (1.0 - ADAM_B2) * (g * g)
    m_hat = nm / (1.0 - ADAM_B1 ** ADAM_STEP)
    v_hat = nv / (1.0 - ADAM_B2 ** ADAM_STEP)
    return -ADAM_LR * (m_hat / (jnp.sqrt(v_hat) + ADAM_EPS) + ADAM_WD * w), nm, nv


def _adamw_staged(st0, st1, w, m, v, *, name):
    _, rows, cols = w.shape
    st_cols = st0.shape[2]
    tr = max(t for t in range(16, 129, 16) if rows % t == 0)
    nr = rows // tr

    def body(s0_ref, s1_ref, w_ref, m_ref, v_ref, g_ref, d_ref, nm_ref, nv_ref):
        for layer, s_ref in enumerate((s0_ref, s1_ref)):
            @pl.when(pl.program_id(0) == layer)
            def _(s_ref=s_ref):
                total = s_ref[0, :, 0:cols].astype(F32)
                for dev in range(1, N_DEV):
                    total = total + s_ref[dev, :, 0:cols].astype(F32)
                g_ref[0] = total

        d_ref[0], nm_ref[0], nv_ref[0] = _adam(g_ref[0], w_ref[0], m_ref[0], v_ref[0])

    st_spec = lambda layer: pl.BlockSpec(
        (N_DEV, tr, st_cols), lambda l, i: (0, jnp.where(l == layer, i, (nr - 1) * (1 - layer)), 0))
    par = pl.BlockSpec((1, tr, cols), lambda l, i: (l, i, 0))
    return _pcall(
        body, grid=(DEPTH, nr), in_specs=[st_spec(0), st_spec(1), par, par, par], out_specs=[par] * 4,
        out_shape=[SDS(w.shape, F32)] * 4,
        compiler_params=_cp(("arbitrary", "arbitrary")), name=name)(st0, st1, w, m, v)


def _adamw_small(g, w, m, v, *, name):
    def body(g_ref, w_ref, m_ref, v_ref, d_ref, nm_ref, nv_ref):
        d_ref[...], nm_ref[...], nv_ref[...] = _adam(g_ref[...], w_ref[...], m_ref[...], v_ref[...])

    return _pcall(body, out_shape=[SDS(g.shape, F32)] * 3, compiler_params=_cp(), name=name)(g, w, m, v)


def _pack_rows(arrays):
    flat = jnp.concatenate([a.reshape(-1).astype(F32) for a in arrays])
    rows = -(-flat.shape[0] // (8 * D)) * 8
    return jnp.pad(flat, (0, rows * D - flat.shape[0])).reshape(rows, D)


def _unpack_rows(pack, like):
    flat = pack.reshape(-1)
    out, at = [], 0
    for a in like:
        out.append(flat[at:at + a.size].reshape(a.shape))
        at += a.size
    return out


def _layer_fwd(x, h, mem, win, mixer_weights, ffn_weights, after_up, small, g_next, tag):
    proj = _mm(h, win, tm=1024, tn=1536, name=f"proj_{tag}")
    wkv, w4, cw_a, cw_b, cw_f = mixer_weights(proj)
    za = _bra_fwd(proj, cw_a, name=f"bra_fwd_{tag}")
    cb = _brb_conv_fwd(proj, cw_b, small["conv_b_bias"], name=f"brb_conv_fwd_{tag}")
    sb = _ln_silu_fwd(cb, small["ln_b_g"], small["ln_b_b"], name=f"ln_silu_fwd_{tag}")
    memn, kv = _kv_prep(mem, small["norm_mem_g"], wkv, name=f"kv_prep_{tag}")
    o = _attn_fwd(proj, kv, name=f"attn_fwd_{tag}")
    ya, yb, yc, mg, x1, h2 = _mix_out(x, za, sb, o, proj, w4, small["b_gate"], small["norm_ffn_g"],
                                      name=f"mix_out_{tag}")
    wup, wdn = ffn_weights(h2)
    u2 = _mm(h2, wup, tm=1024, tn=1536, name=f"up_{tag}")
    token = after_up(u2)
    act, c2 = _ffn_act(u2, cw_f if token is None else _behind(cw_f, token), name=f"ffn_act_{tag}")
    x2, h_next = _mm_res_norm(act, wdn, x1, g_next, name=f"down_{tag}")
    saved = dict(x=x, h=h, proj=proj, za=za, cb=cb, sb=sb, memn=memn, kv=kv, o=o, ya=ya, yb=yb, yc=yc,
                 mg=mg, x1=x1, h2=h2, u2=u2, c2=c2, act=act)
    return x2, h_next, (win, wup, wkv, w4, wdn, cw_a, cw_b, cw_f), saved


def _behind(operand, token):
    return operand + token[0:1, 0:1]


def _layer_bwd(dx2, dx2b, mem, wts, small, sv, start, tag):
    win, wup, wkv, w4, wdn, cw_a, cw_b, cw_f = wts
    dact = _mm(dx2b, wdn, tb=True, tm=1024, tn=768, name=f"d_act_{tag}")
    dwdn = _mm(sv["act"], dx2b, ta=True, tm=768, tn=1024, name=f"dw_down_{tag}")
    du2, dcw_f = _ffn_bwd(sv["u2"], sv["c2"], dact, cw_f, name=f"ffn_bwd_{tag}")
    dwup_t = _mm(du2, sv["h2"], ta=True, tm=C_UP_P, tn=1024, name=f"dw_up_{tag}")
    token = start(("wdn", "wupT"), (dwdn, dwup_t), f"ffn_{tag}")
    dx1, dx1b, dg_ffn = _mm_nt_normbwd(du2, wup, sv["x1"], dx2, _behind(small["norm_ffn_g"], token),
                                       name=f"d_h2_{tag}")

    dya, dyb, dyc, dza, dsb, do, dproj, dbg = _mix_bwd(dx1b, sv["ya"], sv["yb"], sv["yc"], sv["proj"], w4,
                                                      small["b_gate"], name=f"mix_bwd_{tag}")
    dw4 = jnp.stack([
        _mm(a, b, ta=True, tm=1024, tn=512, name=f"dw_{nm}_{tag}")
        for nm, a, b in (("a_out", sv["za"], dya), ("b_out", sv["sb"], dyb), ("att_out", sv["o"], dyc),
                         ("o", sv["mg"], dx1b))])
    dq, dk, dv = _attn_bwd(sv["proj"], sv["kv"], do, name=f"attn_bwd_{tag}")
    dwkv, dg_mem = _kv_bwd(mem, small["norm_mem_g"], sv["memn"], dk, dv, wkv, name=f"kv_bwd_{tag}")
    token = start(("w4", "wkv"), (dw4, dwkv), f"mix_{tag}")
    dproj, dcw_a = _bra_bwd(sv["proj"], dza, _behind(cw_a, token), dproj, name=f"bra_bwd_{tag}")
    dcb, ln_sums = _ln_silu_bwd(sv["cb"], dsb, small["ln_b_g"], small["ln_b_b"], name=f"ln_silu_bwd_{tag}")
    dproj, dcw_b = _brb_conv_bwd(sv["proj"], dcb, dq, cw_b, dproj, name=f"brb_conv_bwd_{tag}")
    dwin = _mm(sv["h"], dproj, ta=True, tm=1024, tn=768, name=f"dw_in_{tag}")
    token = start(("win",), (dwin,), f"in_{tag}")
    dx, dxb, dg_mix = _mm_nt_normbwd(dproj, win, sv["x"], dx1, _behind(small["norm_mix_g"], token),
                                     tk=4608, name=f"d_h_{tag}")

    small_grads = [dg_mix[0:1], dg_mem[0:1], dbg[0:1].reshape(3, D), ln_sums[2:3], ln_sums[0:1], ln_sums[1:2],
                   dg_ffn[0:1], dcw_a[0:K_A], dcw_b[0:K_B], dcw_f[0:K_F].reshape(K_F * 2 * FF_P // D, D)]
    return dx, dxb, small_grads, token


_SMALL_ROWS = (1, 1, 3, 1, 1, 1, 1, K_A, K_B, K_F * 2 * FF_P // D)
_CV_ROWS = 48


def kernel(x, mem, norm_mix_g, norm_mem_g, w_in, b_gate, conv_a_w, w_a_out, conv_b_w, conv_b_bias, ln_b_g, ln_b_b, w_b_out, w_kv, w_att_out, w_o, norm_ffn_g, w_up, conv_ffn_w, w_down, norm_final_g, loss_target, m_norm_mix_g, m_norm_mem_g, m_w_in, m_b_gate, m_conv_a_w, m_w_a_out, m_conv_b_w, m_conv_b_bias, m_ln_b_g, m_ln_b_b, m_w_b_out, m_w_kv, m_w_att_out, m_w_o, m_norm_ffn_g, m_w_up, m_conv_ffn_w, m_w_down, m_norm_final_g, v_norm_mix_g, v_norm_mem_g, v_w_in, v_b_gate, v_conv_a_w, v_w_a_out, v_conv_b_w, v_conv_b_bias, v_ln_b_g, v_ln_b_b, v_w_b_out, v_w_kv, v_w_att_out, v_w_o, v_norm_ffn_g, v_w_up, v_conv_ffn_w, v_w_down, v_norm_final_g):
    me = _my_index()
    me_arr = me.astype(jnp.int32).reshape(1)
    x0, mem0, tgt = x.reshape(x.shape[1:]), mem.reshape(mem.shape[1:]), loss_target.reshape(x.shape[1:])
    up_pad = ((0, 0), (0, 0), (0, C_UP_P - C_UP))

    ag_groups = (("win",), ("wkv", "w4", "cv"), ("wup", "wdn"))
    kinds = ag_groups[0] + ag_groups[1] + ag_groups[2]
    smalls, ag_handles = [], []
    token = jnp.zeros((8, LANE), F32)
    for l in range(DEPTH):
        cv = jnp.zeros((_CV_ROWS, C_UP_P), F32)
        cv = cv.at[0:K_F, 0:C_UP].set(conv_ffn_w[l]).at[3:3 + K_A, 0:R_O].set(conv_a_w[l])
        cv = cv.at[8:8 + K_B, 0:R_O].set(conv_b_w[l])
        shards = dict(
            win=w_in[l], wup=jnp.pad(w_up[l], up_pad[1:]), wkv=w_kv[l],
            w4=jnp.stack([w_a_out[l], w_b_out[l], w_att_out[l], w_o[l]]), wdn=w_down[l], cv=cv)
        whole = dict({k: SDS(*_WHOLE[k]) for k in kinds if k != "cv"}, cv=SDS((N_DEV,) + cv.shape, F32))
        lands = {k: _place_own(k, shards[k], whole[k], True, me_arr, name=f"ag_own_{k}_l{l}") for k in kinds}
        per_layer = []
        for g, grp in enumerate(ag_groups):
            handle, token = _gather2_start([lands[k] for k in grp], grp, token, name=f"ag_start_l{l}_g{g}")
            per_layer.append(handle)
        ag_handles.append(per_layer)
        smalls.append(dict(
            norm_mix_g=norm_mix_g[l][None], norm_mem_g=norm_mem_g[l][None], b_gate=b_gate[l][None],
            conv_b_bias=conv_b_bias[l][None], ln_b_g=ln_b_g[l][None], ln_b_b=ln_b_b[l][None],
            norm_ffn_g=norm_ffn_g[l][None]))

    def forward_group(l, g, after):
        ag_handles[l][g], tok = _gather2_forward(ag_handles[l][g], ag_groups[g], after, name=f"ag_forward_l{l}_g{g}")
        return tok

    def group_of(l, g):
        def wait(after):
            if l == 0:
                after = forward_group(0, g, after)
            return _gather2_wait(ag_handles[l][g], ag_groups[g], after, name=f"ag_wait_l{l}_g{g}")
        return wait

    def mixer_weights(l):
        def wait(after):
            if l > 0:
                after = forward_group(l, 2, after)
            wkv, w4, cvg = group_of(l, 1)(after)
            cw_f = jnp.stack([cvg[d, 0:K_F, :] for d in UP_ORDER], axis=1).reshape(K_F, 2 * FF_P)
            cw_a = cvg[:, 3:3 + K_A, 0:R_O].transpose(1, 0, 2).reshape(K_A, D)
            cw_b = cvg[:, 8:8 + K_B, 0:R_O].transpose(1, 0, 2).reshape(K_B, D)
            return (wkv, w4, jnp.pad(cw_a, ((0, 8 - K_A), (0, 0))), jnp.pad(cw_b, ((0, 32 - K_B), (0, 0))),
                    jnp.pad(cw_f, ((0, 8 - K_F), (0, 0))))
        return wait

    wts, saved = [], []
    xs = x0
    h = _rms_fwd(xs, smalls[0]["norm_mix_g"], name="rms_fwd")
    behind = forward_group(0, 0, token)

    def next_layer_forwarding(l):
        def hook(after):
            return None if l + 1 == DEPTH else forward_group(l + 1, 0, after)
        return hook

    for l in range(DEPTH):
        g_next = smalls[l + 1]["norm_mix_g"] if l + 1 < DEPTH else norm_final_g[None]
        if l > 0:
            behind = forward_group(l, 1, behind)
        (win,) = _gather2_wait(ag_handles[l][0], ag_groups[0], behind, name=f"ag_wait_l{l}_g0")
        xs, h, w_l, sv = _layer_fwd(xs, h, mem0, win, mixer_weights(l), group_of(l, 2), next_layer_forwarding(l),
                                    smalls[l], g_next, f"l{l}")
        behind = h
        wts.append(w_l)
        saved.append(sv)
    dx, dxb, head_sums = _loss_head(xs, tgt, norm_final_g[None], name="loss_head")

    rs_handles = []
    small_grads = [None] * DEPTH

    def start_scatter(grp, arrays, name):
        maps = _scatter_maps(grp)
        lands = [_place_own(k, a, SDS((N_DEV,) + _SHARD[k], BF), False, me_arr, name=f"rs_own_{k}_{name}")
                 for k, a in zip(grp, arrays)]
        handle, tok = _exchange_start(list(arrays), lands, maps, rs_handles[-1][2] if rs_handles else head_sums,
                                      name=f"rs_start_{name}")
        rs_handles.append((grp, handle, tok, name))
        return tok

    for l in reversed(range(DEPTH)):
        dx, dxb, small_grads[l], token = _layer_bwd(dx, dxb, mem0, wts[l], smalls[l], saved[l], start_scatter,
                                                    f"l{l}")

    pack = jnp.concatenate(small_grads[0] + small_grads[1] + [head_sums[1:2], head_sums[0:1]], axis=0)
    pack = jnp.pad(pack, ((0, -pack.shape[0] % 8), (0, 0)))
    small_maps = (lambda srcs, lands, a, idx: srcs[a]), (lambda lands, a, idx: lands[a].at[idx])
    small_land = _place_own("cv", pack, SDS((N_DEV,) + pack.shape, F32), True, me_arr, name="small_own")
    small_handle, small_token = _exchange_start([pack], [small_land], small_maps, dx, name="small_start")

    staged = [dict() for _ in range(DEPTH)]
    for grp, handle, _, name in rs_handles[:-1]:
        staged[int(name[-1])].update(zip(grp, _exchange_wait(handle, _scatter_maps(grp), small_token,
                                                             name=f"rs_wait_{name}")))

    def big_update(kind, w, m, v, name):
        return _adamw_staged(staged[0][kind], staged[1][kind], w, m, v, name=name)

    r_up = [jnp.swapaxes(a, 1, 2) for a in big_update(
        "wupT", jnp.swapaxes(w_up, 1, 2), jnp.swapaxes(m_w_up, 1, 2), jnp.swapaxes(v_w_up, 1, 2), "adamw_w_up")]
    r_kv = big_update("wkv", w_kv, m_w_kv, v_w_kv, "adamw_w_kv")
    r_dn = big_update("wdn", w_down, m_w_down, v_w_down, "adamw_w_down")

    def four(a, b, c, d_):
        return jnp.stack([a, b, c, d_], axis=1).reshape(DEPTH, 4 * R_O, D)

    r_4 = _adamw_staged(
        staged[0]["w4"].reshape(N_DEV, 4 * R_O, D), staged[1]["w4"].reshape(N_DEV, 4 * R_O, D),
        four(w_a_out, w_b_out, w_att_out, w_o), four(m_w_a_out, m_w_b_out, m_w_att_out, m_w_o),
        four(v_w_a_out, v_w_b_out, v_w_att_out, v_w_o), name="adamw_w_out")
    grp, handle, _, name = rs_handles[-1]
    staged[0].update(zip(grp, _exchange_wait(handle, _scatter_maps(grp), r_4[0], name=f"rs_wait_{name}")))
    r_in = big_update("win", w_in, m_w_in, v_w_in, "adamw_w_in")
    r_a, r_b, r_att, r_o = ([a.reshape(DEPTH, 4, R_O, D)[:, j] for a in r_4] for j in range(4))

    (gathered,) = _exchange_wait(small_handle, small_maps, r_in[0], name="small_wait")
    total = _sum_slots(gathered, name="small_sum")
    per_layer = sum(_SMALL_ROWS)
    parts = []
    for l in range(DEPTH):
        at, one = l * per_layer, []
        for rows in _SMALL_ROWS:
            one.append(total[at:at + rows])
            at += rows
        parts.append(one)
    g_final = total[DEPTH * per_layer]
    loss = 0.5 / D * jnp.sum(total[DEPTH * per_layer + 1])

    def both(i):
        return jnp.stack([parts[0][i], parts[1][i]])

    g_norm_mix, g_norm_mem = both(0)[:, 0], both(1)[:, 0]
    g_b_gate = both(2).reshape(DEPTH, 3 * D)
    g_cbias, g_lng, g_lnb, g_norm_ffn = both(3)[:, 0], both(4)[:, 0], both(5)[:, 0], both(6)[:, 0]
    g_conv_a = lax.dynamic_slice_in_dim(both(7), me * R_O, R_O, axis=2)
    g_conv_b = lax.dynamic_slice_in_dim(both(8), me * R_O, R_O, axis=2)
    g_conv_f = lax.dynamic_slice_in_dim(both(9).reshape(DEPTH, K_F, 2 * FF_P), _up_slot(me) * C_UP_P, C_UP, axis=2)

    small_g = [g_norm_mix, g_norm_mem, g_b_gate, g_conv_a, g_conv_b, g_cbias, g_lng, g_lnb, g_norm_ffn, g_conv_f,
               g_final]
    small_w = [norm_mix_g, norm_mem_g, b_gate, conv_a_w, conv_b_w, conv_b_bias, ln_b_g, ln_b_b, norm_ffn_g,
               conv_ffn_w, norm_final_g]
    small_m = [m_norm_mix_g, m_norm_mem_g, m_b_gate, m_conv_a_w, m_conv_b_w, m_conv_b_bias, m_ln_b_g, m_ln_b_b,
               m_norm_ffn_g, m_conv_ffn_w, m_norm_final_g]
    small_v = [v_norm_mix_g, v_norm_mem_g, v_b_gate, v_conv_a_w, v_conv_b_w, v_conv_b_bias, v_ln_b_g, v_ln_b_b,
               v_norm_ffn_g, v_conv_ffn_w, v_norm_final_g]
    upd = _adamw_small(_pack_rows(small_g), _pack_rows(small_w), _pack_rows(small_m), _pack_rows(small_v),
                       name="adamw_small")
    s_d, s_m, s_v = (_unpack_rows(p, small_w) for p in upd)
    (d_norm_mix, d_norm_mem, d_b_gate, d_conv_a, d_conv_b, d_cbias, d_lng, d_lnb, d_norm_ffn, d_conv_f,
     d_final) = s_d
    (nm_norm_mix, nm_norm_mem, nm_b_gate, nm_conv_a, nm_conv_b, nm_cbias, nm_lng, nm_lnb, nm_norm_ffn, nm_conv_f,
     nm_final) = s_m
    (nv_norm_mix, nv_norm_mem, nv_b_gate, nv_conv_a, nv_conv_b, nv_cbias, nv_lng, nv_lnb, nv_norm_ffn, nv_conv_f,
     nv_final) = s_v

    grads = [g_norm_mix, g_norm_mem, r_in[0], g_b_gate, g_conv_a, r_a[0], g_conv_b, g_cbias, g_lng, g_lnb, r_b[0],
             r_kv[0], r_att[0], r_o[0], g_norm_ffn, r_up[0], g_conv_f, r_dn[0], g_final]
    deltas = [d_norm_mix, d_norm_mem, r_in[1], d_b_gate, d_conv_a, r_a[1], d_conv_b, d_cbias, d_lng, d_lnb, r_b[1],
              r_kv[1], r_att[1], r_o[1], d_norm_ffn, r_up[1], d_conv_f, r_dn[1], d_final]
    new_m = [nm_norm_mix, nm_norm_mem, r_in[2], nm_b_gate, nm_conv_a, r_a[2], nm_conv_b, nm_cbias, nm_lng, nm_lnb,
             r_b[2], r_kv[2], r_att[2], r_o[2], nm_norm_ffn, r_up[2], nm_conv_f, r_dn[2], nm_final]
    new_v = [nv_norm_mix, nv_norm_mem, r_in[3], nv_b_gate, nv_conv_a, r_a[3], nv_conv_b, nv_cbias, nv_lng, nv_lnb,
             r_b[3], r_kv[3], r_att[3], r_o[3], nv_norm_ffn, r_up[3], nv_conv_f, r_dn[3], nv_final]
    return (loss, dx[None], *grads, *deltas, *new_m, *new_v)
```

```python
import functools

import jax
import jax.numpy as jnp
import numpy as np
from jax import lax
from jax.experimental import pallas as pl
from jax.experimental.pallas import tpu as pltpu

F32 = jnp.float32
BF = jnp.bfloat16
SDS = jax.ShapeDtypeStruct
MESH = pl.DeviceIdType.MESH
ANY = pl.BlockSpec(memory_space=pl.ANY)

N_DEV = 8
DEPTH = 2
D = 1024
N_HEADS = 4
HEAD = D // N_HEADS
D_FF = 2816
K_A, K_B, K_F = 3, 31, 3
NORM_EPS = 1e-6

C_IN = 9 * D // N_DEV
C_KV = 2 * D // N_DEV
C_UP = 2 * D_FF // N_DEV
LANE = 128
C_UP_P = -(-C_UP // LANE) * LANE
FF_P = 4 * C_UP_P
R_O = D // N_DEV
R_DN = D_FF // N_DEV

VMEM_LIMIT = 56 * 1024 * 1024
TM = 512
TR = 512
TR_FFN = 1024
SUB = 128
H_S, H_L = 16, 32

ADAM_LR, ADAM_B1, ADAM_B2, ADAM_EPS, ADAM_WD, ADAM_STEP = 0.001, 0.9, 0.999, 1e-08, 0.01, 10

UP_ORDER = (0, 4, 1, 5, 2, 6, 3, 7)


def _pcall(body, **kw):
    return pl.pallas_call(body, **kw)


def _cp(sem=None, **kw):
    return pltpu.CompilerParams(dimension_semantics=sem, vmem_limit_bytes=VMEM_LIMIT, **kw)


def _dot(a, b):
    return jnp.dot(a, b, preferred_element_type=F32)


def _dot_nt(a, b):
    return lax.dot_general(a, b, (((1,), (1,)), ((), ())), preferred_element_type=F32)


def _dot_tn(a, b):
    return lax.dot_general(a, b, (((0,), (0,)), ((), ())), preferred_element_type=F32)


def _sigmoid(z):
    return 1.0 / (1.0 + jnp.exp(-z))


def _rms(xv):
    return lax.rsqrt(jnp.mean(xv * xv, axis=-1, keepdims=True) + NORM_EPS)


def _up_slot(idx):
    return jnp.where(idx < 4, 2 * idx, 2 * (idx - 4) + 1)


def _dn_row(idx):
    return C_UP_P * (idx // 2) + R_DN * (idx % 2)


def _mm(a, b, *, ta=False, tb=False, out_dtype=BF, tm=TM, tn=512, tk=None, name):
    m, k_dim = (a.shape[1], a.shape[0]) if ta else a.shape
    n = b.shape[0] if tb else b.shape[1]
    tm, tn = min(tm, m), min(tn, n)
    tk = k_dim if tk is None else min(tk, k_dim)
    nk = k_dim // tk
    assert m % tm == 0 and n % tn == 0 and k_dim % tk == 0
    dims = (((0 if ta else 1,), (1 if tb else 0,)), ((), ()))

    def body(a_ref, b_ref, o_ref, *scratch):
        part = lax.dot_general(a_ref[...], b_ref[...], dims, preferred_element_type=F32)
        if nk == 1:
            o_ref[...] = part.astype(o_ref.dtype)
            return
        acc = scratch[0]
        k = pl.program_id(2)

        @pl.when(k == 0)
        def _():
            acc[...] = part

        @pl.when(k > 0)
        def _():
            acc[...] += part

        @pl.when(k == nk - 1)
        def _():
            o_ref[...] = acc[...].astype(o_ref.dtype)

    a_spec = pl.BlockSpec((tk, tm), lambda i, j, k: (k, i)) if ta else pl.BlockSpec((tm, tk), lambda i, j, k: (i, k))
    b_spec = pl.BlockSpec((tn, tk), lambda i, j, k: (j, k)) if tb else pl.BlockSpec((tk, tn), lambda i, j, k: (k, j))
    return _pcall(
        body, grid=(m // tm, n // tn, nk), in_specs=[a_spec, b_spec],
        out_specs=pl.BlockSpec((tm, tn), lambda i, j, k: (i, j)),
        out_shape=SDS((m, n), out_dtype),
        scratch_shapes=[pltpu.VMEM((tm, tn), F32)] if nk > 1 else [],
        compiler_params=_cp(("parallel", "parallel", "arbitrary")), name=name)(a, b)


def _mm_res_norm(a, w, x, g, *, name):
    s, k_dim = a.shape
    tm = min(TM, s)

    def body(a_ref, w_ref, x_ref, g_ref, xo_ref, h_ref):
        xo = x_ref[...] + _dot(a_ref[...], w_ref[...])
        xo_ref[...] = xo
        h_ref[...] = ((xo * _rms(xo)) * g_ref[...]).astype(BF)

    return _pcall(
        body, grid=(s // tm,),
        in_specs=[pl.BlockSpec((tm, k_dim), lambda i: (i, 0)),
                  pl.BlockSpec((k_dim, D), lambda i: (0, 0), pipeline_mode=pl.Buffered(1)),
                  pl.BlockSpec((tm, D), lambda i: (i, 0)), pl.BlockSpec((1, D), lambda i: (0, 0))],
        out_specs=[pl.BlockSpec((tm, D), lambda i: (i, 0))] * 2,
        out_shape=[SDS((s, D), F32), SDS((s, D), BF)],
        compiler_params=_cp(("parallel",)), name=name)(a, w, x, g)


def _mm_nt_normbwd(da, w, x, dres, g, *, tk=None, name):
    s, k_dim = da.shape
    tm = min(TM, s)
    tk = k_dim if tk is None else tk
    nk = k_dim // tk
    assert k_dim % tk == 0

    def body(da_ref, w_ref, x_ref, dres_ref, g_ref, dx_ref, dxb_ref, dg_ref, *scratch):
        i, k = pl.program_id(0), pl.program_id(1)
        part = _dot_nt(da_ref[...], w_ref[...])
        if nk > 1:
            acc = scratch[0]

            @pl.when(k == 0)
            def _():
                acc[...] = part

            @pl.when(k > 0)
            def _():
                acc[...] += part

        @pl.when((i == 0) & (k == 0))
        def _():
            dg_ref[...] = jnp.zeros_like(dg_ref)

        @pl.when(k == nk - 1)
        def _():
            dh = acc[...] if nk > 1 else part
            xv = x_ref[...]
            r = _rms(xv)
            xn = xv * r
            dg_ref[0:1, :] += jnp.sum(dh * xn, axis=0, keepdims=True)
            dxn = dh * g_ref[...]
            dx = dres_ref[...] + r * (dxn - xn * jnp.mean(dxn * xn, axis=-1, keepdims=True))
            dx_ref[...] = dx
            dxb_ref[...] = dx.astype(BF)

    row = lambda i, k: (i, 0)
    w_spec = (pl.BlockSpec((D, tk), lambda i, k: (0, k)) if nk > 1 else
              pl.BlockSpec((D, tk), lambda i, k: (0, 0), pipeline_mode=pl.Buffered(1)))
    return _pcall(
        body, grid=(s // tm, nk),
        in_specs=[pl.BlockSpec((tm, tk), lambda i, k: (i, k)), w_spec,
                  pl.BlockSpec((tm, D), row), pl.BlockSpec((tm, D), row), pl.BlockSpec((1, D), lambda i, k: (0, 0))],
        out_specs=[pl.BlockSpec((tm, D), row), pl.BlockSpec((tm, D), row), pl.BlockSpec((8, D), lambda i, k: (0, 0))],
        out_shape=[SDS((s, D), F32), SDS((s, D), BF), SDS((8, D), F32)],
        scratch_shapes=[pltpu.VMEM((tm, D), F32)] if nk > 1 else [],
        compiler_params=_cp(("arbitrary", "arbitrary")), name=name)(da, w, x, dres, g)


def _rms_fwd(x, g, *, name):
    s = x.shape[0]
    tm = min(TM, s)

    def body(x_ref, g_ref, h_ref):
        xv = x_ref[...]
        h_ref[...] = ((xv * _rms(xv)) * g_ref[...]).astype(BF)

    return _pcall(
        body, grid=(s // tm,),
        in_specs=[pl.BlockSpec((tm, D), lambda i: (i, 0)), pl.BlockSpec((1, D), lambda i: (0, 0))],
        out_specs=pl.BlockSpec((tm, D), lambda i: (i, 0)), out_shape=SDS((s, D), BF),
        compiler_params=_cp(("parallel",)), name=name)(x, g)


def _loss_head(x, tgt, g, *, name):
    s = x.shape[0]
    tm = min(TM, s)

    def body(x_ref, t_ref, g_ref, dx_ref, dxb_ref, sums_ref):
        @pl.when(pl.program_id(0) == 0)
        def _():
            sums_ref[...] = jnp.zeros_like(sums_ref)

        xv = x_ref[...]
        r = _rms(xv)
        xn = xv * r
        diff = xn * g_ref[...] - t_ref[...]
        sums_ref[0:1, :] += jnp.sum(diff * diff, axis=0, keepdims=True)
        dy = diff * (1.0 / D)
        sums_ref[1:2, :] += jnp.sum(dy * xn, axis=0, keepdims=True)
        dxn = dy * g_ref[...]
        dx = r * (dxn - xn * jnp.mean(dxn * xn, axis=-1, keepdims=True))
        dx_ref[...] = dx
        dxb_ref[...] = dx.astype(BF)

    row = lambda i: (i, 0)
    return _pcall(
        body, grid=(s // tm,),
        in_specs=[pl.BlockSpec((tm, D), row), pl.BlockSpec((tm, D), row), pl.BlockSpec((1, D), lambda i: (0, 0))],
        out_specs=[pl.BlockSpec((tm, D), row), pl.BlockSpec((tm, D), row), pl.BlockSpec((8, D), lambda i: (0, 0))],
        out_shape=[SDS((s, D), F32), SDS((s, D), BF), SDS((8, D), F32)],
        compiler_params=_cp(("arbitrary",)), name=name)(x, tgt, g)


def _halo_before(i, tr, h):
    return jnp.maximum(i * (tr // h) - 1, 0)


def _halo_after(i, tr, h, s):
    return jnp.minimum((i + 1) * (tr // h), s // h - 1)


def _taps(buf, w_ref, sl, k_w, base, rows):
    acc = None
    for k in range(k_w):
        t = w_ref[k:k + 1, sl] * buf[base + k:base + k + rows, sl]
        acc = t if acc is None else acc + t
    return acc


def _taps_rev(buf, w_ref, sl, k_w, base, rows):
    acc = None
    for k in range(k_w):
        t = w_ref[k:k + 1, sl] * buf[base + k_w - 1 - k:base + k_w - 1 - k + rows, sl]
        acc = t if acc is None else acc + t
    return acc


def _tap_grads(dw_ref, dc, buf, sl, k_w, base, rows):
    for k in range(k_w):
        dw_ref[k:k + 1, sl] += jnp.sum(dc * buf[base + k:base + k + rows, sl], axis=0, keepdims=True)


def _bra_fwd(proj, cw, *, name):
    s = proj.shape[0]
    tr, h = min(TR, s), H_S
    sub = min(SUB, tr)

    def body(cur, halo, w_ref, za_ref, cvb):
        i = pl.program_id(0)
        hv = halo[:, D:2 * D].astype(F32) * halo[:, 2 * D:3 * D].astype(F32)
        cvb[0:h, :] = jnp.where(i == 0, 0.0, hv)
        cvb[h:h + tr, :] = cur[:, D:2 * D].astype(F32) * cur[:, 2 * D:3 * D].astype(F32)
        for c in range(D // LANE):
            sl = slice(LANE * c, LANE * c + LANE)
            ca = _taps(cvb, w_ref, sl, K_A, h - (K_A - 1), tr)
            za_ref[:, sl] = (cur[:, sl].astype(F32) * ca).astype(BF)

    return _pcall(
        body, grid=(s // tr,),
        in_specs=[pl.BlockSpec((tr, 3 * D), lambda i: (i, 0)),
                  pl.BlockSpec((h, 3 * D), lambda i: (_halo_before(i, tr, h), 0)),
                  pl.BlockSpec((8, D), lambda i: (0, 0))],
        out_specs=pl.BlockSpec((tr, D), lambda i: (i, 0)), out_shape=SDS((s, D), BF),
        scratch_shapes=[pltpu.VMEM((h + tr, D), F32)],
        compiler_params=_cp(("parallel",)), name=name)(proj, proj, cw)


def _bra_bwd(proj, dza, cw, dproj, *, name):
    s = proj.shape[0]
    tr, h = min(TR, s), H_S
    sub = min(SUB, tr)
    n = s // tr

    def body(before, cur, after, dz_cur, dz_after, w_ref, dproj_in, da_ref, dw_ref, cvb, dcab):
        del dproj_in
        i = pl.program_id(0)

        @pl.when(i == 0)
        def _():
            dw_ref[...] = jnp.zeros_like(dw_ref)

        first, last = i == 0, i == n - 1
        cvb[0:h, :] = jnp.where(first, 0.0, before[:, D:2 * D].astype(F32) * before[:, 2 * D:3 * D].astype(F32))
        cvb[h:h + tr, :] = cur[:, D:2 * D].astype(F32) * cur[:, 2 * D:3 * D].astype(F32)
        dcab[0:tr, :] = dz_cur[...].astype(F32) * cur[:, 0:D].astype(F32)
        dcab[tr:tr + h, :] = jnp.where(last, 0.0, dz_after[...].astype(F32) * after[:, 0:D].astype(F32))
        for c in range(D // LANE):
            sl = slice(LANE * c, LANE * c + LANE)
            gl, vl = slice(D + LANE * c, D + LANE * c + LANE), slice(2 * D + LANE * c, 2 * D + LANE * c + LANE)
            for r0 in range(0, tr, sub):
                rows = slice(r0, r0 + sub)
                ca = _taps(cvb, w_ref, sl, K_A, h - (K_A - 1) + r0, sub)
                da_ref[rows, sl] = (dz_cur[rows, sl].astype(F32) * ca).astype(BF)
                dcv = _taps_rev(dcab, w_ref, sl, K_A, r0, sub)
                da_ref[rows, gl] = (dcv * cur[rows, vl].astype(F32)).astype(BF)
                da_ref[rows, vl] = (dcv * cur[rows, gl].astype(F32)).astype(BF)
                _tap_grads(dw_ref, dcab[rows, sl], cvb, sl, K_A, h - (K_A - 1) + r0, sub)

    return _pcall(
        body, grid=(n,),
        in_specs=[pl.BlockSpec((h, 3 * D), lambda i: (_halo_before(i, tr, h), 0)),
                  pl.BlockSpec((tr, 3 * D), lambda i: (i, 0)),
                  pl.BlockSpec((h, 3 * D), lambda i: (_halo_after(i, tr, h, s), 0)),
                  pl.BlockSpec((tr, D), lambda i: (i, 0)),
                  pl.BlockSpec((h, D), lambda i: (_halo_after(i, tr, h, s), 0)),
                  pl.BlockSpec((8, D), lambda i: (0, 0)), ANY],
        out_specs=[pl.BlockSpec((tr, 3 * D), lambda i: (i, 0)), pl.BlockSpec((8, D), lambda i: (0, 0))],
        out_shape=[SDS(dproj.shape, BF), SDS((8, D), F32)], input_output_aliases={6: 0},
        scratch_shapes=[pltpu.VMEM((h + tr, D), F32), pltpu.VMEM((tr + h, D), F32)],
        compiler_params=_cp(("arbitrary",)), name=name)(proj, proj, proj, dza, dza, cw, dproj)


_U_COL, _UG_COL = 3, 4


def _brb_conv_fwd(proj, cw, bias, *, name):
    s = proj.shape[0]
    tr, h = min(TR, s), H_L
    sub = min(SUB, tr)

    def body(u_cur, ug_cur, u_halo, ug_halo, w_ref, b_ref, cb_ref, glb, shifted):
        i = pl.program_id(0)
        glb[0:h, :] = jnp.where(i == 0, 0.0, u_halo[...].astype(F32) * _sigmoid(ug_halo[...].astype(F32)))
        glb[h:h + tr, :] = u_cur[...].astype(F32) * _sigmoid(ug_cur[...].astype(F32))
        for c in range(D // LANE):
            sl = slice(LANE * c, LANE * c + LANE)
            for r in range(1, 8):
                shifted[r] = glb[8 - r:8 - r + tr + 24, sl]
            for r0 in range(0, tr, sub):
                acc = None
                for k in range(K_B):
                    q, r = divmod(K_B - 1 - k, 8)
                    at = r0 - 8 * q
                    win = shifted[r, 24 + at:24 + at + sub, :] if r else glb[h + at:h + at + sub, sl]
                    term = w_ref[k:k + 1, sl] * win
                    acc = term if acc is None else acc + term
                cb_ref[r0:r0 + sub, sl] = (acc + b_ref[:, sl]).astype(BF)

    return _pcall(
        body, grid=(s // tr,),
        in_specs=[pl.BlockSpec((tr, D), lambda i: (i, _U_COL)), pl.BlockSpec((tr, D), lambda i: (i, _UG_COL)),
                  pl.BlockSpec((h, D), lambda i: (_halo_before(i, tr, h), _U_COL)),
                  pl.BlockSpec((h, D), lambda i: (_halo_before(i, tr, h), _UG_COL)),
                  pl.BlockSpec((32, D), lambda i: (0, 0)), pl.BlockSpec((1, D), lambda i: (0, 0))],
        out_specs=pl.BlockSpec((tr, D), lambda i: (i, 0)), out_shape=SDS((s, D), BF),
        scratch_shapes=[pltpu.VMEM((h + tr, D), F32), pltpu.VMEM((8, tr + 24, LANE), F32)],
        compiler_params=_cp(("parallel",)), name=name)(proj, proj, proj, proj, cw, bias)


def _brb_conv_bwd(proj, dcb, dq, cw, dproj, *, name):
    s = proj.shape[0]
    tr, h = min(TR, s), H_L
    sub = min(SUB, tr)
    n = s // tr
    nb = -(-(tr + 24) // sub)
    sel = _row_selector(sub, list(range(8)))

    def body(u_cur, ug_cur, d_cur, d_after, dq_ref, sel_ref, w_ref, dproj_in, db_ref, dw_ref, dcbb, shifted):
        del dproj_in
        i = pl.program_id(0)

        @pl.when(i == 0)
        def _():
            dw_ref[...] = jnp.zeros_like(dw_ref)

        db_ref[:, 2 * D:3 * D] = dq_ref[...]
        after = d_after[...]
        dcbb[0:tr, :] = d_cur[...]
        dcbb[tr:tr + h, :] = jnp.where(i == n - 1, jnp.zeros_like(after), after)
        dcbb[tr + h:(nb + 1) * sub, :] = jnp.zeros(((nb + 1) * sub - h - tr, D), BF)
        for c in range(D // LANE):
            sl = slice(LANE * c, LANE * c + LANE)
            for blk in range(nb):
                res = _dot(sel_ref[...], dcbb[blk * sub:(blk + 2) * sub, sl])
                for r in range(8):
                    shifted[r, blk * sub:(blk + 1) * sub, :] = res[r * sub:(r + 1) * sub]
            for r0 in range(0, tr, sub):
                u = u_cur[r0:r0 + sub, sl].astype(F32)
                sg = _sigmoid(ug_cur[r0:r0 + sub, sl].astype(F32))
                glu = u * sg
                dglu = None
                for k in range(K_B):
                    q, r = divmod(K_B - 1 - k, 8)
                    at = r0 + 8 * q
                    win = shifted[r, at:at + sub, :]
                    term = w_ref[k:k + 1, sl] * win
                    dglu = term if dglu is None else dglu + term
                    dw_ref[k:k + 1, sl] += jnp.sum(win * glu, axis=0, keepdims=True)
                db_ref[r0:r0 + sub, sl] = (dglu * sg).astype(BF)
                db_ref[r0:r0 + sub, D + LANE * c:D + LANE * c + LANE] = (dglu * u * sg * (1.0 - sg)).astype(BF)

    return _pcall(
        body, grid=(n,),
        in_specs=[pl.BlockSpec((tr, D), lambda i: (i, _U_COL)), pl.BlockSpec((tr, D), lambda i: (i, _UG_COL)),
                  pl.BlockSpec((tr, D), lambda i: (i, 0)),
                  pl.BlockSpec((h, D), lambda i: (_halo_after(i, tr, h, s), 0)),
                  pl.BlockSpec((tr, D), lambda i: (i, 0)),
                  pl.BlockSpec(sel.shape, lambda i: (0, 0)),
                  pl.BlockSpec((32, D), lambda i: (0, 0)), ANY],
        out_specs=[pl.BlockSpec((tr, 3 * D), lambda i: (i, 1)), pl.BlockSpec((32, D), lambda i: (0, 0))],
        out_shape=[SDS(dproj.shape, BF), SDS((32, D), F32)], input_output_aliases={7: 0},
        scratch_shapes=[pltpu.VMEM(((nb + 1) * sub, D), BF), pltpu.VMEM((8, nb * sub, LANE), F32)],
        compiler_params=_cp(("arbitrary",)), name=name)(proj, proj, dcb, dcb, dq, sel, cw, dproj)


def _ln_silu_fwd(cb, g, b, *, name):
    s = cb.shape[0]
    tm = min(TM, s)

    def body(cb_ref, g_ref, b_ref, sb_ref):
        z = cb_ref[...].astype(F32)
        zc = z - jnp.mean(z, axis=-1, keepdims=True)
        ln = (zc * lax.rsqrt(jnp.mean(zc * zc, axis=-1, keepdims=True) + NORM_EPS)) * g_ref[...] + b_ref[...]
        sb_ref[...] = (ln * _sigmoid(ln)).astype(BF)

    row = lambda i: (i, 0)
    vec = pl.BlockSpec((1, D), lambda i: (0, 0))
    return _pcall(
        body, grid=(s // tm,), in_specs=[pl.BlockSpec((tm, D), row), vec, vec],
        out_specs=pl.BlockSpec((tm, D), row), out_shape=SDS((s, D), BF),
        compiler_params=_cp(("parallel",)), name=name)(cb, g, b)


def _ln_silu_bwd(cb, dsb, g, b, *, name):
    s = cb.shape[0]
    tm = min(TM, s)

    def body(cb_ref, dsb_ref, g_ref, b_ref, dcb_ref, sums_ref):
        @pl.when(pl.program_id(0) == 0)
        def _():
            sums_ref[...] = jnp.zeros_like(sums_ref)

        z = cb_ref[...].astype(F32)
        zc = z - jnp.mean(z, axis=-1, keepdims=True)
        rstd = lax.rsqrt(jnp.mean(zc * zc, axis=-1, keepdims=True) + NORM_EPS)
        lnh = zc * rstd
        ln = lnh * g_ref[...] + b_ref[...]
        sg = _sigmoid(ln)
        dln = dsb_ref[...].astype(F32) * (sg * (1.0 + ln * (1.0 - sg)))
        sums_ref[0:1, :] += jnp.sum(dln * lnh, axis=0, keepdims=True)
        sums_ref[1:2, :] += jnp.sum(dln, axis=0, keepdims=True)
        dlnh = dln * g_ref[...]
        dz = rstd * (dlnh - jnp.mean(dlnh, axis=-1, keepdims=True)
                     - lnh * jnp.mean(dlnh * lnh, axis=-1, keepdims=True))
        sums_ref[2:3, :] += jnp.sum(dz, axis=0, keepdims=True)
        dcb_ref[...] = dz.astype(BF)

    row = lambda i: (i, 0)
    vec = pl.BlockSpec((1, D), lambda i: (0, 0))
    return _pcall(
        body, grid=(s // tm,), in_specs=[pl.BlockSpec((tm, D), row), pl.BlockSpec((tm, D), row), vec, vec],
        out_specs=[pl.BlockSpec((tm, D), row), pl.BlockSpec((8, D), lambda i: (0, 0))],
        out_shape=[SDS((s, D), BF), SDS((8, D), F32)],
        compiler_params=_cp(("arbitrary",)), name=name)(cb, dsb, g, b)


_Q_COL = 5 * D // HEAD


def _kv_prep(mem, g, wkv, *, name):
    m = mem.shape[0]

    def body(mem_ref, g_ref, w_ref, memn_ref, kv_ref):
        mv = mem_ref[...]
        memn = ((mv * _rms(mv)) * g_ref[...]).astype(BF)
        memn_ref[...] = memn
        for dev in range(N_DEV):
            kv_ref[:, dev * C_KV:(dev + 1) * C_KV] = _dot(memn, w_ref[dev]).astype(BF)

    return _pcall(body, out_shape=[SDS((m, D), BF), SDS((m, 2 * D), BF)],
                  compiler_params=_cp(), name=name)(mem, g, wkv)


def _softmax_rows(q, k):
    sc = _dot_nt(q, k) * (1.0 / (HEAD ** 0.5))
    e = jnp.exp(sc - jnp.max(sc, axis=-1, keepdims=True))
    return e / jnp.sum(e, axis=-1, keepdims=True)


def _attn_fwd(proj, kv, *, name):
    s, m = proj.shape[0], kv.shape[0]
    tm = min(TM, s)

    def body(q_ref, kv_ref, o_ref):
        for hd in range(N_HEADS):
            cols = slice(hd * HEAD, (hd + 1) * HEAD)
            p = _softmax_rows(q_ref[:, cols], kv_ref[:, cols])
            o_ref[:, cols] = _dot(p.astype(BF), kv_ref[:, D + hd * HEAD:D + (hd + 1) * HEAD]).astype(BF)

    return _pcall(
        body, grid=(s // tm,),
        in_specs=[pl.BlockSpec((tm, D), lambda i: (i, _Q_COL // N_HEADS)),
                  pl.BlockSpec((m, 2 * D), lambda i: (0, 0))],
        out_specs=pl.BlockSpec((tm, D), lambda i: (i, 0)), out_shape=SDS((s, D), BF),
        compiler_params=_cp(("parallel",)), name=name)(proj, kv)


def _attn_bwd(proj, kv, do, *, name):
    s, m = proj.shape[0], kv.shape[0]
    tm = min(TM, s)

    def body(q_ref, kv_ref, do_ref, dq_ref, dk_ref, dv_ref):
        @pl.when(pl.program_id(0) == 0)
        def _():
            dk_ref[...] = jnp.zeros_like(dk_ref)
            dv_ref[...] = jnp.zeros_like(dv_ref)

        for hd in range(N_HEADS):
            cols = slice(hd * HEAD, (hd + 1) * HEAD)
            q, k, dov = q_ref[:, cols], kv_ref[:, cols], do_ref[:, cols]
            p = _softmax_rows(q, k)
            dp = _dot_nt(dov, kv_ref[:, D + hd * HEAD:D + (hd + 1) * HEAD])
            dv_ref[:, cols] += _dot_tn(p.astype(BF), dov)
            ds = (p * (dp - jnp.sum(dp * p, axis=-1, keepdims=True)) * (1.0 / (HEAD ** 0.5))).astype(BF)
            dq_ref[:, cols] = _dot(ds, k).astype(BF)
            dk_ref[:, cols] += _dot_tn(ds, q)

    return _pcall(
        body, grid=(s // tm,),
        in_specs=[pl.BlockSpec((tm, D), lambda i: (i, _Q_COL // N_HEADS)),
                  pl.BlockSpec((m, 2 * D), lambda i: (0, 0)),
                  pl.BlockSpec((tm, D), lambda i: (i, 0))],
        out_specs=[pl.BlockSpec((tm, D), lambda i: (i, 0)),
                   pl.BlockSpec((m, D), lambda i: (0, 0)),
                   pl.BlockSpec((m, D), lambda i: (0, 0))],
        out_shape=[SDS((s, D), BF), SDS((m, D), F32), SDS((m, D), F32)],
        compiler_params=_cp(("arbitrary",)), name=name)(proj, kv, do)


def _kv_bwd(mem, g, memn, dk, dv, wkv, *, name):
    def body(mem_ref, g_ref, memn_ref, dk_ref, dv_ref, w_ref, dw_ref, dg_ref):
        memn = memn_ref[...]
        dmemn = None
        for dev in range(N_DEV):
            d_ref, col = (dk_ref, dev) if dev < N_HEADS else (dv_ref, dev - N_HEADS)
            dslab = d_ref[:, col * C_KV:(col + 1) * C_KV].astype(BF)
            dw_ref[dev] = _dot_tn(memn, dslab).astype(BF)
            part = _dot_nt(dslab, w_ref[dev])
            dmemn = part if dmemn is None else dmemn + part
        mv = mem_ref[...]
        dg_ref[...] = jnp.zeros_like(dg_ref)
        dg_ref[0:1, :] = jnp.sum(dmemn * (mv * _rms(mv)), axis=0, keepdims=True)

    assert C_KV == HEAD
    return _pcall(body, out_shape=[SDS((N_DEV, D, C_KV), BF), SDS((8, D), F32)],
                  compiler_params=_cp(), name=name)(mem, g, memn, dk, dv, wkv)


_TM_MIX = 512


def _mix_out(x, za, sb, o, proj, w4, bg, g_next, *, name):
    s = x.shape[0]
    tm = min(_TM_MIX, s)

    def body(x_ref, za_ref, sb_ref, o_ref, pg_ref, w4_ref, bg_ref, gn_ref,
             ya_ref, yb_ref, yc_ref, mg_ref, x1_ref, h_ref):
        ys = (_dot(za_ref[...], w4_ref[0]), _dot(sb_ref[...], w4_ref[1]), _dot(o_ref[...], w4_ref[2]))
        merged = None
        for j, (y, y_ref) in enumerate(zip(ys, (ya_ref, yb_ref, yc_ref))):
            y_ref[...] = y.astype(BF)
            gate = _sigmoid(pg_ref[:, j * D:(j + 1) * D].astype(F32) + bg_ref[:, j * D:(j + 1) * D])
            merged = gate * y if merged is None else merged + gate * y
        mg = merged.astype(BF)
        mg_ref[...] = mg
        x1 = x_ref[...] + _dot(mg, w4_ref[3])
        x1_ref[...] = x1
        h_ref[...] = ((x1 * _rms(x1)) * gn_ref[...]).astype(BF)

    row = lambda i: (i, 0)
    act = pl.BlockSpec((tm, D), row)
    return _pcall(
        body, grid=(s // tm,),
        in_specs=[act, act, act, act, pl.BlockSpec((tm, 3 * D), lambda i: (i, 2)),
                  pl.BlockSpec((4, D, D), lambda i: (0, 0, 0), pipeline_mode=pl.Buffered(1)), pl.BlockSpec((1, 3 * D), lambda i: (0, 0)),
                  pl.BlockSpec((1, D), lambda i: (0, 0))],
        out_specs=[act] * 6,
        out_shape=[SDS((s, D), BF)] * 4 + [SDS((s, D), F32), SDS((s, D), BF)],
        compiler_params=_cp(("parallel",)), name=name)(x, za, sb, o, proj, w4, bg, g_next)


def _mix_bwd(dxb, ya, yb, yc, proj, w4, bg, *, name):
    s = dxb.shape[0]
    tm = min(_TM_MIX, s)

    def body(dx_ref, ya_ref, yb_ref, yc_ref, pg_ref, w4_ref, bg_ref,
             dya_ref, dyb_ref, dyc_ref, dza_ref, dsb_ref, do_ref, dgt_ref, dbg_ref):
        @pl.when(pl.program_id(0) == 0)
        def _():
            dbg_ref[...] = jnp.zeros_like(dbg_ref)

        dm = _dot_nt(dx_ref[...], w4_ref[3])
        for j, (y_ref, dy_ref, din_ref) in enumerate(zip((ya_ref, yb_ref, yc_ref), (dya_ref, dyb_ref, dyc_ref),
                                                         (dza_ref, dsb_ref, do_ref))):
            cols = slice(j * D, (j + 1) * D)
            gate = _sigmoid(pg_ref[:, cols].astype(F32) + bg_ref[:, cols])
            dy = (dm * gate).astype(BF)
            dy_ref[...] = dy
            din_ref[...] = _dot_nt(dy, w4_ref[j]).astype(BF)
            dpre = dm * y_ref[...].astype(F32) * gate * (1.0 - gate)
            dgt_ref[:, cols] = dpre.astype(BF)
            dbg_ref[0:1, cols] += jnp.sum(dpre, axis=0, keepdims=True)

    row = lambda i: (i, 0)
    act = pl.BlockSpec((tm, D), row)
    return _pcall(
        body, grid=(s // tm,),
        in_specs=[act, act, act, act, pl.BlockSpec((tm, 3 * D), lambda i: (i, 2)),
                  pl.BlockSpec((4, D, D), lambda i: (0, 0, 0), pipeline_mode=pl.Buffered(1)), pl.BlockSpec((1, 3 * D), lambda i: (0, 0))],
        out_specs=[act] * 6 + [pl.BlockSpec((tm, 3 * D), lambda i: (i, 2)),
                               pl.BlockSpec((8, 3 * D), lambda i: (0, 0))],
        out_shape=[SDS((s, D), BF)] * 6 + [SDS((s, 9 * D), BF), SDS((8, 3 * D), F32)],
        compiler_params=_cp(("arbitrary",)), name=name)(dxb, ya, yb, yc, proj, w4, bg)


_PAIR = 2 * C_UP_P


def _row_selector(sub, first_cols):
    rows = np.arange(len(first_cols) * sub)
    col = np.asarray(first_cols)[rows // sub] + rows % sub
    return jnp.asarray(np.arange(2 * sub)[None, :] == col[:, None], BF)


def _ffn_act(u2, cw, *, name):
    s = u2.shape[0]
    tr = min(TR_FFN, s)
    sub = min(SUB, tr)
    sel = _row_selector(sub, [sub - (K_F - 1 - k) for k in range(K_F)])

    def body(cur, prev, sel_ref, w_ref, act_ref, c2_ref, xb, win):
        i = pl.program_id(1)
        before = prev[...]
        xb[0:sub, :] = jnp.where(i == 0, jnp.zeros_like(before), before)
        xb[sub:sub + tr, :] = cur[...]
        for r0 in range(0, tr, sub):
            win[...] = _dot(sel_ref[...], xb[r0:r0 + 2 * sub, :])
            for c in range(C_UP_P // LANE):
                gl = slice(LANE * c, LANE * c + LANE)
                ul = slice(C_UP_P + LANE * c, C_UP_P + LANE * c + LANE)
                gt = sum(w_ref[k:k + 1, gl] * win[k * sub:(k + 1) * sub, gl] for k in range(K_F))
                up = sum(w_ref[k:k + 1, ul] * win[k * sub:(k + 1) * sub, ul] for k in range(K_F))
                c2_ref[r0:r0 + sub, gl] = gt.astype(BF)
                c2_ref[r0:r0 + sub, ul] = up.astype(BF)
                act_ref[r0:r0 + sub, gl] = (gt * _sigmoid(gt) * up).astype(BF)

    return _pcall(
        body, grid=(4, s // tr),
        in_specs=[pl.BlockSpec((tr, _PAIR), lambda p, i: (i, p)),
                  pl.BlockSpec((sub, _PAIR), lambda p, i: (_halo_before(i, tr, sub), p)),
                  pl.BlockSpec(sel.shape, lambda p, i: (0, 0)),
                  pl.BlockSpec((8, _PAIR), lambda p, i: (0, p))],
        out_specs=[pl.BlockSpec((tr, C_UP_P), lambda p, i: (i, p)), pl.BlockSpec((tr, _PAIR), lambda p, i: (i, p))],
        out_shape=[SDS((s, FF_P), BF), SDS((s, 2 * FF_P), BF)],
        scratch_shapes=[pltpu.VMEM((sub + tr, _PAIR), BF), pltpu.VMEM((K_F * sub, _PAIR), F32)],
        compiler_params=_cp(("parallel", "parallel")), name=name)(u2, u2, sel, cw)


def _ffn_bwd(u2, c2, dact, cw, *, name):
    s = u2.shape[0]
    tr, h = min(TR_FFN, s), H_S
    sub = min(SUB, tr)
    n = s // tr
    sel = _row_selector(sub, [K_F - 1 - k for k in range(K_F)])

    def body(u_cur, c_cur, c_after, da_cur, da_after, sel_ref, w_ref, du_ref, dw_ref, dcb, win):
        i = pl.program_id(1)
        last = i == n - 1

        @pl.when(i == 0)
        def _():
            dw_ref[...] = jnp.zeros_like(dw_ref)

        def conv_grad(gt, up, da):
            gt, up, da = gt.astype(F32), up.astype(F32), da.astype(F32)
            sg = _sigmoid(gt)
            return (da * up * (sg * (1.0 + gt * (1.0 - sg)))).astype(BF), (da * (gt * sg)).astype(BF)

        for c in range(C_UP_P // LANE):
            gl = slice(LANE * c, LANE * c + LANE)
            ul = slice(C_UP_P + LANE * c, C_UP_P + LANE * c + LANE)
            for r0 in range(0, tr, sub):
                rows = slice(r0, r0 + sub)
                dcb[rows, gl], dcb[rows, ul] = conv_grad(c_cur[rows, gl], c_cur[rows, ul], da_cur[rows, gl])
            dg, du_ = conv_grad(c_after[:, gl], c_after[:, ul], da_after[:, gl])
            dcb[tr:tr + h, gl] = jnp.where(last, jnp.zeros_like(dg), dg)
            dcb[tr:tr + h, ul] = jnp.where(last, jnp.zeros_like(du_), du_)
        dcb[tr + h:tr + sub, :] = jnp.zeros((sub - h, _PAIR), BF)
        for r0 in range(0, tr, sub):
            win[...] = _dot(sel_ref[...], dcb[r0:r0 + 2 * sub, :])
            for c in range(_PAIR // LANE):
                sl = slice(LANE * c, LANE * c + LANE)
                u = u_cur[r0:r0 + sub, sl].astype(F32)
                du = None
                for k in range(K_F):
                    wk = win[k * sub:(k + 1) * sub, sl]
                    term = w_ref[k:k + 1, sl] * wk
                    du = term if du is None else du + term
                    dw_ref[k:k + 1, sl] += jnp.sum(wk * u, axis=0, keepdims=True)
                du_ref[r0:r0 + sub, sl] = du.astype(BF)

    return _pcall(
        body, grid=(4, n),
        in_specs=[pl.BlockSpec((tr, _PAIR), lambda p, i: (i, p)),
                  pl.BlockSpec((tr, _PAIR), lambda p, i: (i, p)),
                  pl.BlockSpec((h, _PAIR), lambda p, i: (_halo_after(i, tr, h, s), p)),
                  pl.BlockSpec((tr, C_UP_P), lambda p, i: (i, p)),
                  pl.BlockSpec((h, C_UP_P), lambda p, i: (_halo_after(i, tr, h, s), p)),
                  pl.BlockSpec(sel.shape, lambda p, i: (0, 0)),
                  pl.BlockSpec((8, _PAIR), lambda p, i: (0, p))],
        out_specs=[pl.BlockSpec((tr, _PAIR), lambda p, i: (i, p)), pl.BlockSpec((8, _PAIR), lambda p, i: (0, p))],
        out_shape=[SDS((s, 2 * FF_P), BF), SDS((8, 2 * FF_P), F32)],
        scratch_shapes=[pltpu.VMEM((tr + sub, _PAIR), BF), pltpu.VMEM((K_F * sub, _PAIR), F32)],
        compiler_params=_cp(("parallel", "arbitrary")), name=name)(u2, c2, c2, dact, dact, sel, cw)


def _relations():
    x, y, c = lax.axis_index("x"), lax.axis_index("y"), lax.axis_index("c")
    out = []
    for r in range(1, N_DEV):
        rx, ry, rc = (r >> 2) & 1, (r >> 1) & 1, r & 1
        out.append((r, (x ^ rx, y ^ ry, c ^ rc)))
    return out


def _my_index():
    return 4 * lax.axis_index("x") + 2 * lax.axis_index("y") + lax.axis_index("c")


def _slab(kind, ref, idx):
    if kind == "win":
        return ref.at[:, pl.ds(pl.multiple_of(idx * C_IN, LANE), C_IN)]
    if kind == "wup":
        return ref.at[:, pl.ds(pl.multiple_of(_up_slot(idx) * C_UP_P, LANE), C_UP_P)]
    if kind == "wupT":
        return ref.at[pl.ds(pl.multiple_of(_up_slot(idx) * C_UP_P, LANE), C_UP_P), :]
    if kind == "wkv":
        return ref.at[idx]
    if kind == "w4":
        return ref.at[:, pl.ds(pl.multiple_of(idx * R_O, 16), R_O), :]
    if kind == "wdn":
        return ref.at[pl.ds(pl.multiple_of(_dn_row(idx), 16), R_DN), :]
    assert kind == "cv"
    return ref.at[idx]


_WHOLE = {"win": ((D, 9 * D), BF), "wup": ((D, 2 * FF_P), BF), "wkv": ((N_DEV, D, C_KV), BF),
          "w4": ((4, D, D), BF), "wdn": ((FF_P, D), BF)}
_SHARD = {"win": (D, C_IN), "wup": (D, C_UP_P), "wupT": (C_UP_P, D), "wkv": (D, C_KV), "w4": (4, R_O, D),
          "wdn": (R_DN, D)}
HBM_SPEC = pl.BlockSpec(memory_space=pltpu.HBM)
SEM_SPEC = pl.BlockSpec(memory_space=pltpu.SEMAPHORE)
_DATAFLOW = pltpu.SideEffectType.DATAFLOW_SIDE_EFFECTING


def _scatter_maps(kinds):
    return ((lambda srcs, lands, a, idx: _slab(kinds[a], srcs[a], idx)),
            (lambda lands, a, idx: lands[a].at[idx]))


_SLOTTED = ("wkv", "cv")


def _own_slab_blocks(kind, shard_shape):
    if kind in ("win", "wup"):
        rows, slot = 256, (_up_slot if kind == "wup" else (lambda m: m))
        return (shard_shape[0] // rows, (rows, shard_shape[1]), (lambda i, me: (i, slot(me[0]))),
                (lambda i, me: (i, 0)), (lambda i, me: (me[0], i, 0)))
    if kind == "wupT":
        rows = 256
        steps = shard_shape[0] // rows
        return (steps, (rows, D), (lambda i, me: (_up_slot(me[0]) * steps + i, 0)), (lambda i, me: (i, 0)),
                (lambda i, me: (me[0], i, 0)))
    if kind == "w4":
        return (1, shard_shape, (lambda i, me: (0, me[0], 0)), (lambda i, me: (0, 0, 0)),
                (lambda i, me: (me[0], 0, 0, 0)))
    if kind == "wdn":
        rows = 32
        return (R_DN // rows, (rows, D), (lambda i, me: (_dn_row(me[0]) // rows + i, 0)), (lambda i, me: (i, 0)),
                (lambda i, me: (me[0], i, 0)))
    assert kind in _SLOTTED
    rows = min(256, shard_shape[0])
    return (shard_shape[0] // rows, (rows, shard_shape[1]), (lambda i, me: (me[0], i, 0)),
            (lambda i, me: (i, 0)), (lambda i, me: (me[0], i, 0)))


def _place_own(kind, src, out_sds, gather, me_arr, *, name):
    shard_shape = src.shape if gather else out_sds.shape[1:]
    steps, blk, whole_idx, shard_idx, staging_idx = _own_slab_blocks(kind, shard_shape)
    slotted = kind in _SLOTTED
    whole_spec = pl.BlockSpec(((None,) if slotted else ()) + tuple(blk), whole_idx)
    if gather:
        in_spec, out_spec = pl.BlockSpec(tuple(blk), shard_idx), whole_spec
    else:
        in_spec, out_spec = whole_spec, pl.BlockSpec((None,) + tuple(blk), staging_idx)
    zero_init = gather and kind == "wdn"

    def body(me_ref, src_ref, *rest):
        rest[-1][...] = src_ref[...].astype(rest[-1].dtype)

    operands = (me_arr, src) + ((jnp.zeros(out_sds.shape, out_sds.dtype),) if zero_init else ())
    return _pcall(
        body,
        grid_spec=pltpu.PrefetchScalarGridSpec(
            num_scalar_prefetch=1, grid=(steps,), in_specs=[in_spec] + ([ANY] if zero_init else []),
            out_specs=out_spec),
        out_shape=out_sds, input_output_aliases={2: 0} if zero_init else {},
        compiler_params=_cp(("arbitrary",)), name=name)(*operands)


def _peer_copies(n, src_of, dst_of, src_r, land_r, ssem, rsem):
    me = _my_index()
    out = []
    for r, peer in _relations():
        p_idx = 4 * peer[0] + 2 * peer[1] + peer[2]
        for a in range(n):
            def copy(src_idx, dst_idx, a=a, r=r, peer=peer):
                sem = a * (N_DEV - 1) + r - 1
                return pltpu.make_async_remote_copy(
                    src_ref=src_of(src_r, land_r, a, src_idx), dst_ref=dst_of(land_r, a, dst_idx),
                    send_sem=ssem.at[sem], recv_sem=rsem.at[sem], device_id=peer, device_id_type=MESH)
            out.append((functools.partial(copy, p_idx, me), functools.partial(copy, me, p_idx)))
    return out


def _exchange_start(srcs, lands, maps, after, *, name):
    n, ns = len(lands), len(srcs)
    src_of, dst_of = maps

    def body(*refs):
        src_r, land_r = refs[:ns], refs[ns:ns + n]
        ssem, rsem, token = refs[ns + n + 1], refs[ns + n + 2], refs[-1]
        for send, _ in _peer_copies(n, src_of, dst_of, src_r, land_r, ssem, rsem):
            send().start()
        token[...] = jnp.zeros_like(token)

    flight = list(srcs) + list(lands)
    outs = pl.pallas_call(
        body, name=name,
        out_shape=(pltpu.SemaphoreType.DMA((n * (N_DEV - 1),)), pltpu.SemaphoreType.DMA((n * (N_DEV - 1),)),
                   *[pltpu.HBM(a.shape, a.dtype) for a in flight], SDS((8, LANE), F32)),
        in_specs=[HBM_SPEC] * (ns + n) + [ANY],
        out_specs=(SEM_SPEC, SEM_SPEC, *[HBM_SPEC] * (ns + n), pl.BlockSpec(memory_space=pltpu.VMEM)),
        input_output_aliases={i: 2 + i for i in range(ns + n)},
        compiler_params=pltpu.CompilerParams(has_side_effects=_DATAFLOW),
    )(*[pltpu.with_memory_space_constraint(a, pltpu.HBM) for a in flight], after)
    return (outs[0], outs[1], list(outs[2:2 + ns + n]), ns), outs[-1]


def _exchange_wait(handle, maps, after, *, name):
    ssem, rsem, flight, ns = handle
    n = len(flight) - ns
    src_of, dst_of = maps

    def body(*refs):
        src_r, land_r, ssem_r, rsem_r = refs[:ns], refs[ns:ns + n], refs[ns + n], refs[ns + n + 1]
        for send, arrival in _peer_copies(n, src_of, dst_of, src_r, land_r, ssem_r, rsem_r):
            send().wait_send()
            arrival().wait_recv()

    outs = pl.pallas_call(
        body, name=name, out_shape=[pltpu.HBM(a.shape, a.dtype) for a in flight],
        in_specs=[HBM_SPEC] * (ns + n) + [SEM_SPEC, SEM_SPEC, ANY], out_specs=[HBM_SPEC] * (ns + n),
        input_output_aliases={i: i for i in range(ns + n)},
        compiler_params=pltpu.CompilerParams(has_side_effects=_DATAFLOW),
    )(*flight, ssem, rsem, after)
    return list(outs[ns:])


_SIBLING = 1
_ICI = (2, 4, 6)


def _rel_peer(r):
    x, y, c = lax.axis_index("x"), lax.axis_index("y"), lax.axis_index("c")
    peer = (x ^ ((r >> 2) & 1), y ^ ((r >> 1) & 1), c ^ (r & 1))
    return peer, 4 * peer[0] + 2 * peer[1] + peer[2]


def _rcopy(ref, ssem, rsem, peer):
    return pltpu.make_async_remote_copy(src_ref=ref, dst_ref=ref, send_sem=ssem, recv_sem=rsem, device_id=peer,
                                        device_id_type=MESH)


def _gather2_start(lands, kinds, after, *, name):
    n = len(lands)

    def body(*refs):
        land_r, (send1, recv_sib, recv_ici), token = refs[:n], refs[n + 1:n + 4], refs[-1]
        me = _my_index()
        for a in range(n):
            own = _slab(kinds[a], land_r[a], me)
            for j, r in enumerate((_SIBLING,) + _ICI):
                rsem = recv_sib.at[a] if r == _SIBLING else recv_ici.at[3 * a + j - 1]
                _rcopy(own, send1.at[4 * a + j], rsem, _rel_peer(r)[0]).start()
        token[...] = jnp.zeros_like(token)

    sems = [pltpu.SemaphoreType.DMA((4 * n,)), pltpu.SemaphoreType.DMA((n,)), pltpu.SemaphoreType.DMA((3 * n,))]
    outs = pl.pallas_call(
        body, name=name, out_shape=(*sems, *[pltpu.HBM(a.shape, a.dtype) for a in lands], SDS((8, LANE), F32)),
        in_specs=[HBM_SPEC] * n + [ANY],
        out_specs=(SEM_SPEC,) * 3 + (HBM_SPEC,) * n + (pl.BlockSpec(memory_space=pltpu.VMEM),),
        input_output_aliases={i: 3 + i for i in range(n)},
        compiler_params=pltpu.CompilerParams(has_side_effects=_DATAFLOW),
    )(*[pltpu.with_memory_space_constraint(a, pltpu.HBM) for a in lands], after)
    return dict(send1=outs[0], recv_sib=outs[1], recv_ici=outs[2], lands=list(outs[3:3 + n])), outs[-1]


def _gather2_forward(handle, kinds, after, *, name):
    lands = handle["lands"]
    n = len(lands)

    def body(*refs):
        land_r, recv_ici, (fwd_send, fwd_recv), token = refs[:n], refs[n], refs[n + 2:n + 4], refs[-1]
        sibling = _rel_peer(_SIBLING)[0]
        for a in range(n):
            for j, r in enumerate(_ICI):
                got = _slab(kinds[a], land_r[a], _rel_peer(r)[1])
                _rcopy(got, fwd_send.at[3 * a + j], recv_ici.at[3 * a + j], sibling).wait_recv()
                _rcopy(got, fwd_send.at[3 * a + j], fwd_recv.at[3 * a + j], sibling).start()
        token[...] = jnp.zeros_like(token)

    sems = [pltpu.SemaphoreType.DMA((3 * n,)), pltpu.SemaphoreType.DMA((3 * n,))]
    outs = pl.pallas_call(
        body, name=name, out_shape=(*sems, *[pltpu.HBM(a.shape, a.dtype) for a in lands], SDS((8, LANE), F32)),
        in_specs=[HBM_SPEC] * n + [SEM_SPEC, ANY],
        out_specs=(SEM_SPEC,) * 2 + (HBM_SPEC,) * n + (pl.BlockSpec(memory_space=pltpu.VMEM),),
        input_output_aliases={i: 2 + i for i in range(n)},
        compiler_params=pltpu.CompilerParams(has_side_effects=_DATAFLOW),
    )(*lands, handle["recv_ici"], after)
    return dict(handle, fwd_send=outs[0], fwd_recv=outs[1], lands=list(outs[2:2 + n])), outs[-1]


def _gather2_wait(handle, kinds, after, *, name):
    lands = handle["lands"]
    n = len(lands)

    def body(*refs):
        land_r, (send1, recv_sib, fwd_send, fwd_recv) = refs[:n], refs[n:n + 4]
        me = _my_index()
        sibling, sib_idx = _rel_peer(_SIBLING)
        for a in range(n):
            own = _slab(kinds[a], land_r[a], me)
            for j, r in enumerate((_SIBLING,) + _ICI):
                _rcopy(own, send1.at[4 * a + j], recv_sib.at[a], _rel_peer(r)[0]).wait_send()
            theirs = _slab(kinds[a], land_r[a], sib_idx)
            _rcopy(theirs, send1.at[4 * a], recv_sib.at[a], sibling).wait_recv()
            for j, r in enumerate(_ICI):
                passed_on = _slab(kinds[a], land_r[a], _rel_peer(r)[1])
                _rcopy(passed_on, fwd_send.at[3 * a + j], fwd_recv.at[3 * a + j], sibling).wait_send()
                arrived = _slab(kinds[a], land_r[a], _rel_peer(r ^ _SIBLING)[1])
                _rcopy(arrived, fwd_send.at[3 * a + j], fwd_recv.at[3 * a + j], sibling).wait_recv()

    outs = pl.pallas_call(
        body, name=name, out_shape=[pltpu.HBM(a.shape, a.dtype) for a in lands],
        in_specs=[HBM_SPEC] * n + [SEM_SPEC] * 4 + [ANY], out_specs=[HBM_SPEC] * n,
        input_output_aliases={i: i for i in range(n)},
        compiler_params=pltpu.CompilerParams(has_side_effects=_DATAFLOW),
    )(*lands, handle["send1"], handle["recv_sib"], handle["fwd_send"], handle["fwd_recv"], after)
    return list(outs)


def _sum_slots(gathered, *, name):
    def body(g_ref, out_ref):
        total = g_ref[0]
        for dev in range(1, N_DEV):
            total = total + g_ref[dev]
        out_ref[...] = total

    return _pcall(body, out_shape=SDS(gathered.shape[1:], F32), compiler_params=_cp(), name=name)(gathered)


def _adam(g, w, m, v):
    nm = ADAM_B1 * m + (1.0 - ADAM_B1) * g
    nv = ADAM_B2 * v + (1.0 - ADAM_B2) * (g * g)
    m_hat = nm / (1.0 - ADAM_B1 ** ADAM_STEP)
    v_hat = nv / (1.0 - ADAM_B2 ** ADAM_STEP)
    return -ADAM_LR * (m_hat / (jnp.sqrt(v_hat) + ADAM_EPS) + ADAM_WD * w), nm, nv


def _adamw_staged(st0, st1, w, m, v, *, name):
    _, rows, cols = w.shape
    st_cols = st0.shape[2]
    tr = max(t for t in range(16, 129, 16) if rows % t == 0)
    nr = rows // tr

    def body(s0_ref, s1_ref, w_ref, m_ref, v_ref, g_ref, d_ref, nm_ref, nv_ref):
        for layer, s_ref in enumerate((s0_ref, s1_ref)):
            @pl.when(pl.program_id(0) == layer)
            def _(s_ref=s_ref):
                total = s_ref[0, :, 0:cols].astype(F32)
                for dev in range(1, N_DEV):
                    total = total + s_ref[dev, :, 0:cols].astype(F32)
                g_ref[0] = total

        d_ref[0], nm_ref[0], nv_ref[0] = _adam(g_ref[0], w_ref[0], m_ref[0], v_ref[0])

    st_spec = lambda layer: pl.BlockSpec(
        (N_DEV, tr, st_cols), lambda l, i: (0, jnp.where(l == layer, i, (nr - 1) * (1 - layer)), 0))
    par = pl.BlockSpec((1, tr, cols), lambda l, i: (l, i, 0))
    return _pcall(
        body, grid=(DEPTH, nr), in_specs=[st_spec(0), st_spec(1), par, par, par], out_specs=[par] * 4,
        out_shape=[SDS(w.shape, F32)] * 4,
        compiler_params=_cp(("arbitrary", "arbitrary")), name=name)(st0, st1, w, m, v)


def _adamw_small(g, w, m, v, *, name):
    def body(g_ref, w_ref, m_ref, v_ref, d_ref, nm_ref, nv_ref):
        d_ref[...], nm_ref[...], nv_ref[...] = _adam(g_ref[...], w_ref[...], m_ref[...], v_ref[...])

    return _pcall(body, out_shape=[SDS(g.shape, F32)] * 3, compiler_params=_cp(), name=name)(g, w, m, v)


def _pack_rows(arrays):
    flat = jnp.concatenate([a.reshape(-1).astype(F32) for a in arrays])
    rows = -(-flat.shape[0] // (8 * D)) * 8
    return jnp.pad(flat, (0, rows * D - flat.shape[0])).reshape(rows, D)


def _unpack_rows(pack, like):
    flat = pack.reshape(-1)
    out, at = [], 0
    for a in like:
        out.append(flat[at:at + a.size].reshape(a.shape))
        at += a.size
    return out


def _layer_fwd(x, h, mem, win, mixer_weights, ffn_weights, after_up, small, g_next, tag):
    proj = _mm(h, win, tm=1024, tn=1536, name=f"proj_{tag}")
    wkv, w4, cw_a, cw_b, cw_f = mixer_weights(proj)
    za = _bra_fwd(proj, cw_a, name=f"bra_fwd_{tag}")
    cb = _brb_conv_fwd(proj, cw_b, small["conv_b_bias"], name=f"brb_conv_fwd_{tag}")
    sb = _ln_silu_fwd(cb, small["ln_b_g"], small["ln_b_b"], name=f"ln_silu_fwd_{tag}")
    memn, kv = _kv_prep(mem, small["norm_mem_g"], wkv, name=f"kv_prep_{tag}")
    o = _attn_fwd(proj, kv, name=f"attn_fwd_{tag}")
    ya, yb, yc, mg, x1, h2 = _mix_out(x, za, sb, o, proj, w4, small["b_gate"], small["norm_ffn_g"],
                                      name=f"mix_out_{tag}")
    wup, wdn = ffn_weights(h2)
    u2 = _mm(h2, wup, tm=1024, tn=1536, name=f"up_{tag}")
    token = after_up(u2)
    act, c2 = _ffn_act(u2, cw_f if token is None else _behind(cw_f, token), name=f"ffn_act_{tag}")
    x2, h_next = _mm_res_norm(act, wdn, x1, g_next, name=f"down_{tag}")
    saved = dict(x=x, h=h, proj=proj, za=za, cb=cb, sb=sb, memn=memn, kv=kv, o=o, ya=ya, yb=yb, yc=yc,
                 mg=mg, x1=x1, h2=h2, u2=u2, c2=c2, act=act)
    return x2, h_next, (win, wup, wkv, w4, wdn, cw_a, cw_b, cw_f), saved


def _behind(operand, token):
    return operand + token[0:1, 0:1]


def _layer_bwd(dx2, dx2b, mem, wts, small, sv, start, tag):
    win, wup, wkv, w4, wdn, cw_a, cw_b, cw_f = wts
    dact = _mm(dx2b, wdn, tb=True, tm=1024, tn=768, name=f"d_act_{tag}")
    dwdn = _mm(sv["act"], dx2b, ta=True, tm=768, tn=1024, name=f"dw_down_{tag}")
    du2, dcw_f = _ffn_bwd(sv["u2"], sv["c2"], dact, cw_f, name=f"ffn_bwd_{tag}")
    dwup_t = _mm(du2, sv["h2"], ta=True, tm=C_UP_P, tn=1024, name=f"dw_up_{tag}")
    token = start(("wdn", "wupT"), (dwdn, dwup_t), f"ffn_{tag}")
    dx1, dx1b, dg_ffn = _mm_nt_normbwd(du2, wup, sv["x1"], dx2, _behind(small["norm_ffn_g"], token),
                                       name=f"d_h2_{tag}")

    dya, dyb, dyc, dza, dsb, do, dproj, dbg = _mix_bwd(dx1b, sv["ya"], sv["yb"], sv["yc"], sv["proj"], w4,
                                                      small["b_gate"], name=f"mix_bwd_{tag}")
    dw4 = jnp.stack([
        _mm(a, b, ta=True, tm=1024, tn=512, name=f"dw_{nm}_{tag}")
        for nm, a, b in (("a_out", sv["za"], dya), ("b_out", sv["sb"], dyb), ("att_out", sv["o"], dyc),
                         ("o", sv["mg"], dx1b))])
    dq, dk, dv = _attn_bwd(sv["proj"], sv["kv"], do, name=f"attn_bwd_{tag}")
    dwkv, dg_mem = _kv_bwd(mem, small["norm_mem_g"], sv["memn"], dk, dv, wkv, name=f"kv_bwd_{tag}")
    token = start(("w4", "wkv"), (dw4, dwkv), f"mix_{tag}")
    dproj, dcw_a = _bra_bwd(sv["proj"], dza, _behind(cw_a, token), dproj, name=f"bra_bwd_{tag}")
    dcb, ln_sums = _ln_silu_bwd(sv["cb"], dsb, small["ln_b_g"], small["ln_b_b"], name=f"ln_silu_bwd_{tag}")
    dproj, dcw_b = _brb_conv_bwd(sv["proj"], dcb, dq, cw_b, dproj, name=f"brb_conv_bwd_{tag}")
    dwin = _mm(sv["h"], dproj, ta=True, tm=1024, tn=768, name=f"dw_in_{tag}")
    token = start(("win",), (dwin,), f"in_{tag}")
    dx, dxb, dg_mix = _mm_nt_normbwd(dproj, win, sv["x"], dx1, _behind(small["norm_mix_g"], token),
                                     tk=4608, name=f"d_h_{tag}")

    small_grads = [dg_mix[0:1], dg_mem[0:1], dbg[0:1].reshape(3, D), ln_sums[2:3], ln_sums[0:1], ln_sums[1:2],
                   dg_ffn[0:1], dcw_a[0:K_A], dcw_b[0:K_B], dcw_f[0:K_F].reshape(K_F * 2 * FF_P // D, D)]
    return dx, dxb, small_grads, token


_SMALL_ROWS = (1, 1, 3, 1, 1, 1, 1, K_A, K_B, K_F * 2 * FF_P // D)
_CV_ROWS = 48


def kernel(x, mem, norm_mix_g, norm_mem_g, w_in, b_gate, conv_a_w, w_a_out, conv_b_w, conv_b_bias, ln_b_g, ln_b_b, w_b_out, w_kv, w_att_out, w_o, norm_ffn_g, w_up, conv_ffn_w, w_down, norm_final_g, loss_target, m_norm_mix_g, m_norm_mem_g, m_w_in, m_b_gate, m_conv_a_w, m_w_a_out, m_conv_b_w, m_conv_b_bias, m_ln_b_g, m_ln_b_b, m_w_b_out, m_w_kv, m_w_att_out, m_w_o, m_norm_ffn_g, m_w_up, m_conv_ffn_w, m_w_down, m_norm_final_g, v_norm_mix_g, v_norm_mem_g, v_w_in, v_b_gate, v_conv_a_w, v_w_a_out, v_conv_b_w, v_conv_b_bias, v_ln_b_g, v_ln_b_b, v_w_b_out, v_w_kv, v_w_att_out, v_w_o, v_norm_ffn_g, v_w_up, v_conv_ffn_w, v_w_down, v_norm_final_g):
    me = _my_index()
    me_arr = me.astype(jnp.int32).reshape(1)
    x0, mem0, tgt = x.reshape(x.shape[1:]), mem.reshape(mem.shape[1:]), loss_target.reshape(x.shape[1:])
    up_pad = ((0, 0), (0, 0), (0, C_UP_P - C_UP))

    ag_groups = (("win",), ("wkv", "w4", "cv"), ("wup", "wdn"))
    kinds = ag_groups[0] + ag_groups[1] + ag_groups[2]
    smalls, ag_handles = [], []
    token = jnp.zeros((8, LANE), F32)
    for l in range(DEPTH):
        cv = jnp.zeros((_CV_ROWS, C_UP_P), F32)
        cv = cv.at[0:K_F, 0:C_UP].set(conv_ffn_w[l]).at[3:3 + K_A, 0:R_O].set(conv_a_w[l])
        cv = cv.at[8:8 + K_B, 0:R_O].set(conv_b_w[l])
        shards = dict(
            win=w_in[l], wup=jnp.pad(w_up[l], up_pad[1:]), wkv=w_kv[l],
            w4=jnp.stack([w_a_out[l], w_b_out[l], w_att_out[l], w_o[l]]), wdn=w_down[l], cv=cv)
        whole = dict({k: SDS(*_WHOLE[k]) for k in kinds if k != "cv"}, cv=SDS((N_DEV,) + cv.shape, F32))
        lands = {k: _place_own(k, shards[k], whole[k], True, me_arr, name=f"ag_own_{k}_l{l}") for k in kinds}
        per_layer = []
        for g, grp in enumerate(ag_groups):
            handle, token = _gather2_start([lands[k] for k in grp], grp, token, name=f"ag_start_l{l}_g{g}")
            per_layer.append(handle)
        ag_handles.append(per_layer)
        smalls.append(dict(
            norm_mix_g=norm_mix_g[l][None], norm_mem_g=norm_mem_g[l][None], b_gate=b_gate[l][None],
            conv_b_bias=conv_b_bias[l][None], ln_b_g=ln_b_g[l][None], ln_b_b=ln_b_b[l][None],
            norm_ffn_g=norm_ffn_g[l][None]))

    def forward_group(l, g, after):
        ag_handles[l][g], tok = _gather2_forward(ag_handles[l][g], ag_groups[g], after, name=f"ag_forward_l{l}_g{g}")
        return tok

    def group_of(l, g):
        def wait(after):
            if l == 0:
                after = forward_group(0, g, after)
            return _gather2_wait(ag_handles[l][g], ag_groups[g], after, name=f"ag_wait_l{l}_g{g}")
        return wait

    def mixer_weights(l):
        def wait(after):
            if l > 0:
                after = forward_group(l, 2, after)
            wkv, w4, cvg = group_of(l, 1)(after)
            cw_f = jnp.stack([cvg[d, 0:K_F, :] for d in UP_ORDER], axis=1).reshape(K_F, 2 * FF_P)
            cw_a = cvg[:, 3:3 + K_A, 0:R_O].transpose(1, 0, 2).reshape(K_A, D)
            cw_b = cvg[:, 8:8 + K_B, 0:R_O].transpose(1, 0, 2).reshape(K_B, D)
            return (wkv, w4, jnp.pad(cw_a, ((0, 8 - K_A), (0, 0))), jnp.pad(cw_b, ((0, 32 - K_B), (0, 0))),
                    jnp.pad(cw_f, ((0, 8 - K_F), (0, 0))))
        return wait

    wts, saved = [], []
    xs = x0
    h = _rms_fwd(xs, smalls[0]["norm_mix_g"], name="rms_fwd")
    behind = forward_group(0, 0, token)

    def next_layer_forwarding(l):
        def hook(after):
            return None if l + 1 == DEPTH else forward_group(l + 1, 0, after)
        return hook

    for l in range(DEPTH):
        g_next = smalls[l + 1]["norm_mix_g"] if l + 1 < DEPTH else norm_final_g[None]
        if l > 0:
            behind = forward_group(l, 1, behind)
        (win,) = _gather2_wait(ag_handles[l][0], ag_groups[0], behind, name=f"ag_wait_l{l}_g0")
        xs, h, w_l, sv = _layer_fwd(xs, h, mem0, win, mixer_weights(l), group_of(l, 2), next_layer_forwarding(l),
                                    smalls[l], g_next, f"l{l}")
        behind = h
        wts.append(w_l)
        saved.append(sv)
    dx, dxb, head_sums = _loss_head(xs, tgt, norm_final_g[None], name="loss_head")

    rs_handles = []
    small_grads = [None] * DEPTH

    def start_scatter(grp, arrays, name):
        maps = _scatter_maps(grp)
        lands = [_place_own(k, a, SDS((N_DEV,) + _SHARD[k], BF), False, me_arr, name=f"rs_own_{k}_{name}")
                 for k, a in zip(grp, arrays)]
        handle, tok = _exchange_start(list(arrays), lands, maps, rs_handles[-1][2] if rs_handles else head_sums,
                                      name=f"rs_start_{name}")
        rs_handles.append((grp, handle, tok, name))
        return tok

    for l in reversed(range(DEPTH)):
        dx, dxb, small_grads[l], token = _layer_bwd(dx, dxb, mem0, wts[l], smalls[l], saved[l], start_scatter,
                                                    f"l{l}")

    pack = jnp.concatenate(small_grads[0] + small_grads[1] + [head_sums[1:2], head_sums[0:1]], axis=0)
    pack = jnp.pad(pack, ((0, -pack.shape[0] % 8), (0, 0)))
    small_maps = (lambda srcs, lands, a, idx: srcs[a]), (lambda lands, a, idx: lands[a].at[idx])
    small_land = _place_own("cv", pack, SDS((N_DEV,) + pack.shape, F32), True, me_arr, name="small_own")
    small_handle, small_token = _exchange_start([pack], [small_land], small_maps, dx, name="small_start")

    staged = [dict() for _ in range(DEPTH)]
    for grp, handle, _, name in rs_handles[:-1]:
        staged[int(name[-1])].update(zip(grp, _exchange_wait(handle, _scatter_maps(grp), small_token,
                                                             name=f"rs_wait_{name}")))

    def big_update(kind, w, m, v, name):
        return _adamw_staged(staged[0][kind], staged[1][kind], w, m, v, name=name)

    r_up = [jnp.swapaxes(a, 1, 2) for a in big_update(
        "wupT", jnp.swapaxes(w_up, 1, 2), jnp.swapaxes(m_w_up, 1, 2), jnp.swapaxes(v_w_up, 1, 2), "adamw_w_up")]
    r_kv = big_update("wkv", w_kv, m_w_kv, v_w_kv, "adamw_w_kv")
    r_dn = big_update("wdn", w_down, m_w_down, v_w_down, "adamw_w_down")

    def four(a, b, c, d_):
        return jnp.stack([a, b, c, d_], axis=1).reshape(DEPTH, 4 * R_O, D)

    r_4 = _adamw_staged(
        staged[0]["w4"].reshape(N_DEV, 4 * R_O, D), staged[1]["w4"].reshape(N_DEV, 4 * R_O, D),
        four(w_a_out, w_b_out, w_att_out, w_o), four(m_w_a_out, m_w_b_out, m_w_att_out, m_w_o),
        four(v_w_a_out, v_w_b_out, v_w_att_out, v_w_o), name="adamw_w_out")
    grp, handle, _, name = rs_handles[-1]
    staged[0].update(zip(grp, _exchange_wait(handle, _scatter_maps(grp), r_4[0], name=f"rs_wait_{name}")))
    r_in = big_update("win", w_in, m_w_in, v_w_in, "adamw_w_in")
    r_a, r_b, r_att, r_o = ([a.reshape(DEPTH, 4, R_O, D)[:, j] for a in r_4] for j in range(4))

    (gathered,) = _exchange_wait(small_handle, small_maps, r_in[0], name="small_wait")
    total = _sum_slots(gathered, name="small_sum")
    per_layer = sum(_SMALL_ROWS)
    parts = []
    for l in range(DEPTH):
        at, one = l * per_layer, []
        for rows in _SMALL_ROWS:
            one.append(total[at:at + rows])
            at += rows
        parts.append(one)
    g_final = total[DEPTH * per_layer]
    loss = 0.5 / D * jnp.sum(total[DEPTH * per_layer + 1])

    def both(i):
        return jnp.stack([parts[0][i], parts[1][i]])

    g_norm_mix, g_norm_mem = both(0)[:, 0], both(1)[:, 0]
    g_b_gate = both(2).reshape(DEPTH, 3 * D)
    g_cbias, g_lng, g_lnb, g_norm_ffn = both(3)[:, 0], both(4)[:, 0], both(5)[:, 0], both(6)[:, 0]
    g_conv_a = lax.dynamic_slice_in_dim(both(7), me * R_O, R_O, axis=2)
    g_conv_b = lax.dynamic_slice_in_dim(both(8), me * R_O, R_O, axis=2)
    g_conv_f = lax.dynamic_slice_in_dim(both(9).reshape(DEPTH, K_F, 2 * FF_P), _up_slot(me) * C_UP_P, C_UP, axis=2)

    small_g = [g_norm_mix, g_norm_mem, g_b_gate, g_conv_a, g_conv_b, g_cbias, g_lng, g_lnb, g_norm_ffn, g_conv_f,
               g_final]
    small_w = [norm_mix_g, norm_mem_g, b_gate, conv_a_w, conv_b_w, conv_b_bias, ln_b_g, ln_b_b, norm_ffn_g,
               conv_ffn_w, norm_final_g]
    small_m = [m_norm_mix_g, m_norm_mem_g, m_b_gate, m_conv_a_w, m_conv_b_w, m_conv_b_bias, m_ln_b_g, m_ln_b_b,
               m_norm_ffn_g, m_conv_ffn_w, m_norm_final_g]
    small_v = [v_norm_mix_g, v_norm_mem_g, v_b_gate, v_conv_a_w, v_conv_b_w, v_conv_b_bias, v_ln_b_g, v_ln_b_b,
               v_norm_ffn_g, v_conv_ffn_w, v_norm_final_g]
    upd = _adamw_small(_pack_rows(small_g), _pack_rows(small_w), _pack_rows(small_m), _pack_rows(small_v),
                       name="adamw_small")
    s_d, s_m, s_v = (_unpack_rows(p, small_w) for p in upd)
    (d_norm_mix, d_norm_mem, d_b_gate, d_conv_a, d_conv_b, d_cbias, d_lng, d_lnb, d_norm_ffn, d_conv_f,
     d_final) = s_d
    (nm_norm_mix, nm_norm_mem, nm_b_gate, nm_conv_a, nm_conv_b, nm_cbias, nm_lng, nm_lnb, nm_norm_ffn, nm_conv_f,
     nm_final) = s_m
    (nv_norm_mix, nv_norm_mem, nv_b_gate, nv_conv_a, nv_conv_b, nv_cbias, nv_lng, nv_lnb, nv_norm_ffn, nv_conv_f,
     nv_final) = s_v

    grads = [g_norm_mix, g_norm_mem, r_in[0], g_b_gate, g_conv_a, r_a[0], g_conv_b, g_cbias, g_lng, g_lnb, r_b[0],
             r_kv[0], r_att[0], r_o[0], g_norm_ffn, r_up[0], g_conv_f, r_dn[0], g_final]
    deltas = [d_norm_mix, d_norm_mem, r_in[1], d_b_gate, d_conv_a, r_a[1], d_conv_b, d_cbias, d_lng, d_lnb, r_b[1],
              r_kv[1], r_att[1], r_o[1], d_norm_ffn, r_up[1], d_conv_f, r_dn[1], d_final]
    new_m = [nm_norm_mix, nm_norm_mem, r_in[2], nm_b_gate, nm_conv_a, r_a[2], nm_conv_b, nm_cbias, nm_lng, nm_lnb,
             r_b[2], r_kv[2], r_att[2], r_o[2], nm_norm_ffn, r_up[2], nm_conv_f, r_dn[2], nm_final]
    new_v = [nv_norm_mix, nv_norm_mem, r_in[3], nv_b_gate, nv_conv_a, r_a[3], nv_conv_b, nv_cbias, nv_lng, nv_lnb,
             r_b[3], r_kv[3], r_att[3], r_o[3], nv_norm_ffn, r_up[3], nv_conv_f, r_dn[3], nv_final]
    return (loss, dx[None], *grads, *deltas, *new_m, *new_v)
```

```python
import functools

import jax
import jax.numpy as jnp
import numpy as np
from jax import lax
from jax.experimental import pallas as pl
from jax.experimental.pallas import tpu as pltpu

F32 = jnp.float32
BF = jnp.bfloat16
SDS = jax.ShapeDtypeStruct
MESH = pl.DeviceIdType.MESH
ANY = pl.BlockSpec(memory_space=pl.ANY)

N_DEV = 8
DEPTH = 2
D = 1024
N_HEADS = 4
HEAD = D // N_HEADS
D_FF = 2816
K_A, K_B, K_F = 3, 31, 3
NORM_EPS = 1e-6

C_IN = 9 * D // N_DEV
C_KV = 2 * D // N_DEV
C_UP = 2 * D_FF // N_DEV
LANE = 128
C_UP_P = -(-C_UP // LANE) * LANE
FF_P = 4 * C_UP_P
R_O = D // N_DEV
R_DN = D_FF // N_DEV

VMEM_LIMIT = 56 * 1024 * 1024
TM = 512
TR = 512
TR_FFN = 1024
SUB = 128
H_S, H_L = 16, 32

ADAM_LR, ADAM_B1, ADAM_B2, ADAM_EPS, ADAM_WD, ADAM_STEP = 0.001, 0.9, 0.999, 1e-08, 0.01, 10

UP_ORDER = (0, 4, 1, 5, 2, 6, 3, 7)


def _pcall(body, **kw):
    return pl.pallas_call(body, **kw)


def _cp(sem=None, **kw):
    return pltpu.CompilerParams(dimension_semantics=sem, vmem_limit_bytes=VMEM_LIMIT, **kw)


def _dot(a, b):
    return jnp.dot(a, b, preferred_element_type=F32)


def _dot_nt(a, b):
    return lax.dot_general(a, b, (((1,), (1,)), ((), ())), preferred_element_type=F32)


def _dot_tn(a, b):
    return lax.dot_general(a, b, (((0,), (0,)), ((), ())), preferred_element_type=F32)


def _sigmoid(z):
    return 1.0 / (1.0 + jnp.exp(-z))


def _rms(xv):
    return lax.rsqrt(jnp.mean(xv * xv, axis=-1, keepdims=True) + NORM_EPS)


def _up_slot(idx):
    return jnp.where(idx < 4, 2 * idx, 2 * (idx - 4) + 1)


def _dn_row(idx):
    return C_UP_P * (idx // 2) + R_DN * (idx % 2)


def _mm(a, b, *, ta=False, tb=False, out_dtype=BF, tm=TM, tn=512, tk=None, name):
    m, k_dim = (a.shape[1], a.shape[0]) if ta else a.shape
    n = b.shape[0] if tb else b.shape[1]
    tm, tn = min(tm, m), min(tn, n)
    tk = k_dim if tk is None else min(tk, k_dim)
    nk = k_dim // tk
    assert m % tm == 0 and n % tn == 0 and k_dim % tk == 0
    dims = (((0 if ta else 1,), (1 if tb else 0,)), ((), ()))

    def body(a_ref, b_ref, o_ref, *scratch):
        part = lax.dot_general(a_ref[...], b_ref[...], dims, preferred_element_type=F32)
        if nk == 1:
            o_ref[...] = part.astype(o_ref.dtype)
            return
        acc = scratch[0]
        k = pl.program_id(2)

        @pl.when(k == 0)
        def _():
            acc[...] = part

        @pl.when(k > 0)
        def _():
            acc[...] += part

        @pl.when(k == nk - 1)
        def _():
            o_ref[...] = acc[...].astype(o_ref.dtype)

    a_spec = pl.BlockSpec((tk, tm), lambda i, j, k: (k, i)) if ta else pl.BlockSpec((tm, tk), lambda i, j, k: (i, k))
    b_spec = pl.BlockSpec((tn, tk), lambda i, j, k: (j, k)) if tb else pl.BlockSpec((tk, tn), lambda i, j, k: (k, j))
    return _pcall(
        body, grid=(m // tm, n // tn, nk), in_specs=[a_spec, b_spec],
        out_specs=pl.BlockSpec((tm, tn), lambda i, j, k: (i, j)),
        out_shape=SDS((m, n), out_dtype),
        scratch_shapes=[pltpu.VMEM((tm, tn), F32)] if nk > 1 else [],
        compiler_params=_cp(("parallel", "parallel", "arbitrary")), name=name)(a, b)


def _mm_res_norm(a, w, x, g, *, name):
    s, k_dim = a.shape
    tm = min(2 * TM, s)

    def body(a_ref, w_ref, x_ref, g_ref, xo_ref, h_ref):
        xo = x_ref[...] + _dot(a_ref[...], w_ref[...])
        xo_ref[...] = xo
        h_ref[...] = ((xo * _rms(xo)) * g_ref[...]).astype(BF)

    return _pcall(
        body, grid=(s // tm,),
        in_specs=[pl.BlockSpec((tm, k_dim), lambda i: (i, 0)),
                  pl.BlockSpec((k_dim, D), lambda i: (0, 0), pipeline_mode=pl.Buffered(1)),
                  pl.BlockSpec((tm, D), lambda i: (i, 0)), pl.BlockSpec((1, D), lambda i: (0, 0))],
        out_specs=[pl.BlockSpec((tm, D), lambda i: (i, 0))] * 2,
        out_shape=[SDS((s, D), F32), SDS((s, D), BF)],
        compiler_params=_cp(("parallel",)), name=name)(a, w, x, g)


def _mm_nt_normbwd(da, w, x, dres, g, *, tk=None, name):
    s, k_dim = da.shape
    tm = min(TM, s)
    tk = k_dim if tk is None else tk
    nk = k_dim // tk
    assert k_dim % tk == 0

    def body(da_ref, w_ref, x_ref, dres_ref, g_ref, dx_ref, dxb_ref, dg_ref, *scratch):
        i, k = pl.program_id(0), pl.program_id(1)
        part = _dot_nt(da_ref[...], w_ref[...])
        if nk > 1:
            acc = scratch[0]

            @pl.when(k == 0)
            def _():
                acc[...] = part

            @pl.when(k > 0)
            def _():
                acc[...] += part

        @pl.when((i == 0) & (k == 0))
        def _():
            dg_ref[...] = jnp.zeros_like(dg_ref)

        @pl.when(k == nk - 1)
        def _():
            dh = acc[...] if nk > 1 else part
            xv = x_ref[...]
            r = _rms(xv)
            xn = xv * r
            dg_ref[0:1, :] += jnp.sum(dh * xn, axis=0, keepdims=True)
            dxn = dh * g_ref[...]
            dx = dres_ref[...] + r * (dxn - xn * jnp.mean(dxn * xn, axis=-1, keepdims=True))
            dx_ref[...] = dx
            dxb_ref[...] = dx.astype(BF)

    row = lambda i, k: (i, 0)
    w_spec = (pl.BlockSpec((D, tk), lambda i, k: (0, k)) if nk > 1 else
              pl.BlockSpec((D, tk), lambda i, k: (0, 0), pipeline_mode=pl.Buffered(1)))
    return _pcall(
        body, grid=(s // tm, nk),
        in_specs=[pl.BlockSpec((tm, tk), lambda i, k: (i, k)), w_spec,
                  pl.BlockSpec((tm, D), row), pl.BlockSpec((tm, D), row), pl.BlockSpec((1, D), lambda i, k: (0, 0))],
        out_specs=[pl.BlockSpec((tm, D), row), pl.BlockSpec((tm, D), row), pl.BlockSpec((8, D), lambda i, k: (0, 0))],
        out_shape=[SDS((s, D), F32), SDS((s, D), BF), SDS((8, D), F32)],
        scratch_shapes=[pltpu.VMEM((tm, D), F32)] if nk > 1 else [],
        compiler_params=_cp(("arbitrary", "arbitrary")), name=name)(da, w, x, dres, g)


def _rms_fwd(x, g, *, name):
    s = x.shape[0]
    tm = min(TM, s)

    def body(x_ref, g_ref, h_ref):
        xv = x_ref[...]
        h_ref[...] = ((xv * _rms(xv)) * g_ref[...]).astype(BF)

    return _pcall(
        body, grid=(s // tm,),
        in_specs=[pl.BlockSpec((tm, D), lambda i: (i, 0)), pl.BlockSpec((1, D), lambda i: (0, 0))],
        out_specs=pl.BlockSpec((tm, D), lambda i: (i, 0)), out_shape=SDS((s, D), BF),
        compiler_params=_cp(("parallel",)), name=name)(x, g)


def _loss_head(x, tgt, g, *, name):
    s = x.shape[0]
    tm = min(TM, s)

    def body(x_ref, t_ref, g_ref, dx_ref, dxb_ref, sums_ref):
        @pl.when(pl.program_id(0) == 0)
        def _():
            sums_ref[...] = jnp.zeros_like(sums_ref)

        xv = x_ref[...]
        r = _rms(xv)
        xn = xv * r
        diff = xn * g_ref[...] - t_ref[...]
        sums_ref[0:1, :] += jnp.sum(diff * diff, axis=0, keepdims=True)
        dy = diff * (1.0 / D)
        sums_ref[1:2, :] += jnp.sum(dy * xn, axis=0, keepdims=True)
        dxn = dy * g_ref[...]
        dx = r * (dxn - xn * jnp.mean(dxn * xn, axis=-1, keepdims=True))
        dx_ref[...] = dx
        dxb_ref[...] = dx.astype(BF)

    row = lambda i: (i, 0)
    return _pcall(
        body, grid=(s // tm,),
        in_specs=[pl.BlockSpec((tm, D), row), pl.BlockSpec((tm, D), row), pl.BlockSpec((1, D), lambda i: (0, 0))],
        out_specs=[pl.BlockSpec((tm, D), row), pl.BlockSpec((tm, D), row), pl.BlockSpec((8, D), lambda i: (0, 0))],
        out_shape=[SDS((s, D), F32), SDS((s, D), BF), SDS((8, D), F32)],
        compiler_params=_cp(("arbitrary",)), name=name)(x, tgt, g)


def _halo_before(i, tr, h):
    return jnp.maximum(i * (tr // h) - 1, 0)


def _halo_after(i, tr, h, s):
    return jnp.minimum((i + 1) * (tr // h), s // h - 1)


def _taps(buf, w_ref, sl, k_w, base, rows):
    acc = None
    for k in range(k_w):
        t = w_ref[k:k + 1, sl] * buf[base + k:base + k + rows, sl]
        acc = t if acc is None else acc + t
    return acc


def _taps_rev(buf, w_ref, sl, k_w, base, rows):
    acc = None
    for k in range(k_w):
        t = w_ref[k:k + 1, sl] * buf[base + k_w - 1 - k:base + k_w - 1 - k + rows, sl]
        acc = t if acc is None else acc + t
    return acc


def _tap_grads(dw_ref, dc, buf, sl, k_w, base, rows):
    for k in range(k_w):
        dw_ref[k:k + 1, sl] += jnp.sum(dc * buf[base + k:base + k + rows, sl], axis=0, keepdims=True)


def _bra_fwd(proj, cw, *, name):
    s = proj.shape[0]
    tr, h = min(TR, s), H_S
    sub = min(SUB, tr)

    def body(cur, halo, w_ref, za_ref, cvb):
        i = pl.program_id(0)
        hv = halo[:, D:2 * D].astype(F32) * halo[:, 2 * D:3 * D].astype(F32)
        cvb[0:h, :] = jnp.where(i == 0, 0.0, hv)
        cvb[h:h + tr, :] = cur[:, D:2 * D].astype(F32) * cur[:, 2 * D:3 * D].astype(F32)
        for c in range(D // LANE):
            sl = slice(LANE * c, LANE * c + LANE)
            ca = _taps(cvb, w_ref, sl, K_A, h - (K_A - 1), tr)
            za_ref[:, sl] = (cur[:, sl].astype(F32) * ca).astype(BF)

    return _pcall(
        body, grid=(s // tr,),
        in_specs=[pl.BlockSpec((tr, 3 * D), lambda i: (i, 0)),
                  pl.BlockSpec((h, 3 * D), lambda i: (_halo_before(i, tr, h), 0)),
                  pl.BlockSpec((8, D), lambda i: (0, 0))],
        out_specs=pl.BlockSpec((tr, D), lambda i: (i, 0)), out_shape=SDS((s, D), BF),
        scratch_shapes=[pltpu.VMEM((h + tr, D), F32)],
        compiler_params=_cp(("parallel",)), name=name)(proj, proj, cw)


def _bra_bwd(proj, dza, cw, dproj, *, name):
    s = proj.shape[0]
    tr, h = min(TR, s), H_S
    sub = min(SUB, tr)
    n = s // tr

    def body(before, cur, after, dz_cur, dz_after, w_ref, dproj_in, da_ref, dw_ref, cvb, dcab):
        del dproj_in
        i = pl.program_id(0)

        @pl.when(i == 0)
        def _():
            dw_ref[...] = jnp.zeros_like(dw_ref)

        first, last = i == 0, i == n - 1
        cvb[0:h, :] = jnp.where(first, 0.0, before[:, D:2 * D].astype(F32) * before[:, 2 * D:3 * D].astype(F32))
        cvb[h:h + tr, :] = cur[:, D:2 * D].astype(F32) * cur[:, 2 * D:3 * D].astype(F32)
        dcab[0:tr, :] = dz_cur[...].astype(F32) * cur[:, 0:D].astype(F32)
        dcab[tr:tr + h, :] = jnp.where(last, 0.0, dz_after[...].astype(F32) * after[:, 0:D].astype(F32))
        for c in range(D // LANE):
            sl = slice(LANE * c, LANE * c + LANE)
            gl, vl = slice(D + LANE * c, D + LANE * c + LANE), slice(2 * D + LANE * c, 2 * D + LANE * c + LANE)
            for r0 in range(0, tr, sub):
                rows = slice(r0, r0 + sub)
                ca = _taps(cvb, w_ref, sl, K_A, h - (K_A - 1) + r0, sub)
                da_ref[rows, sl] = (dz_cur[rows, sl].astype(F32) * ca).astype(BF)
                dcv = _taps_rev(dcab, w_ref, sl, K_A, r0, sub)
                da_ref[rows, gl] = (dcv * cur[rows, vl].astype(F32)).astype(BF)
                da_ref[rows, vl] = (dcv * cur[rows, gl].astype(F32)).astype(BF)
                _tap_grads(dw_ref, dcab[rows, sl], cvb, sl, K_A, h - (K_A - 1) + r0, sub)

    return _pcall(
        body, grid=(n,),
        in_specs=[pl.BlockSpec((h, 3 * D), lambda i: (_halo_before(i, tr, h), 0)),
                  pl.BlockSpec((tr, 3 * D), lambda i: (i, 0)),
                  pl.BlockSpec((h, 3 * D), lambda i: (_halo_after(i, tr, h, s), 0)),
                  pl.BlockSpec((tr, D), lambda i: (i, 0)),
                  pl.BlockSpec((h, D), lambda i: (_halo_after(i, tr, h, s), 0)),
                  pl.BlockSpec((8, D), lambda i: (0, 0)), ANY],
        out_specs=[pl.BlockSpec((tr, 3 * D), lambda i: (i, 0)), pl.BlockSpec((8, D), lambda i: (0, 0))],
        out_shape=[SDS(dproj.shape, BF), SDS((8, D), F32)], input_output_aliases={6: 0},
        scratch_shapes=[pltpu.VMEM((h + tr, D), F32), pltpu.VMEM((tr + h, D), F32)],
        compiler_params=_cp(("arbitrary",)), name=name)(proj, proj, proj, dza, dza, cw, dproj)


_U_COL, _UG_COL = 3, 4


def _brb_conv_fwd(proj, cw, bias, *, name):
    s = proj.shape[0]
    tr, h = min(TR, s), H_L
    sub = min(SUB, tr)

    def body(u_cur, ug_cur, u_halo, ug_halo, w_ref, b_ref, cb_ref, glb, shifted):
        i = pl.program_id(0)
        glb[0:h, :] = jnp.where(i == 0, 0.0, u_halo[...].astype(F32) * _sigmoid(ug_halo[...].astype(F32)))
        glb[h:h + tr, :] = u_cur[...].astype(F32) * _sigmoid(ug_cur[...].astype(F32))
        for c in range(D // LANE):
            sl = slice(LANE * c, LANE * c + LANE)
            for r in range(1, 8):
                shifted[r] = glb[8 - r:8 - r + tr + 24, sl]
            for r0 in range(0, tr, sub):
                acc = None
                for k in range(K_B):
                    q, r = divmod(K_B - 1 - k, 8)
                    at = r0 - 8 * q
                    win = shifted[r, 24 + at:24 + at + sub, :] if r else glb[h + at:h + at + sub, sl]
                    term = w_ref[k:k + 1, sl] * win
                    acc = term if acc is None else acc + term
                cb_ref[r0:r0 + sub, sl] = (acc + b_ref[:, sl]).astype(BF)

    return _pcall(
        body, grid=(s // tr,),
        in_specs=[pl.BlockSpec((tr, D), lambda i: (i, _U_COL)), pl.BlockSpec((tr, D), lambda i: (i, _UG_COL)),
                  pl.BlockSpec((h, D), lambda i: (_halo_before(i, tr, h), _U_COL)),
                  pl.BlockSpec((h, D), lambda i: (_halo_before(i, tr, h), _UG_COL)),
                  pl.BlockSpec((32, D), lambda i: (0, 0)), pl.BlockSpec((1, D), lambda i: (0, 0))],
        out_specs=pl.BlockSpec((tr, D), lambda i: (i, 0)), out_shape=SDS((s, D), BF),
        scratch_shapes=[pltpu.VMEM((h + tr, D), F32), pltpu.VMEM((8, tr + 24, LANE), F32)],
        compiler_params=_cp(("parallel",)), name=name)(proj, proj, proj, proj, cw, bias)


def _brb_conv_bwd(proj, dcb, dq, cw, dproj, *, name):
    s = proj.shape[0]
    tr, h = min(TR, s), H_L
    sub = min(SUB, tr)
    n = s // tr
    nb = -(-(tr + 24) // sub)
    sel = _row_selector(sub, list(range(8)))

    def body(u_cur, ug_cur, d_cur, d_after, dq_ref, sel_ref, w_ref, dproj_in, db_ref, dw_ref, dcbb, shifted):
        del dproj_in
        i = pl.program_id(0)

        @pl.when(i == 0)
        def _():
            dw_ref[...] = jnp.zeros_like(dw_ref)

        db_ref[:, 2 * D:3 * D] = dq_ref[...]
        after = d_after[...]
        dcbb[0:tr, :] = d_cur[...]
        dcbb[tr:tr + h, :] = jnp.where(i == n - 1, jnp.zeros_like(after), after)
        dcbb[tr + h:(nb + 1) * sub, :] = jnp.zeros(((nb + 1) * sub - h - tr, D), BF)
        for c in range(D // LANE):
            sl = slice(LANE * c, LANE * c + LANE)
            for blk in range(nb):
                res = _dot(sel_ref[...], dcbb[blk * sub:(blk + 2) * sub, sl])
                for r in range(8):
                    shifted[r, blk * sub:(blk + 1) * sub, :] = res[r * sub:(r + 1) * sub]
            for r0 in range(0, tr, sub):
                u = u_cur[r0:r0 + sub, sl].astype(F32)
                sg = _sigmoid(ug_cur[r0:r0 + sub, sl].astype(F32))
                glu = u * sg
                dglu = None
                for k in range(K_B):
                    q, r = divmod(K_B - 1 - k, 8)
                    at = r0 + 8 * q
                    win = shifted[r, at:at + sub, :]
                    term = w_ref[k:k + 1, sl] * win
                    dglu = term if dglu is None else dglu + term
                    dw_ref[k:k + 1, sl] += jnp.sum(win * glu, axis=0, keepdims=True)
                db_ref[r0:r0 + sub, sl] = (dglu * sg).astype(BF)
                db_ref[r0:r0 + sub, D + LANE * c:D + LANE * c + LANE] = (dglu * u * sg * (1.0 - sg)).astype(BF)

    return _pcall(
        body, grid=(n,),
        in_specs=[pl.BlockSpec((tr, D), lambda i: (i, _U_COL)), pl.BlockSpec((tr, D), lambda i: (i, _UG_COL)),
                  pl.BlockSpec((tr, D), lambda i: (i, 0)),
                  pl.BlockSpec((h, D), lambda i: (_halo_after(i, tr, h, s), 0)),
                  pl.BlockSpec((tr, D), lambda i: (i, 0)),
                  pl.BlockSpec(sel.shape, lambda i: (0, 0)),
                  pl.BlockSpec((32, D), lambda i: (0, 0)), ANY],
        out_specs=[pl.BlockSpec((tr, 3 * D), lambda i: (i, 1)), pl.BlockSpec((32, D), lambda i: (0, 0))],
        out_shape=[SDS(dproj.shape, BF), SDS((32, D), F32)], input_output_aliases={7: 0},
        scratch_shapes=[pltpu.VMEM(((nb + 1) * sub, D), BF), pltpu.VMEM((8, nb * sub, LANE), F32)],
        compiler_params=_cp(("arbitrary",)), name=name)(proj, proj, dcb, dcb, dq, sel, cw, dproj)


def _ln_silu_fwd(cb, g, b, *, name):
    s = cb.shape[0]
    tm = min(TM, s)

    def body(cb_ref, g_ref, b_ref, sb_ref):
        z = cb_ref[...].astype(F32)
        zc = z - jnp.mean(z, axis=-1, keepdims=True)
        ln = (zc * lax.rsqrt(jnp.mean(zc * zc, axis=-1, keepdims=True) + NORM_EPS)) * g_ref[...] + b_ref[...]
        sb_ref[...] = (ln * _sigmoid(ln)).astype(BF)

    row = lambda i: (i, 0)
    vec = pl.BlockSpec((1, D), lambda i: (0, 0))
    return _pcall(
        body, grid=(s // tm,), in_specs=[pl.BlockSpec((tm, D), row), vec, vec],
        out_specs=pl.BlockSpec((tm, D), row), out_shape=SDS((s, D), BF),
        compiler_params=_cp(("parallel",)), name=name)(cb, g, b)


def _ln_silu_bwd(cb, dsb, g, b, *, name):
    s = cb.shape[0]
    tm = min(TM, s)

    def body(cb_ref, dsb_ref, g_ref, b_ref, dcb_ref, sums_ref):
        @pl.when(pl.program_id(0) == 0)
        def _():
            sums_ref[...] = jnp.zeros_like(sums_ref)

        z = cb_ref[...].astype(F32)
        zc = z - jnp.mean(z, axis=-1, keepdims=True)
        rstd = lax.rsqrt(jnp.mean(zc * zc, axis=-1, keepdims=True) + NORM_EPS)
        lnh = zc * rstd
        ln = lnh * g_ref[...] + b_ref[...]
        sg = _sigmoid(ln)
        dln = dsb_ref[...].astype(F32) * (sg * (1.0 + ln * (1.0 - sg)))
        sums_ref[0:1, :] += jnp.sum(dln * lnh, axis=0, keepdims=True)
        sums_ref[1:2, :] += jnp.sum(dln, axis=0, keepdims=True)
        dlnh = dln * g_ref[...]
        dz = rstd * (dlnh - jnp.mean(dlnh, axis=-1, keepdims=True)
                     - lnh * jnp.mean(dlnh * lnh, axis=-1, keepdims=True))
        sums_ref[2:3, :] += jnp.sum(dz, axis=0, keepdims=True)
        dcb_ref[...] = dz.astype(BF)

    row = lambda i: (i, 0)
    vec = pl.BlockSpec((1, D), lambda i: (0, 0))
    return _pcall(
        body, grid=(s // tm,), in_specs=[pl.BlockSpec((tm, D), row), pl.BlockSpec((tm, D), row), vec, vec],
        out_specs=[pl.BlockSpec((tm, D), row), pl.BlockSpec((8, D), lambda i: (0, 0))],
        out_shape=[SDS((s, D), BF), SDS((8, D), F32)],
        compiler_params=_cp(("arbitrary",)), name=name)(cb, dsb, g, b)


_Q_COL = 5 * D // HEAD


def _kv_prep(mem, g, wkv, *, name):
    m = mem.shape[0]

    def body(mem_ref, g_ref, w_ref, memn_ref, kv_ref):
        mv = mem_ref[...]
        memn = ((mv * _rms(mv)) * g_ref[...]).astype(BF)
        memn_ref[...] = memn
        for dev in range(N_DEV):
            kv_ref[:, dev * C_KV:(dev + 1) * C_KV] = _dot(memn, w_ref[dev]).astype(BF)

    return _pcall(body, out_shape=[SDS((m, D), BF), SDS((m, 2 * D), BF)],
                  compiler_params=_cp(), name=name)(mem, g, wkv)


def _softmax_rows(q, k):
    sc = _dot_nt(q, k) * (1.0 / (HEAD ** 0.5))
    e = jnp.exp(sc - jnp.max(sc, axis=-1, keepdims=True))
    return e / jnp.sum(e, axis=-1, keepdims=True)


def _attn_fwd(proj, kv, *, name):
    s, m = proj.shape[0], kv.shape[0]
    tm = min(TM, s)

    def body(q_ref, kv_ref, o_ref):
        for hd in range(N_HEADS):
            cols = slice(hd * HEAD, (hd + 1) * HEAD)
            p = _softmax_rows(q_ref[:, cols], kv_ref[:, cols])
            o_ref[:, cols] = _dot(p.astype(BF), kv_ref[:, D + hd * HEAD:D + (hd + 1) * HEAD]).astype(BF)

    return _pcall(
        body, grid=(s // tm,),
        in_specs=[pl.BlockSpec((tm, D), lambda i: (i, _Q_COL // N_HEADS)),
                  pl.BlockSpec((m, 2 * D), lambda i: (0, 0))],
        out_specs=pl.BlockSpec((tm, D), lambda i: (i, 0)), out_shape=SDS((s, D), BF),
        compiler_params=_cp(("parallel",)), name=name)(proj, kv)


def _attn_bwd(proj, kv, do, *, name):
    s, m = proj.shape[0], kv.shape[0]
    tm = min(TM, s)

    def body(q_ref, kv_ref, do_ref, dq_ref, dk_ref, dv_ref):
        @pl.when(pl.program_id(0) == 0)
        def _():
            dk_ref[...] = jnp.zeros_like(dk_ref)
            dv_ref[...] = jnp.zeros_like(dv_ref)

        for hd in range(N_HEADS):
            cols = slice(hd * HEAD, (hd + 1) * HEAD)
            q, k, dov = q_ref[:, cols], kv_ref[:, cols], do_ref[:, cols]
            p = _softmax_rows(q, k)
            dp = _dot_nt(dov, kv_ref[:, D + hd * HEAD:D + (hd + 1) * HEAD])
            dv_ref[:, cols] += _dot_tn(p.astype(BF), dov)
            ds = (p * (dp - jnp.sum(dp * p, axis=-1, keepdims=True)) * (1.0 / (HEAD ** 0.5))).astype(BF)
            dq_ref[:, cols] = _dot(ds, k).astype(BF)
            dk_ref[:, cols] += _dot_tn(ds, q)

    return _pcall(
        body, grid=(s // tm,),
        in_specs=[pl.BlockSpec((tm, D), lambda i: (i, _Q_COL // N_HEADS)),
                  pl.BlockSpec((m, 2 * D), lambda i: (0, 0)),
                  pl.BlockSpec((tm, D), lambda i: (i, 0))],
        out_specs=[pl.BlockSpec((tm, D), lambda i: (i, 0)),
                   pl.BlockSpec((m, D), lambda i: (0, 0)),
                   pl.BlockSpec((m, D), lambda i: (0, 0))],
        out_shape=[SDS((s, D), BF), SDS((m, D), F32), SDS((m, D), F32)],
        compiler_params=_cp(("arbitrary",)), name=name)(proj, kv, do)


def _kv_bwd(mem, g, memn, dk, dv, wkv, *, name):
    def body(mem_ref, g_ref, memn_ref, dk_ref, dv_ref, w_ref, dw_ref, dg_ref):
        memn = memn_ref[...]
        dmemn = None
        for dev in range(N_DEV):
            d_ref, col = (dk_ref, dev) if dev < N_HEADS else (dv_ref, dev - N_HEADS)
            dslab = d_ref[:, col * C_KV:(col + 1) * C_KV].astype(BF)
            dw_ref[dev] = _dot_tn(memn, dslab).astype(BF)
            part = _dot_nt(dslab, w_ref[dev])
            dmemn = part if dmemn is None else dmemn + part
        mv = mem_ref[...]
        dg_ref[...] = jnp.zeros_like(dg_ref)
        dg_ref[0:1, :] = jnp.sum(dmemn * (mv * _rms(mv)), axis=0, keepdims=True)

    assert C_KV == HEAD
    return _pcall(body, out_shape=[SDS((N_DEV, D, C_KV), BF), SDS((8, D), F32)],
                  compiler_params=_cp(), name=name)(mem, g, memn, dk, dv, wkv)


_TM_MIX = 512


def _mix_out(x, za, sb, o, proj, w4, bg, g_next, *, name):
    s = x.shape[0]
    tm = min(_TM_MIX, s)

    def body(x_ref, za_ref, sb_ref, o_ref, pg_ref, w4_ref, bg_ref, gn_ref,
             ya_ref, yb_ref, yc_ref, mg_ref, x1_ref, h_ref):
        ys = (_dot(za_ref[...], w4_ref[0]), _dot(sb_ref[...], w4_ref[1]), _dot(o_ref[...], w4_ref[2]))
        merged = None
        for j, (y, y_ref) in enumerate(zip(ys, (ya_ref, yb_ref, yc_ref))):
            y_ref[...] = y.astype(BF)
            gate = _sigmoid(pg_ref[:, j * D:(j + 1) * D].astype(F32) + bg_ref[:, j * D:(j + 1) * D])
            merged = gate * y if merged is None else merged + gate * y
        mg = merged.astype(BF)
        mg_ref[...] = mg
        x1 = x_ref[...] + _dot(mg, w4_ref[3])
        x1_ref[...] = x1
        h_ref[...] = ((x1 * _rms(x1)) * gn_ref[...]).astype(BF)

    row = lambda i: (i, 0)
    act = pl.BlockSpec((tm, D), row)
    return _pcall(
        body, grid=(s // tm,),
        in_specs=[act, act, act, act, pl.BlockSpec((tm, 3 * D), lambda i: (i, 2)),
                  pl.BlockSpec((4, D, D), lambda i: (0, 0, 0), pipeline_mode=pl.Buffered(1)), pl.BlockSpec((1, 3 * D), lambda i: (0, 0)),
                  pl.BlockSpec((1, D), lambda i: (0, 0))],
        out_specs=[act] * 6,
        out_shape=[SDS((s, D), BF)] * 4 + [SDS((s, D), F32), SDS((s, D), BF)],
        compiler_params=_cp(("parallel",)), name=name)(x, za, sb, o, proj, w4, bg, g_next)


def _mix_bwd(dxb, ya, yb, yc, proj, w4, bg, *, name):
    s = dxb.shape[0]
    tm = min(_TM_MIX, s)

    def body(dx_ref, ya_ref, yb_ref, yc_ref, pg_ref, w4_ref, bg_ref,
             dya_ref, dyb_ref, dyc_ref, dza_ref, dsb_ref, do_ref, dgt_ref, dbg_ref):
        @pl.when(pl.program_id(0) == 0)
        def _():
            dbg_ref[...] = jnp.zeros_like(dbg_ref)

        dm = _dot_nt(dx_ref[...], w4_ref[3])
        for j, (y_ref, dy_ref, din_ref) in enumerate(zip((ya_ref, yb_ref, yc_ref), (dya_ref, dyb_ref, dyc_ref),
                                                         (dza_ref, dsb_ref, do_ref))):
            cols = slice(j * D, (j + 1) * D)
            gate = _sigmoid(pg_ref[:, cols].astype(F32) + bg_ref[:, cols])
            dy = (dm * gate).astype(BF)
            dy_ref[...] = dy
            din_ref[...] = _dot_nt(dy, w4_ref[j]).astype(BF)
            dpre = dm * y_ref[...].astype(F32) * gate * (1.0 - gate)
            dgt_ref[:, cols] = dpre.astype(BF)
            dbg_ref[0:1, cols] += jnp.sum(dpre, axis=0, keepdims=True)

    row = lambda i: (i, 0)
    act = pl.BlockSpec((tm, D), row)
    return _pcall(
        body, grid=(s // tm,),
        in_specs=[act, act, act, act, pl.BlockSpec((tm, 3 * D), lambda i: (i, 2)),
                  pl.BlockSpec((4, D, D), lambda i: (0, 0, 0), pipeline_mode=pl.Buffered(1)), pl.BlockSpec((1, 3 * D), lambda i: (0, 0))],
        out_specs=[act] * 6 + [pl.BlockSpec((tm, 3 * D), lambda i: (i, 2)),
                               pl.BlockSpec((8, 3 * D), lambda i: (0, 0))],
        out_shape=[SDS((s, D), BF)] * 6 + [SDS((s, 9 * D), BF), SDS((8, 3 * D), F32)],
        compiler_params=_cp(("arbitrary",)), name=name)(dxb, ya, yb, yc, proj, w4, bg)


_PAIR = 2 * C_UP_P


def _row_selector(sub, first_cols):
    rows = np.arange(len(first_cols) * sub)
    col = np.asarray(first_cols)[rows // sub] + rows % sub
    return jnp.asarray(np.arange(2 * sub)[None, :] == col[:, None], BF)


def _ffn_act(u2, cw, *, name):
    s = u2.shape[0]
    tr = min(TR_FFN, s)
    sub = min(SUB, tr)
    sel = _row_selector(sub, [sub - (K_F - 1 - k) for k in range(K_F)])

    def body(cur, prev, sel_ref, w_ref, act_ref, c2_ref, xb, win):
        i = pl.program_id(1)
        before = prev[...]
        xb[0:sub, :] = jnp.where(i == 0, jnp.zeros_like(before), before)
        xb[sub:sub + tr, :] = cur[...]
        for r0 in range(0, tr, sub):
            win[...] = _dot(sel_ref[...], xb[r0:r0 + 2 * sub, :])
            for c in range(C_UP_P // LANE):
                gl = slice(LANE * c, LANE * c + LANE)
                ul = slice(C_UP_P + LANE * c, C_UP_P + LANE * c + LANE)
                gt = sum(w_ref[k:k + 1, gl] * win[k * sub:(k + 1) * sub, gl] for k in range(K_F))
                up = sum(w_ref[k:k + 1, ul] * win[k * sub:(k + 1) * sub, ul] for k in range(K_F))
                c2_ref[r0:r0 + sub, gl] = gt.astype(BF)
                c2_ref[r0:r0 + sub, ul] = up.astype(BF)
                act_ref[r0:r0 + sub, gl] = (gt * _sigmoid(gt) * up).astype(BF)

    return _pcall(
        body, grid=(4, s // tr),
        in_specs=[pl.BlockSpec((tr, _PAIR), lambda p, i: (i, p)),
                  pl.BlockSpec((sub, _PAIR), lambda p, i: (_halo_before(i, tr, sub), p)),
                  pl.BlockSpec(sel.shape, lambda p, i: (0, 0)),
                  pl.BlockSpec((8, _PAIR), lambda p, i: (0, p))],
        out_specs=[pl.BlockSpec((tr, C_UP_P), lambda p, i: (i, p)), pl.BlockSpec((tr, _PAIR), lambda p, i: (i, p))],
        out_shape=[SDS((s, FF_P), BF), SDS((s, 2 * FF_P), BF)],
        scratch_shapes=[pltpu.VMEM((sub + tr, _PAIR), BF), pltpu.VMEM((K_F * sub, _PAIR), F32)],
        compiler_params=_cp(("parallel", "parallel")), name=name)(u2, u2, sel, cw)


def _ffn_bwd(u2, c2, dact, cw, *, name):
    s = u2.shape[0]
    tr, h = min(TR_FFN, s), H_S
    sub = min(SUB, tr)
    n = s // tr
    sel = _row_selector(sub, [K_F - 1 - k for k in range(K_F)])

    def body(u_cur, c_cur, c_after, da_cur, da_after, sel_ref, w_ref, du_ref, dw_ref, dcb, win):
        i = pl.program_id(1)
        last = i == n - 1

        @pl.when(i == 0)
        def _():
            dw_ref[...] = jnp.zeros_like(dw_ref)

        def conv_grad(gt, up, da):
            gt, up, da = gt.astype(F32), up.astype(F32), da.astype(F32)
            sg = _sigmoid(gt)
            return (da * up * (sg * (1.0 + gt * (1.0 - sg)))).astype(BF), (da * (gt * sg)).astype(BF)

        for c in range(C_UP_P // LANE):
            gl = slice(LANE * c, LANE * c + LANE)
            ul = slice(C_UP_P + LANE * c, C_UP_P + LANE * c + LANE)
            for r0 in range(0, tr, sub):
                rows = slice(r0, r0 + sub)
                dcb[rows, gl], dcb[rows, ul] = conv_grad(c_cur[rows, gl], c_cur[rows, ul], da_cur[rows, gl])
            dg, du_ = conv_grad(c_after[:, gl], c_after[:, ul], da_after[:, gl])
            dcb[tr:tr + h, gl] = jnp.where(last, jnp.zeros_like(dg), dg)
            dcb[tr:tr + h, ul] = jnp.where(last, jnp.zeros_like(du_), du_)
        dcb[tr + h:tr + sub, :] = jnp.zeros((sub - h, _PAIR), BF)
        for r0 in range(0, tr, sub):
            win[...] = _dot(sel_ref[...], dcb[r0:r0 + 2 * sub, :])
            for c in range(_PAIR // LANE):
                sl = slice(LANE * c, LANE * c + LANE)
                u = u_cur[r0:r0 + sub, sl].astype(F32)
                du = None
                for k in range(K_F):
                    wk = win[k * sub:(k + 1) * sub, sl]
                    term = w_ref[k:k + 1, sl] * wk
                    du = term if du is None else du + term
                    dw_ref[k:k + 1, sl] += jnp.sum(wk * u, axis=0, keepdims=True)
                du_ref[r0:r0 + sub, sl] = du.astype(BF)

    return _pcall(
        body, grid=(4, n),
        in_specs=[pl.BlockSpec((tr, _PAIR), lambda p, i: (i, p)),
                  pl.BlockSpec((tr, _PAIR), lambda p, i: (i, p)),
                  pl.BlockSpec((h, _PAIR), lambda p, i: (_halo_after(i, tr, h, s), p)),
                  pl.BlockSpec((tr, C_UP_P), lambda p, i: (i, p)),
                  pl.BlockSpec((h, C_UP_P), lambda p, i: (_halo_after(i, tr, h, s), p)),
                  pl.BlockSpec(sel.shape, lambda p, i: (0, 0)),
                  pl.BlockSpec((8, _PAIR), lambda p, i: (0, p))],
        out_specs=[pl.BlockSpec((tr, _PAIR), lambda p, i: (i, p)), pl.BlockSpec((8, _PAIR), lambda p, i: (0, p))],
        out_shape=[SDS((s, 2 * FF_P), BF), SDS((8, 2 * FF_P), F32)],
        scratch_shapes=[pltpu.VMEM((tr + sub, _PAIR), BF), pltpu.VMEM((K_F * sub, _PAIR), F32)],
        compiler_params=_cp(("parallel", "arbitrary")), name=name)(u2, c2, c2, dact, dact, sel, cw)


def _relations():
    x, y, c = lax.axis_index("x"), lax.axis_index("y"), lax.axis_index("c")
    out = []
    for r in range(1, N_DEV):
        rx, ry, rc = (r >> 2) & 1, (r >> 1) & 1, r & 1
        out.append((r, (x ^ rx, y ^ ry, c ^ rc)))
    return out


def _my_index():
    return 4 * lax.axis_index("x") + 2 * lax.axis_index("y") + lax.axis_index("c")


def _slab(kind, ref, idx):
    if kind == "win":
        return ref.at[:, pl.ds(pl.multiple_of(idx * C_IN, LANE), C_IN)]
    if kind == "wup":
        return ref.at[:, pl.ds(pl.multiple_of(_up_slot(idx) * C_UP_P, LANE), C_UP_P)]
    if kind == "wupT":
        return ref.at[pl.ds(pl.multiple_of(_up_slot(idx) * C_UP_P, LANE), C_UP_P), :]
    if kind == "wkv":
        return ref.at[idx]
    if kind == "w4":
        return ref.at[:, pl.ds(pl.multiple_of(idx * R_O, 16), R_O), :]
    if kind == "wdn":
        return ref.at[pl.ds(pl.multiple_of(_dn_row(idx), 16), R_DN), :]
    assert kind == "cv"
    return ref.at[idx]


_WHOLE = {"win": ((D, 9 * D), BF), "wup": ((D, 2 * FF_P), BF), "wkv": ((N_DEV, D, C_KV), BF),
          "w4": ((4, D, D), BF), "wdn": ((FF_P, D), BF)}
_SHARD = {"win": (D, C_IN), "wup": (D, C_UP_P), "wupT": (C_UP_P, D), "wkv": (D, C_KV), "w4": (4, R_O, D),
          "wdn": (R_DN, D)}
HBM_SPEC = pl.BlockSpec(memory_space=pltpu.HBM)
SEM_SPEC = pl.BlockSpec(memory_space=pltpu.SEMAPHORE)
_DATAFLOW = pltpu.SideEffectType.DATAFLOW_SIDE_EFFECTING


def _scatter_maps(kinds):
    return ((lambda srcs, lands, a, idx: _slab(kinds[a], srcs[a], idx)),
            (lambda lands, a, idx: lands[a].at[idx]))


_SLOTTED = ("wkv", "cv")


def _own_slab_blocks(kind, shard_shape):
    if kind in ("win", "wup"):
        rows, slot = 256, (_up_slot if kind == "wup" else (lambda m: m))
        return (shard_shape[0] // rows, (rows, shard_shape[1]), (lambda i, me: (i, slot(me[0]))),
                (lambda i, me: (i, 0)), (lambda i, me: (me[0], i, 0)))
    if kind == "wupT":
        rows = 256
        steps = shard_shape[0] // rows
        return (steps, (rows, D), (lambda i, me: (_up_slot(me[0]) * steps + i, 0)), (lambda i, me: (i, 0)),
                (lambda i, me: (me[0], i, 0)))
    if kind == "w4":
        return (1, shard_shape, (lambda i, me: (0, me[0], 0)), (lambda i, me: (0, 0, 0)),
                (lambda i, me: (me[0], 0, 0, 0)))
    if kind == "wdn":
        rows = 32
        return (R_DN // rows, (rows, D), (lambda i, me: (_dn_row(me[0]) // rows + i, 0)), (lambda i, me: (i, 0)),
                (lambda i, me: (me[0], i, 0)))
    assert kind in _SLOTTED
    rows = min(256, shard_shape[0])
    return (shard_shape[0] // rows, (rows, shard_shape[1]), (lambda i, me: (me[0], i, 0)),
            (lambda i, me: (i, 0)), (lambda i, me: (me[0], i, 0)))


def _place_own(kind, src, out_sds, gather, me_arr, *, name):
    shard_shape = src.shape if gather else out_sds.shape[1:]
    steps, blk, whole_idx, shard_idx, staging_idx = _own_slab_blocks(kind, shard_shape)
    slotted = kind in _SLOTTED
    whole_spec = pl.BlockSpec(((None,) if slotted else ()) + tuple(blk), whole_idx)
    if gather:
        in_spec, out_spec = pl.BlockSpec(tuple(blk), shard_idx), whole_spec
    else:
        in_spec, out_spec = whole_spec, pl.BlockSpec((None,) + tuple(blk), staging_idx)
    zero_init = gather and kind == "wdn"

    def body(me_ref, src_ref, *rest):
        rest[-1][...] = src_ref[...].astype(rest[-1].dtype)

    operands = (me_arr, src) + ((jnp.zeros(out_sds.shape, out_sds.dtype),) if zero_init else ())
    return _pcall(
        body,
        grid_spec=pltpu.PrefetchScalarGridSpec(
            num_scalar_prefetch=1, grid=(steps,), in_specs=[in_spec] + ([ANY] if zero_init else []),
            out_specs=out_spec),
        out_shape=out_sds, input_output_aliases={2: 0} if zero_init else {},
        compiler_params=_cp(("arbitrary",)), name=name)(*operands)


def _peer_copies(n, src_of, dst_of, src_r, land_r, ssem, rsem):
    me = _my_index()
    out = []
    for r, peer in _relations():
        p_idx = 4 * peer[0] + 2 * peer[1] + peer[2]
        for a in range(n):
            def copy(src_idx, dst_idx, a=a, r=r, peer=peer):
                sem = a * (N_DEV - 1) + r - 1
                return pltpu.make_async_remote_copy(
                    src_ref=src_of(src_r, land_r, a, src_idx), dst_ref=dst_of(land_r, a, dst_idx),
                    send_sem=ssem.at[sem], recv_sem=rsem.at[sem], device_id=peer, device_id_type=MESH)
            out.append((functools.partial(copy, p_idx, me), functools.partial(copy, me, p_idx)))
    return out


def _exchange_start(srcs, lands, maps, after, *, name):
    n, ns = len(lands), len(srcs)
    src_of, dst_of = maps

    def body(*refs):
        src_r, land_r = refs[:ns], refs[ns:ns + n]
        ssem, rsem, token = refs[ns + n + 1], refs[ns + n + 2], refs[-1]
        for send, _ in _peer_copies(n, src_of, dst_of, src_r, land_r, ssem, rsem):
            send().start()
        token[...] = jnp.zeros_like(token)

    flight = list(srcs) + list(lands)
    outs = pl.pallas_call(
        body, name=name,
        out_shape=(pltpu.SemaphoreType.DMA((n * (N_DEV - 1),)), pltpu.SemaphoreType.DMA((n * (N_DEV - 1),)),
                   *[pltpu.HBM(a.shape, a.dtype) for a in flight], SDS((8, LANE), F32)),
        in_specs=[HBM_SPEC] * (ns + n) + [ANY],
        out_specs=(SEM_SPEC, SEM_SPEC, *[HBM_SPEC] * (ns + n), pl.BlockSpec(memory_space=pltpu.VMEM)),
        input_output_aliases={i: 2 + i for i in range(ns + n)},
        compiler_params=pltpu.CompilerParams(has_side_effects=_DATAFLOW),
    )(*[pltpu.with_memory_space_constraint(a, pltpu.HBM) for a in flight], after)
    return (outs[0], outs[1], list(outs[2:2 + ns + n]), ns), outs[-1]


def _exchange_wait(handle, maps, after, *, name):
    ssem, rsem, flight, ns = handle
    n = len(flight) - ns
    src_of, dst_of = maps

    def body(*refs):
        src_r, land_r, ssem_r, rsem_r = refs[:ns], refs[ns:ns + n], refs[ns + n], refs[ns + n + 1]
        for send, arrival in _peer_copies(n, src_of, dst_of, src_r, land_r, ssem_r, rsem_r):
            send().wait_send()
            arrival().wait_recv()

    outs = pl.pallas_call(
        body, name=name, out_shape=[pltpu.HBM(a.shape, a.dtype) for a in flight],
        in_specs=[HBM_SPEC] * (ns + n) + [SEM_SPEC, SEM_SPEC, ANY], out_specs=[HBM_SPEC] * (ns + n),
        input_output_aliases={i: i for i in range(ns + n)},
        compiler_params=pltpu.CompilerParams(has_side_effects=_DATAFLOW),
    )(*flight, ssem, rsem, after)
    return list(outs[ns:])


_SIBLING = 1
_ICI = (2, 4, 6)


def _rel_peer(r):
    x, y, c = lax.axis_index("x"), lax.axis_index("y"), lax.axis_index("c")
    peer = (x ^ ((r >> 2) & 1), y ^ ((r >> 1) & 1), c ^ (r & 1))
    return peer, 4 * peer[0] + 2 * peer[1] + peer[2]


def _rcopy(ref, ssem, rsem, peer):
    return pltpu.make_async_remote_copy(src_ref=ref, dst_ref=ref, send_sem=ssem, recv_sem=rsem, device_id=peer,
                                        device_id_type=MESH)


def _gather2_start(lands, kinds, after, *, name):
    n = len(lands)

    def body(*refs):
        land_r, (send1, recv_sib, recv_ici), token = refs[:n], refs[n + 1:n + 4], refs[-1]
        me = _my_index()
        for a in range(n):
            own = _slab(kinds[a], land_r[a], me)
            for j, r in enumerate((_SIBLING,) + _ICI):
                rsem = recv_sib.at[a] if r == _SIBLING else recv_ici.at[3 * a + j - 1]
                _rcopy(own, send1.at[4 * a + j], rsem, _rel_peer(r)[0]).start()
        token[...] = jnp.zeros_like(token)

    sems = [pltpu.SemaphoreType.DMA((4 * n,)), pltpu.SemaphoreType.DMA((n,)), pltpu.SemaphoreType.DMA((3 * n,))]
    outs = pl.pallas_call(
        body, name=name, out_shape=(*sems, *[pltpu.HBM(a.shape, a.dtype) for a in lands], SDS((8, LANE), F32)),
        in_specs=[HBM_SPEC] * n + [ANY],
        out_specs=(SEM_SPEC,) * 3 + (HBM_SPEC,) * n + (pl.BlockSpec(memory_space=pltpu.VMEM),),
        input_output_aliases={i: 3 + i for i in range(n)},
        compiler_params=pltpu.CompilerParams(has_side_effects=_DATAFLOW),
    )(*[pltpu.with_memory_space_constraint(a, pltpu.HBM) for a in lands], after)
    return dict(send1=outs[0], recv_sib=outs[1], recv_ici=outs[2], lands=list(outs[3:3 + n])), outs[-1]


def _gather2_forward(handle, kinds, after, *, name):
    lands = handle["lands"]
    n = len(lands)

    def body(*refs):
        land_r, recv_ici, (fwd_send, fwd_recv), token = refs[:n], refs[n], refs[n + 2:n + 4], refs[-1]
        sibling = _rel_peer(_SIBLING)[0]
        for a in range(n):
            for j, r in enumerate(_ICI):
                got = _slab(kinds[a], land_r[a], _rel_peer(r)[1])
                _rcopy(got, fwd_send.at[3 * a + j], recv_ici.at[3 * a + j], sibling).wait_recv()
                _rcopy(got, fwd_send.at[3 * a + j], fwd_recv.at[3 * a + j], sibling).start()
        token[...] = jnp.zeros_like(token)

    sems = [pltpu.SemaphoreType.DMA((3 * n,)), pltpu.SemaphoreType.DMA((3 * n,))]
    outs = pl.pallas_call(
        body, name=name, out_shape=(*sems, *[pltpu.HBM(a.shape, a.dtype) for a in lands], SDS((8, LANE), F32)),
        in_specs=[HBM_SPEC] * n + [SEM_SPEC, ANY],
        out_specs=(SEM_SPEC,) * 2 + (HBM_SPEC,) * n + (pl.BlockSpec(memory_space=pltpu.VMEM),),
        input_output_aliases={i: 2 + i for i in range(n)},
        compiler_params=pltpu.CompilerParams(has_side_effects=_DATAFLOW),
    )(*lands, handle["recv_ici"], after)
    return dict(handle, fwd_send=outs[0], fwd_recv=outs[1], lands=list(outs[2:2 + n])), outs[-1]


def _gather2_wait(handle, kinds, after, *, name):
    lands = handle["lands"]
    n = len(lands)

    def body(*refs):
        land_r, (send1, recv_sib, fwd_send, fwd_recv) = refs[:n], refs[n:n + 4]
        me = _my_index()
        sibling, sib_idx = _rel_peer(_SIBLING)
        for a in range(n):
            own = _slab(kinds[a], land_r[a], me)
            for j, r in enumerate((_SIBLING,) + _ICI):
                _rcopy(own, send1.at[4 * a + j], recv_sib.at[a], _rel_peer(r)[0]).wait_send()
            theirs = _slab(kinds[a], land_r[a], sib_idx)
            _rcopy(theirs, send1.at[4 * a], recv_sib.at[a], sibling).wait_recv()
            for j, r in enumerate(_ICI):
                passed_on = _slab(kinds[a], land_r[a], _rel_peer(r)[1])
                _rcopy(passed_on, fwd_send.at[3 * a + j], fwd_recv.at[3 * a + j], sibling).wait_send()
                arrived = _slab(kinds[a], land_r[a], _rel_peer(r ^ _SIBLING)[1])
                _rcopy(arrived, fwd_send.at[3 * a + j], fwd_recv.at[3 * a + j], sibling).wait_recv()

    outs = pl.pallas_call(
        body, name=name, out_shape=[pltpu.HBM(a.shape, a.dtype) for a in lands],
        in_specs=[HBM_SPEC] * n + [SEM_SPEC] * 4 + [ANY], out_specs=[HBM_SPEC] * n,
        input_output_aliases={i: i for i in range(n)},
        compiler_params=pltpu.CompilerParams(has_side_effects=_DATAFLOW),
    )(*lands, handle["send1"], handle["recv_sib"], handle["fwd_send"], handle["fwd_recv"], after)
    return list(outs)


def _sum_slots(gathered, *, name):
    def body(g_ref, out_ref):
        total = g_ref[0]
        for dev in range(1, N_DEV):
            total = total + g_ref[dev]
        out_ref[...] = total

    return _pcall(body, out_shape=SDS(gathered.shape[1:], F32), compiler_params=_cp(), name=name)(gathered)


def _adam(g, w, m, v):
    nm = ADAM_B1 * m + (1.0 - ADAM_B1) * g
    nv = ADAM_B2 * v + (1.0 - ADAM_B2) * (g * g)
    m_hat = nm / (1.0 - ADAM_B1 ** ADAM_STEP)
    v_hat = nv / (1.0 - ADAM_B2 ** ADAM_STEP)
    return -ADAM_LR * (m_hat / (jnp.sqrt(v_hat) + ADAM_EPS) + ADAM_WD * w), nm, nv


def _adamw_staged(st0, st1, w, m, v, *, name):
    _, rows, cols = w.shape
    st_cols = st0.shape[2]
    tr = max(t for t in range(16, 129, 16) if rows % t == 0)
    nr = rows // tr

    def body(s0_ref, s1_ref, w_ref, m_ref, v_ref, g_ref, d_ref, nm_ref, nv_ref):
        for layer, s_ref in enumerate((s0_ref, s1_ref)):
            @pl.when(pl.program_id(0) == layer)
            def _(s_ref=s_ref):
                total = s_ref[0, :, 0:cols].astype(F32)
                for dev in range(1, N_DEV):
                    total = total + s_ref[dev, :, 0:cols].astype(F32)
                g_ref[0] = total

        d_ref[0], nm_ref[0], nv_ref[0] = _adam(g_ref[0], w_ref[0], m_ref[0], v_ref[0])

    st_spec = lambda layer: pl.BlockSpec(
        (N_DEV, tr, st_cols), lambda l, i: (0, jnp.where(l == layer, i, (nr - 1) * (1 - layer)), 0))
    par = pl.BlockSpec((1, tr, cols), lambda l, i: (l, i, 0))
    return _pcall(
        body, grid=(DEPTH, nr), in_specs=[st_spec(0), st_spec(1), par, par, par], out_specs=[par] * 4,
        out_shape=[SDS(w.shape, F32)] * 4,
        compiler_params=_cp(("arbitrary", "arbitrary")), name=name)(st0, st1, w, m, v)


def _adamw_small(g, w, m, v, *, name):
    def body(g_ref, w_ref, m_ref, v_ref, d_ref, nm_ref, nv_ref):
        d_ref[...], nm_ref[...], nv_ref[...] = _adam(g_ref[...], w_ref[...], m_ref[...], v_ref[...])

    return _pcall(body, out_shape=[SDS(g.shape, F32)] * 3, compiler_params=_cp(), name=name)(g, w, m, v)


def _pack_rows(arrays):
    flat = jnp.concatenate([a.reshape(-1).astype(F32) for a in arrays])
    rows = -(-flat.shape[0] // (8 * D)) * 8
    return jnp.pad(flat, (0, rows * D - flat.shape[0])).reshape(rows, D)


def _unpack_rows(pack, like):
    flat = pack.reshape(-1)
    out, at = [], 0
    for a in like:
        out.append(flat[at:at + a.size].reshape(a.shape))
        at += a.size
    return out


def _layer_fwd(x, h, mem, win, mixer_weights, ffn_weights, after_up, small, g_next, tag):
    proj = _mm(h, win, tm=1024, tn=1536, name=f"proj_{tag}")
    wkv, w4, cw_a, cw_b, cw_f = mixer_weights(proj)
    za = _bra_fwd(proj, cw_a, name=f"bra_fwd_{tag}")
    cb = _brb_conv_fwd(proj, cw_b, small["conv_b_bias"], name=f"brb_conv_fwd_{tag}")
    sb = _ln_silu_fwd(cb, small["ln_b_g"], small["ln_b_b"], name=f"ln_silu_fwd_{tag}")
    memn, kv = _kv_prep(mem, small["norm_mem_g"], wkv, name=f"kv_prep_{tag}")
    o = _attn_fwd(proj, kv, name=f"attn_fwd_{tag}")
    ya, yb, yc, mg, x1, h2 = _mix_out(x, za, sb, o, proj, w4, small["b_gate"], small["norm_ffn_g"],
                                      name=f"mix_out_{tag}")
    wup, wdn = ffn_weights(h2)
    u2 = _mm(h2, wup, tm=1024, tn=1536, name=f"up_{tag}")
    token = after_up(u2)
    act, c2 = _ffn_act(u2, cw_f if token is None else _behind(cw_f, token), name=f"ffn_act_{tag}")
    x2, h_next = _mm_res_norm(act, wdn, x1, g_next, name=f"down_{tag}")
    saved = dict(x=x, h=h, proj=proj, za=za, cb=cb, sb=sb, memn=memn, kv=kv, o=o, ya=ya, yb=yb, yc=yc,
                 mg=mg, x1=x1, h2=h2, u2=u2, c2=c2, act=act)
    return x2, h_next, (win, wup, wkv, w4, wdn, cw_a, cw_b, cw_f), saved


def _behind(operand, token):
    return operand + token[0:1, 0:1]


def _layer_bwd(dx2, dx2b, mem, wts, small, sv, start, tag):
    win, wup, wkv, w4, wdn, cw_a, cw_b, cw_f = wts
    dact = _mm(dx2b, wdn, tb=True, tm=1024, tn=768, name=f"d_act_{tag}")
    dwdn = _mm(sv["act"], dx2b, ta=True, tm=768, tn=1024, name=f"dw_down_{tag}")
    du2, dcw_f = _ffn_bwd(sv["u2"], sv["c2"], dact, cw_f, name=f"ffn_bwd_{tag}")
    dwup_t = _mm(du2, sv["h2"], ta=True, tm=C_UP_P, tn=1024, name=f"dw_up_{tag}")
    token = start(("wdn", "wupT"), (dwdn, dwup_t), f"ffn_{tag}")
    dx1, dx1b, dg_ffn = _mm_nt_normbwd(du2, wup, sv["x1"], dx2, _behind(small["norm_ffn_g"], token),
                                       name=f"d_h2_{tag}")

    dya, dyb, dyc, dza, dsb, do, dproj, dbg = _mix_bwd(dx1b, sv["ya"], sv["yb"], sv["yc"], sv["proj"], w4,
                                                      small["b_gate"], name=f"mix_bwd_{tag}")
    dw4 = jnp.stack([
        _mm(a, b, ta=True, tm=1024, tn=512, name=f"dw_{nm}_{tag}")
        for nm, a, b in (("a_out", sv["za"], dya), ("b_out", sv["sb"], dyb), ("att_out", sv["o"], dyc),
                         ("o", sv["mg"], dx1b))])
    dq, dk, dv = _attn_bwd(sv["proj"], sv["kv"], do, name=f"attn_bwd_{tag}")
    dwkv, dg_mem = _kv_bwd(mem, small["norm_mem_g"], sv["memn"], dk, dv, wkv, name=f"kv_bwd_{tag}")
    token = start(("w4", "wkv"), (dw4, dwkv), f"mix_{tag}")
    dproj, dcw_a = _bra_bwd(sv["proj"], dza, _behind(cw_a, token), dproj, name=f"bra_bwd_{tag}")
    dcb, ln_sums = _ln_silu_bwd(sv["cb"], dsb, small["ln_b_g"], small["ln_b_b"], name=f"ln_silu_bwd_{tag}")
    dproj, dcw_b = _brb_conv_bwd(sv["proj"], dcb, dq, cw_b, dproj, name=f"brb_conv_bwd_{tag}")
    dwin = _mm(sv["h"], dproj, ta=True, tm=1024, tn=768, name=f"dw_in_{tag}")
    token = start(("win",), (dwin,), f"in_{tag}")
    dx, dxb, dg_mix = _mm_nt_normbwd(dproj, win, sv["x"], dx1, _behind(small["norm_mix_g"], token),
                                     tk=4608, name=f"d_h_{tag}")

    small_grads = [dg_mix[0:1], dg_mem[0:1], dbg[0:1].reshape(3, D), ln_sums[2:3], ln_sums[0:1], ln_sums[1:2],
                   dg_ffn[0:1], dcw_a[0:K_A], dcw_b[0:K_B], dcw_f[0:K_F].reshape(K_F * 2 * FF_P // D, D)]
    return dx, dxb, small_grads, token


_SMALL_ROWS = (1, 1, 3, 1, 1, 1, 1, K_A, K_B, K_F * 2 * FF_P // D)
_CV_ROWS = 48


def kernel(x, mem, norm_mix_g, norm_mem_g, w_in, b_gate, conv_a_w, w_a_out, conv_b_w, conv_b_bias, ln_b_g, ln_b_b, w_b_out, w_kv, w_att_out, w_o, norm_ffn_g, w_up, conv_ffn_w, w_down, norm_final_g, loss_target, m_norm_mix_g, m_norm_mem_g, m_w_in, m_b_gate, m_conv_a_w, m_w_a_out, m_conv_b_w, m_conv_b_bias, m_ln_b_g, m_ln_b_b, m_w_b_out, m_w_kv, m_w_att_out, m_w_o, m_norm_ffn_g, m_w_up, m_conv_ffn_w, m_w_down, m_norm_final_g, v_norm_mix_g, v_norm_mem_g, v_w_in, v_b_gate, v_conv_a_w, v_w_a_out, v_conv_b_w, v_conv_b_bias, v_ln_b_g, v_ln_b_b, v_w_b_out, v_w_kv, v_w_att_out, v_w_o, v_norm_ffn_g, v_w_up, v_conv_ffn_w, v_w_down, v_norm_final_g):
    me = _my_index()
    me_arr = me.astype(jnp.int32).reshape(1)
    x0, mem0, tgt = x.reshape(x.shape[1:]), mem.reshape(mem.shape[1:]), loss_target.reshape(x.shape[1:])
    up_pad = ((0, 0), (0, 0), (0, C_UP_P - C_UP))

    ag_groups = (("win",), ("wkv", "w4", "cv"), ("wup", "wdn"))
    kinds = ag_groups[0] + ag_groups[1] + ag_groups[2]
    smalls, ag_handles = [], []
    token = jnp.zeros((8, LANE), F32)
    for l in range(DEPTH):
        cv = jnp.zeros((_CV_ROWS, C_UP_P), F32)
        cv = cv.at[0:K_F, 0:C_UP].set(conv_ffn_w[l]).at[3:3 + K_A, 0:R_O].set(conv_a_w[l])
        cv = cv.at[8:8 + K_B, 0:R_O].set(conv_b_w[l])
        shards = dict(
            win=w_in[l], wup=jnp.pad(w_up[l], up_pad[1:]), wkv=w_kv[l],
            w4=jnp.stack([w_a_out[l], w_b_out[l], w_att_out[l], w_o[l]]), wdn=w_down[l], cv=cv)
        whole = dict({k: SDS(*_WHOLE[k]) for k in kinds if k != "cv"}, cv=SDS((N_DEV,) + cv.shape, F32))
        lands = {k: _place_own(k, shards[k], whole[k], True, me_arr, name=f"ag_own_{k}_l{l}") for k in kinds}
        per_layer = []
        for g, grp in enumerate(ag_groups):
            handle, token = _gather2_start([lands[k] for k in grp], grp, token, name=f"ag_start_l{l}_g{g}")
            per_layer.append(handle)
        ag_handles.append(per_layer)
        smalls.append(dict(
            norm_mix_g=norm_mix_g[l][None], norm_mem_g=norm_mem_g[l][None], b_gate=b_gate[l][None],
            conv_b_bias=conv_b_bias[l][None], ln_b_g=ln_b_g[l][None], ln_b_b=ln_b_b[l][None],
            norm_ffn_g=norm_ffn_g[l][None]))

    def forward_group(l, g, after):
        ag_handles[l][g], tok = _gather2_forward(ag_handles[l][g], ag_groups[g], after, name=f"ag_forward_l{l}_g{g}")
        return tok

    def group_of(l, g):
        def wait(after):
            if l == 0:
                after = forward_group(0, g, after)
            return _gather2_wait(ag_handles[l][g], ag_groups[g], after, name=f"ag_wait_l{l}_g{g}")
        return wait

    def mixer_weights(l):
        def wait(after):
            if l > 0:
                after = forward_group(l, 2, after)
            wkv, w4, cvg = group_of(l, 1)(after)
            cw_f = jnp.stack([cvg[d, 0:K_F, :] for d in UP_ORDER], axis=1).reshape(K_F, 2 * FF_P)
            cw_a = cvg[:, 3:3 + K_A, 0:R_O].transpose(1, 0, 2).reshape(K_A, D)
            cw_b = cvg[:, 8:8 + K_B, 0:R_O].transpose(1, 0, 2).reshape(K_B, D)
            return (wkv, w4, jnp.pad(cw_a, ((0, 8 - K_A), (0, 0))), jnp.pad(cw_b, ((0, 32 - K_B), (0, 0))),
                    jnp.pad(cw_f, ((0, 8 - K_F), (0, 0))))
        return wait

    wts, saved = [], []
    xs = x0
    h = _rms_fwd(xs, smalls[0]["norm_mix_g"], name="rms_fwd")
    behind = forward_group(0, 0, token)

    def next_layer_forwarding(l):
        def hook(after):
            return None if l + 1 == DEPTH else forward_group(l + 1, 0, after)
        return hook

    for l in range(DEPTH):
        g_next = smalls[l + 1]["norm_mix_g"] if l + 1 < DEPTH else norm_final_g[None]
        if l > 0:
            behind = forward_group(l, 1, behind)
        (win,) = _gather2_wait(ag_handles[l][0], ag_groups[0], behind, name=f"ag_wait_l{l}_g0")
        xs, h, w_l, sv = _layer_fwd(xs, h, mem0, win, mixer_weights(l), group_of(l, 2), next_layer_forwarding(l),
                                    smalls[l], g_next, f"l{l}")
        behind = h
        wts.append(w_l)
        saved.append(sv)
    dx, dxb, head_sums = _loss_head(xs, tgt, norm_final_g[None], name="loss_head")

    rs_handles = []
    small_grads = [None] * DEPTH

    def start_scatter(grp, arrays, name):
        maps = _scatter_maps(grp)
        lands = [_place_own(k, a, SDS((N_DEV,) + _SHARD[k], BF), False, me_arr, name=f"rs_own_{k}_{name}")
                 for k, a in zip(grp, arrays)]
        handle, tok = _exchange_start(list(arrays), lands, maps, rs_handles[-1][2] if rs_handles else head_sums,
                                      name=f"rs_start_{name}")
        rs_handles.append((grp, handle, tok, name))
        return tok

    for l in reversed(range(DEPTH)):
        dx, dxb, small_grads[l], token = _layer_bwd(dx, dxb, mem0, wts[l], smalls[l], saved[l], start_scatter,
                                                    f"l{l}")

    pack = jnp.concatenate(small_grads[0] + small_grads[1] + [head_sums[1:2], head_sums[0:1]], axis=0)
    pack = jnp.pad(pack, ((0, -pack.shape[0] % 8), (0, 0)))
    small_maps = (lambda srcs, lands, a, idx: srcs[a]), (lambda lands, a, idx: lands[a].at[idx])
    small_land = _place_own("cv", pack, SDS((N_DEV,) + pack.shape, F32), True, me_arr, name="small_own")
    small_handle, small_token = _exchange_start([pack], [small_land], small_maps, dx, name="small_start")

    staged = [dict() for _ in range(DEPTH)]
    for grp, handle, _, name in rs_handles[:-1]:
        staged[int(name[-1])].update(zip(grp, _exchange_wait(handle, _scatter_maps(grp), small_token,
                                                             name=f"rs_wait_{name}")))

    def big_update(kind, w, m, v, name):
        return _adamw_staged(staged[0][kind], staged[1][kind], w, m, v, name=name)

    r_up = [jnp.swapaxes(a, 1, 2) for a in big_update(
        "wupT", jnp.swapaxes(w_up, 1, 2), jnp.swapaxes(m_w_up, 1, 2), jnp.swapaxes(v_w_up, 1, 2), "adamw_w_up")]
    r_kv = big_update("wkv", w_kv, m_w_kv, v_w_kv, "adamw_w_kv")
    r_dn = big_update("wdn", w_down, m_w_down, v_w_down, "adamw_w_down")

    def four(a, b, c, d_):
        return jnp.stack([a, b, c, d_], axis=1).reshape(DEPTH, 4 * R_O, D)

    r_4 = _adamw_staged(
        staged[0]["w4"].reshape(N_DEV, 4 * R_O, D), staged[1]["w4"].reshape(N_DEV, 4 * R_O, D),
        four(w_a_out, w_b_out, w_att_out, w_o), four(m_w_a_out, m_w_b_out, m_w_att_out, m_w_o),
        four(v_w_a_out, v_w_b_out, v_w_att_out, v_w_o), name="adamw_w_out")
    grp, handle, _, name = rs_handles[-1]
    staged[0].update(zip(grp, _exchange_wait(handle, _scatter_maps(grp), r_4[0], name=f"rs_wait_{name}")))
    r_in = big_update("win", w_in, m_w_in, v_w_in, "adamw_w_in")
    r_a, r_b, r_att, r_o = ([a.reshape(DEPTH, 4, R_O, D)[:, j] for a in r_4] for j in range(4))

    (gathered,) = _exchange_wait(small_handle, small_maps, r_in[0], name="small_wait")
    total = _sum_slots(gathered, name="small_sum")
    per_layer = sum(_SMALL_ROWS)
    parts = []
    for l in range(DEPTH):
        at, one = l * per_layer, []
        for rows in _SMALL_ROWS:
            one.append(total[at:at + rows])
            at += rows
        parts.append(one)
    g_final = total[DEPTH * per_layer]
    loss = 0.5 / D * jnp.sum(total[DEPTH * per_layer + 1])

    def both(i):
        return jnp.stack([parts[0][i], parts[1][i]])

    g_norm_mix, g_norm_mem = both(0)[:, 0], both(1)[:, 0]
    g_b_gate = both(2).reshape(DEPTH, 3 * D)
    g_cbias, g_lng, g_lnb, g_norm_ffn = both(3)[:, 0], both(4)[:, 0], both(5)[:, 0], both(6)[:, 0]
    g_conv_a = lax.dynamic_slice_in_dim(both(7), me * R_O, R_O, axis=2)
    g_conv_b = lax.dynamic_slice_in_dim(both(8), me * R_O, R_O, axis=2)
    g_conv_f = lax.dynamic_slice_in_dim(both(9).reshape(DEPTH, K_F, 2 * FF_P), _up_slot(me) * C_UP_P, C_UP, axis=2)

    small_g = [g_norm_mix, g_norm_mem, g_b_gate, g_conv_a, g_conv_b, g_cbias, g_lng, g_lnb, g_norm_ffn, g_conv_f,
               g_final]
    small_w = [norm_mix_g, norm_mem_g, b_gate, conv_a_w, conv_b_w, conv_b_bias, ln_b_g, ln_b_b, norm_ffn_g,
               conv_ffn_w, norm_final_g]
    small_m = [m_norm_mix_g, m_norm_mem_g, m_b_gate, m_conv_a_w, m_conv_b_w, m_conv_b_bias, m_ln_b_g, m_ln_b_b,
               m_norm_ffn_g, m_conv_ffn_w, m_norm_final_g]
    small_v = [v_norm_mix_g, v_norm_mem_g, v_b_gate, v_conv_a_w, v_conv_b_w, v_conv_b_bias, v_ln_b_g, v_ln_b_b,
               v_norm_ffn_g, v_conv_ffn_w, v_norm_final_g]
    upd = _adamw_small(_pack_rows(small_g), _pack_rows(small_w), _pack_rows(small_m), _pack_rows(small_v),
                       name="adamw_small")
    s_d, s_m, s_v = (_unpack_rows(p, small_w) for p in upd)
    (d_norm_mix, d_norm_mem, d_b_gate, d_conv_a, d_conv_b, d_cbias, d_lng, d_lnb, d_norm_ffn, d_conv_f,
     d_final) = s_d
    (nm_norm_mix, nm_norm_mem, nm_b_gate, nm_conv_a, nm_conv_b, nm_cbias, nm_lng, nm_lnb, nm_norm_ffn, nm_conv_f,
     nm_final) = s_m
    (nv_norm_mix, nv_norm_mem, nv_b_gate, nv_conv_a, nv_conv_b, nv_cbias, nv_lng, nv_lnb, nv_norm_ffn, nv_conv_f,
     nv_final) = s_v

    grads = [g_norm_mix, g_norm_mem, r_in[0], g_b_gate, g_conv_a, r_a[0], g_conv_b, g_cbias, g_lng, g_lnb, r_b[0],
             r_kv[0], r_att[0], r_o[0], g_norm_ffn, r_up[0], g_conv_f, r_dn[0], g_final]
    deltas = [d_norm_mix, d_norm_mem, r_in[1], d_b_gate, d_conv_a, r_a[1], d_conv_b, d_cbias, d_lng, d_lnb, r_b[1],
              r_kv[1], r_att[1], r_o[1], d_norm_ffn, r_up[1], d_conv_f, r_dn[1], d_final]
    new_m = [nm_norm_mix, nm_norm_mem, r_in[2], nm_b_gate, nm_conv_a, r_a[2], nm_conv_b, nm_cbias, nm_lng, nm_lnb,
             r_b[2], r_kv[2], r_att[2], r_o[2], nm_norm_ffn, r_up[2], nm_conv_f, r_dn[2], nm_final]
    new_v = [nv_norm_mix, nv_norm_mem, r_in[3], nv_b_gate, nv_conv_a, r_a[3], nv_conv_b, nv_cbias, nv_lng, nv_lnb,
             r_b[3], r_kv[3], r_att[3], r_o[3], nv_norm_ffn, r_up[3], nv_conv_f, r_dn[3], nv_final]
    return (loss, dx[None], *grads, *deltas, *new_m, *new_v)
```

```python
import functools

import jax
import jax.numpy as jnp
import numpy as np
from jax import lax
from jax.experimental import pallas as pl
from jax.experimental.pallas import tpu as pltpu

F32 = jnp.float32
BF = jnp.bfloat16
SDS = jax.ShapeDtypeStruct
MESH = pl.DeviceIdType.MESH
ANY = pl.BlockSpec(memory_space=pl.ANY)

N_DEV = 8
DEPTH = 2
D = 1024
N_HEADS = 4
HEAD = D // N_HEADS
D_FF = 2816
K_A, K_B, K_F = 3, 31, 3
NORM_EPS = 1e-6

C_IN = 9 * D // N_DEV
C_KV = 2 * D // N_DEV
C_UP = 2 * D_FF // N_DEV
LANE = 128
C_UP_P = -(-C_UP // LANE) * LANE
FF_P = 4 * C_UP_P
R_O = D // N_DEV
R_DN = D_FF // N_DEV

VMEM_LIMIT = 56 * 1024 * 1024
TM = 512
TR = 512
TR_FFN = 1024
SUB = 128
H_S, H_L = 16, 32

ADAM_LR, ADAM_B1, ADAM_B2, ADAM_EPS, ADAM_WD, ADAM_STEP = 0.001, 0.9, 0.999, 1e-08, 0.01, 10

UP_ORDER = (0, 4, 1, 5, 2, 6, 3, 7)


def _pcall(body, **kw):
    return pl.pallas_call(body, **kw)


def _cp(sem=None, **kw):
    return pltpu.CompilerParams(dimension_semantics=sem, vmem_limit_bytes=VMEM_LIMIT, **kw)


def _dot(a, b):
    return jnp.dot(a, b, preferred_element_type=F32)


def _dot_nt(a, b):
    return lax.dot_general(a, b, (((1,), (1,)), ((), ())), preferred_element_type=F32)


def _dot_tn(a, b):
    return lax.dot_general(a, b, (((0,), (0,)), ((), ())), preferred_element_type=F32)


def _sigmoid(z):
    return 1.0 / (1.0 + jnp.exp(-z))


def _rms(xv):
    return lax.rsqrt(jnp.mean(xv * xv, axis=-1, keepdims=True) + NORM_EPS)


def _up_slot(idx):
    return jnp.where(idx < 4, 2 * idx, 2 * (idx - 4) + 1)


def _dn_row(idx):
    return C_UP_P * (idx // 2) + R_DN * (idx % 2)


def _mm(a, b, *, ta=False, tb=False, out_dtype=BF, tm=TM, tn=512, tk=None, name):
    m, k_dim = (a.shape[1], a.shape[0]) if ta else a.shape
    n = b.shape[0] if tb else b.shape[1]
    tm, tn = min(tm, m), min(tn, n)
    tk = k_dim if tk is None else min(tk, k_dim)
    nk = k_dim // tk
    assert m % tm == 0 and n % tn == 0 and k_dim % tk == 0
    dims = (((0 if ta else 1,), (1 if tb else 0,)), ((), ()))

    def body(a_ref, b_ref, o_ref, *scratch):
        part = lax.dot_general(a_ref[...], b_ref[...], dims, preferred_element_type=F32)
        if nk == 1:
            o_ref[...] = part.astype(o_ref.dtype)
            return
        acc = scratch[0]
        k = pl.program_id(2)

        @pl.when(k == 0)
        def _():
            acc[...] = part

        @pl.when(k > 0)
        def _():
            acc[...] += part

        @pl.when(k == nk - 1)
        def _():
            o_ref[...] = acc[...].astype(o_ref.dtype)

    a_spec = pl.BlockSpec((tk, tm), lambda i, j, k: (k, i)) if ta else pl.BlockSpec((tm, tk), lambda i, j, k: (i, k))
    b_spec = pl.BlockSpec((tn, tk), lambda i, j, k: (j, k)) if tb else pl.BlockSpec((tk, tn), lambda i, j, k: (k, j))
    return _pcall(
        body, grid=(m // tm, n // tn, nk), in_specs=[a_spec, b_spec],
        out_specs=pl.BlockSpec((tm, tn), lambda i, j, k: (i, j)),
        out_shape=SDS((m, n), out_dtype),
        scratch_shapes=[pltpu.VMEM((tm, tn), F32)] if nk > 1 else [],
        compiler_params=_cp(("parallel", "parallel", "arbitrary")), name=name)(a, b)


def _mm_res_norm(a, w, x, g, *, name):
    s, k_dim = a.shape
    tm = min(TM, s)

    def body(a_ref, w_ref, x_ref, g_ref, xo_ref, h_ref):
        xo = x_ref[...] + _dot(a_ref[...], w_ref[...])
        xo_ref[...] = xo
        h_ref[...] = ((xo * _rms(xo)) * g_ref[...]).astype(BF)

    return _pcall(
        body, grid=(s // tm,),
        in_specs=[pl.BlockSpec((tm, k_dim), lambda i: (i, 0)),
                  pl.BlockSpec((k_dim, D), lambda i: (0, 0), pipeline_mode=pl.Buffered(1)),
                  pl.BlockSpec((tm, D), lambda i: (i, 0)), pl.BlockSpec((1, D), lambda i: (0, 0))],
        out_specs=[pl.BlockSpec((tm, D), lambda i: (i, 0))] * 2,
        out_shape=[SDS((s, D), F32), SDS((s, D), BF)],
        compiler_params=_cp(("parallel",)), name=name)(a, w, x, g)


def _mm_nt_normbwd(da, w, x, dres, g, *, tk=None, name):
    s, k_dim = da.shape
    tm = min(TM, s)
    tk = k_dim if tk is None else tk
    nk = k_dim // tk
    assert k_dim % tk == 0

    def body(da_ref, w_ref, x_ref, dres_ref, g_ref, dx_ref, dxb_ref, dg_ref, *scratch):
        i, k = pl.program_id(0), pl.program_id(1)
        part = _dot_nt(da_ref[...], w_ref[...])
        if nk > 1:
            acc = scratch[0]

            @pl.when(k == 0)
            def _():
                acc[...] = part

            @pl.when(k > 0)
            def _():
                acc[...] += part

        @pl.when((i == 0) & (k == 0))
        def _():
            dg_ref[...] = jnp.zeros_like(dg_ref)

        @pl.when(k == nk - 1)
        def _():
            dh = acc[...] if nk > 1 else part
            xv = x_ref[...]
            r = _rms(xv)
            xn = xv * r
            dg_ref[0:1, :] += jnp.sum(dh * xn, axis=0, keepdims=True)
            dxn = dh * g_ref[...]
            dx = dres_ref[...] + r * (dxn - xn * jnp.mean(dxn * xn, axis=-1, keepdims=True))
            dx_ref[...] = dx
            dxb_ref[...] = dx.astype(BF)

    row = lambda i, k: (i, 0)
    w_spec = (pl.BlockSpec((D, tk), lambda i, k: (0, k)) if nk > 1 else
              pl.BlockSpec((D, tk), lambda i, k: (0, 0), pipeline_mode=pl.Buffered(1)))
    return _pcall(
        body, grid=(s // tm, nk),
        in_specs=[pl.BlockSpec((tm, tk), lambda i, k: (i, k)), w_spec,
                  pl.BlockSpec((tm, D), row), pl.BlockSpec((tm, D), row), pl.BlockSpec((1, D), lambda i, k: (0, 0))],
        out_specs=[pl.BlockSpec((tm, D), row), pl.BlockSpec((tm, D), row), pl.BlockSpec((8, D), lambda i, k: (0, 0))],
        out_shape=[SDS((s, D), F32), SDS((s, D), BF), SDS((8, D), F32)],
        scratch_shapes=[pltpu.VMEM((tm, D), F32)] if nk > 1 else [],
        compiler_params=_cp(("arbitrary", "arbitrary")), name=name)(da, w, x, dres, g)


def _rms_fwd(x, g, *, name):
    s = x.shape[0]
    tm = min(TM, s)

    def body(x_ref, g_ref, h_ref):
        xv = x_ref[...]
        h_ref[...] = ((xv * _rms(xv)) * g_ref[...]).astype(BF)

    return _pcall(
        body, grid=(s // tm,),
        in_specs=[pl.BlockSpec((tm, D), lambda i: (i, 0)), pl.BlockSpec((1, D), lambda i: (0, 0))],
        out_specs=pl.BlockSpec((tm, D), lambda i: (i, 0)), out_shape=SDS((s, D), BF),
        compiler_params=_cp(("parallel",)), name=name)(x, g)


def _loss_head(x, tgt, g, *, name):
    s = x.shape[0]
    tm = min(TM, s)

    def body(x_ref, t_ref, g_ref, dx_ref, dxb_ref, sums_ref):
        @pl.when(pl.program_id(0) == 0)
        def _():
            sums_ref[...] = jnp.zeros_like(sums_ref)

        xv = x_ref[...]
        r = _rms(xv)
        xn = xv * r
        diff = xn * g_ref[...] - t_ref[...]
        sums_ref[0:1, :] += jnp.sum(diff * diff, axis=0, keepdims=True)
        dy = diff * (1.0 / D)
        sums_ref[1:2, :] += jnp.sum(dy * xn, axis=0, keepdims=True)
        dxn = dy * g_ref[...]
        dx = r * (dxn - xn * jnp.mean(dxn * xn, axis=-1, keepdims=True))
        dx_ref[...] = dx
        dxb_ref[...] = dx.astype(BF)

    row = lambda i: (i, 0)
    return _pcall(
        body, grid=(s // tm,),
        in_specs=[pl.BlockSpec((tm, D), row), pl.BlockSpec((tm, D), row), pl.BlockSpec((1, D), lambda i: (0, 0))],
        out_specs=[pl.BlockSpec((tm, D), row), pl.BlockSpec((tm, D), row), pl.BlockSpec((8, D), lambda i: (0, 0))],
        out_shape=[SDS((s, D), F32), SDS((s, D), BF), SDS((8, D), F32)],
        compiler_params=_cp(("arbitrary",)), name=name)(x, tgt, g)


def _halo_before(i, tr, h):
    return jnp.maximum(i * (tr // h) - 1, 0)


def _halo_after(i, tr, h, s):
    return jnp.minimum((i + 1) * (tr // h), s // h - 1)


def _taps(buf, w_ref, sl, k_w, base, rows):
    acc = None
    for k in range(k_w):
        t = w_ref[k:k + 1, sl] * buf[base + k:base + k + rows, sl]
        acc = t if acc is None else acc + t
    return acc


def _taps_rev(buf, w_ref, sl, k_w, base, rows):
    acc = None
    for k in range(k_w):
        t = w_ref[k:k + 1, sl] * buf[base + k_w - 1 - k:base + k_w - 1 - k + rows, sl]
        acc = t if acc is None else acc + t
    return acc


def _tap_grads(dw_ref, dc, buf, sl, k_w, base, rows):
    for k in range(k_w):
        dw_ref[k:k + 1, sl] += jnp.sum(dc * buf[base + k:base + k + rows, sl], axis=0, keepdims=True)


def _bra_fwd(proj, cw, *, name):
    s = proj.shape[0]
    tr, h = min(TR, s), H_S
    sub = min(SUB, tr)

    def body(cur, halo, w_ref, za_ref, cvb):
        i = pl.program_id(0)
        hv = halo[:, D:2 * D].astype(F32) * halo[:, 2 * D:3 * D].astype(F32)
        cvb[0:h, :] = jnp.where(i == 0, 0.0, hv)
        cvb[h:h + tr, :] = cur[:, D:2 * D].astype(F32) * cur[:, 2 * D:3 * D].astype(F32)
        for c in range(D // LANE):
            sl = slice(LANE * c, LANE * c + LANE)
            ca = _taps(cvb, w_ref, sl, K_A, h - (K_A - 1), tr)
            za_ref[:, sl] = (cur[:, sl].astype(F32) * ca).astype(BF)

    return _pcall(
        body, grid=(s // tr,),
        in_specs=[pl.BlockSpec((tr, 3 * D), lambda i: (i, 0)),
                  pl.BlockSpec((h, 3 * D), lambda i: (_halo_before(i, tr, h), 0)),
                  pl.BlockSpec((8, D), lambda i: (0, 0))],
        out_specs=pl.BlockSpec((tr, D), lambda i: (i, 0)), out_shape=SDS((s, D), BF),
        scratch_shapes=[pltpu.VMEM((h + tr, D), F32)],
        compiler_params=_cp(("parallel",)), name=name)(proj, proj, cw)


def _bra_bwd(proj, dza, cw, dproj, *, name):
    s = proj.shape[0]
    tr, h = min(TR, s), H_S
    sub = min(SUB, tr)
    n = s // tr

    def body(before, cur, after, dz_cur, dz_after, w_ref, dproj_in, da_ref, dw_ref, cvb, dcab):
        del dproj_in
        i = pl.program_id(0)

        @pl.when(i == 0)
        def _():
            dw_ref[...] = jnp.zeros_like(dw_ref)

        first, last = i == 0, i == n - 1
        cvb[0:h, :] = jnp.where(first, 0.0, before[:, D:2 * D].astype(F32) * before[:, 2 * D:3 * D].astype(F32))
        cvb[h:h + tr, :] = cur[:, D:2 * D].astype(F32) * cur[:, 2 * D:3 * D].astype(F32)
        dcab[0:tr, :] = dz_cur[...].astype(F32) * cur[:, 0:D].astype(F32)
        dcab[tr:tr + h, :] = jnp.where(last, 0.0, dz_after[...].astype(F32) * after[:, 0:D].astype(F32))
        for c in range(D // LANE):
            sl = slice(LANE * c, LANE * c + LANE)
            gl, vl = slice(D + LANE * c, D + LANE * c + LANE), slice(2 * D + LANE * c, 2 * D + LANE * c + LANE)
            for r0 in range(0, tr, sub):
                rows = slice(r0, r0 + sub)
                ca = _taps(cvb, w_ref, sl, K_A, h - (K_A - 1) + r0, sub)
                da_ref[rows, sl] = (dz_cur[rows, sl].astype(F32) * ca).astype(BF)
                dcv = _taps_rev(dcab, w_ref, sl, K_A, r0, sub)
                da_ref[rows, gl] = (dcv * cur[rows, vl].astype(F32)).astype(BF)
                da_ref[rows, vl] = (dcv * cur[rows, gl].astype(F32)).astype(BF)
                _tap_grads(dw_ref, dcab[rows, sl], cvb, sl, K_A, h - (K_A - 1) + r0, sub)

    return _pcall(
        body, grid=(n,),
        in_specs=[pl.BlockSpec((h, 3 * D), lambda i: (_halo_before(i, tr, h), 0)),
                  pl.BlockSpec((tr, 3 * D), lambda i: (i, 0)),
                  pl.BlockSpec((h, 3 * D), lambda i: (_halo_after(i, tr, h, s), 0)),
                  pl.BlockSpec((tr, D), lambda i: (i, 0)),
                  pl.BlockSpec((h, D), lambda i: (_halo_after(i, tr, h, s), 0)),
                  pl.BlockSpec((8, D), lambda i: (0, 0)), ANY],
        out_specs=[pl.BlockSpec((tr, 3 * D), lambda i: (i, 0)), pl.BlockSpec((8, D), lambda i: (0, 0))],
        out_shape=[SDS(dproj.shape, BF), SDS((8, D), F32)], input_output_aliases={6: 0},
        scratch_shapes=[pltpu.VMEM((h + tr, D), F32), pltpu.VMEM((tr + h, D), F32)],
        compiler_params=_cp(("arbitrary",)), name=name)(proj, proj, proj, dza, dza, cw, dproj)


_U_COL, _UG_COL = 3, 4


def _brb_conv_fwd(proj, cw, bias, *, name):
    s = proj.shape[0]
    tr, h = min(TR, s), H_L
    sub = min(SUB, tr)

    def body(u_cur, ug_cur, u_halo, ug_halo, w_ref, b_ref, cb_ref, glb, shifted):
        i = pl.program_id(0)
        glb[0:h, :] = jnp.where(i == 0, 0.0, u_halo[...].astype(F32) * _sigmoid(ug_halo[...].astype(F32)))
        glb[h:h + tr, :] = u_cur[...].astype(F32) * _sigmoid(ug_cur[...].astype(F32))
        for c in range(D // LANE):
            sl = slice(LANE * c, LANE * c + LANE)
            for r in range(1, 8):
                shifted[r] = glb[8 - r:8 - r + tr + 24, sl]
            for r0 in range(0, tr, sub):
                acc = None
                for k in range(K_B):
                    q, r = divmod(K_B - 1 - k, 8)
                    at = r0 - 8 * q
                    win = shifted[r, 24 + at:24 + at + sub, :] if r else glb[h + at:h + at + sub, sl]
                    term = w_ref[k:k + 1, sl] * win
                    acc = term if acc is None else acc + term
                cb_ref[r0:r0 + sub, sl] = (acc + b_ref[:, sl]).astype(BF)

    return _pcall(
        body, grid=(s // tr,),
        in_specs=[pl.BlockSpec((tr, D), lambda i: (i, _U_COL)), pl.BlockSpec((tr, D), lambda i: (i, _UG_COL)),
                  pl.BlockSpec((h, D), lambda i: (_halo_before(i, tr, h), _U_COL)),
                  pl.BlockSpec((h, D), lambda i: (_halo_before(i, tr, h), _UG_COL)),
                  pl.BlockSpec((32, D), lambda i: (0, 0)), pl.BlockSpec((1, D), lambda i: (0, 0))],
        out_specs=pl.BlockSpec((tr, D), lambda i: (i, 0)), out_shape=SDS((s, D), BF),
        scratch_shapes=[pltpu.VMEM((h + tr, D), F32), pltpu.VMEM((8, tr + 24, LANE), F32)],
        compiler_params=_cp(("parallel",)), name=name)(proj, proj, proj, proj, cw, bias)


def _brb_conv_bwd(proj, dcb, dq, cw, dproj, *, name):
    s = proj.shape[0]
    tr, h = min(TR, s), H_L
    sub = min(SUB, tr)
    n = s // tr
    nb = -(-(tr + 24) // sub)
    sel = _row_selector(sub, list(range(8)))

    def body(u_cur, ug_cur, d_cur, d_after, dq_ref, sel_ref, w_ref, dproj_in, db_ref, dw_ref, dcbb, shifted):
        del dproj_in
        i = pl.program_id(0)

        @pl.when(i == 0)
        def _():
            dw_ref[...] = jnp.zeros_like(dw_ref)

        db_ref[:, 2 * D:3 * D] = dq_ref[...]
        after = d_after[...]
        dcbb[0:tr, :] = d_cur[...]
        dcbb[tr:tr + h, :] = jnp.where(i == n - 1, jnp.zeros_like(after), after)
        dcbb[tr + h:(nb + 1) * sub, :] = jnp.zeros(((nb + 1) * sub - h - tr, D), BF)
        for c in range(D // LANE):
            sl = slice(LANE * c, LANE * c + LANE)
            for blk in range(nb):
                res = _dot(sel_ref[...], dcbb[blk * sub:(blk + 2) * sub, sl])
                for r in range(8):
                    shifted[r, blk * sub:(blk + 1) * sub, :] = res[r * sub:(r + 1) * sub]
            for r0 in range(0, tr, sub):
                u = u_cur[r0:r0 + sub, sl].astype(F32)
                sg = _sigmoid(ug_cur[r0:r0 + sub, sl].astype(F32))
                glu = u * sg
                dglu = None
                for k in range(K_B):
                    q, r = divmod(K_B - 1 - k, 8)
                    at = r0 + 8 * q
                    win = shifted[r, at:at + sub, :]
                    term = w_ref[k:k + 1, sl] * win
                    dglu = term if dglu is None else dglu + term
                    dw_ref[k:k + 1, sl] += jnp.sum(win * glu, axis=0, keepdims=True)
                db_ref[r0:r0 + sub, sl] = (dglu * sg).astype(BF)
                db_ref[r0:r0 + sub, D + LANE * c:D + LANE * c + LANE] = (dglu * u * sg * (1.0 - sg)).astype(BF)

    return _pcall(
        body, grid=(n,),
        in_specs=[pl.BlockSpec((tr, D), lambda i: (i, _U_COL)), pl.BlockSpec((tr, D), lambda i: (i, _UG_COL)),
                  pl.BlockSpec((tr, D), lambda i: (i, 0)),
                  pl.BlockSpec((h, D), lambda i: (_halo_after(i, tr, h, s), 0)),
                  pl.BlockSpec((tr, D), lambda i: (i, 0)),
                  pl.BlockSpec(sel.shape, lambda i: (0, 0)),
                  pl.BlockSpec((32, D), lambda i: (0, 0)), ANY],
        out_specs=[pl.BlockSpec((tr, 3 * D), lambda i: (i, 1)), pl.BlockSpec((32, D), lambda i: (0, 0))],
        out_shape=[SDS(dproj.shape, BF), SDS((32, D), F32)], input_output_aliases={7: 0},
        scratch_shapes=[pltpu.VMEM(((nb + 1) * sub, D), BF), pltpu.VMEM((8, nb * sub, LANE), F32)],
        compiler_params=_cp(("arbitrary",)), name=name)(proj, proj, dcb, dcb, dq, sel, cw, dproj)


def _ln_silu_fwd(cb, g, b, *, name):
    s = cb.shape[0]
    tm = min(TM, s)

    def body(cb_ref, g_ref, b_ref, sb_ref):
        z = cb_ref[...].astype(F32)
        zc = z - jnp.mean(z, axis=-1, keepdims=True)
        ln = (zc * lax.rsqrt(jnp.mean(zc * zc, axis=-1, keepdims=True) + NORM_EPS)) * g_ref[...] + b_ref[...]
        sb_ref[...] = (ln * _sigmoid(ln)).astype(BF)

    row = lambda i: (i, 0)
    vec = pl.BlockSpec((1, D), lambda i: (0, 0))
    return _pcall(
        body, grid=(s // tm,), in_specs=[pl.BlockSpec((tm, D), row), vec, vec],
        out_specs=pl.BlockSpec((tm, D), row), out_shape=SDS((s, D), BF),
        compiler_params=_cp(("parallel",)), name=name)(cb, g, b)


def _ln_silu_bwd(cb, dsb, g, b, *, name):
    s = cb.shape[0]
    tm = min(TM, s)

    def body(cb_ref, dsb_ref, g_ref, b_ref, dcb_ref, sums_ref):
        @pl.when(pl.program_id(0) == 0)
        def _():
            sums_ref[...] = jnp.zeros_like(sums_ref)

        z = cb_ref[...].astype(F32)
        zc = z - jnp.mean(z, axis=-1, keepdims=True)
        rstd = lax.rsqrt(jnp.mean(zc * zc, axis=-1, keepdims=True) + NORM_EPS)
        lnh = zc * rstd
        ln = lnh * g_ref[...] + b_ref[...]
        sg = _sigmoid(ln)
        dln = dsb_ref[...].astype(F32) * (sg * (1.0 + ln * (1.0 - sg)))
        sums_ref[0:1, :] += jnp.sum(dln * lnh, axis=0, keepdims=True)
        sums_ref[1:2, :] += jnp.sum(dln, axis=0, keepdims=True)
        dlnh = dln * g_ref[...]
        dz = rstd * (dlnh - jnp.mean(dlnh, axis=-1, keepdims=True)
                     - lnh * jnp.mean(dlnh * lnh, axis=-1, keepdims=True))
        sums_ref[2:3, :] += jnp.sum(dz, axis=0, keepdims=True)
        dcb_ref[...] = dz.astype(BF)

    row = lambda i: (i, 0)
    vec = pl.BlockSpec((1, D), lambda i: (0, 0))
    return _pcall(
        body, grid=(s // tm,), in_specs=[pl.BlockSpec((tm, D), row), pl.BlockSpec((tm, D), row), vec, vec],
        out_specs=[pl.BlockSpec((tm, D), row), pl.BlockSpec((8, D), lambda i: (0, 0))],
        out_shape=[SDS((s, D), BF), SDS((8, D), F32)],
        compiler_params=_cp(("arbitrary",)), name=name)(cb, dsb, g, b)


_Q_COL = 5 * D // HEAD


def _kv_prep(mem, g, wkv, *, name):
    m = mem.shape[0]

    def body(mem_ref, g_ref, w_ref, memn_ref, kv_ref):
        mv = mem_ref[...]
        memn = ((mv * _rms(mv)) * g_ref[...]).astype(BF)
        memn_ref[...] = memn
        for dev in range(N_DEV):
            kv_ref[:, dev * C_KV:(dev + 1) * C_KV] = _dot(memn, w_ref[dev]).astype(BF)

    return _pcall(body, out_shape=[SDS((m, D), BF), SDS((m, 2 * D), BF)],
                  compiler_params=_cp(), name=name)(mem, g, wkv)


def _softmax_rows(q, k):
    sc = _dot_nt(q, k) * (1.0 / (HEAD ** 0.5))
    e = jnp.exp(sc - jnp.max(sc, axis=-1, keepdims=True))
    return e / jnp.sum(e, axis=-1, keepdims=True)


def _attn_fwd(proj, kv, *, name):
    s, m = proj.shape[0], kv.shape[0]
    tm = min(TM, s)

    def body(q_ref, kv_ref, o_ref):
        for hd in range(N_HEADS):
            cols = slice(hd * HEAD, (hd + 1) * HEAD)
            p = _softmax_rows(q_ref[:, cols], kv_ref[:, cols])
            o_ref[:, cols] = _dot(p.astype(BF), kv_ref[:, D + hd * HEAD:D + (hd + 1) * HEAD]).astype(BF)

    return _pcall(
        body, grid=(s // tm,),
        in_specs=[pl.BlockSpec((tm, D), lambda i: (i, _Q_COL // N_HEADS)),
                  pl.BlockSpec((m, 2 * D), lambda i: (0, 0))],
        out_specs=pl.BlockSpec((tm, D), lambda i: (i, 0)), out_shape=SDS((s, D), BF),
        compiler_params=_cp(("parallel",)), name=name)(proj, kv)


def _attn_bwd(proj, kv, do, *, name):
    s, m = proj.shape[0], kv.shape[0]
    tm = min(TM, s)

    def body(q_ref, kv_ref, do_ref, dq_ref, dk_ref, dv_ref):
        @pl.when(pl.program_id(0) == 0)
        def _():
            dk_ref[...] = jnp.zeros_like(dk_ref)
            dv_ref[...] = jnp.zeros_like(dv_ref)

        for hd in range(N_HEADS):
            cols = slice(hd * HEAD, (hd + 1) * HEAD)
            q, k, dov = q_ref[:, cols], kv_ref[:, cols], do_ref[:, cols]
            p = _softmax_rows(q, k)
            dp = _dot_nt(dov, kv_ref[:, D + hd * HEAD:D + (hd + 1) * HEAD])
            dv_ref[:, cols] += _dot_tn(p.astype(BF), dov)
            ds = (p * (dp - jnp.sum(dp * p, axis=-1, keepdims=True)) * (1.0 / (HEAD ** 0.5))).astype(BF)
            dq_ref[:, cols] = _dot(ds, k).astype(BF)
            dk_ref[:, cols] += _dot_tn(ds, q)

    return _pcall(
        body, grid=(s // tm,),
        in_specs=[pl.BlockSpec((tm, D), lambda i: (i, _Q_COL // N_HEADS)),
                  pl.BlockSpec((m, 2 * D), lambda i: (0, 0)),
                  pl.BlockSpec((tm, D), lambda i: (i, 0))],
        out_specs=[pl.BlockSpec((tm, D), lambda i: (i, 0)),
                   pl.BlockSpec((m, D), lambda i: (0, 0)),
                   pl.BlockSpec((m, D), lambda i: (0, 0))],
        out_shape=[SDS((s, D), BF), SDS((m, D), F32), SDS((m, D), F32)],
        compiler_params=_cp(("arbitrary",)), name=name)(proj, kv, do)


def _kv_bwd(mem, g, memn, dk, dv, wkv, *, name):
    def body(mem_ref, g_ref, memn_ref, dk_ref, dv_ref, w_ref, dw_ref, dg_ref):
        memn = memn_ref[...]
        dmemn = None
        for dev in range(N_DEV):
            d_ref, col = (dk_ref, dev) if dev < N_HEADS else (dv_ref, dev - N_HEADS)
            dslab = d_ref[:, col * C_KV:(col + 1) * C_KV].astype(BF)
            dw_ref[dev] = _dot_tn(memn, dslab).astype(BF)
            part = _dot_nt(dslab, w_ref[dev])
            dmemn = part if dmemn is None else dmemn + part
        mv = mem_ref[...]
        dg_ref[...] = jnp.zeros_like(dg_ref)
        dg_ref[0:1, :] = jnp.sum(dmemn * (mv * _rms(mv)), axis=0, keepdims=True)

    assert C_KV == HEAD
    return _pcall(body, out_shape=[SDS((N_DEV, D, C_KV), BF), SDS((8, D), F32)],
                  compiler_params=_cp(), name=name)(mem, g, memn, dk, dv, wkv)


_TM_MIX = 512


def _mix_out(x, za, sb, o, proj, w4, bg, g_next, *, name):
    s = x.shape[0]
    tm = min(_TM_MIX, s)

    def body(x_ref, za_ref, sb_ref, o_ref, pg_ref, w4_ref, bg_ref, gn_ref,
             ya_ref, yb_ref, yc_ref, mg_ref, x1_ref, h_ref):
        ys = (_dot(za_ref[...], w4_ref[0]), _dot(sb_ref[...], w4_ref[1]), _dot(o_ref[...], w4_ref[2]))
        merged = None
        for j, (y, y_ref) in enumerate(zip(ys, (ya_ref, yb_ref, yc_ref))):
            y_ref[...] = y.astype(BF)
            gate = _sigmoid(pg_ref[:, j * D:(j + 1) * D].astype(F32) + bg_ref[:, j * D:(j + 1) * D])
            merged = gate * y if merged is None else merged + gate * y
        mg = merged.astype(BF)
        mg_ref[...] = mg
        x1 = x_ref[...] + _dot(mg, w4_ref[3])
        x1_ref[...] = x1
        h_ref[...] = ((x1 * _rms(x1)) * gn_ref[...]).astype(BF)

    row = lambda i: (i, 0)
    act = pl.BlockSpec((tm, D), row)
    return _pcall(
        body, grid=(s // tm,),
        in_specs=[act, act, act, act, pl.BlockSpec((tm, 3 * D), lambda i: (i, 2)),
                  pl.BlockSpec((4, D, D), lambda i: (0, 0, 0), pipeline_mode=pl.Buffered(1)), pl.BlockSpec((1, 3 * D), lambda i: (0, 0)),
                  pl.BlockSpec((1, D), lambda i: (0, 0))],
        out_specs=[act] * 6,
        out_shape=[SDS((s, D), BF)] * 4 + [SDS((s, D), F32), SDS((s, D), BF)],
        compiler_params=_cp(("parallel",)), name=name)(x, za, sb, o, proj, w4, bg, g_next)


def _mix_bwd(dxb, ya, yb, yc, proj, w4, bg, *, name):
    s = dxb.shape[0]
    tm = min(_TM_MIX, s)

    def body(dx_ref, ya_ref, yb_ref, yc_ref, pg_ref, w4_ref, bg_ref,
             dya_ref, dyb_ref, dyc_ref, dza_ref, dsb_ref, do_ref, dgt_ref, dbg_ref):
        @pl.when(pl.program_id(0) == 0)
        def _():
            dbg_ref[...] = jnp.zeros_like(dbg_ref)

        dm = _dot_nt(dx_ref[...], w4_ref[3])
        for j, (y_ref, dy_ref, din_ref) in enumerate(zip((ya_ref, yb_ref, yc_ref), (dya_ref, dyb_ref, dyc_ref),
                                                         (dza_ref, dsb_ref, do_ref))):
            cols = slice(j * D, (j + 1) * D)
            gate = _sigmoid(pg_ref[:, cols].astype(F32) + bg_ref[:, cols])
            dy = (dm * gate).astype(BF)
            dy_ref[...] = dy
            din_ref[...] = _dot_nt(dy, w4_ref[j]).astype(BF)
            dpre = dm * y_ref[...].astype(F32) * gate * (1.0 - gate)
            dgt_ref[:, cols] = dpre.astype(BF)
            dbg_ref[0:1, cols] += jnp.sum(dpre, axis=0, keepdims=True)

    row = lambda i: (i, 0)
    act = pl.BlockSpec((tm, D), row)
    return _pcall(
        body, grid=(s // tm,),
        in_specs=[act, act, act, act, pl.BlockSpec((tm, 3 * D), lambda i: (i, 2)),
                  pl.BlockSpec((4, D, D), lambda i: (0, 0, 0), pipeline_mode=pl.Buffered(1)), pl.BlockSpec((1, 3 * D), lambda i: (0, 0))],
        out_specs=[act] * 6 + [pl.BlockSpec((tm, 3 * D), lambda i: (i, 2)),
                               pl.BlockSpec((8, 3 * D), lambda i: (0, 0))],
        out_shape=[SDS((s, D), BF)] * 6 + [SDS((s, 9 * D), BF), SDS((8, 3 * D), F32)],
        compiler_params=_cp(("arbitrary",)), name=name)(dxb, ya, yb, yc, proj, w4, bg)


_PAIR = 2 * C_UP_P


def _row_selector(sub, first_cols):
    rows = np.arange(len(first_cols) * sub)
    col = np.asarray(first_cols)[rows // sub] + rows % sub
    return jnp.asarray(np.arange(2 * sub)[None, :] == col[:, None], BF)


def _ffn_act(u2, cw, *, name):
    s = u2.shape[0]
    tr = min(TR_FFN, s)
    sub = min(SUB, tr)
    sel = _row_selector(sub, [sub - (K_F - 1 - k) for k in range(K_F)])

    def body(cur, prev, sel_ref, w_ref, act_ref, c2_ref, xb, win):
        i = pl.program_id(1)
        before = prev[...]
        xb[0:sub, :] = jnp.where(i == 0, jnp.zeros_like(before), before)
        xb[sub:sub + tr, :] = cur[...]
        for r0 in range(0, tr, sub):
            win[...] = _dot(sel_ref[...], xb[r0:r0 + 2 * sub, :])
            for c in range(C_UP_P // LANE):
                gl = slice(LANE * c, LANE * c + LANE)
                ul = slice(C_UP_P + LANE * c, C_UP_P + LANE * c + LANE)
                gt = sum(w_ref[k:k + 1, gl] * win[k * sub:(k + 1) * sub, gl] for k in range(K_F))
                up = sum(w_ref[k:k + 1, ul] * win[k * sub:(k + 1) * sub, ul] for k in range(K_F))
                c2_ref[r0:r0 + sub, gl] = gt.astype(BF)
                c2_ref[r0:r0 + sub, ul] = up.astype(BF)
                act_ref[r0:r0 + sub, gl] = (gt * _sigmoid(gt) * up).astype(BF)

    return _pcall(
        body, grid=(4, s // tr),
        in_specs=[pl.BlockSpec((tr, _PAIR), lambda p, i: (i, p)),
                  pl.BlockSpec((sub, _PAIR), lambda p, i: (_halo_before(i, tr, sub), p)),
                  pl.BlockSpec(sel.shape, lambda p, i: (0, 0)),
                  pl.BlockSpec((8, _PAIR), lambda p, i: (0, p))],
        out_specs=[pl.BlockSpec((tr, C_UP_P), lambda p, i: (i, p)), pl.BlockSpec((tr, _PAIR), lambda p, i: (i, p))],
        out_shape=[SDS((s, FF_P), BF), SDS((s, 2 * FF_P), BF)],
        scratch_shapes=[pltpu.VMEM((sub + tr, _PAIR), BF), pltpu.VMEM((K_F * sub, _PAIR), F32)],
        compiler_params=_cp(("parallel", "parallel")), name=name)(u2, u2, sel, cw)


def _ffn_bwd(u2, c2, dact, cw, *, name):
    s = u2.shape[0]
    tr, h = min(TR_FFN, s), H_S
    sub = min(SUB, tr)
    n = s // tr
    sel = _row_selector(sub, [K_F - 1 - k for k in range(K_F)])

    def body(u_cur, c_cur, c_after, da_cur, da_after, sel_ref, w_ref, du_ref, dw_ref, dcb, win):
        i = pl.program_id(1)
        last = i == n - 1

        @pl.when(i == 0)
        def _():
            dw_ref[...] = jnp.zeros_like(dw_ref)

        def conv_grad(gt, up, da):
            gt, up, da = gt.astype(F32), up.astype(F32), da.astype(F32)
            sg = _sigmoid(gt)
            return (da * up * (sg * (1.0 + gt * (1.0 - sg)))).astype(BF), (da * (gt * sg)).astype(BF)

        for c in range(C_UP_P // LANE):
            gl = slice(LANE * c, LANE * c + LANE)
            ul = slice(C_UP_P + LANE * c, C_UP_P + LANE * c + LANE)
            for r0 in range(0, tr, sub):
                rows = slice(r0, r0 + sub)
                dcb[rows, gl], dcb[rows, ul] = conv_grad(c_cur[rows, gl], c_cur[rows, ul], da_cur[rows, gl])
            dg, du_ = conv_grad(c_after[:, gl], c_after[:, ul], da_after[:, gl])
            dcb[tr:tr + h, gl] = jnp.where(last, jnp.zeros_like(dg), dg)
            dcb[tr:tr + h, ul] = jnp.where(last, jnp.zeros_like(du_), du_)
        dcb[tr + h:tr + sub, :] = jnp.zeros((sub - h, _PAIR), BF)
        for r0 in range(0, tr, sub):
            win[...] = _dot(sel_ref[...], dcb[r0:r0 + 2 * sub, :])
            for c in range(_PAIR // LANE):
                sl = slice(LANE * c, LANE * c + LANE)
                u = u_cur[r0:r0 + sub, sl].astype(F32)
                du = None
                for k in range(K_F):
                    wk = win[k * sub:(k + 1) * sub, sl]
                    term = w_ref[k:k + 1, sl] * wk
                    du = term if du is None else du + term
                    dw_ref[k:k + 1, sl] += jnp.sum(wk * u, axis=0, keepdims=True)
                du_ref[r0:r0 + sub, sl] = du.astype(BF)

    return _pcall(
        body, grid=(4, n),
        in_specs=[pl.BlockSpec((tr, _PAIR), lambda p, i: (i, p)),
                  pl.BlockSpec((tr, _PAIR), lambda p, i: (i, p)),
                  pl.BlockSpec((h, _PAIR), lambda p, i: (_halo_after(i, tr, h, s), p)),
                  pl.BlockSpec((tr, C_UP_P), lambda p, i: (i, p)),
                  pl.BlockSpec((h, C_UP_P), lambda p, i: (_halo_after(i, tr, h, s), p)),
                  pl.BlockSpec(sel.shape, lambda p, i: (0, 0)),
                  pl.BlockSpec((8, _PAIR), lambda p, i: (0, p))],
        out_specs=[pl.BlockSpec((tr, _PAIR), lambda p, i: (i, p)), pl.BlockSpec((8, _PAIR), lambda p, i: (0, p))],
        out_shape=[SDS((s, 2 * FF_P), BF), SDS((8, 2 * FF_P), F32)],
        scratch_shapes=[pltpu.VMEM((tr + sub, _PAIR), BF), pltpu.VMEM((K_F * sub, _PAIR), F32)],
        compiler_params=_cp(("parallel", "arbitrary")), name=name)(u2, c2, c2, dact, dact, sel, cw)


def _relations():
    x, y, c = lax.axis_index("x"), lax.axis_index("y"), lax.axis_index("c")
    out = []
    for r in range(1, N_DEV):
        rx, ry, rc = (r >> 2) & 1, (r >> 1) & 1, r & 1
        out.append((r, (x ^ rx, y ^ ry, c ^ rc)))
    return out


def _my_index():
    return 4 * lax.axis_index("x") + 2 * lax.axis_index("y") + lax.axis_index("c")


def _slab(kind, ref, idx):
    if kind == "win":
        return ref.at[:, pl.ds(pl.multiple_of(idx * C_IN, LANE), C_IN)]
    if kind == "wup":
        return ref.at[:, pl.ds(pl.multiple_of(_up_slot(idx) * C_UP_P, LANE), C_UP_P)]
    if kind == "wupT":
        return ref.at[pl.ds(pl.multiple_of(_up_slot(idx) * C_UP_P, LANE), C_UP_P), :]
    if kind == "wkv":
        return ref.at[idx]
    if kind == "w4":
        return ref.at[:, pl.ds(pl.multiple_of(idx * R_O, 16), R_O), :]
    if kind == "wdn":
        return ref.at[pl.ds(pl.multiple_of(_dn_row(idx), 16), R_DN), :]
    assert kind == "cv"
    return ref.at[idx]


_WHOLE = {"win": ((D, 9 * D), BF), "wup": ((D, 2 * FF_P), BF), "wkv": ((N_DEV, D, C_KV), BF),
          "w4": ((4, D, D), BF), "wdn": ((FF_P, D), BF)}
_SHARD = {"win": (D, C_IN), "wup": (D, C_UP_P), "wupT": (C_UP_P, D), "wkv": (D, C_KV), "w4": (4, R_O, D),
          "wdn": (R_DN, D)}
HBM_SPEC = pl.BlockSpec(memory_space=pltpu.HBM)
SEM_SPEC = pl.BlockSpec(memory_space=pltpu.SEMAPHORE)
_DATAFLOW = pltpu.SideEffectType.DATAFLOW_SIDE_EFFECTING


def _scatter_maps(kinds):
    return ((lambda srcs, lands, a, idx: _slab(kinds[a], srcs[a], idx)),
            (lambda lands, a, idx: lands[a].at[idx]))


_SLOTTED = ("wkv", "cv")


def _own_slab_blocks(kind, shard_shape):
    if kind in ("win", "wup"):
        rows, slot = 256, (_up_slot if kind == "wup" else (lambda m: m))
        return (shard_shape[0] // rows, (rows, shard_shape[1]), (lambda i, me: (i, slot(me[0]))),
                (lambda i, me: (i, 0)), (lambda i, me: (me[0], i, 0)))
    if kind == "wupT":
        rows = 256
        steps = shard_shape[0] // rows
        return (steps, (rows, D), (lambda i, me: (_up_slot(me[0]) * steps + i, 0)), (lambda i, me: (i, 0)),
                (lambda i, me: (me[0], i, 0)))
    if kind == "w4":
        return (1, shard_shape, (lambda i, me: (0, me[0], 0)), (lambda i, me: (0, 0, 0)),
                (lambda i, me: (me[0], 0, 0, 0)))
    if kind == "wdn":
        rows = 32
        return (R_DN // rows, (rows, D), (lambda i, me: (_dn_row(me[0]) // rows + i, 0)), (lambda i, me: (i, 0)),
                (lambda i, me: (me[0], i, 0)))
    assert kind in _SLOTTED
    rows = min(256, shard_shape[0])
    return (shard_shape[0] // rows, (rows, shard_shape[1]), (lambda i, me: (me[0], i, 0)),
            (lambda i, me: (i, 0)), (lambda i, me: (me[0], i, 0)))


def _place_own(kind, src, out_sds, gather, me_arr, *, after=None, name):
    shard_shape = src.shape if gather else out_sds.shape[1:]
    steps, blk, whole_idx, shard_idx, staging_idx = _own_slab_blocks(kind, shard_shape)
    slotted = kind in _SLOTTED
    whole_spec = pl.BlockSpec(((None,) if slotted else ()) + tuple(blk), whole_idx)
    if gather:
        in_spec, out_spec = pl.BlockSpec(tuple(blk), shard_idx), whole_spec
    else:
        in_spec, out_spec = whole_spec, pl.BlockSpec((None,) + tuple(blk), staging_idx)
    zero_init = gather and kind == "wdn"

    def body(me_ref, src_ref, *rest):
        rest[-1][...] = src_ref[...].astype(rest[-1].dtype)

    operands = (me_arr, src) + ((jnp.zeros(out_sds.shape, out_sds.dtype),) if zero_init else ())
    operands += () if after is None else (after,)
    return _pcall(
        body,
        grid_spec=pltpu.PrefetchScalarGridSpec(
            num_scalar_prefetch=1, grid=(steps,), in_specs=[in_spec] + [ANY] * (len(operands) - 2),
            out_specs=out_spec),
        out_shape=out_sds, input_output_aliases={2: 0} if zero_init else {},
        compiler_params=_cp(("arbitrary",)), name=name)(*operands)


def _peer_copies(n, src_of, dst_of, src_r, land_r, ssem, rsem):
    me = _my_index()
    out = []
    for r, peer in _relations():
        p_idx = 4 * peer[0] + 2 * peer[1] + peer[2]
        for a in range(n):
            def copy(src_idx, dst_idx, a=a, r=r, peer=peer):
                sem = a * (N_DEV - 1) + r - 1
                return pltpu.make_async_remote_copy(
                    src_ref=src_of(src_r, land_r, a, src_idx), dst_ref=dst_of(land_r, a, dst_idx),
                    send_sem=ssem.at[sem], recv_sem=rsem.at[sem], device_id=peer, device_id_type=MESH)
            out.append((functools.partial(copy, p_idx, me), functools.partial(copy, me, p_idx)))
    return out


def _exchange_start(srcs, lands, maps, after, *, name):
    n, ns = len(lands), len(srcs)
    src_of, dst_of = maps

    def body(*refs):
        src_r, land_r = refs[:ns], refs[ns:ns + n]
        ssem, rsem, token = refs[ns + n + 1], refs[ns + n + 2], refs[-1]
        for send, _ in _peer_copies(n, src_of, dst_of, src_r, land_r, ssem, rsem):
            send().start()
        token[...] = jnp.zeros_like(token)

    flight = list(srcs) + list(lands)
    outs = pl.pallas_call(
        body, name=name,
        out_shape=(pltpu.SemaphoreType.DMA((n * (N_DEV - 1),)), pltpu.SemaphoreType.DMA((n * (N_DEV - 1),)),
                   *[pltpu.HBM(a.shape, a.dtype) for a in flight], SDS((8, LANE), F32)),
        in_specs=[HBM_SPEC] * (ns + n) + [ANY],
        out_specs=(SEM_SPEC, SEM_SPEC, *[HBM_SPEC] * (ns + n), pl.BlockSpec(memory_space=pltpu.VMEM)),
        input_output_aliases={i: 2 + i for i in range(ns + n)},
        compiler_params=pltpu.CompilerParams(has_side_effects=_DATAFLOW),
    )(*[pltpu.with_memory_space_constraint(a, pltpu.HBM) for a in flight], after)
    return (outs[0], outs[1], list(outs[2:2 + ns + n]), ns), outs[-1]


def _exchange_wait(handle, maps, after, *, name):
    ssem, rsem, flight, ns = handle
    n = len(flight) - ns
    src_of, dst_of = maps

    def body(*refs):
        src_r, land_r, ssem_r, rsem_r = refs[:ns], refs[ns:ns + n], refs[ns + n], refs[ns + n + 1]
        for send, arrival in _peer_copies(n, src_of, dst_of, src_r, land_r, ssem_r, rsem_r):
            send().wait_send()
            arrival().wait_recv()

    outs = pl.pallas_call(
        body, name=name, out_shape=[pltpu.HBM(a.shape, a.dtype) for a in flight],
        in_specs=[HBM_SPEC] * (ns + n) + [SEM_SPEC, SEM_SPEC, ANY], out_specs=[HBM_SPEC] * (ns + n),
        input_output_aliases={i: i for i in range(ns + n)},
        compiler_params=pltpu.CompilerParams(has_side_effects=_DATAFLOW),
    )(*flight, ssem, rsem, after)
    return list(outs[ns:])


_SIBLING = 1
_ICI = (2, 4, 6)


def _rel_peer(r):
    x, y, c = lax.axis_index("x"), lax.axis_index("y"), lax.axis_index("c")
    peer = (x ^ ((r >> 2) & 1), y ^ ((r >> 1) & 1), c ^ (r & 1))
    return peer, 4 * peer[0] + 2 * peer[1] + peer[2]


def _rcopy(ref, ssem, rsem, peer):
    return pltpu.make_async_remote_copy(src_ref=ref, dst_ref=ref, send_sem=ssem, recv_sem=rsem, device_id=peer,
                                        device_id_type=MESH)


def _gather2_start(lands, kinds, after, *, name):
    n = len(lands)

    def body(*refs):
        land_r, (send1, recv_sib, recv_ici), token = refs[:n], refs[n + 1:n + 4], refs[-1]
        me = _my_index()
        for a in range(n):
            own = _slab(kinds[a], land_r[a], me)
            for j, r in enumerate((_SIBLING,) + _ICI):
                rsem = recv_sib.at[a] if r == _SIBLING else recv_ici.at[3 * a + j - 1]
                _rcopy(own, send1.at[4 * a + j], rsem, _rel_peer(r)[0]).start()
        token[...] = jnp.zeros_like(token)

    sems = [pltpu.SemaphoreType.DMA((4 * n,)), pltpu.SemaphoreType.DMA((n,)), pltpu.SemaphoreType.DMA((3 * n,))]
    outs = pl.pallas_call(
        body, name=name, out_shape=(*sems, *[pltpu.HBM(a.shape, a.dtype) for a in lands], SDS((8, LANE), F32)),
        in_specs=[HBM_SPEC] * n + [ANY],
        out_specs=(SEM_SPEC,) * 3 + (HBM_SPEC,) * n + (pl.BlockSpec(memory_space=pltpu.VMEM),),
        input_output_aliases={i: 3 + i for i in range(n)},
        compiler_params=pltpu.CompilerParams(has_side_effects=_DATAFLOW),
    )(*[pltpu.with_memory_space_constraint(a, pltpu.HBM) for a in lands], after)
    return dict(send1=outs[0], recv_sib=outs[1], recv_ici=outs[2], lands=list(outs[3:3 + n])), outs[-1]


def _gather2_forward(handle, kinds, after, *, name):
    lands = handle["lands"]
    n = len(lands)

    def body(*refs):
        land_r, recv_ici, (fwd_send, fwd_recv), token = refs[:n], refs[n], refs[n + 2:n + 4], refs[-1]
        sibling = _rel_peer(_SIBLING)[0]
        for a in range(n):
            for j, r in enumerate(_ICI):
                got = _slab(kinds[a], land_r[a], _rel_peer(r)[1])
                _rcopy(got, fwd_send.at[3 * a + j], recv_ici.at[3 * a + j], sibling).wait_recv()
                _rcopy(got, fwd_send.at[3 * a + j], fwd_recv.at[3 * a + j], sibling).start()
        token[...] = jnp.zeros_like(token)

    sems = [pltpu.SemaphoreType.DMA((3 * n,)), pltpu.SemaphoreType.DMA((3 * n,))]
    outs = pl.pallas_call(
        body, name=name, out_shape=(*sems, *[pltpu.HBM(a.shape, a.dtype) for a in lands], SDS((8, LANE), F32)),
        in_specs=[HBM_SPEC] * n + [SEM_SPEC, ANY],
        out_specs=(SEM_SPEC,) * 2 + (HBM_SPEC,) * n + (pl.BlockSpec(memory_space=pltpu.VMEM),),
        input_output_aliases={i: 2 + i for i in range(n)},
        compiler_params=pltpu.CompilerParams(has_side_effects=_DATAFLOW),
    )(*lands, handle["recv_ici"], after)
    return dict(handle, fwd_send=outs[0], fwd_recv=outs[1], lands=list(outs[2:2 + n])), outs[-1]


def _gather2_wait(handle, kinds, after, *, name):
    lands = handle["lands"]
    n = len(lands)

    def body(*refs):
        land_r, (send1, recv_sib, fwd_send, fwd_recv) = refs[:n], refs[n:n + 4]
        me = _my_index()
        sibling, sib_idx = _rel_peer(_SIBLING)
        for a in range(n):
            own = _slab(kinds[a], land_r[a], me)
            for j, r in enumerate((_SIBLING,) + _ICI):
                _rcopy(own, send1.at[4 * a + j], recv_sib.at[a], _rel_peer(r)[0]).wait_send()
            theirs = _slab(kinds[a], land_r[a], sib_idx)
            _rcopy(theirs, send1.at[4 * a], recv_sib.at[a], sibling).wait_recv()
            for j, r in enumerate(_ICI):
                passed_on = _slab(kinds[a], land_r[a], _rel_peer(r)[1])
                _rcopy(passed_on, fwd_send.at[3 * a + j], fwd_recv.at[3 * a + j], sibling).wait_send()
                arrived = _slab(kinds[a], land_r[a], _rel_peer(r ^ _SIBLING)[1])
                _rcopy(arrived, fwd_send.at[3 * a + j], fwd_recv.at[3 * a + j], sibling).wait_recv()

    outs = pl.pallas_call(
        body, name=name, out_shape=[pltpu.HBM(a.shape, a.dtype) for a in lands],
        in_specs=[HBM_SPEC] * n + [SEM_SPEC] * 4 + [ANY], out_specs=[HBM_SPEC] * n,
        input_output_aliases={i: i for i in range(n)},
        compiler_params=pltpu.CompilerParams(has_side_effects=_DATAFLOW),
    )(*lands, handle["send1"], handle["recv_sib"], handle["fwd_send"], handle["fwd_recv"], after)
    return list(outs)


def _sum_slots(gathered, *, name):
    def body(g_ref, out_ref):
        total = g_ref[0]
        for dev in range(1, N_DEV):
            total = total + g_ref[dev]
        out_ref[...] = total

    return _pcall(body, out_shape=SDS(gathered.shape[1:], F32), compiler_params=_cp(), name=name)(gathered)


def _adam(g, w, m, v):
    nm = ADAM_B1 * m + (1.0 - ADAM_B1) * g
    nv = ADAM_B2 * v + (1.0 - ADAM_B2) * (g * g)
    m_hat = nm / (1.0 - ADAM_B1 ** ADAM_STEP)
    v_hat = nv / (1.0 - ADAM_B2 ** ADAM_STEP)
    return -ADAM_LR * (m_hat / (jnp.sqrt(v_hat) + ADAM_EPS) + ADAM_WD * w), nm, nv


def _adamw_staged(st0, st1, w, m, v, *, name):
    _, rows, cols = w.shape
    st_cols = st0.shape[2]
    tr = max(t for t in range(16, 129, 16) if rows % t == 0)
    nr = rows // tr

    def body(s0_ref, s1_ref, w_ref, m_ref, v_ref, g_ref, d_ref, nm_ref, nv_ref):
        for layer, s_ref in enumerate((s0_ref, s1_ref)):
            @pl.when(pl.program_id(0) == layer)
            def _(s_ref=s_ref):
                total = s_ref[0, :, 0:cols].astype(F32)
                for dev in range(1, N_DEV):
                    total = total + s_ref[dev, :, 0:cols].astype(F32)
                g_ref[0] = total

        d_ref[0], nm_ref[0], nv_ref[0] = _adam(g_ref[0], w_ref[0], m_ref[0], v_ref[0])

    st_spec = lambda layer: pl.BlockSpec(
        (N_DEV, tr, st_cols), lambda l, i: (0, jnp.where(l == layer, i, (nr - 1) * (1 - layer)), 0))
    par = pl.BlockSpec((1, tr, cols), lambda l, i: (l, i, 0))
    return _pcall(
        body, grid=(DEPTH, nr), in_specs=[st_spec(0), st_spec(1), par, par, par], out_specs=[par] * 4,
        out_shape=[SDS(w.shape, F32)] * 4,
        compiler_params=_cp(("arbitrary", "arbitrary")), name=name)(st0, st1, w, m, v)


def _adamw_small(g, w, m, v, *, name):
    def body(g_ref, w_ref, m_ref, v_ref, d_ref, nm_ref, nv_ref):
        d_ref[...], nm_ref[...], nv_ref[...] = _adam(g_ref[...], w_ref[...], m_ref[...], v_ref[...])

    return _pcall(body, out_shape=[SDS(g.shape, F32)] * 3, compiler_params=_cp(), name=name)(g, w, m, v)


def _pack_rows(arrays):
    flat = jnp.concatenate([a.reshape(-1).astype(F32) for a in arrays])
    rows = -(-flat.shape[0] // (8 * D)) * 8
    return jnp.pad(flat, (0, rows * D - flat.shape[0])).reshape(rows, D)


def _unpack_rows(pack, like):
    flat = pack.reshape(-1)
    out, at = [], 0
    for a in like:
        out.append(flat[at:at + a.size].reshape(a.shape))
        at += a.size
    return out


def _layer_fwd(x, h, mem, win, mixer_weights, ffn_weights, after_up, small, g_next, tag):
    proj = _mm(h, win, tm=1024, tn=1536, name=f"proj_{tag}")
    wkv, w4, cw_a, cw_b, cw_f = mixer_weights(proj)
    za = _bra_fwd(proj, cw_a, name=f"bra_fwd_{tag}")
    cb = _brb_conv_fwd(proj, cw_b, small["conv_b_bias"], name=f"brb_conv_fwd_{tag}")
    sb = _ln_silu_fwd(cb, small["ln_b_g"], small["ln_b_b"], name=f"ln_silu_fwd_{tag}")
    memn, kv = _kv_prep(mem, small["norm_mem_g"], wkv, name=f"kv_prep_{tag}")
    o = _attn_fwd(proj, kv, name=f"attn_fwd_{tag}")
    ya, yb, yc, mg, x1, h2 = _mix_out(x, za, sb, o, proj, w4, small["b_gate"], small["norm_ffn_g"],
                                      name=f"mix_out_{tag}")
    wup, wdn = ffn_weights(h2)
    u2 = _mm(h2, wup, tm=1024, tn=1536, name=f"up_{tag}")
    token = after_up(u2)
    act, c2 = _ffn_act(u2, cw_f if token is None else _behind(cw_f, token), name=f"ffn_act_{tag}")
    x2, h_next = _mm_res_norm(act, wdn, x1, g_next, name=f"down_{tag}")
    saved = dict(x=x, h=h, proj=proj, za=za, cb=cb, sb=sb, memn=memn, kv=kv, o=o, ya=ya, yb=yb, yc=yc,
                 mg=mg, x1=x1, h2=h2, u2=u2, c2=c2, act=act)
    return x2, h_next, (win, wup, wkv, w4, wdn, cw_a, cw_b, cw_f), saved


def _behind(operand, token):
    return operand + token[0:1, 0:1]


def _layer_bwd(dx2, dx2b, mem, wts, small, sv, start, tag):
    win, wup, wkv, w4, wdn, cw_a, cw_b, cw_f = wts
    dact = _mm(dx2b, wdn, tb=True, tm=1024, tn=768, name=f"d_act_{tag}")
    dwdn = _mm(sv["act"], dx2b, ta=True, tm=768, tn=1024, name=f"dw_down_{tag}")
    du2, dcw_f = _ffn_bwd(sv["u2"], sv["c2"], dact, cw_f, name=f"ffn_bwd_{tag}")
    dwup_t = _mm(du2, sv["h2"], ta=True, tm=C_UP_P, tn=1024, name=f"dw_up_{tag}")
    token = start(("wdn", "wupT"), (dwdn, dwup_t), f"ffn_{tag}")
    dx1, dx1b, dg_ffn = _mm_nt_normbwd(du2, wup, sv["x1"], dx2, _behind(small["norm_ffn_g"], token),
                                       name=f"d_h2_{tag}")

    dya, dyb, dyc, dza, dsb, do, dproj, dbg = _mix_bwd(dx1b, sv["ya"], sv["yb"], sv["yc"], sv["proj"], w4,
                                                      small["b_gate"], name=f"mix_bwd_{tag}")
    dw4 = jnp.stack([
        _mm(a, b, ta=True, tm=1024, tn=512, name=f"dw_{nm}_{tag}")
        for nm, a, b in (("a_out", sv["za"], dya), ("b_out", sv["sb"], dyb), ("att_out", sv["o"], dyc),
                         ("o", sv["mg"], dx1b))])
    dq, dk, dv = _attn_bwd(sv["proj"], sv["kv"], do, name=f"attn_bwd_{tag}")
    dwkv, dg_mem = _kv_bwd(mem, small["norm_mem_g"], sv["memn"], dk, dv, wkv, name=f"kv_bwd_{tag}")
    token = start(("w4", "wkv"), (dw4, dwkv), f"mix_{tag}")
    dproj, dcw_a = _bra_bwd(sv["proj"], dza, _behind(cw_a, token), dproj, name=f"bra_bwd_{tag}")
    dcb, ln_sums = _ln_silu_bwd(sv["cb"], dsb, small["ln_b_g"], small["ln_b_b"], name=f"ln_silu_bwd_{tag}")
    dproj, dcw_b = _brb_conv_bwd(sv["proj"], dcb, dq, cw_b, dproj, name=f"brb_conv_bwd_{tag}")
    dwin = _mm(sv["h"], dproj, ta=True, tm=1024, tn=768, name=f"dw_in_{tag}")
    token = start(("win",), (dwin,), f"in_{tag}")
    dx, dxb, dg_mix = _mm_nt_normbwd(dproj, win, sv["x"], dx1, _behind(small["norm_mix_g"], token),
                                     tk=4608, name=f"d_h_{tag}")

    small_grads = [dg_mix[0:1], dg_mem[0:1], dbg[0:1].reshape(3, D), ln_sums[2:3], ln_sums[0:1], ln_sums[1:2],
                   dg_ffn[0:1], dcw_a[0:K_A], dcw_b[0:K_B], dcw_f[0:K_F].reshape(K_F * 2 * FF_P // D, D)]
    return dx, dxb, small_grads, token


_SMALL_ROWS = (1, 1, 3, 1, 1, 1, 1, K_A, K_B, K_F * 2 * FF_P // D)
_CV_ROWS = 48


def kernel(x, mem, norm_mix_g, norm_mem_g, w_in, b_gate, conv_a_w, w_a_out, conv_b_w, conv_b_bias, ln_b_g, ln_b_b, w_b_out, w_kv, w_att_out, w_o, norm_ffn_g, w_up, conv_ffn_w, w_down, norm_final_g, loss_target, m_norm_mix_g, m_norm_mem_g, m_w_in, m_b_gate, m_conv_a_w, m_w_a_out, m_conv_b_w, m_conv_b_bias, m_ln_b_g, m_ln_b_b, m_w_b_out, m_w_kv, m_w_att_out, m_w_o, m_norm_ffn_g, m_w_up, m_conv_ffn_w, m_w_down, m_norm_final_g, v_norm_mix_g, v_norm_mem_g, v_w_in, v_b_gate, v_conv_a_w, v_w_a_out, v_conv_b_w, v_conv_b_bias, v_ln_b_g, v_ln_b_b, v_w_b_out, v_w_kv, v_w_att_out, v_w_o, v_norm_ffn_g, v_w_up, v_conv_ffn_w, v_w_down, v_norm_final_g):
    me = _my_index()
    me_arr = me.astype(jnp.int32).reshape(1)
    x0, mem0, tgt = x.reshape(x.shape[1:]), mem.reshape(mem.shape[1:]), loss_target.reshape(x.shape[1:])
    up_pad = ((0, 0), (0, 0), (0, C_UP_P - C_UP))

    ag_groups = (("win",), ("wkv", "w4", "cv"), ("wup", "wdn"))
    kinds = ag_groups[0] + ag_groups[1] + ag_groups[2]
    smalls, ag_handles = [], []
    token = jnp.zeros((8, LANE), F32)
    for l in range(DEPTH):
        cv = jnp.zeros((_CV_ROWS, C_UP_P), F32)
        cv = cv.at[0:K_F, 0:C_UP].set(conv_ffn_w[l]).at[3:3 + K_A, 0:R_O].set(conv_a_w[l])
        cv = cv.at[8:8 + K_B, 0:R_O].set(conv_b_w[l])
        shards = dict(
            win=w_in[l], wup=jnp.pad(w_up[l], up_pad[1:]), wkv=w_kv[l],
            w4=jnp.stack([w_a_out[l], w_b_out[l], w_att_out[l], w_o[l]]), wdn=w_down[l], cv=cv)
        whole = dict({k: SDS(*_WHOLE[k]) for k in kinds if k != "cv"}, cv=SDS((N_DEV,) + cv.shape, F32))
        per_layer = []
        for g, grp in enumerate(ag_groups):
            lands = [_place_own(k, shards[k], whole[k], True, me_arr, after=token, name=f"ag_own_{k}_l{l}")
                     for k in grp]
            handle, token = _gather2_start(lands, grp, token, name=f"ag_start_l{l}_g{g}")
            per_layer.append(handle)
        ag_handles.append(per_layer)
        smalls.append(dict(
            norm_mix_g=norm_mix_g[l][None], norm_mem_g=norm_mem_g[l][None], b_gate=b_gate[l][None],
            conv_b_bias=conv_b_bias[l][None], ln_b_g=ln_b_g[l][None], ln_b_b=ln_b_b[l][None],
            norm_ffn_g=norm_ffn_g[l][None]))

    def forward_group(l, g, after):
        ag_handles[l][g], tok = _gather2_forward(ag_handles[l][g], ag_groups[g], after, name=f"ag_forward_l{l}_g{g}")
        return tok

    def group_of(l, g):
        def wait(after):
            if l == 0:
                after = forward_group(0, g, after)
            return _gather2_wait(ag_handles[l][g], ag_groups[g], after, name=f"ag_wait_l{l}_g{g}")
        return wait

    def mixer_weights(l):
        def wait(after):
            if l > 0:
                after = forward_group(l, 2, after)
            wkv, w4, cvg = group_of(l, 1)(after)
            cw_f = jnp.stack([cvg[d, 0:K_F, :] for d in UP_ORDER], axis=1).reshape(K_F, 2 * FF_P)
            cw_a = cvg[:, 3:3 + K_A, 0:R_O].transpose(1, 0, 2).reshape(K_A, D)
            cw_b = cvg[:, 8:8 + K_B, 0:R_O].transpose(1, 0, 2).reshape(K_B, D)
            return (wkv, w4, jnp.pad(cw_a, ((0, 8 - K_A), (0, 0))), jnp.pad(cw_b, ((0, 32 - K_B), (0, 0))),
                    jnp.pad(cw_f, ((0, 8 - K_F), (0, 0))))
        return wait

    wts, saved = [], []
    xs = x0
    h = _rms_fwd(xs, smalls[0]["norm_mix_g"], name="rms_fwd")
    behind = forward_group(0, 0, token)

    def next_layer_forwarding(l):
        def hook(after):
            return None if l + 1 == DEPTH else forward_group(l + 1, 0, after)
        return hook

    for l in range(DEPTH):
        g_next = smalls[l + 1]["norm_mix_g"] if l + 1 < DEPTH else norm_final_g[None]
        if l > 0:
            behind = forward_group(l, 1, behind)
        (win,) = _gather2_wait(ag_handles[l][0], ag_groups[0], behind, name=f"ag_wait_l{l}_g0")
        xs, h, w_l, sv = _layer_fwd(xs, h, mem0, win, mixer_weights(l), group_of(l, 2), next_layer_forwarding(l),
                                    smalls[l], g_next, f"l{l}")
        behind = h
        wts.append(w_l)
        saved.append(sv)
    dx, dxb, head_sums = _loss_head(xs, tgt, norm_final_g[None], name="loss_head")

    rs_handles = []
    small_grads = [None] * DEPTH

    def start_scatter(grp, arrays, name):
        maps = _scatter_maps(grp)
        lands = [_place_own(k, a, SDS((N_DEV,) + _SHARD[k], BF), False, me_arr, name=f"rs_own_{k}_{name}")
                 for k, a in zip(grp, arrays)]
        handle, tok = _exchange_start(list(arrays), lands, maps, rs_handles[-1][2] if rs_handles else head_sums,
                                      name=f"rs_start_{name}")
        rs_handles.append((grp, handle, tok, name))
        return tok

    for l in reversed(range(DEPTH)):
        dx, dxb, small_grads[l], token = _layer_bwd(dx, dxb, mem0, wts[l], smalls[l], saved[l], start_scatter,
                                                    f"l{l}")

    pack = jnp.concatenate(small_grads[0] + small_grads[1] + [head_sums[1:2], head_sums[0:1]], axis=0)
    pack = jnp.pad(pack, ((0, -pack.shape[0] % 8), (0, 0)))
    small_maps = (lambda srcs, lands, a, idx: srcs[a]), (lambda lands, a, idx: lands[a].at[idx])
    small_land = _place_own("cv", pack, SDS((N_DEV,) + pack.shape, F32), True, me_arr, name="small_own")
    small_handle, small_token = _exchange_start([pack], [small_land], small_maps, dx, name="small_start")

    staged = [dict() for _ in range(DEPTH)]
    for grp, handle, _, name in rs_handles[:-1]:
        staged[int(name[-1])].update(zip(grp, _exchange_wait(handle, _scatter_maps(grp), small_token,
                                                             name=f"rs_wait_{name}")))

    def big_update(kind, w, m, v, name):
        return _adamw_staged(staged[0][kind], staged[1][kind], w, m, v, name=name)

    r_up = [jnp.swapaxes(a, 1, 2) for a in big_update(
        "wupT", jnp.swapaxes(w_up, 1, 2), jnp.swapaxes(m_w_up, 1, 2), jnp.swapaxes(v_w_up, 1, 2), "adamw_w_up")]
    r_kv = big_update("wkv", w_kv, m_w_kv, v_w_kv, "adamw_w_kv")
    r_dn = big_update("wdn", w_down, m_w_down, v_w_down, "adamw_w_down")

    def four(a, b, c, d_):
        return jnp.stack([a, b, c, d_], axis=1).reshape(DEPTH, 4 * R_O, D)

    r_4 = _adamw_staged(
        staged[0]["w4"].reshape(N_DEV, 4 * R_O, D), staged[1]["w4"].reshape(N_DEV, 4 * R_O, D),
        four(w_a_out, w_b_out, w_att_out, w_o), four(m_w_a_out, m_w_b_out, m_w_att_out, m_w_o),
        four(v_w_a_out, v_w_b_out, v_w_att_out, v_w_o), name="adamw_w_out")
    grp, handle, _, name = rs_handles[-1]
    staged[0].update(zip(grp, _exchange_wait(handle, _scatter_maps(grp), r_4[0], name=f"rs_wait_{name}")))
    r_in = big_update("win", w_in, m_w_in, v_w_in, "adamw_w_in")
    r_a, r_b, r_att, r_o = ([a.reshape(DEPTH, 4, R_O, D)[:, j] for a in r_4] for j in range(4))

    (gathered,) = _exchange_wait(small_handle, small_maps, r_in[0], name="small_wait")
    total = _sum_slots(gathered, name="small_sum")
    per_layer = sum(_SMALL_ROWS)
    parts = []
    for l in range(DEPTH):
        at, one = l * per_layer, []
        for rows in _SMALL_ROWS:
            one.append(total[at:at + rows])
            at += rows
        parts.append(one)
    g_final = total[DEPTH * per_layer]
    loss = 0.5 / D * jnp.sum(total[DEPTH * per_layer + 1])

    def both(i):
        return jnp.stack([parts[0][i], parts[1][i]])

    g_norm_mix, g_norm_mem = both(0)[:, 0], both(1)[:, 0]
    g_b_gate = both(2).reshape(DEPTH, 3 * D)
    g_cbias, g_lng, g_lnb, g_norm_ffn = both(3)[:, 0], both(4)[:, 0], both(5)[:, 0], both(6)[:, 0]
    g_conv_a = lax.dynamic_slice_in_dim(both(7), me * R_O, R_O, axis=2)
    g_conv_b = lax.dynamic_slice_in_dim(both(8), me * R_O, R_O, axis=2)
    g_conv_f = lax.dynamic_slice_in_dim(both(9).reshape(DEPTH, K_F, 2 * FF_P), _up_slot(me) * C_UP_P, C_UP, axis=2)

    small_g = [g_norm_mix, g_norm_mem, g_b_gate, g_conv_a, g_conv_b, g_cbias, g_lng, g_lnb, g_norm_ffn, g_conv_f,
               g_final]
    small_w = [norm_mix_g, norm_mem_g, b_gate, conv_a_w, conv_b_w, conv_b_bias, ln_b_g, ln_b_b, norm_ffn_g,
               conv_ffn_w, norm_final_g]
    small_m = [m_norm_mix_g, m_norm_mem_g, m_b_gate, m_conv_a_w, m_conv_b_w, m_conv_b_bias, m_ln_b_g, m_ln_b_b,
               m_norm_ffn_g, m_conv_ffn_w, m_norm_final_g]
    small_v = [v_norm_mix_g, v_norm_mem_g, v_b_gate, v_conv_a_w, v_conv_b_w, v_conv_b_bias, v_ln_b_g, v_ln_b_b,
               v_norm_ffn_g, v_conv_ffn_w, v_norm_final_g]
    upd = _adamw_small(_pack_rows(small_g), _pack_rows(small_w), _pack_rows(small_m), _pack_rows(small_v),
                       name="adamw_small")
    s_d, s_m, s_v = (_unpack_rows(p, small_w) for p in upd)
    (d_norm_mix, d_norm_mem, d_b_gate, d_conv_a, d_conv_b, d_cbias, d_lng, d_lnb, d_norm_ffn, d_conv_f,
     d_final) = s_d
    (nm_norm_mix, nm_norm_mem, nm_b_gate, nm_conv_a, nm_conv_b, nm_cbias, nm_lng, nm_lnb, nm_norm_ffn, nm_conv_f,
     nm_final) = s_m
    (nv_norm_mix, nv_norm_mem, nv_b_gate, nv_conv_a, nv_conv_b, nv_cbias, nv_lng, nv_lnb, nv_norm_ffn, nv_conv_f,
     nv_final) = s_v

    grads = [g_norm_mix, g_norm_mem, r_in[0], g_b_gate, g_conv_a, r_a[0], g_conv_b, g_cbias, g_lng, g_lnb, r_b[0],
             r_kv[0], r_att[0], r_o[0], g_norm_ffn, r_up[0], g_conv_f, r_dn[0], g_final]
    deltas = [d_norm_mix, d_norm_mem, r_in[1], d_b_gate, d_conv_a, r_a[1], d_conv_b, d_cbias, d_lng, d_lnb, r_b[1],
              r_kv[1], r_att[1], r_o[1], d_norm_ffn, r_up[1], d_conv_f, r_dn[1], d_final]
    new_m = [nm_norm_mix, nm_norm_mem, r_in[2], nm_b_gate, nm_conv_a, r_a[2], nm_conv_b, nm_cbias, nm_lng, nm_lnb,
             r_b[2], r_kv[2], r_att[2], r_o[2], nm_norm_ffn, r_up[2], nm_conv_f, r_dn[2], nm_final]
    new_v = [nv_norm_mix, nv_norm_mem, r_in[3], nv_b_gate, nv_conv_a, r_a[3], nv_conv_b, nv_cbias, nv_lng, nv_lnb,
             r_b[3], r_kv[3], r_att[3], r_o[3], nv_norm_ffn, r_up[3], nv_conv_f, r_dn[3], nv_final]
    return (loss, dx[None], *grads, *deltas, *new_m, *new_v)
```

```python
import functools

import jax
import jax.numpy as jnp
import numpy as np
from jax import lax
from jax.experimental import pallas as pl
from jax.experimental.pallas import tpu as pltpu

F32 = jnp.float32
BF = jnp.bfloat16
SDS = jax.ShapeDtypeStruct
MESH = pl.DeviceIdType.MESH
ANY = pl.BlockSpec(memory_space=pl.ANY)

N_DEV = 8
DEPTH = 2
D = 1024
N_HEADS = 4
HEAD = D // N_HEADS
D_FF = 2816
K_A, K_B, K_F = 3, 31, 3
NORM_EPS = 1e-6

C_IN = 9 * D // N_DEV
C_KV = 2 * D // N_DEV
C_UP = 2 * D_FF // N_DEV
LANE = 128
C_UP_P = -(-C_UP // LANE) * LANE
FF_P = 4 * C_UP_P
R_O = D // N_DEV
R_DN = D_FF // N_DEV

VMEM_LIMIT = 56 * 1024 * 1024
TM = 512
TR = 512
TR_FFN = 1024
SUB = 128
H_S, H_L = 16, 32

ADAM_LR, ADAM_B1, ADAM_B2, ADAM_EPS, ADAM_WD, ADAM_STEP = 0.001, 0.9, 0.999, 1e-08, 0.01, 10

UP_ORDER = (0, 4, 1, 5, 2, 6, 3, 7)


def _pcall(body, **kw):
    return pl.pallas_call(body, **kw)


def _cp(sem=None, **kw):
    return pltpu.CompilerParams(dimension_semantics=sem, vmem_limit_bytes=VMEM_LIMIT, **kw)


def _dot(a, b):
    return jnp.dot(a, b, preferred_element_type=F32)


def _dot_nt(a, b):
    return lax.dot_general(a, b, (((1,), (1,)), ((), ())), preferred_element_type=F32)


def _dot_tn(a, b):
    return lax.dot_general(a, b, (((0,), (0,)), ((), ())), preferred_element_type=F32)


def _sigmoid(z):
    return 1.0 / (1.0 + jnp.exp(-z))


def _rms(xv):
    return lax.rsqrt(jnp.mean(xv * xv, axis=-1, keepdims=True) + NORM_EPS)


def _up_slot(idx):
    return jnp.where(idx < 4, 2 * idx, 2 * (idx - 4) + 1)


def _dn_row(idx):
    return C_UP_P * (idx // 2) + R_DN * (idx % 2)


def _mm(a, b, *, ta=False, tb=False, out_dtype=BF, tm=TM, tn=512, tk=None, name):
    m, k_dim = (a.shape[1], a.shape[0]) if ta else a.shape
    n = b.shape[0] if tb else b.shape[1]
    tm, tn = min(tm, m), min(tn, n)
    tk = k_dim if tk is None else min(tk, k_dim)
    nk = k_dim // tk
    assert m % tm == 0 and n % tn == 0 and k_dim % tk == 0
    dims = (((0 if ta else 1,), (1 if tb else 0,)), ((), ()))

    def body(a_ref, b_ref, o_ref, *scratch):
        part = lax.dot_general(a_ref[...], b_ref[...], dims, preferred_element_type=F32)
        if nk == 1:
            o_ref[...] = part.astype(o_ref.dtype)
            return
        acc = scratch[0]
        k = pl.program_id(2)

        @pl.when(k == 0)
        def _():
            acc[...] = part

        @pl.when(k > 0)
        def _():
            acc[...] += part

        @pl.when(k == nk - 1)
        def _():
            o_ref[...] = acc[...].astype(o_ref.dtype)

    a_spec = pl.BlockSpec((tk, tm), lambda i, j, k: (k, i)) if ta else pl.BlockSpec((tm, tk), lambda i, j, k: (i, k))
    b_spec = pl.BlockSpec((tn, tk), lambda i, j, k: (j, k)) if tb else pl.BlockSpec((tk, tn), lambda i, j, k: (k, j))
    return _pcall(
        body, grid=(m // tm, n // tn, nk), in_specs=[a_spec, b_spec],
        out_specs=pl.BlockSpec((tm, tn), lambda i, j, k: (i, j)),
        out_shape=SDS((m, n), out_dtype),
        scratch_shapes=[pltpu.VMEM((tm, tn), F32)] if nk > 1 else [],
        compiler_params=_cp(("parallel", "parallel", "arbitrary")), name=name)(a, b)


def _mm_res_norm(a, w, x, g, *, name):
    s, k_dim = a.shape
    tm = min(TM, s)

    def body(a_ref, w_ref, x_ref, g_ref, xo_ref, h_ref):
        xo = x_ref[...] + _dot(a_ref[...], w_ref[...])
        xo_ref[...] = xo
        h_ref[...] = ((xo * _rms(xo)) * g_ref[...]).astype(BF)

    return _pcall(
        body, grid=(s // tm,),
        in_specs=[pl.BlockSpec((tm, k_dim), lambda i: (i, 0)),
                  pl.BlockSpec((k_dim, D), lambda i: (0, 0), pipeline_mode=pl.Buffered(1)),
                  pl.BlockSpec((tm, D), lambda i: (i, 0)), pl.BlockSpec((1, D), lambda i: (0, 0))],
        out_specs=[pl.BlockSpec((tm, D), lambda i: (i, 0))] * 2,
        out_shape=[SDS((s, D), F32), SDS((s, D), BF)],
        compiler_params=_cp(("parallel",)), name=name)(a, w, x, g)


def _mm_nt_normbwd(da, w, x, dres, g, *, tk=None, name):
    s, k_dim = da.shape
    tm = min(TM, s)
    tk = k_dim if tk is None else tk
    nk = k_dim // tk
    assert k_dim % tk == 0

    def body(da_ref, w_ref, x_ref, dres_ref, g_ref, dx_ref, dxb_ref, dg_ref, *scratch):
        i, k = pl.program_id(0), pl.program_id(1)
        part = _dot_nt(da_ref[...], w_ref[...])
        if nk > 1:
            acc = scratch[0]

            @pl.when(k == 0)
            def _():
                acc[...] = part

            @pl.when(k > 0)
            def _():
                acc[...] += part

        @pl.when((i == 0) & (k == 0))
        def _():
            dg_ref[...] = jnp.zeros_like(dg_ref)

        @pl.when(k == nk - 1)
        def _():
            dh = acc[...] if nk > 1 else part
            xv = x_ref[...]
            r = _rms(xv)
            xn = xv * r
            dg_ref[0:1, :] += jnp.sum(dh * xn, axis=0, keepdims=True)
            dxn = dh * g_ref[...]
            dx = dres_ref[...] + r * (dxn - xn * jnp.mean(dxn * xn, axis=-1, keepdims=True))
            dx_ref[...] = dx
            dxb_ref[...] = dx.astype(BF)

    row = lambda i, k: (i, 0)
    w_spec = (pl.BlockSpec((D, tk), lambda i, k: (0, k)) if nk > 1 else
              pl.BlockSpec((D, tk), lambda i, k: (0, 0), pipeline_mode=pl.Buffered(1)))
    return _pcall(
        body, grid=(s // tm, nk),
        in_specs=[pl.BlockSpec((tm, tk), lambda i, k: (i, k)), w_spec,
                  pl.BlockSpec((tm, D), row), pl.BlockSpec((tm, D), row), pl.BlockSpec((1, D), lambda i, k: (0, 0))],
        out_specs=[pl.BlockSpec((tm, D), row), pl.BlockSpec((tm, D), row), pl.BlockSpec((8, D), lambda i, k: (0, 0))],
        out_shape=[SDS((s, D), F32), SDS((s, D), BF), SDS((8, D), F32)],
        scratch_shapes=[pltpu.VMEM((tm, D), F32)] if nk > 1 else [],
        compiler_params=_cp(("arbitrary", "arbitrary")), name=name)(da, w, x, dres, g)


def _rms_fwd(x, g, *, name):
    s = x.shape[0]
    tm = min(TM, s)

    def body(x_ref, g_ref, h_ref):
        xv = x_ref[...]
        h_ref[...] = ((xv * _rms(xv)) * g_ref[...]).astype(BF)

    return _pcall(
        body, grid=(s // tm,),
        in_specs=[pl.BlockSpec((tm, D), lambda i: (i, 0)), pl.BlockSpec((1, D), lambda i: (0, 0))],
        out_specs=pl.BlockSpec((tm, D), lambda i: (i, 0)), out_shape=SDS((s, D), BF),
        compiler_params=_cp(("parallel",)), name=name)(x, g)


def _loss_head(x, tgt, g, *, name):
    s = x.shape[0]
    tm = min(TM, s)

    def body(x_ref, t_ref, g_ref, dx_ref, dxb_ref, sums_ref):
        @pl.when(pl.program_id(0) == 0)
        def _():
            sums_ref[...] = jnp.zeros_like(sums_ref)

        xv = x_ref[...]
        r = _rms(xv)
        xn = xv * r
        diff = xn * g_ref[...] - t_ref[...]
        sums_ref[0:1, :] += jnp.sum(diff * diff, axis=0, keepdims=True)
        dy = diff * (1.0 / D)
        sums_ref[1:2, :] += jnp.sum(dy * xn, axis=0, keepdims=True)
        dxn = dy * g_ref[...]
        dx = r * (dxn - xn * jnp.mean(dxn * xn, axis=-1, keepdims=True))
        dx_ref[...] = dx
        dxb_ref[...] = dx.astype(BF)

    row = lambda i: (i, 0)
    return _pcall(
        body, grid=(s // tm,),
        in_specs=[pl.BlockSpec((tm, D), row), pl.BlockSpec((tm, D), row), pl.BlockSpec((1, D), lambda i: (0, 0))],
        out_specs=[pl.BlockSpec((tm, D), row), pl.BlockSpec((tm, D), row), pl.BlockSpec((8, D), lambda i: (0, 0))],
        out_shape=[SDS((s, D), F32), SDS((s, D), BF), SDS((8, D), F32)],
        compiler_params=_cp(("arbitrary",)), name=name)(x, tgt, g)


def _halo_before(i, tr, h):
    return jnp.maximum(i * (tr // h) - 1, 0)


def _halo_after(i, tr, h, s):
    return jnp.minimum((i + 1) * (tr // h), s // h - 1)


def _taps(buf, w_ref, sl, k_w, base, rows):
    acc = None
    for k in range(k_w):
        t = w_ref[k:k + 1, sl] * buf[base + k:base + k + rows, sl]
        acc = t if acc is None else acc + t
    return acc


def _taps_rev(buf, w_ref, sl, k_w, base, rows):
    acc = None
    for k in range(k_w):
        t = w_ref[k:k + 1, sl] * buf[base + k_w - 1 - k:base + k_w - 1 - k + rows, sl]
        acc = t if acc is None else acc + t
    return acc


def _tap_grads(dw_ref, dc, buf, sl, k_w, base, rows):
    for k in range(k_w):
        dw_ref[k:k + 1, sl] += jnp.sum(dc * buf[base + k:base + k + rows, sl], axis=0, keepdims=True)


def _bra_fwd(proj, cw, *, name):
    s = proj.shape[0]
    tr, h = min(TR, s), H_S
    sub = min(SUB, tr)

    def body(cur, halo, w_ref, za_ref, cvb):
        i = pl.program_id(0)
        hv = halo[:, D:2 * D].astype(F32) * halo[:, 2 * D:3 * D].astype(F32)
        cvb[0:h, :] = jnp.where(i == 0, 0.0, hv)
        cvb[h:h + tr, :] = cur[:, D:2 * D].astype(F32) * cur[:, 2 * D:3 * D].astype(F32)
        for c in range(D // LANE):
            sl = slice(LANE * c, LANE * c + LANE)
            ca = _taps(cvb, w_ref, sl, K_A, h - (K_A - 1), tr)
            za_ref[:, sl] = (cur[:, sl].astype(F32) * ca).astype(BF)

    return _pcall(
        body, grid=(s // tr,),
        in_specs=[pl.BlockSpec((tr, 3 * D), lambda i: (i, 0)),
                  pl.BlockSpec((h, 3 * D), lambda i: (_halo_before(i, tr, h), 0)),
                  pl.BlockSpec((8, D), lambda i: (0, 0))],
        out_specs=pl.BlockSpec((tr, D), lambda i: (i, 0)), out_shape=SDS((s, D), BF),
        scratch_shapes=[pltpu.VMEM((h + tr, D), F32)],
        compiler_params=_cp(("parallel",)), name=name)(proj, proj, cw)


def _bra_bwd(proj, dza, cw, dproj, *, name):
    s = proj.shape[0]
    tr, h = min(TR, s), H_S
    sub = min(SUB, tr)
    n = s // tr

    def body(before, cur, after, dz_cur, dz_after, w_ref, dproj_in, da_ref, dw_ref, cvb, dcab):
        del dproj_in
        i = pl.program_id(0)

        @pl.when(i == 0)
        def _():
            dw_ref[...] = jnp.zeros_like(dw_ref)

        first, last = i == 0, i == n - 1
        cvb[0:h, :] = jnp.where(first, 0.0, before[:, D:2 * D].astype(F32) * before[:, 2 * D:3 * D].astype(F32))
        cvb[h:h + tr, :] = cur[:, D:2 * D].astype(F32) * cur[:, 2 * D:3 * D].astype(F32)
        dcab[0:tr, :] = dz_cur[...].astype(F32) * cur[:, 0:D].astype(F32)
        dcab[tr:tr + h, :] = jnp.where(last, 0.0, dz_after[...].astype(F32) * after[:, 0:D].astype(F32))
        for c in range(D // LANE):
            sl = slice(LANE * c, LANE * c + LANE)
            gl, vl = slice(D + LANE * c, D + LANE * c + LANE), slice(2 * D + LANE * c, 2 * D + LANE * c + LANE)
            for r0 in range(0, tr, sub):
                rows = slice(r0, r0 + sub)
                ca = _taps(cvb, w_ref, sl, K_A, h - (K_A - 1) + r0, sub)
                da_ref[rows, sl] = (dz_cur[rows, sl].astype(F32) * ca).astype(BF)
                dcv = _taps_rev(dcab, w_ref, sl, K_A, r0, sub)
                da_ref[rows, gl] = (dcv * cur[rows, vl].astype(F32)).astype(BF)
                da_ref[rows, vl] = (dcv * cur[rows, gl].astype(F32)).astype(BF)
                _tap_grads(dw_ref, dcab[rows, sl], cvb, sl, K_A, h - (K_A - 1) + r0, sub)

    return _pcall(
        body, grid=(n,),
        in_specs=[pl.BlockSpec((h, 3 * D), lambda i: (_halo_before(i, tr, h), 0)),
                  pl.BlockSpec((tr, 3 * D), lambda i: (i, 0)),
                  pl.BlockSpec((h, 3 * D), lambda i: (_halo_after(i, tr, h, s), 0)),
                  pl.BlockSpec((tr, D), lambda i: (i, 0)),
                  pl.BlockSpec((h, D), lambda i: (_halo_after(i, tr, h, s), 0)),
                  pl.BlockSpec((8, D), lambda i: (0, 0)), ANY],
        out_specs=[pl.BlockSpec((tr, 3 * D), lambda i: (i, 0)), pl.BlockSpec((8, D), lambda i: (0, 0))],
        out_shape=[SDS(dproj.shape, BF), SDS((8, D), F32)], input_output_aliases={6: 0},
        scratch_shapes=[pltpu.VMEM((h + tr, D), F32), pltpu.VMEM((tr + h, D), F32)],
        compiler_params=_cp(("arbitrary",)), name=name)(proj, proj, proj, dza, dza, cw, dproj)


_U_COL, _UG_COL = 3, 4


def _brb_conv_fwd(proj, cw, bias, *, name):
    s = proj.shape[0]
    tr, h = min(TR, s), H_L
    sub = min(SUB, tr)

    def body(u_cur, ug_cur, u_halo, ug_halo, w_ref, b_ref, cb_ref, glb, shifted):
        i = pl.program_id(0)
        glb[0:h, :] = jnp.where(i == 0, 0.0, u_halo[...].astype(F32) * _sigmoid(ug_halo[...].astype(F32)))
        glb[h:h + tr, :] = u_cur[...].astype(F32) * _sigmoid(ug_cur[...].astype(F32))
        for c in range(D // LANE):
            sl = slice(LANE * c, LANE * c + LANE)
            for r in range(1, 8):
                shifted[r] = glb[8 - r:8 - r + tr + 24, sl]
            for r0 in range(0, tr, sub):
                acc = None
                for k in range(K_B):
                    q, r = divmod(K_B - 1 - k, 8)
                    at = r0 - 8 * q
                    win = shifted[r, 24 + at:24 + at + sub, :] if r else glb[h + at:h + at + sub, sl]
                    term = w_ref[k:k + 1, sl] * win
                    acc = term if acc is None else acc + term
                cb_ref[r0:r0 + sub, sl] = (acc + b_ref[:, sl]).astype(BF)

    return _pcall(
        body, grid=(s // tr,),
        in_specs=[pl.BlockSpec((tr, D), lambda i: (i, _U_COL)), pl.BlockSpec((tr, D), lambda i: (i, _UG_COL)),
                  pl.BlockSpec((h, D), lambda i: (_halo_before(i, tr, h), _U_COL)),
                  pl.BlockSpec((h, D), lambda i: (_halo_before(i, tr, h), _UG_COL)),
                  pl.BlockSpec((32, D), lambda i: (0, 0)), pl.BlockSpec((1, D), lambda i: (0, 0))],
        out_specs=pl.BlockSpec((tr, D), lambda i: (i, 0)), out_shape=SDS((s, D), BF),
        scratch_shapes=[pltpu.VMEM((h + tr, D), F32), pltpu.VMEM((8, tr + 24, LANE), F32)],
        compiler_params=_cp(("parallel",)), name=name)(proj, proj, proj, proj, cw, bias)


def _brb_conv_bwd(proj, dcb, dq, cw, dproj, *, name):
    s = proj.shape[0]
    tr, h = min(TR, s), H_L
    sub = min(SUB, tr)
    n = s // tr
    nb = -(-(tr + 24) // sub)
    sel = _row_selector(sub, list(range(8)))

    def body(u_cur, ug_cur, d_cur, d_after, dq_ref, sel_ref, w_ref, dproj_in, db_ref, dw_ref, dcbb, shifted):
        del dproj_in
        i = pl.program_id(0)

        @pl.when(i == 0)
        def _():
            dw_ref[...] = jnp.zeros_like(dw_ref)

        db_ref[:, 2 * D:3 * D] = dq_ref[...]
        after = d_after[...]
        dcbb[0:tr, :] = d_cur[...]
        dcbb[tr:tr + h, :] = jnp.where(i == n - 1, jnp.zeros_like(after), after)
        dcbb[tr + h:(nb + 1) * sub, :] = jnp.zeros(((nb + 1) * sub - h - tr, D), BF)
        for c in range(D // LANE):
            sl = slice(LANE * c, LANE * c + LANE)
            for blk in range(nb):
                res = _dot(sel_ref[...], dcbb[blk * sub:(blk + 2) * sub, sl])
                for r in range(8):
                    shifted[r, blk * sub:(blk + 1) * sub, :] = res[r * sub:(r + 1) * sub]
            for r0 in range(0, tr, sub):
                u = u_cur[r0:r0 + sub, sl].astype(F32)
                sg = _sigmoid(ug_cur[r0:r0 + sub, sl].astype(F32))
                glu = u * sg
                dglu = None
                for k in range(K_B):
                    q, r = divmod(K_B - 1 - k, 8)
                    at = r0 + 8 * q
                    win = shifted[r, at:at + sub, :]
                    term = w_ref[k:k + 1, sl] * win
                    dglu = term if dglu is None else dglu + term
                    dw_ref[k:k + 1, sl] += jnp.sum(win * glu, axis=0, keepdims=True)
                db_ref[r0:r0 + sub, sl] = (dglu * sg).astype(BF)
                db_ref[r0:r0 + sub, D + LANE * c:D + LANE * c + LANE] = (dglu * u * sg * (1.0 - sg)).astype(BF)

    return _pcall(
        body, grid=(n,),
        in_specs=[pl.BlockSpec((tr, D), lambda i: (i, _U_COL)), pl.BlockSpec((tr, D), lambda i: (i, _UG_COL)),
                  pl.BlockSpec((tr, D), lambda i: (i, 0)),
                  pl.BlockSpec((h, D), lambda i: (_halo_after(i, tr, h, s), 0)),
                  pl.BlockSpec((tr, D), lambda i: (i, 0)),
                  pl.BlockSpec(sel.shape, lambda i: (0, 0)),
                  pl.BlockSpec((32, D), lambda i: (0, 0)), ANY],
        out_specs=[pl.BlockSpec((tr, 3 * D), lambda i: (i, 1)), pl.BlockSpec((32, D), lambda i: (0, 0))],
        out_shape=[SDS(dproj.shape, BF), SDS((32, D), F32)], input_output_aliases={7: 0},
        scratch_shapes=[pltpu.VMEM(((nb + 1) * sub, D), BF), pltpu.VMEM((8, nb * sub, LANE), F32)],
        compiler_params=_cp(("arbitrary",)), name=name)(proj, proj, dcb, dcb, dq, sel, cw, dproj)


def _ln_silu_fwd(cb, g, b, *, name):
    s = cb.shape[0]
    tm = min(TM, s)

    def body(cb_ref, g_ref, b_ref, sb_ref):
        z = cb_ref[...].astype(F32)
        zc = z - jnp.mean(z, axis=-1, keepdims=True)
        ln = (zc * lax.rsqrt(jnp.mean(zc * zc, axis=-1, keepdims=True) + NORM_EPS)) * g_ref[...] + b_ref[...]
        sb_ref[...] = (ln * _sigmoid(ln)).astype(BF)

    row = lambda i: (i, 0)
    vec = pl.BlockSpec((1, D), lambda i: (0, 0))
    return _pcall(
        body, grid=(s // tm,), in_specs=[pl.BlockSpec((tm, D), row), vec, vec],
        out_specs=pl.BlockSpec((tm, D), row), out_shape=SDS((s, D), BF),
        compiler_params=_cp(("parallel",)), name=name)(cb, g, b)


def _ln_silu_bwd(cb, dsb, g, b, *, name):
    s = cb.shape[0]
    tm = min(TM, s)

    def body(cb_ref, dsb_ref, g_ref, b_ref, dcb_ref, sums_ref):
        @pl.when(pl.program_id(0) == 0)
        def _():
            sums_ref[...] = jnp.zeros_like(sums_ref)

        z = cb_ref[...].astype(F32)
        zc = z - jnp.mean(z, axis=-1, keepdims=True)
        rstd = lax.rsqrt(jnp.mean(zc * zc, axis=-1, keepdims=True) + NORM_EPS)
        lnh = zc * rstd
        ln = lnh * g_ref[...] + b_ref[...]
        sg = _sigmoid(ln)
        dln = dsb_ref[...].astype(F32) * (sg * (1.0 + ln * (1.0 - sg)))
        sums_ref[0:1, :] += jnp.sum(dln * lnh, axis=0, keepdims=True)
        sums_ref[1:2, :] += jnp.sum(dln, axis=0, keepdims=True)
        dlnh = dln * g_ref[...]
        dz = rstd * (dlnh - jnp.mean(dlnh, axis=-1, keepdims=True)
                     - lnh * jnp.mean(dlnh * lnh, axis=-1, keepdims=True))
        sums_ref[2:3, :] += jnp.sum(dz, axis=0, keepdims=True)
        dcb_ref[...] = dz.astype(BF)

    row = lambda i: (i, 0)
    vec = pl.BlockSpec((1, D), lambda i: (0, 0))
    return _pcall(
        body, grid=(s // tm,), in_specs=[pl.BlockSpec((tm, D), row), pl.BlockSpec((tm, D), row), vec, vec],
        out_specs=[pl.BlockSpec((tm, D), row), pl.BlockSpec((8, D), lambda i: (0, 0))],
        out_shape=[SDS((s, D), BF), SDS((8, D), F32)],
        compiler_params=_cp(("arbitrary",)), name=name)(cb, dsb, g, b)


_Q_COL = 5 * D // HEAD


def _kv_prep(mem, g, wkv, *, name):
    m = mem.shape[0]

    def body(mem_ref, g_ref, w_ref, memn_ref, kv_ref):
        mv = mem_ref[...]
        memn = ((mv * _rms(mv)) * g_ref[...]).astype(BF)
        memn_ref[...] = memn
        for dev in range(N_DEV):
            kv_ref[:, dev * C_KV:(dev + 1) * C_KV] = _dot(memn, w_ref[dev]).astype(BF)

    return _pcall(body, out_shape=[SDS((m, D), BF), SDS((m, 2 * D), BF)],
                  compiler_params=_cp(), name=name)(mem, g, wkv)


def _softmax_rows(q, k):
    sc = _dot_nt(q, k) * (1.0 / (HEAD ** 0.5))
    e = jnp.exp(sc - jnp.max(sc, axis=-1, keepdims=True))
    return e / jnp.sum(e, axis=-1, keepdims=True)


def _attn_fwd(proj, kv, *, name):
    s, m = proj.shape[0], kv.shape[0]
    tm = min(TM, s)

    def body(q_ref, kv_ref, o_ref):
        for hd in range(N_HEADS):
            cols = slice(hd * HEAD, (hd + 1) * HEAD)
            p = _softmax_rows(q_ref[:, cols], kv_ref[:, cols])
            o_ref[:, cols] = _dot(p.astype(BF), kv_ref[:, D + hd * HEAD:D + (hd + 1) * HEAD]).astype(BF)

    return _pcall(
        body, grid=(s // tm,),
        in_specs=[pl.BlockSpec((tm, D), lambda i: (i, _Q_COL // N_HEADS)),
                  pl.BlockSpec((m, 2 * D), lambda i: (0, 0))],
        out_specs=pl.BlockSpec((tm, D), lambda i: (i, 0)), out_shape=SDS((s, D), BF),
        compiler_params=_cp(("parallel",)), name=name)(proj, kv)


def _attn_bwd(proj, kv, do, *, name):
    s, m = proj.shape[0], kv.shape[0]
    tm = min(TM, s)

    def body(q_ref, kv_ref, do_ref, dq_ref, dk_ref, dv_ref):
        @pl.when(pl.program_id(0) == 0)
        def _():
            dk_ref[...] = jnp.zeros_like(dk_ref)
            dv_ref[...] = jnp.zeros_like(dv_ref)

        for hd in range(N_HEADS):
            cols = slice(hd * HEAD, (hd + 1) * HEAD)
            q, k, dov = q_ref[:, cols], kv_ref[:, cols], do_ref[:, cols]
            p = _softmax_rows(q, k)
            dp = _dot_nt(dov, kv_ref[:, D + hd * HEAD:D + (hd + 1) * HEAD])
            dv_ref[:, cols] += _dot_tn(p.astype(BF), dov)
            ds = (p * (dp - jnp.sum(dp * p, axis=-1, keepdims=True)) * (1.0 / (HEAD ** 0.5))).astype(BF)
            dq_ref[:, cols] = _dot(ds, k).astype(BF)
            dk_ref[:, cols] += _dot_tn(ds, q)

    return _pcall(
        body, grid=(s // tm,),
        in_specs=[pl.BlockSpec((tm, D), lambda i: (i, _Q_COL // N_HEADS)),
                  pl.BlockSpec((m, 2 * D), lambda i: (0, 0)),
                  pl.BlockSpec((tm, D), lambda i: (i, 0))],
        out_specs=[pl.BlockSpec((tm, D), lambda i: (i, 0)),
                   pl.BlockSpec((m, D), lambda i: (0, 0)),
                   pl.BlockSpec((m, D), lambda i: (0, 0))],
        out_shape=[SDS((s, D), BF), SDS((m, D), F32), SDS((m, D), F32)],
        compiler_params=_cp(("arbitrary",)), name=name)(proj, kv, do)


def _kv_bwd(mem, g, memn, dk, dv, wkv, *, name):
    def body(mem_ref, g_ref, memn_ref, dk_ref, dv_ref, w_ref, dw_ref, dg_ref):
        memn = memn_ref[...]
        dmemn = None
        for dev in range(N_DEV):
            d_ref, col = (dk_ref, dev) if dev < N_HEADS else (dv_ref, dev - N_HEADS)
            dslab = d_ref[:, col * C_KV:(col + 1) * C_KV].astype(BF)
            dw_ref[dev] = _dot_tn(memn, dslab).astype(BF)
            part = _dot_nt(dslab, w_ref[dev])
            dmemn = part if dmemn is None else dmemn + part
        mv = mem_ref[...]
        dg_ref[...] = jnp.zeros_like(dg_ref)
        dg_ref[0:1, :] = jnp.sum(dmemn * (mv * _rms(mv)), axis=0, keepdims=True)

    assert C_KV == HEAD
    return _pcall(body, out_shape=[SDS((N_DEV, D, C_KV), BF), SDS((8, D), F32)],
                  compiler_params=_cp(), name=name)(mem, g, memn, dk, dv, wkv)


_TM_MIX = 512


def _mix_out(x, za, sb, o, proj, w4, bg, g_next, *, name):
    s = x.shape[0]
    tm = min(_TM_MIX, s)

    def body(x_ref, za_ref, sb_ref, o_ref, pg_ref, w4_ref, bg_ref, gn_ref,
             ya_ref, yb_ref, yc_ref, mg_ref, x1_ref, h_ref):
        ys = (_dot(za_ref[...], w4_ref[0]), _dot(sb_ref[...], w4_ref[1]), _dot(o_ref[...], w4_ref[2]))
        merged = None
        for j, (y, y_ref) in enumerate(zip(ys, (ya_ref, yb_ref, yc_ref))):
            y_ref[...] = y.astype(BF)
            gate = _sigmoid(pg_ref[:, j * D:(j + 1) * D].astype(F32) + bg_ref[:, j * D:(j + 1) * D])
            merged = gate * y if merged is None else merged + gate * y
        mg = merged.astype(BF)
        mg_ref[...] = mg
        x1 = x_ref[...] + _dot(mg, w4_ref[3])
        x1_ref[...] = x1
        h_ref[...] = ((x1 * _rms(x1)) * gn_ref[...]).astype(BF)

    row = lambda i: (i, 0)
    act = pl.BlockSpec((tm, D), row)
    return _pcall(
        body, grid=(s // tm,),
        in_specs=[act, act, act, act, pl.BlockSpec((tm, 3 * D), lambda i: (i, 2)),
                  pl.BlockSpec((4, D, D), lambda i: (0, 0, 0), pipeline_mode=pl.Buffered(1)), pl.BlockSpec((1, 3 * D), lambda i: (0, 0)),
                  pl.BlockSpec((1, D), lambda i: (0, 0))],
        out_specs=[act] * 6,
        out_shape=[SDS((s, D), BF)] * 4 + [SDS((s, D), F32), SDS((s, D), BF)],
        compiler_params=_cp(("parallel",)), name=name)(x, za, sb, o, proj, w4, bg, g_next)


def _mix_bwd(dxb, ya, yb, yc, proj, w4, bg, *, name):
    s = dxb.shape[0]
    tm = min(_TM_MIX, s)

    def body(dx_ref, ya_ref, yb_ref, yc_ref, pg_ref, w4_ref, bg_ref,
             dya_ref, dyb_ref, dyc_ref, dza_ref, dsb_ref, do_ref, dgt_ref, dbg_ref):
        @pl.when(pl.program_id(0) == 0)
        def _():
            dbg_ref[...] = jnp.zeros_like(dbg_ref)

        dm = _dot_nt(dx_ref[...], w4_ref[3])
        for j, (y_ref, dy_ref, din_ref) in enumerate(zip((ya_ref, yb_ref, yc_ref), (dya_ref, dyb_ref, dyc_ref),
                                                         (dza_ref, dsb_ref, do_ref))):
            cols = slice(j * D, (j + 1) * D)
            gate = _sigmoid(pg_ref[:, cols].astype(F32) + bg_ref[:, cols])
            dy = (dm * gate).astype(BF)
            dy_ref[...] = dy
            din_ref[...] = _dot_nt(dy, w4_ref[j]).astype(BF)
            dpre = dm * y_ref[...].astype(F32) * gate * (1.0 - gate)
            dgt_ref[:, cols] = dpre.astype(BF)
            dbg_ref[0:1, cols] += jnp.sum(dpre, axis=0, keepdims=True)

    row = lambda i: (i, 0)
    act = pl.BlockSpec((tm, D), row)
    return _pcall(
        body, grid=(s // tm,),
        in_specs=[act, act, act, act, pl.BlockSpec((tm, 3 * D), lambda i: (i, 2)),
                  pl.BlockSpec((4, D, D), lambda i: (0, 0, 0), pipeline_mode=pl.Buffered(1)), pl.BlockSpec((1, 3 * D), lambda i: (0, 0))],
        out_specs=[act] * 6 + [pl.BlockSpec((tm, 3 * D), lambda i: (i, 2)),
                               pl.BlockSpec((8, 3 * D), lambda i: (0, 0))],
        out_shape=[SDS((s, D), BF)] * 6 + [SDS((s, 9 * D), BF), SDS((8, 3 * D), F32)],
        compiler_params=_cp(("arbitrary",)), name=name)(dxb, ya, yb, yc, proj, w4, bg)


_PAIR = 2 * C_UP_P


def _row_selector(sub, first_cols):
    rows = np.arange(len(first_cols) * sub)
    col = np.asarray(first_cols)[rows // sub] + rows % sub
    return jnp.asarray(np.arange(2 * sub)[None, :] == col[:, None], BF)


def _ffn_act(u2, cw, *, name):
    s = u2.shape[0]
    tr = min(TR_FFN, s)
    sub = min(SUB, tr)
    sel = _row_selector(sub, [sub - (K_F - 1 - k) for k in range(K_F)])

    def body(cur, prev, sel_ref, w_ref, act_ref, c2_ref, xb, win):
        i = pl.program_id(1)
        before = prev[...]
        xb[0:sub, :] = jnp.where(i == 0, jnp.zeros_like(before), before)
        xb[sub:sub + tr, :] = cur[...]
        for r0 in range(0, tr, sub):
            win[...] = _dot(sel_ref[...], xb[r0:r0 + 2 * sub, :])
            for c in range(C_UP_P // LANE):
                gl = slice(LANE * c, LANE * c + LANE)
                ul = slice(C_UP_P + LANE * c, C_UP_P + LANE * c + LANE)
                gt = sum(w_ref[k:k + 1, gl] * win[k * sub:(k + 1) * sub, gl] for k in range(K_F))
                up = sum(w_ref[k:k + 1, ul] * win[k * sub:(k + 1) * sub, ul] for k in range(K_F))
                c2_ref[r0:r0 + sub, gl] = gt.astype(BF)
                c2_ref[r0:r0 + sub, ul] = up.astype(BF)
                act_ref[r0:r0 + sub, gl] = (gt * _sigmoid(gt) * up).astype(BF)

    return _pcall(
        body, grid=(4, s // tr),
        in_specs=[pl.BlockSpec((tr, _PAIR), lambda p, i: (i, p)),
                  pl.BlockSpec((sub, _PAIR), lambda p, i: (_halo_before(i, tr, sub), p)),
                  pl.BlockSpec(sel.shape, lambda p, i: (0, 0)),
                  pl.BlockSpec((8, _PAIR), lambda p, i: (0, p))],
        out_specs=[pl.BlockSpec((tr, C_UP_P), lambda p, i: (i, p)), pl.BlockSpec((tr, _PAIR), lambda p, i: (i, p))],
        out_shape=[SDS((s, FF_P), BF), SDS((s, 2 * FF_P), BF)],
        scratch_shapes=[pltpu.VMEM((sub + tr, _PAIR), BF), pltpu.VMEM((K_F * sub, _PAIR), F32)],
        compiler_params=_cp(("parallel", "parallel")), name=name)(u2, u2, sel, cw)


def _ffn_bwd(u2, c2, dact, cw, *, name):
    s = u2.shape[0]
    tr, h = min(TR_FFN, s), H_S
    sub = min(SUB, tr)
    n = s // tr
    sel = _row_selector(sub, [K_F - 1 - k for k in range(K_F)])

    def body(u_cur, c_cur, c_after, da_cur, da_after, sel_ref, w_ref, du_ref, dw_ref, dcb, win):
        i = pl.program_id(1)
        last = i == n - 1

        @pl.when(i == 0)
        def _():
            dw_ref[...] = jnp.zeros_like(dw_ref)

        def conv_grad(gt, up, da):
            gt, up, da = gt.astype(F32), up.astype(F32), da.astype(F32)
            sg = _sigmoid(gt)
            return (da * up * (sg * (1.0 + gt * (1.0 - sg)))).astype(BF), (da * (gt * sg)).astype(BF)

        for c in range(C_UP_P // LANE):
            gl = slice(LANE * c, LANE * c + LANE)
            ul = slice(C_UP_P + LANE * c, C_UP_P + LANE * c + LANE)
            for r0 in range(0, tr, sub):
                rows = slice(r0, r0 + sub)
                dcb[rows, gl], dcb[rows, ul] = conv_grad(c_cur[rows, gl], c_cur[rows, ul], da_cur[rows, gl])
            dg, du_ = conv_grad(c_after[:, gl], c_after[:, ul], da_after[:, gl])
            dcb[tr:tr + h, gl] = jnp.where(last, jnp.zeros_like(dg), dg)
            dcb[tr:tr + h, ul] = jnp.where(last, jnp.zeros_like(du_), du_)
        dcb[tr + h:tr + sub, :] = jnp.zeros((sub - h, _PAIR), BF)
        for r0 in range(0, tr, sub):
            win[...] = _dot(sel_ref[...], dcb[r0:r0 + 2 * sub, :])
            for c in range(_PAIR // LANE):
                sl = slice(LANE * c, LANE * c + LANE)
                u = u_cur[r0:r0 + sub, sl].astype(F32)
                du = None
                for k in range(K_F):
                    wk = win[k * sub:(k + 1) * sub, sl]
                    term = w_ref[k:k + 1, sl] * wk
                    du = term if du is None else du + term
                    dw_ref[k:k + 1, sl] += jnp.sum(wk * u, axis=0, keepdims=True)
                du_ref[r0:r0 + sub, sl] = du.astype(BF)

    return _pcall(
        body, grid=(4, n),
        in_specs=[pl.BlockSpec((tr, _PAIR), lambda p, i: (i, p)),
                  pl.BlockSpec((tr, _PAIR), lambda p, i: (i, p)),
                  pl.BlockSpec((h, _PAIR), lambda p, i: (_halo_after(i, tr, h, s), p)),
                  pl.BlockSpec((tr, C_UP_P), lambda p, i: (i, p)),
                  pl.BlockSpec((h, C_UP_P), lambda p, i: (_halo_after(i, tr, h, s), p)),
                  pl.BlockSpec(sel.shape, lambda p, i: (0, 0)),
                  pl.BlockSpec((8, _PAIR), lambda p, i: (0, p))],
        out_specs=[pl.BlockSpec((tr, _PAIR), lambda p, i: (i, p)), pl.BlockSpec((8, _PAIR), lambda p, i: (0, p))],
        out_shape=[SDS((s, 2 * FF_P), BF), SDS((8, 2 * FF_P), F32)],
        scratch_shapes=[pltpu.VMEM((tr + sub, _PAIR), BF), pltpu.VMEM((K_F * sub, _PAIR), F32)],
        compiler_params=_cp(("parallel", "arbitrary")), name=name)(u2, c2, c2, dact, dact, sel, cw)


def _relations():
    x, y, c = lax.axis_index("x"), lax.axis_index("y"), lax.axis_index("c")
    out = []
    for r in range(1, N_DEV):
        rx, ry, rc = (r >> 2) & 1, (r >> 1) & 1, r & 1
        out.append((r, (x ^ rx, y ^ ry, c ^ rc)))
    return out


def _my_index():
    return 4 * lax.axis_index("x") + 2 * lax.axis_index("y") + lax.axis_index("c")


def _slab(kind, ref, idx):
    if kind == "win":
        return ref.at[:, pl.ds(pl.multiple_of(idx * C_IN, LANE), C_IN)]
    if kind == "wup":
        return ref.at[:, pl.ds(pl.multiple_of(_up_slot(idx) * C_UP_P, LANE), C_UP_P)]
    if kind == "wupT":
        return ref.at[pl.ds(pl.multiple_of(_up_slot(idx) * C_UP_P, LANE), C_UP_P), :]
    if kind == "wkv":
        return ref.at[idx]
    if kind == "w4":
        return ref.at[:, pl.ds(pl.multiple_of(idx * R_O, 16), R_O), :]
    if kind == "wdn":
        return ref.at[pl.ds(pl.multiple_of(_dn_row(idx), 16), R_DN), :]
    assert kind == "cv"
    return ref.at[idx]


_WHOLE = {"win": ((D, 9 * D), BF), "wup": ((D, 2 * FF_P), BF), "wkv": ((N_DEV, D, C_KV), BF),
          "w4": ((4, D, D), BF), "wdn": ((FF_P, D), BF)}
_SHARD = {"win": (D, C_IN), "wup": (D, C_UP_P), "wupT": (C_UP_P, D), "wkv": (D, C_KV), "w4": (4, R_O, D),
          "wdn": (R_DN, D)}
HBM_SPEC = pl.BlockSpec(memory_space=pltpu.HBM)
SEM_SPEC = pl.BlockSpec(memory_space=pltpu.SEMAPHORE)
_DATAFLOW = pltpu.SideEffectType.DATAFLOW_SIDE_EFFECTING


def _scatter_maps(kinds):
    return ((lambda srcs, lands, a, idx: _slab(kinds[a], srcs[a], idx)),
            (lambda lands, a, idx: lands[a].at[idx]))


_SLOTTED = ("wkv", "cv")


def _own_slab_blocks(kind, shard_shape):
    if kind in ("win", "wup"):
        rows, slot = 256, (_up_slot if kind == "wup" else (lambda m: m))
        return (shard_shape[0] // rows, (rows, shard_shape[1]), (lambda i, me: (i, slot(me[0]))),
                (lambda i, me: (i, 0)), (lambda i, me: (me[0], i, 0)))
    if kind == "wupT":
        rows = 256
        steps = shard_shape[0] // rows
        return (steps, (rows, D), (lambda i, me: (_up_slot(me[0]) * steps + i, 0)), (lambda i, me: (i, 0)),
                (lambda i, me: (me[0], i, 0)))
    if kind == "w4":
        return (1, shard_shape, (lambda i, me: (0, me[0], 0)), (lambda i, me: (0, 0, 0)),
                (lambda i, me: (me[0], 0, 0, 0)))
    if kind == "wdn":
        rows = 32
        return (R_DN // rows, (rows, D), (lambda i, me: (_dn_row(me[0]) // rows + i, 0)), (lambda i, me: (i, 0)),
                (lambda i, me: (me[0], i, 0)))
    assert kind in _SLOTTED
    rows = min(256, shard_shape[0])
    return (shard_shape[0] // rows, (rows, shard_shape[1]), (lambda i, me: (me[0], i, 0)),
            (lambda i, me: (i, 0)), (lambda i, me: (me[0], i, 0)))


def _place_own(kind, src, out_sds, gather, me_arr, *, after=None, name):
    shard_shape = src.shape if gather else out_sds.shape[1:]
    steps, blk, whole_idx, shard_idx, staging_idx = _own_slab_blocks(kind, shard_shape)
    slotted = kind in _SLOTTED
    whole_spec = pl.BlockSpec(((None,) if slotted else ()) + tuple(blk), whole_idx)
    if gather:
        in_spec, out_spec = pl.BlockSpec(tuple(blk), shard_idx), whole_spec
    else:
        in_spec, out_spec = whole_spec, pl.BlockSpec((None,) + tuple(blk), staging_idx)
    zero_init = gather and kind == "wdn"

    def body(me_ref, src_ref, *rest):
        rest[-1][...] = src_ref[...].astype(rest[-1].dtype)

    operands = (me_arr, src) + ((jnp.zeros(out_sds.shape, out_sds.dtype),) if zero_init else ())
    operands += () if after is None else (after,)
    return _pcall(
        body,
        grid_spec=pltpu.PrefetchScalarGridSpec(
            num_scalar_prefetch=1, grid=(steps,), in_specs=[in_spec] + [ANY] * (len(operands) - 2),
            out_specs=out_spec),
        out_shape=out_sds, input_output_aliases={2: 0} if zero_init else {},
        compiler_params=_cp(("arbitrary",)), name=name)(*operands)


def _peer_copies(n, src_of, dst_of, src_r, land_r, ssem, rsem):
    me = _my_index()
    out = []
    for r, peer in _relations():
        p_idx = 4 * peer[0] + 2 * peer[1] + peer[2]
        for a in range(n):
            def copy(src_idx, dst_idx, a=a, r=r, peer=peer):
                sem = a * (N_DEV - 1) + r - 1
                return pltpu.make_async_remote_copy(
                    src_ref=src_of(src_r, land_r, a, src_idx), dst_ref=dst_of(land_r, a, dst_idx),
                    send_sem=ssem.at[sem], recv_sem=rsem.at[sem], device_id=peer, device_id_type=MESH)
            out.append((functools.partial(copy, p_idx, me), functools.partial(copy, me, p_idx)))
    return out


def _exchange_start(srcs, lands, maps, after, *, name):
    n, ns = len(lands), len(srcs)
    src_of, dst_of = maps

    def body(*refs):
        src_r, land_r = refs[:ns], refs[ns:ns + n]
        ssem, rsem, token = refs[ns + n + 1], refs[ns + n + 2], refs[-1]
        for send, _ in _peer_copies(n, src_of, dst_of, src_r, land_r, ssem, rsem):
            send().start()
        token[...] = jnp.zeros_like(token)

    flight = list(srcs) + list(lands)
    outs = pl.pallas_call(
        body, name=name,
        out_shape=(pltpu.SemaphoreType.DMA((n * (N_DEV - 1),)), pltpu.SemaphoreType.DMA((n * (N_DEV - 1),)),
                   *[pltpu.HBM(a.shape, a.dtype) for a in flight], SDS((8, LANE), F32)),
        in_specs=[HBM_SPEC] * (ns + n) + [ANY],
        out_specs=(SEM_SPEC, SEM_SPEC, *[HBM_SPEC] * (ns + n), pl.BlockSpec(memory_space=pltpu.VMEM)),
        input_output_aliases={i: 2 + i for i in range(ns + n)},
        compiler_params=pltpu.CompilerParams(has_side_effects=_DATAFLOW),
    )(*[pltpu.with_memory_space_constraint(a, pltpu.HBM) for a in flight], after)
    return (outs[0], outs[1], list(outs[2:2 + ns + n]), ns), outs[-1]


def _exchange_wait(handle, maps, after, *, name):
    ssem, rsem, flight, ns = handle
    n = len(flight) - ns
    src_of, dst_of = maps

    def body(*refs):
        src_r, land_r, ssem_r, rsem_r = refs[:ns], refs[ns:ns + n], refs[ns + n], refs[ns + n + 1]
        for send, arrival in _peer_copies(n, src_of, dst_of, src_r, land_r, ssem_r, rsem_r):
            send().wait_send()
            arrival().wait_recv()

    outs = pl.pallas_call(
        body, name=name, out_shape=[pltpu.HBM(a.shape, a.dtype) for a in flight],
        in_specs=[HBM_SPEC] * (ns + n) + [SEM_SPEC, SEM_SPEC, ANY], out_specs=[HBM_SPEC] * (ns + n),
        input_output_aliases={i: i for i in range(ns + n)},
        compiler_params=pltpu.CompilerParams(has_side_effects=_DATAFLOW),
    )(*flight, ssem, rsem, after)
    return list(outs[ns:])


_SIBLING = 1
_ICI = (2, 4, 6)


def _rel_peer(r):
    x, y, c = lax.axis_index("x"), lax.axis_index("y"), lax.axis_index("c")
    peer = (x ^ ((r >> 2) & 1), y ^ ((r >> 1) & 1), c ^ (r & 1))
    return peer, 4 * peer[0] + 2 * peer[1] + peer[2]


def _rcopy(ref, ssem, rsem, peer):
    return pltpu.make_async_remote_copy(src_ref=ref, dst_ref=ref, send_sem=ssem, recv_sem=rsem, device_id=peer,
                                        device_id_type=MESH)


def _gather2_start(lands, kinds, after, *, name):
    n = len(lands)

    def body(*refs):
        land_r, (send1, recv_sib, recv_ici), token = refs[:n], refs[n + 1:n + 4], refs[-1]
        me = _my_index()
        for a in range(n):
            own = _slab(kinds[a], land_r[a], me)
            for j, r in enumerate((_SIBLING,) + _ICI):
                rsem = recv_sib.at[a] if r == _SIBLING else recv_ici.at[3 * a + j - 1]
                _rcopy(own, send1.at[4 * a + j], rsem, _rel_peer(r)[0]).start()
        token[...] = jnp.zeros_like(token)

    sems = [pltpu.SemaphoreType.DMA((4 * n,)), pltpu.SemaphoreType.DMA((n,)), pltpu.SemaphoreType.DMA((3 * n,))]
    outs = pl.pallas_call(
        body, name=name, out_shape=(*sems, *[pltpu.HBM(a.shape, a.dtype) for a in lands], SDS((8, LANE), F32)),
        in_specs=[HBM_SPEC] * n + [ANY],
        out_specs=(SEM_SPEC,) * 3 + (HBM_SPEC,) * n + (pl.BlockSpec(memory_space=pltpu.VMEM),),
        input_output_aliases={i: 3 + i for i in range(n)},
        compiler_params=pltpu.CompilerParams(has_side_effects=_DATAFLOW),
    )(*[pltpu.with_memory_space_constraint(a, pltpu.HBM) for a in lands], after)
    return dict(send1=outs[0], recv_sib=outs[1], recv_ici=outs[2], lands=list(outs[3:3 + n])), outs[-1]


def _gather2_forward(handle, kinds, after, *, name):
    lands = handle["lands"]
    n = len(lands)

    def body(*refs):
        land_r, recv_ici, (fwd_send, fwd_recv), token = refs[:n], refs[n], refs[n + 2:n + 4], refs[-1]
        sibling = _rel_peer(_SIBLING)[0]
        for a in range(n):
            for j, r in enumerate(_ICI):
                got = _slab(kinds[a], land_r[a], _rel_peer(r)[1])
                _rcopy(got, fwd_send.at[3 * a + j], recv_ici.at[3 * a + j], sibling).wait_recv()
                _rcopy(got, fwd_send.at[3 * a + j], fwd_recv.at[3 * a + j], sibling).start()
        token[...] = jnp.zeros_like(token)

    sems = [pltpu.SemaphoreType.DMA((3 * n,)), pltpu.SemaphoreType.DMA((3 * n,))]
    outs = pl.pallas_call(
        body, name=name, out_shape=(*sems, *[pltpu.HBM(a.shape, a.dtype) for a in lands], SDS((8, LANE), F32)),
        in_specs=[HBM_SPEC] * n + [SEM_SPEC, ANY],
        out_specs=(SEM_SPEC,) * 2 + (HBM_SPEC,) * n + (pl.BlockSpec(memory_space=pltpu.VMEM),),
        input_output_aliases={i: 2 + i for i in range(n)},
        compiler_params=pltpu.CompilerParams(has_side_effects=_DATAFLOW),
    )(*lands, handle["recv_ici"], after)
    return dict(handle, fwd_send=outs[0], fwd_recv=outs[1], lands=list(outs[2:2 + n])), outs[-1]


def _gather2_wait(handle, kinds, after, *, name):
    lands = handle["lands"]
    n = len(lands)

    def body(*refs):
        land_r, (send1, recv_sib, fwd_send, fwd_recv) = refs[:n], refs[n:n + 4]
        me = _my_index()
        sibling, sib_idx = _rel_peer(_SIBLING)
        for a in range(n):
            own = _slab(kinds[a], land_r[a], me)
            for j, r in enumerate((_SIBLING,) + _ICI):
                _rcopy(own, send1.at[4 * a + j], recv_sib.at[a], _rel_peer(r)[0]).wait_send()
            theirs = _slab(kinds[a], land_r[a], sib_idx)
            _rcopy(theirs, send1.at[4 * a], recv_sib.at[a], sibling).wait_recv()
            for j, r in enumerate(_ICI):
                passed_on = _slab(kinds[a], land_r[a], _rel_peer(r)[1])
                _rcopy(passed_on, fwd_send.at[3 * a + j], fwd_recv.at[3 * a + j], sibling).wait_send()
                arrived = _slab(kinds[a], land_r[a], _rel_peer(r ^ _SIBLING)[1])
                _rcopy(arrived, fwd_send.at[3 * a + j], fwd_recv.at[3 * a + j], sibling).wait_recv()

    outs = pl.pallas_call(
        body, name=name, out_shape=[pltpu.HBM(a.shape, a.dtype) for a in lands],
        in_specs=[HBM_SPEC] * n + [SEM_SPEC] * 4 + [ANY], out_specs=[HBM_SPEC] * n,
        input_output_aliases={i: i for i in range(n)},
        compiler_params=pltpu.CompilerParams(has_side_effects=_DATAFLOW),
    )(*lands, handle["send1"], handle["recv_sib"], handle["fwd_send"], handle["fwd_recv"], after)
    return list(outs)


def _sum_slots(gathered, *, name):
    def body(g_ref, out_ref):
        total = g_ref[0]
        for dev in range(1, N_DEV):
            total = total + g_ref[dev]
        out_ref[...] = total

    return _pcall(body, out_shape=SDS(gathered.shape[1:], F32), compiler_params=_cp(), name=name)(gathered)


def _adam(g, w, m, v):
    nm = ADAM_B1 * m + (1.0 - ADAM_B1) * g
    nv = ADAM_B2 * v + (1.0 - ADAM_B2) * (g * g)
    m_hat = nm / (1.0 - ADAM_B1 ** ADAM_STEP)
    v_hat = nv / (1.0 - ADAM_B2 ** ADAM_STEP)
    return -ADAM_LR * (m_hat / (jnp.sqrt(v_hat) + ADAM_EPS) + ADAM_WD * w), nm, nv


def _adamw_staged(st0, st1, w, m, v, *, name):
    _, rows, cols = w.shape
    st_cols = st0.shape[2]
    tr = max(t for t in range(16, 129, 16) if rows % t == 0)
    nr = rows // tr

    def body(s0_ref, s1_ref, w_ref, m_ref, v_ref, g_ref, d_ref, nm_ref, nv_ref):
        for layer, s_ref in enumerate((s0_ref, s1_ref)):
            @pl.when(pl.program_id(0) == layer)
            def _(s_ref=s_ref):
                total = s_ref[0, :, 0:cols].astype(F32)
                for dev in range(1, N_DEV):
                    total = total + s_ref[dev, :, 0:cols].astype(F32)
                g_ref[0] = total

        d_ref[0], nm_ref[0], nv_ref[0] = _adam(g_ref[0], w_ref[0], m_ref[0], v_ref[0])

    st_spec = lambda layer: pl.BlockSpec(
        (N_DEV, tr, st_cols), lambda l, i: (0, jnp.where(l == layer, i, (nr - 1) * (1 - layer)), 0))
    par = pl.BlockSpec((1, tr, cols), lambda l, i: (l, i, 0))
    return _pcall(
        body, grid=(DEPTH, nr), in_specs=[st_spec(0), st_spec(1), par, par, par], out_specs=[par] * 4,
        out_shape=[SDS(w.shape, F32)] * 4,
        compiler_params=_cp(("arbitrary", "arbitrary")), name=name)(st0, st1, w, m, v)


def _adamw_small(g, w, m, v, *, name):
    def body(g_ref, w_ref, m_ref, v_ref, d_ref, nm_ref, nv_ref):
        d_ref[...], nm_ref[...], nv_ref[...] = _adam(g_ref[...], w_ref[...], m_ref[...], v_ref[...])

    return _pcall(body, out_shape=[SDS(g.shape, F32)] * 3, compiler_params=_cp(), name=name)(g, w, m, v)


def _pack_rows(arrays):
    flat = jnp.concatenate([a.reshape(-1).astype(F32) for a in arrays])
    rows = -(-flat.shape[0] // (8 * D)) * 8
    return jnp.pad(flat, (0, rows * D - flat.shape[0])).reshape(rows, D)


def _unpack_rows(pack, like):
    flat = pack.reshape(-1)
    out, at = [], 0
    for a in like:
        out.append(flat[at:at + a.size].reshape(a.shape))
        at += a.size
    return out


def _layer_fwd(x, h, mem, win, mixer_weights, ffn_weights, after_up, small, g_next, tag):
    proj = _mm(h, win, tm=1024, tn=1536, name=f"proj_{tag}")
    wkv, w4, cw_a, cw_b, cw_f = mixer_weights(proj)
    za = _bra_fwd(proj, cw_a, name=f"bra_fwd_{tag}")
    cb = _brb_conv_fwd(proj, cw_b, small["conv_b_bias"], name=f"brb_conv_fwd_{tag}")
    sb = _ln_silu_fwd(cb, small["ln_b_g"], small["ln_b_b"], name=f"ln_silu_fwd_{tag}")
    memn, kv = _kv_prep(mem, small["norm_mem_g"], wkv, name=f"kv_prep_{tag}")
    o = _attn_fwd(proj, kv, name=f"attn_fwd_{tag}")
    ya, yb, yc, mg, x1, h2 = _mix_out(x, za, sb, o, proj, w4, small["b_gate"], small["norm_ffn_g"],
                                      name=f"mix_out_{tag}")
    wup, wdn = ffn_weights(h2)
    u2 = _mm(h2, wup, tm=1024, tn=1536, name=f"up_{tag}")
    token = after_up(u2)
    act, c2 = _ffn_act(u2, cw_f if token is None else _behind(cw_f, token), name=f"ffn_act_{tag}")
    x2, h_next = _mm_res_norm(act, wdn, x1, g_next, name=f"down_{tag}")
    saved = dict(x=x, h=h, proj=proj, za=za, cb=cb, sb=sb, memn=memn, kv=kv, o=o, ya=ya, yb=yb, yc=yc,
                 mg=mg, x1=x1, h2=h2, u2=u2, c2=c2, act=act)
    return x2, h_next, (win, wup, wkv, w4, wdn, cw_a, cw_b, cw_f), saved


def _behind(operand, token):
    return operand + token[0:1, 0:1]


def _layer_bwd(dx2, dx2b, mem, wts, small, sv, start, tag):
    win, wup, wkv, w4, wdn, cw_a, cw_b, cw_f = wts
    dact = _mm(dx2b, wdn, tb=True, tm=1024, tn=768, name=f"d_act_{tag}")
    dwdn = _mm(sv["act"], dx2b, ta=True, tm=768, tn=1024, name=f"dw_down_{tag}")
    du2, dcw_f = _ffn_bwd(sv["u2"], sv["c2"], dact, cw_f, name=f"ffn_bwd_{tag}")
    dwup_t = _mm(du2, sv["h2"], ta=True, tm=C_UP_P, tn=1024, name=f"dw_up_{tag}")
    token = start(("wdn", "wupT"), (dwdn, dwup_t), f"ffn_{tag}")
    dx1, dx1b, dg_ffn = _mm_nt_normbwd(du2, wup, sv["x1"], dx2, _behind(small["norm_ffn_g"], token),
                                       name=f"d_h2_{tag}")

    dya, dyb, dyc, dza, dsb, do, dproj, dbg = _mix_bwd(dx1b, sv["ya"], sv["yb"], sv["yc"], sv["proj"], w4,
                                                      small["b_gate"], name=f"mix_bwd_{tag}")
    dw4 = jnp.stack([
        _mm(a, b, ta=True, tm=1024, tn=512, name=f"dw_{nm}_{tag}")
        for nm, a, b in (("a_out", sv["za"], dya), ("b_out", sv["sb"], dyb), ("att_out", sv["o"], dyc),
                         ("o", sv["mg"], dx1b))])
    dq, dk, dv = _attn_bwd(sv["proj"], sv["kv"], do, name=f"attn_bwd_{tag}")
    dwkv, dg_mem = _kv_bwd(mem, small["norm_mem_g"], sv["memn"], dk, dv, wkv, name=f"kv_bwd_{tag}")
    token = start(("w4", "wkv"), (dw4, dwkv), f"mix_{tag}")
    dproj, dcw_a = _bra_bwd(sv["proj"], dza, _behind(cw_a, token), dproj, name=f"bra_bwd_{tag}")
    dcb, ln_sums = _ln_silu_bwd(sv["cb"], dsb, small["ln_b_g"], small["ln_b_b"], name=f"ln_silu_bwd_{tag}")
    dproj, dcw_b = _brb_conv_bwd(sv["proj"], dcb, dq, cw_b, dproj, name=f"brb_conv_bwd_{tag}")
    dwin = _mm(sv["h"], dproj, ta=True, tm=1024, tn=768, name=f"dw_in_{tag}")
    token = start(("win",), (dwin,), f"in_{tag}")
    dx, dxb, dg_mix = _mm_nt_normbwd(dproj, win, sv["x"], dx1, _behind(small["norm_mix_g"], token),
                                     tk=4608, name=f"d_h_{tag}")

    small_grads = [dg_mix[0:1], dg_mem[0:1], dbg[0:1].reshape(3, D), ln_sums[2:3], ln_sums[0:1], ln_sums[1:2],
                   dg_ffn[0:1], dcw_a[0:K_A], dcw_b[0:K_B], dcw_f[0:K_F].reshape(K_F * 2 * FF_P // D, D)]
    return dx, dxb, small_grads, token


_SMALL_ROWS = (1, 1, 3, 1, 1, 1, 1, K_A, K_B, K_F * 2 * FF_P // D)
_CV_ROWS = 48


def kernel(x, mem, norm_mix_g, norm_mem_g, w_in, b_gate, conv_a_w, w_a_out, conv_b_w, conv_b_bias, ln_b_g, ln_b_b, w_b_out, w_kv, w_att_out, w_o, norm_ffn_g, w_up, conv_ffn_w, w_down, norm_final_g, loss_target, m_norm_mix_g, m_norm_mem_g, m_w_in, m_b_gate, m_conv_a_w, m_w_a_out, m_conv_b_w, m_conv_b_bias, m_ln_b_g, m_ln_b_b, m_w_b_out, m_w_kv, m_w_att_out, m_w_o, m_norm_ffn_g, m_w_up, m_conv_ffn_w, m_w_down, m_norm_final_g, v_norm_mix_g, v_norm_mem_g, v_w_in, v_b_gate, v_conv_a_w, v_w_a_out, v_conv_b_w, v_conv_b_bias, v_ln_b_g, v_ln_b_b, v_w_b_out, v_w_kv, v_w_att_out, v_w_o, v_norm_ffn_g, v_w_up, v_conv_ffn_w, v_w_down, v_norm_final_g):
    me = _my_index()
    me_arr = me.astype(jnp.int32).reshape(1)
    x0, mem0, tgt = x.reshape(x.shape[1:]), mem.reshape(mem.shape[1:]), loss_target.reshape(x.shape[1:])
    up_pad = ((0, 0), (0, 0), (0, C_UP_P - C_UP))

    ag_groups = (("win",), ("wkv", "w4", "cv"), ("wup", "wdn"))
    kinds = ag_groups[0] + ag_groups[1] + ag_groups[2]
    smalls, ag_handles = [], []
    token = jnp.zeros((8, LANE), F32)
    for l in range(DEPTH):
        cv = jnp.zeros((_CV_ROWS, C_UP_P), F32)
        cv = cv.at[0:K_F, 0:C_UP].set(conv_ffn_w[l]).at[3:3 + K_A, 0:R_O].set(conv_a_w[l])
        cv = cv.at[8:8 + K_B, 0:R_O].set(conv_b_w[l])
        shards = dict(
            win=w_in[l], wup=jnp.pad(w_up[l], up_pad[1:]), wkv=w_kv[l],
            w4=jnp.stack([w_a_out[l], w_b_out[l], w_att_out[l], w_o[l]]), wdn=w_down[l], cv=cv)
        whole = dict({k: SDS(*_WHOLE[k]) for k in kinds if k != "cv"}, cv=SDS((N_DEV,) + cv.shape, F32))
        per_layer = []
        for g, grp in enumerate(ag_groups):
            lands = [_place_own(k, shards[k], whole[k], True, me_arr, after=token, name=f"ag_own_{k}_l{l}")
                     for k in grp]
            handle, token = _gather2_start(lands, grp, token, name=f"ag_start_l{l}_g{g}")
            per_layer.append(handle)
        ag_handles.append(per_layer)
        if l == 0:
            started_l0 = token
        smalls.append(dict(
            norm_mix_g=norm_mix_g[l][None], norm_mem_g=norm_mem_g[l][None], b_gate=b_gate[l][None],
            conv_b_bias=conv_b_bias[l][None], ln_b_g=ln_b_g[l][None], ln_b_b=ln_b_b[l][None],
            norm_ffn_g=norm_ffn_g[l][None]))

    def forward_group(l, g, after):
        ag_handles[l][g], tok = _gather2_forward(ag_handles[l][g], ag_groups[g], after, name=f"ag_forward_l{l}_g{g}")
        return tok

    def group_of(l, g):
        def wait(after):
            if l == 0:
                after = forward_group(0, g, after)
            return _gather2_wait(ag_handles[l][g], ag_groups[g], after, name=f"ag_wait_l{l}_g{g}")
        return wait

    def mixer_weights(l):
        def wait(after):
            if l > 0:
                after = forward_group(l, 2, after)
            wkv, w4, cvg = group_of(l, 1)(after)
            cw_f = jnp.stack([cvg[d, 0:K_F, :] for d in UP_ORDER], axis=1).reshape(K_F, 2 * FF_P)
            cw_a = cvg[:, 3:3 + K_A, 0:R_O].transpose(1, 0, 2).reshape(K_A, D)
            cw_b = cvg[:, 8:8 + K_B, 0:R_O].transpose(1, 0, 2).reshape(K_B, D)
            return (wkv, w4, jnp.pad(cw_a, ((0, 8 - K_A), (0, 0))), jnp.pad(cw_b, ((0, 32 - K_B), (0, 0))),
                    jnp.pad(cw_f, ((0, 8 - K_F), (0, 0))))
        return wait

    wts, saved = [], []
    xs = x0
    h = _rms_fwd(xs, _behind(smalls[0]["norm_mix_g"], started_l0), name="rms_fwd")
    behind = forward_group(0, 0, token)

    def next_layer_forwarding(l):
        def hook(after):
            return None if l + 1 == DEPTH else forward_group(l + 1, 0, after)
        return hook

    for l in range(DEPTH):
        g_next = smalls[l + 1]["norm_mix_g"] if l + 1 < DEPTH else norm_final_g[None]
        if l > 0:
            behind = forward_group(l, 1, behind)
        (win,) = _gather2_wait(ag_handles[l][0], ag_groups[0], behind, name=f"ag_wait_l{l}_g0")
        xs, h, w_l, sv = _layer_fwd(xs, h, mem0, win, mixer_weights(l), group_of(l, 2), next_layer_forwarding(l),
                                    smalls[l], g_next, f"l{l}")
        behind = h
        wts.append(w_l)
        saved.append(sv)
    dx, dxb, head_sums = _loss_head(xs, tgt, norm_final_g[None], name="loss_head")

    rs_handles = []
    small_grads = [None] * DEPTH

    def start_scatter(grp, arrays, name):
        maps = _scatter_maps(grp)
        lands = [_place_own(k, a, SDS((N_DEV,) + _SHARD[k], BF), False, me_arr, name=f"rs_own_{k}_{name}")
                 for k, a in zip(grp, arrays)]
        handle, tok = _exchange_start(list(arrays), lands, maps, rs_handles[-1][2] if rs_handles else head_sums,
                                      name=f"rs_start_{name}")
        rs_handles.append((grp, handle, tok, name))
        return tok

    for l in reversed(range(DEPTH)):
        dx, dxb, small_grads[l], token = _layer_bwd(dx, dxb, mem0, wts[l], smalls[l], saved[l], start_scatter,
                                                    f"l{l}")

    pack = jnp.concatenate(small_grads[0] + small_grads[1] + [head_sums[1:2], head_sums[0:1]], axis=0)
    pack = jnp.pad(pack, ((0, -pack.shape[0] % 8), (0, 0)))
    small_maps = (lambda srcs, lands, a, idx: srcs[a]), (lambda lands, a, idx: lands[a].at[idx])
    small_land = _place_own("cv", pack, SDS((N_DEV,) + pack.shape, F32), True, me_arr, name="small_own")
    small_handle, small_token = _exchange_start([pack], [small_land], small_maps, dx, name="small_start")

    staged = [dict() for _ in range(DEPTH)]
    for grp, handle, _, name in rs_handles[:-1]:
        staged[int(name[-1])].update(zip(grp, _exchange_wait(handle, _scatter_maps(grp), small_token,
                                                             name=f"rs_wait_{name}")))

    def big_update(kind, w, m, v, name):
        return _adamw_staged(staged[0][kind], staged[1][kind], w, m, v, name=name)

    r_up = [jnp.swapaxes(a, 1, 2) for a in big_update(
        "wupT", jnp.swapaxes(w_up, 1, 2), jnp.swapaxes(m_w_up, 1, 2), jnp.swapaxes(v_w_up, 1, 2), "adamw_w_up")]
    r_kv = big_update("wkv", w_kv, m_w_kv, v_w_kv, "adamw_w_kv")
    r_dn = big_update("wdn", w_down, m_w_down, v_w_down, "adamw_w_down")

    def four(a, b, c, d_):
        return jnp.stack([a, b, c, d_], axis=1).reshape(DEPTH, 4 * R_O, D)

    r_4 = _adamw_staged(
        staged[0]["w4"].reshape(N_DEV, 4 * R_O, D), staged[1]["w4"].reshape(N_DEV, 4 * R_O, D),
        four(w_a_out, w_b_out, w_att_out, w_o), four(m_w_a_out, m_w_b_out, m_w_att_out, m_w_o),
        four(v_w_a_out, v_w_b_out, v_w_att_out, v_w_o), name="adamw_w_out")
    grp, handle, _, name = rs_handles[-1]
    staged[0].update(zip(grp, _exchange_wait(handle, _scatter_maps(grp), r_4[0], name=f"rs_wait_{name}")))
    r_in = big_update("win", w_in, m_w_in, v_w_in, "adamw_w_in")
    r_a, r_b, r_att, r_o = ([a.reshape(DEPTH, 4, R_O, D)[:, j] for a in r_4] for j in range(4))

    (gathered,) = _exchange_wait(small_handle, small_maps, r_in[0], name="small_wait")
    total = _sum_slots(gathered, name="small_sum")
    per_layer = sum(_SMALL_ROWS)
    parts = []
    for l in range(DEPTH):
        at, one = l * per_layer, []
        for rows in _SMALL_ROWS:
            one.append(total[at:at + rows])
            at += rows
        parts.append(one)
    g_final = total[DEPTH * per_layer]
    loss = 0.5 / D * jnp.sum(total[DEPTH * per_layer + 1])

    def both(i):
        return jnp.stack([parts[0][i], parts[1][i]])

    g_norm_mix, g_norm_mem = both(0)[:, 0], both(1)[:, 0]
    g_b_gate = both(2).reshape(DEPTH, 3 * D)
    g_cbias, g_lng, g_lnb, g_norm_ffn = both(3)[:, 0], both(4)[:, 0], both(5)[:, 0], both(6)[:, 0]
    g_conv_a = lax.dynamic_slice_in_dim(both(7), me * R_O, R_O, axis=2)
    g_conv_b = lax.dynamic_slice_in_dim(both(8), me * R_O, R_O, axis=2)
    g_conv_f = lax.dynamic_slice_in_dim(both(9).reshape(DEPTH, K_F, 2 * FF_P), _up_slot(me) * C_UP_P, C_UP, axis=2)

    small_g = [g_norm_mix, g_norm_mem, g_b_gate, g_conv_a, g_conv_b, g_cbias, g_lng, g_lnb, g_norm_ffn, g_conv_f,
               g_final]
    small_w = [norm_mix_g, norm_mem_g, b_gate, conv_a_w, conv_b_w, conv_b_bias, ln_b_g, ln_b_b, norm_ffn_g,
               conv_ffn_w, norm_final_g]
    small_m = [m_norm_mix_g, m_norm_mem_g, m_b_gate, m_conv_a_w, m_conv_b_w, m_conv_b_bias, m_ln_b_g, m_ln_b_b,
               m_norm_ffn_g, m_conv_ffn_w, m_norm_final_g]
    small_v = [v_norm_mix_g, v_norm_mem_g, v_b_gate, v_conv_a_w, v_conv_b_w, v_conv_b_bias, v_ln_b_g, v_ln_b_b,
               v_norm_ffn_g, v_conv_ffn_w, v_norm_final_g]
    upd = _adamw_small(_pack_rows(small_g), _pack_rows(small_w), _pack_rows(small_m), _pack_rows(small_v),
                       name="adamw_small")
    s_d, s_m, s_v = (_unpack_rows(p, small_w) for p in upd)
    (d_norm_mix, d_norm_mem, d_b_gate, d_conv_a, d_conv_b, d_cbias, d_lng, d_lnb, d_norm_ffn, d_conv_f,
     d_final) = s_d
    (nm_norm_mix, nm_norm_mem, nm_b_gate, nm_conv_a, nm_conv_b, nm_cbias, nm_lng, nm_lnb, nm_norm_ffn, nm_conv_f,
     nm_final) = s_m
    (nv_norm_mix, nv_norm_mem, nv_b_gate, nv_conv_a, nv_conv_b, nv_cbias, nv_lng, nv_lnb, nv_norm_ffn, nv_conv_f,
     nv_final) = s_v

    grads = [g_norm_mix, g_norm_mem, r_in[0], g_b_gate, g_conv_a, r_a[0], g_conv_b, g_cbias, g_lng, g_lnb, r_b[0],
             r_kv[0], r_att[0], r_o[0], g_norm_ffn, r_up[0], g_conv_f, r_dn[0], g_final]
    deltas = [d_norm_mix, d_norm_mem, r_in[1], d_b_gate, d_conv_a, r_a[1], d_conv_b, d_cbias, d_lng, d_lnb, r_b[1],
              r_kv[1], r_att[1], r_o[1], d_norm_ffn, r_up[1], d_conv_f, r_dn[1], d_final]
    new_m = [nm_norm_mix, nm_norm_mem, r_in[2], nm_b_gate, nm_conv_a, r_a[2], nm_conv_b, nm_cbias, nm_lng, nm_lnb,
             r_b[2], r_kv[2], r_att[2], r_o[2], nm_norm_ffn, r_up[2], nm_conv_f, r_dn[2], nm_final]
    new_v = [nv_norm_mix, nv_norm_mem, r_in[3], nv_b_gate, nv_conv_a, r_a[3], nv_conv_b, nv_cbias, nv_lng, nv_lnb,
             r_b[3], r_kv[3], r_att[3], r_o[3], nv_norm_ffn, r_up[3], nv_conv_f, r_dn[3], nv_final]
    return (loss, dx[None], *grads, *deltas, *new_m, *new_v)
```

```python
import functools

import jax
import jax.numpy as jnp
import numpy as np
from jax import lax
from jax.experimental import pallas as pl
from jax.experimental.pallas import tpu as pltpu

F32 = jnp.float32
BF = jnp.bfloat16
SDS = jax.ShapeDtypeStruct
MESH = pl.DeviceIdType.MESH
ANY = pl.BlockSpec(memory_space=pl.ANY)

N_DEV = 8
DEPTH = 2
D = 1024
N_HEADS = 4
HEAD = D // N_HEADS
D_FF = 2816
K_A, K_B, K_F = 3, 31, 3
NORM_EPS = 1e-6

C_IN = 9 * D // N_DEV
C_KV = 2 * D // N_DEV
C_UP = 2 * D_FF // N_DEV
LANE = 128
C_UP_P = -(-C_UP // LANE) * LANE
FF_P = 4 * C_UP_P
R_O = D // N_DEV
R_DN = D_FF // N_DEV

VMEM_LIMIT = 56 * 1024 * 1024
TM = 512
TR = 512
TR_FFN = 1024
SUB = 128
H_S, H_L = 16, 32

ADAM_LR, ADAM_B1, ADAM_B2, ADAM_EPS, ADAM_WD, ADAM_STEP = 0.001, 0.9, 0.999, 1e-08, 0.01, 10

UP_ORDER = (0, 4, 1, 5, 2, 6, 3, 7)


def _pcall(body, **kw):
    return pl.pallas_call(body, **kw)


def _cp(sem=None, **kw):
    return pltpu.CompilerParams(dimension_semantics=sem, vmem_limit_bytes=VMEM_LIMIT, **kw)


def _dot(a, b):
    return jnp.dot(a, b, preferred_element_type=F32)


def _dot_nt(a, b):
    return lax.dot_general(a, b, (((1,), (1,)), ((), ())), preferred_element_type=F32)


def _dot_tn(a, b):
    return lax.dot_general(a, b, (((0,), (0,)), ((), ())), preferred_element_type=F32)


def _sigmoid(z):
    return 1.0 / (1.0 + jnp.exp(-z))


def _rms(xv):
    return lax.rsqrt(jnp.mean(xv * xv, axis=-1, keepdims=True) + NORM_EPS)


def _up_slot(idx):
    return jnp.where(idx < 4, 2 * idx, 2 * (idx - 4) + 1)


def _dn_row(idx):
    return C_UP_P * (idx // 2) + R_DN * (idx % 2)


def _mm(a, b, *, ta=False, tb=False, out_dtype=BF, tm=TM, tn=512, tk=None, name):
    m, k_dim = (a.shape[1], a.shape[0]) if ta else a.shape
    n = b.shape[0] if tb else b.shape[1]
    tm, tn = min(tm, m), min(tn, n)
    tk = k_dim if tk is None else min(tk, k_dim)
    nk = k_dim // tk
    assert m % tm == 0 and n % tn == 0 and k_dim % tk == 0
    dims = (((0 if ta else 1,), (1 if tb else 0,)), ((), ()))

    def body(a_ref, b_ref, o_ref, *scratch):
        part = lax.dot_general(a_ref[...], b_ref[...], dims, preferred_element_type=F32)
        if nk == 1:
            o_ref[...] = part.astype(o_ref.dtype)
            return
        acc = scratch[0]
        k = pl.program_id(2)

        @pl.when(k == 0)
        def _():
            acc[...] = part

        @pl.when(k > 0)
        def _():
            acc[...] += part

        @pl.when(k == nk - 1)
        def _():
            o_ref[...] = acc[...].astype(o_ref.dtype)

    a_spec = pl.BlockSpec((tk, tm), lambda i, j, k: (k, i)) if ta else pl.BlockSpec((tm, tk), lambda i, j, k: (i, k))
    b_spec = pl.BlockSpec((tn, tk), lambda i, j, k: (j, k)) if tb else pl.BlockSpec((tk, tn), lambda i, j, k: (k, j))
    return _pcall(
        body, grid=(m // tm, n // tn, nk), in_specs=[a_spec, b_spec],
        out_specs=pl.BlockSpec((tm, tn), lambda i, j, k: (i, j)),
        out_shape=SDS((m, n), out_dtype),
        scratch_shapes=[pltpu.VMEM((tm, tn), F32)] if nk > 1 else [],
        compiler_params=_cp(("parallel", "parallel", "arbitrary")), name=name)(a, b)


def _mm_res_norm(a, w, x, g, *, name):
    s, k_dim = a.shape
    tm = min(TM, s)

    def body(a_ref, w_ref, x_ref, g_ref, xo_ref, h_ref):
        xo = x_ref[...] + _dot(a_ref[...], w_ref[...])
        xo_ref[...] = xo
        h_ref[...] = ((xo * _rms(xo)) * g_ref[...]).astype(BF)

    return _pcall(
        body, grid=(s // tm,),
        in_specs=[pl.BlockSpec((tm, k_dim), lambda i: (i, 0)),
                  pl.BlockSpec((k_dim, D), lambda i: (0, 0), pipeline_mode=pl.Buffered(1)),
                  pl.BlockSpec((tm, D), lambda i: (i, 0)), pl.BlockSpec((1, D), lambda i: (0, 0))],
        out_specs=[pl.BlockSpec((tm, D), lambda i: (i, 0))] * 2,
        out_shape=[SDS((s, D), F32), SDS((s, D), BF)],
        compiler_params=_cp(("parallel",)), name=name)(a, w, x, g)


def _mm_nt_normbwd(da, w, x, dres, g, *, tk=None, name):
    s, k_dim = da.shape
    tm = min(TM, s)
    tk = k_dim if tk is None else tk
    nk = k_dim // tk
    assert k_dim % tk == 0

    def body(da_ref, w_ref, x_ref, dres_ref, g_ref, dx_ref, dxb_ref, dg_ref, *scratch):
        i, k = pl.program_id(0), pl.program_id(1)
        part = _dot_nt(da_ref[...], w_ref[...])
        if nk > 1:
            acc = scratch[0]

            @pl.when(k == 0)
            def _():
                acc[...] = part

            @pl.when(k > 0)
            def _():
                acc[...] += part

        @pl.when((i == 0) & (k == 0))
        def _():
            dg_ref[...] = jnp.zeros_like(dg_ref)

        @pl.when(k == nk - 1)
        def _():
            dh = acc[...] if nk > 1 else part
            xv = x_ref[...]
            r = _rms(xv)
            xn = xv * r
            dg_ref[0:1, :] += jnp.sum(dh * xn, axis=0, keepdims=True)
            dxn = dh * g_ref[...]
            dx = dres_ref[...] + r * (dxn - xn * jnp.mean(dxn * xn, axis=-1, keepdims=True))
            dx_ref[...] = dx
            dxb_ref[...] = dx.astype(BF)

    row = lambda i, k: (i, 0)
    w_spec = (pl.BlockSpec((D, tk), lambda i, k: (0, k)) if nk > 1 else
              pl.BlockSpec((D, tk), lambda i, k: (0, 0), pipeline_mode=pl.Buffered(1)))
    return _pcall(
        body, grid=(s // tm, nk),
        in_specs=[pl.BlockSpec((tm, tk), lambda i, k: (i, k)), w_spec,
                  pl.BlockSpec((tm, D), row), pl.BlockSpec((tm, D), row), pl.BlockSpec((1, D), lambda i, k: (0, 0))],
        out_specs=[pl.BlockSpec((tm, D), row), pl.BlockSpec((tm, D), row), pl.BlockSpec((8, D), lambda i, k: (0, 0))],
        out_shape=[SDS((s, D), F32), SDS((s, D), BF), SDS((8, D), F32)],
        scratch_shapes=[pltpu.VMEM((tm, D), F32)] if nk > 1 else [],
        compiler_params=_cp(("arbitrary", "arbitrary")), name=name)(da, w, x, dres, g)


def _rms_fwd(x, g, *, name):
    s = x.shape[0]
    tm = min(TM, s)

    def body(x_ref, g_ref, h_ref):
        xv = x_ref[...]
        h_ref[...] = ((xv * _rms(xv)) * g_ref[...]).astype(BF)

    return _pcall(
        body, grid=(s // tm,),
        in_specs=[pl.BlockSpec((tm, D), lambda i: (i, 0)), pl.BlockSpec((1, D), lambda i: (0, 0))],
        out_specs=pl.BlockSpec((tm, D), lambda i: (i, 0)), out_shape=SDS((s, D), BF),
        compiler_params=_cp(("parallel",)), name=name)(x, g)


def _loss_head(x, tgt, g, *, name):
    s = x.shape[0]
    tm = min(TM, s)

    def body(x_ref, t_ref, g_ref, dx_ref, dxb_ref, sums_ref):
        @pl.when(pl.program_id(0) == 0)
        def _():
            sums_ref[...] = jnp.zeros_like(sums_ref)

        xv = x_ref[...]
        r = _rms(xv)
        xn = xv * r
        diff = xn * g_ref[...] - t_ref[...]
        sums_ref[0:1, :] += jnp.sum(diff * diff, axis=0, keepdims=True)
        dy = diff * (1.0 / D)
        sums_ref[1:2, :] += jnp.sum(dy * xn, axis=0, keepdims=True)
        dxn = dy * g_ref[...]
        dx = r * (dxn - xn * jnp.mean(dxn * xn, axis=-1, keepdims=True))
        dx_ref[...] = dx
        dxb_ref[...] = dx.astype(BF)

    row = lambda i: (i, 0)
    return _pcall(
        body, grid=(s // tm,),
        in_specs=[pl.BlockSpec((tm, D), row), pl.BlockSpec((tm, D), row), pl.BlockSpec((1, D), lambda i: (0, 0))],
        out_specs=[pl.BlockSpec((tm, D), row), pl.BlockSpec((tm, D), row), pl.BlockSpec((8, D), lambda i: (0, 0))],
        out_shape=[SDS((s, D), F32), SDS((s, D), BF), SDS((8, D), F32)],
        compiler_params=_cp(("arbitrary",)), name=name)(x, tgt, g)


def _halo_before(i, tr, h):
    return jnp.maximum(i * (tr // h) - 1, 0)


def _halo_after(i, tr, h, s):
    return jnp.minimum((i + 1) * (tr // h), s // h - 1)


def _taps(buf, w_ref, sl, k_w, base, rows):
    acc = None
    for k in range(k_w):
        t = w_ref[k:k + 1, sl] * buf[base + k:base + k + rows, sl]
        acc = t if acc is None else acc + t
    return acc


def _taps_rev(buf, w_ref, sl, k_w, base, rows):
    acc = None
    for k in range(k_w):
        t = w_ref[k:k + 1, sl] * buf[base + k_w - 1 - k:base + k_w - 1 - k + rows, sl]
        acc = t if acc is None else acc + t
    return acc


def _tap_grads(dw_ref, dc, buf, sl, k_w, base, rows):
    for k in range(k_w):
        dw_ref[k:k + 1, sl] += jnp.sum(dc * buf[base + k:base + k + rows, sl], axis=0, keepdims=True)


def _bra_fwd(proj, cw, *, name):
    s = proj.shape[0]
    tr, h = min(TR, s), H_S
    sub = min(SUB, tr)

    def body(cur, halo, w_ref, za_ref, cvb):
        i = pl.program_id(0)
        hv = halo[:, D:2 * D].astype(F32) * halo[:, 2 * D:3 * D].astype(F32)
        cvb[0:h, :] = jnp.where(i == 0, 0.0, hv)
        cvb[h:h + tr, :] = cur[:, D:2 * D].astype(F32) * cur[:, 2 * D:3 * D].astype(F32)
        for c in range(D // LANE):
            sl = slice(LANE * c, LANE * c + LANE)
            ca = _taps(cvb, w_ref, sl, K_A, h - (K_A - 1), tr)
            za_ref[:, sl] = (cur[:, sl].astype(F32) * ca).astype(BF)

    return _pcall(
        body, grid=(s // tr,),
        in_specs=[pl.BlockSpec((tr, 3 * D), lambda i: (i, 0)),
                  pl.BlockSpec((h, 3 * D), lambda i: (_halo_before(i, tr, h), 0)),
                  pl.BlockSpec((8, D), lambda i: (0, 0))],
        out_specs=pl.BlockSpec((tr, D), lambda i: (i, 0)), out_shape=SDS((s, D), BF),
        scratch_shapes=[pltpu.VMEM((h + tr, D), F32)],
        compiler_params=_cp(("parallel",)), name=name)(proj, proj, cw)


def _bra_bwd(proj, dza, cw, dproj, *, name):
    s = proj.shape[0]
    tr, h = min(TR, s), H_S
    sub = min(SUB, tr)
    n = s // tr

    def body(before, cur, after, dz_cur, dz_after, w_ref, dproj_in, da_ref, dw_ref, cvb, dcab):
        del dproj_in
        i = pl.program_id(0)

        @pl.when(i == 0)
        def _():
            dw_ref[...] = jnp.zeros_like(dw_ref)

        first, last = i == 0, i == n - 1
        cvb[0:h, :] = jnp.where(first, 0.0, before[:, D:2 * D].astype(F32) * before[:, 2 * D:3 * D].astype(F32))
        cvb[h:h + tr, :] = cur[:, D:2 * D].astype(F32) * cur[:, 2 * D:3 * D].astype(F32)
        dcab[0:tr, :] = dz_cur[...].astype(F32) * cur[:, 0:D].astype(F32)
        dcab[tr:tr + h, :] = jnp.where(last, 0.0, dz_after[...].astype(F32) * after[:, 0:D].astype(F32))
        for c in range(D // LANE):
            sl = slice(LANE * c, LANE * c + LANE)
            gl, vl = slice(D + LANE * c, D + LANE * c + LANE), slice(2 * D + LANE * c, 2 * D + LANE * c + LANE)
            for r0 in range(0, tr, sub):
                rows = slice(r0, r0 + sub)
                ca = _taps(cvb, w_ref, sl, K_A, h - (K_A - 1) + r0, sub)
                da_ref[rows, sl] = (dz_cur[rows, sl].astype(F32) * ca).astype(BF)
                dcv = _taps_rev(dcab, w_ref, sl, K_A, r0, sub)
                da_ref[rows, gl] = (dcv * cur[rows, vl].astype(F32)).astype(BF)
                da_ref[rows, vl] = (dcv * cur[rows, gl].astype(F32)).astype(BF)
                _tap_grads(dw_ref, dcab[rows, sl], cvb, sl, K_A, h - (K_A - 1) + r0, sub)

    return _pcall(
        body, grid=(n,),
        in_specs=[pl.BlockSpec((h, 3 * D), lambda i: (_halo_before(i, tr, h), 0)),
                  pl.BlockSpec((tr, 3 * D), lambda i: (i, 0)),
                  pl.BlockSpec((h, 3 * D), lambda i: (_halo_after(i, tr, h, s), 0)),
                  pl.BlockSpec((tr, D), lambda i: (i, 0)),
                  pl.BlockSpec((h, D), lambda i: (_halo_after(i, tr, h, s), 0)),
                  pl.BlockSpec((8, D), lambda i: (0, 0)), ANY],
        out_specs=[pl.BlockSpec((tr, 3 * D), lambda i: (i, 0)), pl.BlockSpec((8, D), lambda i: (0, 0))],
        out_shape=[SDS(dproj.shape, BF), SDS((8, D), F32)], input_output_aliases={6: 0},
        scratch_shapes=[pltpu.VMEM((h + tr, D), F32), pltpu.VMEM((tr + h, D), F32)],
        compiler_params=_cp(("arbitrary",)), name=name)(proj, proj, proj, dza, dza, cw, dproj)


_U_COL, _UG_COL = 3, 4


def _brb_conv_fwd(proj, cw, bias, ln_g, ln_b, *, name):
    s = proj.shape[0]
    tr, h = min(TR, s), H_L
    sub = min(SUB, tr)

    def body(u_cur, ug_cur, u_halo, ug_halo, w_ref, b_ref, g_ref, bb_ref, cb_ref, sb_ref, glb, shifted):
        i = pl.program_id(0)
        glb[0:h, :] = jnp.where(i == 0, 0.0, u_halo[...].astype(F32) * _sigmoid(ug_halo[...].astype(F32)))
        glb[h:h + tr, :] = u_cur[...].astype(F32) * _sigmoid(ug_cur[...].astype(F32))
        for c in range(D // LANE):
            sl = slice(LANE * c, LANE * c + LANE)
            for r in range(1, 8):
                shifted[r] = glb[8 - r:8 - r + tr + 24, sl]
            for r0 in range(0, tr, sub):
                acc = None
                for k in range(K_B):
                    q, r = divmod(K_B - 1 - k, 8)
                    at = r0 - 8 * q
                    win = shifted[r, 24 + at:24 + at + sub, :] if r else glb[h + at:h + at + sub, sl]
                    term = w_ref[k:k + 1, sl] * win
                    acc = term if acc is None else acc + term
                cb_ref[r0:r0 + sub, sl] = (acc + b_ref[:, sl]).astype(BF)
        for r0 in range(0, tr, sub):
            z = cb_ref[r0:r0 + sub, :].astype(F32)
            zc = z - jnp.mean(z, axis=-1, keepdims=True)
            ln = (zc * lax.rsqrt(jnp.mean(zc * zc, axis=-1, keepdims=True) + NORM_EPS)) * g_ref[...] + bb_ref[...]
            sb_ref[r0:r0 + sub, :] = (ln * _sigmoid(ln)).astype(BF)

    vec = pl.BlockSpec((1, D), lambda i: (0, 0))
    return _pcall(
        body, grid=(s // tr,),
        in_specs=[pl.BlockSpec((tr, D), lambda i: (i, _U_COL)), pl.BlockSpec((tr, D), lambda i: (i, _UG_COL)),
                  pl.BlockSpec((h, D), lambda i: (_halo_before(i, tr, h), _U_COL)),
                  pl.BlockSpec((h, D), lambda i: (_halo_before(i, tr, h), _UG_COL)),
                  pl.BlockSpec((32, D), lambda i: (0, 0)), vec, vec, vec],
        out_specs=[pl.BlockSpec((tr, D), lambda i: (i, 0))] * 2, out_shape=[SDS((s, D), BF)] * 2,
        scratch_shapes=[pltpu.VMEM((h + tr, D), F32), pltpu.VMEM((8, tr + 24, LANE), F32)],
        compiler_params=_cp(("parallel",)), name=name)(proj, proj, proj, proj, cw, bias, ln_g, ln_b)


def _brb_conv_bwd(proj, dcb, dq, cw, dproj, *, name):
    s = proj.shape[0]
    tr, h = min(TR, s), H_L
    sub = min(SUB, tr)
    n = s // tr
    nb = -(-(tr + 24) // sub)
    sel = _row_selector(sub, list(range(8)))

    def body(u_cur, ug_cur, d_cur, d_after, dq_ref, sel_ref, w_ref, dproj_in, db_ref, dw_ref, dcbb, shifted):
        del dproj_in
        i = pl.program_id(0)

        @pl.when(i == 0)
        def _():
            dw_ref[...] = jnp.zeros_like(dw_ref)

        db_ref[:, 2 * D:3 * D] = dq_ref[...]
        after = d_after[...]
        dcbb[0:tr, :] = d_cur[...]
        dcbb[tr:tr + h, :] = jnp.where(i == n - 1, jnp.zeros_like(after), after)
        dcbb[tr + h:(nb + 1) * sub, :] = jnp.zeros(((nb + 1) * sub - h - tr, D), BF)
        for c in range(D // LANE):
            sl = slice(LANE * c, LANE * c + LANE)
            for blk in range(nb):
                res = _dot(sel_ref[...], dcbb[blk * sub:(blk + 2) * sub, sl])
                for r in range(8):
                    shifted[r, blk * sub:(blk + 1) * sub, :] = res[r * sub:(r + 1) * sub]
            for r0 in range(0, tr, sub):
                u = u_cur[r0:r0 + sub, sl].astype(F32)
                sg = _sigmoid(ug_cur[r0:r0 + sub, sl].astype(F32))
                glu = u * sg
                dglu = None
                for k in range(K_B):
                    q, r = divmod(K_B - 1 - k, 8)
                    at = r0 + 8 * q
                    win = shifted[r, at:at + sub, :]
                    term = w_ref[k:k + 1, sl] * win
                    dglu = term if dglu is None else dglu + term
                    dw_ref[k:k + 1, sl] += jnp.sum(win * glu, axis=0, keepdims=True)
                db_ref[r0:r0 + sub, sl] = (dglu * sg).astype(BF)
                db_ref[r0:r0 + sub, D + LANE * c:D + LANE * c + LANE] = (dglu * u * sg * (1.0 - sg)).astype(BF)

    return _pcall(
        body, grid=(n,),
        in_specs=[pl.BlockSpec((tr, D), lambda i: (i, _U_COL)), pl.BlockSpec((tr, D), lambda i: (i, _UG_COL)),
                  pl.BlockSpec((tr, D), lambda i: (i, 0)),
                  pl.BlockSpec((h, D), lambda i: (_halo_after(i, tr, h, s), 0)),
                  pl.BlockSpec((tr, D), lambda i: (i, 0)),
                  pl.BlockSpec(sel.shape, lambda i: (0, 0)),
                  pl.BlockSpec((32, D), lambda i: (0, 0)), ANY],
        out_specs=[pl.BlockSpec((tr, 3 * D), lambda i: (i, 1)), pl.BlockSpec((32, D), lambda i: (0, 0))],
        out_shape=[SDS(dproj.shape, BF), SDS((32, D), F32)], input_output_aliases={7: 0},
        scratch_shapes=[pltpu.VMEM(((nb + 1) * sub, D), BF), pltpu.VMEM((8, nb * sub, LANE), F32)],
        compiler_params=_cp(("arbitrary",)), name=name)(proj, proj, dcb, dcb, dq, sel, cw, dproj)


def _ln_silu_fwd(cb, g, b, *, name):
    s = cb.shape[0]
    tm = min(TM, s)

    def body(cb_ref, g_ref, b_ref, sb_ref):
        z = cb_ref[...].astype(F32)
        zc = z - jnp.mean(z, axis=-1, keepdims=True)
        ln = (zc * lax.rsqrt(jnp.mean(zc * zc, axis=-1, keepdims=True) + NORM_EPS)) * g_ref[...] + b_ref[...]
        sb_ref[...] = (ln * _sigmoid(ln)).astype(BF)

    row = lambda i: (i, 0)
    vec = pl.BlockSpec((1, D), lambda i: (0, 0))
    return _pcall(
        body, grid=(s // tm,), in_specs=[pl.BlockSpec((tm, D), row), vec, vec],
        out_specs=pl.BlockSpec((tm, D), row), out_shape=SDS((s, D), BF),
        compiler_params=_cp(("parallel",)), name=name)(cb, g, b)


def _ln_silu_bwd(cb, dsb, g, b, *, name):
    s = cb.shape[0]
    tm = min(TM, s)

    def body(cb_ref, dsb_ref, g_ref, b_ref, dcb_ref, sums_ref):
        @pl.when(pl.program_id(0) == 0)
        def _():
            sums_ref[...] = jnp.zeros_like(sums_ref)

        z = cb_ref[...].astype(F32)
        zc = z - jnp.mean(z, axis=-1, keepdims=True)
        rstd = lax.rsqrt(jnp.mean(zc * zc, axis=-1, keepdims=True) + NORM_EPS)
        lnh = zc * rstd
        ln = lnh * g_ref[...] + b_ref[...]
        sg = _sigmoid(ln)
        dln = dsb_ref[...].astype(F32) * (sg * (1.0 + ln * (1.0 - sg)))
        sums_ref[0:1, :] += jnp.sum(dln * lnh, axis=0, keepdims=True)
        sums_ref[1:2, :] += jnp.sum(dln, axis=0, keepdims=True)
        dlnh = dln * g_ref[...]
        dz = rstd * (dlnh - jnp.mean(dlnh, axis=-1, keepdims=True)
                     - lnh * jnp.mean(dlnh * lnh, axis=-1, keepdims=True))
        sums_ref[2:3, :] += jnp.sum(dz, axis=0, keepdims=True)
        dcb_ref[...] = dz.astype(BF)

    row = lambda i: (i, 0)
    vec = pl.BlockSpec((1, D), lambda i: (0, 0))
    return _pcall(
        body, grid=(s // tm,), in_specs=[pl.BlockSpec((tm, D), row), pl.BlockSpec((tm, D), row), vec, vec],
        out_specs=[pl.BlockSpec((tm, D), row), pl.BlockSpec((8, D), lambda i: (0, 0))],
        out_shape=[SDS((s, D), BF), SDS((8, D), F32)],
        compiler_params=_cp(("arbitrary",)), name=name)(cb, dsb, g, b)


_Q_COL = 5 * D // HEAD


def _kv_prep(mem, g, wkv, *, name):
    m = mem.shape[0]

    def body(mem_ref, g_ref, w_ref, memn_ref, kv_ref):
        mv = mem_ref[...]
        memn = ((mv * _rms(mv)) * g_ref[...]).astype(BF)
        memn_ref[...] = memn
        for dev in range(N_DEV):
            kv_ref[:, dev * C_KV:(dev + 1) * C_KV] = _dot(memn, w_ref[dev]).astype(BF)

    return _pcall(body, out_shape=[SDS((m, D), BF), SDS((m, 2 * D), BF)],
                  compiler_params=_cp(), name=name)(mem, g, wkv)


def _softmax_rows(q, k):
    sc = _dot_nt(q, k) * (1.0 / (HEAD ** 0.5))
    e = jnp.exp(sc - jnp.max(sc, axis=-1, keepdims=True))
    return e / jnp.sum(e, axis=-1, keepdims=True)


def _attn_fwd(proj, kv, *, name):
    s, m = proj.shape[0], kv.shape[0]
    tm = min(TM, s)

    def body(q_ref, kv_ref, o_ref):
        for hd in range(N_HEADS):
            cols = slice(hd * HEAD, (hd + 1) * HEAD)
            p = _softmax_rows(q_ref[:, cols], kv_ref[:, cols])
            o_ref[:, cols] = _dot(p.astype(BF), kv_ref[:, D + hd * HEAD:D + (hd + 1) * HEAD]).astype(BF)

    return _pcall(
        body, grid=(s // tm,),
        in_specs=[pl.BlockSpec((tm, D), lambda i: (i, _Q_COL // N_HEADS)),
                  pl.BlockSpec((m, 2 * D), lambda i: (0, 0))],
        out_specs=pl.BlockSpec((tm, D), lambda i: (i, 0)), out_shape=SDS((s, D), BF),
        compiler_params=_cp(("parallel",)), name=name)(proj, kv)


def _attn_bwd(proj, kv, do, *, name):
    s, m = proj.shape[0], kv.shape[0]
    tm = min(TM, s)

    def body(q_ref, kv_ref, do_ref, dq_ref, dk_ref, dv_ref):
        @pl.when(pl.program_id(0) == 0)
        def _():
            dk_ref[...] = jnp.zeros_like(dk_ref)
            dv_ref[...] = jnp.zeros_like(dv_ref)

        for hd in range(N_HEADS):
            cols = slice(hd * HEAD, (hd + 1) * HEAD)
            q, k, dov = q_ref[:, cols], kv_ref[:, cols], do_ref[:, cols]
            p = _softmax_rows(q, k)
            dp = _dot_nt(dov, kv_ref[:, D + hd * HEAD:D + (hd + 1) * HEAD])
            dv_ref[:, cols] += _dot_tn(p.astype(BF), dov)
            ds = (p * (dp - jnp.sum(dp * p, axis=-1, keepdims=True)) * (1.0 / (HEAD ** 0.5))).astype(BF)
            dq_ref[:, cols] = _dot(ds, k).astype(BF)
            dk_ref[:, cols] += _dot_tn(ds, q)

    return _pcall(
        body, grid=(s // tm,),
        in_specs=[pl.BlockSpec((tm, D), lambda i: (i, _Q_COL // N_HEADS)),
                  pl.BlockSpec((m, 2 * D), lambda i: (0, 0)),
                  pl.BlockSpec((tm, D), lambda i: (i, 0))],
        out_specs=[pl.BlockSpec((tm, D), lambda i: (i, 0)),
                   pl.BlockSpec((m, D), lambda i: (0, 0)),
                   pl.BlockSpec((m, D), lambda i: (0, 0))],
        out_shape=[SDS((s, D), BF), SDS((m, D), F32), SDS((m, D), F32)],
        compiler_params=_cp(("arbitrary",)), name=name)(proj, kv, do)


def _kv_bwd(mem, g, memn, dk, dv, wkv, *, name):
    def body(mem_ref, g_ref, memn_ref, dk_ref, dv_ref, w_ref, dw_ref, dg_ref):
        memn = memn_ref[...]
        dmemn = None
        for dev in range(N_DEV):
            d_ref, col = (dk_ref, dev) if dev < N_HEADS else (dv_ref, dev - N_HEADS)
            dslab = d_ref[:, col * C_KV:(col + 1) * C_KV].astype(BF)
            dw_ref[dev] = _dot_tn(memn, dslab).astype(BF)
            part = _dot_nt(dslab, w_ref[dev])
            dmemn = part if dmemn is None else dmemn + part
        mv = mem_ref[...]
        dg_ref[...] = jnp.zeros_like(dg_ref)
        dg_ref[0:1, :] = jnp.sum(dmemn * (mv * _rms(mv)), axis=0, keepdims=True)

    assert C_KV == HEAD
    return _pcall(body, out_shape=[SDS((N_DEV, D, C_KV), BF), SDS((8, D), F32)],
                  compiler_params=_cp(), name=name)(mem, g, memn, dk, dv, wkv)


_TM_MIX = 512


def _mix_out(x, za, sb, o, proj, w4, bg, g_next, *, name):
    s = x.shape[0]
    tm = min(_TM_MIX, s)

    def body(x_ref, za_ref, sb_ref, o_ref, pg_ref, w4_ref, bg_ref, gn_ref,
             ya_ref, yb_ref, yc_ref, mg_ref, x1_ref, h_ref):
        ys = (_dot(za_ref[...], w4_ref[0]), _dot(sb_ref[...], w4_ref[1]), _dot(o_ref[...], w4_ref[2]))
        merged = None
        for j, (y, y_ref) in enumerate(zip(ys, (ya_ref, yb_ref, yc_ref))):
            y_ref[...] = y.astype(BF)
            gate = _sigmoid(pg_ref[:, j * D:(j + 1) * D].astype(F32) + bg_ref[:, j * D:(j + 1) * D])
            merged = gate * y if merged is None else merged + gate * y
        mg = merged.astype(BF)
        mg_ref[...] = mg
        x1 = x_ref[...] + _dot(mg, w4_ref[3])
        x1_ref[...] = x1
        h_ref[...] = ((x1 * _rms(x1)) * gn_ref[...]).astype(BF)

    row = lambda i: (i, 0)
    act = pl.BlockSpec((tm, D), row)
    return _pcall(
        body, grid=(s // tm,),
        in_specs=[act, act, act, act, pl.BlockSpec((tm, 3 * D), lambda i: (i, 2)),
                  pl.BlockSpec((4, D, D), lambda i: (0, 0, 0), pipeline_mode=pl.Buffered(1)), pl.BlockSpec((1, 3 * D), lambda i: (0, 0)),
                  pl.BlockSpec((1, D), lambda i: (0, 0))],
        out_specs=[act] * 6,
        out_shape=[SDS((s, D), BF)] * 4 + [SDS((s, D), F32), SDS((s, D), BF)],
        compiler_params=_cp(("parallel",)), name=name)(x, za, sb, o, proj, w4, bg, g_next)


def _mix_bwd(dxb, ya, yb, yc, proj, w4, bg, *, name):
    s = dxb.shape[0]
    tm = min(_TM_MIX, s)

    def body(dx_ref, ya_ref, yb_ref, yc_ref, pg_ref, w4_ref, bg_ref,
             dya_ref, dyb_ref, dyc_ref, dza_ref, dsb_ref, do_ref, dgt_ref, dbg_ref):
        @pl.when(pl.program_id(0) == 0)
        def _():
            dbg_ref[...] = jnp.zeros_like(dbg_ref)

        dm = _dot_nt(dx_ref[...], w4_ref[3])
        for j, (y_ref, dy_ref, din_ref) in enumerate(zip((ya_ref, yb_ref, yc_ref), (dya_ref, dyb_ref, dyc_ref),
                                                         (dza_ref, dsb_ref, do_ref))):
            cols = slice(j * D, (j + 1) * D)
            gate = _sigmoid(pg_ref[:, cols].astype(F32) + bg_ref[:, cols])
            dy = (dm * gate).astype(BF)
            dy_ref[...] = dy
            din_ref[...] = _dot_nt(dy, w4_ref[j]).astype(BF)
            dpre = dm * y_ref[...].astype(F32) * gate * (1.0 - gate)
            dgt_ref[:, cols] = dpre.astype(BF)
            dbg_ref[0:1, cols] += jnp.sum(dpre, axis=0, keepdims=True)

    row = lambda i: (i, 0)
    act = pl.BlockSpec((tm, D), row)
    return _pcall(
        body, grid=(s // tm,),
        in_specs=[act, act, act, act, pl.BlockSpec((tm, 3 * D), lambda i: (i, 2)),
                  pl.BlockSpec((4, D, D), lambda i: (0, 0, 0), pipeline_mode=pl.Buffered(1)), pl.BlockSpec((1, 3 * D), lambda i: (0, 0))],
        out_specs=[act] * 6 + [pl.BlockSpec((tm, 3 * D), lambda i: (i, 2)),
                               pl.BlockSpec((8, 3 * D), lambda i: (0, 0))],
        out_shape=[SDS((s, D), BF)] * 6 + [SDS((s, 9 * D), BF), SDS((8, 3 * D), F32)],
        compiler_params=_cp(("arbitrary",)), name=name)(dxb, ya, yb, yc, proj, w4, bg)


_PAIR = 2 * C_UP_P


def _row_selector(sub, first_cols):
    rows = np.arange(len(first_cols) * sub)
    col = np.asarray(first_cols)[rows // sub] + rows % sub
    return jnp.asarray(np.arange(2 * sub)[None, :] == col[:, None], BF)


def _ffn_act(u2, cw, *, name):
    s = u2.shape[0]
    tr = min(TR_FFN, s)
    sub = min(SUB, tr)
    sel = _row_selector(sub, [sub - (K_F - 1 - k) for k in range(K_F)])

    def body(cur, prev, sel_ref, w_ref, act_ref, c2_ref, xb, win):
        i = pl.program_id(1)
        before = prev[...]
        xb[0:sub, :] = jnp.where(i == 0, jnp.zeros_like(before), before)
        xb[sub:sub + tr, :] = cur[...]
        for r0 in range(0, tr, sub):
            win[...] = _dot(sel_ref[...], xb[r0:r0 + 2 * sub, :])
            for c in range(C_UP_P // LANE):
                gl = slice(LANE * c, LANE * c + LANE)
                ul = slice(C_UP_P + LANE * c, C_UP_P + LANE * c + LANE)
                gt = sum(w_ref[k:k + 1, gl] * win[k * sub:(k + 1) * sub, gl] for k in range(K_F))
                up = sum(w_ref[k:k + 1, ul] * win[k * sub:(k + 1) * sub, ul] for k in range(K_F))
                c2_ref[r0:r0 + sub, gl] = gt.astype(BF)
                c2_ref[r0:r0 + sub, ul] = up.astype(BF)
                act_ref[r0:r0 + sub, gl] = (gt * _sigmoid(gt) * up).astype(BF)

    return _pcall(
        body, grid=(4, s // tr),
        in_specs=[pl.BlockSpec((tr, _PAIR), lambda p, i: (i, p)),
                  pl.BlockSpec((sub, _PAIR), lambda p, i: (_halo_before(i, tr, sub), p)),
                  pl.BlockSpec(sel.shape, lambda p, i: (0, 0)),
                  pl.BlockSpec((8, _PAIR), lambda p, i: (0, p))],
        out_specs=[pl.BlockSpec((tr, C_UP_P), lambda p, i: (i, p)), pl.BlockSpec((tr, _PAIR), lambda p, i: (i, p))],
        out_shape=[SDS((s, FF_P), BF), SDS((s, 2 * FF_P), BF)],
        scratch_shapes=[pltpu.VMEM((sub + tr, _PAIR), BF), pltpu.VMEM((K_F * sub, _PAIR), F32)],
        compiler_params=_cp(("parallel", "parallel")), name=name)(u2, u2, sel, cw)


def _ffn_bwd(u2, c2, dact, cw, *, name):
    s = u2.shape[0]
    tr, h = min(TR_FFN, s), H_S
    sub = min(SUB, tr)
    n = s // tr
    sel = _row_selector(sub, [K_F - 1 - k for k in range(K_F)])

    def body(u_cur, c_cur, c_after, da_cur, da_after, sel_ref, w_ref, du_ref, dw_ref, dcb, win):
        i = pl.program_id(1)
        last = i == n - 1

        @pl.when(i == 0)
        def _():
            dw_ref[...] = jnp.zeros_like(dw_ref)

        def conv_grad(gt, up, da):
            gt, up, da = gt.astype(F32), up.astype(F32), da.astype(F32)
            sg = _sigmoid(gt)
            return (da * up * (sg * (1.0 + gt * (1.0 - sg)))).astype(BF), (da * (gt * sg)).astype(BF)

        for c in range(C_UP_P // LANE):
            gl = slice(LANE * c, LANE * c + LANE)
            ul = slice(C_UP_P + LANE * c, C_UP_P + LANE * c + LANE)
            for r0 in range(0, tr, sub):
                rows = slice(r0, r0 + sub)
                dcb[rows, gl], dcb[rows, ul] = conv_grad(c_cur[rows, gl], c_cur[rows, ul], da_cur[rows, gl])
            dg, du_ = conv_grad(c_after[:, gl], c_after[:, ul], da_after[:, gl])
            dcb[tr:tr + h, gl] = jnp.where(last, jnp.zeros_like(dg), dg)
            dcb[tr:tr + h, ul] = jnp.where(last, jnp.zeros_like(du_), du_)
        dcb[tr + h:tr + sub, :] = jnp.zeros((sub - h, _PAIR), BF)
        for r0 in range(0, tr, sub):
            win[...] = _dot(sel_ref[...], dcb[r0:r0 + 2 * sub, :])
            for c in range(_PAIR // LANE):
                sl = slice(LANE * c, LANE * c + LANE)
                u = u_cur[r0:r0 + sub, sl].astype(F32)
                du = None
                for k in range(K_F):
                    wk = win[k * sub:(k + 1) * sub, sl]
                    term = w_ref[k:k + 1, sl] * wk
                    du = term if du is None else du + term
                    dw_ref[k:k + 1, sl] += jnp.sum(wk * u, axis=0, keepdims=True)
                du_ref[r0:r0 + sub, sl] = du.astype(BF)

    return _pcall(
        body, grid=(4, n),
        in_specs=[pl.BlockSpec((tr, _PAIR), lambda p, i: (i, p)),
                  pl.BlockSpec((tr, _PAIR), lambda p, i: (i, p)),
                  pl.BlockSpec((h, _PAIR), lambda p, i: (_halo_after(i, tr, h, s), p)),
                  pl.BlockSpec((tr, C_UP_P), lambda p, i: (i, p)),
                  pl.BlockSpec((h, C_UP_P), lambda p, i: (_halo_after(i, tr, h, s), p)),
                  pl.BlockSpec(sel.shape, lambda p, i: (0, 0)),
                  pl.BlockSpec((8, _PAIR), lambda p, i: (0, p))],
        out_specs=[pl.BlockSpec((tr, _PAIR), lambda p, i: (i, p)), pl.BlockSpec((8, _PAIR), lambda p, i: (0, p))],
        out_shape=[SDS((s, 2 * FF_P), BF), SDS((8, 2 * FF_P), F32)],
        scratch_shapes=[pltpu.VMEM((tr + sub, _PAIR), BF), pltpu.VMEM((K_F * sub, _PAIR), F32)],
        compiler_params=_cp(("parallel", "arbitrary")), name=name)(u2, c2, c2, dact, dact, sel, cw)


def _relations():
    x, y, c = lax.axis_index("x"), lax.axis_index("y"), lax.axis_index("c")
    out = []
    for r in range(1, N_DEV):
        rx, ry, rc = (r >> 2) & 1, (r >> 1) & 1, r & 1
        out.append((r, (x ^ rx, y ^ ry, c ^ rc)))
    return out


def _my_index():
    return 4 * lax.axis_index("x") + 2 * lax.axis_index("y") + lax.axis_index("c")


def _slab(kind, ref, idx):
    if kind == "win":
        return ref.at[:, pl.ds(pl.multiple_of(idx * C_IN, LANE), C_IN)]
    if kind == "wup":
        return ref.at[:, pl.ds(pl.multiple_of(_up_slot(idx) * C_UP_P, LANE), C_UP_P)]
    if kind == "wupT":
        return ref.at[pl.ds(pl.multiple_of(_up_slot(idx) * C_UP_P, LANE), C_UP_P), :]
    if kind == "wkv":
        return ref.at[idx]
    if kind == "w4":
        return ref.at[:, pl.ds(pl.multiple_of(idx * R_O, 16), R_O), :]
    if kind == "wdn":
        return ref.at[pl.ds(pl.multiple_of(_dn_row(idx), 16), R_DN), :]
    assert kind == "cv"
    return ref.at[idx]


_WHOLE = {"win": ((D, 9 * D), BF), "wup": ((D, 2 * FF_P), BF), "wkv": ((N_DEV, D, C_KV), BF),
          "w4": ((4, D, D), BF), "wdn": ((FF_P, D), BF)}
_SHARD = {"win": (D, C_IN), "wup": (D, C_UP_P), "wupT": (C_UP_P, D), "wkv": (D, C_KV), "w4": (4, R_O, D),
          "wdn": (R_DN, D)}
HBM_SPEC = pl.BlockSpec(memory_space=pltpu.HBM)
SEM_SPEC = pl.BlockSpec(memory_space=pltpu.SEMAPHORE)
_DATAFLOW = pltpu.SideEffectType.DATAFLOW_SIDE_EFFECTING


def _scatter_maps(kinds):
    return ((lambda srcs, lands, a, idx: _slab(kinds[a], srcs[a], idx)),
            (lambda lands, a, idx: lands[a].at[idx]))


_SLOTTED = ("wkv", "cv")


def _own_slab_blocks(kind, shard_shape):
    if kind in ("win", "wup"):
        rows, slot = 256, (_up_slot if kind == "wup" else (lambda m: m))
        return (shard_shape[0] // rows, (rows, shard_shape[1]), (lambda i, me: (i, slot(me[0]))),
                (lambda i, me: (i, 0)), (lambda i, me: (me[0], i, 0)))
    if kind == "wupT":
        rows = 256
        steps = shard_shape[0] // rows
        return (steps, (rows, D), (lambda i, me: (_up_slot(me[0]) * steps + i, 0)), (lambda i, me: (i, 0)),
                (lambda i, me: (me[0], i, 0)))
    if kind == "w4":
        return (1, shard_shape, (lambda i, me: (0, me[0], 0)), (lambda i, me: (0, 0, 0)),
                (lambda i, me: (me[0], 0, 0, 0)))
    if kind == "wdn":
        rows = 32
        return (R_DN // rows, (rows, D), (lambda i, me: (_dn_row(me[0]) // rows + i, 0)), (lambda i, me: (i, 0)),
                (lambda i, me: (me[0], i, 0)))
    assert kind in _SLOTTED
    rows = min(256, shard_shape[0])
    return (shard_shape[0] // rows, (rows, shard_shape[1]), (lambda i, me: (me[0], i, 0)),
            (lambda i, me: (i, 0)), (lambda i, me: (me[0], i, 0)))


def _place_own(kind, src, out_sds, gather, me_arr, *, after=None, name):
    shard_shape = src.shape if gather else out_sds.shape[1:]
    steps, blk, whole_idx, shard_idx, staging_idx = _own_slab_blocks(kind, shard_shape)
    slotted = kind in _SLOTTED
    whole_spec = pl.BlockSpec(((None,) if slotted else ()) + tuple(blk), whole_idx)
    if gather:
        in_spec, out_spec = pl.BlockSpec(tuple(blk), shard_idx), whole_spec
    else:
        in_spec, out_spec = whole_spec, pl.BlockSpec((None,) + tuple(blk), staging_idx)
    zero_init = gather and kind == "wdn"

    def body(me_ref, src_ref, *rest):
        rest[-1][...] = src_ref[...].astype(rest[-1].dtype)

    operands = (me_arr, src) + ((jnp.zeros(out_sds.shape, out_sds.dtype),) if zero_init else ())
    operands += () if after is None else (after,)
    return _pcall(
        body,
        grid_spec=pltpu.PrefetchScalarGridSpec(
            num_scalar_prefetch=1, grid=(steps,), in_specs=[in_spec] + [ANY] * (len(operands) - 2),
            out_specs=out_spec),
        out_shape=out_sds, input_output_aliases={2: 0} if zero_init else {},
        compiler_params=_cp(("arbitrary",)), name=name)(*operands)


def _peer_copies(n, src_of, dst_of, src_r, land_r, ssem, rsem):
    me = _my_index()
    out = []
    for r, peer in _relations():
        p_idx = 4 * peer[0] + 2 * peer[1] + peer[2]
        for a in range(n):
            def copy(src_idx, dst_idx, a=a, r=r, peer=peer):
                sem = a * (N_DEV - 1) + r - 1
                return pltpu.make_async_remote_copy(
                    src_ref=src_of(src_r, land_r, a, src_idx), dst_ref=dst_of(land_r, a, dst_idx),
                    send_sem=ssem.at[sem], recv_sem=rsem.at[sem], device_id=peer, device_id_type=MESH)
            out.append((functools.partial(copy, p_idx, me), functools.partial(copy, me, p_idx)))
    return out


def _exchange_start(srcs, lands, maps, after, *, name):
    n, ns = len(lands), len(srcs)
    src_of, dst_of = maps

    def body(*refs):
        src_r, land_r = refs[:ns], refs[ns:ns + n]
        ssem, rsem, token = refs[ns + n + 1], refs[ns + n + 2], refs[-1]
        for send, _ in _peer_copies(n, src_of, dst_of, src_r, land_r, ssem, rsem):
            send().start()
        token[...] = jnp.zeros_like(token)

    flight = list(srcs) + list(lands)
    outs = pl.pallas_call(
        body, name=name,
        out_shape=(pltpu.SemaphoreType.DMA((n * (N_DEV - 1),)), pltpu.SemaphoreType.DMA((n * (N_DEV - 1),)),
                   *[pltpu.HBM(a.shape, a.dtype) for a in flight], SDS((8, LANE), F32)),
        in_specs=[HBM_SPEC] * (ns + n) + [ANY],
        out_specs=(SEM_SPEC, SEM_SPEC, *[HBM_SPEC] * (ns + n), pl.BlockSpec(memory_space=pltpu.VMEM)),
        input_output_aliases={i: 2 + i for i in range(ns + n)},
        compiler_params=pltpu.CompilerParams(has_side_effects=_DATAFLOW),
    )(*[pltpu.with_memory_space_constraint(a, pltpu.HBM) for a in flight], after)
    return (outs[0], outs[1], list(outs[2:2 + ns + n]), ns), outs[-1]


def _exchange_wait(handle, maps, after, *, name):
    ssem, rsem, flight, ns = handle
    n = len(flight) - ns
    src_of, dst_of = maps

    def body(*refs):
        src_r, land_r, ssem_r, rsem_r = refs[:ns], refs[ns:ns + n], refs[ns + n], refs[ns + n + 1]
        for send, arrival in _peer_copies(n, src_of, dst_of, src_r, land_r, ssem_r, rsem_r):
            send().wait_send()
            arrival().wait_recv()

    outs = pl.pallas_call(
        body, name=name, out_shape=[pltpu.HBM(a.shape, a.dtype) for a in flight],
        in_specs=[HBM_SPEC] * (ns + n) + [SEM_SPEC, SEM_SPEC, ANY], out_specs=[HBM_SPEC] * (ns + n),
        input_output_aliases={i: i for i in range(ns + n)},
        compiler_params=pltpu.CompilerParams(has_side_effects=_DATAFLOW),
    )(*flight, ssem, rsem, after)
    return list(outs[ns:])


_SIBLING = 1
_ICI = (2, 4, 6)


def _rel_peer(r):
    x, y, c = lax.axis_index("x"), lax.axis_index("y"), lax.axis_index("c")
    peer = (x ^ ((r >> 2) & 1), y ^ ((r >> 1) & 1), c ^ (r & 1))
    return peer, 4 * peer[0] + 2 * peer[1] + peer[2]


def _rcopy(ref, ssem, rsem, peer):
    return pltpu.make_async_remote_copy(src_ref=ref, dst_ref=ref, send_sem=ssem, recv_sem=rsem, device_id=peer,
                                        device_id_type=MESH)


def _gather2_start(lands, kinds, after, *, name):
    n = len(lands)

    def body(*refs):
        land_r, (send1, recv_sib, recv_ici), token = refs[:n], refs[n + 1:n + 4], refs[-1]
        me = _my_index()
        for a in range(n):
            own = _slab(kinds[a], land_r[a], me)
            for j, r in enumerate((_SIBLING,) + _ICI):
                rsem = recv_sib.at[a] if r == _SIBLING else recv_ici.at[3 * a + j - 1]
                _rcopy(own, send1.at[4 * a + j], rsem, _rel_peer(r)[0]).start()
        token[...] = jnp.zeros_like(token)

    sems = [pltpu.SemaphoreType.DMA((4 * n,)), pltpu.SemaphoreType.DMA((n,)), pltpu.SemaphoreType.DMA((3 * n,))]
    outs = pl.pallas_call(
        body, name=name, out_shape=(*sems, *[pltpu.HBM(a.shape, a.dtype) for a in lands], SDS((8, LANE), F32)),
        in_specs=[HBM_SPEC] * n + [ANY],
        out_specs=(SEM_SPEC,) * 3 + (HBM_SPEC,) * n + (pl.BlockSpec(memory_space=pltpu.VMEM),),
        input_output_aliases={i: 3 + i for i in range(n)},
        compiler_params=pltpu.CompilerParams(has_side_effects=_DATAFLOW),
    )(*[pltpu.with_memory_space_constraint(a, pltpu.HBM) for a in lands], after)
    return dict(send1=outs[0], recv_sib=outs[1], recv_ici=outs[2], lands=list(outs[3:3 + n])), outs[-1]


def _gather2_forward(handle, kinds, after, *, name):
    lands = handle["lands"]
    n = len(lands)

    def body(*refs):
        land_r, recv_ici, (fwd_send, fwd_recv), token = refs[:n], refs[n], refs[n + 2:n + 4], refs[-1]
        sibling = _rel_peer(_SIBLING)[0]
        for a in range(n):
            for j, r in enumerate(_ICI):
                got = _slab(kinds[a], land_r[a], _rel_peer(r)[1])
                _rcopy(got, fwd_send.at[3 * a + j], recv_ici.at[3 * a + j], sibling).wait_recv()
                _rcopy(got, fwd_send.at[3 * a + j], fwd_recv.at[3 * a + j], sibling).start()
        token[...] = jnp.zeros_like(token)

    sems = [pltpu.SemaphoreType.DMA((3 * n,)), pltpu.SemaphoreType.DMA((3 * n,))]
    outs = pl.pallas_call(
        body, name=name, out_shape=(*sems, *[pltpu.HBM(a.shape, a.dtype) for a in lands], SDS((8, LANE), F32)),
        in_specs=[HBM_SPEC] * n + [SEM_SPEC, ANY],
        out_specs=(SEM_SPEC,) * 2 + (HBM_SPEC,) * n + (pl.BlockSpec(memory_space=pltpu.VMEM),),
        input_output_aliases={i: 2 + i for i in range(n)},
        compiler_params=pltpu.CompilerParams(has_side_effects=_DATAFLOW),
    )(*lands, handle["recv_ici"], after)
    return dict(handle, fwd_send=outs[0], fwd_recv=outs[1], lands=list(outs[2:2 + n])), outs[-1]


def _gather2_wait(handle, kinds, after, *, name):
    lands = handle["lands"]
    n = len(lands)

    def body(*refs):
        land_r, (send1, recv_sib, fwd_send, fwd_recv) = refs[:n], refs[n:n + 4]
        me = _my_index()
        sibling, sib_idx = _rel_peer(_SIBLING)
        for a in range(n):
            own = _slab(kinds[a], land_r[a], me)
            for j, r in enumerate((_SIBLING,) + _ICI):
                _rcopy(own, send1.at[4 * a + j], recv_sib.at[a], _rel_peer(r)[0]).wait_send()
            theirs = _slab(kinds[a], land_r[a], sib_idx)
            _rcopy(theirs, send1.at[4 * a], recv_sib.at[a], sibling).wait_recv()
            for j, r in enumerate(_ICI):
                passed_on = _slab(kinds[a], land_r[a], _rel_peer(r)[1])
                _rcopy(passed_on, fwd_send.at[3 * a + j], fwd_recv.at[3 * a + j], sibling).wait_send()
                arrived = _slab(kinds[a], land_r[a], _rel_peer(r ^ _SIBLING)[1])
                _rcopy(arrived, fwd_send.at[3 * a + j], fwd_recv.at[3 * a + j], sibling).wait_recv()

    outs = pl.pallas_call(
        body, name=name, out_shape=[pltpu.HBM(a.shape, a.dtype) for a in lands],
        in_specs=[HBM_SPEC] * n + [SEM_SPEC] * 4 + [ANY], out_specs=[HBM_SPEC] * n,
        input_output_aliases={i: i for i in range(n)},
        compiler_params=pltpu.CompilerParams(has_side_effects=_DATAFLOW),
    )(*lands, handle["send1"], handle["recv_sib"], handle["fwd_send"], handle["fwd_recv"], after)
    return list(outs)


def _sum_slots(gathered, *, name):
    def body(g_ref, out_ref):
        total = g_ref[0]
        for dev in range(1, N_DEV):
            total = total + g_ref[dev]
        out_ref[...] = total

    return _pcall(body, out_shape=SDS(gathered.shape[1:], F32), compiler_params=_cp(), name=name)(gathered)


def _adam(g, w, m, v):
    nm = ADAM_B1 * m + (1.0 - ADAM_B1) * g
    nv = ADAM_B2 * v + (1.0 - ADAM_B2) * (g * g)
    m_hat = nm / (1.0 - ADAM_B1 ** ADAM_STEP)
    v_hat = nv / (1.0 - ADAM_B2 ** ADAM_STEP)
    return -ADAM_LR * (m_hat / (jnp.sqrt(v_hat) + ADAM_EPS) + ADAM_WD * w), nm, nv


def _adamw_staged(st0, st1, w, m, v, *, name):
    _, rows, cols = w.shape
    st_cols = st0.shape[2]
    tr = max(t for t in range(16, 129, 16) if rows % t == 0)
    nr = rows // tr

    def body(s0_ref, s1_ref, w_ref, m_ref, v_ref, g_ref, d_ref, nm_ref, nv_ref):
        for layer, s_ref in enumerate((s0_ref, s1_ref)):
            @pl.when(pl.program_id(0) == layer)
            def _(s_ref=s_ref):
                total = s_ref[0, :, 0:cols].astype(F32)
                for dev in range(1, N_DEV):
                    total = total + s_ref[dev, :, 0:cols].astype(F32)
                g_ref[0] = total

        d_ref[0], nm_ref[0], nv_ref[0] = _adam(g_ref[0], w_ref[0], m_ref[0], v_ref[0])

    st_spec = lambda layer: pl.BlockSpec(
        (N_DEV, tr, st_cols), lambda l, i: (0, jnp.where(l == layer, i, (nr - 1) * (1 - layer)), 0))
    par = pl.BlockSpec((1, tr, cols), lambda l, i: (l, i, 0))
    return _pcall(
        body, grid=(DEPTH, nr), in_specs=[st_spec(0), st_spec(1), par, par, par], out_specs=[par] * 4,
        out_shape=[SDS(w.shape, F32)] * 4,
        compiler_params=_cp(("arbitrary", "arbitrary")), name=name)(st0, st1, w, m, v)


def _adamw_small(g, w, m, v, *, name):
    def body(g_ref, w_ref, m_ref, v_ref, d_ref, nm_ref, nv_ref):
        d_ref[...], nm_ref[...], nv_ref[...] = _adam(g_ref[...], w_ref[...], m_ref[...], v_ref[...])

    return _pcall(body, out_shape=[SDS(g.shape, F32)] * 3, compiler_params=_cp(), name=name)(g, w, m, v)


def _pack_rows(arrays):
    flat = jnp.concatenate([a.reshape(-1).astype(F32) for a in arrays])
    rows = -(-flat.shape[0] // (8 * D)) * 8
    return jnp.pad(flat, (0, rows * D - flat.shape[0])).reshape(rows, D)


def _unpack_rows(pack, like):
    flat = pack.reshape(-1)
    out, at = [], 0
    for a in like:
        out.append(flat[at:at + a.size].reshape(a.shape))
        at += a.size
    return out


def _layer_fwd(x, h, mem, win, mixer_weights, ffn_weights, after_up, small, g_next, tag):
    proj = _mm(h, win, tm=1024, tn=1536, name=f"proj_{tag}")
    wkv, w4, cw_a, cw_b, cw_f = mixer_weights(proj)
    za = _bra_fwd(proj, cw_a, name=f"bra_fwd_{tag}")
    cb, sb = _brb_conv_fwd(proj, cw_b, small["conv_b_bias"], small["ln_b_g"], small["ln_b_b"],
                           name=f"brb_conv_fwd_{tag}")
    memn, kv = _kv_prep(mem, small["norm_mem_g"], wkv, name=f"kv_prep_{tag}")
    o = _attn_fwd(proj, kv, name=f"attn_fwd_{tag}")
    ya, yb, yc, mg, x1, h2 = _mix_out(x, za, sb, o, proj, w4, small["b_gate"], small["norm_ffn_g"],
                                      name=f"mix_out_{tag}")
    wup, wdn = ffn_weights(h2)
    u2 = _mm(h2, wup, tm=1024, tn=1536, name=f"up_{tag}")
    token = after_up(u2)
    act, c2 = _ffn_act(u2, cw_f if token is None else _behind(cw_f, token), name=f"ffn_act_{tag}")
    x2, h_next = _mm_res_norm(act, wdn, x1, g_next, name=f"down_{tag}")
    saved = dict(x=x, h=h, proj=proj, za=za, cb=cb, sb=sb, memn=memn, kv=kv, o=o, ya=ya, yb=yb, yc=yc,
                 mg=mg, x1=x1, h2=h2, u2=u2, c2=c2, act=act)
    return x2, h_next, (win, wup, wkv, w4, wdn, cw_a, cw_b, cw_f), saved


def _behind(operand, token):
    return operand + token[0:1, 0:1]


def _layer_bwd(dx2, dx2b, mem, wts, small, sv, start, tag):
    win, wup, wkv, w4, wdn, cw_a, cw_b, cw_f = wts
    dact = _mm(dx2b, wdn, tb=True, tm=1024, tn=768, name=f"d_act_{tag}")
    dwdn = _mm(sv["act"], dx2b, ta=True, tm=768, tn=1024, name=f"dw_down_{tag}")
    du2, dcw_f = _ffn_bwd(sv["u2"], sv["c2"], dact, cw_f, name=f"ffn_bwd_{tag}")
    dwup_t = _mm(du2, sv["h2"], ta=True, tm=C_UP_P, tn=1024, name=f"dw_up_{tag}")
    token = start(("wdn", "wupT"), (dwdn, dwup_t), f"ffn_{tag}")
    dx1, dx1b, dg_ffn = _mm_nt_normbwd(du2, wup, sv["x1"], dx2, _behind(small["norm_ffn_g"], token),
                                       name=f"d_h2_{tag}")

    dya, dyb, dyc, dza, dsb, do, dproj, dbg = _mix_bwd(dx1b, sv["ya"], sv["yb"], sv["yc"], sv["proj"], w4,
                                                      small["b_gate"], name=f"mix_bwd_{tag}")
    dw4 = jnp.stack([
        _mm(a, b, ta=True, tm=1024, tn=512, name=f"dw_{nm}_{tag}")
        for nm, a, b in (("a_out", sv["za"], dya), ("b_out", sv["sb"], dyb), ("att_out", sv["o"], dyc),
                         ("o", sv["mg"], dx1b))])
    dq, dk, dv = _attn_bwd(sv["proj"], sv["kv"], do, name=f"attn_bwd_{tag}")
    dwkv, dg_mem = _kv_bwd(mem, small["norm_mem_g"], sv["memn"], dk, dv, wkv, name=f"kv_bwd_{tag}")
    token = start(("w4", "wkv"), (dw4, dwkv), f"mix_{tag}")
    dproj, dcw_a = _bra_bwd(sv["proj"], dza, _behind(cw_a, token), dproj, name=f"bra_bwd_{tag}")
    dcb, ln_sums = _ln_silu_bwd(sv["cb"], dsb, small["ln_b_g"], small["ln_b_b"], name=f"ln_silu_bwd_{tag}")
    dproj, dcw_b = _brb_conv_bwd(sv["proj"], dcb, dq, cw_b, dproj, name=f"brb_conv_bwd_{tag}")
    dwin = _mm(sv["h"], dproj, ta=True, tm=1024, tn=768, name=f"dw_in_{tag}")
    token = start(("win",), (dwin,), f"in_{tag}")
    dx, dxb, dg_mix = _mm_nt_normbwd(dproj, win, sv["x"], dx1, _behind(small["norm_mix_g"], token),
                                     tk=4608, name=f"d_h_{tag}")

    small_grads = [dg_mix[0:1], dg_mem[0:1], dbg[0:1].reshape(3, D), ln_sums[2:3], ln_sums[0:1], ln_sums[1:2],
                   dg_ffn[0:1], dcw_a[0:K_A], dcw_b[0:K_B], dcw_f[0:K_F].reshape(K_F * 2 * FF_P // D, D)]
    return dx, dxb, small_grads, token


_SMALL_ROWS = (1, 1, 3, 1, 1, 1, 1, K_A, K_B, K_F * 2 * FF_P // D)
_CV_ROWS = 48


def kernel(x, mem, norm_mix_g, norm_mem_g, w_in, b_gate, conv_a_w, w_a_out, conv_b_w, conv_b_bias, ln_b_g, ln_b_b, w_b_out, w_kv, w_att_out, w_o, norm_ffn_g, w_up, conv_ffn_w, w_down, norm_final_g, loss_target, m_norm_mix_g, m_norm_mem_g, m_w_in, m_b_gate, m_conv_a_w, m_w_a_out, m_conv_b_w, m_conv_b_bias, m_ln_b_g, m_ln_b_b, m_w_b_out, m_w_kv, m_w_att_out, m_w_o, m_norm_ffn_g, m_w_up, m_conv_ffn_w, m_w_down, m_norm_final_g, v_norm_mix_g, v_norm_mem_g, v_w_in, v_b_gate, v_conv_a_w, v_w_a_out, v_conv_b_w, v_conv_b_bias, v_ln_b_g, v_ln_b_b, v_w_b_out, v_w_kv, v_w_att_out, v_w_o, v_norm_ffn_g, v_w_up, v_conv_ffn_w, v_w_down, v_norm_final_g):
    me = _my_index()
    me_arr = me.astype(jnp.int32).reshape(1)
    x0, mem0, tgt = x.reshape(x.shape[1:]), mem.reshape(mem.shape[1:]), loss_target.reshape(x.shape[1:])
    up_pad = ((0, 0), (0, 0), (0, C_UP_P - C_UP))

    ag_groups = (("win",), ("wkv", "w4", "cv"), ("wup", "wdn"))
    kinds = ag_groups[0] + ag_groups[1] + ag_groups[2]
    smalls, ag_handles = [], []
    token = jnp.zeros((8, LANE), F32)
    for l in range(DEPTH):
        cv = jnp.zeros((_CV_ROWS, C_UP_P), F32)
        cv = cv.at[0:K_F, 0:C_UP].set(conv_ffn_w[l]).at[3:3 + K_A, 0:R_O].set(conv_a_w[l])
        cv = cv.at[8:8 + K_B, 0:R_O].set(conv_b_w[l])
        shards = dict(
            win=w_in[l], wup=jnp.pad(w_up[l], up_pad[1:]), wkv=w_kv[l],
            w4=jnp.stack([w_a_out[l], w_b_out[l], w_att_out[l], w_o[l]]), wdn=w_down[l], cv=cv)
        whole = dict({k: SDS(*_WHOLE[k]) for k in kinds if k != "cv"}, cv=SDS((N_DEV,) + cv.shape, F32))
        per_layer = []
        for g, grp in enumerate(ag_groups):
            lands = [_place_own(k, shards[k], whole[k], True, me_arr, after=token, name=f"ag_own_{k}_l{l}")
                     for k in grp]
            handle, token = _gather2_start(lands, grp, token, name=f"ag_start_l{l}_g{g}")
            per_layer.append(handle)
        ag_handles.append(per_layer)
        if l == 0:
            started_l0 = token
        smalls.append(dict(
            norm_mix_g=norm_mix_g[l][None], norm_mem_g=norm_mem_g[l][None], b_gate=b_gate[l][None],
            conv_b_bias=conv_b_bias[l][None], ln_b_g=ln_b_g[l][None], ln_b_b=ln_b_b[l][None],
            norm_ffn_g=norm_ffn_g[l][None]))

    def forward_group(l, g, after):
        ag_handles[l][g], tok = _gather2_forward(ag_handles[l][g], ag_groups[g], after, name=f"ag_forward_l{l}_g{g}")
        return tok

    def group_of(l, g):
        def wait(after):
            if l == 0:
                after = forward_group(0, g, after)
            return _gather2_wait(ag_handles[l][g], ag_groups[g], after, name=f"ag_wait_l{l}_g{g}")
        return wait

    def mixer_weights(l):
        def wait(after):
            if l > 0:
                after = forward_group(l, 2, after)
            wkv, w4, cvg = group_of(l, 1)(after)
            cw_f = jnp.stack([cvg[d, 0:K_F, :] for d in UP_ORDER], axis=1).reshape(K_F, 2 * FF_P)
            cw_a = cvg[:, 3:3 + K_A, 0:R_O].transpose(1, 0, 2).reshape(K_A, D)
            cw_b = cvg[:, 8:8 + K_B, 0:R_O].transpose(1, 0, 2).reshape(K_B, D)
            return (wkv, w4, jnp.pad(cw_a, ((0, 8 - K_A), (0, 0))), jnp.pad(cw_b, ((0, 32 - K_B), (0, 0))),
                    jnp.pad(cw_f, ((0, 8 - K_F), (0, 0))))
        return wait

    wts, saved = [], []
    xs = x0
    h = _rms_fwd(xs, _behind(smalls[0]["norm_mix_g"], started_l0), name="rms_fwd")
    behind = forward_group(0, 0, token)

    def next_layer_forwarding(l):
        def hook(after):
            return None if l + 1 == DEPTH else forward_group(l + 1, 0, after)
        return hook

    for l in range(DEPTH):
        g_next = smalls[l + 1]["norm_mix_g"] if l + 1 < DEPTH else norm_final_g[None]
        if l > 0:
            behind = forward_group(l, 1, behind)
        (win,) = _gather2_wait(ag_handles[l][0], ag_groups[0], behind, name=f"ag_wait_l{l}_g0")
        xs, h, w_l, sv = _layer_fwd(xs, h, mem0, win, mixer_weights(l), group_of(l, 2), next_layer_forwarding(l),
                                    smalls[l], g_next, f"l{l}")
        behind = h
        wts.append(w_l)
        saved.append(sv)
    dx, dxb, head_sums = _loss_head(xs, tgt, norm_final_g[None], name="loss_head")

    rs_handles = []
    small_grads = [None] * DEPTH

    def start_scatter(grp, arrays, name):
        maps = _scatter_maps(grp)
        lands = [_place_own(k, a, SDS((N_DEV,) + _SHARD[k], BF), False, me_arr, name=f"rs_own_{k}_{name}")
                 for k, a in zip(grp, arrays)]
        handle, tok = _exchange_start(list(arrays), lands, maps, rs_handles[-1][2] if rs_handles else head_sums,
                                      name=f"rs_start_{name}")
        rs_handles.append((grp, handle, tok, name))
        return tok

    for l in reversed(range(DEPTH)):
        dx, dxb, small_grads[l], token = _layer_bwd(dx, dxb, mem0, wts[l], smalls[l], saved[l], start_scatter,
                                                    f"l{l}")

    pack = jnp.concatenate(small_grads[0] + small_grads[1] + [head_sums[1:2], head_sums[0:1]], axis=0)
    pack = jnp.pad(pack, ((0, -pack.shape[0] % 8), (0, 0)))
    small_maps = (lambda srcs, lands, a, idx: srcs[a]), (lambda lands, a, idx: lands[a].at[idx])
    small_land = _place_own("cv", pack, SDS((N_DEV,) + pack.shape, F32), True, me_arr, name="small_own")
    small_handle, small_token = _exchange_start([pack], [small_land], small_maps, dx, name="small_start")

    staged = [dict() for _ in range(DEPTH)]
    for grp, handle, _, name in rs_handles[:-1]:
        staged[int(name[-1])].update(zip(grp, _exchange_wait(handle, _scatter_maps(grp), small_token,
                                                             name=f"rs_wait_{name}")))

    def big_update(kind, w, m, v, name):
        return _adamw_staged(staged[0][kind], staged[1][kind], w, m, v, name=name)

    r_up = [jnp.swapaxes(a, 1, 2) for a in big_update(
        "wupT", jnp.swapaxes(w_up, 1, 2), jnp.swapaxes(m_w_up, 1, 2), jnp.swapaxes(v_w_up, 1, 2), "adamw_w_up")]
    r_kv = big_update("wkv", w_kv, m_w_kv, v_w_kv, "adamw_w_kv")
    r_dn = big_update("wdn", w_down, m_w_down, v_w_down, "adamw_w_down")

    def four(a, b, c, d_):
        return jnp.stack([a, b, c, d_], axis=1).reshape(DEPTH, 4 * R_O, D)

    r_4 = _adamw_staged(
        staged[0]["w4"].reshape(N_DEV, 4 * R_O, D), staged[1]["w4"].reshape(N_DEV, 4 * R_O, D),
        four(w_a_out, w_b_out, w_att_out, w_o), four(m_w_a_out, m_w_b_out, m_w_att_out, m_w_o),
        four(v_w_a_out, v_w_b_out, v_w_att_out, v_w_o), name="adamw_w_out")
    grp, handle, _, name = rs_handles[-1]
    staged[0].update(zip(grp, _exchange_wait(handle, _scatter_maps(grp), r_4[0], name=f"rs_wait_{name}")))
    r_in = big_update("win", w_in, m_w_in, v_w_in, "adamw_w_in")
    r_a, r_b, r_att, r_o = ([a.reshape(DEPTH, 4, R_O, D)[:, j] for a in r_4] for j in range(4))

    (gathered,) = _exchange_wait(small_handle, small_maps, r_in[0], name="small_wait")
    total = _sum_slots(gathered, name="small_sum")
    per_layer = sum(_SMALL_ROWS)
    parts = []
    for l in range(DEPTH):
        at, one = l * per_layer, []
        for rows in _SMALL_ROWS:
            one.append(total[at:at + rows])
            at += rows
        parts.append(one)
    g_final = total[DEPTH * per_layer]
    loss = 0.5 / D * jnp.sum(total[DEPTH * per_layer + 1])

    def both(i):
        return jnp.stack([parts[0][i], parts[1][i]])

    g_norm_mix, g_norm_mem = both(0)[:, 0], both(1)[:, 0]
    g_b_gate = both(2).reshape(DEPTH, 3 * D)
    g_cbias, g_lng, g_lnb, g_norm_ffn = both(3)[:, 0], both(4)[:, 0], both(5)[:, 0], both(6)[:, 0]
    g_conv_a = lax.dynamic_slice_in_dim(both(7), me * R_O, R_O, axis=2)
    g_conv_b = lax.dynamic_slice_in_dim(both(8), me * R_O, R_O, axis=2)
    g_conv_f = lax.dynamic_slice_in_dim(both(9).reshape(DEPTH, K_F, 2 * FF_P), _up_slot(me) * C_UP_P, C_UP, axis=2)

    small_g = [g_norm_mix, g_norm_mem, g_b_gate, g_conv_a, g_conv_b, g_cbias, g_lng, g_lnb, g_norm_ffn, g_conv_f,
               g_final]
    small_w = [norm_mix_g, norm_mem_g, b_gate, conv_a_w, conv_b_w, conv_b_bias, ln_b_g, ln_b_b, norm_ffn_g,
               conv_ffn_w, norm_final_g]
    small_m = [m_norm_mix_g, m_norm_mem_g, m_b_gate, m_conv_a_w, m_conv_b_w, m_conv_b_bias, m_ln_b_g, m_ln_b_b,
               m_norm_ffn_g, m_conv_ffn_w, m_norm_final_g]
    small_v = [v_norm_mix_g, v_norm_mem_g, v_b_gate, v_conv_a_w, v_conv_b_w, v_conv_b_bias, v_ln_b_g, v_ln_b_b,
               v_norm_ffn_g, v_conv_ffn_w, v_norm_final_g]
    upd = _adamw_small(_pack_rows(small_g), _pack_rows(small_w), _pack_rows(small_m), _pack_rows(small_v),
                       name="adamw_small")
    s_d, s_m, s_v = (_unpack_rows(p, small_w) for p in upd)
    (d_norm_mix, d_norm_mem, d_b_gate, d_conv_a, d_conv_b, d_cbias, d_lng, d_lnb, d_norm_ffn, d_conv_f,
     d_final) = s_d
    (nm_norm_mix, nm_norm_mem, nm_b_gate, nm_conv_a, nm_conv_b, nm_cbias, nm_lng, nm_lnb, nm_norm_ffn, nm_conv_f,
     nm_final) = s_m
    (nv_norm_mix, nv_norm_mem, nv_b_gate, nv_conv_a, nv_conv_b, nv_cbias, nv_lng, nv_lnb, nv_norm_ffn, nv_conv_f,
     nv_final) = s_v

    grads = [g_norm_mix, g_norm_mem, r_in[0], g_b_gate, g_conv_a, r_a[0], g_conv_b, g_cbias, g_lng, g_lnb, r_b[0],
             r_kv[0], r_att[0], r_o[0], g_norm_ffn, r_up[0], g_conv_f, r_dn[0], g_final]
    deltas = [d_norm_mix, d_norm_mem, r_in[1], d_b_gate, d_conv_a, r_a[1], d_conv_b, d_cbias, d_lng, d_lnb, r_b[1],
              r_kv[1], r_att[1], r_o[1], d_norm_ffn, r_up[1], d_conv_f, r_dn[1], d_final]
    new_m = [nm_norm_mix, nm_norm_mem, r_in[2], nm_b_gate, nm_conv_a, r_a[2], nm_conv_b, nm_cbias, nm_lng, nm_lnb,
             r_b[2], r_kv[2], r_att[2], r_o[2], nm_norm_ffn, r_up[2], nm_conv_f, r_dn[2], nm_final]
    new_v = [nv_norm_mix, nv_norm_mem, r_in[3], nv_b_gate, nv_conv_a, r_a[3], nv_conv_b, nv_cbias, nv_lng, nv_lnb,
             r_b[3], r_kv[3], r_att[3], r_o[3], nv_norm_ffn, r_up[3], nv_conv_f, r_dn[3], nv_final]
    return (loss, dx[None], *grads, *deltas, *new_m, *new_v)
```
